```python
import math
import jax, jax.numpy as jnp
from jax import lax
import numpy as np

D_MODEL = 1024
BATCH = 8
SEQ = 4096
DEPTH = 2

D_MIX = D_MODEL
N_MIXERS = 4
D_GROUP = D_MIX // N_MIXERS

MLA_HEADS = 4
MLA_Q_RANK = D_MODEL // 4
MLA_KV_RANK = D_MODEL // 8
MLA_NOPE = 64
MLA_ROPE = 32
MLA_V = D_GROUP // MLA_HEADS
ROPE_THETA = 10000.0

CONV_WIDTH = 3
CONV_CH = D_GROUP

POOL_WINDOWS = (2, 4, 8, 16)
POOL_GROUPS = len(POOL_WINDOWS)
POOL_CH = D_GROUP // POOL_GROUPS

SWA_HEADS = 4
SWA_KV_HEADS = 2
SWA_HEAD_DIM = D_GROUP // SWA_HEADS
SWA_WINDOW = 128

BLOCK = 128

D_FF = -(-8 * D_MODEL // (3 * 256)) * 256

RMS_EPS = 1e-6

IN_SPLITS = (MLA_Q_RANK, MLA_KV_RANK, MLA_ROPE,
             CONV_CH, CONV_CH, CONV_CH,
             D_GROUP,
             SWA_HEADS * SWA_HEAD_DIM,
             SWA_KV_HEADS * SWA_HEAD_DIM,
             SWA_KV_HEADS * SWA_HEAD_DIM)
D_IN = sum(IN_SPLITS)

kernel_name = "hybrid_parallel_mla_conv_pool_swa"


def _split_points():
    pts, acc = [], 0
    for w in IN_SPLITS[:-1]:
        acc += w
        pts.append(acc)
    return pts


def _alibi_slopes(n):
    return np.asarray([2.0 ** (-8.0 * (i + 1) / n) for i in range(n)], dtype=np.float32)


def rmsnorm(x, g):
    xf = x.astype(jnp.float32)
    y = xf * lax.rsqrt(jnp.mean(xf * xf, axis=-1, keepdims=True) + RMS_EPS)
    return (y * g.astype(jnp.float32)).astype(x.dtype)


def rope_tables(seq, dim, dtype):
    inv = 1.0 / (ROPE_THETA ** (jnp.arange(0, dim, 2, dtype=jnp.float32) / dim))
    ang = jnp.arange(seq, dtype=jnp.float32)[:, None] * inv[None, :]
    return jnp.cos(ang).astype(dtype), jnp.sin(ang).astype(dtype)


def apply_rope(x, cos, sin):
    x1, x2 = jnp.split(x, 2, axis=-1)
    c = cos[:, None, :]
    s = sin[:, None, :]
    return jnp.concatenate([x1 * c - x2 * s, x1 * s + x2 * c], axis=-1)


def mla_attention(c_q, c_kv, k_r, q_norm_g, kv_norm_g, w_uq, w_ukv, cos, sin):
    b, s, _ = c_q.shape
    dqk = MLA_NOPE + MLA_ROPE
    q = (rmsnorm(c_q, q_norm_g) @ w_uq).reshape(b, s, MLA_HEADS, dqk)
    q_nope, q_rot = q[..., :MLA_NOPE], q[..., MLA_NOPE:]
    q_rot = apply_rope(q_rot, cos, sin)
    kv = (rmsnorm(c_kv, kv_norm_g) @ w_ukv).reshape(b, s, MLA_HEADS, MLA_NOPE + MLA_V)
    k_nope, v = kv[..., :MLA_NOPE], kv[..., MLA_NOPE:]
    k_rot = apply_rope(k_r[:, :, None, :], cos, sin)
    k = jnp.concatenate([k_nope, jnp.broadcast_to(k_rot, (b, s, MLA_HEADS, MLA_ROPE))], axis=-1)
    q = jnp.concatenate([q_nope, q_rot], axis=-1) * (1.0 / math.sqrt(dqk))
    nb = s // BLOCK
    qb = q.reshape(b, nb, BLOCK, MLA_HEADS, dqk).transpose(1, 0, 2, 3, 4)
    key_pos = jnp.arange(s)

    def one_block(args):
        q_blk, i = args
        sc = jnp.einsum('bqhd,bkhd->bhqk', q_blk, k).astype(jnp.float32)
        q_pos = i * BLOCK + jnp.arange(BLOCK)
        causal = key_pos[None, :] <= q_pos[:, None]
        sc = jnp.where(causal[None, None], sc, -jnp.inf)
        p = jax.nn.softmax(sc, axis=-1).astype(v.dtype)
        return jnp.einsum('bhqk,bkhd->bqhd', p, v)

    out = lax.map(one_block, (qb, jnp.arange(nb)))
    return out.transpose(1, 0, 2, 3, 4).reshape(b, s, MLA_HEADS * MLA_V)


def short_gated_conv(gate_b, gate_c, u, conv_w):
    z = gate_c * u
    y = lax.conv_general_dilated(
        z, conv_w[:, None, :].astype(z.dtype), window_strides=(1,),
        padding=[(CONV_WIDTH - 1, 0)], dimension_numbers=('NWC', 'WIO', 'NWC'),
        feature_group_count=CONV_CH)
    return gate_b * y


def multiscale_pool(u, pool_w, pool_scale):
    b, s, _ = u.shape
    uf = u.astype(jnp.float32)
    cs = jnp.cumsum(uf, axis=1)
    pos = jnp.arange(s)
    outs = []
    for g, w in enumerate(POOL_WINDOWS):
        cs_g = cs[:, :, g * POOL_CH:(g + 1) * POOL_CH]
        lag = jnp.pad(cs_g, ((0, 0), (w, 0), (0, 0)))[:, :s]
        count = jnp.minimum(pos + 1, w).astype(jnp.float32)[None, :, None]
        outs.append((cs_g - lag) / count)
    pooled = jnp.stack(outs, axis=2) - uf.reshape(b, s, POOL_GROUPS, POOL_CH)
    mixed = jnp.einsum('bsgc,gcd->bsgd', pooled.astype(u.dtype), pool_w)
    return mixed.reshape(b, s, D_GROUP) * pool_scale


def swa_sink_attention(q, k, v, sinks, slopes):
    b, s, _, hd = q.shape
    grp = SWA_HEADS // SWA_KV_HEADS
    nb = s // BLOCK
    qb = q.reshape(b, nb, BLOCK, SWA_KV_HEADS, grp, hd)
    kb = k.reshape(b, nb, BLOCK, SWA_KV_HEADS, hd)
    vb = v.reshape(b, nb, BLOCK, SWA_KV_HEADS, hd)

    def with_prev(t):
        prev = jnp.pad(t, ((0, 0), (1, 0), (0, 0), (0, 0), (0, 0)))[:, :nb]
        return jnp.concatenate([prev, t], axis=2)

    kk, vv = with_prev(kb), with_prev(vb)
    sc = jnp.einsum('bnqkgd,bnskd->bnkgqs', qb, kk).astype(jnp.float32) * (1.0 / math.sqrt(hd))
    blk = jnp.arange(nb)[:, None] * BLOCK
    q_pos = blk + jnp.arange(BLOCK)[None, :]
    k_pos = blk - BLOCK + jnp.arange(2 * BLOCK)[None, :]
    dist = q_pos[:, :, None] - k_pos[:, None, :]
    valid = (dist >= 0) & (dist < SWA_WINDOW) & (k_pos[:, None, :] >= 0)
    sl = jnp.asarray(slopes).reshape(SWA_KV_HEADS, grp)
    bias = -sl[None, None, :, :, None, None] * dist.astype(jnp.float32)[None, :, None, None, :, :]
    sc = jnp.where(valid[None, :, None, None], sc + bias, -jnp.inf)
    sink = jnp.broadcast_to(
        sinks.astype(jnp.float32).reshape(SWA_KV_HEADS, grp)[None, None, :, :, None, None],
        sc.shape[:-1] + (1,))
    p = jax.nn.softmax(jnp.concatenate([sc, sink], axis=-1), axis=-1)[..., :-1].astype(v.dtype)
    out = jnp.einsum('bnkgqs,bnskd->bnqkgd', p, vv)
    return out.reshape(b, s, SWA_HEADS * hd)


def _fwd_setup_inputs(seed: int = 0) -> dict:
    key = jax.random.key(seed)
    ks = jax.random.split(key, 17)
    f32 = jnp.float32

    def dense(k, shape, fan_in):
        return jax.random.normal(k, shape, f32) * fan_in ** -0.5

    def gain(k, shape):
        return 1.0 + 0.05 * jax.random.normal(k, shape, f32)

    return {
        "x": jax.random.normal(ks[0], (BATCH, SEQ, D_MODEL), f32),
        "attn_norm": gain(ks[1], (DEPTH, D_MODEL)),
        "w_in": dense(ks[2], (DEPTH, D_MODEL, D_IN), D_MODEL),
        "mla_q_norm": gain(ks[3], (DEPTH, MLA_Q_RANK)),
        "w_uq": dense(ks[4], (DEPTH, MLA_Q_RANK, MLA_HEADS * (MLA_NOPE + MLA_ROPE)), MLA_Q_RANK),
        "mla_kv_norm": gain(ks[5], (DEPTH, MLA_KV_RANK)),
        "w_ukv": dense(ks[6], (DEPTH, MLA_KV_RANK, MLA_HEADS * (MLA_NOPE + MLA_V)), MLA_KV_RANK),
        "conv_w": dense(ks[7], (DEPTH, CONV_WIDTH, CONV_CH), CONV_WIDTH),
        "pool_w": dense(ks[8], (DEPTH, POOL_GROUPS, POOL_CH, POOL_CH), POOL_CH),
        "pool_scale": gain(ks[9], (DEPTH, D_GROUP)),
        "swa_sinks": 0.5 * jax.random.normal(ks[10], (DEPTH, SWA_HEADS), f32),
        "mix_norm": gain(ks[11], (DEPTH, D_MIX)),
        "w_o": dense(ks[12], (DEPTH, D_MIX, D_MODEL), D_MIX),
        "ffn_norm": gain(ks[13], (DEPTH, D_MODEL)),
        "w_gate_up": dense(ks[14], (DEPTH, D_MODEL, 2 * D_FF), D_MODEL),
        "w_down": dense(ks[15], (DEPTH, D_FF, D_MODEL), D_FF),
        "final_norm": gain(ks[16], (D_MODEL,)),
    }


def _fwd_reference(x, attn_norm, w_in, mla_q_norm, w_uq, mla_kv_norm, w_ukv, conv_w, pool_w,
              pool_scale, swa_sinks, mix_norm, w_o, ffn_norm, w_gate_up, w_down, final_norm):
    b, s, _ = x.shape
    cos, sin = rope_tables(s, MLA_ROPE, x.dtype)
    slopes = _alibi_slopes(SWA_HEADS)
    pts = _split_points()
    for l in range(DEPTH):
        h = rmsnorm(x, attn_norm[l])
        proj = h @ w_in[l]
        (c_q, c_kv, k_r, g_b, g_c, u_conv, u_pool,
         q_sw, k_sw, v_sw) = jnp.split(proj, pts, axis=-1)
        y_a = mla_attention(c_q, c_kv, k_r, mla_q_norm[l], mla_kv_norm[l],
                            w_uq[l], w_ukv[l], cos, sin)
        y_b = short_gated_conv(g_b, g_c, u_conv, conv_w[l])
        y_c = multiscale_pool(u_pool, pool_w[l], pool_scale[l])
        y_d = swa_sink_attention(q_sw.reshape(b, s, SWA_HEADS, SWA_HEAD_DIM),
                                 k_sw.reshape(b, s, SWA_KV_HEADS, SWA_HEAD_DIM),
                                 v_sw.reshape(b, s, SWA_KV_HEADS, SWA_HEAD_DIM),
                                 swa_sinks[l], slopes)
        groups = jnp.stack([y_a, y_b, y_c, y_d], axis=2)
        gf = groups.astype(jnp.float32)
        gf = gf * lax.rsqrt(jnp.mean(gf * gf, axis=-1, keepdims=True) + RMS_EPS)
        mixed = (gf.reshape(b, s, D_MIX) * mix_norm[l].astype(jnp.float32)).astype(x.dtype)
        x = x + mixed @ w_o[l]
        h2 = rmsnorm(x, ffn_norm[l])
        gate, up = jnp.split(h2 @ w_gate_up[l], 2, axis=-1)
        x = x + (jax.nn.silu(gate) * up) @ w_down[l]
    return rmsnorm(x, final_norm)


import jax as _jax
import jax.numpy as _jnp

TWIN_FORMAT = 'train_step'
FWD_PARAMS = ['x', 'attn_norm', 'w_in', 'mla_q_norm', 'w_uq', 'mla_kv_norm', 'w_ukv', 'conv_w', 'pool_w', 'pool_scale', 'swa_sinks', 'mix_norm', 'w_o', 'ffn_norm', 'w_gate_up', 'w_down', 'final_norm']
TWIN_WEIGHTS = ['attn_norm', 'w_in', 'mla_q_norm', 'w_uq', 'mla_kv_norm', 'w_ukv', 'conv_w', 'pool_w', 'pool_scale', 'swa_sinks', 'mix_norm', 'w_o', 'ffn_norm', 'w_gate_up', 'w_down', 'final_norm']
TWIN_DIFF_INPUT = 'x'
TWIN_INPUTS = ['x', 'attn_norm', 'w_in', 'mla_q_norm', 'w_uq', 'mla_kv_norm', 'w_ukv', 'conv_w', 'pool_w', 'pool_scale', 'swa_sinks', 'mix_norm', 'w_o', 'ffn_norm', 'w_gate_up', 'w_down', 'final_norm', 'loss_target', 'm_attn_norm', 'm_w_in', 'm_mla_q_norm', 'm_w_uq', 'm_mla_kv_norm', 'm_w_ukv', 'm_conv_w', 'm_pool_w', 'm_pool_scale', 'm_swa_sinks', 'm_mix_norm', 'm_w_o', 'm_ffn_norm', 'm_w_gate_up', 'm_w_down', 'm_final_norm', 'v_attn_norm', 'v_w_in', 'v_mla_q_norm', 'v_w_uq', 'v_mla_kv_norm', 'v_w_ukv', 'v_conv_w', 'v_pool_w', 'v_pool_scale', 'v_swa_sinks', 'v_mix_norm', 'v_w_o', 'v_ffn_norm', 'v_w_gate_up', 'v_w_down', 'v_final_norm']
TWIN_OUTPUTS = ['loss', 'grad_x', 'grad_attn_norm', 'grad_w_in', 'grad_mla_q_norm', 'grad_w_uq', 'grad_mla_kv_norm', 'grad_w_ukv', 'grad_conv_w', 'grad_pool_w', 'grad_pool_scale', 'grad_swa_sinks', 'grad_mix_norm', 'grad_w_o', 'grad_ffn_norm', 'grad_w_gate_up', 'grad_w_down', 'grad_final_norm', 'delta_attn_norm', 'delta_w_in', 'delta_mla_q_norm', 'delta_w_uq', 'delta_mla_kv_norm', 'delta_w_ukv', 'delta_conv_w', 'delta_pool_w', 'delta_pool_scale', 'delta_swa_sinks', 'delta_mix_norm', 'delta_w_o', 'delta_ffn_norm', 'delta_w_gate_up', 'delta_w_down', 'delta_final_norm', 'new_m_attn_norm', 'new_m_w_in', 'new_m_mla_q_norm', 'new_m_w_uq', 'new_m_mla_kv_norm', 'new_m_w_ukv', 'new_m_conv_w', 'new_m_pool_w', 'new_m_pool_scale', 'new_m_swa_sinks', 'new_m_mix_norm', 'new_m_w_o', 'new_m_ffn_norm', 'new_m_w_gate_up', 'new_m_w_down', 'new_m_final_norm', 'new_v_attn_norm', 'new_v_w_in', 'new_v_mla_q_norm', 'new_v_w_uq', 'new_v_mla_kv_norm', 'new_v_w_ukv', 'new_v_conv_w', 'new_v_pool_w', 'new_v_pool_scale', 'new_v_swa_sinks', 'new_v_mix_norm', 'new_v_w_o', 'new_v_ffn_norm', 'new_v_w_gate_up', 'new_v_w_down', 'new_v_final_norm']
TWIN_LEAF_KINDS = {'loss': 'loss', 'grad_x': 'grad_x', 'grad_attn_norm': 'grad_w', 'grad_w_in': 'grad_w', 'grad_mla_q_norm': 'grad_w', 'grad_w_uq': 'grad_w', 'grad_mla_kv_norm': 'grad_w', 'grad_w_ukv': 'grad_w', 'grad_conv_w': 'grad_w', 'grad_pool_w': 'grad_w', 'grad_pool_scale': 'grad_w', 'grad_swa_sinks': 'grad_w', 'grad_mix_norm': 'grad_w', 'grad_w_o': 'grad_w', 'grad_ffn_norm': 'grad_w', 'grad_w_gate_up': 'grad_w', 'grad_w_down': 'grad_w', 'grad_final_norm': 'grad_w', 'delta_attn_norm': 'delta_w', 'delta_w_in': 'delta_w', 'delta_mla_q_norm': 'delta_w', 'delta_w_uq': 'delta_w', 'delta_mla_kv_norm': 'delta_w', 'delta_w_ukv': 'delta_w', 'delta_conv_w': 'delta_w', 'delta_pool_w': 'delta_w', 'delta_pool_scale': 'delta_w', 'delta_swa_sinks': 'delta_w', 'delta_mix_norm': 'delta_w', 'delta_w_o': 'delta_w', 'delta_ffn_norm': 'delta_w', 'delta_w_gate_up': 'delta_w', 'delta_w_down': 'delta_w', 'delta_final_norm': 'delta_w', 'new_m_attn_norm': 'new_m', 'new_m_w_in': 'new_m', 'new_m_mla_q_norm': 'new_m', 'new_m_w_uq': 'new_m', 'new_m_mla_kv_norm': 'new_m', 'new_m_w_ukv': 'new_m', 'new_m_conv_w': 'new_m', 'new_m_pool_w': 'new_m', 'new_m_pool_scale': 'new_m', 'new_m_swa_sinks': 'new_m', 'new_m_mix_norm': 'new_m', 'new_m_w_o': 'new_m', 'new_m_ffn_norm': 'new_m', 'new_m_w_gate_up': 'new_m', 'new_m_w_down': 'new_m', 'new_m_final_norm': 'new_m', 'new_v_attn_norm': 'new_v', 'new_v_w_in': 'new_v', 'new_v_mla_q_norm': 'new_v', 'new_v_w_uq': 'new_v', 'new_v_mla_kv_norm': 'new_v', 'new_v_w_ukv': 'new_v', 'new_v_conv_w': 'new_v', 'new_v_pool_w': 'new_v', 'new_v_pool_scale': 'new_v', 'new_v_swa_sinks': 'new_v', 'new_v_mix_norm': 'new_v', 'new_v_w_o': 'new_v', 'new_v_ffn_norm': 'new_v', 'new_v_w_gate_up': 'new_v', 'new_v_w_down': 'new_v', 'new_v_final_norm': 'new_v'}


def _forward(args):
    return _fwd_reference(*[args[k] for k in FWD_PARAMS])


def _output_shape():
    out = _jax.eval_shape(lambda: _forward(_fwd_setup_inputs(0)))
    return out.shape, out.dtype

N_MICROBATCH = 1
ADAM_LR = 0.001
ADAM_B1 = 0.9
ADAM_B2 = 0.999
ADAM_EPS = 1e-08
ADAM_WD = 0.01
ADAM_STEP = 10
PER_EXAMPLE_BATCH_AXIS = {'x': 0, 'loss_target': 0}
SHARED_INPUTS = []
_WEIGHT_DTYPES = {'attn_norm': _jnp.float32, 'w_in': _jnp.float32, 'mla_q_norm': _jnp.float32, 'w_uq': _jnp.float32, 'mla_kv_norm': _jnp.float32, 'w_ukv': _jnp.float32, 'conv_w': _jnp.float32, 'pool_w': _jnp.float32, 'pool_scale': _jnp.float32, 'swa_sinks': _jnp.float32, 'mix_norm': _jnp.float32, 'w_o': _jnp.float32, 'ffn_norm': _jnp.float32, 'w_gate_up': _jnp.float32, 'w_down': _jnp.float32, 'final_norm': _jnp.float32}
MOMENT_SCALE = {'attn_norm': 2.146362e-01, 'w_in': 1.431937e-01, 'mla_q_norm': 1.469700e-01, 'w_uq': 1.014621e-01, 'mla_kv_norm': 4.050055e-01, 'w_ukv': 1.452787e-01, 'conv_w': 1.348819e-01, 'pool_w': 1.328809e-01, 'pool_scale': 1.287423e-01, 'swa_sinks': 1.021694e-01, 'mix_norm': 1.380603e-01, 'w_o': 1.444320e-01, 'ffn_norm': 9.787500e-02, 'w_gate_up': 4.256658e-02, 'w_down': 6.937305e-02, 'final_norm': 3.208075e+01}


def _to_microbatches(a, axis):
    t = _jnp.moveaxis(a, axis, 0)
    t = t.reshape((N_MICROBATCH, t.shape[0] // N_MICROBATCH) + t.shape[1:])
    return _jnp.moveaxis(t, 1, axis + 1)


def setup_inputs(seed: int = 0) -> dict:
    inp = _fwd_setup_inputs(seed)
    key = _jax.random.fold_in(_jax.random.key(seed), 7919)
    shape, _ = _output_shape()
    out = dict(inp)
    out["loss_target"] = _jax.random.normal(_jax.random.fold_in(key, 0), shape, _jnp.float32)
    for i, name in enumerate(TWIN_WEIGHTS):
        w = inp[name].astype(_jnp.float32)
        if MOMENT_SCALE is None:
            s = _jnp.sqrt(_jnp.mean(_jnp.square(w)) + 1e-30)
        else:
            s = MOMENT_SCALE[name]
        km, kv = _jax.random.split(_jax.random.fold_in(key, i + 1))
        out[name] = w
        out["m_" + name] = s * _jax.random.normal(km, w.shape, _jnp.float32)
        out["v_" + name] = (s * s) * _jax.random.uniform(kv, w.shape, _jnp.float32, 0.5, 1.5)
    if N_MICROBATCH > 1:
        for name, axis in PER_EXAMPLE_BATCH_AXIS.items():
            out[name] = _to_microbatches(out[name], axis)
    return {'x': out['x'], 'attn_norm': out['attn_norm'], 'w_in': out['w_in'], 'mla_q_norm': out['mla_q_norm'], 'w_uq': out['w_uq'], 'mla_kv_norm': out['mla_kv_norm'], 'w_ukv': out['w_ukv'], 'conv_w': out['conv_w'], 'pool_w': out['pool_w'], 'pool_scale': out['pool_scale'], 'swa_sinks': out['swa_sinks'], 'mix_norm': out['mix_norm'], 'w_o': out['w_o'], 'ffn_norm': out['ffn_norm'], 'w_gate_up': out['w_gate_up'], 'w_down': out['w_down'], 'final_norm': out['final_norm'], 'loss_target': out['loss_target'], 'm_attn_norm': out['m_attn_norm'], 'm_w_in': out['m_w_in'], 'm_mla_q_norm': out['m_mla_q_norm'], 'm_w_uq': out['m_w_uq'], 'm_mla_kv_norm': out['m_mla_kv_norm'], 'm_w_ukv': out['m_w_ukv'], 'm_conv_w': out['m_conv_w'], 'm_pool_w': out['m_pool_w'], 'm_pool_scale': out['m_pool_scale'], 'm_swa_sinks': out['m_swa_sinks'], 'm_mix_norm': out['m_mix_norm'], 'm_w_o': out['m_w_o'], 'm_ffn_norm': out['m_ffn_norm'], 'm_w_gate_up': out['m_w_gate_up'], 'm_w_down': out['m_w_down'], 'm_final_norm': out['m_final_norm'], 'v_attn_norm': out['v_attn_norm'], 'v_w_in': out['v_w_in'], 'v_mla_q_norm': out['v_mla_q_norm'], 'v_w_uq': out['v_w_uq'], 'v_mla_kv_norm': out['v_mla_kv_norm'], 'v_w_ukv': out['v_w_ukv'], 'v_conv_w': out['v_conv_w'], 'v_pool_w': out['v_pool_w'], 'v_pool_scale': out['v_pool_scale'], 'v_swa_sinks': out['v_swa_sinks'], 'v_mix_norm': out['v_mix_norm'], 'v_w_o': out['v_w_o'], 'v_ffn_norm': out['v_ffn_norm'], 'v_w_gate_up': out['v_w_gate_up'], 'v_w_down': out['v_w_down'], 'v_final_norm': out['v_final_norm']}


def _loss(weights, diff, rest, loss_target):
    with _jax.named_scope("forward"):
        args = {**rest, TWIN_DIFF_INPUT: diff, **{k: w.astype(_WEIGHT_DTYPES[k]) for k, w in weights.items()}}
        y = _forward(args)
    with _jax.named_scope("loss_head"):
        err = _jnp.square(y.astype(_jnp.float32) - loss_target)
        return 0.5 * _jnp.sum(_jnp.mean(err, axis=-1)) if err.ndim else 0.5 * err


def _adamw(w, g, m, v):
    m = ADAM_B1 * m + (1.0 - ADAM_B1) * g
    v = ADAM_B2 * v + (1.0 - ADAM_B2) * _jnp.square(g)
    m_hat = m / (1.0 - ADAM_B1 ** ADAM_STEP)
    v_hat = v / (1.0 - ADAM_B2 ** ADAM_STEP)
    delta = -ADAM_LR * (m_hat / (_jnp.sqrt(v_hat) + ADAM_EPS) + ADAM_WD * w)
    return delta, m, v


def reference(x, attn_norm, w_in, mla_q_norm, w_uq, mla_kv_norm, w_ukv, conv_w, pool_w, pool_scale, swa_sinks, mix_norm, w_o, ffn_norm, w_gate_up, w_down, final_norm, loss_target, m_attn_norm, m_w_in, m_mla_q_norm, m_w_uq, m_mla_kv_norm, m_w_ukv, m_conv_w, m_pool_w, m_pool_scale, m_swa_sinks, m_mix_norm, m_w_o, m_ffn_norm, m_w_gate_up, m_w_down, m_final_norm, v_attn_norm, v_w_in, v_mla_q_norm, v_w_uq, v_mla_kv_norm, v_w_ukv, v_conv_w, v_pool_w, v_pool_scale, v_swa_sinks, v_mix_norm, v_w_o, v_ffn_norm, v_w_gate_up, v_w_down, v_final_norm):
    given = dict(x=x, attn_norm=attn_norm, w_in=w_in, mla_q_norm=mla_q_norm, w_uq=w_uq, mla_kv_norm=mla_kv_norm, w_ukv=w_ukv, conv_w=conv_w, pool_w=pool_w, pool_scale=pool_scale, swa_sinks=swa_sinks, mix_norm=mix_norm, w_o=w_o, ffn_norm=ffn_norm, w_gate_up=w_gate_up, w_down=w_down, final_norm=final_norm, loss_target=loss_target, m_attn_norm=m_attn_norm, m_w_in=m_w_in, m_mla_q_norm=m_mla_q_norm, m_w_uq=m_w_uq, m_mla_kv_norm=m_mla_kv_norm, m_w_ukv=m_w_ukv, m_conv_w=m_conv_w, m_pool_w=m_pool_w, m_pool_scale=m_pool_scale, m_swa_sinks=m_swa_sinks, m_mix_norm=m_mix_norm, m_w_o=m_w_o, m_ffn_norm=m_ffn_norm, m_w_gate_up=m_w_gate_up, m_w_down=m_w_down, m_final_norm=m_final_norm, v_attn_norm=v_attn_norm, v_w_in=v_w_in, v_mla_q_norm=v_mla_q_norm, v_w_uq=v_w_uq, v_mla_kv_norm=v_mla_kv_norm, v_w_ukv=v_w_ukv, v_conv_w=v_conv_w, v_pool_w=v_pool_w, v_pool_scale=v_pool_scale, v_swa_sinks=v_swa_sinks, v_mix_norm=v_mix_norm, v_w_o=v_w_o, v_ffn_norm=v_ffn_norm, v_w_gate_up=v_w_gate_up, v_w_down=v_w_down, v_final_norm=v_final_norm)
    weights = {n: given[n] for n in TWIN_WEIGHTS}
    shared = {n: given[n] for n in SHARED_INPUTS}
    per_example = {n: given[n] for n in ['x']}
    grad_fn = _jax.value_and_grad(_loss, argnums=(0, 1))

    def one_microbatch(ex, loss_target):
        ex = dict(ex)
        diff = ex.pop(TWIN_DIFF_INPUT)
        return grad_fn(weights, diff, {**shared, **ex}, loss_target)

    if N_MICROBATCH == 1:
        loss, (grad_w, grad_x) = one_microbatch(per_example, given["loss_target"])
    else:
        def body(carry, xs):
            loss_sum, grad_sum = carry
            l_k, (gw_k, gx_k) = one_microbatch(xs[0], xs[1])
            with _jax.named_scope("update"):
                return (loss_sum + l_k, _jax.tree.map(_jnp.add, grad_sum, gw_k)), gx_k

        init = (_jnp.zeros((), _jnp.float32), _jax.tree.map(_jnp.zeros_like, weights))
        (loss, grad_w), grad_x = _jax.lax.scan(body, init, (per_example, given["loss_target"]))
    with _jax.named_scope("update"):
        delta_w, new_m, new_v = {}, {}, {}
        for n in TWIN_WEIGHTS:
            delta_w[n], new_m[n], new_v[n] = _adamw(weights[n], grad_w[n], given["m_" + n], given["v_" + n])
    return (loss, grad_x, *[grad_w[n] for n in TWIN_WEIGHTS], *[delta_w[n] for n in TWIN_WEIGHTS],
            *[new_m[n] for n in TWIN_WEIGHTS], *[new_v[n] for n in TWIN_WEIGHTS])
```

```python
import functools
import math

import numpy as np
import jax
import jax.numpy as jnp
from jax import lax
from jax.experimental import pallas as pl
from jax.experimental.pallas import tpu as pltpu

F32, BF16 = jnp.float32, jnp.bfloat16
SDS = jax.ShapeDtypeStruct
BS = pl.BlockSpec
MESH = pl.DeviceIdType.MESH

D = 1024
DEPTH = 2
Q_RANK, KV_RANK, ROPE, NOPE, HEADS = 256, 128, 32, 64, 4
D_FF = 2816
D_IN = 1952
D_INP = 2048
EPS = 1e-6
SWA_WINDOW = 128
BLK = 128
SLOPES = tuple(2.0 ** (-8.0 * (i + 1) / 4) for i in range(4))
QK_SCALE = 1.0 / math.sqrt(NOPE + ROPE)
SWA_SCALE = 1.0 / math.sqrt(64)
LR, B1, B2, ADAM_EPS, WD, STEP = 0.001, 0.9, 0.999, 1e-08, 0.01, 10

LANES = 1024
VMEM_LIMIT = 48 * 1024 * 1024
NEG_INF = float("-inf")

C_CQ, C_CKV, C_KR, C_GB, C_GC, C_UC, C_UP, C_QS, C_KS, C_VS = 0, 256, 384, 512, 768, 1024, 1280, 1536, 1792, 1920


def _pc(body, *, name, grid, in_specs, out_specs, out_shape, scratch=()):
    return pl.pallas_call(
        body, name=name, grid=grid, in_specs=in_specs, out_specs=out_specs, out_shape=out_shape,
        scratch_shapes=scratch,
        compiler_params=pltpu.CompilerParams(dimension_semantics=("arbitrary",) * len(grid),
                                             vmem_limit_bytes=VMEM_LIMIT))


def _dot(a, b):
    return jnp.dot(a, b, preferred_element_type=F32)


def _dot_nt(a, b):
    return lax.dot_general(a, b, (((1,), (1,)), ((), ())), preferred_element_type=F32)


def _dot_tn(a, b):
    return lax.dot_general(a, b, (((0,), (0,)), ((), ())), preferred_element_type=F32)


def _tile(n, cap):
    if n <= cap:
        return n
    t = cap - cap % 128
    while n % t:
        t -= 128
    return t


def _norm_mm(x, g, w, name, tn):
    S, K = x.shape
    N = w.shape[1]
    tm = min(512, S)

    def body(x_ref, g_ref, w_ref, y_ref, h_ref):
        xv = x_ref[...]
        r = lax.rsqrt(jnp.mean(xv * xv, axis=-1, keepdims=True) + EPS)
        h = (xv * r * g_ref[...]).astype(BF16)
        h_ref[...] = h
        y_ref[...] = _dot(h, w_ref[...])

    return _pc(body, name=name, grid=(S // tm, N // tn),
               in_specs=[BS((tm, K), lambda i, j: (i, 0)), BS((1, K), lambda i, j: (0, 0)),
                         BS((K, tn), lambda i, j: (0, j))],
               out_specs=[BS((tm, tn), lambda i, j: (i, j)), BS((tm, K), lambda i, j: (i, 0))],
               out_shape=[SDS((S, N), F32), SDS((S, K), BF16)])(x, g, w)


def _mix_out(x0, oh, yb, yc, yd, gmix, wo, name):
    S = x0.shape[0]
    tm = min(512, S)

    def body(x_ref, oh_ref, yb_ref, yc_ref, yd_ref, g_ref, w_ref, x1_ref, ycat_ref, mixed_ref):
        ya0 = oh_ref[:, 0:128] + oh_ref[:, 128:256]
        ya1 = oh_ref[:, 256:384] + oh_ref[:, 384:512]
        groups = [jnp.concatenate([ya0, ya1], axis=1), yb_ref[...], yc_ref[...], yd_ref[...]]
        for gi, yg in enumerate(groups):
            sl = slice(gi * 256, (gi + 1) * 256)
            r = lax.rsqrt(jnp.mean(yg * yg, axis=-1, keepdims=True) + EPS)
            ycat_ref[:, sl] = yg
            mixed_ref[:, sl] = (yg * r * g_ref[:, sl]).astype(BF16)
        x1_ref[...] = x_ref[...] + _dot(mixed_ref[...], w_ref[...])

    row = lambda w: BS((tm, w), lambda i: (i, 0))
    return _pc(body, name=name, grid=(S // tm,),
               in_specs=[row(D), row(512), row(256), row(256), row(256), BS((1, D), lambda i: (0, 0)),
                         BS((D, D), lambda i: (0, 0))],
               out_specs=[row(D), row(D), row(D)],
               out_shape=[SDS((S, D), F32), SDS((S, D), F32), SDS((S, D), BF16)])(x0, oh, yb, yc, yd, gmix, wo)


def _swiglu_mm_res(x1, gu, wdown, name):
    S = x1.shape[0]
    tm = min(256, S)

    def body(x_ref, gate_ref, up_ref, w_ref, x2_ref, act_ref):
        gt = gate_ref[...]
        act = (gt / (1.0 + jnp.exp(-gt)) * up_ref[...]).astype(BF16)
        act_ref[...] = act
        x2_ref[...] = x_ref[...] + _dot(act, w_ref[...])

    return _pc(body, name=name, grid=(S // tm,),
               in_specs=[BS((tm, D), lambda i: (i, 0)), BS((tm, D_FF), lambda i: (i, 0)),
                         BS((tm, D_FF), lambda i: (i, 1)), BS((D_FF, D), lambda i: (0, 0))],
               out_specs=[BS((tm, D), lambda i: (i, 0)), BS((tm, D_FF), lambda i: (i, 0))],
               out_shape=[SDS((S, D), F32), SDS((S, D_FF), BF16)])(x1, gu, gu, wdown)


def _loss_head(x, g, tgt, name):
    S = x.shape[0]
    tm = min(512, S)

    def body(x_ref, g_ref, t_ref, dx_ref, dg_ref, loss_ref):
        @pl.when(pl.program_id(0) == 0)
        def _():
            dg_ref[...] = jnp.zeros_like(dg_ref)
            loss_ref[...] = jnp.zeros_like(loss_ref)

        xv = x_ref[...]
        r = lax.rsqrt(jnp.mean(xv * xv, axis=-1, keepdims=True) + EPS)
        xh = xv * r
        gv = g_ref[...]
        diff = xh * gv - t_ref[...]
        loss_ref[...] += jnp.sum(diff * diff)
        dy = diff * (1.0 / D)
        dg_ref[...] += jnp.sum(dy * xh, axis=0, keepdims=True)
        dxh = dy * gv
        dx_ref[...] = r * (dxh - xh * jnp.mean(dxh * xh, axis=-1, keepdims=True))

    return _pc(body, name=name, grid=(S // tm,),
               in_specs=[BS((tm, D), lambda i: (i, 0)), BS((1, D), lambda i: (0, 0)), BS((tm, D), lambda i: (i, 0))],
               out_specs=[BS((tm, D), lambda i: (i, 0)), BS((8, D), lambda i: (0, 0)), BS((8, 128), lambda i: (0, 0))],
               out_shape=[SDS((S, D), F32), SDS((8, D), F32), SDS((8, 128), F32)])(x, g, tgt)


def _mm_tn(a, b, name):
    S, Ka = a.shape
    N = b.shape[1]
    ta, tn = _tile(Ka, 512), _tile(N, 512)

    def body(a_ref, b_ref, o_ref):
        o_ref[...] = _dot_tn(a_ref[...], b_ref[...])

    return _pc(body, name=name, grid=(Ka // ta, N // tn),
               in_specs=[BS((S, ta), lambda i, j: (0, i)), BS((S, tn), lambda i, j: (0, j))],
               out_specs=BS((ta, tn), lambda i, j: (i, j)),
               out_shape=SDS((Ka, N), F32))(a, b)


def _bwd_down(dx2, wdown, gu, name):
    S = dx2.shape[0]
    tm = min(256, S)

    def body(dx_ref, w_ref, gate_ref, up_ref, dgu_ref):
        dact = _dot_nt(dx_ref[...].astype(BF16), w_ref[...])
        gt = gate_ref[...]
        sg = 1.0 / (1.0 + jnp.exp(-gt))
        dgu_ref[:, 0:D_FF] = (dact * up_ref[...] * (sg * (1.0 + gt * (1.0 - sg)))).astype(BF16)
        dgu_ref[:, D_FF:2 * D_FF] = (dact * (gt * sg)).astype(BF16)

    return _pc(body, name=name, grid=(S // tm,),
               in_specs=[BS((tm, D), lambda i: (i, 0)), BS((D_FF, D), lambda i: (0, 0)),
                         BS((tm, D_FF), lambda i: (i, 0)), BS((tm, D_FF), lambda i: (i, 1))],
               out_specs=BS((tm, 2 * D_FF), lambda i: (i, 0)),
               out_shape=SDS((S, 2 * D_FF), BF16))(dx2, wdown, gu, gu)


def _mm_nt_normbwd(dy, w, x, g, dres, ngroups, name):
    S, K = dy.shape
    tm = min(256, S)
    gw = D // ngroups
    has_res = dres is not None

    def body(*refs):
        if has_res:
            dy_ref, w_ref, x_ref, g_ref, res_ref, dx_ref, dg_ref = refs
        else:
            dy_ref, w_ref, x_ref, g_ref, dx_ref, dg_ref = refs

        @pl.when(pl.program_id(0) == 0)
        def _():
            dg_ref[...] = jnp.zeros_like(dg_ref)

        dh = _dot_nt(dy_ref[...].astype(BF16), w_ref[...])
        for gi in range(ngroups):
            sl = slice(gi * gw, (gi + 1) * gw)
            xg = x_ref[:, sl]
            r = lax.rsqrt(jnp.mean(xg * xg, axis=-1, keepdims=True) + EPS)
            xh = xg * r
            dhg = dh[:, sl]
            dg_ref[:, sl] += jnp.sum(dhg * xh, axis=0, keepdims=True)
            dxh = dhg * g_ref[:, sl]
            dxg = r * (dxh - xh * jnp.mean(dxh * xh, axis=-1, keepdims=True))
            if has_res:
                dxg = dxg + res_ref[:, sl]
            dx_ref[:, sl] = dxg

    in_specs = [BS((tm, K), lambda i: (i, 0)), BS((D, K), lambda i: (0, 0)), BS((tm, D), lambda i: (i, 0)),
                BS((1, D), lambda i: (0, 0))]
    args = [dy, w, x, g]
    if has_res:
        in_specs.append(BS((tm, D), lambda i: (i, 0)))
        args.append(dres)
    return _pc(body, name=name, grid=(S // tm,), in_specs=in_specs,
               out_specs=[BS((tm, D), lambda i: (i, 0)), BS((8, D), lambda i: (0, 0))],
               out_shape=[SDS((S, D), F32), SDS((8, D), F32)])(*args)


def _rope(x, c, s1, s2):
    return x * c + pltpu.roll(x, 112, axis=1) * s1 + pltpu.roll(x, 16, axis=1) * s2


def _rope_t(dy, c, s1, s2):
    return dy * c + pltpu.roll(dy * s1, 16, axis=1) + pltpu.roll(dy * s2, 112, axis=1)


def _mla_prep(proj, gq, gkv, wuq, wk, wv, tabs, name):
    S = proj.shape[0]
    tm = min(512, S)
    tc, ts1, ts2 = tabs

    def body(cq_ref, ckv_ref, kr_ref, gq_ref, gkv_ref, wuq_ref, wk_ref, wv_ref, c_ref, s1_ref, s2_ref,
             q_ref, k_ref, v_ref):
        c, s1, s2 = c_ref[...], s1_ref[...], s2_ref[...]
        cq = cq_ref[...]
        rq = lax.rsqrt(jnp.mean(cq * cq, axis=-1, keepdims=True) + EPS)
        qa = _dot((cq * rq * gq_ref[...]).astype(BF16), wuq_ref[...])
        ckv = ckv_ref[...]
        rkv = lax.rsqrt(jnp.mean(ckv * ckv, axis=-1, keepdims=True) + EPS)
        ckvn = (ckv * rkv * gkv_ref[...]).astype(BF16)
        ka = _dot(ckvn, wk_ref[...])
        v_ref[...] = _dot(ckvn, wv_ref[...]).astype(BF16)
        krr = _rope(kr_ref[...], c, s1, s2)
        for h in range(HEADS):
            sl = slice(h * 128, (h + 1) * 128)
            q_ref[:, sl] = (_rope(qa[:, sl], c, s1, s2) * QK_SCALE).astype(BF16)
            k_ref[:, sl] = (ka[:, sl] + krr).astype(BF16)

    full = lambda a, b: BS((a, b), lambda i: (0, 0))
    tab = BS((tm, 128), lambda i: (i, 0))
    return _pc(body, name=name, grid=(S // tm,),
               in_specs=[BS((tm, 256), lambda i: (i, 0)), BS((tm, 128), lambda i: (i, 2)), BS((tm, 128), lambda i: (i, 3)),
                         full(1, 256), full(1, 128), full(256, 512), full(128, 512), full(128, 512), tab, tab, tab],
               out_specs=[BS((tm, 512), lambda i: (i, 0))] * 3,
               out_shape=[SDS((S, 512), BF16)] * 3)(proj, proj, proj, gq, gkv, wuq, wk, wv, tc, ts1, ts2)


def _mla_prep_bwd(dq, dk, dv, proj, gq, gkv, wuq, wk, wv, tabs, name):
    S = proj.shape[0]
    tm = min(512, S)
    tc, ts1, ts2 = tabs

    def body(dq_ref, dk_ref, dv_ref, cq_ref, ckv_ref, gq_ref, gkv_ref, wuq_ref, wk_ref, wv_ref, c_ref, s1_ref, s2_ref,
             dcq_ref, dckv_ref, dkr_ref, dwuq_ref, dwk_ref, dwv_ref, dgq_ref, dgkv_ref):
        @pl.when(pl.program_id(0) == 0)
        def _():
            for r in (dwuq_ref, dwk_ref, dwv_ref, dgq_ref, dgkv_ref):
                r[...] = jnp.zeros_like(r)

        c, s1, s2 = c_ref[...], s1_ref[...], s2_ref[...]
        dqp = jnp.concatenate(
            [_rope_t(dq_ref[:, h * 128:(h + 1) * 128] * QK_SCALE, c, s1, s2) for h in range(HEADS)], axis=1).astype(BF16)
        cq = cq_ref[...]
        rq = lax.rsqrt(jnp.mean(cq * cq, axis=-1, keepdims=True) + EPS)
        cqh = cq * rq
        gq_v = gq_ref[...]
        dwuq_ref[...] += _dot_tn((cqh * gq_v).astype(BF16), dqp)
        dcqn = _dot_nt(dqp, wuq_ref[...])
        dgq_ref[...] += jnp.sum(dcqn * cqh, axis=0, keepdims=True)
        dxh = dcqn * gq_v
        dcq_ref[...] = (rq * (dxh - cqh * jnp.mean(dxh * cqh, axis=-1, keepdims=True))).astype(BF16)

        dkb = dk_ref[...].astype(BF16)
        dvb = dv_ref[...].astype(BF16)
        ckv = ckv_ref[...]
        rkv = lax.rsqrt(jnp.mean(ckv * ckv, axis=-1, keepdims=True) + EPS)
        ckh = ckv * rkv
        gkv_v = gkv_ref[...]
        ckvn = (ckh * gkv_v).astype(BF16)
        dwk_ref[...] += _dot_tn(ckvn, dkb)
        dwv_ref[...] += _dot_tn(ckvn, dvb)
        dckvn = _dot_nt(dkb, wk_ref[...]) + _dot_nt(dvb, wv_ref[...])
        dgkv_ref[...] += jnp.sum(dckvn * ckh, axis=0, keepdims=True)
        dyh = dckvn * gkv_v
        dckv_ref[...] = (rkv * (dyh - ckh * jnp.mean(dyh * ckh, axis=-1, keepdims=True))).astype(BF16)
        dks = dk_ref[:, 0:128] + dk_ref[:, 128:256] + dk_ref[:, 256:384] + dk_ref[:, 384:512]
        dkr_ref[...] = _rope_t(dks, c, s1, s2).astype(BF16)

    full = lambda a, b: BS((a, b), lambda i: (0, 0))
    tab = BS((tm, 128), lambda i: (i, 0))
    row = lambda w: BS((tm, w), lambda i: (i, 0))
    return _pc(body, name=name, grid=(S // tm,),
               in_specs=[row(512), row(512), row(512), BS((tm, 256), lambda i: (i, 0)), BS((tm, 128), lambda i: (i, 2)),
                         full(1, 256), full(1, 128), full(256, 512), full(128, 512), full(128, 512), tab, tab, tab],
               out_specs=[row(256), row(128), row(128), full(256, 512), full(128, 512), full(128, 512),
                          full(8, 256), full(8, 128)],
               out_shape=[SDS((S, 256), BF16), SDS((S, 128), BF16), SDS((S, 128), BF16), SDS((256, 512), F32),
                          SDS((128, 512), F32), SDS((128, 512), F32), SDS((8, 256), F32), SDS((8, 128), F32)])(
        dq, dk, dv, proj, proj, gq, gkv, wuq, wk, wv, tc, ts1, ts2)


def _mla_attn(q, k, v, name):
    S = q.shape[0]
    t = min(512, S)
    n = S // t

    def body(q_ref, k_ref, v_ref, o_ref, lse_ref, m_sc, l_sc, acc_sc):
        i, j = pl.program_id(1), pl.program_id(2)

        @pl.when(j == 0)
        def _():
            m_sc[...] = jnp.full_like(m_sc, NEG_INF)
            l_sc[...] = jnp.zeros_like(l_sc)
            acc_sc[...] = jnp.zeros_like(acc_sc)

        @pl.when(j <= i)
        def _():
            s = _dot_nt(q_ref[...], k_ref[...])
            row = lax.broadcasted_iota(jnp.int32, (t, t), 0) + i * t
            col = lax.broadcasted_iota(jnp.int32, (t, t), 1) + j * t
            s = jnp.where(col <= row, s, NEG_INF)
            m_prev = m_sc[...]
            m_new = jnp.maximum(m_prev, jnp.max(s, axis=-1, keepdims=True))
            p = jnp.exp(s - m_new)
            alpha = jnp.exp(m_prev - m_new)
            l_sc[...] = alpha * l_sc[...] + jnp.sum(p, axis=-1, keepdims=True)
            acc_sc[...] = alpha * acc_sc[...] + _dot(p.astype(BF16), v_ref[...])
            m_sc[...] = m_new

        @pl.when(j == i)
        def _():
            o_ref[...] = acc_sc[...] / l_sc[...]
            lse_ref[...] = jnp.broadcast_to(m_sc[...] + jnp.log(l_sc[...]), (t, 128))

    qs = BS((t, 128), lambda h, i, j: (i, h))
    ks = BS((t, 128), lambda h, i, j: (jnp.minimum(j, i), h))
    return _pc(body, name=name, grid=(HEADS, n, n), in_specs=[qs, ks, ks], out_specs=[qs, qs],
               out_shape=[SDS((S, 512), F32), SDS((S, 512), F32)],
               scratch=[pltpu.VMEM((t, 1), F32), pltpu.VMEM((t, 1), F32), pltpu.VMEM((t, 128), F32)])(q, k, v)


def _mla_attn_bwd(q, k, v, dya, oh, lse, name):
    S = q.shape[0]
    t = min(512, S)
    n = S // t

    def body(q_ref, k_ref, v_ref, do_ref, oh_ref, lse_ref, dq_ref, dk_ref, dv_ref):
        j, i = pl.program_id(1), pl.program_id(2)

        @pl.when((j == 0) & (i == 0))
        def _():
            dq_ref[...] = jnp.zeros_like(dq_ref)

        @pl.when(i == 0)
        def _():
            dk_ref[...] = jnp.zeros_like(dk_ref)
            dv_ref[...] = jnp.zeros_like(dv_ref)

        @pl.when(i >= j)
        def _():
            qv, kv = q_ref[...], k_ref[...]
            s = _dot_nt(qv, kv)
            row = lax.broadcasted_iota(jnp.int32, (t, t), 0) + i * t
            col = lax.broadcasted_iota(jnp.int32, (t, t), 1) + j * t
            p = jnp.where(col <= row, jnp.exp(s - lse_ref[:, 0:1]), 0.0)
            do = do_ref[...]
            delta = jnp.sum(do * oh_ref[...], axis=-1, keepdims=True)
            dob = do.astype(BF16)
            dv_ref[...] += _dot_tn(p.astype(BF16), dob)
            dp = _dot_nt(dob, v_ref[...])
            ds = (p * (dp - delta)).astype(BF16)
            rows = pl.ds(pl.multiple_of(i * t, t), t)
            dq_ref[rows, :] += _dot(ds, kv)
            dk_ref[...] += _dot_tn(ds, qv)

    qs = BS((t, 128), lambda h, j, i: (jnp.maximum(i, j), h))
    ks = BS((t, 128), lambda h, j, i: (j, h))
    dos = BS((t, 128), lambda h, j, i: (jnp.maximum(i, j), h // 2))
    return _pc(body, name=name, grid=(HEADS, n, n), in_specs=[qs, ks, ks, dos, qs, qs],
               out_specs=[BS((S, 128), lambda h, j, i: (0, h)), ks, ks],
               out_shape=[SDS((S, 512), F32)] * 3)(q, k, v, dya, oh, lse)


def _swa_scores(qm, kk, valid, distf, slope, sink):
    sc = _dot_nt(qm, kk) * SWA_SCALE
    sc = jnp.where(valid, sc - slope * distf, NEG_INF)
    m = jnp.maximum(jnp.max(sc, axis=-1, keepdims=True), sink)
    e = jnp.exp(sc - m)
    esink = jnp.exp(sink - m)
    den = jnp.sum(e, axis=-1, keepdims=True) + esink
    return e / den, esink / den


def _swa_masks():
    r = lax.broadcasted_iota(jnp.int32, (BLK, 2 * BLK), 0)
    c = lax.broadcasted_iota(jnp.int32, (BLK, 2 * BLK), 1)
    dist = r + BLK - c
    return (dist >= 0) & (dist < SWA_WINDOW), c >= BLK, dist.astype(F32)


def _swa(proj, sinks, name):
    S = proj.shape[0]
    nb = S // BLK

    def body(q_ref, k_ref, v_ref, sink_ref, o_ref, kp, vp):
        kp[0:BLK, :] = jnp.zeros((BLK, 128), BF16)
        vp[0:BLK, :] = jnp.zeros((BLK, 128), BF16)
        kp[BLK:, :] = k_ref[...].astype(BF16)
        vp[BLK:, :] = v_ref[...].astype(BF16)
        lo = lax.broadcasted_iota(jnp.int32, (BLK, 128), 1) < 64
        band, cur, distf = _swa_masks()

        def blk(i, carry):
            st = pl.multiple_of(i * BLK, BLK)
            kk = kp[pl.ds(st, 2 * BLK), :]
            vv = vp[pl.ds(st, 2 * BLK), :]
            valid = band & (cur | (i > 0))
            for b in range(2):
                qb = q_ref[pl.ds(st, BLK), b * 128:(b + 1) * 128]
                outs = []
                for s in range(2):
                    h = b + 2 * s
                    qm = jnp.where(lo if s == 0 else ~lo, qb, 0.0).astype(BF16)
                    p, _ = _swa_scores(qm, kk, valid, distf, SLOPES[h], sink_ref[h])
                    outs.append(_dot(p.astype(BF16), vv))
                o_ref[pl.ds(st, BLK), b * 128:(b + 1) * 128] = jnp.where(lo, outs[0], outs[1])
            return carry

        lax.fori_loop(0, nb, blk, 0)

    return _pc(body, name=name, grid=(1,),
               in_specs=[BS((S, 256), lambda i: (0, C_QS // 256)), BS((S, 128), lambda i: (0, C_KS // 128)),
                         BS((S, 128), lambda i: (0, C_VS // 128)), BS(memory_space=pltpu.SMEM)],
               out_specs=BS((S, 256), lambda i: (0, 0)),
               out_shape=SDS((S, 256), F32),
               scratch=[pltpu.VMEM((S + BLK, 128), BF16), pltpu.VMEM((S + BLK, 128), BF16)])(proj, proj, proj, sinks)


def _swa_bwd(proj, sinks, dyd, name):
    S = proj.shape[0]
    nb = S // BLK

    def body(q_ref, k_ref, v_ref, sink_ref, do_ref, dq_ref, dk_ref, dv_ref, dsink_ref, kp, vp, dkp, dvp):
        kp[0:BLK, :] = jnp.zeros((BLK, 128), BF16)
        vp[0:BLK, :] = jnp.zeros((BLK, 128), BF16)
        kp[BLK:, :] = k_ref[...].astype(BF16)
        vp[BLK:, :] = v_ref[...].astype(BF16)
        dkp[...] = jnp.zeros_like(dkp)
        dvp[...] = jnp.zeros_like(dvp)
        lo = lax.broadcasted_iota(jnp.int32, (BLK, 128), 1) < 64
        lane8 = lax.broadcasted_iota(jnp.int32, (8, 128), 1)
        band, cur, distf = _swa_masks()

        def blk(i, dsink):
            st = pl.multiple_of(i * BLK, BLK)
            kk = kp[pl.ds(st, 2 * BLK), :]
            vv = vp[pl.ds(st, 2 * BLK), :]
            valid = band & (cur | (i > 0))
            dkk = jnp.zeros((2 * BLK, 128), F32)
            dvv = jnp.zeros((2 * BLK, 128), F32)
            for b in range(2):
                qb = q_ref[pl.ds(st, BLK), b * 128:(b + 1) * 128]
                dob = do_ref[pl.ds(st, BLK), b * 128:(b + 1) * 128]
                dqs = []
                for s in range(2):
                    h = b + 2 * s
                    half = lo if s == 0 else ~lo
                    qm = jnp.where(half, qb, 0.0).astype(BF16)
                    dom = jnp.where(half, dob, 0.0).astype(BF16)
                    p, psink = _swa_scores(qm, kk, valid, distf, SLOPES[h], sink_ref[h])
                    dp = _dot_nt(dom, vv)
                    dvv = dvv + _dot_tn(p.astype(BF16), dom)
                    delta = jnp.sum(p * dp, axis=-1, keepdims=True)
                    dsink = dsink + jnp.where(lane8 == h, -jnp.sum(psink * delta), 0.0)
                    dsc = (p * (dp - delta) * SWA_SCALE).astype(BF16)
                    dqs.append(_dot(dsc, kk))
                    dkk = dkk + _dot_tn(dsc, qm)
                dq_ref[pl.ds(st, BLK), b * 128:(b + 1) * 128] = jnp.where(lo, dqs[0], dqs[1]).astype(BF16)
            dkp[pl.ds(st, 2 * BLK), :] += dkk
            dvp[pl.ds(st, 2 * BLK), :] += dvv
            return dsink

        dsink_ref[...] = lax.fori_loop(0, nb, blk, jnp.zeros((8, 128), F32))
        dk_ref[...] = dkp[BLK:, :].astype(BF16)
        dv_ref[...] = dvp[BLK:, :].astype(BF16)

    return _pc(body, name=name, grid=(1,),
               in_specs=[BS((S, 256), lambda i: (0, C_QS // 256)), BS((S, 128), lambda i: (0, C_KS // 128)),
                         BS((S, 128), lambda i: (0, C_VS // 128)), BS(memory_space=pltpu.SMEM),
                         BS((S, 256), lambda i: (0, 3))],
               out_specs=[BS((S, 256), lambda i: (0, 0)), BS((S, 128), lambda i: (0, 0)), BS((S, 128), lambda i: (0, 0)),
                          BS((8, 128), lambda i: (0, 0))],
               out_shape=[SDS((S, 256), BF16), SDS((S, 128), BF16), SDS((S, 128), BF16), SDS((8, 128), F32)],
               scratch=[pltpu.VMEM((S + BLK, 128), BF16), pltpu.VMEM((S + BLK, 128), BF16),
                        pltpu.VMEM((S + BLK, 128), F32), pltpu.VMEM((S + BLK, 128), F32)])(proj, proj, proj, sinks, dyd)


def _down(x, k, t):
    return jnp.where(t >= k, pltpu.roll(x, k, axis=0), 0.0)


def _up(x, k, t):
    n = x.shape[0]
    return jnp.where(t < n - k, pltpu.roll(x, n - k, axis=0), 0.0)


def _conv(proj, w8, name):
    S = proj.shape[0]

    def body(gb_ref, gc_ref, u_ref, w_ref, y_ref):
        t = lax.broadcasted_iota(jnp.int32, (S, 128), 0)
        z = gc_ref[...] * u_ref[...]
        c = w_ref[2:3, :] * z + w_ref[1:2, :] * _down(z, 1, t) + w_ref[0:1, :] * _down(z, 2, t)
        y_ref[...] = gb_ref[...] * c

    col = lambda c0: BS((S, 128), lambda i: (0, c0 // 128 + i))
    return _pc(body, name=name, grid=(2,),
               in_specs=[col(C_GB), col(C_GC), col(C_UC), BS((8, 128), lambda i: (0, i))],
               out_specs=BS((S, 128), lambda i: (0, i)), out_shape=SDS((S, 256), F32))(proj, proj, proj, w8)


def _conv_bwd(proj, w8, dycat, name):
    S = proj.shape[0]

    def body(gb_ref, gc_ref, u_ref, w_ref, dy_ref, dgb_ref, dgc_ref, du_ref, dw_ref):
        t = lax.broadcasted_iota(jnp.int32, (S, 128), 0)
        gc, u = gc_ref[...], u_ref[...]
        z = gc * u
        z1, z2 = _down(z, 1, t), _down(z, 2, t)
        w0, w1, w2 = w_ref[0:1, :], w_ref[1:2, :], w_ref[2:3, :]
        dy = dy_ref[...]
        dgb_ref[...] = (dy * (w2 * z + w1 * z1 + w0 * z2)).astype(BF16)
        dc = dy * gb_ref[...]
        dz = w2 * dc + w1 * _up(dc, 1, t) + w0 * _up(dc, 2, t)
        dgc_ref[...] = (dz * u).astype(BF16)
        du_ref[...] = (dz * gc).astype(BF16)
        row = lax.broadcasted_iota(jnp.int32, (8, 128), 0)
        sums = [jnp.sum(dc * zz, axis=0, keepdims=True) for zz in (z2, z1, z)]
        dw_ref[...] = jnp.where(row == 0, sums[0], jnp.where(row == 1, sums[1], jnp.where(row == 2, sums[2], 0.0)))

    col = lambda c0: BS((S, 128), lambda i: (0, c0 // 128 + i))
    out = BS((S, 128), lambda i: (0, i))
    return _pc(body, name=name, grid=(2,),
               in_specs=[col(C_GB), col(C_GC), col(C_UC), BS((8, 128), lambda i: (0, i)), col(256)],
               out_specs=[out, out, out, BS((8, 128), lambda i: (0, i))],
               out_shape=[SDS((S, 256), BF16)] * 3 + [SDS((8, 256), F32)])(proj, proj, proj, w8, dycat)


def _pool_parts(u, t, first):
    lo = lax.broadcasted_iota(jnp.int32, u.shape, 1) < 64
    s2 = u + _down(u, 1, t)
    s4 = s2 + _down(s2, 2, t)
    s8 = s4 + _down(s4, 4, t)
    s16 = s8 + _down(s8, 8, t)
    win = jnp.where(lo, jnp.where(first, s2, s8), jnp.where(first, s4, s16))
    wv = jnp.where(lo, jnp.where(first, 2, 8), jnp.where(first, 4, 16))
    cnt = jnp.minimum(t + 1, wv).astype(F32)
    return win, cnt, lo


def _pool(proj, pwd, scale, name):
    S = proj.shape[0]

    def body(u_ref, pw_ref, sc_ref, y_ref):
        t = lax.broadcasted_iota(jnp.int32, (S, 128), 0)
        u = u_ref[...]
        win, cnt, _ = _pool_parts(u, t, pl.program_id(0) == 0)
        pooled = win / cnt - u
        y_ref[...] = _dot(pooled.astype(BF16), pw_ref[...]) * sc_ref[...]

    return _pc(body, name=name, grid=(2,),
               in_specs=[BS((S, 128), lambda i: (0, C_UP // 128 + i)), BS((128, 128), lambda i: (i, 0)),
                         BS((1, 128), lambda i: (0, i))],
               out_specs=BS((S, 128), lambda i: (0, i)), out_shape=SDS((S, 256), F32))(proj, pwd, scale)


def _pool_bwd(proj, pwd, scale, dycat, name):
    S = proj.shape[0]

    def body(u_ref, pw_ref, sc_ref, dy_ref, du_ref, dpw_ref, dsc_ref):
        t = lax.broadcasted_iota(jnp.int32, (S, 128), 0)
        first = pl.program_id(0) == 0
        u = u_ref[...]
        win, cnt, lo = _pool_parts(u, t, first)
        pooled = (win / cnt - u).astype(BF16)
        pw = pw_ref[...]
        dy = dy_ref[...]
        dsc_ref[...] = jnp.broadcast_to(jnp.sum(dy * _dot(pooled, pw), axis=0, keepdims=True), (8, 128))
        dmb = (dy * sc_ref[...]).astype(BF16)
        dpw_ref[...] = _dot_tn(pooled, dmb)
        dpooled = _dot_nt(dmb, pw)
        a1 = dpooled / cnt
        a2 = a1 + _up(a1, 1, t)
        a4 = a2 + _up(a2, 2, t)
        a8 = a4 + _up(a4, 4, t)
        a16 = a8 + _up(a8, 8, t)
        dwin = jnp.where(lo, jnp.where(first, a2, a8), jnp.where(first, a4, a16))
        du_ref[...] = (dwin - dpooled).astype(BF16)

    return _pc(body, name=name, grid=(2,),
               in_specs=[BS((S, 128), lambda i: (0, C_UP // 128 + i)), BS((128, 128), lambda i: (i, 0)),
                         BS((1, 128), lambda i: (0, i)), BS((S, 128), lambda i: (0, 4 + i))],
               out_specs=[BS((S, 128), lambda i: (0, i)), BS((128, 128), lambda i: (i, 0)), BS((8, 128), lambda i: (0, i))],
               out_shape=[SDS((S, 256), BF16), SDS((256, 128), F32), SDS((8, 256), F32)])(proj, pwd, scale, dycat)


def _adamw(w, g, m, v, name):
    R = w.shape[0]
    tr = R
    for cand in (512, 448, 384, 256, 128, 64, 32, 16, 8):
        if R % cand == 0:
            tr = cand
            break

    def body(w_ref, g_ref, m_ref, v_ref, d_ref, nm_ref, nv_ref):
        gv = g_ref[...]
        m_new = B1 * m_ref[...] + (1.0 - B1) * gv
        v_new = B2 * v_ref[...] + (1.0 - B2) * (gv * gv)
        m_hat = m_new / (1.0 - B1 ** STEP)
        v_hat = v_new / (1.0 - B2 ** STEP)
        d_ref[...] = -LR * (m_hat / (jnp.sqrt(v_hat) + ADAM_EPS) + WD * w_ref[...])
        nm_ref[...] = m_new
        nv_ref[...] = v_new

    sp = BS((tr, LANES), lambda i: (i, 0))
    return _pc(body, name=name, grid=(R // tr,), in_specs=[sp] * 4, out_specs=[sp] * 3,
               out_shape=[SDS((R, LANES), F32)] * 3)(w, g, m, v)


def _pair_sum(g4, got, c_arr, name):
    _, _, H, _ = g4.shape
    tr = H // 7 if H % 7 == 0 and (H // 7) % 16 == 0 else H

    def body(c_ref, a_ref, b_ref, t32_ref, t16_ref):
        s = a_ref[0, 0] + b_ref[0]
        t32_ref[0] = s
        t16_ref[0] = s.astype(BF16)

    grid_spec = pltpu.PrefetchScalarGridSpec(
        num_scalar_prefetch=1, grid=(4, H // tr),
        in_specs=[BS((1, 1, tr, LANES), lambda k, t, c: (k, c[0], t, 0)), BS((1, tr, LANES), lambda k, t, c: (k, t, 0))],
        out_specs=[BS((1, tr, LANES), lambda k, t, c: (k, t, 0)), BS((1, tr, LANES), lambda k, t, c: (k, t, 0))])
    return pl.pallas_call(body, name=name, grid_spec=grid_spec,
                          out_shape=[SDS((4, H, LANES), F32), SDS((4, H, LANES), BF16)],
                          compiler_params=pltpu.CompilerParams(dimension_semantics=("arbitrary", "arbitrary"),
                                                               vmem_limit_bytes=VMEM_LIMIT))(c_arr, g4, got)


def _chip_sum(t32, got3, k_arr, name):
    _, H, _ = t32.shape
    tr = H // 7 if H % 7 == 0 and (H // 7) % 16 == 0 else H

    def body(k_ref, a_ref, b_ref, u_ref):
        u_ref[...] = ((a_ref[0] + b_ref[0].astype(F32)) + b_ref[1].astype(F32)) + b_ref[2].astype(F32)

    grid_spec = pltpu.PrefetchScalarGridSpec(
        num_scalar_prefetch=1, grid=(H // tr,),
        in_specs=[BS((1, tr, LANES), lambda t, k: (k[0], t, 0)), BS((3, tr, LANES), lambda t, k: (0, t, 0))],
        out_specs=BS((tr, LANES), lambda t, k: (t, 0)))
    return pl.pallas_call(body, name=name, grid_spec=grid_spec, out_shape=SDS((H, LANES), F32),
                          compiler_params=pltpu.CompilerParams(dimension_semantics=("arbitrary",),
                                                               vmem_limit_bytes=VMEM_LIMIT))(k_arr, t32, got3)


def _me():
    return lax.axis_index("x"), lax.axis_index("y"), lax.axis_index("c")


def _other_chips(x, y):
    return [(1 - x, y), (x, 1 - y), (1 - x, 1 - y)]


ANY = BS(memory_space=pl.ANY)
COMM_PARAMS = pltpu.CompilerParams(has_side_effects=True)


def _gather_shards(shard, name):
    R = shard.shape[0]
    H = R // 2

    def body(s_ref, o_ref, send_sems, recv_sems, local_sem):
        x, y, c = _me()
        sib = (x, y, 1 - c)
        chips = _other_chips(x, y)

        def blk(cx, cy, half):
            return o_ref.at[2 * cx + cy, pl.ds(half * H, H), :]

        mine = pltpu.make_async_copy(s_ref, o_ref.at[2 * x + y], local_sem)
        mine.start()

        def copy(k, dst, to, src=None):
            return pltpu.make_async_remote_copy(src_ref=dst if src is None else src, dst_ref=dst,
                                                send_sem=send_sems.at[k], recv_sem=recv_sems.at[k],
                                                device_id=to, device_id_type=MESH)

        first = [copy(j, blk(x, y, c), (*chip, c), src=s_ref.at[pl.ds(c * H, H), :]) for j, chip in enumerate(chips)]
        for cp in first:
            cp.start()
        passed = [copy(3 + j, blk(*chip, c), sib) for j, chip in enumerate(chips)]
        for j, chip in enumerate(chips):
            copy(j, blk(*chip, c), (x, y, c)).wait_recv()
            passed[j].start()
        for j, chip in enumerate(chips):
            copy(3 + j, blk(*chip, 1 - c), (x, y, c)).wait_recv()
        for cp in first + passed:
            cp.wait_send()
        mine.wait()

    return pl.pallas_call(body, name=name, out_shape=SDS((4, R, LANES), shard.dtype), in_specs=[ANY], out_specs=ANY,
                          scratch_shapes=[pltpu.SemaphoreType.DMA((6,)), pltpu.SemaphoreType.DMA((6,)),
                                          pltpu.SemaphoreType.DMA],
                          compiler_params=COMM_PARAMS)(shard)


def _swap_half(g4, name):
    _, _, H, _ = g4.shape

    def body(g_ref, o_ref, send_sem, recv_sem):
        x, y, c = _me()
        cp = pltpu.make_async_remote_copy(src_ref=g_ref.at[:, 1 - c], dst_ref=o_ref, send_sem=send_sem, recv_sem=recv_sem,
                                          device_id=(x, y, 1 - c), device_id_type=MESH)
        cp.start()
        cp.wait()

    return pl.pallas_call(body, name=name, out_shape=SDS((4, H, LANES), g4.dtype), in_specs=[ANY], out_specs=ANY,
                          scratch_shapes=[pltpu.SemaphoreType.DMA, pltpu.SemaphoreType.DMA],
                          compiler_params=COMM_PARAMS)(g4)


def _exchange_chips(t16, name):
    _, H, _ = t16.shape

    def body(t_ref, o_ref, send_sems, recv_sems):
        x, y, c = _me()
        cps = [pltpu.make_async_remote_copy(src_ref=t_ref.at[2 * cx + cy], dst_ref=o_ref.at[j],
                                            send_sem=send_sems.at[j], recv_sem=recv_sems.at[j],
                                            device_id=(cx, cy, c), device_id_type=MESH)
               for j, (cx, cy) in enumerate(_other_chips(x, y))]
        for cp in cps:
            cp.start()
        for cp in cps:
            cp.wait()

    return pl.pallas_call(body, name=name, out_shape=SDS((3, H, LANES), t16.dtype), in_specs=[ANY], out_specs=ANY,
                          scratch_shapes=[pltpu.SemaphoreType.DMA((3,)), pltpu.SemaphoreType.DMA((3,))],
                          compiler_params=COMM_PARAMS)(t16)


def _join_halves(u, name):
    H = u.shape[0]

    def body(u_ref, o_ref, send_sem, recv_sem, local_sem):
        x, y, c = _me()
        mine = pltpu.make_async_copy(u_ref, o_ref.at[c], local_sem)
        mine.start()
        cp = pltpu.make_async_remote_copy(src_ref=u_ref, dst_ref=o_ref.at[c], send_sem=send_sem, recv_sem=recv_sem,
                                          device_id=(x, y, 1 - c), device_id_type=MESH)
        cp.start()
        cp.wait()
        mine.wait()

    return pl.pallas_call(body, name=name, out_shape=SDS((2, H, LANES), u.dtype), in_specs=[ANY], out_specs=ANY,
                          scratch_shapes=[pltpu.SemaphoreType.DMA, pltpu.SemaphoreType.DMA, pltpu.SemaphoreType.DMA],
                          compiler_params=COMM_PARAMS)(u)


def _allsum_small(v, name):
    M = v.shape[0]

    def body(x_ref, o_ref, all_ref, send_sems, recv_sems, local_sem):
        x, y, c = _me()
        me, sib = (x, y, c), (x, y, 1 - c)
        chips = _other_chips(x, y)

        def rows(px, py, pc):
            return all_ref.at[pl.ds((4 * px + 2 * py + pc) * M, M), :]

        def copy(k, block, to, src=None):
            return pltpu.make_async_remote_copy(src_ref=rows(*block) if src is None else src, dst_ref=rows(*block),
                                                send_sem=send_sems.at[k], recv_sem=recv_sems.at[k],
                                                device_id=to, device_id_type=MESH)

        mine = pltpu.make_async_copy(x_ref, rows(*me), local_sem)
        mine.start()
        first = [copy(0, me, sib, src=x_ref)]
        first += [copy(1 + j, me, (*chip, c), src=x_ref) for j, chip in enumerate(chips)]
        for cp in first:
            cp.start()
        passed = [copy(4 + j, (*chip, c), sib) for j, chip in enumerate(chips)]
        for j, chip in enumerate(chips):
            copy(1 + j, (*chip, c), me).wait_recv()
            passed[j].start()
        copy(0, sib, me).wait_recv()
        for j, chip in enumerate(chips):
            copy(4 + j, (*chip, 1 - c), me).wait_recv()
        for cp in first + passed:
            cp.wait_send()
        mine.wait()
        acc = all_ref[0:M, :]
        for d in range(1, 8):
            acc = acc + all_ref[d * M:(d + 1) * M, :]
        o_ref[...] = acc

    vm = BS(memory_space=pltpu.VMEM)
    return pl.pallas_call(body, name=name, out_shape=SDS((M, LANES), F32), in_specs=[vm], out_specs=vm,
                          scratch_shapes=[pltpu.VMEM((8 * M, LANES), F32), pltpu.SemaphoreType.DMA((7,)),
                                          pltpu.SemaphoreType.DMA((7,)), pltpu.SemaphoreType.DMA],
                          compiler_params=COMM_PARAMS)(v)


BIG = ("w_in", "w_uq", "w_ukv", "w_o", "w_gate_up", "w_down", "conv_w")
SMALL = ("attn_norm", "mla_q_norm", "mla_kv_norm", "pool_w", "pool_scale", "swa_sinks", "mix_norm", "ffn_norm",
         "final_norm")
ORDER = ("attn_norm", "w_in", "mla_q_norm", "w_uq", "mla_kv_norm", "w_ukv", "conv_w", "pool_w", "pool_scale",
         "swa_sinks", "mix_norm", "w_o", "ffn_norm", "w_gate_up", "w_down", "final_norm")


def _rows_for(n, mult):
    r = -(-n // LANES)
    return -(-r // mult) * mult


def _pack(arrs, rows, dtype):
    flat = jnp.concatenate([a.reshape(-1).astype(dtype) for a in arrs])
    return jnp.pad(flat, (0, rows * LANES - flat.shape[0])).reshape(rows, LANES)


def _unpack(buf, shapes):
    flat = buf.reshape(-1)
    out, off = [], 0
    for s in shapes:
        n = int(np.prod(s))
        out.append(flat[off:off + n].reshape(s))
        off += n
    return out


def _rope_tables(S):
    inv = 1.0 / (10000.0 ** (jnp.arange(0, ROPE, 2, dtype=F32) / ROPE))
    ang = jnp.arange(S, dtype=F32)[:, None] * inv[None, :]
    cos, sin = jnp.cos(ang), jnp.sin(ang)
    z = lambda w: jnp.zeros((S, w), F32)
    tc = jnp.concatenate([jnp.ones((S, 64), F32), cos, cos, jnp.ones((S, 32), F32)], axis=1)
    ts1 = jnp.concatenate([z(64), -sin, z(48)], axis=1)
    ts2 = jnp.concatenate([z(80), sin, z(32)], axis=1)
    return tc, ts1, ts2


def _swap_mid(a, axis):
    b = jnp.split(a, 4, axis=axis)
    return jnp.concatenate([b[0], b[2], b[1], b[3]], axis=axis)


def _pad_w_in(w):
    z = lambda n: jnp.zeros(w.shape[:-1] + (n,), w.dtype)
    return jnp.concatenate([w[..., 0:384], z(64), w[..., 384:416], z(32), w[..., 416:1440],
                            _swap_mid(w[..., 1440:1696], -1), w[..., 1696:1952]], axis=-1)


def _unpad_w_in(d):
    return jnp.concatenate([d[..., 0:384], d[..., 448:480], d[..., 512:1536], _swap_mid(d[..., 1536:1792], -1),
                            d[..., 1792:2048]], axis=-1)


def _pad_heads(w, widths, offs):
    cols = []
    for h in range(HEADS):
        src0, n = widths[h]
        z = lambda k: jnp.zeros(w.shape[:-1] + (k,), w.dtype)
        cols += [z(offs[h]), w[..., src0:src0 + n], z(128 - offs[h] - n)]
    return jnp.concatenate(cols, axis=-1)


UQ_SRC = [(h * 96, 96) for h in range(HEADS)]
KN_SRC = [(h * 128, 64) for h in range(HEADS)]
V_SRC = [(h * 128 + 64, 64) for h in range(HEADS)]
ZERO_OFF = [0] * HEADS
V_OFF = [(h % 2) * 64 for h in range(HEADS)]


def _unpad_heads(d, widths, offs):
    return [d[..., h * 128 + offs[h]: h * 128 + offs[h] + widths[h][1]] for h in range(HEADS)]


def kernel(x, attn_norm, w_in, mla_q_norm, w_uq, mla_kv_norm, w_ukv, conv_w, pool_w, pool_scale, swa_sinks, mix_norm, w_o, ffn_norm, w_gate_up, w_down, final_norm, loss_target, m_attn_norm, m_w_in, m_mla_q_norm, m_w_uq, m_mla_kv_norm, m_w_ukv, m_conv_w, m_pool_w, m_pool_scale, m_swa_sinks, m_mix_norm, m_w_o, m_ffn_norm, m_w_gate_up, m_w_down, m_final_norm, v_attn_norm, v_w_in, v_mla_q_norm, v_w_uq, v_mla_kv_norm, v_w_ukv, v_conv_w, v_pool_w, v_pool_scale, v_swa_sinks, v_mix_norm, v_w_o, v_ffn_norm, v_w_gate_up, v_w_down, v_final_norm):
    W = dict(attn_norm=attn_norm, w_in=w_in, mla_q_norm=mla_q_norm, w_uq=w_uq, mla_kv_norm=mla_kv_norm, w_ukv=w_ukv,
             conv_w=conv_w, pool_w=pool_w, pool_scale=pool_scale, swa_sinks=swa_sinks, mix_norm=mix_norm, w_o=w_o,
             ffn_norm=ffn_norm, w_gate_up=w_gate_up, w_down=w_down, final_norm=final_norm)
    M1 = dict(attn_norm=m_attn_norm, w_in=m_w_in, mla_q_norm=m_mla_q_norm, w_uq=m_w_uq, mla_kv_norm=m_mla_kv_norm,
              w_ukv=m_w_ukv, conv_w=m_conv_w, pool_w=m_pool_w, pool_scale=m_pool_scale, swa_sinks=m_swa_sinks,
              mix_norm=m_mix_norm, w_o=m_w_o, ffn_norm=m_ffn_norm, w_gate_up=m_w_gate_up, w_down=m_w_down,
              final_norm=m_final_norm)
    V2 = dict(attn_norm=v_attn_norm, w_in=v_w_in, mla_q_norm=v_mla_q_norm, w_uq=v_w_uq, mla_kv_norm=v_mla_kv_norm,
              w_ukv=v_w_ukv, conv_w=v_conv_w, pool_w=v_pool_w, pool_scale=v_pool_scale, swa_sinks=v_swa_sinks,
              mix_norm=v_mix_norm, w_o=v_w_o, ffn_norm=v_ffn_norm, w_gate_up=v_w_gate_up, w_down=v_w_down,
              final_norm=v_final_norm)
    S = x.shape[1]
    xc, yc, cc = _me()
    c_arr = jnp.reshape(cc, (1,)).astype(jnp.int32)
    k_arr = jnp.reshape(2 * xc + yc, (1,)).astype(jnp.int32)

    big_shapes = [W[n].shape for n in BIG]
    parts = [W[n].astype(BF16) for n in BIG[:-1]] + [lax.bitcast_convert_type(conv_w, BF16)]
    n_big = sum(int(np.prod(p.shape)) for p in parts)
    R16 = _rows_for(n_big, 32)
    gathered = _gather_shards(_pack(parts, R16, BF16), "gather_weights")
    shard_shapes16 = [p.shape for p in parts]
    per_chip = [_unpack(gathered[k], shard_shapes16) for k in range(4)]
    full = {}
    for idx, n in enumerate(BIG[:-1]):
        ax = 1 if n in ("w_o", "w_down") else 2
        full[n] = jnp.concatenate([per_chip[k][idx] for k in range(4)], axis=ax)
    conv_full = jnp.concatenate([lax.bitcast_convert_type(per_chip[k][-1], F32) for k in range(4)], axis=2)

    win_p = _pad_w_in(full["w_in"])
    wuq_p = _pad_heads(full["w_uq"], UQ_SRC, ZERO_OFF)
    wk_p = _pad_heads(full["w_ukv"], KN_SRC, ZERO_OFF)
    wv_p = _pad_heads(full["w_ukv"], V_SRC, V_OFF)
    wo_p = jnp.concatenate([full["w_o"][:, 0:768], _swap_mid(full["w_o"][:, 768:1024], 1)], axis=1)
    gmix_p = jnp.concatenate([mix_norm[:, 0:768], _swap_mid(mix_norm[:, 768:1024], 1)], axis=1)
    conv8 = jnp.pad(conv_full, ((0, 0), (0, 5), (0, 0)))
    pwd = jnp.concatenate([jnp.concatenate(
        [jnp.pad(pool_w[:, 2 * b], ((0, 0), (0, 0), (0, 64))), jnp.pad(pool_w[:, 2 * b + 1], ((0, 0), (0, 0), (64, 0)))],
        axis=1) for b in range(2)], axis=1).astype(BF16)
    tabs = _rope_tables(S)

    xs = [x[0]]
    saved = []
    for l in range(DEPTH):
        x0 = xs[-1]
        proj, h = _norm_mm(x0, attn_norm[l:l + 1], win_p[l], f"in_proj{l}", 1024)
        q, k, v = _mla_prep(proj, mla_q_norm[l:l + 1], mla_kv_norm[l:l + 1], wuq_p[l], wk_p[l], wv_p[l], tabs, f"mla_prep{l}")
        oh, lse = _mla_attn(q, k, v, f"mla_attn{l}")
        yb = _conv(proj, conv8[l], f"conv{l}")
        ycp = _pool(proj, pwd[l], pool_scale[l:l + 1], f"pool{l}")
        yd = _swa(proj, swa_sinks[l], f"swa{l}")
        x1, ycat, mixed = _mix_out(x0, oh, yb, ycp, yd, gmix_p[l:l + 1], wo_p[l], f"mix_out{l}")
        gu, h2 = _norm_mm(x1, ffn_norm[l:l + 1], full["w_gate_up"][l], f"gate_up{l}", 1408)
        x2, act = _swiglu_mm_res(x1, gu, full["w_down"][l], f"down{l}")
        saved.append(dict(x0=x0, proj=proj, h=h, q=q, k=k, v=v, oh=oh, lse=lse, x1=x1, ycat=ycat, mixed=mixed,
                          gu=gu, h2=h2, act=act))
        xs.append(x2)

    dx, dg_final, loss_tile = _loss_head(xs[-1], final_norm.reshape(1, D), loss_target[0], "loss_head")
    loss = lax.psum(loss_tile[0, 0] * (0.5 / D), ("x", "y", "c"))

    G = {n: [None] * DEPTH for n in ORDER if n != "final_norm"}
    for l in reversed(range(DEPTH)):
        sv = saved[l]
        dgu = _bwd_down(dx, full["w_down"][l], sv["gu"], f"down_bwd{l}")
        G["w_down"][l] = _mm_tn(sv["act"], dx.astype(BF16), f"dw_down{l}")
        G["w_gate_up"][l] = _mm_tn(sv["h2"], dgu, f"dw_gate_up{l}")
        dx1, dg = _mm_nt_normbwd(dgu, full["w_gate_up"][l], sv["x1"], ffn_norm[l:l + 1], dx, 1, f"gate_up_bwd{l}")
        G["ffn_norm"][l] = dg[0]
        G["w_o"][l] = _mm_tn(sv["mixed"], dx1.astype(BF16), f"dw_o{l}")
        dycat, dg = _mm_nt_normbwd(dx1, wo_p[l], sv["ycat"], gmix_p[l:l + 1], None, 4, f"mix_bwd{l}")
        G["mix_norm"][l] = jnp.concatenate([dg[0, 0:768], _swap_mid(dg[0, 768:1024], 0)])
        G["w_o"][l] = jnp.concatenate([G["w_o"][l][0:768], _swap_mid(G["w_o"][l][768:1024], 0)], axis=0)

        proj = sv["proj"]
        dq, dk, dv = _mla_attn_bwd(sv["q"], sv["k"], sv["v"], dycat, sv["oh"], sv["lse"], f"mla_attn_bwd{l}")
        dcq, dckv, dkr, dwuq, dwk, dwv, dgq, dgkv = _mla_prep_bwd(
            dq, dk, dv, proj, mla_q_norm[l:l + 1], mla_kv_norm[l:l + 1], wuq_p[l], wk_p[l], wv_p[l], tabs, f"mla_prep_bwd{l}")
        dgb, dgc, duc, dcw = _conv_bwd(proj, conv8[l], dycat, f"conv_bwd{l}")
        dup, dpw, dps = _pool_bwd(proj, pwd[l], pool_scale[l:l + 1], dycat, f"pool_bwd{l}")
        dqs, dks, dvs, dsink = _swa_bwd(proj, swa_sinks[l], dycat, f"swa_bwd{l}")
        dproj = jnp.concatenate([dcq, dckv, dkr, dgb, dgc, duc, dup, dqs, dks, dvs], axis=1)
        G["w_in"][l] = _unpad_w_in(_mm_tn(sv["h"], dproj, f"dw_in{l}"))
        dx, dg = _mm_nt_normbwd(dproj, win_p[l], sv["x0"], attn_norm[l:l + 1], dx1, 1, f"in_proj_bwd{l}")
        G["attn_norm"][l] = dg[0]
        G["mla_q_norm"][l] = dgq[0]
        G["mla_kv_norm"][l] = dgkv[0]
        G["w_uq"][l] = jnp.concatenate(_unpad_heads(dwuq, UQ_SRC, ZERO_OFF), axis=1)
        kn, vv = _unpad_heads(dwk, KN_SRC, ZERO_OFF), _unpad_heads(dwv, V_SRC, V_OFF)
        G["w_ukv"][l] = jnp.concatenate([t for h in range(HEADS) for t in (kn[h], vv[h])], axis=1)
        G["conv_w"][l] = dcw[0:3]
        G["pool_w"][l] = jnp.stack([dpw[0:64, 0:64], dpw[64:128, 64:128], dpw[128:192, 0:64], dpw[192:256, 64:128]])
        G["pool_scale"][l] = dps[0]
        G["swa_sinks"][l] = dsink[0, 0:4]
    grad_x = dx[None]
    Gl = {n: jnp.stack(G[n]) for n in G}
    Gl["final_norm"] = dg_final[0]

    def shard_of(n, k):
        g = Gl[n]
        if n in ("w_o", "w_down"):
            w_ = g.shape[1] // 4
            return g[:, k * w_:(k + 1) * w_]
        w_ = g.shape[2] // 4
        return g[:, :, k * w_:(k + 1) * w_]

    n32 = sum(int(np.prod(W[n].shape)) for n in BIG)
    R32 = _rows_for(n32, 32)
    g4 = jnp.stack([_pack([shard_of(n, k) for n in BIG], R32, F32) for k in range(4)]).reshape(4, 2, R32 // 2, LANES)
    got = _swap_half(g4, "rs_swap_cores")
    t32, t16 = _pair_sum(g4, got, c_arr, "rs_pair_sum")
    got3 = _exchange_chips(t16, "rs_exchange_chips")
    u = _chip_sum(t32, got3, k_arr, "rs_chip_sum")
    gsum = _join_halves(u, "rs_join_cores").reshape(R32, LANES)

    d_b, m_b, v_b = _adamw(_pack([W[n] for n in BIG], R32, F32), gsum, _pack([M1[n] for n in BIG], R32, F32),
                           _pack([V2[n] for n in BIG], R32, F32), "adamw_sharded")
    res = {}
    for key, buf in (("g", gsum), ("d", d_b), ("m", m_b), ("v", v_b)):
        for n, a in zip(BIG, _unpack(buf, big_shapes)):
            res[key, n] = a

    small_shapes = [W[n].shape for n in SMALL]
    ns = sum(int(np.prod(s)) for s in small_shapes)
    Rs = _rows_for(ns, 8)
    gs = _allsum_small(_pack([Gl[n] for n in SMALL], Rs, F32), "allsum_replicated")
    d_s, m_s, v_s = _adamw(_pack([W[n] for n in SMALL], Rs, F32), gs, _pack([M1[n] for n in SMALL], Rs, F32),
                           _pack([V2[n] for n in SMALL], Rs, F32), "adamw_replicated")
    for key, buf in (("g", gs), ("d", d_s), ("m", m_s), ("v", v_s)):
        for n, a in zip(SMALL, _unpack(buf, small_shapes)):
            res[key, n] = a

    return (loss, grad_x, *[res["g", n] for n in ORDER], *[res["d", n] for n in ORDER],
            *[res["m", n] for n in ORDER], *[res["v", n] for n in ORDER])
```

```python
import math

import numpy as np
import jax
import jax.numpy as jnp
from jax import lax
from jax.experimental import pallas as pl
from jax.experimental.pallas import tpu as pltpu

F32, BF16 = jnp.float32, jnp.bfloat16
SDS = jax.ShapeDtypeStruct
BS = pl.BlockSpec
MESH = pl.DeviceIdType.MESH

D = 1024
DEPTH = 2
HEADS = 4
D_FF = 2816
D_INP = 2048
EPS = 1e-6
SWA_WINDOW = 128
BLK = 128
SLOPES = tuple(2.0 ** (-8.0 * (i + 1) / 4) for i in range(4))
QK_SCALE = 1.0 / math.sqrt(96)
SWA_SCALE = 1.0 / math.sqrt(64)
LR, B1, B2, ADAM_EPS, WD, STEP = 0.001, 0.9, 0.999, 1e-08, 0.01, 10

LANES = 1024
VMEM_LIMIT = 48 * 1024 * 1024
NEG_INF = float("-inf")

C_CQ, C_CKV, C_KR, C_GB, C_GC, C_UC, C_UP, C_QS, C_KS, C_VS = 0, 256, 384, 512, 768, 1024, 1280, 1536, 1792, 1920


def _params(ngrid):
    return pltpu.CompilerParams(dimension_semantics=("arbitrary",) * ngrid, vmem_limit_bytes=VMEM_LIMIT)


def _pc(body, *, name, grid, in_specs, out_specs, out_shape, scratch=(), aliases=None):
    return pl.pallas_call(
        body, name=name, grid=grid, in_specs=in_specs, out_specs=out_specs, out_shape=out_shape,
        scratch_shapes=scratch, input_output_aliases=aliases or {}, compiler_params=_params(len(grid)))


def _dot(a, b):
    return jnp.dot(a, b, preferred_element_type=F32)


def _dot_nt(a, b):
    return lax.dot_general(a, b, (((1,), (1,)), ((), ())), preferred_element_type=F32)


def _dot_tn(a, b):
    return lax.dot_general(a, b, (((0,), (0,)), ((), ())), preferred_element_type=F32)


def _tile(n, cap):
    if n <= cap:
        return n
    t = cap - cap % 128
    while n % t:
        t -= 128
    return t


def _row_tile(a, b, cap=262144):
    bp = -(-b // 128) * 128
    best = None
    for t in range(8, a + 1, 8):
        if a % t == 0 and t * bp <= cap:
            best = t
    return best if best is not None else a


def _g3(a):
    return a.reshape(a.shape[0], 1, a.shape[1])


def _norm_mm(x, g3, l, w, wspec, N, tn, name):
    S, K = x.shape
    tm = min(512, S)

    def body(x_ref, g_ref, w_ref, y_ref, h_ref):
        xv = x_ref[...]
        r = lax.rsqrt(jnp.mean(xv * xv, axis=-1, keepdims=True) + EPS)
        h = (xv * r * g_ref[...]).astype(BF16)
        h_ref[...] = h
        y_ref[...] = _dot(h, w_ref[...])

    return _pc(body, name=name, grid=(S // tm, N // tn),
               in_specs=[BS((tm, K), lambda i, j: (i, 0)), BS((None, 1, K), lambda i, j: (l, 0, 0)), wspec],
               out_specs=[BS((tm, tn), lambda i, j: (i, j)), BS((tm, K), lambda i, j: (i, 0))],
               out_shape=[SDS((S, N), F32), SDS((S, K), BF16)])(x, g3, w)


def _wspec_in(l):
    return BS((None, D, 1024), lambda i, j: (l, 0, j))


def _wspec_gu(l):
    return BS((None, None, D, 2 * D_FF // 4), lambda i, j: (l, j, 0, 0))


def _mix_out(x0, oh, yb, yc, yd, gmix3, wo, l, name):
    S = x0.shape[0]
    tm = min(512, S)

    def body(x_ref, oh_ref, yb_ref, yc_ref, yd_ref, g_ref, w_ref, x1_ref, ycat_ref, mixed_ref):
        ya0 = oh_ref[:, 0:128] + oh_ref[:, 128:256]
        ya1 = oh_ref[:, 256:384] + oh_ref[:, 384:512]
        groups = [jnp.concatenate([ya0, ya1], axis=1), yb_ref[...], yc_ref[...], yd_ref[...]]
        for gi, yg in enumerate(groups):
            sl = slice(gi * 256, (gi + 1) * 256)
            r = lax.rsqrt(jnp.mean(yg * yg, axis=-1, keepdims=True) + EPS)
            ycat_ref[:, sl] = yg
            mixed_ref[:, sl] = (yg * r * g_ref[:, sl]).astype(BF16)
        x1_ref[...] = x_ref[...] + _dot(mixed_ref[...], w_ref[...])

    row = lambda w: BS((tm, w), lambda i: (i, 0))
    return _pc(body, name=name, grid=(S // tm,),
               in_specs=[row(D), row(512), row(256), row(256), row(256), BS((None, 1, D), lambda i: (l, 0, 0)),
                         BS((None, D, D), lambda i: (l, 0, 0))],
               out_specs=[row(D), row(D), row(D)],
               out_shape=[SDS((S, D), F32), SDS((S, D), F32), SDS((S, D), BF16)])(x0, oh, yb, yc, yd, gmix3, wo)


def _swiglu_mm_res(x1, gu, wdown, l, name):
    S = x1.shape[0]
    tm = min(256, S)

    def body(x_ref, gate_ref, up_ref, w_ref, x2_ref, act_ref):
        gt = gate_ref[...]
        act = (gt / (1.0 + jnp.exp(-gt)) * up_ref[...]).astype(BF16)
        act_ref[...] = act
        x2_ref[...] = x_ref[...] + _dot(act, w_ref[...])

    return _pc(body, name=name, grid=(S // tm,),
               in_specs=[BS((tm, D), lambda i: (i, 0)), BS((tm, D_FF), lambda i: (i, 0)),
                         BS((tm, D_FF), lambda i: (i, 1)), BS((None, D_FF, D), lambda i: (l, 0, 0))],
               out_specs=[BS((tm, D), lambda i: (i, 0)), BS((tm, D_FF), lambda i: (i, 0))],
               out_shape=[SDS((S, D), F32), SDS((S, D_FF), BF16)])(x1, gu, gu, wdown)


def _loss_head(x, g, tgt, name):
    S = x.shape[0]
    tm = min(512, S)

    def body(x_ref, g_ref, t_ref, dx_ref, dx16_ref, dg_ref, loss_ref):
        @pl.when(pl.program_id(0) == 0)
        def _():
            dg_ref[...] = jnp.zeros_like(dg_ref)
            loss_ref[...] = jnp.zeros_like(loss_ref)

        xv = x_ref[...]
        r = lax.rsqrt(jnp.mean(xv * xv, axis=-1, keepdims=True) + EPS)
        xh = xv * r
        gv = g_ref[...]
        diff = xh * gv - t_ref[...]
        loss_ref[...] += jnp.sum(diff * diff)
        dy = diff * (1.0 / D)
        dg_ref[...] += jnp.sum(dy * xh, axis=0, keepdims=True)
        dxh = dy * gv
        dx = r * (dxh - xh * jnp.mean(dxh * xh, axis=-1, keepdims=True))
        dx_ref[...] = dx
        dx16_ref[...] = dx.astype(BF16)

    row = BS((tm, D), lambda i: (i, 0))
    return _pc(body, name=name, grid=(S // tm,),
               in_specs=[row, BS((1, D), lambda i: (0, 0)), row],
               out_specs=[row, row, BS((8, D), lambda i: (0, 0)), BS((8, 128), lambda i: (0, 0))],
               out_shape=[SDS((S, D), F32), SDS((S, D), BF16), SDS((8, D), F32), SDS((8, 128), F32)])(x, g, tgt)


def _mm_tn(a, b, l, prev, name, split4=False):
    S, Ka = a.shape
    N = b.shape[1]
    if split4:
        ta, tn = _tile(Ka, 256), N // 4
        out_shape = SDS((2, 4, Ka, tn), F32)
        out_spec = BS((None, None, ta, tn), lambda i, j: (l, j, i, 0))
    else:
        ta, tn = _tile(Ka, 512), _tile(N, 512)
        out_shape = SDS((2, Ka, N), F32)
        out_spec = BS((None, ta, tn), lambda i, j: (l, i, j))

    def body(a_ref, b_ref, *rest):
        rest[-1][...] = _dot_tn(a_ref[...], b_ref[...])

    in_specs = [BS((S, ta), lambda i, j: (0, i)), BS((S, tn), lambda i, j: (0, j))]
    args = [a, b]
    if prev is not None:
        in_specs.append(BS(memory_space=pl.ANY))
        args.append(prev)
    return _pc(body, name=name, grid=(Ka // ta, N // tn), in_specs=in_specs, out_specs=out_spec, out_shape=out_shape,
               aliases={2: 0} if prev is not None else None)(*args)


def _bwd_down(dx16, wdown, gu, l, name):
    S = dx16.shape[0]
    tm = min(256, S)

    def body(dx_ref, w_ref, gate_ref, up_ref, dgu_ref):
        dact = _dot_nt(dx_ref[...], w_ref[...])
        gt = gate_ref[...]
        sg = 1.0 / (1.0 + jnp.exp(-gt))
        dgu_ref[:, 0:D_FF] = (dact * up_ref[...] * (sg * (1.0 + gt * (1.0 - sg)))).astype(BF16)
        dgu_ref[:, D_FF:2 * D_FF] = (dact * (gt * sg)).astype(BF16)

    return _pc(body, name=name, grid=(S // tm,),
               in_specs=[BS((tm, D), lambda i: (i, 0)), BS((None, D_FF, D), lambda i: (l, 0, 0)),
                         BS((tm, D_FF), lambda i: (i, 0)), BS((tm, D_FF), lambda i: (i, 1))],
               out_specs=BS((tm, 2 * D_FF), lambda i: (i, 0)),
               out_shape=SDS((S, 2 * D_FF), BF16))(dx16, wdown, gu, gu)


def _mm_nt_normbwd(dy, w4, l, x, g3, dres, ngroups, name):
    S, K = dy.shape
    _, nk, _, kc = w4.shape
    tm = min(256, S)
    gw = D // ngroups
    has_res = dres is not None

    def body(*refs):
        if has_res:
            dy_ref, w_ref, x_ref, g_ref, res_ref, dx_ref, dx16_ref, dg_ref = refs
        else:
            dy_ref, w_ref, x_ref, g_ref, dx_ref, dg_ref = refs

        @pl.when(pl.program_id(0) == 0)
        def _():
            dg_ref[...] = jnp.zeros_like(dg_ref)

        dh = _dot_nt(dy_ref[:, 0:kc], w_ref[0])
        for k in range(1, nk):
            dh = dh + _dot_nt(dy_ref[:, k * kc:(k + 1) * kc], w_ref[k])
        for gi in range(ngroups):
            sl = slice(gi * gw, (gi + 1) * gw)
            xg = x_ref[:, sl]
            r = lax.rsqrt(jnp.mean(xg * xg, axis=-1, keepdims=True) + EPS)
            xh = xg * r
            dhg = dh[:, sl]
            dg_ref[:, sl] += jnp.sum(dhg * xh, axis=0, keepdims=True)
            dxh = dhg * g_ref[:, sl]
            dxg = r * (dxh - xh * jnp.mean(dxh * xh, axis=-1, keepdims=True))
            if has_res:
                dxg = dxg + res_ref[:, sl]
                dx16_ref[:, sl] = dxg.astype(BF16)
            dx_ref[:, sl] = dxg

    row = BS((tm, D), lambda i: (i, 0))
    in_specs = [BS((tm, K), lambda i: (i, 0)), BS((None, nk, D, kc), lambda i: (l, 0, 0, 0)), row,
                BS((None, 1, D), lambda i: (l, 0, 0))]
    args = [dy, w4, x, g3]
    out_specs, out_shape = [row], [SDS((S, D), F32)]
    if has_res:
        in_specs.append(row)
        args.append(dres)
        out_specs.append(row)
        out_shape.append(SDS((S, D), BF16))
    out_specs.append(BS((8, D), lambda i: (0, 0)))
    out_shape.append(SDS((8, D), F32))
    return _pc(body, name=name, grid=(S // tm,), in_specs=in_specs, out_specs=out_specs, out_shape=out_shape)(*args)


def _rope(x, c, s1, s2):
    return x * c + pltpu.roll(x, 112, axis=1) * s1 + pltpu.roll(x, 16, axis=1) * s2


def _rope_t(dy, c, s1, s2):
    return dy * c + pltpu.roll(dy * s1, 16, axis=1) + pltpu.roll(dy * s2, 112, axis=1)


def _mla_prep(proj, gq3, gkv3, wuq, wk, wv, tabs, l, name):
    S = proj.shape[0]
    tm = min(512, S)
    tc, ts1, ts2 = tabs

    def body(cq_ref, ckv_ref, kr_ref, gq_ref, gkv_ref, wuq_ref, wk_ref, wv_ref, c_ref, s1_ref, s2_ref,
             q_ref, k_ref, v_ref):
        c, s1, s2 = c_ref[...], s1_ref[...], s2_ref[...]
        cq = cq_ref[...]
        rq = lax.rsqrt(jnp.mean(cq * cq, axis=-1, keepdims=True) + EPS)
        qa = _dot((cq * rq * gq_ref[...]).astype(BF16), wuq_ref[...])
        ckv = ckv_ref[...]
        rkv = lax.rsqrt(jnp.mean(ckv * ckv, axis=-1, keepdims=True) + EPS)
        ckvn = (ckv * rkv * gkv_ref[...]).astype(BF16)
        ka = _dot(ckvn, wk_ref[...])
        v_ref[...] = _dot(ckvn, wv_ref[...]).astype(BF16)
        krr = _rope(kr_ref[...], c, s1, s2)
        for h in range(HEADS):
            sl = slice(h * 128, (h + 1) * 128)
            q_ref[:, sl] = (_rope(qa[:, sl], c, s1, s2) * QK_SCALE).astype(BF16)
            k_ref[:, sl] = (ka[:, sl] + krr).astype(BF16)

    lay = lambda a, b: BS((None, a, b), lambda i: (l, 0, 0))
    tab = BS((tm, 128), lambda i: (i, 0))
    return _pc(body, name=name, grid=(S // tm,),
               in_specs=[BS((tm, 256), lambda i: (i, 0)), BS((tm, 128), lambda i: (i, 2)), BS((tm, 128), lambda i: (i, 3)),
                         lay(1, 256), lay(1, 128), lay(256, 512), lay(128, 512), lay(128, 512), tab, tab, tab],
               out_specs=[BS((tm, 512), lambda i: (i, 0))] * 3,
               out_shape=[SDS((S, 512), BF16)] * 3)(proj, proj, proj, gq3, gkv3, wuq, wk, wv, tc, ts1, ts2)


def _mla_prep_bwd(dq, dk, dv, proj, gq3, gkv3, wuq, wk, wv, tabs, l, name):
    S = proj.shape[0]
    tm = min(512, S)
    tc, ts1, ts2 = tabs

    def body(dq_ref, dk_ref, dv_ref, cq_ref, ckv_ref, gq_ref, gkv_ref, wuq_ref, wk_ref, wv_ref, c_ref, s1_ref, s2_ref,
             dcq_ref, dckv_ref, dkr_ref, dwuq_ref, dwk_ref, dwv_ref, dgq_ref, dgkv_ref):
        @pl.when(pl.program_id(0) == 0)
        def _():
            for r in (dwuq_ref, dwk_ref, dwv_ref, dgq_ref, dgkv_ref):
                r[...] = jnp.zeros_like(r)

        c, s1, s2 = c_ref[...], s1_ref[...], s2_ref[...]
        dqp = jnp.concatenate(
            [_rope_t(dq_ref[:, h * 128:(h + 1) * 128] * QK_SCALE, c, s1, s2) for h in range(HEADS)], axis=1).astype(BF16)
        cq = cq_ref[...]
        rq = lax.rsqrt(jnp.mean(cq * cq, axis=-1, keepdims=True) + EPS)
        cqh = cq * rq
        gq_v = gq_ref[...]
        dwuq_ref[...] += _dot_tn((cqh * gq_v).astype(BF16), dqp)
        dcqn = _dot_nt(dqp, wuq_ref[...])
        dgq_ref[...] += jnp.sum(dcqn * cqh, axis=0, keepdims=True)
        dxh = dcqn * gq_v
        dcq_ref[...] = (rq * (dxh - cqh * jnp.mean(dxh * cqh, axis=-1, keepdims=True))).astype(BF16)

        dkb = dk_ref[...].astype(BF16)
        dvb = dv_ref[...].astype(BF16)
        ckv = ckv_ref[...]
        rkv = lax.rsqrt(jnp.mean(ckv * ckv, axis=-1, keepdims=True) + EPS)
        ckh = ckv * rkv
        gkv_v = gkv_ref[...]
        ckvn = (ckh * gkv_v).astype(BF16)
        dwk_ref[...] += _dot_tn(ckvn, dkb)
        dwv_ref[...] += _dot_tn(ckvn, dvb)
        dckvn = _dot_nt(dkb, wk_ref[...]) + _dot_nt(dvb, wv_ref[...])
        dgkv_ref[...] += jnp.sum(dckvn * ckh, axis=0, keepdims=True)
        dyh = dckvn * gkv_v
        dckv_ref[...] = (rkv * (dyh - ckh * jnp.mean(dyh * ckh, axis=-1, keepdims=True))).astype(BF16)
        dks = dk_ref[:, 0:128] + dk_ref[:, 128:256] + dk_ref[:, 256:384] + dk_ref[:, 384:512]
        dkr_ref[...] = _rope_t(dks, c, s1, s2).astype(BF16)

    full = lambda a, b: BS((a, b), lambda i: (0, 0))
    lay = lambda a, b: BS((None, a, b), lambda i: (l, 0, 0))
    tab = BS((tm, 128), lambda i: (i, 0))
    row = lambda w: BS((tm, w), lambda i: (i, 0))
    return _pc(body, name=name, grid=(S // tm,),
               in_specs=[row(512), row(512), row(512), BS((tm, 256), lambda i: (i, 0)), BS((tm, 128), lambda i: (i, 2)),
                         lay(1, 256), lay(1, 128), lay(256, 512), lay(128, 512), lay(128, 512), tab, tab, tab],
               out_specs=[row(256), row(128), row(128), full(256, 512), full(128, 512), full(128, 512),
                          full(8, 256), full(8, 128)],
               out_shape=[SDS((S, 256), BF16), SDS((S, 128), BF16), SDS((S, 128), BF16), SDS((256, 512), F32),
                          SDS((128, 512), F32), SDS((128, 512), F32), SDS((8, 256), F32), SDS((8, 128), F32)])(
        dq, dk, dv, proj, proj, gq3, gkv3, wuq, wk, wv, tc, ts1, ts2)


def _mla_attn(q, k, v, name):
    S = q.shape[0]
    t = min(512, S)
    n = S // t

    def body(q_ref, k_ref, v_ref, o_ref, lse_ref, m_sc, l_sc, acc_sc):
        i, j = pl.program_id(1), pl.program_id(2)

        @pl.when(j == 0)
        def _():
            m_sc[...] = jnp.full_like(m_sc, NEG_INF)
            l_sc[...] = jnp.zeros_like(l_sc)
            acc_sc[...] = jnp.zeros_like(acc_sc)

        @pl.when(j <= i)
        def _():
            s = _dot_nt(q_ref[...], k_ref[...])
            row = lax.broadcasted_iota(jnp.int32, (t, t), 0) + i * t
            col = lax.broadcasted_iota(jnp.int32, (t, t), 1) + j * t
            s = jnp.where(col <= row, s, NEG_INF)
            m_prev = m_sc[...]
            m_new = jnp.maximum(m_prev, jnp.max(s, axis=-1, keepdims=True))
            p = jnp.exp(s - m_new)
            alpha = jnp.exp(m_prev - m_new)
            l_sc[...] = alpha * l_sc[...] + jnp.sum(p, axis=-1, keepdims=True)
            acc_sc[...] = alpha * acc_sc[...] + _dot(p.astype(BF16), v_ref[...])
            m_sc[...] = m_new

        @pl.when(j == i)
        def _():
            o_ref[...] = acc_sc[...] / l_sc[...]
            lse_ref[...] = jnp.broadcast_to(m_sc[...] + jnp.log(l_sc[...]), (t, 128))

    qs = BS((t, 128), lambda h, i, j: (i, h))
    ks = BS((t, 128), lambda h, i, j: (jnp.minimum(j, i), h))
    return _pc(body, name=name, grid=(HEADS, n, n), in_specs=[qs, ks, ks], out_specs=[qs, qs],
               out_shape=[SDS((S, 512), F32), SDS((S, 512), F32)],
               scratch=[pltpu.VMEM((t, 1), F32), pltpu.VMEM((t, 1), F32), pltpu.VMEM((t, 128), F32)])(q, k, v)


def _mla_attn_bwd(q, k, v, dya, oh, lse, name):
    S = q.shape[0]
    t = min(512, S)
    n = S // t

    def body(q_ref, k_ref, v_ref, do_ref, oh_ref, lse_ref, dq_ref, dk_ref, dv_ref):
        j, i = pl.program_id(1), pl.program_id(2)

        @pl.when((j == 0) & (i == 0))
        def _():
            dq_ref[...] = jnp.zeros_like(dq_ref)

        @pl.when(i == 0)
        def _():
            dk_ref[...] = jnp.zeros_like(dk_ref)
            dv_ref[...] = jnp.zeros_like(dv_ref)

        @pl.when(i >= j)
        def _():
            qv, kv = q_ref[...], k_ref[...]
            s = _dot_nt(qv, kv)
            row = lax.broadcasted_iota(jnp.int32, (t, t), 0) + i * t
            col = lax.broadcasted_iota(jnp.int32, (t, t), 1) + j * t
            p = jnp.where(col <= row, jnp.exp(s - lse_ref[:, 0:1]), 0.0)
            do = do_ref[...]
            delta = jnp.sum(do * oh_ref[...], axis=-1, keepdims=True)
            dob = do.astype(BF16)
            dv_ref[...] += _dot_tn(p.astype(BF16), dob)
            dp = _dot_nt(dob, v_ref[...])
            ds = (p * (dp - delta)).astype(BF16)
            rows = pl.ds(pl.multiple_of(i * t, t), t)
            dq_ref[rows, :] += _dot(ds, kv)
            dk_ref[...] += _dot_tn(ds, qv)

    qs = BS((t, 128), lambda h, j, i: (jnp.maximum(i, j), h))
    ks = BS((t, 128), lambda h, j, i: (j, h))
    dos = BS((t, 128), lambda h, j, i: (jnp.maximum(i, j), h // 2))
    return _pc(body, name=name, grid=(HEADS, n, n), in_specs=[qs, ks, ks, dos, qs, qs],
               out_specs=[BS((S, 128), lambda h, j, i: (0, h)), ks, ks],
               out_shape=[SDS((S, 512), F32)] * 3)(q, k, v, dya, oh, lse)


def _swa_scores(qm, kk, valid, distf, slope, sink):
    sc = _dot_nt(qm, kk) * SWA_SCALE
    sc = jnp.where(valid, sc - slope * distf, NEG_INF)
    m = jnp.maximum(jnp.max(sc, axis=-1, keepdims=True), sink)
    e = jnp.exp(sc - m)
    esink = jnp.exp(sink - m)
    den = jnp.sum(e, axis=-1, keepdims=True) + esink
    return e / den, esink / den


def _swa_masks():
    r = lax.broadcasted_iota(jnp.int32, (BLK, 2 * BLK), 0)
    c = lax.broadcasted_iota(jnp.int32, (BLK, 2 * BLK), 1)
    dist = r + BLK - c
    return (dist >= 0) & (dist < SWA_WINDOW), c >= BLK, dist.astype(F32)


def _to_half(xb, pos, b):
    return xb if pos == b else pltpu.roll(xb, 64, axis=1)


def _swa(proj, sinks, l, name):
    S = proj.shape[0]
    nb = S // BLK

    def body(q_ref, k_ref, v_ref, sink_ref, o_ref, kp, vp):
        kp[0:BLK, :] = jnp.zeros((BLK, 128), BF16)
        vp[0:BLK, :] = jnp.zeros((BLK, 128), BF16)
        kp[BLK:, :] = k_ref[...].astype(BF16)
        vp[BLK:, :] = v_ref[...].astype(BF16)
        lo = lax.broadcasted_iota(jnp.int32, (BLK, 128), 1) < 64
        band, cur, distf = _swa_masks()

        def blk(i, carry):
            st = pl.multiple_of(i * BLK, BLK)
            kk = kp[pl.ds(st, 2 * BLK), :]
            vv = vp[pl.ds(st, 2 * BLK), :]
            valid = band & (cur | (i > 0))
            for b in range(2):
                half = lo if b == 0 else ~lo
                qb = q_ref[pl.ds(st, BLK), b * 128:(b + 1) * 128]
                outs = []
                for pos in range(2):
                    h = 2 * b + pos
                    qm = jnp.where(half, _to_half(qb, pos, b), 0.0).astype(BF16)
                    p, _ = _swa_scores(qm, kk, valid, distf, SLOPES[h], sink_ref[l, h])
                    outs.append(_to_half(_dot(p.astype(BF16), vv), pos, b))
                o_ref[pl.ds(st, BLK), b * 128:(b + 1) * 128] = jnp.where(lo, outs[0], outs[1])
            return carry

        lax.fori_loop(0, nb, blk, 0)

    return _pc(body, name=name, grid=(1,),
               in_specs=[BS((S, 256), lambda i: (0, C_QS // 256)), BS((S, 128), lambda i: (0, C_KS // 128)),
                         BS((S, 128), lambda i: (0, C_VS // 128)), BS(memory_space=pltpu.SMEM)],
               out_specs=BS((S, 256), lambda i: (0, 0)),
               out_shape=SDS((S, 256), F32),
               scratch=[pltpu.VMEM((S + BLK, 128), BF16), pltpu.VMEM((S + BLK, 128), BF16)])(proj, proj, proj, sinks)


def _swa_bwd(proj, sinks, dyd, l, name):
    S = proj.shape[0]
    nb = S // BLK

    def body(q_ref, k_ref, v_ref, sink_ref, do_ref, dq_ref, dk_ref, dv_ref, dsink_ref, kp, vp, dkp, dvp):
        kp[0:BLK, :] = jnp.zeros((BLK, 128), BF16)
        vp[0:BLK, :] = jnp.zeros((BLK, 128), BF16)
        kp[BLK:, :] = k_ref[...].astype(BF16)
        vp[BLK:, :] = v_ref[...].astype(BF16)
        dkp[...] = jnp.zeros_like(dkp)
        dvp[...] = jnp.zeros_like(dvp)
        lo = lax.broadcasted_iota(jnp.int32, (BLK, 128), 1) < 64
        lane8 = lax.broadcasted_iota(jnp.int32, (8, 128), 1)
        band, cur, distf = _swa_masks()

        def blk(i, dsink):
            st = pl.multiple_of(i * BLK, BLK)
            kk = kp[pl.ds(st, 2 * BLK), :]
            vv = vp[pl.ds(st, 2 * BLK), :]
            valid = band & (cur | (i > 0))
            dkk = jnp.zeros((2 * BLK, 128), F32)
            dvv = jnp.zeros((2 * BLK, 128), F32)
            for b in range(2):
                half = lo if b == 0 else ~lo
                qb = q_ref[pl.ds(st, BLK), b * 128:(b + 1) * 128]
                dob = do_ref[pl.ds(st, BLK), b * 128:(b + 1) * 128]
                dqs = []
                for pos in range(2):
                    h = 2 * b + pos
                    qm = jnp.where(half, _to_half(qb, pos, b), 0.0).astype(BF16)
                    dom = jnp.where(half, _to_half(dob, pos, b), 0.0).astype(BF16)
                    p, psink = _swa_scores(qm, kk, valid, distf, SLOPES[h], sink_ref[l, h])
                    dp = _dot_nt(dom, vv)
                    dvv = dvv + _dot_tn(p.astype(BF16), dom)
                    delta = jnp.sum(p * dp, axis=-1, keepdims=True)
                    dsink = dsink + jnp.where(lane8 == h, -jnp.sum(psink * delta), 0.0)
                    dsc = (p * (dp - delta) * SWA_SCALE).astype(BF16)
                    dqs.append(_to_half(_dot(dsc, kk), pos, b))
                    dkk = dkk + _dot_tn(dsc, qm)
                dq_ref[pl.ds(st, BLK), b * 128:(b + 1) * 128] = jnp.where(lo, dqs[0], dqs[1]).astype(BF16)
            dkp[pl.ds(st, 2 * BLK), :] += dkk
            dvp[pl.ds(st, 2 * BLK), :] += dvv
            return dsink

        dsink_ref[...] = lax.fori_loop(0, nb, blk, jnp.zeros((8, 128), F32))
        dk_ref[...] = dkp[BLK:, :].astype(BF16)
        dv_ref[...] = dvp[BLK:, :].astype(BF16)

    return _pc(body, name=name, grid=(1,),
               in_specs=[BS((S, 256), lambda i: (0, C_QS // 256)), BS((S, 128), lambda i: (0, C_KS // 128)),
                         BS((S, 128), lambda i: (0, C_VS // 128)), BS(memory_space=pltpu.SMEM),
                         BS((S, 256), lambda i: (0, 3))],
               out_specs=[BS((S, 256), lambda i: (0, 0)), BS((S, 128), lambda i: (0, 0)), BS((S, 128), lambda i: (0, 0)),
                          BS((8, 128), lambda i: (0, 0))],
               out_shape=[SDS((S, 256), BF16), SDS((S, 128), BF16), SDS((S, 128), BF16), SDS((8, 128), F32)],
               scratch=[pltpu.VMEM((S + BLK, 128), BF16), pltpu.VMEM((S + BLK, 128), BF16),
                        pltpu.VMEM((S + BLK, 128), F32), pltpu.VMEM((S + BLK, 128), F32)])(proj, proj, proj, sinks, dyd)


def _down(x, k, t):
    return jnp.where(t >= k, pltpu.roll(x, k, axis=0), 0.0)


def _up(x, k, t):
    n = x.shape[0]
    return jnp.where(t < n - k, pltpu.roll(x, n - k, axis=0), 0.0)


def _conv(proj, w8, l, name):
    S = proj.shape[0]

    def body(gb_ref, gc_ref, u_ref, w_ref, y_ref):
        t = lax.broadcasted_iota(jnp.int32, (S, 128), 0)
        z = gc_ref[...] * u_ref[...]
        c = w_ref[2:3, :] * z + w_ref[1:2, :] * _down(z, 1, t) + w_ref[0:1, :] * _down(z, 2, t)
        y_ref[...] = gb_ref[...] * c

    col = lambda c0: BS((S, 128), lambda i: (0, c0 // 128 + i))
    return _pc(body, name=name, grid=(2,),
               in_specs=[col(C_GB), col(C_GC), col(C_UC), BS((None, 8, 128), lambda i: (l, 0, i))],
               out_specs=BS((S, 128), lambda i: (0, i)), out_shape=SDS((S, 256), F32))(proj, proj, proj, w8)


def _conv_bwd(proj, w8, dycat, l, name):
    S = proj.shape[0]

    def body(gb_ref, gc_ref, u_ref, w_ref, dy_ref, dgb_ref, dgc_ref, du_ref, dw_ref):
        t = lax.broadcasted_iota(jnp.int32, (S, 128), 0)
        gc, u = gc_ref[...], u_ref[...]
        z = gc * u
        z1, z2 = _down(z, 1, t), _down(z, 2, t)
        w0, w1, w2 = w_ref[0:1, :], w_ref[1:2, :], w_ref[2:3, :]
        dy = dy_ref[...]
        dgb_ref[...] = (dy * (w2 * z + w1 * z1 + w0 * z2)).astype(BF16)
        dc = dy * gb_ref[...]
        dz = w2 * dc + w1 * _up(dc, 1, t) + w0 * _up(dc, 2, t)
        dgc_ref[...] = (dz * u).astype(BF16)
        du_ref[...] = (dz * gc).astype(BF16)
        row = lax.broadcasted_iota(jnp.int32, (8, 128), 0)
        sums = [jnp.sum(dc * zz, axis=0, keepdims=True) for zz in (z2, z1, z)]
        dw_ref[...] = jnp.where(row == 0, sums[0], jnp.where(row == 1, sums[1], jnp.where(row == 2, sums[2], 0.0)))

    col = lambda c0: BS((S, 128), lambda i: (0, c0 // 128 + i))
    out = BS((S, 128), lambda i: (0, i))
    return _pc(body, name=name, grid=(2,),
               in_specs=[col(C_GB), col(C_GC), col(C_UC), BS((None, 8, 128), lambda i: (l, 0, i)), col(256)],
               out_specs=[out, out, out, BS((8, 128), lambda i: (0, i))],
               out_shape=[SDS((S, 256), BF16)] * 3 + [SDS((8, 256), F32)])(proj, proj, proj, w8, dycat)


def _pool_parts(u, t, first):
    lo = lax.broadcasted_iota(jnp.int32, u.shape, 1) < 64
    s2 = u + _down(u, 1, t)
    s4 = s2 + _down(s2, 2, t)
    s8 = s4 + _down(s4, 4, t)
    s16 = s8 + _down(s8, 8, t)
    win = jnp.where(lo, jnp.where(first, s2, s8), jnp.where(first, s4, s16))
    wv = jnp.where(lo, jnp.where(first, 2, 8), jnp.where(first, 4, 16))
    cnt = jnp.minimum(t + 1, wv).astype(F32)
    return win, cnt, lo


def _pool(proj, pwd, scale3, l, name):
    S = proj.shape[0]

    def body(u_ref, pw_ref, sc_ref, y_ref):
        t = lax.broadcasted_iota(jnp.int32, (S, 128), 0)
        u = u_ref[...]
        win, cnt, _ = _pool_parts(u, t, pl.program_id(0) == 0)
        pooled = win / cnt - u
        y_ref[...] = _dot(pooled.astype(BF16), pw_ref[...]) * sc_ref[...]

    return _pc(body, name=name, grid=(2,),
               in_specs=[BS((S, 128), lambda i: (0, C_UP // 128 + i)), BS((None, 128, 128), lambda i: (l, i, 0)),
                         BS((None, 1, 128), lambda i: (l, 0, i))],
               out_specs=BS((S, 128), lambda i: (0, i)), out_shape=SDS((S, 256), F32))(proj, pwd, scale3)


def _pool_bwd(proj, pwd, scale3, dycat, l, name):
    S = proj.shape[0]

    def body(u_ref, pw_ref, sc_ref, dy_ref, du_ref, dpw_ref, dsc_ref):
        t = lax.broadcasted_iota(jnp.int32, (S, 128), 0)
        first = pl.program_id(0) == 0
        u = u_ref[...]
        win, cnt, lo = _pool_parts(u, t, first)
        pooled = (win / cnt - u).astype(BF16)
        pw = pw_ref[...]
        dy = dy_ref[...]
        dsc_ref[...] = jnp.broadcast_to(jnp.sum(dy * _dot(pooled, pw), axis=0, keepdims=True), (8, 128))
        dmb = (dy * sc_ref[...]).astype(BF16)
        dpw_ref[...] = _dot_tn(pooled, dmb)
        dpooled = _dot_nt(dmb, pw)
        a1 = dpooled / cnt
        a2 = a1 + _up(a1, 1, t)
        a4 = a2 + _up(a2, 2, t)
        a8 = a4 + _up(a4, 4, t)
        a16 = a8 + _up(a8, 8, t)
        dwin = jnp.where(lo, jnp.where(first, a2, a8), jnp.where(first, a4, a16))
        du_ref[...] = (dwin - dpooled).astype(BF16)

    return _pc(body, name=name, grid=(2,),
               in_specs=[BS((S, 128), lambda i: (0, C_UP // 128 + i)), BS((None, 128, 128), lambda i: (l, i, 0)),
                         BS((None, 1, 128), lambda i: (l, 0, i)), BS((S, 128), lambda i: (0, 4 + i))],
               out_specs=[BS((S, 128), lambda i: (0, i)), BS((128, 128), lambda i: (i, 0)), BS((8, 128), lambda i: (0, i))],
               out_shape=[SDS((S, 256), BF16), SDS((256, 128), F32), SDS((8, 256), F32)])(proj, pwd, scale3, dycat)


def _adamw(w, g, m, v, name):
    n, a, b = w.shape
    tr = _row_tile(a, b)

    def body(w_ref, g_ref, m_ref, v_ref, d_ref, nm_ref, nv_ref):
        gv = g_ref[...]
        m_new = B1 * m_ref[...] + (1.0 - B1) * gv
        v_new = B2 * v_ref[...] + (1.0 - B2) * (gv * gv)
        m_hat = m_new / (1.0 - B1 ** STEP)
        v_hat = v_new / (1.0 - B2 ** STEP)
        d_ref[...] = -LR * (m_hat / (jnp.sqrt(v_hat) + ADAM_EPS) + WD * w_ref[...])
        nm_ref[...] = m_new
        nv_ref[...] = v_new

    sp = BS((None, tr, b), lambda i, t: (i, t, 0))
    return _pc(body, name=name, grid=(n, a // tr), in_specs=[sp] * 4, out_specs=[sp] * 3,
               out_shape=[SDS((n, a, b), F32)] * 3)(w, g, m, v)


def _prefetch_call(body, name, grid, in_specs, out_specs, out_shape):
    gs = pltpu.PrefetchScalarGridSpec(num_scalar_prefetch=1, grid=grid, in_specs=in_specs, out_specs=out_specs)
    return pl.pallas_call(body, name=name, grid_spec=gs, out_shape=out_shape, compiler_params=_params(len(grid)))


def _place(w, kc, dtype, name):
    _, a, b = w.shape

    def body(kc_ref, w_ref, o_ref):
        o_ref[...] = w_ref[...].astype(dtype)

    return _prefetch_call(body, name, (2,), [BS((None, a, b), lambda l, kc: (l, 0, 0))],
                          BS((None, None, a, b), lambda l, kc: (l, kc[0], 0, 0)), SDS((2, 4, a, b), dtype))(kc, w)


def _pair_sum(g, got, kc, name):
    _, _, a, b = g.shape
    tr = _row_tile(a, b)

    def body(kc_ref, a_ref, b_ref, t32_ref, t16_ref):
        s = a_ref[...] + b_ref[...]
        t32_ref[...] = s
        t16_ref[...] = s.astype(BF16)

    sp = BS((None, tr, b), lambda k, t, kc: (k, t, 0))
    return _prefetch_call(body, name, (4, a // tr),
                          [BS((None, None, tr, b), lambda k, t, kc: (kc[1], k, t, 0)), sp], [sp, sp],
                          [SDS((4, a, b), F32), SDS((4, a, b), BF16)])(kc, g, got)


def _chip_sum(t32, got3, kc, name):
    _, a, b = t32.shape
    tr = _row_tile(a, b)

    def body(kc_ref, a_ref, b_ref, u_ref):
        u_ref[...] = ((a_ref[...] + b_ref[0].astype(F32)) + b_ref[1].astype(F32)) + b_ref[2].astype(F32)

    return _prefetch_call(body, name, (a // tr,),
                          [BS((None, tr, b), lambda t, kc: (kc[0], t, 0)), BS((3, tr, b), lambda t, kc: (0, t, 0))],
                          BS((None, tr, b), lambda t, kc: (kc[1], t, 0)), SDS((2, a, b), F32))(kc, t32, got3)


def _me():
    return lax.axis_index("x"), lax.axis_index("y"), lax.axis_index("c")


def _other_chips(x, y):
    return [(1 - x, y), (x, 1 - y), (1 - x, 1 - y)]


ANY = BS(memory_space=pl.ANY)
COMM_PARAMS = pltpu.CompilerParams(has_side_effects=True)


def _gather(arrs, name):
    n = len(arrs)

    def body(*refs):
        outs, send_sems, recv_sems = refs[n:2 * n], refs[2 * n], refs[2 * n + 1]
        x, y, c = _me()
        sib = (x, y, 1 - c)
        chips = _other_chips(x, y)

        def copy(t, k, chip, layer, to):
            r = outs[t].at[layer, 2 * chip[0] + chip[1]]
            return pltpu.make_async_remote_copy(src_ref=r, dst_ref=r, send_sem=send_sems.at[6 * t + k],
                                                recv_sem=recv_sems.at[6 * t + k], device_id=to, device_id_type=MESH)

        first = [copy(t, j, (x, y), c, (*chip, c)) for j, chip in enumerate(chips) for t in range(n)]
        for cp in first:
            cp.start()
        passed = []
        for j, chip in enumerate(chips):
            for t in range(n):
                copy(t, j, chip, c, (x, y, c)).wait_recv()
                passed.append(copy(t, 3 + j, chip, c, sib))
                passed[-1].start()
        for j, chip in enumerate(chips):
            for t in range(n):
                copy(t, 3 + j, chip, 1 - c, (x, y, c)).wait_recv()
        for cp in first + passed:
            cp.wait_send()

    return pl.pallas_call(body, name=name, out_shape=[SDS(a.shape, a.dtype) for a in arrs],
                          in_specs=[ANY] * n, out_specs=[ANY] * n, input_output_aliases={t: t for t in range(n)},
                          scratch_shapes=[pltpu.SemaphoreType.DMA((6 * n,)), pltpu.SemaphoreType.DMA((6 * n,))],
                          compiler_params=COMM_PARAMS)(*arrs)


def _swap_layers(gs, name):
    n = len(gs)

    def body(*refs):
        ins, outs, send_sems, recv_sems = refs[:n], refs[n:2 * n], refs[2 * n], refs[2 * n + 1]
        x, y, c = _me()
        cps = [pltpu.make_async_remote_copy(src_ref=ins[t].at[1 - c], dst_ref=outs[t], send_sem=send_sems.at[t],
                                            recv_sem=recv_sems.at[t], device_id=(x, y, 1 - c), device_id_type=MESH)
               for t in range(n)]
        for cp in cps:
            cp.start()
        for cp in cps:
            cp.wait()

    return pl.pallas_call(body, name=name, out_shape=[SDS(g.shape[1:], g.dtype) for g in gs],
                          in_specs=[ANY] * n, out_specs=[ANY] * n,
                          scratch_shapes=[pltpu.SemaphoreType.DMA((n,)), pltpu.SemaphoreType.DMA((n,))],
                          compiler_params=COMM_PARAMS)(*gs)


def _exchange_chips(ts, name):
    n = len(ts)

    def body(*refs):
        ins, outs, send_sems, recv_sems = refs[:n], refs[n:2 * n], refs[2 * n], refs[2 * n + 1]
        x, y, c = _me()
        cps = [pltpu.make_async_remote_copy(src_ref=ins[t].at[2 * cx + cy], dst_ref=outs[t].at[j],
                                            send_sem=send_sems.at[3 * t + j], recv_sem=recv_sems.at[3 * t + j],
                                            device_id=(cx, cy, c), device_id_type=MESH)
               for j, (cx, cy) in enumerate(_other_chips(x, y)) for t in range(n)]
        for cp in cps:
            cp.start()
        for cp in cps:
            cp.wait()

    return pl.pallas_call(body, name=name, out_shape=[SDS((3,) + t.shape[1:], t.dtype) for t in ts],
                          in_specs=[ANY] * n, out_specs=[ANY] * n,
                          scratch_shapes=[pltpu.SemaphoreType.DMA((3 * n,)), pltpu.SemaphoreType.DMA((3 * n,))],
                          compiler_params=COMM_PARAMS)(*ts)


def _join_layers(us, name):
    n = len(us)

    def body(*refs):
        outs, send_sems, recv_sems = refs[n:2 * n], refs[2 * n], refs[2 * n + 1]
        x, y, c = _me()
        cps = [pltpu.make_async_remote_copy(src_ref=outs[t].at[c], dst_ref=outs[t].at[c], send_sem=send_sems.at[t],
                                            recv_sem=recv_sems.at[t], device_id=(x, y, 1 - c), device_id_type=MESH)
               for t in range(n)]
        for cp in cps:
            cp.start()
        for cp in cps:
            cp.wait()

    return pl.pallas_call(body, name=name, out_shape=[SDS(u.shape, u.dtype) for u in us],
                          in_specs=[ANY] * n, out_specs=[ANY] * n, input_output_aliases={t: t for t in range(n)},
                          scratch_shapes=[pltpu.SemaphoreType.DMA((n,)), pltpu.SemaphoreType.DMA((n,))],
                          compiler_params=COMM_PARAMS)(*us)


def _allsum_small(v, name):
    M = v.shape[0]

    def body(x_ref, o_ref, all_ref, send_sems, recv_sems, local_sem):
        x, y, c = _me()
        me, sib = (x, y, c), (x, y, 1 - c)
        chips = _other_chips(x, y)

        def rows(px, py, pc):
            return all_ref.at[pl.ds((4 * px + 2 * py + pc) * M, M), :]

        def copy(k, block, to, src=None):
            return pltpu.make_async_remote_copy(src_ref=rows(*block) if src is None else src, dst_ref=rows(*block),
                                                send_sem=send_sems.at[k], recv_sem=recv_sems.at[k],
                                                device_id=to, device_id_type=MESH)

        mine = pltpu.make_async_copy(x_ref, rows(*me), local_sem)
        mine.start()
        first = [copy(0, me, sib, src=x_ref)]
        first += [copy(1 + j, me, (*chip, c), src=x_ref) for j, chip in enumerate(chips)]
        for cp in first:
            cp.start()
        passed = [copy(4 + j, (*chip, c), sib) for j, chip in enumerate(chips)]
        for j, chip in enumerate(chips):
            copy(1 + j, (*chip, c), me).wait_recv()
            passed[j].start()
        copy(0, sib, me).wait_recv()
        for j, chip in enumerate(chips):
            copy(4 + j, (*chip, 1 - c), me).wait_recv()
        for cp in first + passed:
            cp.wait_send()
        mine.wait()
        acc = all_ref[0:M, :]
        for d in range(1, 8):
            acc = acc + all_ref[d * M:(d + 1) * M, :]
        o_ref[...] = acc

    vm = BS(memory_space=pltpu.VMEM)
    return pl.pallas_call(body, name=name, out_shape=SDS((M, LANES), F32), in_specs=[vm], out_specs=vm,
                          scratch_shapes=[pltpu.VMEM((8 * M, LANES), F32), pltpu.SemaphoreType.DMA((7,)),
                                          pltpu.SemaphoreType.DMA((7,)), pltpu.SemaphoreType.DMA],
                          compiler_params=pltpu.CompilerParams(has_side_effects=True, vmem_limit_bytes=VMEM_LIMIT))(v)


BIG = ("w_in", "w_o", "w_gate_up", "w_down")
TINY = ("w_uq", "w_ukv", "conv_w")
REPL = ("attn_norm", "mla_q_norm", "mla_kv_norm", "pool_w", "pool_scale", "swa_sinks", "mix_norm", "ffn_norm",
        "final_norm")
ORDER = ("attn_norm", "w_in", "mla_q_norm", "w_uq", "mla_kv_norm", "w_ukv", "conv_w", "pool_w", "pool_scale",
         "swa_sinks", "mix_norm", "w_o", "ffn_norm", "w_gate_up", "w_down", "final_norm")


def _rows_for(n, mult):
    r = -(-n // LANES)
    return -(-r // mult) * mult


def _pack(arrs, rows):
    flat = jnp.concatenate([a.reshape(-1) for a in arrs])
    return jnp.pad(flat, (0, rows * LANES - flat.shape[0])).reshape(rows, LANES)


def _unpack(buf, shapes):
    flat = buf.reshape(-1)
    out, off = [], 0
    for s in shapes:
        n = int(np.prod(s))
        out.append(flat[off:off + n].reshape(s))
        off += n
    return out


def _cols_joined(g):
    return jnp.transpose(g, (0, 2, 1, 3)).reshape(g.shape[0], g.shape[2], 4 * g.shape[3])


def _cols_split(w):
    n, a, b4 = w.shape
    return jnp.transpose(w.reshape(n, a, 4, b4 // 4), (0, 2, 1, 3))


def _rope_tables(S):
    inv = 1.0 / (10000.0 ** (jnp.arange(0, 32, 2, dtype=F32) / 32))
    ang = jnp.arange(S, dtype=F32)[:, None] * inv[None, :]
    cos, sin = jnp.cos(ang), jnp.sin(ang)
    z = lambda w: jnp.zeros((S, w), F32)
    tc = jnp.concatenate([jnp.ones((S, 64), F32), cos, cos, jnp.ones((S, 32), F32)], axis=1)
    ts1 = jnp.concatenate([z(64), -sin, z(48)], axis=1)
    ts2 = jnp.concatenate([z(80), sin, z(32)], axis=1)
    return tc, ts1, ts2


def _pad_w_in(w):
    z = lambda n: jnp.zeros(w.shape[:-1] + (n,), w.dtype)
    return jnp.concatenate([w[..., 0:384], z(64), w[..., 384:416], z(32), w[..., 416:1952]], axis=-1)


def _unpad_w_in(d):
    return jnp.concatenate([d[..., 0:384], d[..., 448:480], d[..., 512:2048]], axis=-1)


def _pad_heads(w, src, offs):
    cols = []
    for h in range(HEADS):
        src0, n = src[h]
        z = lambda k: jnp.zeros(w.shape[:-1] + (k,), w.dtype)
        cols += [z(offs[h]), w[..., src0:src0 + n], z(128 - offs[h] - n)]
    return jnp.concatenate(cols, axis=-1)


UQ_SRC = [(h * 96, 96) for h in range(HEADS)]
KN_SRC = [(h * 128, 64) for h in range(HEADS)]
V_SRC = [(h * 128 + 64, 64) for h in range(HEADS)]
ZERO_OFF = [0] * HEADS
V_OFF = [(h % 2) * 64 for h in range(HEADS)]


def _unpad_heads(d, src, offs):
    return [d[..., h * 128 + offs[h]: h * 128 + offs[h] + src[h][1]] for h in range(HEADS)]


def kernel(x, attn_norm, w_in, mla_q_norm, w_uq, mla_kv_norm, w_ukv, conv_w, pool_w, pool_scale, swa_sinks, mix_norm, w_o, ffn_norm, w_gate_up, w_down, final_norm, loss_target, m_attn_norm, m_w_in, m_mla_q_norm, m_w_uq, m_mla_kv_norm, m_w_ukv, m_conv_w, m_pool_w, m_pool_scale, m_swa_sinks, m_mix_norm, m_w_o, m_ffn_norm, m_w_gate_up, m_w_down, m_final_norm, v_attn_norm, v_w_in, v_mla_q_norm, v_w_uq, v_mla_kv_norm, v_w_ukv, v_conv_w, v_pool_w, v_pool_scale, v_swa_sinks, v_mix_norm, v_w_o, v_ffn_norm, v_w_gate_up, v_w_down, v_final_norm):
    W = dict(attn_norm=attn_norm, w_in=w_in, mla_q_norm=mla_q_norm, w_uq=w_uq, mla_kv_norm=mla_kv_norm, w_ukv=w_ukv,
             conv_w=conv_w, pool_w=pool_w, pool_scale=pool_scale, swa_sinks=swa_sinks, mix_norm=mix_norm, w_o=w_o,
             ffn_norm=ffn_norm, w_gate_up=w_gate_up, w_down=w_down, final_norm=final_norm)
    M1 = dict(attn_norm=m_attn_norm, w_in=m_w_in, mla_q_norm=m_mla_q_norm, w_uq=m_w_uq, mla_kv_norm=m_mla_kv_norm,
              w_ukv=m_w_ukv, conv_w=m_conv_w, pool_w=m_pool_w, pool_scale=m_pool_scale, swa_sinks=m_swa_sinks,
              mix_norm=m_mix_norm, w_o=m_w_o, ffn_norm=m_ffn_norm, w_gate_up=m_w_gate_up, w_down=m_w_down,
              final_norm=m_final_norm)
    V2 = dict(attn_norm=v_attn_norm, w_in=v_w_in, mla_q_norm=v_mla_q_norm, w_uq=v_w_uq, mla_kv_norm=v_mla_kv_norm,
              w_ukv=v_w_ukv, conv_w=v_conv_w, pool_w=v_pool_w, pool_scale=v_pool_scale, swa_sinks=v_swa_sinks,
              mix_norm=v_mix_norm, w_o=v_w_o, ffn_norm=v_ffn_norm, w_gate_up=v_w_gate_up, w_down=v_w_down,
              final_norm=v_final_norm)
    S = x.shape[1]
    xc, yc, cc = _me()
    chip = 2 * xc + yc
    kc = jnp.stack([chip, cc]).astype(jnp.int32)

    names = ("w_in", "w_uq", "w_ukv", "w_o", "w_gate_up", "w_down", "conv_w")
    placed = [_place(W[n], kc, F32 if n == "conv_w" else BF16, f"place_{n}") for n in names]
    gi, gq, gkv, go, gu4, gd, gcv = _gather(placed, "gather_weights")
    win_p = _pad_w_in(_cols_joined(gi))
    wuq_p = _pad_heads(_cols_joined(gq), UQ_SRC, ZERO_OFF)
    wukv = _cols_joined(gkv)
    wk_p = _pad_heads(wukv, KN_SRC, ZERO_OFF)
    wv_p = _pad_heads(wukv, V_SRC, V_OFF)
    conv8 = jnp.pad(_cols_joined(gcv), ((0, 0), (0, 5), (0, 0)))
    wo = go.reshape(2, D, D)
    wdown = gd.reshape(2, D_FF, D)
    pwd = jnp.concatenate([jnp.concatenate(
        [jnp.pad(pool_w[:, 2 * b], ((0, 0), (0, 0), (0, 64))), jnp.pad(pool_w[:, 2 * b + 1], ((0, 0), (0, 0), (64, 0)))],
        axis=1) for b in range(2)], axis=1).astype(BF16)
    tabs = _rope_tables(S)
    g_attn, g_q, g_kv, g_mix, g_ffn, g_ps = (_g3(W[n]) for n in ("attn_norm", "mla_q_norm", "mla_kv_norm", "mix_norm",
                                                                  "ffn_norm", "pool_scale"))

    xs = [x[0]]
    saved = []
    for l in range(DEPTH):
        x0 = xs[-1]
        proj, h = _norm_mm(x0, g_attn, l, win_p, _wspec_in(l), D_INP, 1024, f"in_proj{l}")
        q, k, v = _mla_prep(proj, g_q, g_kv, wuq_p, wk_p, wv_p, tabs, l, f"mla_prep{l}")
        oh, lse = _mla_attn(q, k, v, f"mla_attn{l}")
        yb = _conv(proj, conv8, l, f"conv{l}")
        ycp = _pool(proj, pwd, g_ps, l, f"pool{l}")
        yd = _swa(proj, swa_sinks, l, f"swa{l}")
        x1, ycat, mixed = _mix_out(x0, oh, yb, ycp, yd, g_mix, wo, l, f"mix_out{l}")
        gu, h2 = _norm_mm(x1, g_ffn, l, gu4, _wspec_gu(l), 2 * D_FF, 2 * D_FF // 4, f"gate_up{l}")
        x2, act = _swiglu_mm_res(x1, gu, wdown, l, f"down{l}")
        saved.append(dict(x0=x0, proj=proj, h=h, q=q, k=k, v=v, oh=oh, lse=lse, x1=x1, ycat=ycat, mixed=mixed,
                          gu=gu, h2=h2, act=act))
        xs.append(x2)

    dx, dx16, dg_final, loss_tile = _loss_head(xs[-1], final_norm.reshape(1, D), loss_target[0], "loss_head")
    loss = lax.psum(loss_tile[0, 0] * (0.5 / D), ("x", "y", "c"))

    G = {n: [None] * DEPTH for n in TINY + REPL if n != "final_norm"}
    gw_in = gw_o = gw_gu = gw_down = None
    for l in reversed(range(DEPTH)):
        sv = saved[l]
        dgu = _bwd_down(dx16, wdown, sv["gu"], l, f"down_bwd{l}")
        gw_down = _mm_tn(sv["act"], dx16, l, gw_down, f"dw_down{l}")
        gw_gu = _mm_tn(sv["h2"], dgu, l, gw_gu, f"dw_gate_up{l}", split4=True)
        dx1, dx1_16, dg = _mm_nt_normbwd(dgu, gu4, l, sv["x1"], g_ffn, dx, 1, f"gate_up_bwd{l}")
        G["ffn_norm"][l] = dg[0]
        gw_o = _mm_tn(sv["mixed"], dx1_16, l, gw_o, f"dw_o{l}")
        dycat, dg = _mm_nt_normbwd(dx1_16, wo.reshape(2, 1, D, D), l, sv["ycat"], g_mix, None, 4, f"mix_bwd{l}")
        G["mix_norm"][l] = dg[0]

        proj = sv["proj"]
        dq, dk, dv = _mla_attn_bwd(sv["q"], sv["k"], sv["v"], dycat, sv["oh"], sv["lse"], f"mla_attn_bwd{l}")
        dcq, dckv, dkr, dwuq, dwk, dwv, dgq, dgkv = _mla_prep_bwd(
            dq, dk, dv, proj, g_q, g_kv, wuq_p, wk_p, wv_p, tabs, l, f"mla_prep_bwd{l}")
        dgb, dgc, duc, dcw = _conv_bwd(proj, conv8, dycat, l, f"conv_bwd{l}")
        dup, dpw, dps = _pool_bwd(proj, pwd, g_ps, dycat, l, f"pool_bwd{l}")
        dqs, dks, dvs, dsink = _swa_bwd(proj, swa_sinks, dycat, l, f"swa_bwd{l}")
        dproj = jnp.concatenate([dcq, dckv, dkr, dgb, dgc, duc, dup, dqs, dks, dvs], axis=1)
        gw_in = _mm_tn(sv["h"], dproj, l, gw_in, f"dw_in{l}")
        dx, dx16, dg = _mm_nt_normbwd(dproj, win_p.reshape(2, 1, D, D_INP), l, sv["x0"], g_attn, dx1, 1, f"in_proj_bwd{l}")
        G["attn_norm"][l] = dg[0]
        G["mla_q_norm"][l] = dgq[0]
        G["mla_kv_norm"][l] = dgkv[0]
        G["w_uq"][l] = jnp.concatenate(_unpad_heads(dwuq, UQ_SRC, ZERO_OFF), axis=1)
        kn, vv = _unpad_heads(dwk, KN_SRC, ZERO_OFF), _unpad_heads(dwv, V_SRC, V_OFF)
        G["w_ukv"][l] = jnp.concatenate([t for h in range(HEADS) for t in (kn[h], vv[h])], axis=1)
        G["conv_w"][l] = dcw[0:3]
        G["pool_w"][l] = jnp.stack([dpw[0:64, 0:64], dpw[64:128, 64:128], dpw[128:192, 0:64], dpw[192:256, 64:128]])
        G["pool_scale"][l] = dps[0]
        G["swa_sinks"][l] = dsink[0, 0:4]
    grad_x = dx[None]
    Gl = {n: jnp.stack(G[n]) for n in G}
    Gl["final_norm"] = dg_final[0]

    gbig = [_cols_split(_unpad_w_in(gw_in)), gw_o.reshape(2, 4, D // 4, D), gw_gu, gw_down.reshape(2, 4, D_FF // 4, D)]
    got = _swap_layers(gbig, "rs_swap_cores")
    pairs = [_pair_sum(g, o, kc, f"rs_pair_sum_{n}") for g, o, n in zip(gbig, got, BIG)]
    got3 = _exchange_chips([p[1] for p in pairs], "rs_exchange_chips")
    us = [_chip_sum(p[0], o3, kc, f"rs_chip_sum_{n}") for p, o3, n in zip(pairs, got3, BIG)]
    gsum = _join_layers(us, "rs_join_cores")
    res = {}
    for n, g in zip(BIG, gsum):
        d_, m_, v_ = _adamw(W[n], g, M1[n], V2[n], f"adamw_{n}")
        res["g", n], res["d", n], res["m", n], res["v", n] = g, d_, m_, v_

    small = TINY + REPL
    full_shapes = [Gl[n].shape for n in small]
    Ra = _rows_for(sum(int(np.prod(s)) for s in full_shapes), 8)
    summed = _unpack(_allsum_small(_pack([Gl[n] for n in small], Ra), "allsum_small"), full_shapes)
    gs = {}
    for n, g in zip(small, summed):
        if n in TINY:
            wdt = W[n].shape[2]
            g = lax.dynamic_slice_in_dim(g, chip * wdt, wdt, axis=2)
        gs[n] = g
    own_shapes = [W[n].shape for n in small]
    Rb = _rows_for(sum(int(np.prod(s)) for s in own_shapes), 8)
    pk = lambda src: _pack([src[n] for n in small], Rb).reshape(1, Rb, LANES)
    d_s, m_s, v_s = _adamw(pk(W), pk(gs), pk(M1), pk(V2), "adamw_small")
    for key, buf in (("d", d_s), ("m", m_s), ("v", v_s)):
        for n, a in zip(small, _unpack(buf, own_shapes)):
            res[key, n] = a
    for n in small:
        res["g", n] = gs[n]

    return (loss, grad_x, *[res["g", n] for n in ORDER], *[res["d", n] for n in ORDER],
            *[res["m", n] for n in ORDER], *[res["v", n] for n in ORDER])
```

```python
import math

import numpy as np
import jax
import jax.numpy as jnp
from jax import lax
from jax.experimental import pallas as pl
from jax.experimental.pallas import tpu as pltpu

F32, BF16 = jnp.float32, jnp.bfloat16
SDS = jax.ShapeDtypeStruct
BS = pl.BlockSpec
MESH = pl.DeviceIdType.MESH

D = 1024
DEPTH = 2
HEADS = 4
D_FF = 2816
D_INP = 2048
EPS = 1e-6
SWA_WINDOW = 128
BLK = 128
SLOPES = tuple(2.0 ** (-8.0 * (i + 1) / 4) for i in range(4))
QK_SCALE = 1.0 / math.sqrt(96)
SWA_SCALE = 1.0 / math.sqrt(64)
LR, B1, B2, ADAM_EPS, WD, STEP = 0.001, 0.9, 0.999, 1e-08, 0.01, 10

LANES = 1024
VMEM_LIMIT = 48 * 1024 * 1024
NEG_INF = float("-inf")

C_CQ, C_CKV, C_KR, C_GB, C_GC, C_UC, C_UP, C_QS, C_KS, C_VS = 0, 256, 384, 512, 768, 1024, 1280, 1536, 1792, 1920


def _params(ngrid):
    return pltpu.CompilerParams(dimension_semantics=("arbitrary",) * ngrid, vmem_limit_bytes=VMEM_LIMIT)


def _pc(body, *, name, grid, in_specs, out_specs, out_shape, scratch=(), aliases=None):
    return pl.pallas_call(
        body, name=name, grid=grid, in_specs=in_specs, out_specs=out_specs, out_shape=out_shape,
        scratch_shapes=scratch, input_output_aliases=aliases or {}, compiler_params=_params(len(grid)))


def _dot(a, b):
    return jnp.dot(a, b, preferred_element_type=F32)


def _dot_nt(a, b):
    return lax.dot_general(a, b, (((1,), (1,)), ((), ())), preferred_element_type=F32)


def _dot_tn(a, b):
    return lax.dot_general(a, b, (((0,), (0,)), ((), ())), preferred_element_type=F32)


def _tile(n, cap):
    if n <= cap:
        return n
    t = cap - cap % 128
    while n % t:
        t -= 128
    return t


def _row_tile(a, b, cap=262144):
    bp = -(-b // 128) * 128
    best = None
    for t in range(8, a + 1, 8):
        if a % t == 0 and t * bp <= cap:
            best = t
    return best if best is not None else a


def _g3(a):
    return a.reshape(a.shape[0], 1, a.shape[1])


def _norm_mm(x, g3, l, w, wspec, N, tn, out_dtype, name):
    S, K = x.shape
    tm = min(512, S)

    def body(x_ref, g_ref, w_ref, y_ref, h_ref):
        xv = x_ref[...]
        r = lax.rsqrt(jnp.mean(xv * xv, axis=-1, keepdims=True) + EPS)
        h = (xv * r * g_ref[...]).astype(BF16)
        h_ref[...] = h
        y_ref[...] = _dot(h, w_ref[...]).astype(out_dtype)

    return _pc(body, name=name, grid=(S // tm, N // tn),
               in_specs=[BS((tm, K), lambda i, j: (i, 0)), BS((None, 1, K), lambda i, j: (l, 0, 0)), wspec],
               out_specs=[BS((tm, tn), lambda i, j: (i, j)), BS((tm, K), lambda i, j: (i, 0))],
               out_shape=[SDS((S, N), out_dtype), SDS((S, K), BF16)])(x, g3, w)


def _wspec_in(l):
    return BS((None, D, 1024), lambda i, j: (l, 0, j))


def _wspec_gu(l):
    return BS((None, None, D, 2 * D_FF // 4), lambda i, j: (l, j, 0, 0))


def _mix_out(x0, ya, yb, yc, yd, gmix3, wo, l, name):
    S = x0.shape[0]
    tm = min(512, S)

    def body(x_ref, ya_ref, yb_ref, yc_ref, yd_ref, g_ref, w_ref, x1_ref, ycat_ref, mixed_ref):
        groups = [ya_ref[...], yb_ref[...], yc_ref[...], yd_ref[...]]
        for gi, yg in enumerate(groups):
            sl = slice(gi * 256, (gi + 1) * 256)
            r = lax.rsqrt(jnp.mean(yg * yg, axis=-1, keepdims=True) + EPS)
            ycat_ref[:, sl] = yg
            mixed_ref[:, sl] = (yg * r * g_ref[:, sl]).astype(BF16)
        x1_ref[...] = x_ref[...] + _dot(mixed_ref[...], w_ref[...])

    row = lambda w: BS((tm, w), lambda i: (i, 0))
    return _pc(body, name=name, grid=(S // tm,),
               in_specs=[row(D), row(256), row(256), row(256), row(256), BS((None, 1, D), lambda i: (l, 0, 0)),
                         BS((None, D, D), lambda i: (l, 0, 0))],
               out_specs=[row(D), row(D), row(D)],
               out_shape=[SDS((S, D), F32), SDS((S, D), F32), SDS((S, D), BF16)])(x0, ya, yb, yc, yd, gmix3, wo)


def _swiglu_mm_res(x1, gu, wdown, l, name):
    S = x1.shape[0]
    tm = min(256, S)

    def body(x_ref, gate_ref, up_ref, w_ref, x2_ref, act_ref):
        gt = gate_ref[...].astype(F32)
        act = (gt / (1.0 + jnp.exp(-gt)) * up_ref[...].astype(F32)).astype(BF16)
        act_ref[...] = act
        x2_ref[...] = x_ref[...] + _dot(act, w_ref[...])

    return _pc(body, name=name, grid=(S // tm,),
               in_specs=[BS((tm, D), lambda i: (i, 0)), BS((tm, D_FF), lambda i: (i, 0)),
                         BS((tm, D_FF), lambda i: (i, 1)), BS((None, D_FF, D), lambda i: (l, 0, 0))],
               out_specs=[BS((tm, D), lambda i: (i, 0)), BS((tm, D_FF), lambda i: (i, 0))],
               out_shape=[SDS((S, D), F32), SDS((S, D_FF), BF16)])(x1, gu, gu, wdown)


def _loss_head(x, g, tgt, name):
    S = x.shape[0]
    tm = min(512, S)

    def body(x_ref, g_ref, t_ref, dx_ref, dx16_ref, dg_ref, loss_ref):
        @pl.when(pl.program_id(0) == 0)
        def _():
            dg_ref[...] = jnp.zeros_like(dg_ref)
            loss_ref[...] = jnp.zeros_like(loss_ref)

        xv = x_ref[...]
        r = lax.rsqrt(jnp.mean(xv * xv, axis=-1, keepdims=True) + EPS)
        xh = xv * r
        gv = g_ref[...]
        diff = xh * gv - t_ref[...]
        loss_ref[...] += jnp.sum(diff * diff)
        dy = diff * (1.0 / D)
        dg_ref[...] += jnp.sum(dy * xh, axis=0, keepdims=True)
        dxh = dy * gv
        dx = r * (dxh - xh * jnp.mean(dxh * xh, axis=-1, keepdims=True))
        dx_ref[...] = dx
        dx16_ref[...] = dx.astype(BF16)

    row = BS((tm, D), lambda i: (i, 0))
    return _pc(body, name=name, grid=(S // tm,),
               in_specs=[row, BS((1, D), lambda i: (0, 0)), row],
               out_specs=[row, row, BS((8, D), lambda i: (0, 0)), BS((8, 128), lambda i: (0, 0))],
               out_shape=[SDS((S, D), F32), SDS((S, D), BF16), SDS((8, D), F32), SDS((8, 128), F32)])(x, g, tgt)


def _mm_tn(a, b, l, prev, name, split4=False):
    S, Ka = a.shape
    N = b.shape[1]
    if split4:
        ta, tn = _tile(Ka, 256), N // 4
        out_shape = SDS((2, 4, Ka, tn), F32)
        out_spec = BS((None, None, ta, tn), lambda i, j: (l, j, i, 0))
    else:
        ta, tn = _tile(Ka, 512), _tile(N, 512)
        out_shape = SDS((2, Ka, N), F32)
        out_spec = BS((None, ta, tn), lambda i, j: (l, i, j))

    def body(a_ref, b_ref, *rest):
        rest[-1][...] = _dot_tn(a_ref[...], b_ref[...])

    in_specs = [BS((S, ta), lambda i, j: (0, i)), BS((S, tn), lambda i, j: (0, j))]
    args = [a, b]
    if prev is not None:
        in_specs.append(BS(memory_space=pl.ANY))
        args.append(prev)
    return _pc(body, name=name, grid=(Ka // ta, N // tn), in_specs=in_specs, out_specs=out_spec, out_shape=out_shape,
               aliases={2: 0} if prev is not None else None)(*args)


def _bwd_down(dx16, wdown, gu, l, name):
    S = dx16.shape[0]
    tm = min(256, S)

    def body(dx_ref, w_ref, gate_ref, up_ref, dgu_ref):
        dact = _dot_nt(dx_ref[...], w_ref[...])
        gt = gate_ref[...].astype(F32)
        sg = 1.0 / (1.0 + jnp.exp(-gt))
        dgu_ref[:, 0:D_FF] = (dact * up_ref[...].astype(F32) * (sg * (1.0 + gt * (1.0 - sg)))).astype(BF16)
        dgu_ref[:, D_FF:2 * D_FF] = (dact * (gt * sg)).astype(BF16)

    return _pc(body, name=name, grid=(S // tm,),
               in_specs=[BS((tm, D), lambda i: (i, 0)), BS((None, D_FF, D), lambda i: (l, 0, 0)),
                         BS((tm, D_FF), lambda i: (i, 0)), BS((tm, D_FF), lambda i: (i, 1))],
               out_specs=BS((tm, 2 * D_FF), lambda i: (i, 0)),
               out_shape=SDS((S, 2 * D_FF), BF16))(dx16, wdown, gu, gu)


def _mm_nt_normbwd(dy, w4, l, x, g3, dres, ngroups, name):
    S, K = dy.shape
    _, nk, _, kc = w4.shape
    tm = min(256, S)
    gw = D // ngroups
    has_res = dres is not None

    def body(*refs):
        if has_res:
            dy_ref, w_ref, x_ref, g_ref, res_ref, dx_ref, dx16_ref, dg_ref = refs
        else:
            dy_ref, w_ref, x_ref, g_ref, dx_ref, dg_ref = refs

        @pl.when(pl.program_id(0) == 0)
        def _():
            dg_ref[...] = jnp.zeros_like(dg_ref)

        dh = _dot_nt(dy_ref[:, 0:kc], w_ref[0])
        for k in range(1, nk):
            dh = dh + _dot_nt(dy_ref[:, k * kc:(k + 1) * kc], w_ref[k])
        for gi in range(ngroups):
            sl = slice(gi * gw, (gi + 1) * gw)
            xg = x_ref[:, sl]
            r = lax.rsqrt(jnp.mean(xg * xg, axis=-1, keepdims=True) + EPS)
            xh = xg * r
            dhg = dh[:, sl]
            dg_ref[:, sl] += jnp.sum(dhg * xh, axis=0, keepdims=True)
            dxh = dhg * g_ref[:, sl]
            dxg = r * (dxh - xh * jnp.mean(dxh * xh, axis=-1, keepdims=True))
            if has_res:
                dxg = dxg + res_ref[:, sl]
                dx16_ref[:, sl] = dxg.astype(BF16)
            dx_ref[:, sl] = dxg

    row = BS((tm, D), lambda i: (i, 0))
    in_specs = [BS((tm, K), lambda i: (i, 0)), BS((None, nk, D, kc), lambda i: (l, 0, 0, 0)), row,
                BS((None, 1, D), lambda i: (l, 0, 0))]
    args = [dy, w4, x, g3]
    out_specs, out_shape = [row], [SDS((S, D), F32)]
    if has_res:
        in_specs.append(row)
        args.append(dres)
        out_specs.append(row)
        out_shape.append(SDS((S, D), BF16))
    out_specs.append(BS((8, D), lambda i: (0, 0)))
    out_shape.append(SDS((8, D), F32))
    return _pc(body, name=name, grid=(S // tm,), in_specs=in_specs, out_specs=out_specs, out_shape=out_shape)(*args)


def _rope(x, c, s1, s2):
    return x * c + pltpu.roll(x, 112, axis=1) * s1 + pltpu.roll(x, 16, axis=1) * s2


def _rope_t(dy, c, s1, s2):
    return dy * c + pltpu.roll(dy * s1, 16, axis=1) + pltpu.roll(dy * s2, 112, axis=1)


def _mla_prep(proj, gq3, gkv3, wuq, wk, wv, tabs, l, name):
    S = proj.shape[0]
    tm = min(512, S)
    tc, ts1, ts2 = tabs

    def body(cq_ref, ckv_ref, kr_ref, gq_ref, gkv_ref, wuq_ref, wk_ref, wv_ref, c_ref, s1_ref, s2_ref,
             q_ref, k_ref, v_ref, kt_ref, vt_ref):
        c, s1, s2 = c_ref[...], s1_ref[...], s2_ref[...]
        cq = cq_ref[...]
        rq = lax.rsqrt(jnp.mean(cq * cq, axis=-1, keepdims=True) + EPS)
        qa = _dot((cq * rq * gq_ref[...]).astype(BF16), wuq_ref[...])
        ckv = ckv_ref[...]
        rkv = lax.rsqrt(jnp.mean(ckv * ckv, axis=-1, keepdims=True) + EPS)
        ckvn = (ckv * rkv * gkv_ref[...]).astype(BF16)
        ka = _dot(ckvn, wk_ref[...])
        va = _dot(ckvn, wv_ref[...])
        v_ref[...] = va.astype(BF16)
        vt_ref[...] = va.T.astype(BF16)
        krr = _rope(kr_ref[...], c, s1, s2)
        for h in range(HEADS):
            sl = slice(h * 128, (h + 1) * 128)
            q_ref[:, sl] = (_rope(qa[:, sl], c, s1, s2) * QK_SCALE).astype(BF16)
            kh = ka[:, sl] + krr
            k_ref[:, sl] = kh.astype(BF16)
            kt_ref[sl, :] = kh.T.astype(BF16)

    lay = lambda a, b: BS((None, a, b), lambda i: (l, 0, 0))
    tab = BS((tm, 128), lambda i: (i, 0))
    return _pc(body, name=name, grid=(S // tm,),
               in_specs=[BS((tm, 256), lambda i: (i, 0)), BS((tm, 128), lambda i: (i, 2)), BS((tm, 128), lambda i: (i, 3)),
                         lay(1, 256), lay(1, 128), lay(256, 512), lay(128, 512), lay(128, 512), tab, tab, tab],
               out_specs=[BS((tm, 512), lambda i: (i, 0))] * 3 + [BS((512, tm), lambda i: (0, i))] * 2,
               out_shape=[SDS((S, 512), BF16)] * 3 + [SDS((512, S), BF16)] * 2)(
        proj, proj, proj, gq3, gkv3, wuq, wk, wv, tc, ts1, ts2)


def _mla_prep_bwd(dq, dk, dv, proj, gq3, gkv3, wuq, wk, wv, tabs, l, name):
    S = proj.shape[0]
    tm = min(512, S)
    tc, ts1, ts2 = tabs

    def body(dq_ref, dk_ref, dv_ref, cq_ref, ckv_ref, gq_ref, gkv_ref, wuq_ref, wk_ref, wv_ref, c_ref, s1_ref, s2_ref,
             dcq_ref, dckv_ref, dkr_ref, dwuq_ref, dwk_ref, dwv_ref, dgq_ref, dgkv_ref):
        @pl.when(pl.program_id(0) == 0)
        def _():
            for r in (dwuq_ref, dwk_ref, dwv_ref, dgq_ref, dgkv_ref):
                r[...] = jnp.zeros_like(r)

        c, s1, s2 = c_ref[...], s1_ref[...], s2_ref[...]
        dqp = jnp.concatenate(
            [_rope_t(dq_ref[h * 128:(h + 1) * 128, :].T * QK_SCALE, c, s1, s2) for h in range(HEADS)], axis=1).astype(BF16)
        cq = cq_ref[...]
        rq = lax.rsqrt(jnp.mean(cq * cq, axis=-1, keepdims=True) + EPS)
        cqh = cq * rq
        gq_v = gq_ref[...]
        dwuq_ref[...] += _dot_tn((cqh * gq_v).astype(BF16), dqp)
        dcqn = _dot_nt(dqp, wuq_ref[...])
        dgq_ref[...] += jnp.sum(dcqn * cqh, axis=0, keepdims=True)
        dxh = dcqn * gq_v
        dcq_ref[...] = (rq * (dxh - cqh * jnp.mean(dxh * cqh, axis=-1, keepdims=True))).astype(BF16)

        dkb = dk_ref[...].astype(BF16)
        dvb = dv_ref[...].astype(BF16)
        ckv = ckv_ref[...]
        rkv = lax.rsqrt(jnp.mean(ckv * ckv, axis=-1, keepdims=True) + EPS)
        ckh = ckv * rkv
        gkv_v = gkv_ref[...]
        ckvn = (ckh * gkv_v).astype(BF16)
        dwk_ref[...] += _dot_tn(ckvn, dkb)
        dwv_ref[...] += _dot_tn(ckvn, dvb)
        dckvn = _dot_nt(dkb, wk_ref[...]) + _dot_nt(dvb, wv_ref[...])
        dgkv_ref[...] += jnp.sum(dckvn * ckh, axis=0, keepdims=True)
        dyh = dckvn * gkv_v
        dckv_ref[...] = (rkv * (dyh - ckh * jnp.mean(dyh * ckh, axis=-1, keepdims=True))).astype(BF16)
        dks = dk_ref[:, 0:128] + dk_ref[:, 128:256] + dk_ref[:, 256:384] + dk_ref[:, 384:512]
        dkr_ref[...] = _rope_t(dks, c, s1, s2).astype(BF16)

    full = lambda a, b: BS((a, b), lambda i: (0, 0))
    lay = lambda a, b: BS((None, a, b), lambda i: (l, 0, 0))
    tab = BS((tm, 128), lambda i: (i, 0))
    row = lambda w: BS((tm, w), lambda i: (i, 0))
    return _pc(body, name=name, grid=(S // tm,),
               in_specs=[BS((512, tm), lambda i: (0, i)), row(512), row(512), BS((tm, 256), lambda i: (i, 0)),
                         BS((tm, 128), lambda i: (i, 2)),
                         lay(1, 256), lay(1, 128), lay(256, 512), lay(128, 512), lay(128, 512), tab, tab, tab],
               out_specs=[row(256), row(128), row(128), full(256, 512), full(128, 512), full(128, 512),
                          full(8, 256), full(8, 128)],
               out_shape=[SDS((S, 256), BF16), SDS((S, 128), BF16), SDS((S, 128), BF16), SDS((256, 512), F32),
                          SDS((128, 512), F32), SDS((128, 512), F32), SDS((8, 256), F32), SDS((8, 128), F32)])(
        dq, dk, dv, proj, proj, gq3, gkv3, wuq, wk, wv, tc, ts1, ts2)


def _mla_attn(q, k, vt, name):
    S = q.shape[0]
    t = min(512, S)
    n = S // t

    def body(q_ref, k_ref, vt_ref, ya_ref, lse_ref, m_sc, l_sc, acc_sc):
        i, j = pl.program_id(1), pl.program_id(2)

        @pl.when(j == 0)
        def _():
            m_sc[...] = jnp.full_like(m_sc, NEG_INF)
            l_sc[...] = jnp.zeros_like(l_sc)
            acc_sc[...] = jnp.zeros_like(acc_sc)

        def step(masked):
            for hh in range(2):
                sl = slice(hh * 128, (hh + 1) * 128)
                st = _dot_nt(k_ref[:, sl], q_ref[:, sl])
                if masked:
                    key = lax.broadcasted_iota(jnp.int32, (t, t), 0)
                    qry = lax.broadcasted_iota(jnp.int32, (t, t), 1)
                    st = jnp.where(key <= qry, st, NEG_INF)
                m_prev = m_sc[hh]
                m_new = jnp.maximum(m_prev, jnp.max(st, axis=0, keepdims=True))
                p = jnp.exp(st - m_new)
                alpha = jnp.exp(m_prev - m_new)
                l_sc[hh] = alpha * l_sc[hh] + jnp.sum(p, axis=0, keepdims=True)
                acc_sc[hh] = alpha * acc_sc[hh] + _dot(vt_ref[sl, :], p.astype(BF16))
                m_sc[hh] = m_new

        @pl.when(j < i)
        def _():
            step(False)

        @pl.when(j == i)
        def _():
            step(True)
            ya_ref[...] = (acc_sc[0] / l_sc[0] + acc_sc[1] / l_sc[1]).T
            for hh in range(2):
                lse_ref[hh] = m_sc[hh] + jnp.log(l_sc[hh])

    return _pc(body, name=name, grid=(2, n, n),
               in_specs=[BS((t, 256), lambda p, i, j: (i, p)), BS((t, 256), lambda p, i, j: (jnp.minimum(j, i), p)),
                         BS((256, t), lambda p, i, j: (p, jnp.minimum(j, i)))],
               out_specs=[BS((t, 128), lambda p, i, j: (i, p)), BS((2, 1, t), lambda p, i, j: (p, 0, i))],
               out_shape=[SDS((S, 256), F32), SDS((HEADS, 1, S), F32)],
               scratch=[pltpu.VMEM((2, 1, t), F32), pltpu.VMEM((2, 1, t), F32), pltpu.VMEM((2, 128, t), F32)])(q, k, vt)


def _mla_delta(dycat, ya, name):
    S = ya.shape[0]
    t = min(512, S)

    def body(do_ref, ya_ref, d_ref):
        prod = do_ref[...] * ya_ref[...]
        for p in range(2):
            pt = prod[:, p * 128:(p + 1) * 128].T
            d_ref[2 * p] = jnp.sum(pt[0:64, :], axis=0, keepdims=True)
            d_ref[2 * p + 1] = jnp.sum(pt[64:128, :], axis=0, keepdims=True)

    return _pc(body, name=name, grid=(S // t,),
               in_specs=[BS((t, 256), lambda i: (i, 0)), BS((t, 256), lambda i: (i, 0))],
               out_specs=BS((HEADS, 1, t), lambda i: (0, 0, i)), out_shape=SDS((HEADS, 1, S), F32))(dycat, ya)


def _mla_attn_bwd(q, k, kt, v, dya, lse, delta, name):
    S = q.shape[0]
    t = min(512, S)
    n = S // t

    def body(q_ref, k_ref, kt_ref, v_ref, do_ref, lse_ref, delta_ref, dqt_ref, dk_ref, dv_ref):
        j, i = pl.program_id(1), pl.program_id(2)

        @pl.when((j == 0) & (i == 0))
        def _():
            dqt_ref[...] = jnp.zeros_like(dqt_ref)

        @pl.when(i == 0)
        def _():
            dk_ref[...] = jnp.zeros_like(dk_ref)
            dv_ref[...] = jnp.zeros_like(dv_ref)

        def step(masked):
            qv = q_ref[...]
            p = jnp.exp(_dot_nt(k_ref[...], qv) - lse_ref[...])
            if masked:
                key = lax.broadcasted_iota(jnp.int32, (t, t), 0)
                qry = lax.broadcasted_iota(jnp.int32, (t, t), 1)
                p = jnp.where(key <= qry, p, 0.0)
            dob = do_ref[...].astype(BF16)
            dv_ref[...] += _dot(p.astype(BF16), dob)
            ds = (p * (_dot_nt(v_ref[...], dob) - delta_ref[...])).astype(BF16)
            dk_ref[...] += _dot(ds, qv)
            cols = pl.ds(pl.multiple_of(i * t, t), t)
            dqt_ref[:, cols] += _dot(kt_ref[...], ds)

        @pl.when(i > j)
        def _():
            step(False)

        @pl.when(i == j)
        def _():
            step(True)

    qs = BS((t, 128), lambda h, j, i: (jnp.maximum(i, j), h))
    ks = BS((t, 128), lambda h, j, i: (j, h))
    rowv = BS((None, 1, t), lambda h, j, i: (h, 0, jnp.maximum(i, j)))
    return _pc(body, name=name, grid=(HEADS, n, n),
               in_specs=[qs, ks, BS((128, t), lambda h, j, i: (h, j)), ks,
                         BS((t, 128), lambda h, j, i: (jnp.maximum(i, j), h // 2)), rowv, rowv],
               out_specs=[BS((128, S), lambda h, j, i: (h, 0)), ks, ks],
               out_shape=[SDS((512, S), F32), SDS((S, 512), F32), SDS((S, 512), F32)])(q, k, kt, v, dya, lse, delta)


def _swa_scores(qm, kk, valid, distf, slope, sink):
    sc = _dot_nt(qm, kk) * SWA_SCALE
    sc = jnp.where(valid, sc - slope * distf, NEG_INF)
    m = jnp.maximum(jnp.max(sc, axis=-1, keepdims=True), sink)
    e = jnp.exp(sc - m)
    esink = jnp.exp(sink - m)
    den = jnp.sum(e, axis=-1, keepdims=True) + esink
    return e / den, esink / den


def _swa_masks():
    r = lax.broadcasted_iota(jnp.int32, (BLK, 2 * BLK), 0)
    c = lax.broadcasted_iota(jnp.int32, (BLK, 2 * BLK), 1)
    dist = r + BLK - c
    return (dist >= 0) & (dist < SWA_WINDOW), c >= BLK, dist.astype(F32)


def _to_half(xb, pos, b):
    return xb if pos == b else pltpu.roll(xb, 64, axis=1)


def _swa(proj, sinks, l, name):
    S = proj.shape[0]
    nb = S // BLK

    def body(q_ref, k_ref, v_ref, sink_ref, o_ref, kp, vp):
        kp[0:BLK, :] = jnp.zeros((BLK, 128), BF16)
        vp[0:BLK, :] = jnp.zeros((BLK, 128), BF16)
        kp[BLK:, :] = k_ref[...].astype(BF16)
        vp[BLK:, :] = v_ref[...].astype(BF16)
        lo = lax.broadcasted_iota(jnp.int32, (BLK, 128), 1) < 64
        band, cur, distf = _swa_masks()

        def blk(i, carry):
            st = pl.multiple_of(i * BLK, BLK)
            kk = kp[pl.ds(st, 2 * BLK), :]
            vv = vp[pl.ds(st, 2 * BLK), :]
            valid = band & (cur | (i > 0))
            for b in range(2):
                half = lo if b == 0 else ~lo
                qb = q_ref[pl.ds(st, BLK), b * 128:(b + 1) * 128]
                outs = []
                for pos in range(2):
                    h = 2 * b + pos
                    qm = jnp.where(half, _to_half(qb, pos, b), 0.0).astype(BF16)
                    p, _ = _swa_scores(qm, kk, valid, distf, SLOPES[h], sink_ref[l, h])
                    outs.append(_to_half(_dot(p.astype(BF16), vv), pos, b))
                o_ref[pl.ds(st, BLK), b * 128:(b + 1) * 128] = jnp.where(lo, outs[0], outs[1])
            return carry

        lax.fori_loop(0, nb, blk, 0)

    return _pc(body, name=name, grid=(1,),
               in_specs=[BS((S, 256), lambda i: (0, C_QS // 256)), BS((S, 128), lambda i: (0, C_KS // 128)),
                         BS((S, 128), lambda i: (0, C_VS // 128)), BS(memory_space=pltpu.SMEM)],
               out_specs=BS((S, 256), lambda i: (0, 0)),
               out_shape=SDS((S, 256), F32),
               scratch=[pltpu.VMEM((S + BLK, 128), BF16), pltpu.VMEM((S + BLK, 128), BF16)])(proj, proj, proj, sinks)


def _swa_bwd(proj, sinks, dyd, l, name):
    S = proj.shape[0]
    nb = S // BLK

    def body(q_ref, k_ref, v_ref, sink_ref, do_ref, dq_ref, dk_ref, dv_ref, dsink_ref, kp, vp, dkp, dvp):
        kp[0:BLK, :] = jnp.zeros((BLK, 128), BF16)
        vp[0:BLK, :] = jnp.zeros((BLK, 128), BF16)
        kp[BLK:, :] = k_ref[...].astype(BF16)
        vp[BLK:, :] = v_ref[...].astype(BF16)
        dkp[...] = jnp.zeros_like(dkp)
        dvp[...] = jnp.zeros_like(dvp)
        lo = lax.broadcasted_iota(jnp.int32, (BLK, 128), 1) < 64
        lane8 = lax.broadcasted_iota(jnp.int32, (8, 128), 1)
        band, cur, distf = _swa_masks()

        def blk(i, dsink):
            st = pl.multiple_of(i * BLK, BLK)
            kk = kp[pl.ds(st, 2 * BLK), :]
            vv = vp[pl.ds(st, 2 * BLK), :]
            valid = band & (cur | (i > 0))
            dkk = jnp.zeros((2 * BLK, 128), F32)
            dvv = jnp.zeros((2 * BLK, 128), F32)
            for b in range(2):
                half = lo if b == 0 else ~lo
                qb = q_ref[pl.ds(st, BLK), b * 128:(b + 1) * 128]
                dob = do_ref[pl.ds(st, BLK), b * 128:(b + 1) * 128]
                dqs = []
                for pos in range(2):
                    h = 2 * b + pos
                    qm = jnp.where(half, _to_half(qb, pos, b), 0.0).astype(BF16)
                    dom = jnp.where(half, _to_half(dob, pos, b), 0.0).astype(BF16)
                    p, psink = _swa_scores(qm, kk, valid, distf, SLOPES[h], sink_ref[l, h])
                    dp = _dot_nt(dom, vv)
                    dvv = dvv + _dot_tn(p.astype(BF16), dom)
                    delta = jnp.sum(p * dp, axis=-1, keepdims=True)
                    dsink = dsink + jnp.where(lane8 == h, -jnp.sum(psink * delta), 0.0)
                    dsc = (p * (dp - delta) * SWA_SCALE).astype(BF16)
                    dqs.append(_to_half(_dot(dsc, kk), pos, b))
                    dkk = dkk + _dot_tn(dsc, qm)
                dq_ref[pl.ds(st, BLK), b * 128:(b + 1) * 128] = jnp.where(lo, dqs[0], dqs[1]).astype(BF16)
            dkp[pl.ds(st, 2 * BLK), :] += dkk
            dvp[pl.ds(st, 2 * BLK), :] += dvv
            return dsink

        dsink_ref[...] = lax.fori_loop(0, nb, blk, jnp.zeros((8, 128), F32))
        dk_ref[...] = dkp[BLK:, :].astype(BF16)
        dv_ref[...] = dvp[BLK:, :].astype(BF16)

    return _pc(body, name=name, grid=(1,),
               in_specs=[BS((S, 256), lambda i: (0, C_QS // 256)), BS((S, 128), lambda i: (0, C_KS // 128)),
                         BS((S, 128), lambda i: (0, C_VS // 128)), BS(memory_space=pltpu.SMEM),
                         BS((S, 256), lambda i: (0, 3))],
               out_specs=[BS((S, 256), lambda i: (0, 0)), BS((S, 128), lambda i: (0, 0)), BS((S, 128), lambda i: (0, 0)),
                          BS((8, 128), lambda i: (0, 0))],
               out_shape=[SDS((S, 256), BF16), SDS((S, 128), BF16), SDS((S, 128), BF16), SDS((8, 128), F32)],
               scratch=[pltpu.VMEM((S + BLK, 128), BF16), pltpu.VMEM((S + BLK, 128), BF16),
                        pltpu.VMEM((S + BLK, 128), F32), pltpu.VMEM((S + BLK, 128), F32)])(proj, proj, proj, sinks, dyd)


def _down(x, k, t):
    return jnp.where(t >= k, pltpu.roll(x, k, axis=0), 0.0)


def _up(x, k, t):
    n = x.shape[0]
    return jnp.where(t < n - k, pltpu.roll(x, n - k, axis=0), 0.0)


def _conv(proj, w8, l, name):
    S = proj.shape[0]

    def body(gb_ref, gc_ref, u_ref, w_ref, y_ref):
        t = lax.broadcasted_iota(jnp.int32, (S, 128), 0)
        z = gc_ref[...] * u_ref[...]
        c = w_ref[2:3, :] * z + w_ref[1:2, :] * _down(z, 1, t) + w_ref[0:1, :] * _down(z, 2, t)
        y_ref[...] = gb_ref[...] * c

    col = lambda c0: BS((S, 128), lambda i: (0, c0 // 128 + i))
    return _pc(body, name=name, grid=(2,),
               in_specs=[col(C_GB), col(C_GC), col(C_UC), BS((None, 8, 128), lambda i: (l, 0, i))],
               out_specs=BS((S, 128), lambda i: (0, i)), out_shape=SDS((S, 256), F32))(proj, proj, proj, w8)


def _conv_bwd(proj, w8, dycat, l, name):
    S = proj.shape[0]

    def body(gb_ref, gc_ref, u_ref, w_ref, dy_ref, dgb_ref, dgc_ref, du_ref, dw_ref):
        t = lax.broadcasted_iota(jnp.int32, (S, 128), 0)
        gc, u = gc_ref[...], u_ref[...]
        z = gc * u
        z1, z2 = _down(z, 1, t), _down(z, 2, t)
        w0, w1, w2 = w_ref[0:1, :], w_ref[1:2, :], w_ref[2:3, :]
        dy = dy_ref[...]
        dgb_ref[...] = (dy * (w2 * z + w1 * z1 + w0 * z2)).astype(BF16)
        dc = dy * gb_ref[...]
        dz = w2 * dc + w1 * _up(dc, 1, t) + w0 * _up(dc, 2, t)
        dgc_ref[...] = (dz * u).astype(BF16)
        du_ref[...] = (dz * gc).astype(BF16)
        row = lax.broadcasted_iota(jnp.int32, (8, 128), 0)
        sums = [jnp.sum(dc * zz, axis=0, keepdims=True) for zz in (z2, z1, z)]
        dw_ref[...] = jnp.where(row == 0, sums[0], jnp.where(row == 1, sums[1], jnp.where(row == 2, sums[2], 0.0)))

    col = lambda c0: BS((S, 128), lambda i: (0, c0 // 128 + i))
    out = BS((S, 128), lambda i: (0, i))
    return _pc(body, name=name, grid=(2,),
               in_specs=[col(C_GB), col(C_GC), col(C_UC), BS((None, 8, 128), lambda i: (l, 0, i)), col(256)],
               out_specs=[out, out, out, BS((8, 128), lambda i: (0, i))],
               out_shape=[SDS((S, 256), BF16)] * 3 + [SDS((8, 256), F32)])(proj, proj, proj, w8, dycat)


def _pool_parts(u, t, first):
    lo = lax.broadcasted_iota(jnp.int32, u.shape, 1) < 64
    s2 = u + _down(u, 1, t)
    s4 = s2 + _down(s2, 2, t)
    s8 = s4 + _down(s4, 4, t)
    s16 = s8 + _down(s8, 8, t)
    win = jnp.where(lo, jnp.where(first, s2, s8), jnp.where(first, s4, s16))
    wv = jnp.where(lo, jnp.where(first, 2, 8), jnp.where(first, 4, 16))
    cnt = jnp.minimum(t + 1, wv).astype(F32)
    return win, cnt, lo


def _pool(proj, pwd, scale3, l, name):
    S = proj.shape[0]

    def body(u_ref, pw_ref, sc_ref, y_ref):
        t = lax.broadcasted_iota(jnp.int32, (S, 128), 0)
        u = u_ref[...]
        win, cnt, _ = _pool_parts(u, t, pl.program_id(0) == 0)
        pooled = win / cnt - u
        y_ref[...] = _dot(pooled.astype(BF16), pw_ref[...]) * sc_ref[...]

    return _pc(body, name=name, grid=(2,),
               in_specs=[BS((S, 128), lambda i: (0, C_UP // 128 + i)), BS((None, 128, 128), lambda i: (l, i, 0)),
                         BS((None, 1, 128), lambda i: (l, 0, i))],
               out_specs=BS((S, 128), lambda i: (0, i)), out_shape=SDS((S, 256), F32))(proj, pwd, scale3)


def _pool_bwd(proj, pwd, scale3, dycat, l, name):
    S = proj.shape[0]

    def body(u_ref, pw_ref, sc_ref, dy_ref, du_ref, dpw_ref, dsc_ref):
        t = lax.broadcasted_iota(jnp.int32, (S, 128), 0)
        first = pl.program_id(0) == 0
        u = u_ref[...]
        win, cnt, lo = _pool_parts(u, t, first)
        pooled = (win / cnt - u).astype(BF16)
        pw = pw_ref[...]
        dy = dy_ref[...]
        dsc_ref[...] = jnp.broadcast_to(jnp.sum(dy * _dot(pooled, pw), axis=0, keepdims=True), (8, 128))
        dmb = (dy * sc_ref[...]).astype(BF16)
        dpw_ref[...] = _dot_tn(pooled, dmb)
        dpooled = _dot_nt(dmb, pw)
        a1 = dpooled / cnt
        a2 = a1 + _up(a1, 1, t)
        a4 = a2 + _up(a2, 2, t)
        a8 = a4 + _up(a4, 4, t)
        a16 = a8 + _up(a8, 8, t)
        dwin = jnp.where(lo, jnp.where(first, a2, a8), jnp.where(first, a4, a16))
        du_ref[...] = (dwin - dpooled).astype(BF16)

    return _pc(body, name=name, grid=(2,),
               in_specs=[BS((S, 128), lambda i: (0, C_UP // 128 + i)), BS((None, 128, 128), lambda i: (l, i, 0)),
                         BS((None, 1, 128), lambda i: (l, 0, i)), BS((S, 128), lambda i: (0, 4 + i))],
               out_specs=[BS((S, 128), lambda i: (0, i)), BS((128, 128), lambda i: (i, 0)), BS((8, 128), lambda i: (0, i))],
               out_shape=[SDS((S, 256), BF16), SDS((256, 128), F32), SDS((8, 256), F32)])(proj, pwd, scale3, dycat)


def _adamw(w, g, m, v, name):
    n, a, b = w.shape
    tr = _row_tile(a, b)

    def body(w_ref, g_ref, m_ref, v_ref, d_ref, nm_ref, nv_ref):
        gv = g_ref[...]
        m_new = B1 * m_ref[...] + (1.0 - B1) * gv
        v_new = B2 * v_ref[...] + (1.0 - B2) * (gv * gv)
        m_hat = m_new / (1.0 - B1 ** STEP)
        v_hat = v_new / (1.0 - B2 ** STEP)
        d_ref[...] = -LR * (m_hat / (jnp.sqrt(v_hat) + ADAM_EPS) + WD * w_ref[...])
        nm_ref[...] = m_new
        nv_ref[...] = v_new

    sp = BS((None, tr, b), lambda i, t: (i, t, 0))
    return _pc(body, name=name, grid=(n, a // tr), in_specs=[sp] * 4, out_specs=[sp] * 3,
               out_shape=[SDS((n, a, b), F32)] * 3)(w, g, m, v)


def _prefetch_call(body, name, grid, in_specs, out_specs, out_shape):
    gs = pltpu.PrefetchScalarGridSpec(num_scalar_prefetch=1, grid=grid, in_specs=in_specs, out_specs=out_specs)
    return pl.pallas_call(body, name=name, grid_spec=gs, out_shape=out_shape, compiler_params=_params(len(grid)))


def _place(w, kc, dtype, name):
    _, a, b = w.shape

    def body(kc_ref, w_ref, o_ref):
        o_ref[...] = w_ref[...].astype(dtype)

    return _prefetch_call(body, name, (2,), [BS((None, a, b), lambda l, kc: (l, 0, 0))],
                          BS((None, None, a, b), lambda l, kc: (l, kc[0], 0, 0)), SDS((2, 4, a, b), dtype))(kc, w)


def _pair_sum(g, got, kc, name):
    _, _, a, b = g.shape
    tr = _row_tile(a, b)

    def body(kc_ref, a_ref, b_ref, t32_ref, t16_ref):
        s = a_ref[...] + b_ref[...]
        t32_ref[...] = s
        t16_ref[...] = s.astype(BF16)

    sp = BS((None, tr, b), lambda k, t, kc: (k, t, 0))
    return _prefetch_call(body, name, (4, a // tr),
                          [BS((None, None, tr, b), lambda k, t, kc: (kc[1], k, t, 0)), sp], [sp, sp],
                          [SDS((4, a, b), F32), SDS((4, a, b), BF16)])(kc, g, got)


def _chip_sum(t32, got3, kc, name):
    _, a, b = t32.shape
    tr = _row_tile(a, b)

    def body(kc_ref, a_ref, b_ref, u_ref):
        u_ref[...] = ((a_ref[...] + b_ref[0].astype(F32)) + b_ref[1].astype(F32)) + b_ref[2].astype(F32)

    return _prefetch_call(body, name, (a // tr,),
                          [BS((None, tr, b), lambda t, kc: (kc[0], t, 0)), BS((3, tr, b), lambda t, kc: (0, t, 0))],
                          BS((None, tr, b), lambda t, kc: (kc[1], t, 0)), SDS((2, a, b), F32))(kc, t32, got3)


def _me():
    return lax.axis_index("x"), lax.axis_index("y"), lax.axis_index("c")


def _other_chips(x, y):
    return [(1 - x, y), (x, 1 - y), (1 - x, 1 - y)]


ANY = BS(memory_space=pl.ANY)
COMM_PARAMS = pltpu.CompilerParams(has_side_effects=True)


def _gather(arrs, name):
    n = len(arrs)

    def body(*refs):
        outs, send_sems, recv_sems = refs[n:2 * n], refs[2 * n], refs[2 * n + 1]
        x, y, c = _me()
        sib = (x, y, 1 - c)
        chips = _other_chips(x, y)

        def copy(t, k, chip, layer, to):
            r = outs[t].at[layer, 2 * chip[0] + chip[1]]
            return pltpu.make_async_remote_copy(src_ref=r, dst_ref=r, send_sem=send_sems.at[6 * t + k],
                                                recv_sem=recv_sems.at[6 * t + k], device_id=to, device_id_type=MESH)

        first = [copy(t, j, (x, y), c, (*chip, c)) for j, chip in enumerate(chips) for t in range(n)]
        for cp in first:
            cp.start()
        passed = []
        for j, chip in enumerate(chips):
            for t in range(n):
                copy(t, j, chip, c, (x, y, c)).wait_recv()
                passed.append(copy(t, 3 + j, chip, c, sib))
                passed[-1].start()
        for j, chip in enumerate(chips):
            for t in range(n):
                copy(t, 3 + j, chip, 1 - c, (x, y, c)).wait_recv()
        for cp in first + passed:
            cp.wait_send()

    return pl.pallas_call(body, name=name, out_shape=[SDS(a.shape, a.dtype) for a in arrs],
                          in_specs=[ANY] * n, out_specs=[ANY] * n, input_output_aliases={t: t for t in range(n)},
                          scratch_shapes=[pltpu.SemaphoreType.DMA((6 * n,)), pltpu.SemaphoreType.DMA((6 * n,))],
                          compiler_params=COMM_PARAMS)(*arrs)


def _swap_layers(gs, name):
    n = len(gs)

    def body(*refs):
        ins, outs, send_sems, recv_sems = refs[:n], refs[n:2 * n], refs[2 * n], refs[2 * n + 1]
        x, y, c = _me()
        cps = [pltpu.make_async_remote_copy(src_ref=ins[t].at[1 - c], dst_ref=outs[t], send_sem=send_sems.at[t],
                                            recv_sem=recv_sems.at[t], device_id=(x, y, 1 - c), device_id_type=MESH)
               for t in range(n)]
        for cp in cps:
            cp.start()
        for cp in cps:
            cp.wait()

    return pl.pallas_call(body, name=name, out_shape=[SDS(g.shape[1:], g.dtype) for g in gs],
                          in_specs=[ANY] * n, out_specs=[ANY] * n,
                          scratch_shapes=[pltpu.SemaphoreType.DMA((n,)), pltpu.SemaphoreType.DMA((n,))],
                          compiler_params=COMM_PARAMS)(*gs)


def _exchange_chips(ts, name):
    n = len(ts)

    def body(*refs):
        ins, outs, send_sems, recv_sems = refs[:n], refs[n:2 * n], refs[2 * n], refs[2 * n + 1]
        x, y, c = _me()
        cps = [pltpu.make_async_remote_copy(src_ref=ins[t].at[2 * cx + cy], dst_ref=outs[t].at[j],
                                            send_sem=send_sems.at[3 * t + j], recv_sem=recv_sems.at[3 * t + j],
                                            device_id=(cx, cy, c), device_id_type=MESH)
               for j, (cx, cy) in enumerate(_other_chips(x, y)) for t in range(n)]
        for cp in cps:
            cp.start()
        for cp in cps:
            cp.wait()

    return pl.pallas_call(body, name=name, out_shape=[SDS((3,) + t.shape[1:], t.dtype) for t in ts],
                          in_specs=[ANY] * n, out_specs=[ANY] * n,
                          scratch_shapes=[pltpu.SemaphoreType.DMA((3 * n,)), pltpu.SemaphoreType.DMA((3 * n,))],
                          compiler_params=COMM_PARAMS)(*ts)


def _join_layers(us, name):
    n = len(us)

    def body(*refs):
        outs, send_sems, recv_sems = refs[n:2 * n], refs[2 * n], refs[2 * n + 1]
        x, y, c = _me()
        cps = [pltpu.make_async_remote_copy(src_ref=outs[t].at[c], dst_ref=outs[t].at[c], send_sem=send_sems.at[t],
                                            recv_sem=recv_sems.at[t], device_id=(x, y, 1 - c), device_id_type=MESH)
               for t in range(n)]
        for cp in cps:
            cp.start()
        for cp in cps:
            cp.wait()

    return pl.pallas_call(body, name=name, out_shape=[SDS(u.shape, u.dtype) for u in us],
                          in_specs=[ANY] * n, out_specs=[ANY] * n, input_output_aliases={t: t for t in range(n)},
                          scratch_shapes=[pltpu.SemaphoreType.DMA((n,)), pltpu.SemaphoreType.DMA((n,))],
                          compiler_params=COMM_PARAMS)(*us)


def _allsum_small(v, name):
    M = v.shape[0]

    def body(x_ref, o_ref, all_ref, send_sems, recv_sems, local_sem):
        x, y, c = _me()
        me, sib = (x, y, c), (x, y, 1 - c)
        chips = _other_chips(x, y)

        def rows(px, py, pc):
            return all_ref.at[pl.ds((4 * px + 2 * py + pc) * M, M), :]

        def copy(k, block, to, src=None):
            return pltpu.make_async_remote_copy(src_ref=rows(*block) if src is None else src, dst_ref=rows(*block),
                                                send_sem=send_sems.at[k], recv_sem=recv_sems.at[k],
                                                device_id=to, device_id_type=MESH)

        mine = pltpu.make_async_copy(x_ref, rows(*me), local_sem)
        mine.start()
        first = [copy(0, me, sib, src=x_ref)]
        first += [copy(1 + j, me, (*chip, c), src=x_ref) for j, chip in enumerate(chips)]
        for cp in first:
            cp.start()
        passed = [copy(4 + j, (*chip, c), sib) for j, chip in enumerate(chips)]
        for j, chip in enumerate(chips):
            copy(1 + j, (*chip, c), me).wait_recv()
            passed[j].start()
        copy(0, sib, me).wait_recv()
        for j, chip in enumerate(chips):
            copy(4 + j, (*chip, 1 - c), me).wait_recv()
        for cp in first + passed:
            cp.wait_send()
        mine.wait()
        acc = all_ref[0:M, :]
        for d in range(1, 8):
            acc = acc + all_ref[d * M:(d + 1) * M, :]
        o_ref[...] = acc

    vm = BS(memory_space=pltpu.VMEM)
    return pl.pallas_call(body, name=name, out_shape=SDS((M, LANES), F32), in_specs=[vm], out_specs=vm,
                          scratch_shapes=[pltpu.VMEM((8 * M, LANES), F32), pltpu.SemaphoreType.DMA((7,)),
                                          pltpu.SemaphoreType.DMA((7,)), pltpu.SemaphoreType.DMA],
                          compiler_params=pltpu.CompilerParams(has_side_effects=True, vmem_limit_bytes=VMEM_LIMIT))(v)


BIG = ("w_in", "w_o", "w_gate_up", "w_down")
TINY = ("w_uq", "w_ukv", "conv_w")
REPL = ("attn_norm", "mla_q_norm", "mla_kv_norm", "pool_w", "pool_scale", "swa_sinks", "mix_norm", "ffn_norm",
        "final_norm")
ORDER = ("attn_norm", "w_in", "mla_q_norm", "w_uq", "mla_kv_norm", "w_ukv", "conv_w", "pool_w", "pool_scale",
         "swa_sinks", "mix_norm", "w_o", "ffn_norm", "w_gate_up", "w_down", "final_norm")


def _rows8(shape):
    return -(-int(np.prod(shape)) // (8 * LANES)) * 8


def _pack(arrs):
    parts = []
    for a in arrs:
        r = _rows8(a.shape)
        parts.append(jnp.pad(a.reshape(-1), (0, r * LANES - a.size)).reshape(r, LANES))
    return jnp.concatenate(parts, axis=0)


def _unpack(buf, shapes):
    out, r0 = [], 0
    for s in shapes:
        n, r = int(np.prod(s)), _rows8(s)
        rows = buf[r0:r0 + r]
        out.append(rows.reshape(s) if n == r * LANES else rows.reshape(-1)[:n].reshape(s))
        r0 += r
    return out


def _cols_joined(g):
    return jnp.transpose(g, (0, 2, 1, 3)).reshape(g.shape[0], g.shape[2], 4 * g.shape[3])


def _cols_split(w):
    n, a, b4 = w.shape
    return jnp.transpose(w.reshape(n, a, 4, b4 // 4), (0, 2, 1, 3))


def _rope_tables(S):
    inv = 1.0 / (10000.0 ** (jnp.arange(0, 32, 2, dtype=F32) / 32))
    ang = jnp.arange(S, dtype=F32)[:, None] * inv[None, :]
    cos, sin = jnp.cos(ang), jnp.sin(ang)
    z = lambda w: jnp.zeros((S, w), F32)
    tc = jnp.concatenate([jnp.ones((S, 64), F32), cos, cos, jnp.ones((S, 32), F32)], axis=1)
    ts1 = jnp.concatenate([z(64), -sin, z(48)], axis=1)
    ts2 = jnp.concatenate([z(80), sin, z(32)], axis=1)
    return tc, ts1, ts2


def _pad_w_in(w):
    z = lambda n: jnp.zeros(w.shape[:-1] + (n,), w.dtype)
    return jnp.concatenate([w[..., 0:384], z(64), w[..., 384:416], z(32), w[..., 416:1952]], axis=-1)


def _unpad_w_in(d):
    return jnp.concatenate([d[..., 0:384], d[..., 448:480], d[..., 512:2048]], axis=-1)


def _pad_heads(w, src, offs):
    cols = []
    for h in range(HEADS):
        src0, n = src[h]
        z = lambda k: jnp.zeros(w.shape[:-1] + (k,), w.dtype)
        cols += [z(offs[h]), w[..., src0:src0 + n], z(128 - offs[h] - n)]
    return jnp.concatenate(cols, axis=-1)


UQ_SRC = [(h * 96, 96) for h in range(HEADS)]
KN_SRC = [(h * 128, 64) for h in range(HEADS)]
V_SRC = [(h * 128 + 64, 64) for h in range(HEADS)]
ZERO_OFF = [0] * HEADS
V_OFF = [(h % 2) * 64 for h in range(HEADS)]


def _unpad_heads(d, src, offs):
    return [d[..., h * 128 + offs[h]: h * 128 + offs[h] + src[h][1]] for h in range(HEADS)]


def kernel(x, attn_norm, w_in, mla_q_norm, w_uq, mla_kv_norm, w_ukv, conv_w, pool_w, pool_scale, swa_sinks, mix_norm, w_o, ffn_norm, w_gate_up, w_down, final_norm, loss_target, m_attn_norm, m_w_in, m_mla_q_norm, m_w_uq, m_mla_kv_norm, m_w_ukv, m_conv_w, m_pool_w, m_pool_scale, m_swa_sinks, m_mix_norm, m_w_o, m_ffn_norm, m_w_gate_up, m_w_down, m_final_norm, v_attn_norm, v_w_in, v_mla_q_norm, v_w_uq, v_mla_kv_norm, v_w_ukv, v_conv_w, v_pool_w, v_pool_scale, v_swa_sinks, v_mix_norm, v_w_o, v_ffn_norm, v_w_gate_up, v_w_down, v_final_norm):
    W = dict(attn_norm=attn_norm, w_in=w_in, mla_q_norm=mla_q_norm, w_uq=w_uq, mla_kv_norm=mla_kv_norm, w_ukv=w_ukv,
             conv_w=conv_w, pool_w=pool_w, pool_scale=pool_scale, swa_sinks=swa_sinks, mix_norm=mix_norm, w_o=w_o,
             ffn_norm=ffn_norm, w_gate_up=w_gate_up, w_down=w_down, final_norm=final_norm)
    M1 = dict(attn_norm=m_attn_norm, w_in=m_w_in, mla_q_norm=m_mla_q_norm, w_uq=m_w_uq, mla_kv_norm=m_mla_kv_norm,
              w_ukv=m_w_ukv, conv_w=m_conv_w, pool_w=m_pool_w, pool_scale=m_pool_scale, swa_sinks=m_swa_sinks,
              mix_norm=m_mix_norm, w_o=m_w_o, ffn_norm=m_ffn_norm, w_gate_up=m_w_gate_up, w_down=m_w_down,
              final_norm=m_final_norm)
    V2 = dict(attn_norm=v_attn_norm, w_in=v_w_in, mla_q_norm=v_mla_q_norm, w_uq=v_w_uq, mla_kv_norm=v_mla_kv_norm,
              w_ukv=v_w_ukv, conv_w=v_conv_w, pool_w=v_pool_w, pool_scale=v_pool_scale, swa_sinks=v_swa_sinks,
              mix_norm=v_mix_norm, w_o=v_w_o, ffn_norm=v_ffn_norm, w_gate_up=v_w_gate_up, w_down=v_w_down,
              final_norm=v_final_norm)
    S = x.shape[1]
    xc, yc, cc = _me()
    chip = 2 * xc + yc
    kc = jnp.stack([chip, cc]).astype(jnp.int32)

    names = ("w_in", "w_uq", "w_ukv", "w_o", "w_gate_up", "w_down", "conv_w")
    placed = [_place(W[n], kc, F32 if n == "conv_w" else BF16, f"place_{n}") for n in names]
    gi, gq, gkv, go, gu4, gd, gcv = _gather(placed, "gather_weights")
    win_p = _pad_w_in(_cols_joined(gi))
    wuq_p = _pad_heads(_cols_joined(gq), UQ_SRC, ZERO_OFF)
    wukv = _cols_joined(gkv)
    wk_p = _pad_heads(wukv, KN_SRC, ZERO_OFF)
    wv_p = _pad_heads(wukv, V_SRC, V_OFF)
    conv8 = jnp.pad(_cols_joined(gcv), ((0, 0), (0, 5), (0, 0)))
    wo = go.reshape(2, D, D)
    wdown = gd.reshape(2, D_FF, D)
    pwd = jnp.concatenate([jnp.concatenate(
        [jnp.pad(pool_w[:, 2 * b], ((0, 0), (0, 0), (0, 64))), jnp.pad(pool_w[:, 2 * b + 1], ((0, 0), (0, 0), (64, 0)))],
        axis=1) for b in range(2)], axis=1).astype(BF16)
    tabs = _rope_tables(S)
    g_attn, g_q, g_kv, g_mix, g_ffn, g_ps = (_g3(W[n]) for n in ("attn_norm", "mla_q_norm", "mla_kv_norm", "mix_norm",
                                                                  "ffn_norm", "pool_scale"))

    xs = [x[0]]
    saved = []
    for l in range(DEPTH):
        x0 = xs[-1]
        proj, h = _norm_mm(x0, g_attn, l, win_p, _wspec_in(l), D_INP, 1024, F32, f"in_proj{l}")
        q, k, v, kt, vt = _mla_prep(proj, g_q, g_kv, wuq_p, wk_p, wv_p, tabs, l, f"mla_prep{l}")
        ya, lse = _mla_attn(q, k, vt, f"mla_attn{l}")
        yb = _conv(proj, conv8, l, f"conv{l}")
        ycp = _pool(proj, pwd, g_ps, l, f"pool{l}")
        yd = _swa(proj, swa_sinks, l, f"swa{l}")
        x1, ycat, mixed = _mix_out(x0, ya, yb, ycp, yd, g_mix, wo, l, f"mix_out{l}")
        gu, h2 = _norm_mm(x1, g_ffn, l, gu4, _wspec_gu(l), 2 * D_FF, 2 * D_FF // 4, BF16, f"gate_up{l}")
        x2, act = _swiglu_mm_res(x1, gu, wdown, l, f"down{l}")
        saved.append(dict(x0=x0, proj=proj, h=h, q=q, k=k, kt=kt, v=v, lse=lse, x1=x1, ycat=ycat, mixed=mixed,
                          gu=gu, h2=h2, act=act))
        xs.append(x2)

    dx, dx16, dg_final, loss_tile = _loss_head(xs[-1], final_norm.reshape(1, D), loss_target[0], "loss_head")
    loss = lax.psum(loss_tile[0, 0] * (0.5 / D), ("x", "y", "c"))

    G = {n: [None] * DEPTH for n in TINY + REPL if n != "final_norm"}
    gw_in = gw_o = gw_gu = gw_down = None
    for l in reversed(range(DEPTH)):
        sv = saved[l]
        dgu = _bwd_down(dx16, wdown, sv["gu"], l, f"down_bwd{l}")
        gw_down = _mm_tn(sv["act"], dx16, l, gw_down, f"dw_down{l}")
        gw_gu = _mm_tn(sv["h2"], dgu, l, gw_gu, f"dw_gate_up{l}", split4=True)
        dx1, dx1_16, dg = _mm_nt_normbwd(dgu, gu4, l, sv["x1"], g_ffn, dx, 1, f"gate_up_bwd{l}")
        G["ffn_norm"][l] = dg[0]
        gw_o = _mm_tn(sv["mixed"], dx1_16, l, gw_o, f"dw_o{l}")
        dycat, dg = _mm_nt_normbwd(dx1_16, wo.reshape(2, 1, D, D), l, sv["ycat"], g_mix, None, 4, f"mix_bwd{l}")
        G["mix_norm"][l] = dg[0]

        proj = sv["proj"]
        delta = _mla_delta(dycat, sv["ycat"], f"mla_delta{l}")
        dq, dk, dv = _mla_attn_bwd(sv["q"], sv["k"], sv["kt"], sv["v"], dycat, sv["lse"], delta, f"mla_attn_bwd{l}")
        dcq, dckv, dkr, dwuq, dwk, dwv, dgq, dgkv = _mla_prep_bwd(
            dq, dk, dv, proj, g_q, g_kv, wuq_p, wk_p, wv_p, tabs, l, f"mla_prep_bwd{l}")
        dgb, dgc, duc, dcw = _conv_bwd(proj, conv8, dycat, l, f"conv_bwd{l}")
        dup, dpw, dps = _pool_bwd(proj, pwd, g_ps, dycat, l, f"pool_bwd{l}")
        dqs, dks, dvs, dsink = _swa_bwd(proj, swa_sinks, dycat, l, f"swa_bwd{l}")
        dproj = jnp.concatenate([dcq, dckv, dkr, dgb, dgc, duc, dup, dqs, dks, dvs], axis=1)
        gw_in = _mm_tn(sv["h"], dproj, l, gw_in, f"dw_in{l}")
        dx, dx16, dg = _mm_nt_normbwd(dproj, win_p.reshape(2, 1, D, D_INP), l, sv["x0"], g_attn, dx1, 1, f"in_proj_bwd{l}")
        G["attn_norm"][l] = dg[0]
        G["mla_q_norm"][l] = dgq[0]
        G["mla_kv_norm"][l] = dgkv[0]
        G["w_uq"][l] = jnp.concatenate(_unpad_heads(dwuq, UQ_SRC, ZERO_OFF), axis=1)
        kn, vv = _unpad_heads(dwk, KN_SRC, ZERO_OFF), _unpad_heads(dwv, V_SRC, V_OFF)
        G["w_ukv"][l] = jnp.concatenate([t for h in range(HEADS) for t in (kn[h], vv[h])], axis=1)
        G["conv_w"][l] = dcw[0:3]
        G["pool_w"][l] = jnp.stack([dpw[0:64, 0:64], dpw[64:128, 64:128], dpw[128:192, 0:64], dpw[192:256, 64:128]])
        G["pool_scale"][l] = dps[0]
        G["swa_sinks"][l] = dsink[0, 0:4]
    grad_x = dx[None]
    Gl = {n: jnp.stack(G[n]) for n in G}
    Gl["final_norm"] = dg_final[0]

    gbig = [_cols_split(_unpad_w_in(gw_in)), gw_o.reshape(2, 4, D // 4, D), gw_gu, gw_down.reshape(2, 4, D_FF // 4, D)]
    got = _swap_layers(gbig, "rs_swap_cores")
    pairs = [_pair_sum(g, o, kc, f"rs_pair_sum_{n}") for g, o, n in zip(gbig, got, BIG)]
    got3 = _exchange_chips([p[1] for p in pairs], "rs_exchange_chips")
    us = [_chip_sum(p[0], o3, kc, f"rs_chip_sum_{n}") for p, o3, n in zip(pairs, got3, BIG)]
    gsum = _join_layers(us, "rs_join_cores")
    res = {}
    for n, g in zip(BIG, gsum):
        d_, m_, v_ = _adamw(W[n], g, M1[n], V2[n], f"adamw_{n}")
        res["g", n], res["d", n], res["m", n], res["v", n] = g, d_, m_, v_

    small = TINY + REPL
    full_shapes = [Gl[n].shape for n in small]
    summed = _unpack(_allsum_small(_pack([Gl[n] for n in small]), "allsum_small"), full_shapes)
    gs = {}
    for n, g in zip(small, summed):
        if n in TINY:
            wdt = W[n].shape[2]
            g = lax.dynamic_slice_in_dim(g, chip * wdt, wdt, axis=2)
        gs[n] = g
    own_shapes = [W[n].shape for n in small]
    pk = lambda src: _pack([src[n] for n in small])[None]
    d_s, m_s, v_s = _adamw(pk(W), pk(gs), pk(M1), pk(V2), "adamw_small")
    for key, buf in (("d", d_s), ("m", m_s), ("v", v_s)):
        for n, a in zip(small, _unpack(buf[0], own_shapes)):
            res[key, n] = a
    for n in small:
        res["g", n] = gs[n]

    return (loss, grad_x, *[res["g", n] for n in ORDER], *[res["d", n] for n in ORDER],
            *[res["m", n] for n in ORDER], *[res["v", n] for n in ORDER])
```

```python
import math

import numpy as np
import jax
import jax.numpy as jnp
from jax import lax
from jax.experimental import pallas as pl
from jax.experimental.pallas import tpu as pltpu

F32, BF16 = jnp.float32, jnp.bfloat16
SDS = jax.ShapeDtypeStruct
BS = pl.BlockSpec
MESH = pl.DeviceIdType.MESH

D = 1024
DEPTH = 2
HEADS = 4
D_FF = 2816
D_INP = 2048
EPS = 1e-6
SWA_WINDOW = 128
BLK = 128
SLOPES = tuple(2.0 ** (-8.0 * (i + 1) / 4) for i in range(4))
QK_SCALE = 1.0 / math.sqrt(96)
SWA_SCALE = 1.0 / math.sqrt(64)
LR, B1, B2, ADAM_EPS, WD, STEP = 0.001, 0.9, 0.999, 1e-08, 0.01, 10

LANES = 1024
VMEM_LIMIT = 56 * 1024 * 1024
NEG_INF = float("-inf")

C_CQ, C_CKV, C_KR, C_GB, C_GC, C_UC, C_UP, C_QS, C_KS, C_VS = 0, 256, 384, 512, 768, 1024, 1280, 1536, 1792, 1920


def _params(ngrid):
    return pltpu.CompilerParams(dimension_semantics=("arbitrary",) * ngrid, vmem_limit_bytes=VMEM_LIMIT)


def _pc(body, *, name, grid, in_specs, out_specs, out_shape, scratch=(), aliases=None):
    return pl.pallas_call(
        body, name=name, grid=grid, in_specs=in_specs, out_specs=out_specs, out_shape=out_shape,
        scratch_shapes=scratch, input_output_aliases=aliases or {}, compiler_params=_params(len(grid)))


def _dot(a, b):
    return jnp.dot(a, b, preferred_element_type=F32)


def _dot_nt(a, b):
    return lax.dot_general(a, b, (((1,), (1,)), ((), ())), preferred_element_type=F32)


def _dot_tn(a, b):
    return lax.dot_general(a, b, (((0,), (0,)), ((), ())), preferred_element_type=F32)


def _tile(n, cap):
    if n <= cap:
        return n
    t = cap - cap % 128
    while n % t:
        t -= 128
    return t


def _row_tile(a, b, cap=262144):
    bp = -(-b // 128) * 128
    best = None
    for t in range(8, a + 1, 8):
        if a % t == 0 and t * bp <= cap:
            best = t
    return best if best is not None else a


def _g3(a):
    return a.reshape(a.shape[0], 1, a.shape[1])


def _norm_mm(x, g3, l, w, wspec, N, tn, out_dtype, name):
    S, K = x.shape
    tm = min(512, S)

    def body(x_ref, g_ref, w_ref, y_ref, h_ref):
        @pl.when(pl.program_id(1) == 0)
        def _():
            xv = x_ref[...]
            r = lax.rsqrt(jnp.mean(xv * xv, axis=-1, keepdims=True) + EPS)
            h_ref[...] = (xv * r * g_ref[...]).astype(BF16)

        y_ref[...] = _dot(h_ref[...], w_ref[...]).astype(out_dtype)

    return _pc(body, name=name, grid=(S // tm, N // tn),
               in_specs=[BS((tm, K), lambda i, j: (i, 0)), BS((None, 1, K), lambda i, j: (l, 0, 0)), wspec],
               out_specs=[BS((tm, tn), lambda i, j: (i, j)), BS((tm, K), lambda i, j: (i, 0))],
               out_shape=[SDS((S, N), out_dtype), SDS((S, K), BF16)])(x, g3, w)


def _wspec_in(l):
    return BS((None, D, D_INP), lambda i, j: (l, 0, j))


def _wspec_gu(l):
    return BS((None, None, D, 2 * D_FF // 4), lambda i, j: (l, j, 0, 0))


def _mix_out(x0, ya, yb, yc, yd, gmix3, wo, l, name):
    S = x0.shape[0]
    tm = min(512, S)

    def body(x_ref, ya_ref, yb_ref, yc_ref, yd_ref, g_ref, w_ref, x1_ref, ycat_ref, mixed_ref):
        groups = [ya_ref[...], yb_ref[...], yc_ref[...], yd_ref[...]]
        for gi, yg in enumerate(groups):
            sl = slice(gi * 256, (gi + 1) * 256)
            r = lax.rsqrt(jnp.mean(yg * yg, axis=-1, keepdims=True) + EPS)
            ycat_ref[:, sl] = yg
            mixed_ref[:, sl] = (yg * r * g_ref[:, sl]).astype(BF16)
        x1_ref[...] = x_ref[...] + _dot(mixed_ref[...], w_ref[...])

    row = lambda w: BS((tm, w), lambda i: (i, 0))
    return _pc(body, name=name, grid=(S // tm,),
               in_specs=[row(D), row(256), row(256), row(256), row(256), BS((None, 1, D), lambda i: (l, 0, 0)),
                         BS((None, D, D), lambda i: (l, 0, 0))],
               out_specs=[row(D), row(D), row(D)],
               out_shape=[SDS((S, D), F32), SDS((S, D), F32), SDS((S, D), BF16)])(x0, ya, yb, yc, yd, gmix3, wo)


def _swiglu_mm_res(x1, gu, wdown, l, name):
    S = x1.shape[0]
    tm = min(256, S)

    def body(x_ref, gate_ref, up_ref, w_ref, x2_ref, act_ref):
        gt = gate_ref[...].astype(F32)
        act = (gt * pl.reciprocal(1.0 + jnp.exp(-gt), approx=True) * up_ref[...].astype(F32)).astype(BF16)
        act_ref[...] = act
        x2_ref[...] = x_ref[...] + _dot(act, w_ref[...])

    return _pc(body, name=name, grid=(S // tm,),
               in_specs=[BS((tm, D), lambda i: (i, 0)), BS((tm, D_FF), lambda i: (i, 0)),
                         BS((tm, D_FF), lambda i: (i, 1)), BS((None, D_FF, D), lambda i: (l, 0, 0))],
               out_specs=[BS((tm, D), lambda i: (i, 0)), BS((tm, D_FF), lambda i: (i, 0))],
               out_shape=[SDS((S, D), F32), SDS((S, D_FF), BF16)])(x1, gu, gu, wdown)


def _loss_head(x, g, tgt, name):
    S = x.shape[0]
    tm = min(512, S)

    def body(x_ref, g_ref, t_ref, dx_ref, dx16_ref, dg_ref, loss_ref):
        @pl.when(pl.program_id(0) == 0)
        def _():
            dg_ref[...] = jnp.zeros_like(dg_ref)
            loss_ref[...] = jnp.zeros_like(loss_ref)

        xv = x_ref[...]
        r = lax.rsqrt(jnp.mean(xv * xv, axis=-1, keepdims=True) + EPS)
        xh = xv * r
        gv = g_ref[...]
        diff = xh * gv - t_ref[...]
        loss_ref[...] += jnp.sum(diff * diff)
        dy = diff * (1.0 / D)
        dg_ref[...] += jnp.sum(dy * xh, axis=0, keepdims=True)
        dxh = dy * gv
        dx = r * (dxh - xh * jnp.mean(dxh * xh, axis=-1, keepdims=True))
        dx_ref[...] = dx
        dx16_ref[...] = dx.astype(BF16)

    row = BS((tm, D), lambda i: (i, 0))
    return _pc(body, name=name, grid=(S // tm,),
               in_specs=[row, BS((1, D), lambda i: (0, 0)), row],
               out_specs=[row, row, BS((8, D), lambda i: (0, 0)), BS((8, 128), lambda i: (0, 0))],
               out_shape=[SDS((S, D), F32), SDS((S, D), BF16), SDS((8, D), F32), SDS((8, 128), F32)])(x, g, tgt)


def _mm_tn(a, b, l, prev, name, split4=False):
    S, Ka = a.shape
    N = b.shape[1]
    if split4:
        ta, tn = _tile(Ka, 256), N // 4
        out_shape = SDS((2, 4, Ka, tn), F32)
        out_spec = BS((None, None, ta, tn), lambda j, i: (l, j, i, 0))
    else:
        ta, tn = _tile(Ka, 512), _tile(N, 1024)
        out_shape = SDS((2, Ka, N), F32)
        out_spec = BS((None, ta, tn), lambda j, i: (l, i, j))

    def body(a_ref, b_ref, *rest):
        rest[-1][...] = _dot_tn(a_ref[...], b_ref[...])

    in_specs = [BS((S, ta), lambda j, i: (0, i)), BS((S, tn), lambda j, i: (0, j))]
    args = [a, b]
    if prev is not None:
        in_specs.append(BS(memory_space=pl.ANY))
        args.append(prev)
    return _pc(body, name=name, grid=(N // tn, Ka // ta), in_specs=in_specs, out_specs=out_spec, out_shape=out_shape,
               aliases={2: 0} if prev is not None else None)(*args)


def _bwd_down(dx16, wdown, gu, l, name):
    S = dx16.shape[0]
    tm = min(256, S)

    def body(dx_ref, w_ref, gate_ref, up_ref, dgu_ref):
        dact = _dot_nt(dx_ref[...], w_ref[...])
        gt = gate_ref[...].astype(F32)
        sg = pl.reciprocal(1.0 + jnp.exp(-gt), approx=True)
        dgu_ref[:, 0:D_FF] = (dact * up_ref[...].astype(F32) * (sg * (1.0 + gt * (1.0 - sg)))).astype(BF16)
        dgu_ref[:, D_FF:2 * D_FF] = (dact * (gt * sg)).astype(BF16)

    return _pc(body, name=name, grid=(S // tm,),
               in_specs=[BS((tm, D), lambda i: (i, 0)), BS((None, D_FF, D), lambda i: (l, 0, 0)),
                         BS((tm, D_FF), lambda i: (i, 0)), BS((tm, D_FF), lambda i: (i, 1))],
               out_specs=BS((tm, 2 * D_FF), lambda i: (i, 0)),
               out_shape=SDS((S, 2 * D_FF), BF16))(dx16, wdown, gu, gu)


def _mm_nt_normbwd(dy, w4, l, x, g3, dres, ngroups, name):
    S, K = dy.shape
    _, nk, _, kc = w4.shape
    tm = min(512, S)
    gw = D // ngroups
    has_res = dres is not None

    def body(*refs):
        if has_res:
            dy_ref, w_ref, x_ref, g_ref, res_ref, dx_ref, dx16_ref, dg_ref = refs
        else:
            dy_ref, w_ref, x_ref, g_ref, dx_ref, dg_ref = refs

        @pl.when(pl.program_id(0) == 0)
        def _():
            dg_ref[...] = jnp.zeros_like(dg_ref)

        dh = _dot_nt(dy_ref[:, 0:kc], w_ref[0])
        for k in range(1, nk):
            dh = dh + _dot_nt(dy_ref[:, k * kc:(k + 1) * kc], w_ref[k])
        for gi in range(ngroups):
            sl = slice(gi * gw, (gi + 1) * gw)
            xg = x_ref[:, sl]
            r = lax.rsqrt(jnp.mean(xg * xg, axis=-1, keepdims=True) + EPS)
            xh = xg * r
            dhg = dh[:, sl]
            dg_ref[:, sl] += jnp.sum(dhg * xh, axis=0, keepdims=True)
            dxh = dhg * g_ref[:, sl]
            dxg = r * (dxh - xh * jnp.mean(dxh * xh, axis=-1, keepdims=True))
            if has_res:
                dxg = dxg + res_ref[:, sl]
                dx16_ref[:, sl] = dxg.astype(BF16)
            dx_ref[:, sl] = dxg

    row = BS((tm, D), lambda i: (i, 0))
    in_specs = [BS((tm, K), lambda i: (i, 0)),
                BS((None, nk, D, kc), lambda i: (l, 0, 0, 0), pipeline_mode=pl.Buffered(1)), row,
                BS((None, 1, D), lambda i: (l, 0, 0))]
    args = [dy, w4, x, g3]
    out_specs, out_shape = [row], [SDS((S, D), F32)]
    if has_res:
        in_specs.append(row)
        args.append(dres)
        out_specs.append(row)
        out_shape.append(SDS((S, D), BF16))
    out_specs.append(BS((8, D), lambda i: (0, 0)))
    out_shape.append(SDS((8, D), F32))
    return _pc(body, name=name, grid=(S // tm,), in_specs=in_specs, out_specs=out_specs, out_shape=out_shape)(*args)


def _rope(x, c, s1, s2):
    return x * c + pltpu.roll(x, 112, axis=1) * s1 + pltpu.roll(x, 16, axis=1) * s2


def _rope_t(dy, c, s1, s2):
    return dy * c + pltpu.roll(dy * s1, 16, axis=1) + pltpu.roll(dy * s2, 112, axis=1)


def _mla_prep(proj, gq3, gkv3, wuq, wk, wv, tabs, l, name):
    S = proj.shape[0]
    tm = min(512, S)
    tc, ts1, ts2 = tabs

    def body(cq_ref, ckv_ref, kr_ref, gq_ref, gkv_ref, wuq_ref, wk_ref, wv_ref, c_ref, s1_ref, s2_ref,
             q_ref, k_ref, v_ref, kt_ref, vt_ref):
        c, s1, s2 = c_ref[...], s1_ref[...], s2_ref[...]
        cq = cq_ref[...]
        rq = lax.rsqrt(jnp.mean(cq * cq, axis=-1, keepdims=True) + EPS)
        qa = _dot((cq * rq * gq_ref[...]).astype(BF16), wuq_ref[...])
        ckv = ckv_ref[...]
        rkv = lax.rsqrt(jnp.mean(ckv * ckv, axis=-1, keepdims=True) + EPS)
        ckvn = (ckv * rkv * gkv_ref[...]).astype(BF16)
        ka = _dot(ckvn, wk_ref[...])
        va = _dot(ckvn, wv_ref[...])
        v_ref[...] = va.astype(BF16)
        vt_ref[...] = va.T.astype(BF16)
        krr = _rope(kr_ref[...], c, s1, s2)
        for h in range(HEADS):
            sl = slice(h * 128, (h + 1) * 128)
            q_ref[:, sl] = (_rope(qa[:, sl], c, s1, s2) * QK_SCALE).astype(BF16)
            kh = ka[:, sl] + krr
            k_ref[:, sl] = kh.astype(BF16)
            kt_ref[sl, :] = kh.T.astype(BF16)

    lay = lambda a, b: BS((None, a, b), lambda i: (l, 0, 0))
    tab = BS((tm, 128), lambda i: (i, 0))
    return _pc(body, name=name, grid=(S // tm,),
               in_specs=[BS((tm, 256), lambda i: (i, 0)), BS((tm, 128), lambda i: (i, 2)), BS((tm, 128), lambda i: (i, 3)),
                         lay(1, 256), lay(1, 128), lay(256, 512), lay(128, 512), lay(128, 512), tab, tab, tab],
               out_specs=[BS((tm, 512), lambda i: (i, 0))] * 3 + [BS((512, tm), lambda i: (0, i))] * 2,
               out_shape=[SDS((S, 512), BF16)] * 3 + [SDS((512, S), BF16)] * 2)(
        proj, proj, proj, gq3, gkv3, wuq, wk, wv, tc, ts1, ts2)


def _mla_prep_bwd(dq, dk, dv, proj, gq3, gkv3, wuq, wk, wv, tabs, l, name):
    S = proj.shape[0]
    tm = min(512, S)
    tc, ts1, ts2 = tabs

    def body(dq_ref, dk_ref, dv_ref, cq_ref, ckv_ref, gq_ref, gkv_ref, wuq_ref, wk_ref, wv_ref, c_ref, s1_ref, s2_ref,
             dcq_ref, dckv_ref, dkr_ref, dwuq_ref, dwk_ref, dwv_ref, dgq_ref, dgkv_ref):
        @pl.when(pl.program_id(0) == 0)
        def _():
            for r in (dwuq_ref, dwk_ref, dwv_ref, dgq_ref, dgkv_ref):
                r[...] = jnp.zeros_like(r)

        c, s1, s2 = c_ref[...], s1_ref[...], s2_ref[...]
        dqp = jnp.concatenate(
            [_rope_t(dq_ref[h * 128:(h + 1) * 128, :].T * QK_SCALE, c, s1, s2) for h in range(HEADS)], axis=1).astype(BF16)
        cq = cq_ref[...]
        rq = lax.rsqrt(jnp.mean(cq * cq, axis=-1, keepdims=True) + EPS)
        cqh = cq * rq
        gq_v = gq_ref[...]
        dwuq_ref[...] += _dot_tn((cqh * gq_v).astype(BF16), dqp)
        dcqn = _dot_nt(dqp, wuq_ref[...])
        dgq_ref[...] += jnp.sum(dcqn * cqh, axis=0, keepdims=True)
        dxh = dcqn * gq_v
        dcq_ref[...] = (rq * (dxh - cqh * jnp.mean(dxh * cqh, axis=-1, keepdims=True))).astype(BF16)

        dkb = dk_ref[...].astype(BF16)
        dvb = dv_ref[...].astype(BF16)
        ckv = ckv_ref[...]
        rkv = lax.rsqrt(jnp.mean(ckv * ckv, axis=-1, keepdims=True) + EPS)
        ckh = ckv * rkv
        gkv_v = gkv_ref[...]
        ckvn = (ckh * gkv_v).astype(BF16)
        dwk_ref[...] += _dot_tn(ckvn, dkb)
        dwv_ref[...] += _dot_tn(ckvn, dvb)
        dckvn = _dot_nt(dkb, wk_ref[...]) + _dot_nt(dvb, wv_ref[...])
        dgkv_ref[...] += jnp.sum(dckvn * ckh, axis=0, keepdims=True)
        dyh = dckvn * gkv_v
        dckv_ref[...] = (rkv * (dyh - ckh * jnp.mean(dyh * ckh, axis=-1, keepdims=True))).astype(BF16)
        dks = dk_ref[:, 0:128] + dk_ref[:, 128:256] + dk_ref[:, 256:384] + dk_ref[:, 384:512]
        dkr_ref[...] = _rope_t(dks, c, s1, s2).astype(BF16)

    full = lambda a, b: BS((a, b), lambda i: (0, 0))
    lay = lambda a, b: BS((None, a, b), lambda i: (l, 0, 0))
    tab = BS((tm, 128), lambda i: (i, 0))
    row = lambda w: BS((tm, w), lambda i: (i, 0))
    return _pc(body, name=name, grid=(S // tm,),
               in_specs=[BS((512, tm), lambda i: (0, i)), row(512), row(512), BS((tm, 256), lambda i: (i, 0)),
                         BS((tm, 128), lambda i: (i, 2)),
                         lay(1, 256), lay(1, 128), lay(256, 512), lay(128, 512), lay(128, 512), tab, tab, tab],
               out_specs=[row(256), row(128), row(128), full(256, 512), full(128, 512), full(128, 512),
                          full(8, 256), full(8, 128)],
               out_shape=[SDS((S, 256), BF16), SDS((S, 128), BF16), SDS((S, 128), BF16), SDS((256, 512), F32),
                          SDS((128, 512), F32), SDS((128, 512), F32), SDS((8, 256), F32), SDS((8, 128), F32)])(
        dq, dk, dv, proj, proj, gq3, gkv3, wuq, wk, wv, tc, ts1, ts2)


def _mla_attn(q, k, vt, name):
    S = q.shape[0]
    t = min(512, S)
    n = S // t

    def body(q_ref, k_ref, vt_ref, ya_ref, lse_ref, m_sc, l_sc, acc_sc):
        i, j = pl.program_id(1), pl.program_id(2)

        @pl.when(j == 0)
        def _():
            m_sc[...] = jnp.full_like(m_sc, NEG_INF)
            l_sc[...] = jnp.zeros_like(l_sc)
            acc_sc[...] = jnp.zeros_like(acc_sc)

        def step(masked):
            for hh in range(2):
                sl = slice(hh * 128, (hh + 1) * 128)
                st = _dot_nt(k_ref[:, sl], q_ref[:, sl])
                if masked:
                    key = lax.broadcasted_iota(jnp.int32, (t, t), 0)
                    qry = lax.broadcasted_iota(jnp.int32, (t, t), 1)
                    st = jnp.where(key <= qry, st, NEG_INF)
                m_prev = m_sc[hh]
                m_new = jnp.maximum(m_prev, jnp.max(st, axis=0, keepdims=True))
                p = jnp.exp(st - m_new)
                alpha = jnp.exp(m_prev - m_new)
                l_sc[hh] = alpha * l_sc[hh] + jnp.sum(p, axis=0, keepdims=True)
                acc_sc[hh] = alpha * acc_sc[hh] + _dot(vt_ref[sl, :], p.astype(BF16))
                m_sc[hh] = m_new

        @pl.when(j < i)
        def _():
            step(False)

        @pl.when(j == i)
        def _():
            step(True)
            ya_ref[...] = (acc_sc[0] / l_sc[0] + acc_sc[1] / l_sc[1]).T
            for hh in range(2):
                lse_ref[hh] = m_sc[hh] + jnp.log(l_sc[hh])

    return _pc(body, name=name, grid=(2, n, n),
               in_specs=[BS((t, 256), lambda p, i, j: (i, p)), BS((t, 256), lambda p, i, j: (jnp.minimum(j, i), p)),
                         BS((256, t), lambda p, i, j: (p, jnp.minimum(j, i)))],
               out_specs=[BS((t, 128), lambda p, i, j: (i, p)), BS((2, 1, t), lambda p, i, j: (p, 0, i))],
               out_shape=[SDS((S, 256), F32), SDS((HEADS, 1, S), F32)],
               scratch=[pltpu.VMEM((2, 1, t), F32), pltpu.VMEM((2, 1, t), F32), pltpu.VMEM((2, 128, t), F32)])(q, k, vt)


def _mla_delta(dycat, ya, name):
    S = ya.shape[0]
    t = min(512, S)

    def body(do_ref, ya_ref, d_ref):
        prod = do_ref[...] * ya_ref[...]
        for p in range(2):
            pt = prod[:, p * 128:(p + 1) * 128].T
            d_ref[2 * p] = jnp.sum(pt[0:64, :], axis=0, keepdims=True)
            d_ref[2 * p + 1] = jnp.sum(pt[64:128, :], axis=0, keepdims=True)

    return _pc(body, name=name, grid=(S // t,),
               in_specs=[BS((t, 256), lambda i: (i, 0)), BS((t, 256), lambda i: (i, 0))],
               out_specs=BS((HEADS, 1, t), lambda i: (0, 0, i)), out_shape=SDS((HEADS, 1, S), F32))(dycat, ya)


def _mla_attn_bwd(q, k, kt, v, dya, lse, delta, name):
    S = q.shape[0]
    t = min(512, S)
    n = S // t

    def body(q_ref, k_ref, kt_ref, v_ref, do_ref, lse_ref, delta_ref, dqt_ref, dk_ref, dv_ref):
        j, i = pl.program_id(1), pl.program_id(2)

        @pl.when((j == 0) & (i == 0))
        def _():
            dqt_ref[...] = jnp.zeros_like(dqt_ref)

        @pl.when(i == 0)
        def _():
            dk_ref[...] = jnp.zeros_like(dk_ref)
            dv_ref[...] = jnp.zeros_like(dv_ref)

        def step(masked):
            qv = q_ref[...]
            p = jnp.exp(_dot_nt(k_ref[...], qv) - lse_ref[...])
            if masked:
                key = lax.broadcasted_iota(jnp.int32, (t, t), 0)
                qry = lax.broadcasted_iota(jnp.int32, (t, t), 1)
                p = jnp.where(key <= qry, p, 0.0)
            dob = do_ref[...].astype(BF16)
            dv_ref[...] += _dot(p.astype(BF16), dob)
            ds = (p * (_dot_nt(v_ref[...], dob) - delta_ref[...])).astype(BF16)
            dk_ref[...] += _dot(ds, qv)
            cols = pl.ds(pl.multiple_of(i * t, t), t)
            dqt_ref[:, cols] += _dot(kt_ref[...], ds)

        @pl.when(i > j)
        def _():
            step(False)

        @pl.when(i == j)
        def _():
            step(True)

    qs = BS((t, 128), lambda h, j, i: (jnp.maximum(i, j), h))
    ks = BS((t, 128), lambda h, j, i: (j, h))
    rowv = BS((None, 1, t), lambda h, j, i: (h, 0, jnp.maximum(i, j)))
    return _pc(body, name=name, grid=(HEADS, n, n),
               in_specs=[qs, ks, BS((128, t), lambda h, j, i: (h, j)), ks,
                         BS((t, 128), lambda h, j, i: (jnp.maximum(i, j), h // 2)), rowv, rowv],
               out_specs=[BS((128, S), lambda h, j, i: (h, 0)), ks, ks],
               out_shape=[SDS((512, S), F32), SDS((S, 512), F32), SDS((S, 512), F32)])(q, k, kt, v, dya, lse, delta)


def _swa_scores(qm, kk, valid, distf, slope, sink):
    sc = _dot_nt(qm, kk) * SWA_SCALE
    sc = jnp.where(valid, sc - slope * distf, NEG_INF)
    m = jnp.maximum(jnp.max(sc, axis=-1, keepdims=True), sink)
    e = jnp.exp(sc - m)
    esink = jnp.exp(sink - m)
    den = jnp.sum(e, axis=-1, keepdims=True) + esink
    return e / den, esink / den


def _swa_masks():
    r = lax.broadcasted_iota(jnp.int32, (BLK, 2 * BLK), 0)
    c = lax.broadcasted_iota(jnp.int32, (BLK, 2 * BLK), 1)
    dist = r + BLK - c
    return (dist >= 0) & (dist < SWA_WINDOW), c >= BLK, dist.astype(F32)


def _to_half(xb, pos, b):
    return xb if pos == b else pltpu.roll(xb, 64, axis=1)


def _swa(proj, sinks, l, name):
    S = proj.shape[0]
    nb = S // BLK

    def body(q_ref, k_ref, v_ref, sink_ref, o_ref, kp, vp):
        kp[0:BLK, :] = jnp.zeros((BLK, 128), BF16)
        vp[0:BLK, :] = jnp.zeros((BLK, 128), BF16)
        kp[BLK:, :] = k_ref[...].astype(BF16)
        vp[BLK:, :] = v_ref[...].astype(BF16)
        lo = lax.broadcasted_iota(jnp.int32, (BLK, 128), 1) < 64
        band, cur, distf = _swa_masks()

        def blk(i, carry):
            st = pl.multiple_of(i * BLK, BLK)
            kk = kp[pl.ds(st, 2 * BLK), :]
            vv = vp[pl.ds(st, 2 * BLK), :]
            valid = band & (cur | (i > 0))
            for b in range(2):
                half = lo if b == 0 else ~lo
                qb = q_ref[pl.ds(st, BLK), b * 128:(b + 1) * 128]
                outs = []
                for pos in range(2):
                    h = 2 * b + pos
                    qm = jnp.where(half, _to_half(qb, pos, b), 0.0).astype(BF16)
                    p, _ = _swa_scores(qm, kk, valid, distf, SLOPES[h], sink_ref[l, h])
                    outs.append(_to_half(_dot(p.astype(BF16), vv), pos, b))
                o_ref[pl.ds(st, BLK), b * 128:(b + 1) * 128] = jnp.where(lo, outs[0], outs[1])
            return carry

        lax.fori_loop(0, nb, blk, 0)

    return _pc(body, name=name, grid=(1,),
               in_specs=[BS((S, 256), lambda i: (0, C_QS // 256)), BS((S, 128), lambda i: (0, C_KS // 128)),
                         BS((S, 128), lambda i: (0, C_VS // 128)), BS(memory_space=pltpu.SMEM)],
               out_specs=BS((S, 256), lambda i: (0, 0)),
               out_shape=SDS((S, 256), F32),
               scratch=[pltpu.VMEM((S + BLK, 128), BF16), pltpu.VMEM((S + BLK, 128), BF16)])(proj, proj, proj, sinks)


def _swa_bwd(proj, sinks, dyd, l, name):
    S = proj.shape[0]
    nb = S // BLK

    def body(q_ref, k_ref, v_ref, sink_ref, do_ref, dq_ref, dk_ref, dv_ref, dsink_ref, kp, vp, dkp, dvp):
        kp[0:BLK, :] = jnp.zeros((BLK, 128), BF16)
        vp[0:BLK, :] = jnp.zeros((BLK, 128), BF16)
        kp[BLK:, :] = k_ref[...].astype(BF16)
        vp[BLK:, :] = v_ref[...].astype(BF16)
        dkp[...] = jnp.zeros_like(dkp)
        dvp[...] = jnp.zeros_like(dvp)
        lo = lax.broadcasted_iota(jnp.int32, (BLK, 128), 1) < 64
        lane8 = lax.broadcasted_iota(jnp.int32, (8, 128), 1)
        band, cur, distf = _swa_masks()

        def blk(i, dsink):
            st = pl.multiple_of(i * BLK, BLK)
            kk = kp[pl.ds(st, 2 * BLK), :]
            vv = vp[pl.ds(st, 2 * BLK), :]
            valid = band & (cur | (i > 0))
            dkk = jnp.zeros((2 * BLK, 128), F32)
            dvv = jnp.zeros((2 * BLK, 128), F32)
            for b in range(2):
                half = lo if b == 0 else ~lo
                qb = q_ref[pl.ds(st, BLK), b * 128:(b + 1) * 128]
                dob = do_ref[pl.ds(st, BLK), b * 128:(b + 1) * 128]
                dqs = []
                for pos in range(2):
                    h = 2 * b + pos
                    qm = jnp.where(half, _to_half(qb, pos, b), 0.0).astype(BF16)
                    dom = jnp.where(half, _to_half(dob, pos, b), 0.0).astype(BF16)
                    p, psink = _swa_scores(qm, kk, valid, distf, SLOPES[h], sink_ref[l, h])
                    dp = _dot_nt(dom, vv)
                    dvv = dvv + _dot_tn(p.astype(BF16), dom)
                    delta = jnp.sum(p * dp, axis=-1, keepdims=True)
                    dsink = dsink + jnp.where(lane8 == h, -jnp.sum(psink * delta), 0.0)
                    dsc = (p * (dp - delta) * SWA_SCALE).astype(BF16)
                    dqs.append(_to_half(_dot(dsc, kk), pos, b))
                    dkk = dkk + _dot_tn(dsc, qm)
                dq_ref[pl.ds(st, BLK), b * 128:(b + 1) * 128] = jnp.where(lo, dqs[0], dqs[1]).astype(BF16)
            dkp[pl.ds(st, 2 * BLK), :] += dkk
            dvp[pl.ds(st, 2 * BLK), :] += dvv
            return dsink

        dsink_ref[...] = lax.fori_loop(0, nb, blk, jnp.zeros((8, 128), F32))
        dk_ref[...] = dkp[BLK:, :].astype(BF16)
        dv_ref[...] = dvp[BLK:, :].astype(BF16)

    return _pc(body, name=name, grid=(1,),
               in_specs=[BS((S, 256), lambda i: (0, C_QS // 256)), BS((S, 128), lambda i: (0, C_KS // 128)),
                         BS((S, 128), lambda i: (0, C_VS // 128)), BS(memory_space=pltpu.SMEM),
                         BS((S, 256), lambda i: (0, 3))],
               out_specs=[BS((S, 256), lambda i: (0, 0)), BS((S, 128), lambda i: (0, 0)), BS((S, 128), lambda i: (0, 0)),
                          BS((8, 128), lambda i: (0, 0))],
               out_shape=[SDS((S, 256), BF16), SDS((S, 128), BF16), SDS((S, 128), BF16), SDS((8, 128), F32)],
               scratch=[pltpu.VMEM((S + BLK, 128), BF16), pltpu.VMEM((S + BLK, 128), BF16),
                        pltpu.VMEM((S + BLK, 128), F32), pltpu.VMEM((S + BLK, 128), F32)])(proj, proj, proj, sinks, dyd)


def _down(x, k, t):
    return jnp.where(t >= k, pltpu.roll(x, k, axis=0), 0.0)


def _up(x, k, t):
    n = x.shape[0]
    return jnp.where(t < n - k, pltpu.roll(x, n - k, axis=0), 0.0)


def _conv(proj, w8, l, name):
    S = proj.shape[0]

    def body(gb_ref, gc_ref, u_ref, w_ref, y_ref):
        t = lax.broadcasted_iota(jnp.int32, (S, 128), 0)
        z = gc_ref[...] * u_ref[...]
        c = w_ref[2:3, :] * z + w_ref[1:2, :] * _down(z, 1, t) + w_ref[0:1, :] * _down(z, 2, t)
        y_ref[...] = gb_ref[...] * c

    col = lambda c0: BS((S, 128), lambda i: (0, c0 // 128 + i))
    return _pc(body, name=name, grid=(2,),
               in_specs=[col(C_GB), col(C_GC), col(C_UC), BS((None, 8, 128), lambda i: (l, 0, i))],
               out_specs=BS((S, 128), lambda i: (0, i)), out_shape=SDS((S, 256), F32))(proj, proj, proj, w8)


def _conv_bwd(proj, w8, dycat, l, name):
    S = proj.shape[0]

    def body(gb_ref, gc_ref, u_ref, w_ref, dy_ref, dgb_ref, dgc_ref, du_ref, dw_ref):
        t = lax.broadcasted_iota(jnp.int32, (S, 128), 0)
        gc, u = gc_ref[...], u_ref[...]
        z = gc * u
        z1, z2 = _down(z, 1, t), _down(z, 2, t)
        w0, w1, w2 = w_ref[0:1, :], w_ref[1:2, :], w_ref[2:3, :]
        dy = dy_ref[...]
        dgb_ref[...] = (dy * (w2 * z + w1 * z1 + w0 * z2)).astype(BF16)
        dc = dy * gb_ref[...]
        dz = w2 * dc + w1 * _up(dc, 1, t) + w0 * _up(dc, 2, t)
        dgc_ref[...] = (dz * u).astype(BF16)
        du_ref[...] = (dz * gc).astype(BF16)
        row = lax.broadcasted_iota(jnp.int32, (8, 128), 0)
        sums = [jnp.sum(dc * zz, axis=0, keepdims=True) for zz in (z2, z1, z)]
        dw_ref[...] = jnp.where(row == 0, sums[0], jnp.where(row == 1, sums[1], jnp.where(row == 2, sums[2], 0.0)))

    col = lambda c0: BS((S, 128), lambda i: (0, c0 // 128 + i))
    out = BS((S, 128), lambda i: (0, i))
    return _pc(body, name=name, grid=(2,),
               in_specs=[col(C_GB), col(C_GC), col(C_UC), BS((None, 8, 128), lambda i: (l, 0, i)), col(256)],
               out_specs=[out, out, out, BS((8, 128), lambda i: (0, i))],
               out_shape=[SDS((S, 256), BF16)] * 3 + [SDS((8, 256), F32)])(proj, proj, proj, w8, dycat)


def _pool_parts(u, t, first):
    lo = lax.broadcasted_iota(jnp.int32, u.shape, 1) < 64
    s2 = u + _down(u, 1, t)
    s4 = s2 + _down(s2, 2, t)
    s8 = s4 + _down(s4, 4, t)
    s16 = s8 + _down(s8, 8, t)
    win = jnp.where(lo, jnp.where(first, s2, s8), jnp.where(first, s4, s16))
    wv = jnp.where(lo, jnp.where(first, 2, 8), jnp.where(first, 4, 16))
    cnt = jnp.minimum(t + 1, wv).astype(F32)
    return win, cnt, lo


def _pool(proj, pwd, scale3, l, name):
    S = proj.shape[0]

    def body(u_ref, pw_ref, sc_ref, y_ref):
        t = lax.broadcasted_iota(jnp.int32, (S, 128), 0)
        u = u_ref[...]
        win, cnt, _ = _pool_parts(u, t, pl.program_id(0) == 0)
        pooled = win / cnt - u
        y_ref[...] = _dot(pooled.astype(BF16), pw_ref[...]) * sc_ref[...]

    return _pc(body, name=name, grid=(2,),
               in_specs=[BS((S, 128), lambda i: (0, C_UP // 128 + i)), BS((None, 128, 128), lambda i: (l, i, 0)),
                         BS((None, 1, 128), lambda i: (l, 0, i))],
               out_specs=BS((S, 128), lambda i: (0, i)), out_shape=SDS((S, 256), F32))(proj, pwd, scale3)


def _pool_bwd(proj, pwd, scale3, dycat, l, name):
    S = proj.shape[0]

    def body(u_ref, pw_ref, sc_ref, dy_ref, du_ref, dpw_ref, dsc_ref):
        t = lax.broadcasted_iota(jnp.int32, (S, 128), 0)
        first = pl.program_id(0) == 0
        u = u_ref[...]
        win, cnt, lo = _pool_parts(u, t, first)
        pooled = (win / cnt - u).astype(BF16)
        pw = pw_ref[...]
        dy = dy_ref[...]
        dsc_ref[...] = jnp.broadcast_to(jnp.sum(dy * _dot(pooled, pw), axis=0, keepdims=True), (8, 128))
        dmb = (dy * sc_ref[...]).astype(BF16)
        dpw_ref[...] = _dot_tn(pooled, dmb)
        dpooled = _dot_nt(dmb, pw)
        a1 = dpooled / cnt
        a2 = a1 + _up(a1, 1, t)
        a4 = a2 + _up(a2, 2, t)
        a8 = a4 + _up(a4, 4, t)
        a16 = a8 + _up(a8, 8, t)
        dwin = jnp.where(lo, jnp.where(first, a2, a8), jnp.where(first, a4, a16))
        du_ref[...] = (dwin - dpooled).astype(BF16)

    return _pc(body, name=name, grid=(2,),
               in_specs=[BS((S, 128), lambda i: (0, C_UP // 128 + i)), BS((None, 128, 128), lambda i: (l, i, 0)),
                         BS((None, 1, 128), lambda i: (l, 0, i)), BS((S, 128), lambda i: (0, 4 + i))],
               out_specs=[BS((S, 128), lambda i: (0, i)), BS((128, 128), lambda i: (i, 0)), BS((8, 128), lambda i: (0, i))],
               out_shape=[SDS((S, 256), BF16), SDS((256, 128), F32), SDS((8, 256), F32)])(proj, pwd, scale3, dycat)


def _adamw(w, g, m, v, name):
    n, a, b = w.shape
    tr = _row_tile(a, b)

    def body(w_ref, g_ref, m_ref, v_ref, d_ref, nm_ref, nv_ref):
        gv = g_ref[...]
        m_new = B1 * m_ref[...] + (1.0 - B1) * gv
        v_new = B2 * v_ref[...] + (1.0 - B2) * (gv * gv)
        m_hat = m_new / (1.0 - B1 ** STEP)
        v_hat = v_new / (1.0 - B2 ** STEP)
        d_ref[...] = -LR * (m_hat / (jnp.sqrt(v_hat) + ADAM_EPS) + WD * w_ref[...])
        nm_ref[...] = m_new
        nv_ref[...] = v_new

    sp = BS((None, tr, b), lambda i, t: (i, t, 0))
    return _pc(body, name=name, grid=(n, a // tr), in_specs=[sp] * 4, out_specs=[sp] * 3,
               out_shape=[SDS((n, a, b), F32)] * 3)(w, g, m, v)


def _prefetch_call(body, name, grid, in_specs, out_specs, out_shape):
    gs = pltpu.PrefetchScalarGridSpec(num_scalar_prefetch=1, grid=grid, in_specs=in_specs, out_specs=out_specs)
    return pl.pallas_call(body, name=name, grid_spec=gs, out_shape=out_shape, compiler_params=_params(len(grid)))


def _place(w, kc, dtype, name):
    _, a, b = w.shape

    def body(kc_ref, w_ref, o_ref):
        o_ref[...] = w_ref[...].astype(dtype)

    return _prefetch_call(body, name, (2,), [BS((None, a, b), lambda l, kc: (l, 0, 0))],
                          BS((None, None, a, b), lambda l, kc: (l, kc[0], 0, 0)), SDS((2, 4, a, b), dtype))(kc, w)


def _pair_sum(g, got, kc, name):
    _, _, a, b = g.shape
    tr = _row_tile(a, b)

    def body(kc_ref, a_ref, b_ref, t32_ref, t16_ref):
        s = a_ref[...] + b_ref[...]
        t32_ref[...] = s
        t16_ref[...] = s.astype(BF16)

    sp = BS((None, tr, b), lambda k, t, kc: (k, t, 0))
    return _prefetch_call(body, name, (4, a // tr),
                          [BS((None, None, tr, b), lambda k, t, kc: (kc[1], k, t, 0)), sp], [sp, sp],
                          [SDS((4, a, b), F32), SDS((4, a, b), BF16)])(kc, g, got)


def _chip_sum(t32, got3, kc, name):
    _, a, b = t32.shape
    tr = _row_tile(a, b)

    def body(kc_ref, a_ref, b_ref, u_ref):
        u_ref[...] = ((a_ref[...] + b_ref[0].astype(F32)) + b_ref[1].astype(F32)) + b_ref[2].astype(F32)

    return _prefetch_call(body, name, (a // tr,),
                          [BS((None, tr, b), lambda t, kc: (kc[0], t, 0)), BS((3, tr, b), lambda t, kc: (0, t, 0))],
                          BS((None, tr, b), lambda t, kc: (kc[1], t, 0)), SDS((2, a, b), F32))(kc, t32, got3)


def _me():
    return lax.axis_index("x"), lax.axis_index("y"), lax.axis_index("c")


def _other_chips(x, y):
    return [(1 - x, y), (x, 1 - y), (1 - x, 1 - y)]


ANY = BS(memory_space=pl.ANY)
COMM_PARAMS = pltpu.CompilerParams(has_side_effects=True)


def _gather(arrs, name):
    n = len(arrs)
    split = [a.shape[2] % 32 == 0 for a in arrs]

    def body(*refs):
        outs, send_sems, recv_sems = refs[n:2 * n], refs[2 * n], refs[2 * n + 1]
        x, y, c = _me()
        me, sib = (x, y, c), (x, y, 1 - c)
        xn, yn, dg = _other_chips(x, y)

        def blk(t, chip, layer, half=None):
            r = outs[t].at[layer, 2 * chip[0] + chip[1]]
            if half is None:
                return r
            rows = arrs[t].shape[2] // 2
            return r.at[pl.ds(half * rows, rows)]

        def copy(t, k, ref, to):
            return pltpu.make_async_remote_copy(src_ref=ref, dst_ref=ref, send_sem=send_sems.at[7 * t + k],
                                                recv_sem=recv_sems.at[7 * t + k], device_id=to, device_id_type=MESH)

        sent = []

        def send(cp):
            cp.start()
            sent.append(cp)

        for t in range(n):
            send(copy(t, 0, blk(t, (x, y), c), (*xn, c)))
            send(copy(t, 1, blk(t, (x, y), c), (*yn, c)))
            if not split[t]:
                send(copy(t, 2, blk(t, (x, y), c), (*dg, c)))
        for t in range(n):
            copy(t, 0, blk(t, xn, c), me).wait_recv()
            if split[t]:
                send(copy(t, 2, blk(t, xn, c, 0), (*yn, c)))
            send(copy(t, 4, blk(t, xn, c), sib))
            copy(t, 1, blk(t, yn, c), me).wait_recv()
            if split[t]:
                send(copy(t, 3, blk(t, yn, c, 1), (*xn, c)))
            send(copy(t, 5, blk(t, yn, c), sib))
        for t in range(n):
            if split[t]:
                copy(t, 2, blk(t, dg, c, 0), me).wait_recv()
                copy(t, 3, blk(t, dg, c, 1), me).wait_recv()
            else:
                copy(t, 2, blk(t, dg, c), me).wait_recv()
            send(copy(t, 6, blk(t, dg, c), sib))
        for t in range(n):
            for k, chip in ((4, xn), (5, yn), (6, dg)):
                copy(t, k, blk(t, chip, 1 - c), me).wait_recv()
        for cp in sent:
            cp.wait_send()

    return pl.pallas_call(body, name=name, out_shape=[SDS(a.shape, a.dtype) for a in arrs],
                          in_specs=[ANY] * n, out_specs=[ANY] * n, input_output_aliases={t: t for t in range(n)},
                          scratch_shapes=[pltpu.SemaphoreType.DMA((7 * n,)), pltpu.SemaphoreType.DMA((7 * n,))],
                          compiler_params=COMM_PARAMS)(*arrs)


def _swap_layers(gs, name):
    n = len(gs)

    def body(*refs):
        ins, outs, send_sems, recv_sems = refs[:n], refs[n:2 * n], refs[2 * n], refs[2 * n + 1]
        x, y, c = _me()
        cps = [pltpu.make_async_remote_copy(src_ref=ins[t].at[1 - c], dst_ref=outs[t], send_sem=send_sems.at[t],
                                            recv_sem=recv_sems.at[t], device_id=(x, y, 1 - c), device_id_type=MESH)
               for t in range(n)]
        for cp in cps:
            cp.start()
        for cp in cps:
            cp.wait()

    return pl.pallas_call(body, name=name, out_shape=[SDS(g.shape[1:], g.dtype) for g in gs],
                          in_specs=[ANY] * n, out_specs=[ANY] * n,
                          scratch_shapes=[pltpu.SemaphoreType.DMA((n,)), pltpu.SemaphoreType.DMA((n,))],
                          compiler_params=COMM_PARAMS)(*gs)


def _exchange_chips(ts, name):
    n = len(ts)

    def body(*refs):
        ins, outs, send_sems, recv_sems = refs[:n], refs[n:2 * n], refs[2 * n], refs[2 * n + 1]
        x, y, c = _me()
        cps = [pltpu.make_async_remote_copy(src_ref=ins[t].at[2 * cx + cy], dst_ref=outs[t].at[j],
                                            send_sem=send_sems.at[3 * t + j], recv_sem=recv_sems.at[3 * t + j],
                                            device_id=(cx, cy, c), device_id_type=MESH)
               for j, (cx, cy) in enumerate(_other_chips(x, y)) for t in range(n)]
        for cp in cps:
            cp.start()
        for cp in cps:
            cp.wait()

    return pl.pallas_call(body, name=name, out_shape=[SDS((3,) + t.shape[1:], t.dtype) for t in ts],
                          in_specs=[ANY] * n, out_specs=[ANY] * n,
                          scratch_shapes=[pltpu.SemaphoreType.DMA((3 * n,)), pltpu.SemaphoreType.DMA((3 * n,))],
                          compiler_params=COMM_PARAMS)(*ts)


def _join_layers(us, name):
    n = len(us)

    def body(*refs):
        outs, send_sems, recv_sems = refs[n:2 * n], refs[2 * n], refs[2 * n + 1]
        x, y, c = _me()
        cps = [pltpu.make_async_remote_copy(src_ref=outs[t].at[c], dst_ref=outs[t].at[c], send_sem=send_sems.at[t],
                                            recv_sem=recv_sems.at[t], device_id=(x, y, 1 - c), device_id_type=MESH)
               for t in range(n)]
        for cp in cps:
            cp.start()
        for cp in cps:
            cp.wait()

    return pl.pallas_call(body, name=name, out_shape=[SDS(u.shape, u.dtype) for u in us],
                          in_specs=[ANY] * n, out_specs=[ANY] * n, input_output_aliases={t: t for t in range(n)},
                          scratch_shapes=[pltpu.SemaphoreType.DMA((n,)), pltpu.SemaphoreType.DMA((n,))],
                          compiler_params=COMM_PARAMS)(*us)


def _allsum_small(v, name):
    M = v.shape[0]

    def body(x_ref, o_ref, all_ref, send_sems, recv_sems, local_sem):
        x, y, c = _me()
        me, sib = (x, y, c), (x, y, 1 - c)
        chips = _other_chips(x, y)

        def rows(px, py, pc):
            return all_ref.at[pl.ds((4 * px + 2 * py + pc) * M, M), :]

        def copy(k, block, to, src=None):
            return pltpu.make_async_remote_copy(src_ref=rows(*block) if src is None else src, dst_ref=rows(*block),
                                                send_sem=send_sems.at[k], recv_sem=recv_sems.at[k],
                                                device_id=to, device_id_type=MESH)

        mine = pltpu.make_async_copy(x_ref, rows(*me), local_sem)
        mine.start()
        first = [copy(0, me, sib, src=x_ref)]
        first += [copy(1 + j, me, (*chip, c), src=x_ref) for j, chip in enumerate(chips)]
        for cp in first:
            cp.start()
        passed = [copy(4 + j, (*chip, c), sib) for j, chip in enumerate(chips)]
        for j, chip in enumerate(chips):
            copy(1 + j, (*chip, c), me).wait_recv()
            passed[j].start()
        copy(0, sib, me).wait_recv()
        for j, chip in enumerate(chips):
            copy(4 + j, (*chip, 1 - c), me).wait_recv()
        for cp in first + passed:
            cp.wait_send()
        mine.wait()
        acc = all_ref[0:M, :]
        for d in range(1, 8):
            acc = acc + all_ref[d * M:(d + 1) * M, :]
        o_ref[...] = acc

    vm = BS(memory_space=pltpu.VMEM)
    return pl.pallas_call(body, name=name, out_shape=SDS((M, LANES), F32), in_specs=[vm], out_specs=vm,
                          scratch_shapes=[pltpu.VMEM((8 * M, LANES), F32), pltpu.SemaphoreType.DMA((7,)),
                                          pltpu.SemaphoreType.DMA((7,)), pltpu.SemaphoreType.DMA],
                          compiler_params=pltpu.CompilerParams(has_side_effects=True, vmem_limit_bytes=VMEM_LIMIT))(v)


BIG = ("w_in", "w_o", "w_gate_up", "w_down", "w_uq", "w_ukv")
TINY = ("conv_w",)
REPL = ("attn_norm", "mla_q_norm", "mla_kv_norm", "pool_w", "pool_scale", "swa_sinks", "mix_norm", "ffn_norm",
        "final_norm")
ORDER = ("attn_norm", "w_in", "mla_q_norm", "w_uq", "mla_kv_norm", "w_ukv", "conv_w", "pool_w", "pool_scale",
         "swa_sinks", "mix_norm", "w_o", "ffn_norm", "w_gate_up", "w_down", "final_norm")


def _rows8(shape):
    return -(-int(np.prod(shape)) // (8 * LANES)) * 8


def _pack(arrs):
    parts = []
    for a in arrs:
        r = _rows8(a.shape)
        parts.append(jnp.pad(a.reshape(-1), (0, r * LANES - a.size)).reshape(r, LANES))
    return jnp.concatenate(parts, axis=0)


def _unpack(buf, shapes):
    out, r0 = [], 0
    for s in shapes:
        n, r = int(np.prod(s)), _rows8(s)
        rows = buf[r0:r0 + r]
        out.append(rows.reshape(s) if n == r * LANES else rows.reshape(-1)[:n].reshape(s))
        r0 += r
    return out


def _cols_joined(g):
    return jnp.transpose(g, (0, 2, 1, 3)).reshape(g.shape[0], g.shape[2], 4 * g.shape[3])


def _cols_split(w):
    n, a, b4 = w.shape
    return jnp.transpose(w.reshape(n, a, 4, b4 // 4), (0, 2, 1, 3))


def _rope_tables(S):
    inv = 1.0 / (10000.0 ** (jnp.arange(0, 32, 2, dtype=F32) / 32))
    ang = jnp.arange(S, dtype=F32)[:, None] * inv[None, :]
    cos, sin = jnp.cos(ang), jnp.sin(ang)
    z = lambda w: jnp.zeros((S, w), F32)
    tc = jnp.concatenate([jnp.ones((S, 64), F32), cos, cos, jnp.ones((S, 32), F32)], axis=1)
    ts1 = jnp.concatenate([z(64), -sin, z(48)], axis=1)
    ts2 = jnp.concatenate([z(80), sin, z(32)], axis=1)
    return tc, ts1, ts2


def _pad_w_in(w):
    z = lambda n: jnp.zeros(w.shape[:-1] + (n,), w.dtype)
    return jnp.concatenate([w[..., 0:384], z(64), w[..., 384:416], z(32), w[..., 416:1952]], axis=-1)


def _unpad_w_in(d):
    return jnp.concatenate([d[..., 0:384], d[..., 448:480], d[..., 512:2048]], axis=-1)


def _pad_heads(w, src, offs):
    cols = []
    for h in range(HEADS):
        src0, n = src[h]
        z = lambda k: jnp.zeros(w.shape[:-1] + (k,), w.dtype)
        cols += [z(offs[h]), w[..., src0:src0 + n], z(128 - offs[h] - n)]
    return jnp.concatenate(cols, axis=-1)


UQ_SRC = [(h * 96, 96) for h in range(HEADS)]
KN_SRC = [(h * 128, 64) for h in range(HEADS)]
V_SRC = [(h * 128 + 64, 64) for h in range(HEADS)]
ZERO_OFF = [0] * HEADS
V_OFF = [(h % 2) * 64 for h in range(HEADS)]


def _unpad_heads(d, src, offs):
    return [d[..., h * 128 + offs[h]: h * 128 + offs[h] + src[h][1]] for h in range(HEADS)]


def kernel(x, attn_norm, w_in, mla_q_norm, w_uq, mla_kv_norm, w_ukv, conv_w, pool_w, pool_scale, swa_sinks, mix_norm, w_o, ffn_norm, w_gate_up, w_down, final_norm, loss_target, m_attn_norm, m_w_in, m_mla_q_norm, m_w_uq, m_mla_kv_norm, m_w_ukv, m_conv_w, m_pool_w, m_pool_scale, m_swa_sinks, m_mix_norm, m_w_o, m_ffn_norm, m_w_gate_up, m_w_down, m_final_norm, v_attn_norm, v_w_in, v_mla_q_norm, v_w_uq, v_mla_kv_norm, v_w_ukv, v_conv_w, v_pool_w, v_pool_scale, v_swa_sinks, v_mix_norm, v_w_o, v_ffn_norm, v_w_gate_up, v_w_down, v_final_norm):
    W = dict(attn_norm=attn_norm, w_in=w_in, mla_q_norm=mla_q_norm, w_uq=w_uq, mla_kv_norm=mla_kv_norm, w_ukv=w_ukv,
             conv_w=conv_w, pool_w=pool_w, pool_scale=pool_scale, swa_sinks=swa_sinks, mix_norm=mix_norm, w_o=w_o,
             ffn_norm=ffn_norm, w_gate_up=w_gate_up, w_down=w_down, final_norm=final_norm)
    M1 = dict(attn_norm=m_attn_norm, w_in=m_w_in, mla_q_norm=m_mla_q_norm, w_uq=m_w_uq, mla_kv_norm=m_mla_kv_norm,
              w_ukv=m_w_ukv, conv_w=m_conv_w, pool_w=m_pool_w, pool_scale=m_pool_scale, swa_sinks=m_swa_sinks,
              mix_norm=m_mix_norm, w_o=m_w_o, ffn_norm=m_ffn_norm, w_gate_up=m_w_gate_up, w_down=m_w_down,
              final_norm=m_final_norm)
    V2 = dict(attn_norm=v_attn_norm, w_in=v_w_in, mla_q_norm=v_mla_q_norm, w_uq=v_w_uq, mla_kv_norm=v_mla_kv_norm,
              w_ukv=v_w_ukv, conv_w=v_conv_w, pool_w=v_pool_w, pool_scale=v_pool_scale, swa_sinks=v_swa_sinks,
              mix_norm=v_mix_norm, w_o=v_w_o, ffn_norm=v_ffn_norm, w_gate_up=v_w_gate_up, w_down=v_w_down,
              final_norm=v_final_norm)
    S = x.shape[1]
    xc, yc, cc = _me()
    chip = 2 * xc + yc
    kc = jnp.stack([chip, cc]).astype(jnp.int32)

    names = ("w_in", "w_uq", "w_ukv", "w_o", "w_gate_up", "w_down", "conv_w")
    placed = [_place(W[n], kc, F32 if n == "conv_w" else BF16, f"place_{n}") for n in names]
    gi, gq, gkv, go, gu4, gd, gcv = _gather(placed, "gather_weights")
    win_p = _pad_w_in(_cols_joined(gi))
    wuq_p = _pad_heads(_cols_joined(gq), UQ_SRC, ZERO_OFF)
    wukv = _cols_joined(gkv)
    wk_p = _pad_heads(wukv, KN_SRC, ZERO_OFF)
    wv_p = _pad_heads(wukv, V_SRC, V_OFF)
    conv8 = jnp.pad(_cols_joined(gcv), ((0, 0), (0, 5), (0, 0)))
    wo = go.reshape(2, D, D)
    wdown = gd.reshape(2, D_FF, D)
    pwd = jnp.concatenate([jnp.concatenate(
        [jnp.pad(pool_w[:, 2 * b], ((0, 0), (0, 0), (0, 64))), jnp.pad(pool_w[:, 2 * b + 1], ((0, 0), (0, 0), (64, 0)))],
        axis=1) for b in range(2)], axis=1).astype(BF16)
    tabs = _rope_tables(S)
    g_attn, g_q, g_kv, g_mix, g_ffn, g_ps = (_g3(W[n]) for n in ("attn_norm", "mla_q_norm", "mla_kv_norm", "mix_norm",
                                                                  "ffn_norm", "pool_scale"))

    xs = [x[0]]
    saved = []
    for l in range(DEPTH):
        x0 = xs[-1]
        proj, h = _norm_mm(x0, g_attn, l, win_p, _wspec_in(l), D_INP, D_INP, F32, f"in_proj{l}")
        q, k, v, kt, vt = _mla_prep(proj, g_q, g_kv, wuq_p, wk_p, wv_p, tabs, l, f"mla_prep{l}")
        ya, lse = _mla_attn(q, k, vt, f"mla_attn{l}")
        yb = _conv(proj, conv8, l, f"conv{l}")
        ycp = _pool(proj, pwd, g_ps, l, f"pool{l}")
        yd = _swa(proj, swa_sinks, l, f"swa{l}")
        x1, ycat, mixed = _mix_out(x0, ya, yb, ycp, yd, g_mix, wo, l, f"mix_out{l}")
        gu, h2 = _norm_mm(x1, g_ffn, l, gu4, _wspec_gu(l), 2 * D_FF, 2 * D_FF // 4, BF16, f"gate_up{l}")
        x2, act = _swiglu_mm_res(x1, gu, wdown, l, f"down{l}")
        saved.append(dict(x0=x0, proj=proj, h=h, q=q, k=k, kt=kt, v=v, lse=lse, x1=x1, ycat=ycat, mixed=mixed,
                          gu=gu, h2=h2, act=act))
        xs.append(x2)

    dx, dx16, dg_final, loss_tile = _loss_head(xs[-1], final_norm.reshape(1, D), loss_target[0], "loss_head")
    loss = lax.psum(loss_tile[0, 0] * (0.5 / D), ("x", "y", "c"))

    G = {n: [None] * DEPTH for n in ("w_uq", "w_ukv") + TINY + REPL if n != "final_norm"}
    gw_in = gw_o = gw_gu = gw_down = None
    for l in reversed(range(DEPTH)):
        sv = saved[l]
        dgu = _bwd_down(dx16, wdown, sv["gu"], l, f"down_bwd{l}")
        gw_down = _mm_tn(sv["act"], dx16, l, gw_down, f"dw_down{l}")
        gw_gu = _mm_tn(sv["h2"], dgu, l, gw_gu, f"dw_gate_up{l}", split4=True)
        dx1, dx1_16, dg = _mm_nt_normbwd(dgu, gu4, l, sv["x1"], g_ffn, dx, 1, f"gate_up_bwd{l}")
        G["ffn_norm"][l] = dg[0]
        gw_o = _mm_tn(sv["mixed"], dx1_16, l, gw_o, f"dw_o{l}")
        dycat, dg = _mm_nt_normbwd(dx1_16, wo.reshape(2, 1, D, D), l, sv["ycat"], g_mix, None, 4, f"mix_bwd{l}")
        G["mix_norm"][l] = dg[0]

        proj = sv["proj"]
        delta = _mla_delta(dycat, sv["ycat"], f"mla_delta{l}")
        dq, dk, dv = _mla_attn_bwd(sv["q"], sv["k"], sv["kt"], sv["v"], dycat, sv["lse"], delta, f"mla_attn_bwd{l}")
        dcq, dckv, dkr, dwuq, dwk, dwv, dgq, dgkv = _mla_prep_bwd(
            dq, dk, dv, proj, g_q, g_kv, wuq_p, wk_p, wv_p, tabs, l, f"mla_prep_bwd{l}")
        dgb, dgc, duc, dcw = _conv_bwd(proj, conv8, dycat, l, f"conv_bwd{l}")
        dup, dpw, dps = _pool_bwd(proj, pwd, g_ps, dycat, l, f"pool_bwd{l}")
        dqs, dks, dvs, dsink = _swa_bwd(proj, swa_sinks, dycat, l, f"swa_bwd{l}")
        dproj = jnp.concatenate([dcq, dckv, dkr, dgb, dgc, duc, dup, dqs, dks, dvs], axis=1)
        gw_in = _mm_tn(sv["h"], dproj, l, gw_in, f"dw_in{l}")
        dx, dx16, dg = _mm_nt_normbwd(dproj, win_p.reshape(2, 1, D, D_INP), l, sv["x0"], g_attn, dx1, 1, f"in_proj_bwd{l}")
        G["attn_norm"][l] = dg[0]
        G["mla_q_norm"][l] = dgq[0]
        G["mla_kv_norm"][l] = dgkv[0]
        G["w_uq"][l] = jnp.concatenate(_unpad_heads(dwuq, UQ_SRC, ZERO_OFF), axis=1)
        kn, vv = _unpad_heads(dwk, KN_SRC, ZERO_OFF), _unpad_heads(dwv, V_SRC, V_OFF)
        G["w_ukv"][l] = jnp.concatenate([t for h in range(HEADS) for t in (kn[h], vv[h])], axis=1)
        G["conv_w"][l] = dcw[0:3]
        G["pool_w"][l] = jnp.stack([dpw[0:64, 0:64], dpw[64:128, 64:128], dpw[128:192, 0:64], dpw[192:256, 64:128]])
        G["pool_scale"][l] = dps[0]
        G["swa_sinks"][l] = dsink[0, 0:4]
    grad_x = dx[None]
    Gl = {n: jnp.stack(G[n]) for n in G}
    Gl["final_norm"] = dg_final[0]

    gbig = [_cols_split(_unpad_w_in(gw_in)), gw_o.reshape(2, 4, D // 4, D), gw_gu, gw_down.reshape(2, 4, D_FF // 4, D),
            _cols_split(Gl["w_uq"]), _cols_split(Gl["w_ukv"])]
    got = _swap_layers(gbig, "rs_swap_cores")
    pairs = [_pair_sum(g, o, kc, f"rs_pair_sum_{n}") for g, o, n in zip(gbig, got, BIG)]
    got3 = _exchange_chips([p[1] for p in pairs], "rs_exchange_chips")
    us = [_chip_sum(p[0], o3, kc, f"rs_chip_sum_{n}") for p, o3, n in zip(pairs, got3, BIG)]
    gsum = _join_layers(us, "rs_join_cores")
    res = {}
    for n, g in zip(BIG, gsum):
        d_, m_, v_ = _adamw(W[n], g, M1[n], V2[n], f"adamw_{n}")
        res["g", n], res["d", n], res["m", n], res["v", n] = g, d_, m_, v_

    small = TINY + REPL
    full_shapes = [Gl[n].shape for n in small]
    summed = _unpack(_allsum_small(_pack([Gl[n] for n in small]), "allsum_small"), full_shapes)
    gs = {}
    for n, g in zip(small, summed):
        if n in TINY:
            wdt = W[n].shape[2]
            g = lax.dynamic_slice_in_dim(g, chip * wdt, wdt, axis=2)
        gs[n] = g
    own_shapes = [W[n].shape for n in small]
    pk = lambda src: _pack([src[n] for n in small])[None]
    d_s, m_s, v_s = _adamw(pk(W), pk(gs), pk(M1), pk(V2), "adamw_small")
    for key, buf in (("d", d_s), ("m", m_s), ("v", v_s)):
        for n, a in zip(small, _unpack(buf[0], own_shapes)):
            res[key, n] = a
    for n in small:
        res["g", n] = gs[n]

    return (loss, grad_x, *[res["g", n] for n in ORDER], *[res["d", n] for n in ORDER],
            *[res["m", n] for n in ORDER], *[res["v", n] for n in ORDER])
```

```python
import math

import numpy as np
import jax
import jax.numpy as jnp
from jax import lax
from jax.experimental import pallas as pl
from jax.experimental.pallas import tpu as pltpu

F32, BF16 = jnp.float32, jnp.bfloat16
SDS = jax.ShapeDtypeStruct
BS = pl.BlockSpec
MESH = pl.DeviceIdType.MESH

D = 1024
DEPTH = 2
HEADS = 4
D_FF = 2816
D_INP = 2048
EPS = 1e-6
SWA_WINDOW = 128
BLK = 128
SLOPES = tuple(2.0 ** (-8.0 * (i + 1) / 4) for i in range(4))
QK_SCALE = 1.0 / math.sqrt(96)
SWA_SCALE = 1.0 / math.sqrt(64)
LR, B1, B2, ADAM_EPS, WD, STEP = 0.001, 0.9, 0.999, 1e-08, 0.01, 10

LANES = 1024
VMEM_LIMIT = 56 * 1024 * 1024
NEG_INF = float("-inf")

C_CQ, C_CKV, C_KR, C_GB, C_GC, C_UC, C_UP, C_QS, C_KS, C_VS = 0, 256, 384, 512, 768, 1024, 1280, 1536, 1792, 1920


def _params(ngrid):
    return pltpu.CompilerParams(dimension_semantics=("arbitrary",) * ngrid, vmem_limit_bytes=VMEM_LIMIT)


def _pc(body, *, name, grid, in_specs, out_specs, out_shape, scratch=(), aliases=None):
    return pl.pallas_call(
        body, name=name, grid=grid, in_specs=in_specs, out_specs=out_specs, out_shape=out_shape,
        scratch_shapes=scratch, input_output_aliases=aliases or {}, compiler_params=_params(len(grid)))


def _dot(a, b):
    return jnp.dot(a, b, preferred_element_type=F32)


def _dot_nt(a, b):
    return lax.dot_general(a, b, (((1,), (1,)), ((), ())), preferred_element_type=F32)


def _dot_tn(a, b):
    return lax.dot_general(a, b, (((0,), (0,)), ((), ())), preferred_element_type=F32)


def _tile(n, cap):
    if n <= cap:
        return n
    t = cap - cap % 128
    while n % t:
        t -= 128
    return t


def _row_tile(a, b, cap=262144):
    bp = -(-b // 128) * 128
    best = None
    for t in range(8, a + 1, 8):
        if a % t == 0 and t * bp <= cap:
            best = t
    return best if best is not None else a


def _g3(a):
    return a.reshape(a.shape[0], 1, a.shape[1])


def _norm_mm(x, g3, l, w, wspec, N, tn, out_dtype, name):
    S, K = x.shape
    tm = min(512, S)

    def body(x_ref, g_ref, w_ref, y_ref, h_ref):
        @pl.when(pl.program_id(1) == 0)
        def _():
            xv = x_ref[...]
            r = lax.rsqrt(jnp.mean(xv * xv, axis=-1, keepdims=True) + EPS)
            h_ref[...] = (xv * r * g_ref[...]).astype(BF16)

        y_ref[...] = _dot(h_ref[...], w_ref[...]).astype(out_dtype)

    return _pc(body, name=name, grid=(S // tm, N // tn),
               in_specs=[BS((tm, K), lambda i, j: (i, 0)), BS((None, 1, K), lambda i, j: (l, 0, 0)), wspec],
               out_specs=[BS((tm, tn), lambda i, j: (i, j)), BS((tm, K), lambda i, j: (i, 0))],
               out_shape=[SDS((S, N), out_dtype), SDS((S, K), BF16)])(x, g3, w)


def _wspec_in(l):
    return BS((None, D, D_INP), lambda i, j: (l, 0, j))


def _wspec_gu(l):
    return BS((None, None, D, 2 * D_FF // 4), lambda i, j: (l, j, 0, 0))


def _mix_out(x0, ya, yb, yc, yd, gmix3, wo, l, name):
    S = x0.shape[0]
    tm = min(512, S)

    def body(x_ref, ya_ref, yb_ref, yc_ref, yd_ref, g_ref, w_ref, x1_ref, ycat_ref, mixed_ref):
        groups = [ya_ref[...], yb_ref[...], yc_ref[...], yd_ref[...]]
        for gi, yg in enumerate(groups):
            sl = slice(gi * 256, (gi + 1) * 256)
            r = lax.rsqrt(jnp.mean(yg * yg, axis=-1, keepdims=True) + EPS)
            ycat_ref[:, sl] = yg
            mixed_ref[:, sl] = (yg * r * g_ref[:, sl]).astype(BF16)
        x1_ref[...] = x_ref[...] + _dot(mixed_ref[...], w_ref[...])

    row = lambda w: BS((tm, w), lambda i: (i, 0))
    return _pc(body, name=name, grid=(S // tm,),
               in_specs=[row(D), row(256), row(256), row(256), row(256), BS((None, 1, D), lambda i: (l, 0, 0)),
                         BS((None, D, D), lambda i: (l, 0, 0))],
               out_specs=[row(D), row(D), row(D)],
               out_shape=[SDS((S, D), F32), SDS((S, D), F32), SDS((S, D), BF16)])(x0, ya, yb, yc, yd, gmix3, wo)


def _swiglu_mm_res(x1, gu, wdown, l, name):
    S = x1.shape[0]
    tm = min(256, S)

    def body(x_ref, gate_ref, up_ref, w_ref, x2_ref, act_ref):
        gt = gate_ref[...].astype(F32)
        act = (gt * pl.reciprocal(1.0 + jnp.exp(-gt), approx=True) * up_ref[...].astype(F32)).astype(BF16)
        act_ref[...] = act
        x2_ref[...] = x_ref[...] + _dot(act, w_ref[...])

    return _pc(body, name=name, grid=(S // tm,),
               in_specs=[BS((tm, D), lambda i: (i, 0)), BS((tm, D_FF), lambda i: (i, 0)),
                         BS((tm, D_FF), lambda i: (i, 1)), BS((None, D_FF, D), lambda i: (l, 0, 0))],
               out_specs=[BS((tm, D), lambda i: (i, 0)), BS((tm, D_FF), lambda i: (i, 0))],
               out_shape=[SDS((S, D), F32), SDS((S, D_FF), BF16)])(x1, gu, gu, wdown)


def _loss_head(x, g, tgt, name):
    S = x.shape[0]
    tm = min(512, S)

    def body(x_ref, g_ref, t_ref, dx_ref, dx16_ref, dg_ref, loss_ref):
        @pl.when(pl.program_id(0) == 0)
        def _():
            dg_ref[...] = jnp.zeros_like(dg_ref)
            loss_ref[...] = jnp.zeros_like(loss_ref)

        xv = x_ref[...]
        r = lax.rsqrt(jnp.mean(xv * xv, axis=-1, keepdims=True) + EPS)
        xh = xv * r
        gv = g_ref[...]
        diff = xh * gv - t_ref[...]
        loss_ref[...] += jnp.sum(diff * diff)
        dy = diff * (1.0 / D)
        dg_ref[...] += jnp.sum(dy * xh, axis=0, keepdims=True)
        dxh = dy * gv
        dx = r * (dxh - xh * jnp.mean(dxh * xh, axis=-1, keepdims=True))
        dx_ref[...] = dx
        dx16_ref[...] = dx.astype(BF16)

    row = BS((tm, D), lambda i: (i, 0))
    return _pc(body, name=name, grid=(S // tm,),
               in_specs=[row, BS((1, D), lambda i: (0, 0)), row],
               out_specs=[row, row, BS((8, D), lambda i: (0, 0)), BS((8, 128), lambda i: (0, 0))],
               out_shape=[SDS((S, D), F32), SDS((S, D), BF16), SDS((8, D), F32), SDS((8, 128), F32)])(x, g, tgt)


def _mm_tn(a, b, l, prev, name, split4=False):
    S, Ka = a.shape
    N = b.shape[1]
    if split4:
        ta, tn = _tile(Ka, 256), N // 4
        out_shape = SDS((2, 4, Ka, tn), F32)
        out_spec = BS((None, None, ta, tn), lambda j, i: (l, j, i, 0))
    else:
        ta, tn = _tile(Ka, 512), _tile(N, 1024)
        out_shape = SDS((2, Ka, N), F32)
        out_spec = BS((None, ta, tn), lambda j, i: (l, i, j))

    def body(a_ref, b_ref, *rest):
        rest[-1][...] = _dot_tn(a_ref[...], b_ref[...])

    in_specs = [BS((S, ta), lambda j, i: (0, i)), BS((S, tn), lambda j, i: (0, j))]
    args = [a, b]
    if prev is not None:
        in_specs.append(BS(memory_space=pl.ANY))
        args.append(prev)
    return _pc(body, name=name, grid=(N // tn, Ka // ta), in_specs=in_specs, out_specs=out_spec, out_shape=out_shape,
               aliases={2: 0} if prev is not None else None)(*args)


def _bwd_down(dx16, wdown, gu, l, name):
    S = dx16.shape[0]
    tm = min(256, S)

    def body(dx_ref, w_ref, gate_ref, up_ref, dgu_ref):
        dact = _dot_nt(dx_ref[...], w_ref[...])
        gt = gate_ref[...].astype(F32)
        sg = pl.reciprocal(1.0 + jnp.exp(-gt), approx=True)
        dgu_ref[:, 0:D_FF] = (dact * up_ref[...].astype(F32) * (sg * (1.0 + gt * (1.0 - sg)))).astype(BF16)
        dgu_ref[:, D_FF:2 * D_FF] = (dact * (gt * sg)).astype(BF16)

    return _pc(body, name=name, grid=(S // tm,),
               in_specs=[BS((tm, D), lambda i: (i, 0)), BS((None, D_FF, D), lambda i: (l, 0, 0)),
                         BS((tm, D_FF), lambda i: (i, 0)), BS((tm, D_FF), lambda i: (i, 1))],
               out_specs=BS((tm, 2 * D_FF), lambda i: (i, 0)),
               out_shape=SDS((S, 2 * D_FF), BF16))(dx16, wdown, gu, gu)


def _mm_nt_normbwd(dy, w4, l, x, g3, dres, ngroups, name):
    S, K = dy.shape
    _, nk, _, kc = w4.shape
    tm = min(512, S)
    gw = D // ngroups
    has_res = dres is not None

    def body(*refs):
        if has_res:
            dy_ref, w_ref, x_ref, g_ref, res_ref, dx_ref, dx16_ref, dg_ref = refs
        else:
            dy_ref, w_ref, x_ref, g_ref, dx_ref, dg_ref = refs

        @pl.when(pl.program_id(0) == 0)
        def _():
            dg_ref[...] = jnp.zeros_like(dg_ref)

        dh = _dot_nt(dy_ref[:, 0:kc], w_ref[0])
        for k in range(1, nk):
            dh = dh + _dot_nt(dy_ref[:, k * kc:(k + 1) * kc], w_ref[k])
        for gi in range(ngroups):
            sl = slice(gi * gw, (gi + 1) * gw)
            xg = x_ref[:, sl]
            r = lax.rsqrt(jnp.mean(xg * xg, axis=-1, keepdims=True) + EPS)
            xh = xg * r
            dhg = dh[:, sl]
            dg_ref[:, sl] += jnp.sum(dhg * xh, axis=0, keepdims=True)
            dxh = dhg * g_ref[:, sl]
            dxg = r * (dxh - xh * jnp.mean(dxh * xh, axis=-1, keepdims=True))
            if has_res:
                dxg = dxg + res_ref[:, sl]
                dx16_ref[:, sl] = dxg.astype(BF16)
            dx_ref[:, sl] = dxg

    row = BS((tm, D), lambda i: (i, 0))
    in_specs = [BS((tm, K), lambda i: (i, 0)),
                BS((None, nk, D, kc), lambda i: (l, 0, 0, 0), pipeline_mode=pl.Buffered(1)), row,
                BS((None, 1, D), lambda i: (l, 0, 0))]
    args = [dy, w4, x, g3]
    out_specs, out_shape = [row], [SDS((S, D), F32)]
    if has_res:
        in_specs.append(row)
        args.append(dres)
        out_specs.append(row)
        out_shape.append(SDS((S, D), BF16))
    out_specs.append(BS((8, D), lambda i: (0, 0)))
    out_shape.append(SDS((8, D), F32))
    return _pc(body, name=name, grid=(S // tm,), in_specs=in_specs, out_specs=out_specs, out_shape=out_shape)(*args)


def _rope(x, c, s1, s2):
    return x * c + pltpu.roll(x, 112, axis=1) * s1 + pltpu.roll(x, 16, axis=1) * s2


def _rope_t(dy, c, s1, s2):
    return dy * c + pltpu.roll(dy * s1, 16, axis=1) + pltpu.roll(dy * s2, 112, axis=1)


def _mla_prep(proj, gq3, gkv3, wuq, wk, wv, tabs, l, name):
    S = proj.shape[0]
    tm = min(512, S)
    tc, ts1, ts2 = tabs

    def body(cq_ref, ckv_ref, kr_ref, gq_ref, gkv_ref, wuq_ref, wk_ref, wv_ref, c_ref, s1_ref, s2_ref,
             q_ref, k_ref, v_ref, kt_ref, vt_ref):
        c, s1, s2 = c_ref[...], s1_ref[...], s2_ref[...]
        cq = cq_ref[...]
        rq = lax.rsqrt(jnp.mean(cq * cq, axis=-1, keepdims=True) + EPS)
        qa = _dot((cq * rq * gq_ref[...]).astype(BF16), wuq_ref[...])
        ckv = ckv_ref[...]
        rkv = lax.rsqrt(jnp.mean(ckv * ckv, axis=-1, keepdims=True) + EPS)
        ckvn = (ckv * rkv * gkv_ref[...]).astype(BF16)
        ka = _dot(ckvn, wk_ref[...])
        va = _dot(ckvn, wv_ref[...])
        v_ref[...] = va.astype(BF16)
        vt_ref[...] = va.T.astype(BF16)
        krr = _rope(kr_ref[...], c, s1, s2)
        for h in range(HEADS):
            sl = slice(h * 128, (h + 1) * 128)
            q_ref[:, sl] = (_rope(qa[:, sl], c, s1, s2) * QK_SCALE).astype(BF16)
            kh = ka[:, sl] + krr
            k_ref[:, sl] = kh.astype(BF16)
            kt_ref[sl, :] = kh.T.astype(BF16)

    lay = lambda a, b: BS((None, a, b), lambda i: (l, 0, 0))
    tab = BS((tm, 128), lambda i: (i, 0))
    return _pc(body, name=name, grid=(S // tm,),
               in_specs=[BS((tm, 256), lambda i: (i, 0)), BS((tm, 128), lambda i: (i, 2)), BS((tm, 128), lambda i: (i, 3)),
                         lay(1, 256), lay(1, 128), lay(256, 512), lay(128, 512), lay(128, 512), tab, tab, tab],
               out_specs=[BS((tm, 512), lambda i: (i, 0))] * 3 + [BS((512, tm), lambda i: (0, i))] * 2,
               out_shape=[SDS((S, 512), BF16)] * 3 + [SDS((512, S), BF16)] * 2)(
        proj, proj, proj, gq3, gkv3, wuq, wk, wv, tc, ts1, ts2)


def _mla_prep_bwd(dq, dk, dv, proj, gq3, gkv3, wuq, wk, wv, tabs, l, name):
    S = proj.shape[0]
    tm = min(512, S)
    tc, ts1, ts2 = tabs

    def body(dq_ref, dk_ref, dv_ref, cq_ref, ckv_ref, gq_ref, gkv_ref, wuq_ref, wk_ref, wv_ref, c_ref, s1_ref, s2_ref,
             dcq_ref, dckv_ref, dkr_ref, dwuq_ref, dwk_ref, dwv_ref, dgq_ref, dgkv_ref):
        @pl.when(pl.program_id(0) == 0)
        def _():
            for r in (dwuq_ref, dwk_ref, dwv_ref, dgq_ref, dgkv_ref):
                r[...] = jnp.zeros_like(r)

        c, s1, s2 = c_ref[...], s1_ref[...], s2_ref[...]
        dqp = jnp.concatenate(
            [_rope_t(dq_ref[h * 128:(h + 1) * 128, :].T * QK_SCALE, c, s1, s2) for h in range(HEADS)], axis=1).astype(BF16)
        cq = cq_ref[...]
        rq = lax.rsqrt(jnp.mean(cq * cq, axis=-1, keepdims=True) + EPS)
        cqh = cq * rq
        gq_v = gq_ref[...]
        dwuq_ref[...] += _dot_tn((cqh * gq_v).astype(BF16), dqp)
        dcqn = _dot_nt(dqp, wuq_ref[...])
        dgq_ref[...] += jnp.sum(dcqn * cqh, axis=0, keepdims=True)
        dxh = dcqn * gq_v
        dcq_ref[...] = (rq * (dxh - cqh * jnp.mean(dxh * cqh, axis=-1, keepdims=True))).astype(BF16)

        dkb = dk_ref[...].astype(BF16)
        dvb = dv_ref[...].astype(BF16)
        ckv = ckv_ref[...]
        rkv = lax.rsqrt(jnp.mean(ckv * ckv, axis=-1, keepdims=True) + EPS)
        ckh = ckv * rkv
        gkv_v = gkv_ref[...]
        ckvn = (ckh * gkv_v).astype(BF16)
        dwk_ref[...] += _dot_tn(ckvn, dkb)
        dwv_ref[...] += _dot_tn(ckvn, dvb)
        dckvn = _dot_nt(dkb, wk_ref[...]) + _dot_nt(dvb, wv_ref[...])
        dgkv_ref[...] += jnp.sum(dckvn * ckh, axis=0, keepdims=True)
        dyh = dckvn * gkv_v
        dckv_ref[...] = (rkv * (dyh - ckh * jnp.mean(dyh * ckh, axis=-1, keepdims=True))).astype(BF16)
        dks = dk_ref[:, 0:128] + dk_ref[:, 128:256] + dk_ref[:, 256:384] + dk_ref[:, 384:512]
        dkr_ref[...] = _rope_t(dks, c, s1, s2).astype(BF16)

    full = lambda a, b: BS((a, b), lambda i: (0, 0))
    lay = lambda a, b: BS((None, a, b), lambda i: (l, 0, 0))
    tab = BS((tm, 128), lambda i: (i, 0))
    row = lambda w: BS((tm, w), lambda i: (i, 0))
    return _pc(body, name=name, grid=(S // tm,),
               in_specs=[BS((512, tm), lambda i: (0, i)), row(512), row(512), BS((tm, 256), lambda i: (i, 0)),
                         BS((tm, 128), lambda i: (i, 2)),
                         lay(1, 256), lay(1, 128), lay(256, 512), lay(128, 512), lay(128, 512), tab, tab, tab],
               out_specs=[row(256), row(128), row(128), full(256, 512), full(128, 512), full(128, 512),
                          full(8, 256), full(8, 128)],
               out_shape=[SDS((S, 256), BF16), SDS((S, 128), BF16), SDS((S, 128), BF16), SDS((256, 512), F32),
                          SDS((128, 512), F32), SDS((128, 512), F32), SDS((8, 256), F32), SDS((8, 128), F32)])(
        dq, dk, dv, proj, proj, gq3, gkv3, wuq, wk, wv, tc, ts1, ts2)


def _causal_steps(n, q_outer):
    if q_outer:
        pairs = [(i, j) for i in range(n) for j in range(i + 1)]
    else:
        pairs = [(i, j) for j in range(n) for i in range(j, n)]
    return jnp.asarray([p[0] for p in pairs], jnp.int32), jnp.asarray([p[1] for p in pairs], jnp.int32)


def _mla_attn(q, k, vt, name):
    S = q.shape[0]
    t = min(512, S)
    n = S // t

    qi, kj = _causal_steps(n, True)

    def body(qi_ref, kj_ref, q_ref, k_ref, vt_ref, ya_ref, lse_ref, m_sc, l_sc, acc_sc):
        i, j = qi_ref[pl.program_id(1)], kj_ref[pl.program_id(1)]

        @pl.when(j == 0)
        def _():
            m_sc[...] = jnp.full_like(m_sc, NEG_INF)
            l_sc[...] = jnp.zeros_like(l_sc)
            acc_sc[...] = jnp.zeros_like(acc_sc)

        def step(masked):
            for hh in range(2):
                sl = slice(hh * 128, (hh + 1) * 128)
                st = _dot_nt(k_ref[:, sl], q_ref[:, sl])
                if masked:
                    key = lax.broadcasted_iota(jnp.int32, (t, t), 0)
                    qry = lax.broadcasted_iota(jnp.int32, (t, t), 1)
                    st = jnp.where(key <= qry, st, NEG_INF)
                m_prev = m_sc[hh]
                m_new = jnp.maximum(m_prev, jnp.max(st, axis=0, keepdims=True))
                p = jnp.exp(st - m_new)
                alpha = jnp.exp(m_prev - m_new)
                l_sc[hh] = alpha * l_sc[hh] + jnp.sum(p, axis=0, keepdims=True)
                acc_sc[hh] = alpha * acc_sc[hh] + _dot(vt_ref[sl, :], p.astype(BF16))
                m_sc[hh] = m_new

        @pl.when(j < i)
        def _():
            step(False)

        @pl.when(j == i)
        def _():
            step(True)
            ya_ref[...] = (acc_sc[0] / l_sc[0] + acc_sc[1] / l_sc[1]).T
            for hh in range(2):
                lse_ref[hh] = m_sc[hh] + jnp.log(l_sc[hh])

    gs = pltpu.PrefetchScalarGridSpec(
        num_scalar_prefetch=2, grid=(2, qi.shape[0]),
        in_specs=[BS((t, 256), lambda p, s, qi, kj: (qi[s], p)), BS((t, 256), lambda p, s, qi, kj: (kj[s], p)),
                  BS((256, t), lambda p, s, qi, kj: (p, kj[s]))],
        out_specs=[BS((t, 128), lambda p, s, qi, kj: (qi[s], p)), BS((2, 1, t), lambda p, s, qi, kj: (p, 0, qi[s]))],
        scratch_shapes=[pltpu.VMEM((2, 1, t), F32), pltpu.VMEM((2, 1, t), F32), pltpu.VMEM((2, 128, t), F32)])
    return pl.pallas_call(body, name=name, grid_spec=gs, out_shape=[SDS((S, 256), F32), SDS((HEADS, 1, S), F32)],
                          compiler_params=_params(2))(qi, kj, q, k, vt)


def _mla_delta(dycat, ya, name):
    S = ya.shape[0]
    t = min(512, S)

    def body(do_ref, ya_ref, d_ref):
        prod = do_ref[...] * ya_ref[...]
        for p in range(2):
            pt = prod[:, p * 128:(p + 1) * 128].T
            d_ref[2 * p] = jnp.sum(pt[0:64, :], axis=0, keepdims=True)
            d_ref[2 * p + 1] = jnp.sum(pt[64:128, :], axis=0, keepdims=True)

    return _pc(body, name=name, grid=(S // t,),
               in_specs=[BS((t, 256), lambda i: (i, 0)), BS((t, 256), lambda i: (i, 0))],
               out_specs=BS((HEADS, 1, t), lambda i: (0, 0, i)), out_shape=SDS((HEADS, 1, S), F32))(dycat, ya)


def _mla_attn_bwd(q, k, kt, v, dya, lse, delta, name):
    S = q.shape[0]
    t = min(512, S)
    n = S // t

    qi, kj = _causal_steps(n, False)

    def body(qi_ref, kj_ref, q_ref, k_ref, kt_ref, v_ref, do_ref, lse_ref, delta_ref, dqt_ref, dk_ref, dv_ref):
        i, j = qi_ref[pl.program_id(1)], kj_ref[pl.program_id(1)]

        @pl.when(pl.program_id(1) == 0)
        def _():
            dqt_ref[...] = jnp.zeros_like(dqt_ref)

        @pl.when(i == j)
        def _():
            dk_ref[...] = jnp.zeros_like(dk_ref)
            dv_ref[...] = jnp.zeros_like(dv_ref)

        def step(masked):
            qv = q_ref[...]
            p = jnp.exp(_dot_nt(k_ref[...], qv) - lse_ref[...])
            if masked:
                key = lax.broadcasted_iota(jnp.int32, (t, t), 0)
                qry = lax.broadcasted_iota(jnp.int32, (t, t), 1)
                p = jnp.where(key <= qry, p, 0.0)
            dob = do_ref[...].astype(BF16)
            dv_ref[...] += _dot(p.astype(BF16), dob)
            ds = (p * (_dot_nt(v_ref[...], dob) - delta_ref[...])).astype(BF16)
            dk_ref[...] += _dot(ds, qv)
            cols = pl.ds(pl.multiple_of(i * t, t), t)
            dqt_ref[:, cols] += _dot(kt_ref[...], ds)

        @pl.when(i > j)
        def _():
            step(False)

        @pl.when(i == j)
        def _():
            step(True)

    qs = BS((t, 128), lambda h, s, qi, kj: (qi[s], h))
    ks = BS((t, 128), lambda h, s, qi, kj: (kj[s], h))
    rowv = BS((None, 1, t), lambda h, s, qi, kj: (h, 0, qi[s]))
    gs = pltpu.PrefetchScalarGridSpec(
        num_scalar_prefetch=2, grid=(HEADS, qi.shape[0]),
        in_specs=[qs, ks, BS((128, t), lambda h, s, qi, kj: (h, kj[s])), ks,
                  BS((t, 128), lambda h, s, qi, kj: (qi[s], h // 2)), rowv, rowv],
        out_specs=[BS((128, S), lambda h, s, qi, kj: (h, 0)), ks, ks])
    return pl.pallas_call(body, name=name, grid_spec=gs,
                          out_shape=[SDS((512, S), F32), SDS((S, 512), F32), SDS((S, 512), F32)],
                          compiler_params=_params(2))(qi, kj, q, k, kt, v, dya, lse, delta)


def _swa_scores(qm, kk, valid, distf, slope, sink):
    sc = _dot_nt(qm, kk) * SWA_SCALE
    sc = jnp.where(valid, sc - slope * distf, NEG_INF)
    m = jnp.maximum(jnp.max(sc, axis=-1, keepdims=True), sink)
    e = jnp.exp(sc - m)
    esink = jnp.exp(sink - m)
    den = jnp.sum(e, axis=-1, keepdims=True) + esink
    return e / den, esink / den


def _swa_masks():
    r = lax.broadcasted_iota(jnp.int32, (BLK, 2 * BLK), 0)
    c = lax.broadcasted_iota(jnp.int32, (BLK, 2 * BLK), 1)
    dist = r + BLK - c
    return (dist >= 0) & (dist < SWA_WINDOW), c >= BLK, dist.astype(F32)


def _to_half(xb, pos, b):
    return xb if pos == b else pltpu.roll(xb, 64, axis=1)


def _swa(proj, sinks, l, name):
    S = proj.shape[0]
    nb = S // BLK

    def body(q_ref, k_ref, v_ref, sink_ref, o_ref, kp, vp):
        kp[0:BLK, :] = jnp.zeros((BLK, 128), BF16)
        vp[0:BLK, :] = jnp.zeros((BLK, 128), BF16)
        kp[BLK:, :] = k_ref[...].astype(BF16)
        vp[BLK:, :] = v_ref[...].astype(BF16)
        lo = lax.broadcasted_iota(jnp.int32, (BLK, 128), 1) < 64
        band, cur, distf = _swa_masks()

        def blk(i, carry):
            st = pl.multiple_of(i * BLK, BLK)
            kk = kp[pl.ds(st, 2 * BLK), :]
            vv = vp[pl.ds(st, 2 * BLK), :]
            valid = band & (cur | (i > 0))
            for b in range(2):
                half = lo if b == 0 else ~lo
                qb = q_ref[pl.ds(st, BLK), b * 128:(b + 1) * 128]
                outs = []
                for pos in range(2):
                    h = 2 * b + pos
                    qm = jnp.where(half, _to_half(qb, pos, b), 0.0).astype(BF16)
                    p, _ = _swa_scores(qm, kk, valid, distf, SLOPES[h], sink_ref[l, h])
                    outs.append(_to_half(_dot(p.astype(BF16), vv), pos, b))
                o_ref[pl.ds(st, BLK), b * 128:(b + 1) * 128] = jnp.where(lo, outs[0], outs[1])
            return carry

        lax.fori_loop(0, nb, blk, 0, unroll=2)

    return _pc(body, name=name, grid=(1,),
               in_specs=[BS((S, 256), lambda i: (0, C_QS // 256)), BS((S, 128), lambda i: (0, C_KS // 128)),
                         BS((S, 128), lambda i: (0, C_VS // 128)), BS(memory_space=pltpu.SMEM)],
               out_specs=BS((S, 256), lambda i: (0, 0)),
               out_shape=SDS((S, 256), F32),
               scratch=[pltpu.VMEM((S + BLK, 128), BF16), pltpu.VMEM((S + BLK, 128), BF16)])(proj, proj, proj, sinks)


def _swa_bwd(proj, sinks, dyd, l, name):
    S = proj.shape[0]
    nb = S // BLK

    def body(q_ref, k_ref, v_ref, sink_ref, do_ref, dq_ref, dk_ref, dv_ref, dsink_ref, kp, vp, dkp, dvp):
        kp[0:BLK, :] = jnp.zeros((BLK, 128), BF16)
        vp[0:BLK, :] = jnp.zeros((BLK, 128), BF16)
        kp[BLK:, :] = k_ref[...].astype(BF16)
        vp[BLK:, :] = v_ref[...].astype(BF16)
        dkp[...] = jnp.zeros_like(dkp)
        dvp[...] = jnp.zeros_like(dvp)
        lo = lax.broadcasted_iota(jnp.int32, (BLK, 128), 1) < 64
        lane8 = lax.broadcasted_iota(jnp.int32, (8, 128), 1)
        band, cur, distf = _swa_masks()

        def blk(i, dsink):
            st = pl.multiple_of(i * BLK, BLK)
            kk = kp[pl.ds(st, 2 * BLK), :]
            vv = vp[pl.ds(st, 2 * BLK), :]
            valid = band & (cur | (i > 0))
            dkk = jnp.zeros((2 * BLK, 128), F32)
            dvv = jnp.zeros((2 * BLK, 128), F32)
            for b in range(2):
                half = lo if b == 0 else ~lo
                qb = q_ref[pl.ds(st, BLK), b * 128:(b + 1) * 128]
                dob = do_ref[pl.ds(st, BLK), b * 128:(b + 1) * 128]
                dqs = []
                for pos in range(2):
                    h = 2 * b + pos
                    qm = jnp.where(half, _to_half(qb, pos, b), 0.0).astype(BF16)
                    dom = jnp.where(half, _to_half(dob, pos, b), 0.0).astype(BF16)
                    p, psink = _swa_scores(qm, kk, valid, distf, SLOPES[h], sink_ref[l, h])
                    dp = _dot_nt(dom, vv)
                    dvv = dvv + _dot_tn(p.astype(BF16), dom)
                    delta = jnp.sum(p * dp, axis=-1, keepdims=True)
                    dsink = dsink + jnp.where(lane8 == h, -jnp.sum(psink * delta), 0.0)
                    dsc = (p * (dp - delta) * SWA_SCALE).astype(BF16)
                    dqs.append(_to_half(_dot(dsc, kk), pos, b))
                    dkk = dkk + _dot_tn(dsc, qm)
                dq_ref[pl.ds(st, BLK), b * 128:(b + 1) * 128] = jnp.where(lo, dqs[0], dqs[1]).astype(BF16)
            dkp[pl.ds(st, 2 * BLK), :] += dkk
            dvp[pl.ds(st, 2 * BLK), :] += dvv
            return dsink

        dsink_ref[...] = lax.fori_loop(0, nb, blk, jnp.zeros((8, 128), F32), unroll=2)
        dk_ref[...] = dkp[BLK:, :].astype(BF16)
        dv_ref[...] = dvp[BLK:, :].astype(BF16)

    return _pc(body, name=name, grid=(1,),
               in_specs=[BS((S, 256), lambda i: (0, C_QS // 256)), BS((S, 128), lambda i: (0, C_KS // 128)),
                         BS((S, 128), lambda i: (0, C_VS // 128)), BS(memory_space=pltpu.SMEM),
                         BS((S, 256), lambda i: (0, 3))],
               out_specs=[BS((S, 256), lambda i: (0, 0)), BS((S, 128), lambda i: (0, 0)), BS((S, 128), lambda i: (0, 0)),
                          BS((8, 128), lambda i: (0, 0))],
               out_shape=[SDS((S, 256), BF16), SDS((S, 128), BF16), SDS((S, 128), BF16), SDS((8, 128), F32)],
               scratch=[pltpu.VMEM((S + BLK, 128), BF16), pltpu.VMEM((S + BLK, 128), BF16),
                        pltpu.VMEM((S + BLK, 128), F32), pltpu.VMEM((S + BLK, 128), F32)])(proj, proj, proj, sinks, dyd)


def _down(x, k, t):
    return jnp.where(t >= k, pltpu.roll(x, k, axis=0), 0.0)


def _up(x, k, t):
    n = x.shape[0]
    return jnp.where(t < n - k, pltpu.roll(x, n - k, axis=0), 0.0)


def _conv(proj, w8, l, name):
    S = proj.shape[0]

    def body(gb_ref, gc_ref, u_ref, w_ref, y_ref):
        t = lax.broadcasted_iota(jnp.int32, (S, 128), 0)
        z = gc_ref[...] * u_ref[...]
        c = w_ref[2:3, :] * z + w_ref[1:2, :] * _down(z, 1, t) + w_ref[0:1, :] * _down(z, 2, t)
        y_ref[...] = gb_ref[...] * c

    col = lambda c0: BS((S, 128), lambda i: (0, c0 // 128 + i))
    return _pc(body, name=name, grid=(2,),
               in_specs=[col(C_GB), col(C_GC), col(C_UC), BS((None, 8, 128), lambda i: (l, 0, i))],
               out_specs=BS((S, 128), lambda i: (0, i)), out_shape=SDS((S, 256), F32))(proj, proj, proj, w8)


def _conv_bwd(proj, w8, dycat, l, name):
    S = proj.shape[0]

    def body(gb_ref, gc_ref, u_ref, w_ref, dy_ref, dgb_ref, dgc_ref, du_ref, dw_ref):
        t = lax.broadcasted_iota(jnp.int32, (S, 128), 0)
        gc, u = gc_ref[...], u_ref[...]
        z = gc * u
        z1, z2 = _down(z, 1, t), _down(z, 2, t)
        w0, w1, w2 = w_ref[0:1, :], w_ref[1:2, :], w_ref[2:3, :]
        dy = dy_ref[...]
        dgb_ref[...] = (dy * (w2 * z + w1 * z1 + w0 * z2)).astype(BF16)
        dc = dy * gb_ref[...]
        dz = w2 * dc + w1 * _up(dc, 1, t) + w0 * _up(dc, 2, t)
        dgc_ref[...] = (dz * u).astype(BF16)
        du_ref[...] = (dz * gc).astype(BF16)
        row = lax.broadcasted_iota(jnp.int32, (8, 128), 0)
        sums = [jnp.sum(dc * zz, axis=0, keepdims=True) for zz in (z2, z1, z)]
        dw_ref[...] = jnp.where(row == 0, sums[0], jnp.where(row == 1, sums[1], jnp.where(row == 2, sums[2], 0.0)))

    col = lambda c0: BS((S, 128), lambda i: (0, c0 // 128 + i))
    out = BS((S, 128), lambda i: (0, i))
    return _pc(body, name=name, grid=(2,),
               in_specs=[col(C_GB), col(C_GC), col(C_UC), BS((None, 8, 128), lambda i: (l, 0, i)), col(256)],
               out_specs=[out, out, out, BS((8, 128), lambda i: (0, i))],
               out_shape=[SDS((S, 256), BF16)] * 3 + [SDS((8, 256), F32)])(proj, proj, proj, w8, dycat)


def _pool_parts(u, t, first):
    lo = lax.broadcasted_iota(jnp.int32, u.shape, 1) < 64
    s2 = u + _down(u, 1, t)
    s4 = s2 + _down(s2, 2, t)
    s8 = s4 + _down(s4, 4, t)
    s16 = s8 + _down(s8, 8, t)
    win = jnp.where(lo, jnp.where(first, s2, s8), jnp.where(first, s4, s16))
    wv = jnp.where(lo, jnp.where(first, 2, 8), jnp.where(first, 4, 16))
    cnt = jnp.minimum(t + 1, wv).astype(F32)
    return win, cnt, lo


def _pool(proj, pwd, scale3, l, name):
    S = proj.shape[0]

    def body(u_ref, pw_ref, sc_ref, y_ref):
        t = lax.broadcasted_iota(jnp.int32, (S, 128), 0)
        u = u_ref[...]
        win, cnt, _ = _pool_parts(u, t, pl.program_id(0) == 0)
        pooled = win / cnt - u
        y_ref[...] = _dot(pooled.astype(BF16), pw_ref[...]) * sc_ref[...]

    return _pc(body, name=name, grid=(2,),
               in_specs=[BS((S, 128), lambda i: (0, C_UP // 128 + i)), BS((None, 128, 128), lambda i: (l, i, 0)),
                         BS((None, 1, 128), lambda i: (l, 0, i))],
               out_specs=BS((S, 128), lambda i: (0, i)), out_shape=SDS((S, 256), F32))(proj, pwd, scale3)


def _pool_bwd(proj, pwd, scale3, dycat, l, name):
    S = proj.shape[0]

    def body(u_ref, pw_ref, sc_ref, dy_ref, du_ref, dpw_ref, dsc_ref):
        t = lax.broadcasted_iota(jnp.int32, (S, 128), 0)
        first = pl.program_id(0) == 0
        u = u_ref[...]
        win, cnt, lo = _pool_parts(u, t, first)
        pooled = (win / cnt - u).astype(BF16)
        pw = pw_ref[...]
        dy = dy_ref[...]
        dsc_ref[...] = jnp.broadcast_to(jnp.sum(dy * _dot(pooled, pw), axis=0, keepdims=True), (8, 128))
        dmb = (dy * sc_ref[...]).astype(BF16)
        dpw_ref[...] = _dot_tn(pooled, dmb)
        dpooled = _dot_nt(dmb, pw)
        a1 = dpooled / cnt
        a2 = a1 + _up(a1, 1, t)
        a4 = a2 + _up(a2, 2, t)
        a8 = a4 + _up(a4, 4, t)
        a16 = a8 + _up(a8, 8, t)
        dwin = jnp.where(lo, jnp.where(first, a2, a8), jnp.where(first, a4, a16))
        du_ref[...] = (dwin - dpooled).astype(BF16)

    return _pc(body, name=name, grid=(2,),
               in_specs=[BS((S, 128), lambda i: (0, C_UP // 128 + i)), BS((None, 128, 128), lambda i: (l, i, 0)),
                         BS((None, 1, 128), lambda i: (l, 0, i)), BS((S, 128), lambda i: (0, 4 + i))],
               out_specs=[BS((S, 128), lambda i: (0, i)), BS((128, 128), lambda i: (i, 0)), BS((8, 128), lambda i: (0, i))],
               out_shape=[SDS((S, 256), BF16), SDS((256, 128), F32), SDS((8, 256), F32)])(proj, pwd, scale3, dycat)


def _adamw(w, g, m, v, name):
    n, a, b = w.shape
    tr = _row_tile(a, b)

    def body(w_ref, g_ref, m_ref, v_ref, d_ref, nm_ref, nv_ref):
        gv = g_ref[...]
        m_new = B1 * m_ref[...] + (1.0 - B1) * gv
        v_new = B2 * v_ref[...] + (1.0 - B2) * (gv * gv)
        m_hat = m_new / (1.0 - B1 ** STEP)
        v_hat = v_new / (1.0 - B2 ** STEP)
        d_ref[...] = -LR * (m_hat / (jnp.sqrt(v_hat) + ADAM_EPS) + WD * w_ref[...])
        nm_ref[...] = m_new
        nv_ref[...] = v_new

    sp = BS((None, tr, b), lambda i, t: (i, t, 0))
    return _pc(body, name=name, grid=(n, a // tr), in_specs=[sp] * 4, out_specs=[sp] * 3,
               out_shape=[SDS((n, a, b), F32)] * 3)(w, g, m, v)


def _prefetch_call(body, name, grid, in_specs, out_specs, out_shape):
    gs = pltpu.PrefetchScalarGridSpec(num_scalar_prefetch=1, grid=grid, in_specs=in_specs, out_specs=out_specs)
    return pl.pallas_call(body, name=name, grid_spec=gs, out_shape=out_shape, compiler_params=_params(len(grid)))


def _place(w, kc, dtype, name):
    _, a, b = w.shape

    def body(kc_ref, w_ref, o_ref):
        o_ref[...] = w_ref[...].astype(dtype)

    return _prefetch_call(body, name, (2,), [BS((None, a, b), lambda l, kc: (l, 0, 0))],
                          BS((None, None, a, b), lambda l, kc: (l, kc[0], 0, 0)), SDS((2, 4, a, b), dtype))(kc, w)


def _pair_sum(g, got, kc, name):
    _, _, a, b = g.shape
    tr = _row_tile(a, b)

    def body(kc_ref, a_ref, b_ref, t32_ref, t16_ref):
        s = a_ref[...] + b_ref[...]
        t32_ref[...] = s
        t16_ref[...] = s.astype(BF16)

    sp = BS((None, tr, b), lambda k, t, kc: (k, t, 0))
    return _prefetch_call(body, name, (4, a // tr),
                          [BS((None, None, tr, b), lambda k, t, kc: (kc[1], k, t, 0)), sp], [sp, sp],
                          [SDS((4, a, b), F32), SDS((4, a, b), BF16)])(kc, g, got)


def _chip_sum(t32, got3, kc, name):
    _, a, b = t32.shape
    tr = _row_tile(a, b)

    def body(kc_ref, a_ref, b_ref, u_ref):
        u_ref[...] = ((a_ref[...] + b_ref[0].astype(F32)) + b_ref[1].astype(F32)) + b_ref[2].astype(F32)

    return _prefetch_call(body, name, (a // tr,),
                          [BS((None, tr, b), lambda t, kc: (kc[0], t, 0)), BS((3, tr, b), lambda t, kc: (0, t, 0))],
                          BS((None, tr, b), lambda t, kc: (kc[1], t, 0)), SDS((2, a, b), F32))(kc, t32, got3)


def _me():
    return lax.axis_index("x"), lax.axis_index("y"), lax.axis_index("c")


def _other_chips(x, y):
    return [(1 - x, y), (x, 1 - y), (1 - x, 1 - y)]


ANY = BS(memory_space=pl.ANY)
COMM_PARAMS = pltpu.CompilerParams(has_side_effects=True)


def _gather(arrs, name):
    n = len(arrs)
    split = [a.shape[2] % 32 == 0 for a in arrs]

    def body(*refs):
        outs, send_sems, recv_sems = refs[n:2 * n], refs[2 * n], refs[2 * n + 1]
        x, y, c = _me()
        me, sib = (x, y, c), (x, y, 1 - c)
        xn, yn, dg = _other_chips(x, y)

        def blk(t, chip, layer, half=None):
            r = outs[t].at[layer, 2 * chip[0] + chip[1]]
            if half is None:
                return r
            rows = arrs[t].shape[2] // 2
            return r.at[pl.ds(half * rows, rows)]

        def copy(t, k, ref, to):
            return pltpu.make_async_remote_copy(src_ref=ref, dst_ref=ref, send_sem=send_sems.at[7 * t + k],
                                                recv_sem=recv_sems.at[7 * t + k], device_id=to, device_id_type=MESH)

        sent = []

        def send(cp):
            cp.start()
            sent.append(cp)

        for t in range(n):
            send(copy(t, 0, blk(t, (x, y), c), (*xn, c)))
            send(copy(t, 1, blk(t, (x, y), c), (*yn, c)))
            if not split[t]:
                send(copy(t, 2, blk(t, (x, y), c), (*dg, c)))
        for t in range(n):
            copy(t, 0, blk(t, xn, c), me).wait_recv()
            if split[t]:
                send(copy(t, 2, blk(t, xn, c, 0), (*yn, c)))
            send(copy(t, 4, blk(t, xn, c), sib))
            copy(t, 1, blk(t, yn, c), me).wait_recv()
            if split[t]:
                send(copy(t, 3, blk(t, yn, c, 1), (*xn, c)))
            send(copy(t, 5, blk(t, yn, c), sib))
        for t in range(n):
            if split[t]:
                copy(t, 2, blk(t, dg, c, 0), me).wait_recv()
                copy(t, 3, blk(t, dg, c, 1), me).wait_recv()
            else:
                copy(t, 2, blk(t, dg, c), me).wait_recv()
            send(copy(t, 6, blk(t, dg, c), sib))
        for t in range(n):
            for k, chip in ((4, xn), (5, yn), (6, dg)):
                copy(t, k, blk(t, chip, 1 - c), me).wait_recv()
        for cp in sent:
            cp.wait_send()

    return pl.pallas_call(body, name=name, out_shape=[SDS(a.shape, a.dtype) for a in arrs],
                          in_specs=[ANY] * n, out_specs=[ANY] * n, input_output_aliases={t: t for t in range(n)},
                          scratch_shapes=[pltpu.SemaphoreType.DMA((7 * n,)), pltpu.SemaphoreType.DMA((7 * n,))],
                          compiler_params=COMM_PARAMS)(*arrs)


def _swap_layers(gs, name):
    n = len(gs)

    def body(*refs):
        ins, outs, send_sems, recv_sems = refs[:n], refs[n:2 * n], refs[2 * n], refs[2 * n + 1]
        x, y, c = _me()
        cps = [pltpu.make_async_remote_copy(src_ref=ins[t].at[1 - c], dst_ref=outs[t], send_sem=send_sems.at[t],
                                            recv_sem=recv_sems.at[t], device_id=(x, y, 1 - c), device_id_type=MESH)
               for t in range(n)]
        for cp in cps:
            cp.start()
        for cp in cps:
            cp.wait()

    return pl.pallas_call(body, name=name, out_shape=[SDS(g.shape[1:], g.dtype) for g in gs],
                          in_specs=[ANY] * n, out_specs=[ANY] * n,
                          scratch_shapes=[pltpu.SemaphoreType.DMA((n,)), pltpu.SemaphoreType.DMA((n,))],
                          compiler_params=COMM_PARAMS)(*gs)


def _exchange_chips(ts, name):
    n = len(ts)

    def body(*refs):
        ins, outs, send_sems, recv_sems = refs[:n], refs[n:2 * n], refs[2 * n], refs[2 * n + 1]
        x, y, c = _me()
        cps = [pltpu.make_async_remote_copy(src_ref=ins[t].at[2 * cx + cy], dst_ref=outs[t].at[j],
                                            send_sem=send_sems.at[3 * t + j], recv_sem=recv_sems.at[3 * t + j],
                                            device_id=(cx, cy, c), device_id_type=MESH)
               for j, (cx, cy) in enumerate(_other_chips(x, y)) for t in range(n)]
        for cp in cps:
            cp.start()
        for cp in cps:
            cp.wait()

    return pl.pallas_call(body, name=name, out_shape=[SDS((3,) + t.shape[1:], t.dtype) for t in ts],
                          in_specs=[ANY] * n, out_specs=[ANY] * n,
                          scratch_shapes=[pltpu.SemaphoreType.DMA((3 * n,)), pltpu.SemaphoreType.DMA((3 * n,))],
                          compiler_params=COMM_PARAMS)(*ts)


def _join_layers(us, name):
    n = len(us)

    def body(*refs):
        outs, send_sems, recv_sems = refs[n:2 * n], refs[2 * n], refs[2 * n + 1]
        x, y, c = _me()
        cps = [pltpu.make_async_remote_copy(src_ref=outs[t].at[c], dst_ref=outs[t].at[c], send_sem=send_sems.at[t],
                                            recv_sem=recv_sems.at[t], device_id=(x, y, 1 - c), device_id_type=MESH)
               for t in range(n)]
        for cp in cps:
            cp.start()
        for cp in cps:
            cp.wait()

    return pl.pallas_call(body, name=name, out_shape=[SDS(u.shape, u.dtype) for u in us],
                          in_specs=[ANY] * n, out_specs=[ANY] * n, input_output_aliases={t: t for t in range(n)},
                          scratch_shapes=[pltpu.SemaphoreType.DMA((n,)), pltpu.SemaphoreType.DMA((n,))],
                          compiler_params=COMM_PARAMS)(*us)


def _allsum_small(v, name):
    M = v.shape[0]

    def body(x_ref, o_ref, all_ref, send_sems, recv_sems, local_sem):
        x, y, c = _me()
        me, sib = (x, y, c), (x, y, 1 - c)
        chips = _other_chips(x, y)

        def rows(px, py, pc):
            return all_ref.at[pl.ds((4 * px + 2 * py + pc) * M, M), :]

        def copy(k, block, to, src=None):
            return pltpu.make_async_remote_copy(src_ref=rows(*block) if src is None else src, dst_ref=rows(*block),
                                                send_sem=send_sems.at[k], recv_sem=recv_sems.at[k],
                                                device_id=to, device_id_type=MESH)

        mine = pltpu.make_async_copy(x_ref, rows(*me), local_sem)
        mine.start()
        first = [copy(0, me, sib, src=x_ref)]
        first += [copy(1 + j, me, (*chip, c), src=x_ref) for j, chip in enumerate(chips)]
        for cp in first:
            cp.start()
        passed = [copy(4 + j, (*chip, c), sib) for j, chip in enumerate(chips)]
        for j, chip in enumerate(chips):
            copy(1 + j, (*chip, c), me).wait_recv()
            passed[j].start()
        copy(0, sib, me).wait_recv()
        for j, chip in enumerate(chips):
            copy(4 + j, (*chip, 1 - c), me).wait_recv()
        for cp in first + passed:
            cp.wait_send()
        mine.wait()
        acc = all_ref[0:M, :]
        for d in range(1, 8):
            acc = acc + all_ref[d * M:(d + 1) * M, :]
        o_ref[...] = acc

    vm = BS(memory_space=pltpu.VMEM)
    return pl.pallas_call(body, name=name, out_shape=SDS((M, LANES), F32), in_specs=[vm], out_specs=vm,
                          scratch_shapes=[pltpu.VMEM((8 * M, LANES), F32), pltpu.SemaphoreType.DMA((7,)),
                                          pltpu.SemaphoreType.DMA((7,)), pltpu.SemaphoreType.DMA],
                          compiler_params=pltpu.CompilerParams(has_side_effects=True, vmem_limit_bytes=VMEM_LIMIT))(v)


BIG = ("w_in", "w_o", "w_gate_up", "w_down", "w_uq", "w_ukv")
TINY = ("conv_w",)
REPL = ("attn_norm", "mla_q_norm", "mla_kv_norm", "pool_w", "pool_scale", "swa_sinks", "mix_norm", "ffn_norm",
        "final_norm")
ORDER = ("attn_norm", "w_in", "mla_q_norm", "w_uq", "mla_kv_norm", "w_ukv", "conv_w", "pool_w", "pool_scale",
         "swa_sinks", "mix_norm", "w_o", "ffn_norm", "w_gate_up", "w_down", "final_norm")


def _rows8(shape):
    return -(-int(np.prod(shape)) // (8 * LANES)) * 8


def _pack(arrs):
    parts = []
    for a in arrs:
        r = _rows8(a.shape)
        parts.append(jnp.pad(a.reshape(-1), (0, r * LANES - a.size)).reshape(r, LANES))
    return jnp.concatenate(parts, axis=0)


def _unpack(buf, shapes):
    out, r0 = [], 0
    for s in shapes:
        n, r = int(np.prod(s)), _rows8(s)
        rows = buf[r0:r0 + r]
        out.append(rows.reshape(s) if n == r * LANES else rows.reshape(-1)[:n].reshape(s))
        r0 += r
    return out


def _cols_joined(g):
    return jnp.transpose(g, (0, 2, 1, 3)).reshape(g.shape[0], g.shape[2], 4 * g.shape[3])


def _cols_split(w):
    n, a, b4 = w.shape
    return jnp.transpose(w.reshape(n, a, 4, b4 // 4), (0, 2, 1, 3))


def _rope_tables(S):
    inv = 1.0 / (10000.0 ** (jnp.arange(0, 32, 2, dtype=F32) / 32))
    ang = jnp.arange(S, dtype=F32)[:, None] * inv[None, :]
    cos, sin = jnp.cos(ang), jnp.sin(ang)
    z = lambda w: jnp.zeros((S, w), F32)
    tc = jnp.concatenate([jnp.ones((S, 64), F32), cos, cos, jnp.ones((S, 32), F32)], axis=1)
    ts1 = jnp.concatenate([z(64), -sin, z(48)], axis=1)
    ts2 = jnp.concatenate([z(80), sin, z(32)], axis=1)
    return tc, ts1, ts2


def _pad_w_in(w):
    z = lambda n: jnp.zeros(w.shape[:-1] + (n,), w.dtype)
    return jnp.concatenate([w[..., 0:384], z(64), w[..., 384:416], z(32), w[..., 416:1952]], axis=-1)


def _unpad_w_in(d):
    return jnp.concatenate([d[..., 0:384], d[..., 448:480], d[..., 512:2048]], axis=-1)


def _pad_heads(w, src, offs):
    cols = []
    for h in range(HEADS):
        src0, n = src[h]
        z = lambda k: jnp.zeros(w.shape[:-1] + (k,), w.dtype)
        cols += [z(offs[h]), w[..., src0:src0 + n], z(128 - offs[h] - n)]
    return jnp.concatenate(cols, axis=-1)


UQ_SRC = [(h * 96, 96) for h in range(HEADS)]
KN_SRC = [(h * 128, 64) for h in range(HEADS)]
V_SRC = [(h * 128 + 64, 64) for h in range(HEADS)]
ZERO_OFF = [0] * HEADS
V_OFF = [(h % 2) * 64 for h in range(HEADS)]


def _unpad_heads(d, src, offs):
    return [d[..., h * 128 + offs[h]: h * 128 + offs[h] + src[h][1]] for h in range(HEADS)]


def kernel(x, attn_norm, w_in, mla_q_norm, w_uq, mla_kv_norm, w_ukv, conv_w, pool_w, pool_scale, swa_sinks, mix_norm, w_o, ffn_norm, w_gate_up, w_down, final_norm, loss_target, m_attn_norm, m_w_in, m_mla_q_norm, m_w_uq, m_mla_kv_norm, m_w_ukv, m_conv_w, m_pool_w, m_pool_scale, m_swa_sinks, m_mix_norm, m_w_o, m_ffn_norm, m_w_gate_up, m_w_down, m_final_norm, v_attn_norm, v_w_in, v_mla_q_norm, v_w_uq, v_mla_kv_norm, v_w_ukv, v_conv_w, v_pool_w, v_pool_scale, v_swa_sinks, v_mix_norm, v_w_o, v_ffn_norm, v_w_gate_up, v_w_down, v_final_norm):
    W = dict(attn_norm=attn_norm, w_in=w_in, mla_q_norm=mla_q_norm, w_uq=w_uq, mla_kv_norm=mla_kv_norm, w_ukv=w_ukv,
             conv_w=conv_w, pool_w=pool_w, pool_scale=pool_scale, swa_sinks=swa_sinks, mix_norm=mix_norm, w_o=w_o,
             ffn_norm=ffn_norm, w_gate_up=w_gate_up, w_down=w_down, final_norm=final_norm)
    M1 = dict(attn_norm=m_attn_norm, w_in=m_w_in, mla_q_norm=m_mla_q_norm, w_uq=m_w_uq, mla_kv_norm=m_mla_kv_norm,
              w_ukv=m_w_ukv, conv_w=m_conv_w, pool_w=m_pool_w, pool_scale=m_pool_scale, swa_sinks=m_swa_sinks,
              mix_norm=m_mix_norm, w_o=m_w_o, ffn_norm=m_ffn_norm, w_gate_up=m_w_gate_up, w_down=m_w_down,
              final_norm=m_final_norm)
    V2 = dict(attn_norm=v_attn_norm, w_in=v_w_in, mla_q_norm=v_mla_q_norm, w_uq=v_w_uq, mla_kv_norm=v_mla_kv_norm,
              w_ukv=v_w_ukv, conv_w=v_conv_w, pool_w=v_pool_w, pool_scale=v_pool_scale, swa_sinks=v_swa_sinks,
              mix_norm=v_mix_norm, w_o=v_w_o, ffn_norm=v_ffn_norm, w_gate_up=v_w_gate_up, w_down=v_w_down,
              final_norm=v_final_norm)
    S = x.shape[1]
    xc, yc, cc = _me()
    chip = 2 * xc + yc
    kc = jnp.stack([chip, cc]).astype(jnp.int32)

    names = ("w_in", "w_uq", "w_ukv", "w_o", "w_gate_up", "w_down", "conv_w")
    placed = [_place(W[n], kc, F32 if n == "conv_w" else BF16, f"place_{n}") for n in names]
    gi, gq, gkv, go, gu4, gd, gcv = _gather(placed, "gather_weights")
    win_p = _pad_w_in(_cols_joined(gi))
    wuq_p = _pad_heads(_cols_joined(gq), UQ_SRC, ZERO_OFF)
    wukv = _cols_joined(gkv)
    wk_p = _pad_heads(wukv, KN_SRC, ZERO_OFF)
    wv_p = _pad_heads(wukv, V_SRC, V_OFF)
    conv8 = jnp.pad(_cols_joined(gcv), ((0, 0), (0, 5), (0, 0)))
    wo = go.reshape(2, D, D)
    wdown = gd.reshape(2, D_FF, D)
    pwd = jnp.concatenate([jnp.concatenate(
        [jnp.pad(pool_w[:, 2 * b], ((0, 0), (0, 0), (0, 64))), jnp.pad(pool_w[:, 2 * b + 1], ((0, 0), (0, 0), (64, 0)))],
        axis=1) for b in range(2)], axis=1).astype(BF16)
    tabs = _rope_tables(S)
    g_attn, g_q, g_kv, g_mix, g_ffn, g_ps = (_g3(W[n]) for n in ("attn_norm", "mla_q_norm", "mla_kv_norm", "mix_norm",
                                                                  "ffn_norm", "pool_scale"))

    xs = [x[0]]
    saved = []
    for l in range(DEPTH):
        x0 = xs[-1]
        proj, h = _norm_mm(x0, g_attn, l, win_p, _wspec_in(l), D_INP, D_INP, F32, f"in_proj{l}")
        q, k, v, kt, vt = _mla_prep(proj, g_q, g_kv, wuq_p, wk_p, wv_p, tabs, l, f"mla_prep{l}")
        ya, lse = _mla_attn(q, k, vt, f"mla_attn{l}")
        yb = _conv(proj, conv8, l, f"conv{l}")
        ycp = _pool(proj, pwd, g_ps, l, f"pool{l}")
        yd = _swa(proj, swa_sinks, l, f"swa{l}")
        x1, ycat, mixed = _mix_out(x0, ya, yb, ycp, yd, g_mix, wo, l, f"mix_out{l}")
        gu, h2 = _norm_mm(x1, g_ffn, l, gu4, _wspec_gu(l), 2 * D_FF, 2 * D_FF // 4, BF16, f"gate_up{l}")
        x2, act = _swiglu_mm_res(x1, gu, wdown, l, f"down{l}")
        saved.append(dict(x0=x0, proj=proj, h=h, q=q, k=k, kt=kt, v=v, lse=lse, x1=x1, ycat=ycat, mixed=mixed,
                          gu=gu, h2=h2, act=act))
        xs.append(x2)

    dx, dx16, dg_final, loss_tile = _loss_head(xs[-1], final_norm.reshape(1, D), loss_target[0], "loss_head")
    loss = lax.psum(loss_tile[0, 0] * (0.5 / D), ("x", "y", "c"))

    G = {n: [None] * DEPTH for n in ("w_uq", "w_ukv") + TINY + REPL if n != "final_norm"}
    gw_in = gw_o = gw_gu = gw_down = None
    for l in reversed(range(DEPTH)):
        sv = saved[l]
        dgu = _bwd_down(dx16, wdown, sv["gu"], l, f"down_bwd{l}")
        gw_down = _mm_tn(sv["act"], dx16, l, gw_down, f"dw_down{l}")
        gw_gu = _mm_tn(sv["h2"], dgu, l, gw_gu, f"dw_gate_up{l}", split4=True)
        dx1, dx1_16, dg = _mm_nt_normbwd(dgu, gu4, l, sv["x1"], g_ffn, dx, 1, f"gate_up_bwd{l}")
        G["ffn_norm"][l] = dg[0]
        gw_o = _mm_tn(sv["mixed"], dx1_16, l, gw_o, f"dw_o{l}")
        dycat, dg = _mm_nt_normbwd(dx1_16, wo.reshape(2, 1, D, D), l, sv["ycat"], g_mix, None, 4, f"mix_bwd{l}")
        G["mix_norm"][l] = dg[0]

        proj = sv["proj"]
        delta = _mla_delta(dycat, sv["ycat"], f"mla_delta{l}")
        dq, dk, dv = _mla_attn_bwd(sv["q"], sv["k"], sv["kt"], sv["v"], dycat, sv["lse"], delta, f"mla_attn_bwd{l}")
        dcq, dckv, dkr, dwuq, dwk, dwv, dgq, dgkv = _mla_prep_bwd(
            dq, dk, dv, proj, g_q, g_kv, wuq_p, wk_p, wv_p, tabs, l, f"mla_prep_bwd{l}")
        dgb, dgc, duc, dcw = _conv_bwd(proj, conv8, dycat, l, f"conv_bwd{l}")
        dup, dpw, dps = _pool_bwd(proj, pwd, g_ps, dycat, l, f"pool_bwd{l}")
        dqs, dks, dvs, dsink = _swa_bwd(proj, swa_sinks, dycat, l, f"swa_bwd{l}")
        dproj = jnp.concatenate([dcq, dckv, dkr, dgb, dgc, duc, dup, dqs, dks, dvs], axis=1)
        gw_in = _mm_tn(sv["h"], dproj, l, gw_in, f"dw_in{l}")
        dx, dx16, dg = _mm_nt_normbwd(dproj, win_p.reshape(2, 1, D, D_INP), l, sv["x0"], g_attn, dx1, 1, f"in_proj_bwd{l}")
        G["attn_norm"][l] = dg[0]
        G["mla_q_norm"][l] = dgq[0]
        G["mla_kv_norm"][l] = dgkv[0]
        G["w_uq"][l] = jnp.concatenate(_unpad_heads(dwuq, UQ_SRC, ZERO_OFF), axis=1)
        kn, vv = _unpad_heads(dwk, KN_SRC, ZERO_OFF), _unpad_heads(dwv, V_SRC, V_OFF)
        G["w_ukv"][l] = jnp.concatenate([t for h in range(HEADS) for t in (kn[h], vv[h])], axis=1)
        G["conv_w"][l] = dcw[0:3]
        G["pool_w"][l] = jnp.stack([dpw[0:64, 0:64], dpw[64:128, 64:128], dpw[128:192, 0:64], dpw[192:256, 64:128]])
        G["pool_scale"][l] = dps[0]
        G["swa_sinks"][l] = dsink[0, 0:4]
    grad_x = dx[None]
    Gl = {n: jnp.stack(G[n]) for n in G}
    Gl["final_norm"] = dg_final[0]

    gbig = [_cols_split(_unpad_w_in(gw_in)), gw_o.reshape(2, 4, D // 4, D), gw_gu, gw_down.reshape(2, 4, D_FF // 4, D),
            _cols_split(Gl["w_uq"]), _cols_split(Gl["w_ukv"])]
    got = _swap_layers(gbig, "rs_swap_cores")
    pairs = [_pair_sum(g, o, kc, f"rs_pair_sum_{n}") for g, o, n in zip(gbig, got, BIG)]
    got3 = _exchange_chips([p[1] for p in pairs], "rs_exchange_chips")
    us = [_chip_sum(p[0], o3, kc, f"rs_chip_sum_{n}") for p, o3, n in zip(pairs, got3, BIG)]
    gsum = _join_layers(us, "rs_join_cores")
    res = {}
    for n, g in zip(BIG, gsum):
        d_, m_, v_ = _adamw(W[n], g, M1[n], V2[n], f"adamw_{n}")
        res["g", n], res["d", n], res["m", n], res["v", n] = g, d_, m_, v_

    small = TINY + REPL
    full_shapes = [Gl[n].shape for n in small]
    summed = _unpack(_allsum_small(_pack([Gl[n] for n in small]), "allsum_small"), full_shapes)
    gs = {}
    for n, g in zip(small, summed):
        if n in TINY:
            wdt = W[n].shape[2]
            g = lax.dynamic_slice_in_dim(g, chip * wdt, wdt, axis=2)
        gs[n] = g
    own_shapes = [W[n].shape for n in small]
    pk = lambda src: _pack([src[n] for n in small])[None]
    d_s, m_s, v_s = _adamw(pk(W), pk(gs), pk(M1), pk(V2), "adamw_small")
    for key, buf in (("d", d_s), ("m", m_s), ("v", v_s)):
        for n, a in zip(small, _unpack(buf[0], own_shapes)):
            res[key, n] = a
    for n in small:
        res["g", n] = gs[n]

    return (loss, grad_x, *[res["g", n] for n in ORDER], *[res["d", n] for n in ORDER],
            *[res["m", n] for n in ORDER], *[res["v", n] for n in ORDER])
```

```python
import math

import numpy as np
import jax
import jax.numpy as jnp
from jax import lax
from jax.experimental import pallas as pl
from jax.experimental.pallas import tpu as pltpu

F32, BF16 = jnp.float32, jnp.bfloat16
SDS = jax.ShapeDtypeStruct
BS = pl.BlockSpec
MESH = pl.DeviceIdType.MESH

D = 1024
DEPTH = 2
HEADS = 4
D_FF = 2816
D_INP = 2048
EPS = 1e-6
SWA_WINDOW = 128
BLK = 128
SLOPES = tuple(2.0 ** (-8.0 * (i + 1) / 4) for i in range(4))
QK_SCALE = 1.0 / math.sqrt(96)
SWA_SCALE = 1.0 / math.sqrt(64)
LR, B1, B2, ADAM_EPS, WD, STEP = 0.001, 0.9, 0.999, 1e-08, 0.01, 10

LANES = 1024
VMEM_LIMIT = 56 * 1024 * 1024
NEG_INF = float("-inf")

C_CQ, C_CKV, C_KR, C_GB, C_GC, C_UC, C_UP, C_QS, C_KS, C_VS = 0, 256, 384, 512, 768, 1024, 1280, 1536, 1792, 1920


def _params(ngrid):
    return pltpu.CompilerParams(dimension_semantics=("arbitrary",) * ngrid, vmem_limit_bytes=VMEM_LIMIT)


def _pc(body, *, name, grid, in_specs, out_specs, out_shape, scratch=(), aliases=None):
    return pl.pallas_call(
        body, name=name, grid=grid, in_specs=in_specs, out_specs=out_specs, out_shape=out_shape,
        scratch_shapes=scratch, input_output_aliases=aliases or {}, compiler_params=_params(len(grid)))


def _dot(a, b):
    return jnp.dot(a, b, preferred_element_type=F32)


def _dot_nt(a, b):
    return lax.dot_general(a, b, (((1,), (1,)), ((), ())), preferred_element_type=F32)


def _dot_tn(a, b):
    return lax.dot_general(a, b, (((0,), (0,)), ((), ())), preferred_element_type=F32)


def _tile(n, cap):
    if n <= cap:
        return n
    t = cap - cap % 128
    while n % t:
        t -= 128
    return t


def _row_tile(a, b, cap=262144):
    bp = -(-b // 128) * 128
    best = None
    for t in range(8, a + 1, 8):
        if a % t == 0 and t * bp <= cap:
            best = t
    return best if best is not None else a


def _g3(a):
    return a.reshape(a.shape[0], 1, a.shape[1])


def _norm_mm(x, g3, l, w, wspec, N, tn, out_dtype, name):
    S, K = x.shape
    tm = min(512, S)

    def body(x_ref, g_ref, w_ref, y_ref, h_ref):
        @pl.when(pl.program_id(1) == 0)
        def _():
            xv = x_ref[...]
            r = lax.rsqrt(jnp.mean(xv * xv, axis=-1, keepdims=True) + EPS)
            h_ref[...] = (xv * r * g_ref[...]).astype(BF16)

        y_ref[...] = _dot(h_ref[...], w_ref[...]).astype(out_dtype)

    return _pc(body, name=name, grid=(S // tm, N // tn),
               in_specs=[BS((tm, K), lambda i, j: (i, 0)), BS((None, 1, K), lambda i, j: (l, 0, 0)), wspec],
               out_specs=[BS((tm, tn), lambda i, j: (i, j)), BS((tm, K), lambda i, j: (i, 0))],
               out_shape=[SDS((S, N), out_dtype), SDS((S, K), BF16)])(x, g3, w)


def _wspec_in(l):
    return BS((None, D, D_INP), lambda i, j: (l, 0, j))


def _wspec_gu(l):
    return BS((None, None, D, 2 * D_FF // 4), lambda i, j: (l, j, 0, 0))


def _mix_out(x0, ya, yb, yc, yd, gmix3, wo, l, name):
    S = x0.shape[0]
    tm = min(512, S)

    def body(x_ref, ya_ref, yb_ref, yc_ref, yd_ref, g_ref, w_ref, x1_ref, ycat_ref, mixed_ref):
        groups = [ya_ref[...], yb_ref[...], yc_ref[...], yd_ref[...]]
        for gi, yg in enumerate(groups):
            sl = slice(gi * 256, (gi + 1) * 256)
            r = lax.rsqrt(jnp.mean(yg * yg, axis=-1, keepdims=True) + EPS)
            ycat_ref[:, sl] = yg
            mixed_ref[:, sl] = (yg * r * g_ref[:, sl]).astype(BF16)
        x1_ref[...] = x_ref[...] + _dot(mixed_ref[...], w_ref[...])

    row = lambda w: BS((tm, w), lambda i: (i, 0))
    return _pc(body, name=name, grid=(S // tm,),
               in_specs=[row(D), row(256), row(256), row(256), row(256), BS((None, 1, D), lambda i: (l, 0, 0)),
                         BS((None, D, D), lambda i: (l, 0, 0))],
               out_specs=[row(D), row(D), row(D)],
               out_shape=[SDS((S, D), F32), SDS((S, D), F32), SDS((S, D), BF16)])(x0, ya, yb, yc, yd, gmix3, wo)


def _swiglu_mm_res(x1, gu, wdown, l, name):
    S = x1.shape[0]
    tm = min(256, S)

    def body(x_ref, gate_ref, up_ref, w_ref, x2_ref, act_ref):
        gt = gate_ref[...].astype(F32)
        act = (gt * pl.reciprocal(1.0 + jnp.exp(-gt), approx=True) * up_ref[...].astype(F32)).astype(BF16)
        act_ref[...] = act
        x2_ref[...] = x_ref[...] + _dot(act, w_ref[...])

    return _pc(body, name=name, grid=(S // tm,),
               in_specs=[BS((tm, D), lambda i: (i, 0)), BS((tm, D_FF), lambda i: (i, 0)),
                         BS((tm, D_FF), lambda i: (i, 1)), BS((None, D_FF, D), lambda i: (l, 0, 0))],
               out_specs=[BS((tm, D), lambda i: (i, 0)), BS((tm, D_FF), lambda i: (i, 0))],
               out_shape=[SDS((S, D), F32), SDS((S, D_FF), BF16)])(x1, gu, gu, wdown)


def _loss_head(x, g, tgt, name):
    S = x.shape[0]
    tm = min(512, S)

    def body(x_ref, g_ref, t_ref, dx_ref, dx16_ref, dg_ref, loss_ref):
        @pl.when(pl.program_id(0) == 0)
        def _():
            dg_ref[...] = jnp.zeros_like(dg_ref)
            loss_ref[...] = jnp.zeros_like(loss_ref)

        xv = x_ref[...]
        r = lax.rsqrt(jnp.mean(xv * xv, axis=-1, keepdims=True) + EPS)
        xh = xv * r
        gv = g_ref[...]
        diff = xh * gv - t_ref[...]
        loss_ref[...] += jnp.sum(diff * diff)
        dy = diff * (1.0 / D)
        dg_ref[...] += jnp.sum(dy * xh, axis=0, keepdims=True)
        dxh = dy * gv
        dx = r * (dxh - xh * jnp.mean(dxh * xh, axis=-1, keepdims=True))
        dx_ref[...] = dx
        dx16_ref[...] = dx.astype(BF16)

    row = BS((tm, D), lambda i: (i, 0))
    return _pc(body, name=name, grid=(S // tm,),
               in_specs=[row, BS((1, D), lambda i: (0, 0)), row],
               out_specs=[row, row, BS((8, D), lambda i: (0, 0)), BS((8, 128), lambda i: (0, 0))],
               out_shape=[SDS((S, D), F32), SDS((S, D), BF16), SDS((8, D), F32), SDS((8, 128), F32)])(x, g, tgt)


def _mm_tn(a, b, l, prev, name, split4=False):
    S, Ka = a.shape
    N = b.shape[1]
    if split4:
        ta, tn = _tile(Ka, 256), N // 4
        out_shape = SDS((2, 4, Ka, tn), F32)
        out_spec = BS((None, None, ta, tn), lambda j, i: (l, j, i, 0))
    else:
        ta, tn = _tile(Ka, 512), _tile(N, 1024)
        out_shape = SDS((2, Ka, N), F32)
        out_spec = BS((None, ta, tn), lambda j, i: (l, i, j))

    def body(a_ref, b_ref, *rest):
        rest[-1][...] = _dot_tn(a_ref[...], b_ref[...])

    in_specs = [BS((S, ta), lambda j, i: (0, i)), BS((S, tn), lambda j, i: (0, j))]
    args = [a, b]
    if prev is not None:
        in_specs.append(BS(memory_space=pl.ANY))
        args.append(prev)
    return _pc(body, name=name, grid=(N // tn, Ka // ta), in_specs=in_specs, out_specs=out_spec, out_shape=out_shape,
               aliases={2: 0} if prev is not None else None)(*args)


def _bwd_down(dx16, wdown, gu, l, name):
    S = dx16.shape[0]
    tm = min(256, S)

    def body(dx_ref, w_ref, gate_ref, up_ref, dgu_ref):
        dact = _dot_nt(dx_ref[...], w_ref[...])
        gt = gate_ref[...].astype(F32)
        sg = pl.reciprocal(1.0 + jnp.exp(-gt), approx=True)
        dgu_ref[:, 0:D_FF] = (dact * up_ref[...].astype(F32) * (sg * (1.0 + gt * (1.0 - sg)))).astype(BF16)
        dgu_ref[:, D_FF:2 * D_FF] = (dact * (gt * sg)).astype(BF16)

    return _pc(body, name=name, grid=(S // tm,),
               in_specs=[BS((tm, D), lambda i: (i, 0)), BS((None, D_FF, D), lambda i: (l, 0, 0)),
                         BS((tm, D_FF), lambda i: (i, 0)), BS((tm, D_FF), lambda i: (i, 1))],
               out_specs=BS((tm, 2 * D_FF), lambda i: (i, 0)),
               out_shape=SDS((S, 2 * D_FF), BF16))(dx16, wdown, gu, gu)


def _mm_nt_normbwd(dy, w4, l, x, g3, dres, ngroups, name):
    S, K = dy.shape
    _, nk, _, kc = w4.shape
    tm = min(512, S)
    gw = D // ngroups
    has_res = dres is not None

    def body(*refs):
        if has_res:
            dy_ref, w_ref, x_ref, g_ref, res_ref, dx_ref, dx16_ref, dg_ref = refs
        else:
            dy_ref, w_ref, x_ref, g_ref, dx_ref, dg_ref = refs

        @pl.when(pl.program_id(0) == 0)
        def _():
            dg_ref[...] = jnp.zeros_like(dg_ref)

        dh = _dot_nt(dy_ref[:, 0:kc], w_ref[0])
        for k in range(1, nk):
            dh = dh + _dot_nt(dy_ref[:, k * kc:(k + 1) * kc], w_ref[k])
        for gi in range(ngroups):
            sl = slice(gi * gw, (gi + 1) * gw)
            xg = x_ref[:, sl]
            r = lax.rsqrt(jnp.mean(xg * xg, axis=-1, keepdims=True) + EPS)
            xh = xg * r
            dhg = dh[:, sl]
            dg_ref[:, sl] += jnp.sum(dhg * xh, axis=0, keepdims=True)
            dxh = dhg * g_ref[:, sl]
            dxg = r * (dxh - xh * jnp.mean(dxh * xh, axis=-1, keepdims=True))
            if has_res:
                dxg = dxg + res_ref[:, sl]
                dx16_ref[:, sl] = dxg.astype(BF16)
            dx_ref[:, sl] = dxg

    row = BS((tm, D), lambda i: (i, 0))
    in_specs = [BS((tm, K), lambda i: (i, 0)),
                BS((None, nk, D, kc), lambda i: (l, 0, 0, 0), pipeline_mode=pl.Buffered(1)), row,
                BS((None, 1, D), lambda i: (l, 0, 0))]
    args = [dy, w4, x, g3]
    out_specs, out_shape = [row], [SDS((S, D), F32)]
    if has_res:
        in_specs.append(row)
        args.append(dres)
        out_specs.append(row)
        out_shape.append(SDS((S, D), BF16))
    out_specs.append(BS((8, D), lambda i: (0, 0)))
    out_shape.append(SDS((8, D), F32))
    return _pc(body, name=name, grid=(S // tm,), in_specs=in_specs, out_specs=out_specs, out_shape=out_shape)(*args)


def _rope(x, c, s1, s2):
    return x * c + pltpu.roll(x, 112, axis=1) * s1 + pltpu.roll(x, 16, axis=1) * s2


def _rope_t(dy, c, s1, s2):
    return dy * c + pltpu.roll(dy * s1, 16, axis=1) + pltpu.roll(dy * s2, 112, axis=1)


def _mla_prep(proj, gq3, gkv3, wuq, wk, wv, tabs, l, name):
    S = proj.shape[0]
    tm = min(512, S)
    tc, ts1, ts2 = tabs

    def body(cq_ref, ckv_ref, kr_ref, gq_ref, gkv_ref, wuq_ref, wk_ref, wv_ref, c_ref, s1_ref, s2_ref,
             q_ref, k_ref, v_ref, kt_ref, vt_ref):
        c, s1, s2 = c_ref[...], s1_ref[...], s2_ref[...]
        cq = cq_ref[...]
        rq = lax.rsqrt(jnp.mean(cq * cq, axis=-1, keepdims=True) + EPS)
        qa = _dot((cq * rq * gq_ref[...]).astype(BF16), wuq_ref[...])
        ckv = ckv_ref[...]
        rkv = lax.rsqrt(jnp.mean(ckv * ckv, axis=-1, keepdims=True) + EPS)
        ckvn = (ckv * rkv * gkv_ref[...]).astype(BF16)
        ka = _dot(ckvn, wk_ref[...])
        va = _dot(ckvn, wv_ref[...])
        v_ref[...] = va.astype(BF16)
        vt_ref[...] = va.T.astype(BF16)
        krr = _rope(kr_ref[...], c, s1, s2)
        for h in range(HEADS):
            sl = slice(h * 128, (h + 1) * 128)
            q_ref[:, sl] = (_rope(qa[:, sl], c, s1, s2) * QK_SCALE).astype(BF16)
            kh = ka[:, sl] + krr
            k_ref[:, sl] = kh.astype(BF16)
            kt_ref[sl, :] = kh.T.astype(BF16)

    lay = lambda a, b: BS((None, a, b), lambda i: (l, 0, 0))
    tab = BS((tm, 128), lambda i: (i, 0))
    return _pc(body, name=name, grid=(S // tm,),
               in_specs=[BS((tm, 256), lambda i: (i, 0)), BS((tm, 128), lambda i: (i, 2)), BS((tm, 128), lambda i: (i, 3)),
                         lay(1, 256), lay(1, 128), lay(256, 512), lay(128, 512), lay(128, 512), tab, tab, tab],
               out_specs=[BS((tm, 512), lambda i: (i, 0))] * 3 + [BS((512, tm), lambda i: (0, i))] * 2,
               out_shape=[SDS((S, 512), BF16)] * 3 + [SDS((512, S), BF16)] * 2)(
        proj, proj, proj, gq3, gkv3, wuq, wk, wv, tc, ts1, ts2)


def _mla_prep_bwd(dq, dk, dv, proj, gq3, gkv3, wuq, wk, wv, tabs, l, name):
    S = proj.shape[0]
    tm = min(512, S)
    tc, ts1, ts2 = tabs

    def body(dq_ref, dk_ref, dv_ref, cq_ref, ckv_ref, gq_ref, gkv_ref, wuq_ref, wk_ref, wv_ref, c_ref, s1_ref, s2_ref,
             dcq_ref, dckv_ref, dkr_ref, dwuq_ref, dwk_ref, dwv_ref, dgq_ref, dgkv_ref):
        @pl.when(pl.program_id(0) == 0)
        def _():
            for r in (dwuq_ref, dwk_ref, dwv_ref, dgq_ref, dgkv_ref):
                r[...] = jnp.zeros_like(r)

        c, s1, s2 = c_ref[...], s1_ref[...], s2_ref[...]
        dqp = jnp.concatenate(
            [_rope_t(dq_ref[h * 128:(h + 1) * 128, :].T * QK_SCALE, c, s1, s2) for h in range(HEADS)], axis=1).astype(BF16)
        cq = cq_ref[...]
        rq = lax.rsqrt(jnp.mean(cq * cq, axis=-1, keepdims=True) + EPS)
        cqh = cq * rq
        gq_v = gq_ref[...]
        dwuq_ref[...] += _dot_tn((cqh * gq_v).astype(BF16), dqp)
        dcqn = _dot_nt(dqp, wuq_ref[...])
        dgq_ref[...] += jnp.sum(dcqn * cqh, axis=0, keepdims=True)
        dxh = dcqn * gq_v
        dcq_ref[...] = (rq * (dxh - cqh * jnp.mean(dxh * cqh, axis=-1, keepdims=True))).astype(BF16)

        dkb = dk_ref[...].astype(BF16)
        dvb = dv_ref[...].astype(BF16)
        ckv = ckv_ref[...]
        rkv = lax.rsqrt(jnp.mean(ckv * ckv, axis=-1, keepdims=True) + EPS)
        ckh = ckv * rkv
        gkv_v = gkv_ref[...]
        ckvn = (ckh * gkv_v).astype(BF16)
        dwk_ref[...] += _dot_tn(ckvn, dkb)
        dwv_ref[...] += _dot_tn(ckvn, dvb)
        dckvn = _dot_nt(dkb, wk_ref[...]) + _dot_nt(dvb, wv_ref[...])
        dgkv_ref[...] += jnp.sum(dckvn * ckh, axis=0, keepdims=True)
        dyh = dckvn * gkv_v
        dckv_ref[...] = (rkv * (dyh - ckh * jnp.mean(dyh * ckh, axis=-1, keepdims=True))).astype(BF16)
        dks = dk_ref[:, 0:128] + dk_ref[:, 128:256] + dk_ref[:, 256:384] + dk_ref[:, 384:512]
        dkr_ref[...] = _rope_t(dks, c, s1, s2).astype(BF16)

    full = lambda a, b: BS((a, b), lambda i: (0, 0))
    lay = lambda a, b: BS((None, a, b), lambda i: (l, 0, 0))
    tab = BS((tm, 128), lambda i: (i, 0))
    row = lambda w: BS((tm, w), lambda i: (i, 0))
    return _pc(body, name=name, grid=(S // tm,),
               in_specs=[BS((512, tm), lambda i: (0, i)), row(512), row(512), BS((tm, 256), lambda i: (i, 0)),
                         BS((tm, 128), lambda i: (i, 2)),
                         lay(1, 256), lay(1, 128), lay(256, 512), lay(128, 512), lay(128, 512), tab, tab, tab],
               out_specs=[row(256), row(128), row(128), full(256, 512), full(128, 512), full(128, 512),
                          full(8, 256), full(8, 128)],
               out_shape=[SDS((S, 256), BF16), SDS((S, 128), BF16), SDS((S, 128), BF16), SDS((256, 512), F32),
                          SDS((128, 512), F32), SDS((128, 512), F32), SDS((8, 256), F32), SDS((8, 128), F32)])(
        dq, dk, dv, proj, proj, gq3, gkv3, wuq, wk, wv, tc, ts1, ts2)


def _causal_steps(n, q_outer):
    if q_outer:
        pairs = [(i, j) for i in range(n) for j in range(i + 1)]
    else:
        pairs = [(i, j) for j in range(n) for i in range(j, n)]
    return jnp.asarray([p[0] for p in pairs], jnp.int32), jnp.asarray([p[1] for p in pairs], jnp.int32)


def _mla_attn(q, k, vt, gts, name):
    S = q.shape[0]
    t = min(512, S)
    n = S // t
    ng = len(gts)

    qi, kj = _causal_steps(n, True)
    last = qi.shape[0] - 1

    def body(qi_ref, kj_ref, q_ref, k_ref, vt_ref, *rest):
        (ya_ref, lse_ref), g_refs = rest[ng:ng + 2], rest[ng + 2:2 * ng + 2]
        m_sc, l_sc, acc_sc = rest[2 * ng + 2:2 * ng + 5]
        i, j = qi_ref[pl.program_id(1)], kj_ref[pl.program_id(1)]
        if ng:
            phases = _gather_phases(g_refs, [g.shape for g in gts], rest[2 * ng + 5], rest[2 * ng + 6])
            for ph, (pp, ss) in zip(phases[:2], ((0, 0), (1, 0))):
                pl.when((pl.program_id(0) == pp) & (pl.program_id(1) == ss))(ph)

        @pl.when(j == 0)
        def _():
            m_sc[...] = jnp.full_like(m_sc, NEG_INF)
            l_sc[...] = jnp.zeros_like(l_sc)
            acc_sc[...] = jnp.zeros_like(acc_sc)

        def step(masked):
            for hh in range(2):
                sl = slice(hh * 128, (hh + 1) * 128)
                st = _dot_nt(k_ref[:, sl], q_ref[:, sl])
                if masked:
                    key = lax.broadcasted_iota(jnp.int32, (t, t), 0)
                    qry = lax.broadcasted_iota(jnp.int32, (t, t), 1)
                    st = jnp.where(key <= qry, st, NEG_INF)
                m_prev = m_sc[hh]
                m_new = jnp.maximum(m_prev, jnp.max(st, axis=0, keepdims=True))
                p = jnp.exp(st - m_new)
                alpha = jnp.exp(m_prev - m_new)
                l_sc[hh] = alpha * l_sc[hh] + jnp.sum(p, axis=0, keepdims=True)
                acc_sc[hh] = alpha * acc_sc[hh] + _dot(vt_ref[sl, :], p.astype(BF16))
                m_sc[hh] = m_new

        @pl.when(j < i)
        def _():
            step(False)

        @pl.when(j == i)
        def _():
            step(True)
            ya_ref[...] = (acc_sc[0] / l_sc[0] + acc_sc[1] / l_sc[1]).T
            for hh in range(2):
                lse_ref[hh] = m_sc[hh] + jnp.log(l_sc[hh])

        if ng:
            pl.when((pl.program_id(0) == 1) & (pl.program_id(1) == last))(phases[2])

    gs = pltpu.PrefetchScalarGridSpec(
        num_scalar_prefetch=2, grid=(2, qi.shape[0]),
        in_specs=[BS((t, 256), lambda p, s, qi, kj: (qi[s], p)), BS((t, 256), lambda p, s, qi, kj: (kj[s], p)),
                  BS((256, t), lambda p, s, qi, kj: (p, kj[s]))] + [ANY] * ng,
        out_specs=[BS((t, 128), lambda p, s, qi, kj: (qi[s], p)), BS((2, 1, t), lambda p, s, qi, kj: (p, 0, qi[s]))]
        + [ANY] * ng,
        scratch_shapes=[pltpu.VMEM((2, 1, t), F32), pltpu.VMEM((2, 1, t), F32), pltpu.VMEM((2, 128, t), F32)]
        + ([pltpu.SemaphoreType.DMA((7 * ng,)), pltpu.SemaphoreType.DMA((7 * ng,))] if ng else []))
    out = pl.pallas_call(body, name=name, grid_spec=gs,
                         out_shape=[SDS((S, 256), F32), SDS((HEADS, 1, S), F32)] + [SDS(g.shape, g.dtype) for g in gts],
                         input_output_aliases={5 + m: 2 + m for m in range(ng)},
                         compiler_params=pltpu.CompilerParams(dimension_semantics=("arbitrary",) * 2,
                                                              vmem_limit_bytes=VMEM_LIMIT, has_side_effects=bool(ng)))(
        qi, kj, q, k, vt, *gts)
    return out[0], out[1], list(out[2:])


def _mla_delta(dycat, ya, name):
    S = ya.shape[0]
    t = min(512, S)

    def body(do_ref, ya_ref, d_ref):
        prod = do_ref[...] * ya_ref[...]
        for p in range(2):
            pt = prod[:, p * 128:(p + 1) * 128].T
            d_ref[2 * p] = jnp.sum(pt[0:64, :], axis=0, keepdims=True)
            d_ref[2 * p + 1] = jnp.sum(pt[64:128, :], axis=0, keepdims=True)

    return _pc(body, name=name, grid=(S // t,),
               in_specs=[BS((t, 256), lambda i: (i, 0)), BS((t, 256), lambda i: (i, 0))],
               out_specs=BS((HEADS, 1, t), lambda i: (0, 0, i)), out_shape=SDS((HEADS, 1, S), F32))(dycat, ya)


def _mla_attn_bwd(q, k, kt, v, dya, lse, delta, ts, name):
    S = q.shape[0]
    t = min(512, S)
    n = S // t
    nx = len(ts)

    qi, kj = _causal_steps(n, False)
    last = qi.shape[0] - 1

    def body(qi_ref, kj_ref, q_ref, k_ref, kt_ref, v_ref, do_ref, lse_ref, delta_ref, *rest):
        t_refs, (dqt_ref, dk_ref, dv_ref), got_refs = rest[:nx], rest[nx:nx + 3], rest[nx + 3:2 * nx + 3]
        i, j = qi_ref[pl.program_id(1)], kj_ref[pl.program_id(1)]

        def exchange():
            x, y, c = _me()
            send_sems, recv_sems = rest[2 * nx + 3:]
            return [pltpu.make_async_remote_copy(src_ref=t_refs[m].at[2 * cx + cy], dst_ref=got_refs[m].at[jx],
                                                 send_sem=send_sems.at[3 * m + jx], recv_sem=recv_sems.at[3 * m + jx],
                                                 device_id=(cx, cy, c), device_id_type=MESH)
                    for jx, (cx, cy) in enumerate(_other_chips(x, y)) for m in range(nx)]

        if nx:
            @pl.when((pl.program_id(0) == 0) & (pl.program_id(1) == 0))
            def _():
                for cp in exchange():
                    cp.start()

        @pl.when(pl.program_id(1) == 0)
        def _():
            dqt_ref[...] = jnp.zeros_like(dqt_ref)

        @pl.when(i == j)
        def _():
            dk_ref[...] = jnp.zeros_like(dk_ref)
            dv_ref[...] = jnp.zeros_like(dv_ref)

        def step(masked):
            qv = q_ref[...]
            p = jnp.exp(_dot_nt(k_ref[...], qv) - lse_ref[...])
            if masked:
                key = lax.broadcasted_iota(jnp.int32, (t, t), 0)
                qry = lax.broadcasted_iota(jnp.int32, (t, t), 1)
                p = jnp.where(key <= qry, p, 0.0)
            dob = do_ref[...].astype(BF16)
            dv_ref[...] += _dot(p.astype(BF16), dob)
            ds = (p * (_dot_nt(v_ref[...], dob) - delta_ref[...])).astype(BF16)
            dk_ref[...] += _dot(ds, qv)
            cols = pl.ds(pl.multiple_of(i * t, t), t)
            dqt_ref[:, cols] += _dot(kt_ref[...], ds)

        @pl.when(i > j)
        def _():
            step(False)

        @pl.when(i == j)
        def _():
            step(True)

        if nx:
            @pl.when((pl.program_id(0) == HEADS - 1) & (pl.program_id(1) == last))
            def _():
                for cp in exchange():
                    cp.wait()

    qs = BS((t, 128), lambda h, s, qi, kj: (qi[s], h))
    ks = BS((t, 128), lambda h, s, qi, kj: (kj[s], h))
    rowv = BS((None, 1, t), lambda h, s, qi, kj: (h, 0, qi[s]))
    gs = pltpu.PrefetchScalarGridSpec(
        num_scalar_prefetch=2, grid=(HEADS, qi.shape[0]),
        in_specs=[qs, ks, BS((128, t), lambda h, s, qi, kj: (h, kj[s])), ks,
                  BS((t, 128), lambda h, s, qi, kj: (qi[s], h // 2)), rowv, rowv] + [ANY] * nx,
        out_specs=[BS((128, S), lambda h, s, qi, kj: (h, 0)), ks, ks] + [ANY] * nx,
        scratch_shapes=[pltpu.SemaphoreType.DMA((3 * nx,)), pltpu.SemaphoreType.DMA((3 * nx,))] if nx else [])
    out = pl.pallas_call(body, name=name, grid_spec=gs,
                         out_shape=[SDS((512, S), F32), SDS((S, 512), F32), SDS((S, 512), F32)]
                         + [SDS((3,) + m.shape[1:], m.dtype) for m in ts],
                         compiler_params=pltpu.CompilerParams(dimension_semantics=("arbitrary",) * 2,
                                                              vmem_limit_bytes=VMEM_LIMIT, has_side_effects=bool(nx)))(
        qi, kj, q, k, kt, v, dya, lse, delta, *ts)
    return out[0], out[1], out[2], list(out[3:])


def _swa_scores(qm, kk, valid, distf, slope, sink):
    sc = _dot_nt(qm, kk) * SWA_SCALE
    sc = jnp.where(valid, sc - slope * distf, NEG_INF)
    m = jnp.maximum(jnp.max(sc, axis=-1, keepdims=True), sink)
    e = jnp.exp(sc - m)
    esink = jnp.exp(sink - m)
    den = jnp.sum(e, axis=-1, keepdims=True) + esink
    return e / den, esink / den


def _swa_masks():
    r = lax.broadcasted_iota(jnp.int32, (BLK, 2 * BLK), 0)
    c = lax.broadcasted_iota(jnp.int32, (BLK, 2 * BLK), 1)
    dist = r + BLK - c
    return (dist >= 0) & (dist < SWA_WINDOW), c >= BLK, dist.astype(F32)


def _to_half(xb, pos, b):
    return xb if pos == b else pltpu.roll(xb, 64, axis=1)


def _swa(proj, sinks, l, name):
    S = proj.shape[0]
    nb = S // BLK

    def body(q_ref, k_ref, v_ref, sink_ref, o_ref, kp, vp):
        kp[0:BLK, :] = jnp.zeros((BLK, 128), BF16)
        vp[0:BLK, :] = jnp.zeros((BLK, 128), BF16)
        kp[BLK:, :] = k_ref[...].astype(BF16)
        vp[BLK:, :] = v_ref[...].astype(BF16)
        lo = lax.broadcasted_iota(jnp.int32, (BLK, 128), 1) < 64
        band, cur, distf = _swa_masks()

        def blk(i, carry):
            st = pl.multiple_of(i * BLK, BLK)
            kk = kp[pl.ds(st, 2 * BLK), :]
            vv = vp[pl.ds(st, 2 * BLK), :]
            valid = band & (cur | (i > 0))
            for b in range(2):
                half = lo if b == 0 else ~lo
                qb = q_ref[pl.ds(st, BLK), b * 128:(b + 1) * 128]
                outs = []
                for pos in range(2):
                    h = 2 * b + pos
                    qm = jnp.where(half, _to_half(qb, pos, b), 0.0).astype(BF16)
                    p, _ = _swa_scores(qm, kk, valid, distf, SLOPES[h], sink_ref[l, h])
                    outs.append(_to_half(_dot(p.astype(BF16), vv), pos, b))
                o_ref[pl.ds(st, BLK), b * 128:(b + 1) * 128] = jnp.where(lo, outs[0], outs[1])
            return carry

        lax.fori_loop(0, nb, blk, 0, unroll=2)

    return _pc(body, name=name, grid=(1,),
               in_specs=[BS((S, 256), lambda i: (0, C_QS // 256)), BS((S, 128), lambda i: (0, C_KS // 128)),
                         BS((S, 128), lambda i: (0, C_VS // 128)), BS(memory_space=pltpu.SMEM)],
               out_specs=BS((S, 256), lambda i: (0, 0)),
               out_shape=SDS((S, 256), F32),
               scratch=[pltpu.VMEM((S + BLK, 128), BF16), pltpu.VMEM((S + BLK, 128), BF16)])(proj, proj, proj, sinks)


def _swa_bwd(proj, sinks, dyd, l, name):
    S = proj.shape[0]
    nb = S // BLK

    def body(q_ref, k_ref, v_ref, sink_ref, do_ref, dq_ref, dk_ref, dv_ref, dsink_ref, kp, vp, dkp, dvp):
        kp[0:BLK, :] = jnp.zeros((BLK, 128), BF16)
        vp[0:BLK, :] = jnp.zeros((BLK, 128), BF16)
        kp[BLK:, :] = k_ref[...].astype(BF16)
        vp[BLK:, :] = v_ref[...].astype(BF16)
        dkp[...] = jnp.zeros_like(dkp)
        dvp[...] = jnp.zeros_like(dvp)
        lo = lax.broadcasted_iota(jnp.int32, (BLK, 128), 1) < 64
        lane8 = lax.broadcasted_iota(jnp.int32, (8, 128), 1)
        band, cur, distf = _swa_masks()

        def blk(i, dsink):
            st = pl.multiple_of(i * BLK, BLK)
            kk = kp[pl.ds(st, 2 * BLK), :]
            vv = vp[pl.ds(st, 2 * BLK), :]
            valid = band & (cur | (i > 0))
            dkk = jnp.zeros((2 * BLK, 128), F32)
            dvv = jnp.zeros((2 * BLK, 128), F32)
            for b in range(2):
                half = lo if b == 0 else ~lo
                qb = q_ref[pl.ds(st, BLK), b * 128:(b + 1) * 128]
                dob = do_ref[pl.ds(st, BLK), b * 128:(b + 1) * 128]
                dqs = []
                for pos in range(2):
                    h = 2 * b + pos
                    qm = jnp.where(half, _to_half(qb, pos, b), 0.0).astype(BF16)
                    dom = jnp.where(half, _to_half(dob, pos, b), 0.0).astype(BF16)
                    p, psink = _swa_scores(qm, kk, valid, distf, SLOPES[h], sink_ref[l, h])
                    dp = _dot_nt(dom, vv)
                    dvv = dvv + _dot_tn(p.astype(BF16), dom)
                    delta = jnp.sum(p * dp, axis=-1, keepdims=True)
                    dsink = dsink + jnp.where(lane8 == h, -jnp.sum(psink * delta), 0.0)
                    dsc = (p * (dp - delta) * SWA_SCALE).astype(BF16)
                    dqs.append(_to_half(_dot(dsc, kk), pos, b))
                    dkk = dkk + _dot_tn(dsc, qm)
                dq_ref[pl.ds(st, BLK), b * 128:(b + 1) * 128] = jnp.where(lo, dqs[0], dqs[1]).astype(BF16)
            dkp[pl.ds(st, 2 * BLK), :] += dkk
            dvp[pl.ds(st, 2 * BLK), :] += dvv
            return dsink

        dsink_ref[...] = lax.fori_loop(0, nb, blk, jnp.zeros((8, 128), F32), unroll=2)
        dk_ref[...] = dkp[BLK:, :].astype(BF16)
        dv_ref[...] = dvp[BLK:, :].astype(BF16)

    return _pc(body, name=name, grid=(1,),
               in_specs=[BS((S, 256), lambda i: (0, C_QS // 256)), BS((S, 128), lambda i: (0, C_KS // 128)),
                         BS((S, 128), lambda i: (0, C_VS // 128)), BS(memory_space=pltpu.SMEM),
                         BS((S, 256), lambda i: (0, 3))],
               out_specs=[BS((S, 256), lambda i: (0, 0)), BS((S, 128), lambda i: (0, 0)), BS((S, 128), lambda i: (0, 0)),
                          BS((8, 128), lambda i: (0, 0))],
               out_shape=[SDS((S, 256), BF16), SDS((S, 128), BF16), SDS((S, 128), BF16), SDS((8, 128), F32)],
               scratch=[pltpu.VMEM((S + BLK, 128), BF16), pltpu.VMEM((S + BLK, 128), BF16),
                        pltpu.VMEM((S + BLK, 128), F32), pltpu.VMEM((S + BLK, 128), F32)])(proj, proj, proj, sinks, dyd)


def _down(x, k, t):
    return jnp.where(t >= k, pltpu.roll(x, k, axis=0), 0.0)


def _up(x, k, t):
    n = x.shape[0]
    return jnp.where(t < n - k, pltpu.roll(x, n - k, axis=0), 0.0)


def _conv(proj, w8, l, name):
    S = proj.shape[0]

    def body(gb_ref, gc_ref, u_ref, w_ref, y_ref):
        t = lax.broadcasted_iota(jnp.int32, (S, 128), 0)
        z = gc_ref[...] * u_ref[...]
        c = w_ref[2:3, :] * z + w_ref[1:2, :] * _down(z, 1, t) + w_ref[0:1, :] * _down(z, 2, t)
        y_ref[...] = gb_ref[...] * c

    col = lambda c0: BS((S, 128), lambda i: (0, c0 // 128 + i))
    return _pc(body, name=name, grid=(2,),
               in_specs=[col(C_GB), col(C_GC), col(C_UC), BS((None, 8, 128), lambda i: (l, 0, i))],
               out_specs=BS((S, 128), lambda i: (0, i)), out_shape=SDS((S, 256), F32))(proj, proj, proj, w8)


def _conv_bwd(proj, w8, dycat, l, name):
    S = proj.shape[0]

    def body(gb_ref, gc_ref, u_ref, w_ref, dy_ref, dgb_ref, dgc_ref, du_ref, dw_ref):
        t = lax.broadcasted_iota(jnp.int32, (S, 128), 0)
        gc, u = gc_ref[...], u_ref[...]
        z = gc * u
        z1, z2 = _down(z, 1, t), _down(z, 2, t)
        w0, w1, w2 = w_ref[0:1, :], w_ref[1:2, :], w_ref[2:3, :]
        dy = dy_ref[...]
        dgb_ref[...] = (dy * (w2 * z + w1 * z1 + w0 * z2)).astype(BF16)
        dc = dy * gb_ref[...]
        dz = w2 * dc + w1 * _up(dc, 1, t) + w0 * _up(dc, 2, t)
        dgc_ref[...] = (dz * u).astype(BF16)
        du_ref[...] = (dz * gc).astype(BF16)
        row = lax.broadcasted_iota(jnp.int32, (8, 128), 0)
        sums = [jnp.sum(dc * zz, axis=0, keepdims=True) for zz in (z2, z1, z)]
        dw_ref[...] = jnp.where(row == 0, sums[0], jnp.where(row == 1, sums[1], jnp.where(row == 2, sums[2], 0.0)))

    col = lambda c0: BS((S, 128), lambda i: (0, c0 // 128 + i))
    out = BS((S, 128), lambda i: (0, i))
    return _pc(body, name=name, grid=(2,),
               in_specs=[col(C_GB), col(C_GC), col(C_UC), BS((None, 8, 128), lambda i: (l, 0, i)), col(256)],
               out_specs=[out, out, out, BS((8, 128), lambda i: (0, i))],
               out_shape=[SDS((S, 256), BF16)] * 3 + [SDS((8, 256), F32)])(proj, proj, proj, w8, dycat)


def _pool_parts(u, t, first):
    lo = lax.broadcasted_iota(jnp.int32, u.shape, 1) < 64
    s2 = u + _down(u, 1, t)
    s4 = s2 + _down(s2, 2, t)
    s8 = s4 + _down(s4, 4, t)
    s16 = s8 + _down(s8, 8, t)
    win = jnp.where(lo, jnp.where(first, s2, s8), jnp.where(first, s4, s16))
    wv = jnp.where(lo, jnp.where(first, 2, 8), jnp.where(first, 4, 16))
    cnt = jnp.minimum(t + 1, wv).astype(F32)
    return win, cnt, lo


def _pool(proj, pwd, scale3, l, name):
    S = proj.shape[0]

    def body(u_ref, pw_ref, sc_ref, y_ref):
        t = lax.broadcasted_iota(jnp.int32, (S, 128), 0)
        u = u_ref[...]
        win, cnt, _ = _pool_parts(u, t, pl.program_id(0) == 0)
        pooled = win / cnt - u
        y_ref[...] = _dot(pooled.astype(BF16), pw_ref[...]) * sc_ref[...]

    return _pc(body, name=name, grid=(2,),
               in_specs=[BS((S, 128), lambda i: (0, C_UP // 128 + i)), BS((None, 128, 128), lambda i: (l, i, 0)),
                         BS((None, 1, 128), lambda i: (l, 0, i))],
               out_specs=BS((S, 128), lambda i: (0, i)), out_shape=SDS((S, 256), F32))(proj, pwd, scale3)


def _pool_bwd(proj, pwd, scale3, dycat, l, name):
    S = proj.shape[0]

    def body(u_ref, pw_ref, sc_ref, dy_ref, du_ref, dpw_ref, dsc_ref):
        t = lax.broadcasted_iota(jnp.int32, (S, 128), 0)
        first = pl.program_id(0) == 0
        u = u_ref[...]
        win, cnt, lo = _pool_parts(u, t, first)
        pooled = (win / cnt - u).astype(BF16)
        pw = pw_ref[...]
        dy = dy_ref[...]
        dsc_ref[...] = jnp.broadcast_to(jnp.sum(dy * _dot(pooled, pw), axis=0, keepdims=True), (8, 128))
        dmb = (dy * sc_ref[...]).astype(BF16)
        dpw_ref[...] = _dot_tn(pooled, dmb)
        dpooled = _dot_nt(dmb, pw)
        a1 = dpooled / cnt
        a2 = a1 + _up(a1, 1, t)
        a4 = a2 + _up(a2, 2, t)
        a8 = a4 + _up(a4, 4, t)
        a16 = a8 + _up(a8, 8, t)
        dwin = jnp.where(lo, jnp.where(first, a2, a8), jnp.where(first, a4, a16))
        du_ref[...] = (dwin - dpooled).astype(BF16)

    return _pc(body, name=name, grid=(2,),
               in_specs=[BS((S, 128), lambda i: (0, C_UP // 128 + i)), BS((None, 128, 128), lambda i: (l, i, 0)),
                         BS((None, 1, 128), lambda i: (l, 0, i)), BS((S, 128), lambda i: (0, 4 + i))],
               out_specs=[BS((S, 128), lambda i: (0, i)), BS((128, 128), lambda i: (i, 0)), BS((8, 128), lambda i: (0, i))],
               out_shape=[SDS((S, 256), BF16), SDS((256, 128), F32), SDS((8, 256), F32)])(proj, pwd, scale3, dycat)


def _adamw(w, g, m, v, name):
    n, a, b = w.shape
    tr = _row_tile(a, b)

    def body(w_ref, g_ref, m_ref, v_ref, d_ref, nm_ref, nv_ref):
        gv = g_ref[...]
        m_new = B1 * m_ref[...] + (1.0 - B1) * gv
        v_new = B2 * v_ref[...] + (1.0 - B2) * (gv * gv)
        m_hat = m_new / (1.0 - B1 ** STEP)
        v_hat = v_new / (1.0 - B2 ** STEP)
        d_ref[...] = -LR * (m_hat / (jnp.sqrt(v_hat) + ADAM_EPS) + WD * w_ref[...])
        nm_ref[...] = m_new
        nv_ref[...] = v_new

    sp = BS((None, tr, b), lambda i, t: (i, t, 0))
    return _pc(body, name=name, grid=(n, a // tr), in_specs=[sp] * 4, out_specs=[sp] * 3,
               out_shape=[SDS((n, a, b), F32)] * 3)(w, g, m, v)


def _prefetch_call(body, name, grid, in_specs, out_specs, out_shape):
    gs = pltpu.PrefetchScalarGridSpec(num_scalar_prefetch=1, grid=grid, in_specs=in_specs, out_specs=out_specs)
    return pl.pallas_call(body, name=name, grid_spec=gs, out_shape=out_shape, compiler_params=_params(len(grid)))


def _place(w, kc, dtype, name):
    _, a, b = w.shape

    def body(kc_ref, w_ref, o_ref):
        o_ref[...] = w_ref[...].astype(dtype)

    return _prefetch_call(body, name, (2,), [BS((None, a, b), lambda l, kc: (l, 0, 0))],
                          BS((None, None, a, b), lambda l, kc: (l, kc[0], 0, 0)), SDS((2, 4, a, b), dtype))(kc, w)


def _pair_sum(g, got, kc, name):
    _, _, a, b = g.shape
    tr = _row_tile(a, b)

    def body(kc_ref, a_ref, b_ref, t32_ref, t16_ref):
        s = a_ref[...] + b_ref[...]
        t32_ref[...] = s
        t16_ref[...] = s.astype(BF16)

    sp = BS((None, tr, b), lambda k, t, kc: (k, t, 0))
    return _prefetch_call(body, name, (4, a // tr),
                          [BS((None, None, tr, b), lambda k, t, kc: (kc[1], k, t, 0)), sp], [sp, sp],
                          [SDS((4, a, b), F32), SDS((4, a, b), BF16)])(kc, g, got)


def _chip_sum(t32, got3, kc, name):
    _, a, b = t32.shape
    tr = _row_tile(a, b)

    def body(kc_ref, a_ref, b_ref, u_ref):
        u_ref[...] = ((a_ref[...] + b_ref[0].astype(F32)) + b_ref[1].astype(F32)) + b_ref[2].astype(F32)

    return _prefetch_call(body, name, (a // tr,),
                          [BS((None, tr, b), lambda t, kc: (kc[0], t, 0)), BS((3, tr, b), lambda t, kc: (0, t, 0))],
                          BS((None, tr, b), lambda t, kc: (kc[1], t, 0)), SDS((2, a, b), F32))(kc, t32, got3)


def _me():
    return lax.axis_index("x"), lax.axis_index("y"), lax.axis_index("c")


def _other_chips(x, y):
    return [(1 - x, y), (x, 1 - y), (1 - x, 1 - y)]


ANY = BS(memory_space=pl.ANY)
COMM_PARAMS = pltpu.CompilerParams(has_side_effects=True)


def _gather(arrs, name):
    n = len(arrs)

    def body(*refs):
        for phase in _gather_phases(refs[n:2 * n], [a.shape for a in arrs], refs[2 * n], refs[2 * n + 1]):
            phase()

    return pl.pallas_call(body, name=name, out_shape=[SDS(a.shape, a.dtype) for a in arrs],
                          in_specs=[ANY] * n, out_specs=[ANY] * n, input_output_aliases={t: t for t in range(n)},
                          scratch_shapes=[pltpu.SemaphoreType.DMA((7 * n,)), pltpu.SemaphoreType.DMA((7 * n,))],
                          compiler_params=COMM_PARAMS)(*arrs)


def _gather_phases(outs, shapes, send_sems, recv_sems):
    n = len(outs)
    split = [s[2] % 32 == 0 for s in shapes]

    def plan():
        x, y, c = _me()
        return c, (x, y), (x, y, c), (x, y, 1 - c), _other_chips(x, y)

    def blk(t, chip, layer, half=None):
        r = outs[t].at[layer, 2 * chip[0] + chip[1]]
        if half is None:
            return r
        rows = shapes[t][2] // 2
        return r.at[pl.ds(half * rows, rows)]

    def copy(t, k, ref, to):
        return pltpu.make_async_remote_copy(src_ref=ref, dst_ref=ref, send_sem=send_sems.at[7 * t + k],
                                            recv_sem=recv_sems.at[7 * t + k], device_id=to, device_id_type=MESH)

    def own_sends(t):
        c, chip, me, sib, (xn, yn, dg) = plan()
        cps = [copy(t, 0, blk(t, chip, c), (*xn, c)), copy(t, 1, blk(t, chip, c), (*yn, c))]
        return cps if split[t] else cps + [copy(t, 2, blk(t, chip, c), (*dg, c))]

    def relays(t):
        c, chip, me, sib, (xn, yn, dg) = plan()
        after_x = [copy(t, 4, blk(t, xn, c), sib)]
        after_y = [copy(t, 5, blk(t, yn, c), sib)]
        if split[t]:
            after_x.insert(0, copy(t, 2, blk(t, xn, c, 0), (*yn, c)))
            after_y.insert(0, copy(t, 3, blk(t, yn, c, 1), (*xn, c)))
        return after_x, after_y, [copy(t, 6, blk(t, dg, c), sib)]

    def phase1():
        for t in range(n):
            for cp in own_sends(t):
                cp.start()

    def phase2():
        c, chip, me, sib, (xn, yn, dg) = plan()
        for t in range(n):
            after_x, after_y, _ = relays(t)
            copy(t, 0, blk(t, xn, c), me).wait_recv()
            for cp in after_x:
                cp.start()
            copy(t, 1, blk(t, yn, c), me).wait_recv()
            for cp in after_y:
                cp.start()

    def phase3():
        c, chip, me, sib, (xn, yn, dg) = plan()
        for t in range(n):
            if split[t]:
                copy(t, 2, blk(t, dg, c, 0), me).wait_recv()
                copy(t, 3, blk(t, dg, c, 1), me).wait_recv()
            else:
                copy(t, 2, blk(t, dg, c), me).wait_recv()
            relays(t)[2][0].start()
        for t in range(n):
            for k, peer in ((4, xn), (5, yn), (6, dg)):
                copy(t, k, blk(t, peer, 1 - c), me).wait_recv()
        for t in range(n):
            after_x, after_y, after_d = relays(t)
            for cp in own_sends(t) + after_x + after_y + after_d:
                cp.wait_send()

    return phase1, phase2, phase3


def _swap_layers(gs, name):
    n = len(gs)

    def body(*refs):
        ins, outs, send_sems, recv_sems = refs[:n], refs[n:2 * n], refs[2 * n], refs[2 * n + 1]
        x, y, c = _me()
        cps = [pltpu.make_async_remote_copy(src_ref=ins[t].at[1 - c], dst_ref=outs[t], send_sem=send_sems.at[t],
                                            recv_sem=recv_sems.at[t], device_id=(x, y, 1 - c), device_id_type=MESH)
               for t in range(n)]
        for cp in cps:
            cp.start()
        for cp in cps:
            cp.wait()

    return pl.pallas_call(body, name=name, out_shape=[SDS(g.shape[1:], g.dtype) for g in gs],
                          in_specs=[ANY] * n, out_specs=[ANY] * n,
                          scratch_shapes=[pltpu.SemaphoreType.DMA((n,)), pltpu.SemaphoreType.DMA((n,))],
                          compiler_params=COMM_PARAMS)(*gs)


def _exchange_chips(ts, name):
    n = len(ts)

    def body(*refs):
        ins, outs, send_sems, recv_sems = refs[:n], refs[n:2 * n], refs[2 * n], refs[2 * n + 1]
        x, y, c = _me()
        cps = [pltpu.make_async_remote_copy(src_ref=ins[t].at[2 * cx + cy], dst_ref=outs[t].at[j],
                                            send_sem=send_sems.at[3 * t + j], recv_sem=recv_sems.at[3 * t + j],
                                            device_id=(cx, cy, c), device_id_type=MESH)
               for j, (cx, cy) in enumerate(_other_chips(x, y)) for t in range(n)]
        for cp in cps:
            cp.start()
        for cp in cps:
            cp.wait()

    return pl.pallas_call(body, name=name, out_shape=[SDS((3,) + t.shape[1:], t.dtype) for t in ts],
                          in_specs=[ANY] * n, out_specs=[ANY] * n,
                          scratch_shapes=[pltpu.SemaphoreType.DMA((3 * n,)), pltpu.SemaphoreType.DMA((3 * n,))],
                          compiler_params=COMM_PARAMS)(*ts)


def _join_layers(us, name):
    n = len(us)

    def body(*refs):
        outs, send_sems, recv_sems = refs[n:2 * n], refs[2 * n], refs[2 * n + 1]
        x, y, c = _me()
        cps = [pltpu.make_async_remote_copy(src_ref=outs[t].at[c], dst_ref=outs[t].at[c], send_sem=send_sems.at[t],
                                            recv_sem=recv_sems.at[t], device_id=(x, y, 1 - c), device_id_type=MESH)
               for t in range(n)]
        for cp in cps:
            cp.start()
        for cp in cps:
            cp.wait()

    return pl.pallas_call(body, name=name, out_shape=[SDS(u.shape, u.dtype) for u in us],
                          in_specs=[ANY] * n, out_specs=[ANY] * n, input_output_aliases={t: t for t in range(n)},
                          scratch_shapes=[pltpu.SemaphoreType.DMA((n,)), pltpu.SemaphoreType.DMA((n,))],
                          compiler_params=COMM_PARAMS)(*us)


def _allsum_small(v, name):
    M = v.shape[0]

    def body(x_ref, o_ref, all_ref, send_sems, recv_sems, local_sem):
        x, y, c = _me()
        me, sib = (x, y, c), (x, y, 1 - c)
        chips = _other_chips(x, y)

        def rows(px, py, pc):
            return all_ref.at[pl.ds((4 * px + 2 * py + pc) * M, M), :]

        def copy(k, block, to, src=None):
            return pltpu.make_async_remote_copy(src_ref=rows(*block) if src is None else src, dst_ref=rows(*block),
                                                send_sem=send_sems.at[k], recv_sem=recv_sems.at[k],
                                                device_id=to, device_id_type=MESH)

        mine = pltpu.make_async_copy(x_ref, rows(*me), local_sem)
        mine.start()
        first = [copy(0, me, sib, src=x_ref)]
        first += [copy(1 + j, me, (*chip, c), src=x_ref) for j, chip in enumerate(chips)]
        for cp in first:
            cp.start()
        passed = [copy(4 + j, (*chip, c), sib) for j, chip in enumerate(chips)]
        for j, chip in enumerate(chips):
            copy(1 + j, (*chip, c), me).wait_recv()
            passed[j].start()
        copy(0, sib, me).wait_recv()
        for j, chip in enumerate(chips):
            copy(4 + j, (*chip, 1 - c), me).wait_recv()
        for cp in first + passed:
            cp.wait_send()
        mine.wait()
        acc = all_ref[0:M, :]
        for d in range(1, 8):
            acc = acc + all_ref[d * M:(d + 1) * M, :]
        o_ref[...] = acc

    vm = BS(memory_space=pltpu.VMEM)
    return pl.pallas_call(body, name=name, out_shape=SDS((M, LANES), F32), in_specs=[vm], out_specs=vm,
                          scratch_shapes=[pltpu.VMEM((8 * M, LANES), F32), pltpu.SemaphoreType.DMA((7,)),
                                          pltpu.SemaphoreType.DMA((7,)), pltpu.SemaphoreType.DMA],
                          compiler_params=pltpu.CompilerParams(has_side_effects=True, vmem_limit_bytes=VMEM_LIMIT))(v)


FFN = ("w_gate_up", "w_down")
REST = ("w_in", "w_o", "w_uq", "w_ukv")
BIG = FFN + REST
TINY = ("conv_w",)
REPL = ("attn_norm", "mla_q_norm", "mla_kv_norm", "pool_w", "pool_scale", "swa_sinks", "mix_norm", "ffn_norm",
        "final_norm")
ORDER = ("attn_norm", "w_in", "mla_q_norm", "w_uq", "mla_kv_norm", "w_ukv", "conv_w", "pool_w", "pool_scale",
         "swa_sinks", "mix_norm", "w_o", "ffn_norm", "w_gate_up", "w_down", "final_norm")


def _rows8(shape):
    return -(-int(np.prod(shape)) // (8 * LANES)) * 8


def _pack(arrs):
    parts = []
    for a in arrs:
        r = _rows8(a.shape)
        parts.append(jnp.pad(a.reshape(-1), (0, r * LANES - a.size)).reshape(r, LANES))
    return jnp.concatenate(parts, axis=0)


def _unpack(buf, shapes):
    out, r0 = [], 0
    for s in shapes:
        n, r = int(np.prod(s)), _rows8(s)
        rows = buf[r0:r0 + r]
        out.append(rows.reshape(s) if n == r * LANES else rows.reshape(-1)[:n].reshape(s))
        r0 += r
    return out


def _cols_joined(g):
    return jnp.transpose(g, (0, 2, 1, 3)).reshape(g.shape[0], g.shape[2], 4 * g.shape[3])


def _cols_split(w):
    n, a, b4 = w.shape
    return jnp.transpose(w.reshape(n, a, 4, b4 // 4), (0, 2, 1, 3))


def _rope_tables(S):
    inv = 1.0 / (10000.0 ** (jnp.arange(0, 32, 2, dtype=F32) / 32))
    ang = jnp.arange(S, dtype=F32)[:, None] * inv[None, :]
    cos, sin = jnp.cos(ang), jnp.sin(ang)
    z = lambda w: jnp.zeros((S, w), F32)
    tc = jnp.concatenate([jnp.ones((S, 64), F32), cos, cos, jnp.ones((S, 32), F32)], axis=1)
    ts1 = jnp.concatenate([z(64), -sin, z(48)], axis=1)
    ts2 = jnp.concatenate([z(80), sin, z(32)], axis=1)
    return tc, ts1, ts2


def _pad_w_in(w):
    z = lambda n: jnp.zeros(w.shape[:-1] + (n,), w.dtype)
    return jnp.concatenate([w[..., 0:384], z(64), w[..., 384:416], z(32), w[..., 416:1952]], axis=-1)


def _unpad_w_in(d):
    return jnp.concatenate([d[..., 0:384], d[..., 448:480], d[..., 512:2048]], axis=-1)


def _pad_heads(w, src, offs):
    cols = []
    for h in range(HEADS):
        src0, n = src[h]
        z = lambda k: jnp.zeros(w.shape[:-1] + (k,), w.dtype)
        cols += [z(offs[h]), w[..., src0:src0 + n], z(128 - offs[h] - n)]
    return jnp.concatenate(cols, axis=-1)


UQ_SRC = [(h * 96, 96) for h in range(HEADS)]
KN_SRC = [(h * 128, 64) for h in range(HEADS)]
V_SRC = [(h * 128 + 64, 64) for h in range(HEADS)]
ZERO_OFF = [0] * HEADS
V_OFF = [(h % 2) * 64 for h in range(HEADS)]


def _unpad_heads(d, src, offs):
    return [d[..., h * 128 + offs[h]: h * 128 + offs[h] + src[h][1]] for h in range(HEADS)]


def kernel(x, attn_norm, w_in, mla_q_norm, w_uq, mla_kv_norm, w_ukv, conv_w, pool_w, pool_scale, swa_sinks, mix_norm, w_o, ffn_norm, w_gate_up, w_down, final_norm, loss_target, m_attn_norm, m_w_in, m_mla_q_norm, m_w_uq, m_mla_kv_norm, m_w_ukv, m_conv_w, m_pool_w, m_pool_scale, m_swa_sinks, m_mix_norm, m_w_o, m_ffn_norm, m_w_gate_up, m_w_down, m_final_norm, v_attn_norm, v_w_in, v_mla_q_norm, v_w_uq, v_mla_kv_norm, v_w_ukv, v_conv_w, v_pool_w, v_pool_scale, v_swa_sinks, v_mix_norm, v_w_o, v_ffn_norm, v_w_gate_up, v_w_down, v_final_norm):
    W = dict(attn_norm=attn_norm, w_in=w_in, mla_q_norm=mla_q_norm, w_uq=w_uq, mla_kv_norm=mla_kv_norm, w_ukv=w_ukv,
             conv_w=conv_w, pool_w=pool_w, pool_scale=pool_scale, swa_sinks=swa_sinks, mix_norm=mix_norm, w_o=w_o,
             ffn_norm=ffn_norm, w_gate_up=w_gate_up, w_down=w_down, final_norm=final_norm)
    M1 = dict(attn_norm=m_attn_norm, w_in=m_w_in, mla_q_norm=m_mla_q_norm, w_uq=m_w_uq, mla_kv_norm=m_mla_kv_norm,
              w_ukv=m_w_ukv, conv_w=m_conv_w, pool_w=m_pool_w, pool_scale=m_pool_scale, swa_sinks=m_swa_sinks,
              mix_norm=m_mix_norm, w_o=m_w_o, ffn_norm=m_ffn_norm, w_gate_up=m_w_gate_up, w_down=m_w_down,
              final_norm=m_final_norm)
    V2 = dict(attn_norm=v_attn_norm, w_in=v_w_in, mla_q_norm=v_mla_q_norm, w_uq=v_w_uq, mla_kv_norm=v_mla_kv_norm,
              w_ukv=v_w_ukv, conv_w=v_conv_w, pool_w=v_pool_w, pool_scale=v_pool_scale, swa_sinks=v_swa_sinks,
              mix_norm=v_mix_norm, w_o=v_w_o, ffn_norm=v_ffn_norm, w_gate_up=v_w_gate_up, w_down=v_w_down,
              final_norm=v_final_norm)
    S = x.shape[1]
    xc, yc, cc = _me()
    chip = 2 * xc + yc
    kc = jnp.stack([chip, cc]).astype(jnp.int32)

    first, later = ("w_in", "w_uq", "w_ukv", "conv_w"), ("w_o", "w_gate_up", "w_down")
    placed = {n: _place(W[n], kc, F32 if n == "conv_w" else BF16, f"place_{n}") for n in first + later}
    gi, gq, gkv, gcv = _gather([placed[n] for n in first], "gather_weights")
    win_p = _pad_w_in(_cols_joined(gi))
    wuq_p = _pad_heads(_cols_joined(gq), UQ_SRC, ZERO_OFF)
    wukv = _cols_joined(gkv)
    wk_p = _pad_heads(wukv, KN_SRC, ZERO_OFF)
    wv_p = _pad_heads(wukv, V_SRC, V_OFF)
    conv8 = jnp.pad(_cols_joined(gcv), ((0, 0), (0, 5), (0, 0)))
    pwd =jnp.concatenate([jnp.concatenate(
        [jnp.pad(pool_w[:, 2 * b], ((0, 0), (0, 0), (0, 64))), jnp.pad(pool_w[:, 2 * b + 1], ((0, 0), (0, 0), (64, 0)))],
        axis=1) for b in range(2)], axis=1).astype(BF16)
    tabs = _rope_tables(S)
    g_attn, g_q, g_kv, g_mix, g_ffn, g_ps = (_g3(W[n]) for n in ("attn_norm", "mla_q_norm", "mla_kv_norm", "mix_norm",
                                                                  "ffn_norm", "pool_scale"))

    xs = [x[0]]
    saved = []
    for l in range(DEPTH):
        x0 = xs[-1]
        proj, h = _norm_mm(x0, g_attn, l, win_p, _wspec_in(l), D_INP, D_INP, F32, f"in_proj{l}")
        q, k, v, kt, vt = _mla_prep(proj, g_q, g_kv, wuq_p, wk_p, wv_p, tabs, l, f"mla_prep{l}")
        ya, lse, gathered = _mla_attn(q, k, vt, [placed[n] for n in later] if l == 0 else [], f"mla_attn{l}")
        if l == 0:
            go, gu4, gd = gathered
            wo, wdown = go.reshape(2, D, D), gd.reshape(2, D_FF, D)
        yb = _conv(proj, conv8, l, f"conv{l}")
        ycp = _pool(proj, pwd, g_ps, l, f"pool{l}")
        yd = _swa(proj, swa_sinks, l, f"swa{l}")
        x1, ycat, mixed = _mix_out(x0, ya, yb, ycp, yd, g_mix, wo, l, f"mix_out{l}")
        gu, h2 = _norm_mm(x1, g_ffn, l, gu4, _wspec_gu(l), 2 * D_FF, 2 * D_FF // 4, BF16, f"gate_up{l}")
        x2, act = _swiglu_mm_res(x1, gu, wdown, l, f"down{l}")
        saved.append(dict(x0=x0, proj=proj, h=h, q=q, k=k, kt=kt, v=v, lse=lse, x1=x1, ycat=ycat, mixed=mixed,
                          gu=gu, h2=h2, act=act))
        xs.append(x2)

    dx, dx16, dg_final, loss_tile = _loss_head(xs[-1], final_norm.reshape(1, D), loss_target[0], "loss_head")
    loss = lax.psum(loss_tile[0, 0] * (0.5 / D), ("x", "y", "c"))

    G = {n: [None] * DEPTH for n in ("w_uq", "w_ukv") + TINY + REPL if n != "final_norm"}
    gw_in = gw_o = gw_gu = gw_down = None
    for l in reversed(range(DEPTH)):
        sv = saved[l]
        dgu = _bwd_down(dx16, wdown, sv["gu"], l, f"down_bwd{l}")
        gw_down = _mm_tn(sv["act"], dx16, l, gw_down, f"dw_down{l}")
        gw_gu = _mm_tn(sv["h2"], dgu, l, gw_gu, f"dw_gate_up{l}", split4=True)
        ffn_t16 = []
        if l == 0:
            g_f = [gw_gu, gw_down.reshape(2, 4, D_FF // 4, D)]
            got_f = _swap_layers(g_f, "rs_swap_cores_ffn")
            pairs_f = [_pair_sum(g, o, kc, f"rs_pair_sum_{n}") for g, o, n in zip(g_f, got_f, FFN)]
            ffn_t16 = [p[1] for p in pairs_f]
        dx1, dx1_16, dg = _mm_nt_normbwd(dgu, gu4, l, sv["x1"], g_ffn, dx, 1, f"gate_up_bwd{l}")
        G["ffn_norm"][l] = dg[0]
        gw_o = _mm_tn(sv["mixed"], dx1_16, l, gw_o, f"dw_o{l}")
        dycat, dg = _mm_nt_normbwd(dx1_16, wo.reshape(2, 1, D, D), l, sv["ycat"], g_mix, None, 4, f"mix_bwd{l}")
        G["mix_norm"][l] = dg[0]

        proj = sv["proj"]
        delta = _mla_delta(dycat, sv["ycat"], f"mla_delta{l}")
        dq, dk, dv, got3_l = _mla_attn_bwd(sv["q"], sv["k"], sv["kt"], sv["v"], dycat, sv["lse"], delta, ffn_t16,
                                           f"mla_attn_bwd{l}")
        if l == 0:
            got3_f = got3_l
        dcq, dckv, dkr, dwuq, dwk, dwv, dgq, dgkv = _mla_prep_bwd(
            dq, dk, dv, proj, g_q, g_kv, wuq_p, wk_p, wv_p, tabs, l, f"mla_prep_bwd{l}")
        dgb, dgc, duc, dcw = _conv_bwd(proj, conv8, dycat, l, f"conv_bwd{l}")
        dup, dpw, dps = _pool_bwd(proj, pwd, g_ps, dycat, l, f"pool_bwd{l}")
        dqs, dks, dvs, dsink = _swa_bwd(proj, swa_sinks, dycat, l, f"swa_bwd{l}")
        dproj = jnp.concatenate([dcq, dckv, dkr, dgb, dgc, duc, dup, dqs, dks, dvs], axis=1)
        gw_in = _mm_tn(sv["h"], dproj, l, gw_in, f"dw_in{l}")
        dx, dx16, dg = _mm_nt_normbwd(dproj, win_p.reshape(2, 1, D, D_INP), l, sv["x0"], g_attn, dx1, 1, f"in_proj_bwd{l}")
        G["attn_norm"][l] = dg[0]
        G["mla_q_norm"][l] = dgq[0]
        G["mla_kv_norm"][l] = dgkv[0]
        G["w_uq"][l] = jnp.concatenate(_unpad_heads(dwuq, UQ_SRC, ZERO_OFF), axis=1)
        kn, vv = _unpad_heads(dwk, KN_SRC, ZERO_OFF), _unpad_heads(dwv, V_SRC, V_OFF)
        G["w_ukv"][l] = jnp.concatenate([t for h in range(HEADS) for t in (kn[h], vv[h])], axis=1)
        G["conv_w"][l] = dcw[0:3]
        G["pool_w"][l] = jnp.stack([dpw[0:64, 0:64], dpw[64:128, 64:128], dpw[128:192, 0:64], dpw[192:256, 64:128]])
        G["pool_scale"][l] = dps[0]
        G["swa_sinks"][l] = dsink[0, 0:4]
    grad_x = dx[None]
    Gl = {n: jnp.stack(G[n]) for n in G}
    Gl["final_norm"] = dg_final[0]

    us_f = [_chip_sum(p[0], o3, kc, f"rs_chip_sum_{n}") for p, o3, n in zip(pairs_f, got3_f, FFN)]
    g_r = [_cols_split(_unpad_w_in(gw_in)), gw_o.reshape(2, 4, D // 4, D), _cols_split(Gl["w_uq"]),
           _cols_split(Gl["w_ukv"])]
    got_r = _swap_layers(g_r, "rs_swap_cores")
    pairs_r = [_pair_sum(g, o, kc, f"rs_pair_sum_{n}") for g, o, n in zip(g_r, got_r, REST)]
    got3_r = _exchange_chips([p[1] for p in pairs_r], "rs_exchange_chips")
    us_r = [_chip_sum(p[0], o3, kc, f"rs_chip_sum_{n}") for p, o3, n in zip(pairs_r, got3_r, REST)]
    gsum = _join_layers(us_f + us_r, "rs_join_cores")
    res = {}
    for n, g in zip(BIG, gsum):
        d_, m_, v_ = _adamw(W[n], g, M1[n], V2[n], f"adamw_{n}")
        res["g", n], res["d", n], res["m", n], res["v", n] = g, d_, m_, v_

    small = TINY + REPL
    full_shapes = [Gl[n].shape for n in small]
    summed = _unpack(_allsum_small(_pack([Gl[n] for n in small]), "allsum_small"), full_shapes)
    gs = {}
    for n, g in zip(small, summed):
        if n in TINY:
            wdt = W[n].shape[2]
            g = lax.dynamic_slice_in_dim(g, chip * wdt, wdt, axis=2)
        gs[n] = g
    own_shapes = [W[n].shape for n in small]
    pk = lambda src: _pack([src[n] for n in small])[None]
    d_s, m_s, v_s = _adamw(pk(W), pk(gs), pk(M1), pk(V2), "adamw_small")
    for key, buf in (("d", d_s), ("m", m_s), ("v", v_s)):
        for n, a in zip(small, _unpack(buf[0], own_shapes)):
            res[key, n] = a
    for n in small:
        res["g", n] = gs[n]

    return (loss, grad_x, *[res["g", n] for n in ORDER], *[res["d", n] for n in ORDER],
            *[res["m", n] for n in ORDER], *[res["v", n] for n in ORDER])
```

```python
import math

import numpy as np
import jax
import jax.numpy as jnp
from jax import lax
from jax.experimental import pallas as pl
from jax.experimental.pallas import tpu as pltpu

F32, BF16 = jnp.float32, jnp.bfloat16
SDS = jax.ShapeDtypeStruct
BS = pl.BlockSpec
MESH = pl.DeviceIdType.MESH

D = 1024
DEPTH = 2
HEADS = 4
D_FF = 2816
D_INP = 2048
EPS = 1e-6
SWA_WINDOW = 128
BLK = 128
SLOPES = tuple(2.0 ** (-8.0 * (i + 1) / 4) for i in range(4))
QK_SCALE = 1.0 / math.sqrt(96)
SWA_SCALE = 1.0 / math.sqrt(64)
LR, B1, B2, ADAM_EPS, WD, STEP = 0.001, 0.9, 0.999, 1e-08, 0.01, 10

LANES = 1024
VMEM_LIMIT = 56 * 1024 * 1024
NEG_INF = float("-inf")

C_CQ, C_CKV, C_KR, C_GB, C_GC, C_UC, C_UP, C_QS, C_KS, C_VS = 0, 256, 384, 512, 768, 1024, 1280, 1536, 1792, 1920


def _params(ngrid):
    return pltpu.CompilerParams(dimension_semantics=("arbitrary",) * ngrid, vmem_limit_bytes=VMEM_LIMIT)


def _pc(body, *, name, grid, in_specs, out_specs, out_shape, scratch=(), aliases=None):
    return pl.pallas_call(
        body, name=name, grid=grid, in_specs=in_specs, out_specs=out_specs, out_shape=out_shape,
        scratch_shapes=scratch, input_output_aliases=aliases or {}, compiler_params=_params(len(grid)))


def _dot(a, b):
    return jnp.dot(a, b, preferred_element_type=F32)


def _dot_nt(a, b):
    return lax.dot_general(a, b, (((1,), (1,)), ((), ())), preferred_element_type=F32)


def _dot_tn(a, b):
    return lax.dot_general(a, b, (((0,), (0,)), ((), ())), preferred_element_type=F32)


def _tile(n, cap):
    if n <= cap:
        return n
    t = cap - cap % 128
    while n % t:
        t -= 128
    return t


def _row_tile(a, b, cap=262144):
    bp = -(-b // 128) * 128
    best = None
    for t in range(8, a + 1, 8):
        if a % t == 0 and t * bp <= cap:
            best = t
    return best if best is not None else a


def _g3(a):
    return a.reshape(a.shape[0], 1, a.shape[1])


def _norm_mm(x, g3, l, w, wspec, N, tn, out_dtype, name):
    S, K = x.shape
    tm = min(512, S)

    def body(x_ref, g_ref, w_ref, y_ref, h_ref):
        @pl.when(pl.program_id(1) == 0)
        def _():
            xv = x_ref[...]
            r = lax.rsqrt(jnp.mean(xv * xv, axis=-1, keepdims=True) + EPS)
            h_ref[...] = (xv * r * g_ref[...]).astype(BF16)

        y_ref[...] = _dot(h_ref[...], w_ref[...]).astype(out_dtype)

    return _pc(body, name=name, grid=(S // tm, N // tn),
               in_specs=[BS((tm, K), lambda i, j: (i, 0)), BS((None, 1, K), lambda i, j: (l, 0, 0)), wspec],
               out_specs=[BS((tm, tn), lambda i, j: (i, j)), BS((tm, K), lambda i, j: (i, 0))],
               out_shape=[SDS((S, N), out_dtype), SDS((S, K), BF16)])(x, g3, w)


def _wspec_in(l):
    return BS((None, D, D_INP), lambda i, j: (l, 0, j))


def _wspec_gu(l):
    return BS((None, None, D, 2 * D_FF // 4), lambda i, j: (l, j, 0, 0))


def _mix_out(x0, ya, yb, yc, yd, gmix3, wo, l, name):
    S = x0.shape[0]
    tm = min(512, S)

    def body(x_ref, ya_ref, yb_ref, yc_ref, yd_ref, g_ref, w_ref, x1_ref, ycat_ref, mixed_ref):
        groups = [ya_ref[...], yb_ref[...], yc_ref[...], yd_ref[...]]
        for gi, yg in enumerate(groups):
            sl = slice(gi * 256, (gi + 1) * 256)
            r = lax.rsqrt(jnp.mean(yg * yg, axis=-1, keepdims=True) + EPS)
            ycat_ref[:, sl] = yg
            mixed_ref[:, sl] = (yg * r * g_ref[:, sl]).astype(BF16)
        x1_ref[...] = x_ref[...] + _dot(mixed_ref[...], w_ref[...])

    row = lambda w: BS((tm, w), lambda i: (i, 0))
    return _pc(body, name=name, grid=(S // tm,),
               in_specs=[row(D), row(256), row(256), row(256), row(256), BS((None, 1, D), lambda i: (l, 0, 0)),
                         BS((None, D, D), lambda i: (l, 0, 0))],
               out_specs=[row(D), row(D), row(D)],
               out_shape=[SDS((S, D), F32), SDS((S, D), F32), SDS((S, D), BF16)])(x0, ya, yb, yc, yd, gmix3, wo)


def _swiglu_mm_res(x1, gu, wdown, l, name):
    S = x1.shape[0]
    tm = min(256, S)

    def body(x_ref, gate_ref, up_ref, w_ref, x2_ref, act_ref):
        gt = gate_ref[...].astype(F32)
        act = (gt * pl.reciprocal(1.0 + jnp.exp(-gt), approx=True) * up_ref[...].astype(F32)).astype(BF16)
        act_ref[...] = act
        x2_ref[...] = x_ref[...] + _dot(act, w_ref[...])

    return _pc(body, name=name, grid=(S // tm,),
               in_specs=[BS((tm, D), lambda i: (i, 0)), BS((tm, D_FF), lambda i: (i, 0)),
                         BS((tm, D_FF), lambda i: (i, 1)), BS((None, D_FF, D), lambda i: (l, 0, 0))],
               out_specs=[BS((tm, D), lambda i: (i, 0)), BS((tm, D_FF), lambda i: (i, 0))],
               out_shape=[SDS((S, D), F32), SDS((S, D_FF), BF16)])(x1, gu, gu, wdown)


def _loss_head(x, g, tgt, name):
    S = x.shape[0]
    tm = min(512, S)

    def body(x_ref, g_ref, t_ref, dx_ref, dx16_ref, dg_ref, loss_ref):
        @pl.when(pl.program_id(0) == 0)
        def _():
            dg_ref[...] = jnp.zeros_like(dg_ref)
            loss_ref[...] = jnp.zeros_like(loss_ref)

        xv = x_ref[...]
        r = lax.rsqrt(jnp.mean(xv * xv, axis=-1, keepdims=True) + EPS)
        xh = xv * r
        gv = g_ref[...]
        diff = xh * gv - t_ref[...]
        loss_ref[...] += jnp.sum(diff * diff)
        dy = diff * (1.0 / D)
        dg_ref[...] += jnp.sum(dy * xh, axis=0, keepdims=True)
        dxh = dy * gv
        dx = r * (dxh - xh * jnp.mean(dxh * xh, axis=-1, keepdims=True))
        dx_ref[...] = dx
        dx16_ref[...] = dx.astype(BF16)

    row = BS((tm, D), lambda i: (i, 0))
    return _pc(body, name=name, grid=(S // tm,),
               in_specs=[row, BS((1, D), lambda i: (0, 0)), row],
               out_specs=[row, row, BS((8, D), lambda i: (0, 0)), BS((8, 128), lambda i: (0, 0))],
               out_shape=[SDS((S, D), F32), SDS((S, D), BF16), SDS((8, D), F32), SDS((8, 128), F32)])(x, g, tgt)


def _mm_tn(a, b, l, prev, name, split4=False):
    S, Ka = a.shape
    N = b.shape[1]
    if split4:
        ta, tn = _tile(Ka, 256), N // 4
        out_shape = SDS((2, 4, Ka, tn), F32)
        out_spec = BS((None, None, ta, tn), lambda j, i: (l, j, i, 0))
    else:
        ta, tn = _tile(Ka, 512), _tile(N, 1024)
        out_shape = SDS((2, Ka, N), F32)
        out_spec = BS((None, ta, tn), lambda j, i: (l, i, j))

    def body(a_ref, b_ref, *rest):
        rest[-1][...] = _dot_tn(a_ref[...], b_ref[...])

    in_specs = [BS((S, ta), lambda j, i: (0, i)), BS((S, tn), lambda j, i: (0, j))]
    args = [a, b]
    if prev is not None:
        in_specs.append(BS(memory_space=pl.ANY))
        args.append(prev)
    return _pc(body, name=name, grid=(N // tn, Ka // ta), in_specs=in_specs, out_specs=out_spec, out_shape=out_shape,
               aliases={2: 0} if prev is not None else None)(*args)


def _bwd_down(dx16, wdown, gu, l, name):
    S = dx16.shape[0]
    tm = min(256, S)

    def body(dx_ref, w_ref, gate_ref, up_ref, dgu_ref):
        dact = _dot_nt(dx_ref[...], w_ref[...])
        gt = gate_ref[...].astype(F32)
        sg = pl.reciprocal(1.0 + jnp.exp(-gt), approx=True)
        dgu_ref[:, 0:D_FF] = (dact * up_ref[...].astype(F32) * (sg * (1.0 + gt * (1.0 - sg)))).astype(BF16)
        dgu_ref[:, D_FF:2 * D_FF] = (dact * (gt * sg)).astype(BF16)

    return _pc(body, name=name, grid=(S // tm,),
               in_specs=[BS((tm, D), lambda i: (i, 0)), BS((None, D_FF, D), lambda i: (l, 0, 0)),
                         BS((tm, D_FF), lambda i: (i, 0)), BS((tm, D_FF), lambda i: (i, 1))],
               out_specs=BS((tm, 2 * D_FF), lambda i: (i, 0)),
               out_shape=SDS((S, 2 * D_FF), BF16))(dx16, wdown, gu, gu)


def _mm_nt_normbwd(dy, w4, l, x, g3, dres, ngroups, name, rider=None):
    S, K = dy.shape
    _, nk, _, kc = w4.shape
    tm = min(512, S)
    gw = D // ngroups
    has_res = dres is not None
    nr = rider.n if rider else 0
    n_in, n_out = 4 + has_res, 2 + has_res

    def body(*refs):
        dy_ref, w_ref, x_ref, g_ref = refs[:4]
        res_ref = refs[4] if has_res else None
        outs = refs[n_in + nr:n_in + nr + n_out]
        dx_ref, dg_ref = outs[0], outs[-1]
        dx16_ref = outs[1] if has_res else None
        r_io = (refs[n_in:n_in + nr], refs[n_in + nr + n_out:n_in + 2 * nr + n_out], refs[n_in + 2 * nr + n_out:])
        if rider:
            pl.when(pl.program_id(0) == 0)(lambda: rider.start(*r_io))

        @pl.when(pl.program_id(0) == 0)
        def _():
            dg_ref[...] = jnp.zeros_like(dg_ref)

        dh = _dot_nt(dy_ref[:, 0:kc], w_ref[0])
        for k in range(1, nk):
            dh = dh + _dot_nt(dy_ref[:, k * kc:(k + 1) * kc], w_ref[k])
        for gi in range(ngroups):
            sl = slice(gi * gw, (gi + 1) * gw)
            xg = x_ref[:, sl]
            r = lax.rsqrt(jnp.mean(xg * xg, axis=-1, keepdims=True) + EPS)
            xh = xg * r
            dhg = dh[:, sl]
            dg_ref[:, sl] += jnp.sum(dhg * xh, axis=0, keepdims=True)
            dxh = dhg * g_ref[:, sl]
            dxg = r * (dxh - xh * jnp.mean(dxh * xh, axis=-1, keepdims=True))
            if has_res:
                dxg = dxg + res_ref[:, sl]
                dx16_ref[:, sl] = dxg.astype(BF16)
            dx_ref[:, sl] = dxg
        if rider:
            pl.when(pl.program_id(0) == S // tm - 1)(lambda: rider.wait(*r_io))

    row = BS((tm, D), lambda i: (i, 0))
    in_specs = [BS((tm, K), lambda i: (i, 0)),
                BS((None, nk, D, kc), lambda i: (l, 0, 0, 0), pipeline_mode=pl.Buffered(1)), row,
                BS((None, 1, D), lambda i: (l, 0, 0))]
    args = [dy, w4, x, g3]
    out_specs, out_shape = [row], [SDS((S, D), F32)]
    if has_res:
        in_specs.append(row)
        args.append(dres)
        out_specs.append(row)
        out_shape.append(SDS((S, D), BF16))
    out_specs.append(BS((8, D), lambda i: (0, 0)))
    out_shape.append(SDS((8, D), F32))
    if not rider:
        return _pc(body, name=name, grid=(S // tm,), in_specs=in_specs, out_specs=out_specs, out_shape=out_shape)(*args)
    out = pl.pallas_call(body, name=name, grid=(S // tm,), in_specs=in_specs + [ANY] * nr,
                         out_specs=out_specs + [ANY] * nr, out_shape=out_shape + rider.out_shape,
                         scratch_shapes=rider.scratch(),
                         compiler_params=pltpu.CompilerParams(dimension_semantics=("arbitrary",),
                                                              vmem_limit_bytes=VMEM_LIMIT, has_side_effects=True))(
        *args, *rider.arrs)
    return (*out[:n_out], list(out[n_out:]))


def _rope(x, c, s1, s2):
    return x * c + pltpu.roll(x, 112, axis=1) * s1 + pltpu.roll(x, 16, axis=1) * s2


def _rope_t(dy, c, s1, s2):
    return dy * c + pltpu.roll(dy * s1, 16, axis=1) + pltpu.roll(dy * s2, 112, axis=1)


def _mla_prep(proj, gq3, gkv3, wuq, wk, wv, tabs, l, name):
    S = proj.shape[0]
    tm = min(512, S)
    tc, ts1, ts2 = tabs

    def body(cq_ref, ckv_ref, kr_ref, gq_ref, gkv_ref, wuq_ref, wk_ref, wv_ref, c_ref, s1_ref, s2_ref,
             q_ref, k_ref, v_ref, kt_ref, vt_ref):
        c, s1, s2 = c_ref[...], s1_ref[...], s2_ref[...]
        cq = cq_ref[...]
        rq = lax.rsqrt(jnp.mean(cq * cq, axis=-1, keepdims=True) + EPS)
        qa = _dot((cq * rq * gq_ref[...]).astype(BF16), wuq_ref[...])
        ckv = ckv_ref[...]
        rkv = lax.rsqrt(jnp.mean(ckv * ckv, axis=-1, keepdims=True) + EPS)
        ckvn = (ckv * rkv * gkv_ref[...]).astype(BF16)
        ka = _dot(ckvn, wk_ref[...])
        va = _dot(ckvn, wv_ref[...])
        v_ref[...] = va.astype(BF16)
        vt_ref[...] = va.T.astype(BF16)
        krr = _rope(kr_ref[...], c, s1, s2)
        for h in range(HEADS):
            sl = slice(h * 128, (h + 1) * 128)
            q_ref[:, sl] = (_rope(qa[:, sl], c, s1, s2) * QK_SCALE).astype(BF16)
            kh = ka[:, sl] + krr
            k_ref[:, sl] = kh.astype(BF16)
            kt_ref[sl, :] = kh.T.astype(BF16)

    lay = lambda a, b: BS((None, a, b), lambda i: (l, 0, 0))
    tab = BS((tm, 128), lambda i: (i, 0))
    return _pc(body, name=name, grid=(S // tm,),
               in_specs=[BS((tm, 256), lambda i: (i, 0)), BS((tm, 128), lambda i: (i, 2)), BS((tm, 128), lambda i: (i, 3)),
                         lay(1, 256), lay(1, 128), lay(256, 512), lay(128, 512), lay(128, 512), tab, tab, tab],
               out_specs=[BS((tm, 512), lambda i: (i, 0))] * 3 + [BS((512, tm), lambda i: (0, i))] * 2,
               out_shape=[SDS((S, 512), BF16)] * 3 + [SDS((512, S), BF16)] * 2)(
        proj, proj, proj, gq3, gkv3, wuq, wk, wv, tc, ts1, ts2)


def _mla_prep_bwd(dq, dk, dv, proj, gq3, gkv3, wuq, wk, wv, tabs, l, name):
    S = proj.shape[0]
    tm = min(512, S)
    tc, ts1, ts2 = tabs

    def body(dq_ref, dk_ref, dv_ref, cq_ref, ckv_ref, gq_ref, gkv_ref, wuq_ref, wk_ref, wv_ref, c_ref, s1_ref, s2_ref,
             dcq_ref, dckv_ref, dkr_ref, dwuq_ref, dwk_ref, dwv_ref, dgq_ref, dgkv_ref):
        @pl.when(pl.program_id(0) == 0)
        def _():
            for r in (dwuq_ref, dwk_ref, dwv_ref, dgq_ref, dgkv_ref):
                r[...] = jnp.zeros_like(r)

        c, s1, s2 = c_ref[...], s1_ref[...], s2_ref[...]
        dqp = jnp.concatenate(
            [_rope_t(dq_ref[h * 128:(h + 1) * 128, :].T * QK_SCALE, c, s1, s2) for h in range(HEADS)], axis=1).astype(BF16)
        cq = cq_ref[...]
        rq = lax.rsqrt(jnp.mean(cq * cq, axis=-1, keepdims=True) + EPS)
        cqh = cq * rq
        gq_v = gq_ref[...]
        dwuq_ref[...] += _dot_tn((cqh * gq_v).astype(BF16), dqp)
        dcqn = _dot_nt(dqp, wuq_ref[...])
        dgq_ref[...] += jnp.sum(dcqn * cqh, axis=0, keepdims=True)
        dxh = dcqn * gq_v
        dcq_ref[...] = (rq * (dxh - cqh * jnp.mean(dxh * cqh, axis=-1, keepdims=True))).astype(BF16)

        dkb = dk_ref[...].astype(BF16)
        dvb = dv_ref[...].astype(BF16)
        ckv = ckv_ref[...]
        rkv = lax.rsqrt(jnp.mean(ckv * ckv, axis=-1, keepdims=True) + EPS)
        ckh = ckv * rkv
        gkv_v = gkv_ref[...]
        ckvn = (ckh * gkv_v).astype(BF16)
        dwk_ref[...] += _dot_tn(ckvn, dkb)
        dwv_ref[...] += _dot_tn(ckvn, dvb)
        dckvn = _dot_nt(dkb, wk_ref[...]) + _dot_nt(dvb, wv_ref[...])
        dgkv_ref[...] += jnp.sum(dckvn * ckh, axis=0, keepdims=True)
        dyh = dckvn * gkv_v
        dckv_ref[...] = (rkv * (dyh - ckh * jnp.mean(dyh * ckh, axis=-1, keepdims=True))).astype(BF16)
        dks = dk_ref[:, 0:128] + dk_ref[:, 128:256] + dk_ref[:, 256:384] + dk_ref[:, 384:512]
        dkr_ref[...] = _rope_t(dks, c, s1, s2).astype(BF16)

    full = lambda a, b: BS((a, b), lambda i: (0, 0))
    lay = lambda a, b: BS((None, a, b), lambda i: (l, 0, 0))
    tab = BS((tm, 128), lambda i: (i, 0))
    row = lambda w: BS((tm, w), lambda i: (i, 0))
    return _pc(body, name=name, grid=(S // tm,),
               in_specs=[BS((512, tm), lambda i: (0, i)), row(512), row(512), BS((tm, 256), lambda i: (i, 0)),
                         BS((tm, 128), lambda i: (i, 2)),
                         lay(1, 256), lay(1, 128), lay(256, 512), lay(128, 512), lay(128, 512), tab, tab, tab],
               out_specs=[row(256), row(128), row(128), full(256, 512), full(128, 512), full(128, 512),
                          full(8, 256), full(8, 128)],
               out_shape=[SDS((S, 256), BF16), SDS((S, 128), BF16), SDS((S, 128), BF16), SDS((256, 512), F32),
                          SDS((128, 512), F32), SDS((128, 512), F32), SDS((8, 256), F32), SDS((8, 128), F32)])(
        dq, dk, dv, proj, proj, gq3, gkv3, wuq, wk, wv, tc, ts1, ts2)


def _causal_steps(n, q_outer):
    if q_outer:
        pairs = [(i, j) for i in range(n) for j in range(i + 1)]
    else:
        pairs = [(i, j) for j in range(n) for i in range(j, n)]
    return jnp.asarray([p[0] for p in pairs], jnp.int32), jnp.asarray([p[1] for p in pairs], jnp.int32)


def _mla_attn(q, k, vt, gts, layer, name):
    S = q.shape[0]
    t = min(512, S)
    n = S // t
    ng = len(gts)

    qi, kj = _causal_steps(n, True)
    last = qi.shape[0] - 1

    def body(qi_ref, kj_ref, q_ref, k_ref, vt_ref, *rest):
        (ya_ref, lse_ref), g_refs = rest[ng:ng + 2], rest[ng + 2:2 * ng + 2]
        m_sc, l_sc, acc_sc = rest[2 * ng + 2:2 * ng + 5]
        i, j = qi_ref[pl.program_id(1)], kj_ref[pl.program_id(1)]
        if ng:
            phases = _gather_phases(g_refs, [g.shape for g in gts], rest[2 * ng + 5], rest[2 * ng + 6], layer)
            for ph, (pp, ss) in zip(phases[:2], ((0, 0), (1, 0))):
                pl.when((pl.program_id(0) == pp) & (pl.program_id(1) == ss))(ph)

        @pl.when(j == 0)
        def _():
            m_sc[...] = jnp.full_like(m_sc, NEG_INF)
            l_sc[...] = jnp.zeros_like(l_sc)
            acc_sc[...] = jnp.zeros_like(acc_sc)

        def step(masked):
            for hh in range(2):
                sl = slice(hh * 128, (hh + 1) * 128)
                st = _dot_nt(k_ref[:, sl], q_ref[:, sl])
                if masked:
                    key = lax.broadcasted_iota(jnp.int32, (t, t), 0)
                    qry = lax.broadcasted_iota(jnp.int32, (t, t), 1)
                    st = jnp.where(key <= qry, st, NEG_INF)
                m_prev = m_sc[hh]
                m_new = jnp.maximum(m_prev, jnp.max(st, axis=0, keepdims=True))
                p = jnp.exp(st - m_new)
                alpha = jnp.exp(m_prev - m_new)
                l_sc[hh] = alpha * l_sc[hh] + jnp.sum(p, axis=0, keepdims=True)
                acc_sc[hh] = alpha * acc_sc[hh] + _dot(vt_ref[sl, :], p.astype(BF16))
                m_sc[hh] = m_new

        @pl.when(j < i)
        def _():
            step(False)

        @pl.when(j == i)
        def _():
            step(True)
            ya_ref[...] = (acc_sc[0] / l_sc[0] + acc_sc[1] / l_sc[1]).T
            for hh in range(2):
                lse_ref[hh] = m_sc[hh] + jnp.log(l_sc[hh])

        if ng:
            pl.when((pl.program_id(0) == 1) & (pl.program_id(1) == last))(phases[2])

    gs = pltpu.PrefetchScalarGridSpec(
        num_scalar_prefetch=2, grid=(2, qi.shape[0]),
        in_specs=[BS((t, 256), lambda p, s, qi, kj: (qi[s], p)), BS((t, 256), lambda p, s, qi, kj: (kj[s], p)),
                  BS((256, t), lambda p, s, qi, kj: (p, kj[s]))] + [ANY] * ng,
        out_specs=[BS((t, 128), lambda p, s, qi, kj: (qi[s], p)), BS((2, 1, t), lambda p, s, qi, kj: (p, 0, qi[s]))]
        + [ANY] * ng,
        scratch_shapes=[pltpu.VMEM((2, 1, t), F32), pltpu.VMEM((2, 1, t), F32), pltpu.VMEM((2, 128, t), F32)]
        + ([pltpu.SemaphoreType.DMA((7 * ng,)), pltpu.SemaphoreType.DMA((7 * ng,))] if ng else []))
    out = pl.pallas_call(body, name=name, grid_spec=gs,
                         out_shape=[SDS((S, 256), F32), SDS((HEADS, 1, S), F32)] + [SDS(g.shape, g.dtype) for g in gts],
                         input_output_aliases={5 + m: 2 + m for m in range(ng)},
                         compiler_params=pltpu.CompilerParams(dimension_semantics=("arbitrary",) * 2,
                                                              vmem_limit_bytes=VMEM_LIMIT, has_side_effects=bool(ng)))(
        qi, kj, q, k, vt, *gts)
    return out[0], out[1], list(out[2:])


def _mla_delta(dycat, ya, name):
    S = ya.shape[0]
    t = min(512, S)

    def body(do_ref, ya_ref, d_ref):
        prod = do_ref[...] * ya_ref[...]
        for p in range(2):
            pt = prod[:, p * 128:(p + 1) * 128].T
            d_ref[2 * p] = jnp.sum(pt[0:64, :], axis=0, keepdims=True)
            d_ref[2 * p + 1] = jnp.sum(pt[64:128, :], axis=0, keepdims=True)

    return _pc(body, name=name, grid=(S // t,),
               in_specs=[BS((t, 256), lambda i: (i, 0)), BS((t, 256), lambda i: (i, 0))],
               out_specs=BS((HEADS, 1, t), lambda i: (0, 0, i)), out_shape=SDS((HEADS, 1, S), F32))(dycat, ya)


def _mla_attn_bwd(q, k, kt, v, dya, lse, delta, ts, name):
    S = q.shape[0]
    t = min(512, S)
    n = S // t
    nx = len(ts)

    qi, kj = _causal_steps(n, False)
    last = qi.shape[0] - 1

    def body(qi_ref, kj_ref, q_ref, k_ref, kt_ref, v_ref, do_ref, lse_ref, delta_ref, *rest):
        t_refs, (dqt_ref, dk_ref, dv_ref), got_refs = rest[:nx], rest[nx:nx + 3], rest[nx + 3:2 * nx + 3]
        i, j = qi_ref[pl.program_id(1)], kj_ref[pl.program_id(1)]

        def exchange():
            x, y, c = _me()
            send_sems, recv_sems = rest[2 * nx + 3:]
            return [pltpu.make_async_remote_copy(src_ref=t_refs[m].at[2 * cx + cy], dst_ref=got_refs[m].at[jx],
                                                 send_sem=send_sems.at[3 * m + jx], recv_sem=recv_sems.at[3 * m + jx],
                                                 device_id=(cx, cy, c), device_id_type=MESH)
                    for jx, (cx, cy) in enumerate(_other_chips(x, y)) for m in range(nx)]

        if nx:
            @pl.when((pl.program_id(0) == 0) & (pl.program_id(1) == 0))
            def _():
                for cp in exchange():
                    cp.start()

        @pl.when(pl.program_id(1) == 0)
        def _():
            dqt_ref[...] = jnp.zeros_like(dqt_ref)

        @pl.when(i == j)
        def _():
            dk_ref[...] = jnp.zeros_like(dk_ref)
            dv_ref[...] = jnp.zeros_like(dv_ref)

        def step(masked):
            qv = q_ref[...]
            p = jnp.exp(_dot_nt(k_ref[...], qv) - lse_ref[...])
            if masked:
                key = lax.broadcasted_iota(jnp.int32, (t, t), 0)
                qry = lax.broadcasted_iota(jnp.int32, (t, t), 1)
                p = jnp.where(key <= qry, p, 0.0)
            dob = do_ref[...].astype(BF16)
            dv_ref[...] += _dot(p.astype(BF16), dob)
            ds = (p * (_dot_nt(v_ref[...], dob) - delta_ref[...])).astype(BF16)
            dk_ref[...] += _dot(ds, qv)
            cols = pl.ds(pl.multiple_of(i * t, t), t)
            dqt_ref[:, cols] += _dot(kt_ref[...], ds)

        @pl.when(i > j)
        def _():
            step(False)

        @pl.when(i == j)
        def _():
            step(True)

        if nx:
            @pl.when((pl.program_id(0) == HEADS - 1) & (pl.program_id(1) == last))
            def _():
                for cp in exchange():
                    cp.wait()

    qs = BS((t, 128), lambda h, s, qi, kj: (qi[s], h))
    ks = BS((t, 128), lambda h, s, qi, kj: (kj[s], h))
    rowv = BS((None, 1, t), lambda h, s, qi, kj: (h, 0, qi[s]))
    gs = pltpu.PrefetchScalarGridSpec(
        num_scalar_prefetch=2, grid=(HEADS, qi.shape[0]),
        in_specs=[qs, ks, BS((128, t), lambda h, s, qi, kj: (h, kj[s])), ks,
                  BS((t, 128), lambda h, s, qi, kj: (qi[s], h // 2)), rowv, rowv] + [ANY] * nx,
        out_specs=[BS((128, S), lambda h, s, qi, kj: (h, 0)), ks, ks] + [ANY] * nx,
        scratch_shapes=[pltpu.SemaphoreType.DMA((3 * nx,)), pltpu.SemaphoreType.DMA((3 * nx,))] if nx else [])
    out = pl.pallas_call(body, name=name, grid_spec=gs,
                         out_shape=[SDS((512, S), F32), SDS((S, 512), F32), SDS((S, 512), F32)]
                         + [SDS((3,) + m.shape[1:], m.dtype) for m in ts],
                         compiler_params=pltpu.CompilerParams(dimension_semantics=("arbitrary",) * 2,
                                                              vmem_limit_bytes=VMEM_LIMIT, has_side_effects=bool(nx)))(
        qi, kj, q, k, kt, v, dya, lse, delta, *ts)
    return out[0], out[1], out[2], list(out[3:])


def _swa_scores(qm, kk, valid, distf, slope, sink):
    sc = _dot_nt(qm, kk) * SWA_SCALE
    sc = jnp.where(valid, sc - slope * distf, NEG_INF)
    m = jnp.maximum(jnp.max(sc, axis=-1, keepdims=True), sink)
    e = jnp.exp(sc - m)
    esink = jnp.exp(sink - m)
    den = jnp.sum(e, axis=-1, keepdims=True) + esink
    return e / den, esink / den


def _swa_masks():
    r = lax.broadcasted_iota(jnp.int32, (BLK, 2 * BLK), 0)
    c = lax.broadcasted_iota(jnp.int32, (BLK, 2 * BLK), 1)
    dist = r + BLK - c
    return (dist >= 0) & (dist < SWA_WINDOW), c >= BLK, dist.astype(F32)


def _to_half(xb, pos, b):
    return xb if pos == b else pltpu.roll(xb, 64, axis=1)


def _swa(proj, sinks, l, name):
    S = proj.shape[0]
    nb = S // BLK

    def body(q_ref, k_ref, v_ref, sink_ref, o_ref, kp, vp):
        kp[0:BLK, :] = jnp.zeros((BLK, 128), BF16)
        vp[0:BLK, :] = jnp.zeros((BLK, 128), BF16)
        kp[BLK:, :] = k_ref[...].astype(BF16)
        vp[BLK:, :] = v_ref[...].astype(BF16)
        lo = lax.broadcasted_iota(jnp.int32, (BLK, 128), 1) < 64
        band, cur, distf = _swa_masks()

        def blk(i, carry):
            st = pl.multiple_of(i * BLK, BLK)
            kk = kp[pl.ds(st, 2 * BLK), :]
            vv = vp[pl.ds(st, 2 * BLK), :]
            valid = band & (cur | (i > 0))
            for b in range(2):
                half = lo if b == 0 else ~lo
                qb = q_ref[pl.ds(st, BLK), b * 128:(b + 1) * 128]
                outs = []
                for pos in range(2):
                    h = 2 * b + pos
                    qm = jnp.where(half, _to_half(qb, pos, b), 0.0).astype(BF16)
                    p, _ = _swa_scores(qm, kk, valid, distf, SLOPES[h], sink_ref[l, h])
                    outs.append(_to_half(_dot(p.astype(BF16), vv), pos, b))
                o_ref[pl.ds(st, BLK), b * 128:(b + 1) * 128] = jnp.where(lo, outs[0], outs[1])
            return carry

        lax.fori_loop(0, nb, blk, 0, unroll=2)

    return _pc(body, name=name, grid=(1,),
               in_specs=[BS((S, 256), lambda i: (0, C_QS // 256)), BS((S, 128), lambda i: (0, C_KS // 128)),
                         BS((S, 128), lambda i: (0, C_VS // 128)), BS(memory_space=pltpu.SMEM)],
               out_specs=BS((S, 256), lambda i: (0, 0)),
               out_shape=SDS((S, 256), F32),
               scratch=[pltpu.VMEM((S + BLK, 128), BF16), pltpu.VMEM((S + BLK, 128), BF16)])(proj, proj, proj, sinks)


def _swa_bwd(proj, sinks, dyd, l, name):
    S = proj.shape[0]
    nb = S // BLK

    def body(q_ref, k_ref, v_ref, sink_ref, do_ref, dq_ref, dk_ref, dv_ref, dsink_ref, kp, vp, dkp, dvp):
        kp[0:BLK, :] = jnp.zeros((BLK, 128), BF16)
        vp[0:BLK, :] = jnp.zeros((BLK, 128), BF16)
        kp[BLK:, :] = k_ref[...].astype(BF16)
        vp[BLK:, :] = v_ref[...].astype(BF16)
        dkp[...] = jnp.zeros_like(dkp)
        dvp[...] = jnp.zeros_like(dvp)
        lo = lax.broadcasted_iota(jnp.int32, (BLK, 128), 1) < 64
        lane8 = lax.broadcasted_iota(jnp.int32, (8, 128), 1)
        band, cur, distf = _swa_masks()

        def blk(i, dsink):
            st = pl.multiple_of(i * BLK, BLK)
            kk = kp[pl.ds(st, 2 * BLK), :]
            vv = vp[pl.ds(st, 2 * BLK), :]
            valid = band & (cur | (i > 0))
            dkk = jnp.zeros((2 * BLK, 128), F32)
            dvv = jnp.zeros((2 * BLK, 128), F32)
            for b in range(2):
                half = lo if b == 0 else ~lo
                qb = q_ref[pl.ds(st, BLK), b * 128:(b + 1) * 128]
                dob = do_ref[pl.ds(st, BLK), b * 128:(b + 1) * 128]
                dqs = []
                for pos in range(2):
                    h = 2 * b + pos
                    qm = jnp.where(half, _to_half(qb, pos, b), 0.0).astype(BF16)
                    dom = jnp.where(half, _to_half(dob, pos, b), 0.0).astype(BF16)
                    p, psink = _swa_scores(qm, kk, valid, distf, SLOPES[h], sink_ref[l, h])
                    dp = _dot_nt(dom, vv)
                    dvv = dvv + _dot_tn(p.astype(BF16), dom)
                    delta = jnp.sum(p * dp, axis=-1, keepdims=True)
                    dsink = dsink + jnp.where(lane8 == h, -jnp.sum(psink * delta), 0.0)
                    dsc = (p * (dp - delta) * SWA_SCALE).astype(BF16)
                    dqs.append(_to_half(_dot(dsc, kk), pos, b))
                    dkk = dkk + _dot_tn(dsc, qm)
                dq_ref[pl.ds(st, BLK), b * 128:(b + 1) * 128] = jnp.where(lo, dqs[0], dqs[1]).astype(BF16)
            dkp[pl.ds(st, 2 * BLK), :] += dkk
            dvp[pl.ds(st, 2 * BLK), :] += dvv
            return dsink

        dsink_ref[...] = lax.fori_loop(0, nb, blk, jnp.zeros((8, 128), F32), unroll=2)
        dk_ref[...] = dkp[BLK:, :].astype(BF16)
        dv_ref[...] = dvp[BLK:, :].astype(BF16)

    return _pc(body, name=name, grid=(1,),
               in_specs=[BS((S, 256), lambda i: (0, C_QS // 256)), BS((S, 128), lambda i: (0, C_KS // 128)),
                         BS((S, 128), lambda i: (0, C_VS // 128)), BS(memory_space=pltpu.SMEM),
                         BS((S, 256), lambda i: (0, 3))],
               out_specs=[BS((S, 256), lambda i: (0, 0)), BS((S, 128), lambda i: (0, 0)), BS((S, 128), lambda i: (0, 0)),
                          BS((8, 128), lambda i: (0, 0))],
               out_shape=[SDS((S, 256), BF16), SDS((S, 128), BF16), SDS((S, 128), BF16), SDS((8, 128), F32)],
               scratch=[pltpu.VMEM((S + BLK, 128), BF16), pltpu.VMEM((S + BLK, 128), BF16),
                        pltpu.VMEM((S + BLK, 128), F32), pltpu.VMEM((S + BLK, 128), F32)])(proj, proj, proj, sinks, dyd)


def _down(x, k, t):
    return jnp.where(t >= k, pltpu.roll(x, k, axis=0), 0.0)


def _up(x, k, t):
    n = x.shape[0]
    return jnp.where(t < n - k, pltpu.roll(x, n - k, axis=0), 0.0)


def _conv(proj, w8, l, name):
    S = proj.shape[0]

    def body(gb_ref, gc_ref, u_ref, w_ref, y_ref):
        t = lax.broadcasted_iota(jnp.int32, (S, 128), 0)
        z = gc_ref[...] * u_ref[...]
        c = w_ref[2:3, :] * z + w_ref[1:2, :] * _down(z, 1, t) + w_ref[0:1, :] * _down(z, 2, t)
        y_ref[...] = gb_ref[...] * c

    col = lambda c0: BS((S, 128), lambda i: (0, c0 // 128 + i))
    return _pc(body, name=name, grid=(2,),
               in_specs=[col(C_GB), col(C_GC), col(C_UC), BS((None, 8, 128), lambda i: (l, 0, i))],
               out_specs=BS((S, 128), lambda i: (0, i)), out_shape=SDS((S, 256), F32))(proj, proj, proj, w8)


def _conv_bwd(proj, w8, dycat, l, name):
    S = proj.shape[0]

    def body(gb_ref, gc_ref, u_ref, w_ref, dy_ref, dgb_ref, dgc_ref, du_ref, dw_ref):
        t = lax.broadcasted_iota(jnp.int32, (S, 128), 0)
        gc, u = gc_ref[...], u_ref[...]
        z = gc * u
        z1, z2 = _down(z, 1, t), _down(z, 2, t)
        w0, w1, w2 = w_ref[0:1, :], w_ref[1:2, :], w_ref[2:3, :]
        dy = dy_ref[...]
        dgb_ref[...] = (dy * (w2 * z + w1 * z1 + w0 * z2)).astype(BF16)
        dc = dy * gb_ref[...]
        dz = w2 * dc + w1 * _up(dc, 1, t) + w0 * _up(dc, 2, t)
        dgc_ref[...] = (dz * u).astype(BF16)
        du_ref[...] = (dz * gc).astype(BF16)
        row = lax.broadcasted_iota(jnp.int32, (8, 128), 0)
        sums = [jnp.sum(dc * zz, axis=0, keepdims=True) for zz in (z2, z1, z)]
        dw_ref[...] = jnp.where(row == 0, sums[0], jnp.where(row == 1, sums[1], jnp.where(row == 2, sums[2], 0.0)))

    col = lambda c0: BS((S, 128), lambda i: (0, c0 // 128 + i))
    out = BS((S, 128), lambda i: (0, i))
    return _pc(body, name=name, grid=(2,),
               in_specs=[col(C_GB), col(C_GC), col(C_UC), BS((None, 8, 128), lambda i: (l, 0, i)), col(256)],
               out_specs=[out, out, out, BS((8, 128), lambda i: (0, i))],
               out_shape=[SDS((S, 256), BF16)] * 3 + [SDS((8, 256), F32)])(proj, proj, proj, w8, dycat)


def _pool_parts(u, t, first):
    lo = lax.broadcasted_iota(jnp.int32, u.shape, 1) < 64
    s2 = u + _down(u, 1, t)
    s4 = s2 + _down(s2, 2, t)
    s8 = s4 + _down(s4, 4, t)
    s16 = s8 + _down(s8, 8, t)
    win = jnp.where(lo, jnp.where(first, s2, s8), jnp.where(first, s4, s16))
    wv = jnp.where(lo, jnp.where(first, 2, 8), jnp.where(first, 4, 16))
    cnt = jnp.minimum(t + 1, wv).astype(F32)
    return win, cnt, lo


def _pool(proj, pwd, scale3, l, name):
    S = proj.shape[0]

    def body(u_ref, pw_ref, sc_ref, y_ref):
        t = lax.broadcasted_iota(jnp.int32, (S, 128), 0)
        u = u_ref[...]
        win, cnt, _ = _pool_parts(u, t, pl.program_id(0) == 0)
        pooled = win / cnt - u
        y_ref[...] = _dot(pooled.astype(BF16), pw_ref[...]) * sc_ref[...]

    return _pc(body, name=name, grid=(2,),
               in_specs=[BS((S, 128), lambda i: (0, C_UP // 128 + i)), BS((None, 128, 128), lambda i: (l, i, 0)),
                         BS((None, 1, 128), lambda i: (l, 0, i))],
               out_specs=BS((S, 128), lambda i: (0, i)), out_shape=SDS((S, 256), F32))(proj, pwd, scale3)


def _pool_bwd(proj, pwd, scale3, dycat, l, name):
    S = proj.shape[0]

    def body(u_ref, pw_ref, sc_ref, dy_ref, du_ref, dpw_ref, dsc_ref):
        t = lax.broadcasted_iota(jnp.int32, (S, 128), 0)
        first = pl.program_id(0) == 0
        u = u_ref[...]
        win, cnt, lo = _pool_parts(u, t, first)
        pooled = (win / cnt - u).astype(BF16)
        pw = pw_ref[...]
        dy = dy_ref[...]
        dsc_ref[...] = jnp.broadcast_to(jnp.sum(dy * _dot(pooled, pw), axis=0, keepdims=True), (8, 128))
        dmb = (dy * sc_ref[...]).astype(BF16)
        dpw_ref[...] = _dot_tn(pooled, dmb)
        dpooled = _dot_nt(dmb, pw)
        a1 = dpooled / cnt
        a2 = a1 + _up(a1, 1, t)
        a4 = a2 + _up(a2, 2, t)
        a8 = a4 + _up(a4, 4, t)
        a16 = a8 + _up(a8, 8, t)
        dwin = jnp.where(lo, jnp.where(first, a2, a8), jnp.where(first, a4, a16))
        du_ref[...] = (dwin - dpooled).astype(BF16)

    return _pc(body, name=name, grid=(2,),
               in_specs=[BS((S, 128), lambda i: (0, C_UP // 128 + i)), BS((None, 128, 128), lambda i: (l, i, 0)),
                         BS((None, 1, 128), lambda i: (l, 0, i)), BS((S, 128), lambda i: (0, 4 + i))],
               out_specs=[BS((S, 128), lambda i: (0, i)), BS((128, 128), lambda i: (i, 0)), BS((8, 128), lambda i: (0, i))],
               out_shape=[SDS((S, 256), BF16), SDS((256, 128), F32), SDS((8, 256), F32)])(proj, pwd, scale3, dycat)


def _adamw(w, g, m, v, name, rider=None):
    n, a, b = w.shape
    tr = _row_tile(a, b)
    nr = rider.n if rider else 0

    def body(w_ref, g_ref, m_ref, v_ref, *rest):
        d_ref, nm_ref, nv_ref = rest[nr:nr + 3]
        r_io = (rest[:nr], rest[nr + 3:2 * nr + 3], rest[2 * nr + 3:])
        first = (pl.program_id(0) == 0) & (pl.program_id(1) == 0)
        final = (pl.program_id(0) == n - 1) & (pl.program_id(1) == a // tr - 1)
        if rider:
            pl.when(first)(lambda: rider.start(*r_io))
        gv = g_ref[...]
        m_new = B1 * m_ref[...] + (1.0 - B1) * gv
        v_new = B2 * v_ref[...] + (1.0 - B2) * (gv * gv)
        m_hat = m_new / (1.0 - B1 ** STEP)
        v_hat = v_new / (1.0 - B2 ** STEP)
        d_ref[...] = -LR * (m_hat / (jnp.sqrt(v_hat) + ADAM_EPS) + WD * w_ref[...])
        nm_ref[...] = m_new
        nv_ref[...] = v_new
        if rider:
            pl.when(final)(lambda: rider.wait(*r_io))

    sp = BS((None, tr, b), lambda i, t: (i, t, 0))
    if not rider:
        return _pc(body, name=name, grid=(n, a // tr), in_specs=[sp] * 4, out_specs=[sp] * 3,
                   out_shape=[SDS((n, a, b), F32)] * 3)(w, g, m, v)
    out = pl.pallas_call(body, name=name, grid=(n, a // tr), in_specs=[sp] * 4 + [ANY] * nr,
                         out_specs=[sp] * 3 + [ANY] * nr, out_shape=[SDS((n, a, b), F32)] * 3 + rider.out_shape,
                         scratch_shapes=rider.scratch(),
                         compiler_params=pltpu.CompilerParams(dimension_semantics=("arbitrary",) * 2,
                                                              vmem_limit_bytes=VMEM_LIMIT, has_side_effects=True))(
        w, g, m, v, *rider.arrs)
    return out[0], out[1], out[2], list(out[3:])


def _prefetch_call(body, name, grid, in_specs, out_specs, out_shape):
    gs = pltpu.PrefetchScalarGridSpec(num_scalar_prefetch=1, grid=grid, in_specs=in_specs, out_specs=out_specs)
    return pl.pallas_call(body, name=name, grid_spec=gs, out_shape=out_shape, compiler_params=_params(len(grid)))


def _place(w, kc, dtype, name):
    _, a, b = w.shape

    def body(kc_ref, w_ref, o_ref):
        o_ref[...] = w_ref[...].astype(dtype)

    return _prefetch_call(body, name, (2,), [BS((None, a, b), lambda l, kc: (l, 0, 0))],
                          BS((None, None, a, b), lambda l, kc: (l, kc[0], 0, 0)), SDS((2, 4, a, b), dtype))(kc, w)


def _pair_sum(g, got, kc, name):
    _, _, a, b = g.shape
    tr = _row_tile(a, b)

    def body(kc_ref, a_ref, b_ref, t32_ref, t16_ref):
        s = a_ref[...] + b_ref[...]
        t32_ref[...] = s
        t16_ref[...] = s.astype(BF16)

    sp = BS((None, tr, b), lambda k, t, kc: (k, t, 0))
    return _prefetch_call(body, name, (4, a // tr),
                          [BS((None, None, tr, b), lambda k, t, kc: (kc[1], k, t, 0)), sp], [sp, sp],
                          [SDS((4, a, b), F32), SDS((4, a, b), BF16)])(kc, g, got)


def _chip_sum(t32, got3, kc, name):
    _, a, b = t32.shape
    tr = _row_tile(a, b)

    def body(kc_ref, a_ref, b_ref, u_ref):
        u_ref[...] = ((a_ref[...] + b_ref[0].astype(F32)) + b_ref[1].astype(F32)) + b_ref[2].astype(F32)

    return _prefetch_call(body, name, (a // tr,),
                          [BS((None, tr, b), lambda t, kc: (kc[0], t, 0)), BS((3, tr, b), lambda t, kc: (0, t, 0))],
                          BS((None, tr, b), lambda t, kc: (kc[1], t, 0)), SDS((2, a, b), F32))(kc, t32, got3)


def _me():
    return lax.axis_index("x"), lax.axis_index("y"), lax.axis_index("c")


def _other_chips(x, y):
    return [(1 - x, y), (x, 1 - y), (1 - x, 1 - y)]


ANY = BS(memory_space=pl.ANY)
COMM_PARAMS = pltpu.CompilerParams(has_side_effects=True)


def _gather(arrs, name):
    n = len(arrs)

    def body(*refs):
        for phase in _gather_phases(refs[n:2 * n], [a.shape for a in arrs], refs[2 * n], refs[2 * n + 1]):
            phase()

    return pl.pallas_call(body, name=name, out_shape=[SDS(a.shape, a.dtype) for a in arrs],
                          in_specs=[ANY] * n, out_specs=[ANY] * n, input_output_aliases={t: t for t in range(n)},
                          scratch_shapes=[pltpu.SemaphoreType.DMA((7 * n,)), pltpu.SemaphoreType.DMA((7 * n,))],
                          compiler_params=COMM_PARAMS)(*arrs)


def _gather_phases(outs, shapes, send_sems, recv_sems, layer=None):
    n = len(outs)
    split = [s[2] % 32 == 0 for s in shapes]

    def plan():
        x, y, c = _me()
        return (c if layer is None else layer), (x, y), (x, y, c), (x, y, 1 - c), _other_chips(x, y)

    def role(moving, fn):
        if layer is None:
            fn()
        else:
            c = lax.axis_index("c")
            pl.when((c == layer) if moving else (c != layer))(fn)

    def blk(t, chip, layer, half=None):
        r = outs[t].at[layer, 2 * chip[0] + chip[1]]
        if half is None:
            return r
        rows = shapes[t][2] // 2
        return r.at[pl.ds(half * rows, rows)]

    def copy(t, k, ref, to):
        return pltpu.make_async_remote_copy(src_ref=ref, dst_ref=ref, send_sem=send_sems.at[7 * t + k],
                                            recv_sem=recv_sems.at[7 * t + k], device_id=to, device_id_type=MESH)

    def own_sends(t):
        c, chip, me, sib, (xn, yn, dg) = plan()
        cps = [copy(t, 0, blk(t, chip, c), (*xn, c)), copy(t, 1, blk(t, chip, c), (*yn, c))]
        return cps if split[t] else cps + [copy(t, 2, blk(t, chip, c), (*dg, c))]

    def relays(t):
        c, chip, me, sib, (xn, yn, dg) = plan()
        after_x = [copy(t, 4, blk(t, xn, c), sib)]
        after_y = [copy(t, 5, blk(t, yn, c), sib)]
        if split[t]:
            after_x.insert(0, copy(t, 2, blk(t, xn, c, 0), (*yn, c)))
            after_y.insert(0, copy(t, 3, blk(t, yn, c, 1), (*xn, c)))
        return after_x, after_y, [copy(t, 6, blk(t, dg, c), sib)]

    def send_own():
        for t in range(n):
            for cp in own_sends(t):
                cp.start()

    def relay_neighbours():
        c, chip, me, sib, (xn, yn, dg) = plan()
        for t in range(n):
            after_x, after_y, _ = relays(t)
            copy(t, 0, blk(t, xn, c), me).wait_recv()
            for cp in after_x:
                cp.start()
            copy(t, 1, blk(t, yn, c), me).wait_recv()
            for cp in after_y:
                cp.start()

    def relay_diagonal():
        c, chip, me, sib, (xn, yn, dg) = plan()
        for t in range(n):
            if split[t]:
                copy(t, 2, blk(t, dg, c, 0), me).wait_recv()
                copy(t, 3, blk(t, dg, c, 1), me).wait_recv()
            else:
                copy(t, 2, blk(t, dg, c), me).wait_recv()
            relays(t)[2][0].start()

    def take_sibling():
        _, chip, me, sib, (xn, yn, dg) = plan()
        theirs = 1 - lax.axis_index("c") if layer is None else layer
        for t in range(n):
            for k, peer in ((4, xn), (5, yn), (6, dg)):
                copy(t, k, blk(t, peer, theirs), me).wait_recv()

    def drain_sends():
        for t in range(n):
            after_x, after_y, after_d = relays(t)
            for cp in own_sends(t) + after_x + after_y + after_d:
                cp.wait_send()

    def phase3():
        role(True, relay_diagonal)
        role(False, take_sibling)
        role(True, drain_sends)

    return (lambda: role(True, send_own)), (lambda: role(True, relay_neighbours)), phase3


def _swap_copies(ins, outs, send_sems, recv_sems):
    x, y, c = _me()
    return [pltpu.make_async_remote_copy(src_ref=ins[t].at[1 - c], dst_ref=outs[t], send_sem=send_sems.at[t],
                                         recv_sem=recv_sems.at[t], device_id=(x, y, 1 - c), device_id_type=MESH)
            for t in range(len(ins))]


def _exchange_copies(ins, outs, send_sems, recv_sems):
    x, y, c = _me()
    return [pltpu.make_async_remote_copy(src_ref=ins[t].at[2 * cx + cy], dst_ref=outs[t].at[j],
                                         send_sem=send_sems.at[3 * t + j], recv_sem=recv_sems.at[3 * t + j],
                                         device_id=(cx, cy, c), device_id_type=MESH)
            for j, (cx, cy) in enumerate(_other_chips(x, y)) for t in range(len(ins))]


class _Rider:
    def __init__(self, arrs, out_shape, nsem, copies):
        self.arrs, self.out_shape, self.nsem, self.copies = list(arrs), out_shape, nsem, copies
        self.n = len(self.arrs)

    def scratch(self):
        return [pltpu.SemaphoreType.DMA((self.nsem,)), pltpu.SemaphoreType.DMA((self.nsem,))]

    def start(self, ins, outs, sems):
        for cp in self.copies(ins, outs, *sems):
            cp.start()

    def wait(self, ins, outs, sems):
        for cp in self.copies(ins, outs, *sems):
            cp.wait()


def _swap_rider(gs):
    return _Rider(gs, [SDS(g.shape[1:], g.dtype) for g in gs], len(gs), _swap_copies)


def _exchange_rider(ts):
    return _Rider(ts, [SDS((3,) + t.shape[1:], t.dtype) for t in ts], 3 * len(ts), _exchange_copies)


def _ride_alone(rider, name):
    n = rider.n

    def body(*refs):
        rider.start(refs[:n], refs[n:2 * n], refs[2 * n:])
        rider.wait(refs[:n], refs[n:2 * n], refs[2 * n:])

    return pl.pallas_call(body, name=name, out_shape=rider.out_shape, in_specs=[ANY] * n, out_specs=[ANY] * n,
                          scratch_shapes=rider.scratch(), compiler_params=COMM_PARAMS)(*rider.arrs)


def _join_layers(us, name):
    n = len(us)

    def body(*refs):
        outs, send_sems, recv_sems = refs[n:2 * n], refs[2 * n], refs[2 * n + 1]
        x, y, c = _me()
        cps = [pltpu.make_async_remote_copy(src_ref=outs[t].at[c], dst_ref=outs[t].at[c], send_sem=send_sems.at[t],
                                            recv_sem=recv_sems.at[t], device_id=(x, y, 1 - c), device_id_type=MESH)
               for t in range(n)]
        for cp in cps:
            cp.start()
        for cp in cps:
            cp.wait()

    return pl.pallas_call(body, name=name, out_shape=[SDS(u.shape, u.dtype) for u in us],
                          in_specs=[ANY] * n, out_specs=[ANY] * n, input_output_aliases={t: t for t in range(n)},
                          scratch_shapes=[pltpu.SemaphoreType.DMA((n,)), pltpu.SemaphoreType.DMA((n,))],
                          compiler_params=COMM_PARAMS)(*us)


def _allsum_small(v, name):
    M = v.shape[0]

    def body(x_ref, o_ref, all_ref, send_sems, recv_sems, local_sem):
        x, y, c = _me()
        me, sib = (x, y, c), (x, y, 1 - c)
        chips = _other_chips(x, y)

        def rows(px, py, pc):
            return all_ref.at[pl.ds((4 * px + 2 * py + pc) * M, M), :]

        def copy(k, block, to, src=None):
            return pltpu.make_async_remote_copy(src_ref=rows(*block) if src is None else src, dst_ref=rows(*block),
                                                send_sem=send_sems.at[k], recv_sem=recv_sems.at[k],
                                                device_id=to, device_id_type=MESH)

        mine = pltpu.make_async_copy(x_ref, rows(*me), local_sem)
        mine.start()
        first = [copy(0, me, sib, src=x_ref)]
        first += [copy(1 + j, me, (*chip, c), src=x_ref) for j, chip in enumerate(chips)]
        for cp in first:
            cp.start()
        passed = [copy(4 + j, (*chip, c), sib) for j, chip in enumerate(chips)]
        for j, chip in enumerate(chips):
            copy(1 + j, (*chip, c), me).wait_recv()
            passed[j].start()
        copy(0, sib, me).wait_recv()
        for j, chip in enumerate(chips):
            copy(4 + j, (*chip, 1 - c), me).wait_recv()
        for cp in first + passed:
            cp.wait_send()
        mine.wait()
        acc = all_ref[0:M, :]
        for d in range(1, 8):
            acc = acc + all_ref[d * M:(d + 1) * M, :]
        o_ref[...] = acc

    vm = BS(memory_space=pltpu.VMEM)
    return pl.pallas_call(body, name=name, out_shape=SDS((M, LANES), F32), in_specs=[vm], out_specs=vm,
                          scratch_shapes=[pltpu.VMEM((8 * M, LANES), F32), pltpu.SemaphoreType.DMA((7,)),
                                          pltpu.SemaphoreType.DMA((7,)), pltpu.SemaphoreType.DMA],
                          compiler_params=pltpu.CompilerParams(has_side_effects=True, vmem_limit_bytes=VMEM_LIMIT))(v)


FFN = ("w_gate_up", "w_down")
REST = ("w_in", "w_o", "w_uq", "w_ukv")
BIG = FFN + REST
TINY = ("conv_w",)
REPL = ("attn_norm", "mla_q_norm", "mla_kv_norm", "pool_w", "pool_scale", "swa_sinks", "mix_norm", "ffn_norm",
        "final_norm")
ORDER = ("attn_norm", "w_in", "mla_q_norm", "w_uq", "mla_kv_norm", "w_ukv", "conv_w", "pool_w", "pool_scale",
         "swa_sinks", "mix_norm", "w_o", "ffn_norm", "w_gate_up", "w_down", "final_norm")


def _rows8(shape):
    return -(-int(np.prod(shape)) // (8 * LANES)) * 8


def _pack(arrs):
    parts = []
    for a in arrs:
        r = _rows8(a.shape)
        parts.append(jnp.pad(a.reshape(-1), (0, r * LANES - a.size)).reshape(r, LANES))
    return jnp.concatenate(parts, axis=0)


def _unpack(buf, shapes):
    out, r0 = [], 0
    for s in shapes:
        n, r = int(np.prod(s)), _rows8(s)
        rows = buf[r0:r0 + r]
        out.append(rows.reshape(s) if n == r * LANES else rows.reshape(-1)[:n].reshape(s))
        r0 += r
    return out


def _cols_joined(g):
    return jnp.transpose(g, (0, 2, 1, 3)).reshape(g.shape[0], g.shape[2], 4 * g.shape[3])


def _cols_split(w):
    n, a, b4 = w.shape
    return jnp.transpose(w.reshape(n, a, 4, b4 // 4), (0, 2, 1, 3))


def _rope_tables(S):
    inv = 1.0 / (10000.0 ** (jnp.arange(0, 32, 2, dtype=F32) / 32))
    ang = jnp.arange(S, dtype=F32)[:, None] * inv[None, :]
    cos, sin = jnp.cos(ang), jnp.sin(ang)
    z = lambda w: jnp.zeros((S, w), F32)
    tc = jnp.concatenate([jnp.ones((S, 64), F32), cos, cos, jnp.ones((S, 32), F32)], axis=1)
    ts1 = jnp.concatenate([z(64), -sin, z(48)], axis=1)
    ts2 = jnp.concatenate([z(80), sin, z(32)], axis=1)
    return tc, ts1, ts2


def _pad_w_in(w):
    z = lambda n: jnp.zeros(w.shape[:-1] + (n,), w.dtype)
    return jnp.concatenate([w[..., 0:384], z(64), w[..., 384:416], z(32), w[..., 416:1952]], axis=-1)


def _unpad_w_in(d):
    return jnp.concatenate([d[..., 0:384], d[..., 448:480], d[..., 512:2048]], axis=-1)


def _pad_heads(w, src, offs):
    cols = []
    for h in range(HEADS):
        src0, n = src[h]
        z = lambda k: jnp.zeros(w.shape[:-1] + (k,), w.dtype)
        cols += [z(offs[h]), w[..., src0:src0 + n], z(128 - offs[h] - n)]
    return jnp.concatenate(cols, axis=-1)


UQ_SRC = [(h * 96, 96) for h in range(HEADS)]
KN_SRC = [(h * 128, 64) for h in range(HEADS)]
V_SRC = [(h * 128 + 64, 64) for h in range(HEADS)]
ZERO_OFF = [0] * HEADS
V_OFF = [(h % 2) * 64 for h in range(HEADS)]


def _unpad_heads(d, src, offs):
    return [d[..., h * 128 + offs[h]: h * 128 + offs[h] + src[h][1]] for h in range(HEADS)]


def kernel(x, attn_norm, w_in, mla_q_norm, w_uq, mla_kv_norm, w_ukv, conv_w, pool_w, pool_scale, swa_sinks, mix_norm, w_o, ffn_norm, w_gate_up, w_down, final_norm, loss_target, m_attn_norm, m_w_in, m_mla_q_norm, m_w_uq, m_mla_kv_norm, m_w_ukv, m_conv_w, m_pool_w, m_pool_scale, m_swa_sinks, m_mix_norm, m_w_o, m_ffn_norm, m_w_gate_up, m_w_down, m_final_norm, v_attn_norm, v_w_in, v_mla_q_norm, v_w_uq, v_mla_kv_norm, v_w_ukv, v_conv_w, v_pool_w, v_pool_scale, v_swa_sinks, v_mix_norm, v_w_o, v_ffn_norm, v_w_gate_up, v_w_down, v_final_norm):
    W = dict(attn_norm=attn_norm, w_in=w_in, mla_q_norm=mla_q_norm, w_uq=w_uq, mla_kv_norm=mla_kv_norm, w_ukv=w_ukv,
             conv_w=conv_w, pool_w=pool_w, pool_scale=pool_scale, swa_sinks=swa_sinks, mix_norm=mix_norm, w_o=w_o,
             ffn_norm=ffn_norm, w_gate_up=w_gate_up, w_down=w_down, final_norm=final_norm)
    M1 = dict(attn_norm=m_attn_norm, w_in=m_w_in, mla_q_norm=m_mla_q_norm, w_uq=m_w_uq, mla_kv_norm=m_mla_kv_norm,
              w_ukv=m_w_ukv, conv_w=m_conv_w, pool_w=m_pool_w, pool_scale=m_pool_scale, swa_sinks=m_swa_sinks,
              mix_norm=m_mix_norm, w_o=m_w_o, ffn_norm=m_ffn_norm, w_gate_up=m_w_gate_up, w_down=m_w_down,
              final_norm=m_final_norm)
    V2 = dict(attn_norm=v_attn_norm, w_in=v_w_in, mla_q_norm=v_mla_q_norm, w_uq=v_w_uq, mla_kv_norm=v_mla_kv_norm,
              w_ukv=v_w_ukv, conv_w=v_conv_w, pool_w=v_pool_w, pool_scale=v_pool_scale, swa_sinks=v_swa_sinks,
              mix_norm=v_mix_norm, w_o=v_w_o, ffn_norm=v_ffn_norm, w_gate_up=v_w_gate_up, w_down=v_w_down,
              final_norm=v_final_norm)
    S = x.shape[1]
    xc, yc, cc = _me()
    chip = 2 * xc + yc
    kc = jnp.stack([chip, cc]).astype(jnp.int32)

    first, later = ("w_in", "w_uq", "w_ukv", "conv_w"), ("w_o", "w_gate_up", "w_down")
    placed = {n: _place(W[n], kc, F32 if n == "conv_w" else BF16, f"place_{n}") for n in first + later}
    gi, gq, gkv, gcv = _gather([placed[n] for n in first], "gather_weights")
    later_w = [placed[n] for n in later]
    win_p = _pad_w_in(_cols_joined(gi))
    wuq_p = _pad_heads(_cols_joined(gq), UQ_SRC, ZERO_OFF)
    wukv = _cols_joined(gkv)
    wk_p = _pad_heads(wukv, KN_SRC, ZERO_OFF)
    wv_p = _pad_heads(wukv, V_SRC, V_OFF)
    conv8 = jnp.pad(_cols_joined(gcv), ((0, 0), (0, 5), (0, 0)))
    pwd = jnp.concatenate([jnp.concatenate(
        [jnp.pad(pool_w[:, 2 * b], ((0, 0), (0, 0), (0, 64))), jnp.pad(pool_w[:, 2 * b + 1], ((0, 0), (0, 0), (64, 0)))],
        axis=1) for b in range(2)], axis=1).astype(BF16)
    tabs = _rope_tables(S)
    g_attn, g_q, g_kv, g_mix, g_ffn, g_ps = (_g3(W[n]) for n in ("attn_norm", "mla_q_norm", "mla_kv_norm", "mix_norm",
                                                                  "ffn_norm", "pool_scale"))

    xs = [x[0]]
    saved = []
    for l in range(DEPTH):
        x0 = xs[-1]
        proj, h = _norm_mm(x0, g_attn, l, win_p, _wspec_in(l), D_INP, D_INP, F32, f"in_proj{l}")
        q, k, v, kt, vt = _mla_prep(proj, g_q, g_kv, wuq_p, wk_p, wv_p, tabs, l, f"mla_prep{l}")
        ya, lse, later_w = _mla_attn(q, k, vt, later_w, l, f"mla_attn{l}")
        go, gu4, gd = later_w
        wo, wdown = go.reshape(2, D, D), gd.reshape(2, D_FF, D)
        yb = _conv(proj, conv8, l, f"conv{l}")
        ycp = _pool(proj, pwd, g_ps, l, f"pool{l}")
        yd = _swa(proj, swa_sinks, l, f"swa{l}")
        x1, ycat, mixed = _mix_out(x0, ya, yb, ycp, yd, g_mix, wo, l, f"mix_out{l}")
        gu, h2 = _norm_mm(x1, g_ffn, l, gu4, _wspec_gu(l), 2 * D_FF, 2 * D_FF // 4, BF16, f"gate_up{l}")
        x2, act = _swiglu_mm_res(x1, gu, wdown, l, f"down{l}")
        saved.append(dict(x0=x0, proj=proj, h=h, q=q, k=k, kt=kt, v=v, lse=lse, x1=x1, ycat=ycat, mixed=mixed,
                          gu=gu, h2=h2, act=act))
        xs.append(x2)

    dx, dx16, dg_final, loss_tile = _loss_head(xs[-1], final_norm.reshape(1, D), loss_target[0], "loss_head")
    loss = lax.psum(loss_tile[0, 0] * (0.5 / D), ("x", "y", "c"))

    G = {n: [None] * DEPTH for n in ("w_uq", "w_ukv") + TINY + REPL if n != "final_norm"}
    gw_in = gw_o = gw_gu = gw_down = None
    for l in reversed(range(DEPTH)):
        sv = saved[l]
        dgu = _bwd_down(dx16, wdown, sv["gu"], l, f"down_bwd{l}")
        gw_down = _mm_tn(sv["act"], dx16, l, gw_down, f"dw_down{l}")
        gw_gu = _mm_tn(sv["h2"], dgu, l, gw_gu, f"dw_gate_up{l}", split4=True)
        ffn_t16 = []
        if l == 0:
            g_f = [gw_gu, gw_down.reshape(2, 4, D_FF // 4, D)]
            dx1, dx1_16, dg, got_f = _mm_nt_normbwd(dgu, gu4, l, sv["x1"], g_ffn, dx, 1, f"gate_up_bwd{l}",
                                                    rider=_swap_rider(g_f))
            pairs_f = [_pair_sum(g, o, kc, f"rs_pair_sum_{n}") for g, o, n in zip(g_f, got_f, FFN)]
            ffn_t16 = [p[1] for p in pairs_f]
        else:
            dx1, dx1_16, dg = _mm_nt_normbwd(dgu, gu4, l, sv["x1"], g_ffn, dx, 1, f"gate_up_bwd{l}")
        G["ffn_norm"][l] = dg[0]
        gw_o = _mm_tn(sv["mixed"], dx1_16, l, gw_o, f"dw_o{l}")
        dycat, dg = _mm_nt_normbwd(dx1_16, wo.reshape(2, 1, D, D), l, sv["ycat"], g_mix, None, 4, f"mix_bwd{l}")
        G["mix_norm"][l] = dg[0]

        proj = sv["proj"]
        delta = _mla_delta(dycat, sv["ycat"], f"mla_delta{l}")
        dq, dk, dv, got3_l = _mla_attn_bwd(sv["q"], sv["k"], sv["kt"], sv["v"], dycat, sv["lse"], delta, ffn_t16,
                                           f"mla_attn_bwd{l}")
        if l == 0:
            got3_f = got3_l
        dcq, dckv, dkr, dwuq, dwk, dwv, dgq, dgkv = _mla_prep_bwd(
            dq, dk, dv, proj, g_q, g_kv, wuq_p, wk_p, wv_p, tabs, l, f"mla_prep_bwd{l}")
        dgb, dgc, duc, dcw = _conv_bwd(proj, conv8, dycat, l, f"conv_bwd{l}")
        dup, dpw, dps = _pool_bwd(proj, pwd, g_ps, dycat, l, f"pool_bwd{l}")
        dqs, dks, dvs, dsink = _swa_bwd(proj, swa_sinks, dycat, l, f"swa_bwd{l}")
        dproj = jnp.concatenate([dcq, dckv, dkr, dgb, dgc, duc, dup, dqs, dks, dvs], axis=1)
        gw_in = _mm_tn(sv["h"], dproj, l, gw_in, f"dw_in{l}")
        dx, dx16, dg = _mm_nt_normbwd(dproj, win_p.reshape(2, 1, D, D_INP), l, sv["x0"], g_attn, dx1, 1, f"in_proj_bwd{l}")
        G["attn_norm"][l] = dg[0]
        G["mla_q_norm"][l] = dgq[0]
        G["mla_kv_norm"][l] = dgkv[0]
        G["w_uq"][l] = jnp.concatenate(_unpad_heads(dwuq, UQ_SRC, ZERO_OFF), axis=1)
        kn, vv = _unpad_heads(dwk, KN_SRC, ZERO_OFF), _unpad_heads(dwv, V_SRC, V_OFF)
        G["w_ukv"][l] = jnp.concatenate([t for h in range(HEADS) for t in (kn[h], vv[h])], axis=1)
        G["conv_w"][l] = dcw[0:3]
        G["pool_w"][l] = jnp.stack([dpw[0:64, 0:64], dpw[64:128, 64:128], dpw[128:192, 0:64], dpw[192:256, 64:128]])
        G["pool_scale"][l] = dps[0]
        G["swa_sinks"][l] = dsink[0, 0:4]
    grad_x = dx[None]
    Gl = {n: jnp.stack(G[n]) for n in G}
    Gl["final_norm"] = dg_final[0]

    us_f = [_chip_sum(p[0], o3, kc, f"rs_chip_sum_{n}") for p, o3, n in zip(pairs_f, got3_f, FFN)]
    g_r = [_cols_split(_unpad_w_in(gw_in)), gw_o.reshape(2, 4, D // 4, D), _cols_split(Gl["w_uq"]),
           _cols_split(Gl["w_ukv"])]
    gsum_f = _join_layers(us_f, "rs_join_cores_ffn")
    got_r = _ride_alone(_swap_rider(g_r), "rs_swap_cores")
    pairs_r = [_pair_sum(g, o, kc, f"rs_pair_sum_{n}") for g, o, n in zip(g_r, got_r, REST)]
    res = {}
    exchange_r = _exchange_rider([p[1] for p in pairs_r])
    for n, g in zip(FFN, gsum_f):
        out = _adamw(W[n], g, M1[n], V2[n], f"adamw_{n}", rider=exchange_r if n == FFN[0] else None)
        res["g", n], res["d", n], res["m", n], res["v", n] = g, out[0], out[1], out[2]
        if n == FFN[0]:
            got3_r = out[3]
    us_r = [_chip_sum(p[0], o3, kc, f"rs_chip_sum_{n}") for p, o3, n in zip(pairs_r, got3_r, REST)]
    gsum_r = _join_layers(us_r, "rs_join_cores")
    for n, g in zip(REST, gsum_r):
        d_, m_, v_ = _adamw(W[n], g, M1[n], V2[n], f"adamw_{n}")
        res["g", n], res["d", n], res["m", n], res["v", n] = g, d_, m_, v_

    small = TINY + REPL
    full_shapes = [Gl[n].shape for n in small]
    summed = _unpack(_allsum_small(_pack([Gl[n] for n in small]), "allsum_small"), full_shapes)
    gs = {}
    for n, g in zip(small, summed):
        if n in TINY:
            wdt = W[n].shape[2]
            g = lax.dynamic_slice_in_dim(g, chip * wdt, wdt, axis=2)
        gs[n] = g
    own_shapes = [W[n].shape for n in small]
    pk = lambda src: _pack([src[n] for n in small])[None]
    d_s, m_s, v_s = _adamw(pk(W), pk(gs), pk(M1), pk(V2), "adamw_small")
    for key, buf in (("d", d_s), ("m", m_s), ("v", v_s)):
        for n, a in zip(small, _unpack(buf[0], own_shapes)):
            res[key, n] = a
    for n in small:
        res["g", n] = gs[n]

    return (loss, grad_x, *[res["g", n] for n in ORDER], *[res["d", n] for n in ORDER],
            *[res["m", n] for n in ORDER], *[res["v", n] for n in ORDER])
```

```python
import math

import numpy as np
import jax
import jax.numpy as jnp
from jax import lax
from jax.experimental import pallas as pl
from jax.experimental.pallas import tpu as pltpu

F32, BF16 = jnp.float32, jnp.bfloat16
SDS = jax.ShapeDtypeStruct
BS = pl.BlockSpec
MESH = pl.DeviceIdType.MESH

D = 1024
DEPTH = 2
HEADS = 4
D_FF = 2816
D_INP = 2048
EPS = 1e-6
SWA_WINDOW = 128
BLK = 128
SLOPES = tuple(2.0 ** (-8.0 * (i + 1) / 4) for i in range(4))
QK_SCALE = 1.0 / math.sqrt(96)
SWA_SCALE = 1.0 / math.sqrt(64)
LR, B1, B2, ADAM_EPS, WD, STEP = 0.001, 0.9, 0.999, 1e-08, 0.01, 10

LANES = 1024
VMEM_LIMIT = 56 * 1024 * 1024
NEG_INF = float("-inf")

C_CQ, C_CKV, C_KR, C_GB, C_GC, C_UC, C_UP, C_QS, C_KS, C_VS = 0, 256, 384, 512, 768, 1024, 1280, 1536, 1792, 1920


def _params(ngrid):
    return pltpu.CompilerParams(dimension_semantics=("arbitrary",) * ngrid, vmem_limit_bytes=VMEM_LIMIT)


def _pc(body, *, name, grid, in_specs, out_specs, out_shape, scratch=(), aliases=None):
    return pl.pallas_call(
        body, name=name, grid=grid, in_specs=in_specs, out_specs=out_specs, out_shape=out_shape,
        scratch_shapes=scratch, input_output_aliases=aliases or {}, compiler_params=_params(len(grid)))


def _dot(a, b):
    return jnp.dot(a, b, preferred_element_type=F32)


def _dot_nt(a, b):
    return lax.dot_general(a, b, (((1,), (1,)), ((), ())), preferred_element_type=F32)


def _dot_tn(a, b):
    return lax.dot_general(a, b, (((0,), (0,)), ((), ())), preferred_element_type=F32)


def _tile(n, cap):
    if n <= cap:
        return n
    t = cap - cap % 128
    while n % t:
        t -= 128
    return t


def _row_tile(a, b, cap=262144):
    bp = -(-b // 128) * 128
    best = None
    for t in range(8, a + 1, 8):
        if a % t == 0 and t * bp <= cap:
            best = t
    return best if best is not None else a


def _g3(a):
    return a.reshape(a.shape[0], 1, a.shape[1])


def _norm_mm(x, g3, l, w, wspec, N, tn, out_dtype, name):
    S, K = x.shape
    tm = min(1024 if out_dtype == BF16 else 512, S)

    def body(x_ref, g_ref, w_ref, y_ref, h_ref):
        @pl.when(pl.program_id(1) == 0)
        def _():
            xv = x_ref[...]
            r = lax.rsqrt(jnp.mean(xv * xv, axis=-1, keepdims=True) + EPS)
            h_ref[...] = (xv * r * g_ref[...]).astype(BF16)

        y_ref[...] = _dot(h_ref[...], w_ref[...]).astype(out_dtype)

    return _pc(body, name=name, grid=(S // tm, N // tn),
               in_specs=[BS((tm, K), lambda i, j: (i, 0)), BS((None, 1, K), lambda i, j: (l, 0, 0)), wspec],
               out_specs=[BS((tm, tn), lambda i, j: (i, j)), BS((tm, K), lambda i, j: (i, 0))],
               out_shape=[SDS((S, N), out_dtype), SDS((S, K), BF16)])(x, g3, w)


def _wspec_in(l):
    return BS((None, D, D_INP), lambda i, j: (l, 0, j))


def _wspec_gu(l):
    return BS((None, None, D, 2 * D_FF // 4), lambda i, j: (l, j, 0, 0))


def _mix_out(x0, ya, yb, yc, yd, gmix3, wo, l, name):
    S = x0.shape[0]
    tm = min(512, S)

    def body(x_ref, ya_ref, yb_ref, yc_ref, yd_ref, g_ref, w_ref, x1_ref, ycat_ref, mixed_ref):
        groups = [ya_ref[...], yb_ref[...], yc_ref[...], yd_ref[...]]
        for gi, yg in enumerate(groups):
            sl = slice(gi * 256, (gi + 1) * 256)
            r = lax.rsqrt(jnp.mean(yg * yg, axis=-1, keepdims=True) + EPS)
            ycat_ref[:, sl] = yg
            mixed_ref[:, sl] = (yg * r * g_ref[:, sl]).astype(BF16)
        x1_ref[...] = x_ref[...] + _dot(mixed_ref[...], w_ref[...])

    row = lambda w: BS((tm, w), lambda i: (i, 0))
    return _pc(body, name=name, grid=(S // tm,),
               in_specs=[row(D), row(256), row(256), row(256), row(256), BS((None, 1, D), lambda i: (l, 0, 0)),
                         BS((None, D, D), lambda i: (l, 0, 0))],
               out_specs=[row(D), row(D), row(D)],
               out_shape=[SDS((S, D), F32), SDS((S, D), F32), SDS((S, D), BF16)])(x0, ya, yb, yc, yd, gmix3, wo)


def _swiglu_mm_res(x1, gu, wdown, l, name):
    S = x1.shape[0]
    tm = min(256, S)

    def body(x_ref, gate_ref, up_ref, w_ref, x2_ref, act_ref):
        gt = gate_ref[...].astype(F32)
        act = (gt * pl.reciprocal(1.0 + jnp.exp(-gt), approx=True) * up_ref[...].astype(F32)).astype(BF16)
        act_ref[...] = act
        x2_ref[...] = x_ref[...] + _dot(act, w_ref[...])

    return _pc(body, name=name, grid=(S // tm,),
               in_specs=[BS((tm, D), lambda i: (i, 0)), BS((tm, D_FF), lambda i: (i, 0)),
                         BS((tm, D_FF), lambda i: (i, 1)), BS((None, D_FF, D), lambda i: (l, 0, 0))],
               out_specs=[BS((tm, D), lambda i: (i, 0)), BS((tm, D_FF), lambda i: (i, 0))],
               out_shape=[SDS((S, D), F32), SDS((S, D_FF), BF16)])(x1, gu, gu, wdown)


def _loss_head(x, g, tgt, name):
    S = x.shape[0]
    tm = min(512, S)

    def body(x_ref, g_ref, t_ref, dx_ref, dx16_ref, dg_ref, loss_ref):
        @pl.when(pl.program_id(0) == 0)
        def _():
            dg_ref[...] = jnp.zeros_like(dg_ref)
            loss_ref[...] = jnp.zeros_like(loss_ref)

        xv = x_ref[...]
        r = lax.rsqrt(jnp.mean(xv * xv, axis=-1, keepdims=True) + EPS)
        xh = xv * r
        gv = g_ref[...]
        diff = xh * gv - t_ref[...]
        loss_ref[...] += jnp.sum(diff * diff)
        dy = diff * (1.0 / D)
        dg_ref[...] += jnp.sum(dy * xh, axis=0, keepdims=True)
        dxh = dy * gv
        dx = r * (dxh - xh * jnp.mean(dxh * xh, axis=-1, keepdims=True))
        dx_ref[...] = dx
        dx16_ref[...] = dx.astype(BF16)

    row = BS((tm, D), lambda i: (i, 0))
    return _pc(body, name=name, grid=(S // tm,),
               in_specs=[row, BS((1, D), lambda i: (0, 0)), row],
               out_specs=[row, row, BS((8, D), lambda i: (0, 0)), BS((8, 128), lambda i: (0, 0))],
               out_shape=[SDS((S, D), F32), SDS((S, D), BF16), SDS((8, D), F32), SDS((8, 128), F32)])(x, g, tgt)


def _mm_tn(a, b, l, prev, name, split4=False):
    S, Ka = a.shape
    N = b.shape[1]
    if split4:
        ta, tn = _tile(Ka, 256), N // 4
        out_shape = SDS((2, 4, Ka, tn), F32)
        out_spec = BS((None, None, ta, tn), lambda j, i: (l, j, i, 0))
    else:
        ta, tn = _tile(Ka, 512), _tile(N, 1024)
        out_shape = SDS((2, Ka, N), F32)
        out_spec = BS((None, ta, tn), lambda j, i: (l, i, j))

    def body(a_ref, b_ref, *rest):
        rest[-1][...] = _dot_tn(a_ref[...], b_ref[...])

    in_specs = [BS((S, ta), lambda j, i: (0, i)), BS((S, tn), lambda j, i: (0, j))]
    args = [a, b]
    if prev is not None:
        in_specs.append(BS(memory_space=pl.ANY))
        args.append(prev)
    return _pc(body, name=name, grid=(N // tn, Ka // ta), in_specs=in_specs, out_specs=out_spec, out_shape=out_shape,
               aliases={2: 0} if prev is not None else None)(*args)


def _bwd_down(dx16, wdown, gu, l, name):
    S = dx16.shape[0]
    tm = min(256, S)

    def body(dx_ref, w_ref, gate_ref, up_ref, dgu_ref):
        dxv = dx_ref[...]
        for c0 in range(0, D_FF, 256):
            cs = slice(c0, c0 + 256)
            dact = _dot_nt(dxv, w_ref[cs, :])
            gt = gate_ref[:, cs].astype(F32)
            sg = pl.reciprocal(1.0 + jnp.exp(-gt), approx=True)
            dgu_ref[:, cs] = (dact * up_ref[:, cs].astype(F32) * (sg * (1.0 + gt * (1.0 - sg)))).astype(BF16)
            dgu_ref[:, D_FF + c0:D_FF + c0 + 256] = (dact * (gt * sg)).astype(BF16)

    return _pc(body, name=name, grid=(S // tm,),
               in_specs=[BS((tm, D), lambda i: (i, 0)), BS((None, D_FF, D), lambda i: (l, 0, 0)),
                         BS((tm, D_FF), lambda i: (i, 0)), BS((tm, D_FF), lambda i: (i, 1))],
               out_specs=BS((tm, 2 * D_FF), lambda i: (i, 0)),
               out_shape=SDS((S, 2 * D_FF), BF16))(dx16, wdown, gu, gu)


def _mm_nt_normbwd(dy, w4, l, x, g3, dres, ngroups, name, rider=None):
    S, K = dy.shape
    _, nk, _, kc = w4.shape
    tm = min(512, S)
    gw = D // ngroups
    has_res = dres is not None
    nr = rider.n if rider else 0
    n_in, n_out = 4 + has_res, 2 + has_res

    def body(*refs):
        dy_ref, w_ref, x_ref, g_ref = refs[:4]
        res_ref = refs[4] if has_res else None
        outs = refs[n_in + nr:n_in + nr + n_out]
        dx_ref, dg_ref = outs[0], outs[-1]
        dx16_ref = outs[1] if has_res else None
        r_io = (refs[n_in:n_in + nr], refs[n_in + nr + n_out:n_in + 2 * nr + n_out], refs[n_in + 2 * nr + n_out:])
        if rider:
            pl.when(pl.program_id(0) == 0)(lambda: rider.start(*r_io))

        @pl.when(pl.program_id(0) == 0)
        def _():
            dg_ref[...] = jnp.zeros_like(dg_ref)

        dh = _dot_nt(dy_ref[:, 0:kc], w_ref[0])
        for k in range(1, nk):
            dh = dh + _dot_nt(dy_ref[:, k * kc:(k + 1) * kc], w_ref[k])
        for gi in range(ngroups):
            sl = slice(gi * gw, (gi + 1) * gw)
            xg = x_ref[:, sl]
            r = lax.rsqrt(jnp.mean(xg * xg, axis=-1, keepdims=True) + EPS)
            xh = xg * r
            dhg = dh[:, sl]
            dg_ref[:, sl] += jnp.sum(dhg * xh, axis=0, keepdims=True)
            dxh = dhg * g_ref[:, sl]
            dxg = r * (dxh - xh * jnp.mean(dxh * xh, axis=-1, keepdims=True))
            if has_res:
                dxg = dxg + res_ref[:, sl]
                dx16_ref[:, sl] = dxg.astype(BF16)
            dx_ref[:, sl] = dxg
        if rider:
            pl.when(pl.program_id(0) == S // tm - 1)(lambda: rider.wait(*r_io))

    row = BS((tm, D), lambda i: (i, 0))
    in_specs = [BS((tm, K), lambda i: (i, 0)),
                BS((None, nk, D, kc), lambda i: (l, 0, 0, 0), pipeline_mode=pl.Buffered(1)), row,
                BS((None, 1, D), lambda i: (l, 0, 0))]
    args = [dy, w4, x, g3]
    out_specs, out_shape = [row], [SDS((S, D), F32)]
    if has_res:
        in_specs.append(row)
        args.append(dres)
        out_specs.append(row)
        out_shape.append(SDS((S, D), BF16))
    out_specs.append(BS((8, D), lambda i: (0, 0)))
    out_shape.append(SDS((8, D), F32))
    if not rider:
        return _pc(body, name=name, grid=(S // tm,), in_specs=in_specs, out_specs=out_specs, out_shape=out_shape)(*args)
    out = pl.pallas_call(body, name=name, grid=(S // tm,), in_specs=in_specs + [ANY] * nr,
                         out_specs=out_specs + [ANY] * nr, out_shape=out_shape + rider.out_shape,
                         scratch_shapes=rider.scratch(),
                         compiler_params=pltpu.CompilerParams(dimension_semantics=("arbitrary",),
                                                              vmem_limit_bytes=VMEM_LIMIT, has_side_effects=True))(
        *args, *rider.arrs)
    return (*out[:n_out], list(out[n_out:]))


def _rope(x, c, s1, s2):
    return x * c + pltpu.roll(x, 112, axis=1) * s1 + pltpu.roll(x, 16, axis=1) * s2


def _rope_t(dy, c, s1, s2):
    return dy * c + pltpu.roll(dy * s1, 16, axis=1) + pltpu.roll(dy * s2, 112, axis=1)


def _mla_prep(proj, gq3, gkv3, wuq, wk, wv, tabs, l, name):
    S = proj.shape[0]
    tm = min(512, S)
    tc, ts1, ts2 = tabs

    def body(cq_ref, ckv_ref, kr_ref, gq_ref, gkv_ref, wuq_ref, wk_ref, wv_ref, c_ref, s1_ref, s2_ref,
             q_ref, k_ref, v_ref, kt_ref, vt_ref):
        c, s1, s2 = c_ref[...], s1_ref[...], s2_ref[...]
        cq = cq_ref[...]
        rq = lax.rsqrt(jnp.mean(cq * cq, axis=-1, keepdims=True) + EPS)
        qa = _dot((cq * rq * gq_ref[...]).astype(BF16), wuq_ref[...])
        ckv = ckv_ref[...]
        rkv = lax.rsqrt(jnp.mean(ckv * ckv, axis=-1, keepdims=True) + EPS)
        ckvn = (ckv * rkv * gkv_ref[...]).astype(BF16)
        ka = _dot(ckvn, wk_ref[...])
        va = _dot(ckvn, wv_ref[...])
        v_ref[...] = va.astype(BF16)
        vt_ref[...] = va.T.astype(BF16)
        krr = _rope(kr_ref[...], c, s1, s2)
        for h in range(HEADS):
            sl = slice(h * 128, (h + 1) * 128)
            q_ref[:, sl] = (_rope(qa[:, sl], c, s1, s2) * QK_SCALE).astype(BF16)
            kh = ka[:, sl] + krr
            k_ref[:, sl] = kh.astype(BF16)
            kt_ref[sl, :] = kh.T.astype(BF16)

    lay = lambda a, b: BS((None, a, b), lambda i: (l, 0, 0))
    tab = BS((tm, 128), lambda i: (i, 0))
    return _pc(body, name=name, grid=(S // tm,),
               in_specs=[BS((tm, 256), lambda i: (i, 0)), BS((tm, 128), lambda i: (i, 2)), BS((tm, 128), lambda i: (i, 3)),
                         lay(1, 256), lay(1, 128), lay(256, 512), lay(128, 512), lay(128, 512), tab, tab, tab],
               out_specs=[BS((tm, 512), lambda i: (i, 0))] * 3 + [BS((512, tm), lambda i: (0, i))] * 2,
               out_shape=[SDS((S, 512), BF16)] * 3 + [SDS((512, S), BF16)] * 2)(
        proj, proj, proj, gq3, gkv3, wuq, wk, wv, tc, ts1, ts2)


def _mla_prep_bwd(dq, dk, dv, proj, gq3, gkv3, wuq, wk, wv, tabs, l, name):
    S = proj.shape[0]
    tm = min(512, S)
    tc, ts1, ts2 = tabs

    def body(dq_ref, dk_ref, dv_ref, cq_ref, ckv_ref, gq_ref, gkv_ref, wuq_ref, wk_ref, wv_ref, c_ref, s1_ref, s2_ref,
             dcq_ref, dckv_ref, dkr_ref, dwuq_ref, dwk_ref, dwv_ref, dgq_ref, dgkv_ref):
        @pl.when(pl.program_id(0) == 0)
        def _():
            for r in (dwuq_ref, dwk_ref, dwv_ref, dgq_ref, dgkv_ref):
                r[...] = jnp.zeros_like(r)

        c, s1, s2 = c_ref[...], s1_ref[...], s2_ref[...]
        dqp = jnp.concatenate(
            [_rope_t(dq_ref[h * 128:(h + 1) * 128, :].T * QK_SCALE, c, s1, s2) for h in range(HEADS)], axis=1).astype(BF16)
        cq = cq_ref[...]
        rq = lax.rsqrt(jnp.mean(cq * cq, axis=-1, keepdims=True) + EPS)
        cqh = cq * rq
        gq_v = gq_ref[...]
        dwuq_ref[...] += _dot_tn((cqh * gq_v).astype(BF16), dqp)
        dcqn = _dot_nt(dqp, wuq_ref[...])
        dgq_ref[...] += jnp.sum(dcqn * cqh, axis=0, keepdims=True)
        dxh = dcqn * gq_v
        dcq_ref[...] = (rq * (dxh - cqh * jnp.mean(dxh * cqh, axis=-1, keepdims=True))).astype(BF16)

        dkb = dk_ref[...].astype(BF16)
        dvb = dv_ref[...].astype(BF16)
        ckv = ckv_ref[...]
        rkv = lax.rsqrt(jnp.mean(ckv * ckv, axis=-1, keepdims=True) + EPS)
        ckh = ckv * rkv
        gkv_v = gkv_ref[...]
        ckvn = (ckh * gkv_v).astype(BF16)
        dwk_ref[...] += _dot_tn(ckvn, dkb)
        dwv_ref[...] += _dot_tn(ckvn, dvb)
        dckvn = _dot_nt(dkb, wk_ref[...]) + _dot_nt(dvb, wv_ref[...])
        dgkv_ref[...] += jnp.sum(dckvn * ckh, axis=0, keepdims=True)
        dyh = dckvn * gkv_v
        dckv_ref[...] = (rkv * (dyh - ckh * jnp.mean(dyh * ckh, axis=-1, keepdims=True))).astype(BF16)
        dks = dk_ref[:, 0:128] + dk_ref[:, 128:256] + dk_ref[:, 256:384] + dk_ref[:, 384:512]
        dkr_ref[...] = _rope_t(dks, c, s1, s2).astype(BF16)

    full = lambda a, b: BS((a, b), lambda i: (0, 0))
    lay = lambda a, b: BS((None, a, b), lambda i: (l, 0, 0))
    tab = BS((tm, 128), lambda i: (i, 0))
    row = lambda w: BS((tm, w), lambda i: (i, 0))
    return _pc(body, name=name, grid=(S // tm,),
               in_specs=[BS((512, tm), lambda i: (0, i)), row(512), row(512), BS((tm, 256), lambda i: (i, 0)),
                         BS((tm, 128), lambda i: (i, 2)),
                         lay(1, 256), lay(1, 128), lay(256, 512), lay(128, 512), lay(128, 512), tab, tab, tab],
               out_specs=[row(256), row(128), row(128), full(256, 512), full(128, 512), full(128, 512),
                          full(8, 256), full(8, 128)],
               out_shape=[SDS((S, 256), BF16), SDS((S, 128), BF16), SDS((S, 128), BF16), SDS((256, 512), F32),
                          SDS((128, 512), F32), SDS((128, 512), F32), SDS((8, 256), F32), SDS((8, 128), F32)])(
        dq, dk, dv, proj, proj, gq3, gkv3, wuq, wk, wv, tc, ts1, ts2)


def _causal_steps(n, q_outer):
    if q_outer:
        pairs = [(i, j) for i in range(n) for j in range(i + 1)]
    else:
        pairs = [(i, j) for j in range(n) for i in range(j, n)]
    return jnp.asarray([p[0] for p in pairs], jnp.int32), jnp.asarray([p[1] for p in pairs], jnp.int32)


def _mla_attn(q, k, vt, gts, layer, name):
    S = q.shape[0]
    t = min(512, S)
    n = S // t
    ng = len(gts)

    qi, kj = _causal_steps(n, True)
    last = qi.shape[0] - 1

    def body(qi_ref, kj_ref, q_ref, k_ref, vt_ref, *rest):
        (ya_ref, lse_ref), g_refs = rest[ng:ng + 2], rest[ng + 2:2 * ng + 2]
        m_sc, l_sc, acc_sc = rest[2 * ng + 2:2 * ng + 5]
        i, j = qi_ref[pl.program_id(1)], kj_ref[pl.program_id(1)]
        if ng:
            phases = _gather_phases(g_refs, [g.shape for g in gts], rest[2 * ng + 5], rest[2 * ng + 6], layer)
            for ph, (pp, ss) in zip(phases[:2], ((0, 0), (1, 0))):
                pl.when((pl.program_id(0) == pp) & (pl.program_id(1) == ss))(ph)

        @pl.when(j == 0)
        def _():
            m_sc[...] = jnp.full_like(m_sc, NEG_INF)
            l_sc[...] = jnp.zeros_like(l_sc)
            acc_sc[...] = jnp.zeros_like(acc_sc)

        def step(masked):
            for hh in range(2):
                sl = slice(hh * 128, (hh + 1) * 128)
                st = _dot_nt(k_ref[:, sl], q_ref[:, sl])
                if masked:
                    key = lax.broadcasted_iota(jnp.int32, (t, t), 0)
                    qry = lax.broadcasted_iota(jnp.int32, (t, t), 1)
                    st = jnp.where(key <= qry, st, NEG_INF)
                m_prev = m_sc[hh]
                m_new = jnp.maximum(m_prev, jnp.max(st, axis=0, keepdims=True))
                p = jnp.exp(st - m_new)
                alpha = jnp.exp(m_prev - m_new)
                l_sc[hh] = alpha * l_sc[hh] + jnp.sum(p, axis=0, keepdims=True)
                acc_sc[hh] = alpha * acc_sc[hh] + _dot(vt_ref[sl, :], p.astype(BF16))
                m_sc[hh] = m_new

        @pl.when(j < i)
        def _():
            step(False)

        @pl.when(j == i)
        def _():
            step(True)
            ya_ref[...] = (acc_sc[0] / l_sc[0] + acc_sc[1] / l_sc[1]).T
            for hh in range(2):
                lse_ref[hh] = m_sc[hh] + jnp.log(l_sc[hh])

        if ng:
            pl.when((pl.program_id(0) == 1) & (pl.program_id(1) == last))(phases[2])

    gs = pltpu.PrefetchScalarGridSpec(
        num_scalar_prefetch=2, grid=(2, qi.shape[0]),
        in_specs=[BS((t, 256), lambda p, s, qi, kj: (qi[s], p)), BS((t, 256), lambda p, s, qi, kj: (kj[s], p)),
                  BS((256, t), lambda p, s, qi, kj: (p, kj[s]))] + [ANY] * ng,
        out_specs=[BS((t, 128), lambda p, s, qi, kj: (qi[s], p)), BS((2, 1, t), lambda p, s, qi, kj: (p, 0, qi[s]))]
        + [ANY] * ng,
        scratch_shapes=[pltpu.VMEM((2, 1, t), F32), pltpu.VMEM((2, 1, t), F32), pltpu.VMEM((2, 128, t), F32)]
        + ([pltpu.SemaphoreType.DMA((7 * ng,)), pltpu.SemaphoreType.DMA((7 * ng,))] if ng else []))
    out = pl.pallas_call(body, name=name, grid_spec=gs,
                         out_shape=[SDS((S, 256), F32), SDS((HEADS, 1, S), F32)] + [SDS(g.shape, g.dtype) for g in gts],
                         input_output_aliases={5 + m: 2 + m for m in range(ng)},
                         compiler_params=pltpu.CompilerParams(dimension_semantics=("arbitrary",) * 2,
                                                              vmem_limit_bytes=VMEM_LIMIT, has_side_effects=bool(ng)))(
        qi, kj, q, k, vt, *gts)
    return out[0], out[1], list(out[2:])


def _mla_delta(dycat, ya, name):
    S = ya.shape[0]
    t = min(512, S)

    def body(do_ref, ya_ref, d_ref):
        prod = do_ref[...] * ya_ref[...]
        for p in range(2):
            pt = prod[:, p * 128:(p + 1) * 128].T
            d_ref[2 * p] = jnp.sum(pt[0:64, :], axis=0, keepdims=True)
            d_ref[2 * p + 1] = jnp.sum(pt[64:128, :], axis=0, keepdims=True)

    return _pc(body, name=name, grid=(S // t,),
               in_specs=[BS((t, 256), lambda i: (i, 0)), BS((t, 256), lambda i: (i, 0))],
               out_specs=BS((HEADS, 1, t), lambda i: (0, 0, i)), out_shape=SDS((HEADS, 1, S), F32))(dycat, ya)


def _mla_attn_bwd(q, k, kt, v, dya, lse, delta, rider, name):
    S = q.shape[0]
    t = min(512, S)
    n = S // t
    nr = rider.n if rider else 0

    qi, kj = _causal_steps(n, False)
    last = qi.shape[0] - 1

    def body(qi_ref, kj_ref, q_ref, k_ref, kt_ref, v_ref, do_ref, lse_ref, delta_ref, *rest):
        dqt_ref, dk_ref, dv_ref = rest[nr:nr + 3]
        r_io = (rest[:nr], rest[nr + 3:2 * nr + 3], rest[2 * nr + 3:])
        i, j = qi_ref[pl.program_id(1)], kj_ref[pl.program_id(1)]
        if rider:
            pl.when((pl.program_id(0) == 0) & (pl.program_id(1) == 0))(lambda: rider.start(*r_io))

        @pl.when(pl.program_id(1) == 0)
        def _():
            dqt_ref[...] = jnp.zeros_like(dqt_ref)

        @pl.when(i == j)
        def _():
            dk_ref[...] = jnp.zeros_like(dk_ref)
            dv_ref[...] = jnp.zeros_like(dv_ref)

        def step(masked):
            dob = do_ref[...].astype(BF16)
            cols = pl.ds(pl.multiple_of(i * t, t), t)
            for hh in range(2):
                sl = slice(hh * 128, (hh + 1) * 128)
                qv = q_ref[:, sl]
                p = jnp.exp(_dot_nt(k_ref[:, sl], qv) - lse_ref[hh])
                if masked:
                    key = lax.broadcasted_iota(jnp.int32, (t, t), 0)
                    qry = lax.broadcasted_iota(jnp.int32, (t, t), 1)
                    p = jnp.where(key <= qry, p, 0.0)
                dv_ref[:, sl] += _dot(p.astype(BF16), dob)
                ds = (p * (_dot_nt(v_ref[:, sl], dob) - delta_ref[hh])).astype(BF16)
                dk_ref[:, sl] += _dot(ds, qv)
                dqt_ref[sl, cols] += _dot(kt_ref[sl, :], ds)

        @pl.when(i > j)
        def _():
            step(False)

        @pl.when(i == j)
        def _():
            step(True)

        if rider:
            pl.when((pl.program_id(0) == 1) & (pl.program_id(1) == last))(lambda: rider.wait(*r_io))

    qs = BS((t, 256), lambda p, s, qi, kj: (qi[s], p))
    ks = BS((t, 256), lambda p, s, qi, kj: (kj[s], p))
    rowv = BS((2, 1, t), lambda p, s, qi, kj: (p, 0, qi[s]))
    gs = pltpu.PrefetchScalarGridSpec(
        num_scalar_prefetch=2, grid=(2, qi.shape[0]),
        in_specs=[qs, ks, BS((256, t), lambda p, s, qi, kj: (p, kj[s])), ks,
                  BS((t, 128), lambda p, s, qi, kj: (qi[s], p)), rowv, rowv] + [ANY] * nr,
        out_specs=[BS((256, S), lambda p, s, qi, kj: (p, 0)), ks, ks] + [ANY] * nr,
        scratch_shapes=rider.scratch() if rider else [])
    out = pl.pallas_call(body, name=name, grid_spec=gs,
                         out_shape=[SDS((512, S), F32), SDS((S, 512), F32), SDS((S, 512), F32)]
                         + (rider.out_shape if rider else []),
                         compiler_params=pltpu.CompilerParams(dimension_semantics=("arbitrary",) * 2,
                                                              vmem_limit_bytes=VMEM_LIMIT, has_side_effects=bool(rider)))(
        qi, kj, q, k, kt, v, dya, lse, delta, *(rider.arrs if rider else []))
    return out[0], out[1], out[2], list(out[3:])


def _swa_scores(qm, kk, valid, distf, slope, sink):
    sc = _dot_nt(qm, kk) * SWA_SCALE
    sc = jnp.where(valid, sc - slope * distf, NEG_INF)
    m = jnp.maximum(jnp.max(sc, axis=-1, keepdims=True), sink)
    e = jnp.exp(sc - m)
    esink = jnp.exp(sink - m)
    den = jnp.sum(e, axis=-1, keepdims=True) + esink
    return e / den, esink / den


def _swa_masks():
    r = lax.broadcasted_iota(jnp.int32, (BLK, 2 * BLK), 0)
    c = lax.broadcasted_iota(jnp.int32, (BLK, 2 * BLK), 1)
    dist = r + BLK - c
    return (dist >= 0) & (dist < SWA_WINDOW), c >= BLK, dist.astype(F32)


def _to_half(xb, pos, b):
    return xb if pos == b else pltpu.roll(xb, 64, axis=1)


def _swa(proj, sinks, l, name):
    S = proj.shape[0]
    nb = S // BLK

    def body(q_ref, k_ref, v_ref, sink_ref, o_ref, kp, vp):
        kp[0:BLK, :] = jnp.zeros((BLK, 128), BF16)
        vp[0:BLK, :] = jnp.zeros((BLK, 128), BF16)
        kp[BLK:, :] = k_ref[...].astype(BF16)
        vp[BLK:, :] = v_ref[...].astype(BF16)
        lo = lax.broadcasted_iota(jnp.int32, (BLK, 128), 1) < 64
        band, cur, distf = _swa_masks()

        def blk(i, carry):
            st = pl.multiple_of(i * BLK, BLK)
            kk = kp[pl.ds(st, 2 * BLK), :]
            vv = vp[pl.ds(st, 2 * BLK), :]
            valid = band & (cur | (i > 0))
            for b in range(2):
                half = lo if b == 0 else ~lo
                qb = q_ref[pl.ds(st, BLK), b * 128:(b + 1) * 128]
                outs = []
                for pos in range(2):
                    h = 2 * b + pos
                    qm = jnp.where(half, _to_half(qb, pos, b), 0.0).astype(BF16)
                    p, _ = _swa_scores(qm, kk, valid, distf, SLOPES[h], sink_ref[l, h])
                    outs.append(_to_half(_dot(p.astype(BF16), vv), pos, b))
                o_ref[pl.ds(st, BLK), b * 128:(b + 1) * 128] = jnp.where(lo, outs[0], outs[1])
            return carry

        lax.fori_loop(0, nb, blk, 0, unroll=2)

    return _pc(body, name=name, grid=(1,),
               in_specs=[BS((S, 256), lambda i: (0, C_QS // 256)), BS((S, 128), lambda i: (0, C_KS // 128)),
                         BS((S, 128), lambda i: (0, C_VS // 128)), BS(memory_space=pltpu.SMEM)],
               out_specs=BS((S, 256), lambda i: (0, 0)),
               out_shape=SDS((S, 256), F32),
               scratch=[pltpu.VMEM((S + BLK, 128), BF16), pltpu.VMEM((S + BLK, 128), BF16)])(proj, proj, proj, sinks)


def _swa_bwd(proj, sinks, dyd, l, name):
    S = proj.shape[0]
    nb = S // BLK

    def body(q_ref, k_ref, v_ref, sink_ref, do_ref, dq_ref, dk_ref, dv_ref, dsink_ref, kp, vp, dkp, dvp):
        kp[0:BLK, :] = jnp.zeros((BLK, 128), BF16)
        vp[0:BLK, :] = jnp.zeros((BLK, 128), BF16)
        kp[BLK:, :] = k_ref[...].astype(BF16)
        vp[BLK:, :] = v_ref[...].astype(BF16)
        dkp[...] = jnp.zeros_like(dkp)
        dvp[...] = jnp.zeros_like(dvp)
        lo = lax.broadcasted_iota(jnp.int32, (BLK, 128), 1) < 64
        lane8 = lax.broadcasted_iota(jnp.int32, (8, 128), 1)
        band, cur, distf = _swa_masks()

        def blk(i, dsink):
            st = pl.multiple_of(i * BLK, BLK)
            kk = kp[pl.ds(st, 2 * BLK), :]
            vv = vp[pl.ds(st, 2 * BLK), :]
            valid = band & (cur | (i > 0))
            dkk = jnp.zeros((2 * BLK, 128), F32)
            dvv = jnp.zeros((2 * BLK, 128), F32)
            for b in range(2):
                half = lo if b == 0 else ~lo
                qb = q_ref[pl.ds(st, BLK), b * 128:(b + 1) * 128]
                dob = do_ref[pl.ds(st, BLK), b * 128:(b + 1) * 128]
                dqs = []
                for pos in range(2):
                    h = 2 * b + pos
                    qm = jnp.where(half, _to_half(qb, pos, b), 0.0).astype(BF16)
                    dom = jnp.where(half, _to_half(dob, pos, b), 0.0).astype(BF16)
                    p, psink = _swa_scores(qm, kk, valid, distf, SLOPES[h], sink_ref[l, h])
                    dp = _dot_nt(dom, vv)
                    dvv = dvv + _dot_tn(p.astype(BF16), dom)
                    delta = jnp.sum(p * dp, axis=-1, keepdims=True)
                    dsink = dsink + jnp.where(lane8 == h, -jnp.sum(psink * delta), 0.0)
                    dsc = (p * (dp - delta) * SWA_SCALE).astype(BF16)
                    dqs.append(_to_half(_dot(dsc, kk), pos, b))
                    dkk = dkk + _dot_tn(dsc, qm)
                dq_ref[pl.ds(st, BLK), b * 128:(b + 1) * 128] = jnp.where(lo, dqs[0], dqs[1]).astype(BF16)
            dkp[pl.ds(st, 2 * BLK), :] += dkk
            dvp[pl.ds(st, 2 * BLK), :] += dvv
            return dsink

        dsink_ref[...] = lax.fori_loop(0, nb, blk, jnp.zeros((8, 128), F32), unroll=2)
        dk_ref[...] = dkp[BLK:, :].astype(BF16)
        dv_ref[...] = dvp[BLK:, :].astype(BF16)

    return _pc(body, name=name, grid=(1,),
               in_specs=[BS((S, 256), lambda i: (0, C_QS // 256)), BS((S, 128), lambda i: (0, C_KS // 128)),
                         BS((S, 128), lambda i: (0, C_VS // 128)), BS(memory_space=pltpu.SMEM),
                         BS((S, 256), lambda i: (0, 3))],
               out_specs=[BS((S, 256), lambda i: (0, 0)), BS((S, 128), lambda i: (0, 0)), BS((S, 128), lambda i: (0, 0)),
                          BS((8, 128), lambda i: (0, 0))],
               out_shape=[SDS((S, 256), BF16), SDS((S, 128), BF16), SDS((S, 128), BF16), SDS((8, 128), F32)],
               scratch=[pltpu.VMEM((S + BLK, 128), BF16), pltpu.VMEM((S + BLK, 128), BF16),
                        pltpu.VMEM((S + BLK, 128), F32), pltpu.VMEM((S + BLK, 128), F32)])(proj, proj, proj, sinks, dyd)


def _down(x, k, t):
    return jnp.where(t >= k, pltpu.roll(x, k, axis=0), 0.0)


def _up(x, k, t):
    n = x.shape[0]
    return jnp.where(t < n - k, pltpu.roll(x, n - k, axis=0), 0.0)


def _conv(proj, w8, l, name):
    S = proj.shape[0]

    def body(gb_ref, gc_ref, u_ref, w_ref, y_ref):
        t = lax.broadcasted_iota(jnp.int32, (S, 128), 0)
        z = gc_ref[...] * u_ref[...]
        c = w_ref[2:3, :] * z + w_ref[1:2, :] * _down(z, 1, t) + w_ref[0:1, :] * _down(z, 2, t)
        y_ref[...] = gb_ref[...] * c

    col = lambda c0: BS((S, 128), lambda i: (0, c0 // 128 + i))
    return _pc(body, name=name, grid=(2,),
               in_specs=[col(C_GB), col(C_GC), col(C_UC), BS((None, 8, 128), lambda i: (l, 0, i))],
               out_specs=BS((S, 128), lambda i: (0, i)), out_shape=SDS((S, 256), F32))(proj, proj, proj, w8)


def _conv_bwd(proj, w8, dycat, l, name):
    S = proj.shape[0]

    def body(gb_ref, gc_ref, u_ref, w_ref, dy_ref, dgb_ref, dgc_ref, du_ref, dw_ref):
        t = lax.broadcasted_iota(jnp.int32, (S, 128), 0)
        gc, u = gc_ref[...], u_ref[...]
        z = gc * u
        z1, z2 = _down(z, 1, t), _down(z, 2, t)
        w0, w1, w2 = w_ref[0:1, :], w_ref[1:2, :], w_ref[2:3, :]
        dy = dy_ref[...]
        dgb_ref[...] = (dy * (w2 * z + w1 * z1 + w0 * z2)).astype(BF16)
        dc = dy * gb_ref[...]
        dz = w2 * dc + w1 * _up(dc, 1, t) + w0 * _up(dc, 2, t)
        dgc_ref[...] = (dz * u).astype(BF16)
        du_ref[...] = (dz * gc).astype(BF16)
        row = lax.broadcasted_iota(jnp.int32, (8, 128), 0)
        sums = [jnp.sum(dc * zz, axis=0, keepdims=True) for zz in (z2, z1, z)]
        dw_ref[...] = jnp.where(row == 0, sums[0], jnp.where(row == 1, sums[1], jnp.where(row == 2, sums[2], 0.0)))

    col = lambda c0: BS((S, 128), lambda i: (0, c0 // 128 + i))
    out = BS((S, 128), lambda i: (0, i))
    return _pc(body, name=name, grid=(2,),
               in_specs=[col(C_GB), col(C_GC), col(C_UC), BS((None, 8, 128), lambda i: (l, 0, i)), col(256)],
               out_specs=[out, out, out, BS((8, 128), lambda i: (0, i))],
               out_shape=[SDS((S, 256), BF16)] * 3 + [SDS((8, 256), F32)])(proj, proj, proj, w8, dycat)


def _pool_parts(u, t, first):
    lo = lax.broadcasted_iota(jnp.int32, u.shape, 1) < 64
    s2 = u + _down(u, 1, t)
    s4 = s2 + _down(s2, 2, t)
    s8 = s4 + _down(s4, 4, t)
    s16 = s8 + _down(s8, 8, t)
    win = jnp.where(lo, jnp.where(first, s2, s8), jnp.where(first, s4, s16))
    wv = jnp.where(lo, jnp.where(first, 2, 8), jnp.where(first, 4, 16))
    cnt = jnp.minimum(t + 1, wv).astype(F32)
    return win, cnt, lo


def _pool(proj, pwd, scale3, l, name):
    S = proj.shape[0]

    def body(u_ref, pw_ref, sc_ref, y_ref):
        t = lax.broadcasted_iota(jnp.int32, (S, 128), 0)
        u = u_ref[...]
        win, cnt, _ = _pool_parts(u, t, pl.program_id(0) == 0)
        pooled = win / cnt - u
        y_ref[...] = _dot(pooled.astype(BF16), pw_ref[...]) * sc_ref[...]

    return _pc(body, name=name, grid=(2,),
               in_specs=[BS((S, 128), lambda i: (0, C_UP // 128 + i)), BS((None, 128, 128), lambda i: (l, i, 0)),
                         BS((None, 1, 128), lambda i: (l, 0, i))],
               out_specs=BS((S, 128), lambda i: (0, i)), out_shape=SDS((S, 256), F32))(proj, pwd, scale3)


def _pool_bwd(proj, pwd, scale3, dycat, l, name):
    S = proj.shape[0]

    def body(u_ref, pw_ref, sc_ref, dy_ref, du_ref, dpw_ref, dsc_ref):
        t = lax.broadcasted_iota(jnp.int32, (S, 128), 0)
        first = pl.program_id(0) == 0
        u = u_ref[...]
        win, cnt, lo = _pool_parts(u, t, first)
        pooled = (win / cnt - u).astype(BF16)
        pw = pw_ref[...]
        dy = dy_ref[...]
        dsc_ref[...] = jnp.broadcast_to(jnp.sum(dy * _dot(pooled, pw), axis=0, keepdims=True), (8, 128))
        dmb = (dy * sc_ref[...]).astype(BF16)
        dpw_ref[...] = _dot_tn(pooled, dmb)
        dpooled = _dot_nt(dmb, pw)
        a1 = dpooled / cnt
        a2 = a1 + _up(a1, 1, t)
        a4 = a2 + _up(a2, 2, t)
        a8 = a4 + _up(a4, 4, t)
        a16 = a8 + _up(a8, 8, t)
        dwin = jnp.where(lo, jnp.where(first, a2, a8), jnp.where(first, a4, a16))
        du_ref[...] = (dwin - dpooled).astype(BF16)

    return _pc(body, name=name, grid=(2,),
               in_specs=[BS((S, 128), lambda i: (0, C_UP // 128 + i)), BS((None, 128, 128), lambda i: (l, i, 0)),
                         BS((None, 1, 128), lambda i: (l, 0, i)), BS((S, 128), lambda i: (0, 4 + i))],
               out_specs=[BS((S, 128), lambda i: (0, i)), BS((128, 128), lambda i: (i, 0)), BS((8, 128), lambda i: (0, i))],
               out_shape=[SDS((S, 256), BF16), SDS((256, 128), F32), SDS((8, 256), F32)])(proj, pwd, scale3, dycat)


def _adamw(w, g, m, v, name):
    n, a, b = w.shape
    tr = _row_tile(a, b)

    def body(w_ref, g_ref, m_ref, v_ref, d_ref, nm_ref, nv_ref):
        gv = g_ref[...]
        m_new = B1 * m_ref[...] + (1.0 - B1) * gv
        v_new = B2 * v_ref[...] + (1.0 - B2) * (gv * gv)
        m_hat = m_new / (1.0 - B1 ** STEP)
        v_hat = v_new / (1.0 - B2 ** STEP)
        d_ref[...] = -LR * (m_hat / (jnp.sqrt(v_hat) + ADAM_EPS) + WD * w_ref[...])
        nm_ref[...] = m_new
        nv_ref[...] = v_new

    sp = BS((None, tr, b), lambda i, t: (i, t, 0))
    return _pc(body, name=name, grid=(n, a // tr), in_specs=[sp] * 4, out_specs=[sp] * 3,
               out_shape=[SDS((n, a, b), F32)] * 3)(w, g, m, v)


def _prefetch_call(body, name, grid, in_specs, out_specs, out_shape):
    gs = pltpu.PrefetchScalarGridSpec(num_scalar_prefetch=1, grid=grid, in_specs=in_specs, out_specs=out_specs)
    return pl.pallas_call(body, name=name, grid_spec=gs, out_shape=out_shape, compiler_params=_params(len(grid)))


def _place(w, kc, dtype, name):
    _, a, b = w.shape

    def body(kc_ref, w_ref, o_ref):
        o_ref[...] = w_ref[...].astype(dtype)

    return _prefetch_call(body, name, (2,), [BS((None, a, b), lambda l, kc: (l, 0, 0))],
                          BS((None, None, a, b), lambda l, kc: (l, kc[0], 0, 0)), SDS((2, 4, a, b), dtype))(kc, w)


def _pair_sum(g, got, kc, name):
    _, _, a, b = g.shape
    tr = _row_tile(a, b)

    def body(kc_ref, a_ref, b_ref, t32_ref, t16_ref):
        s = a_ref[...] + b_ref[...]
        t32_ref[...] = s
        t16_ref[...] = s.astype(BF16)

    sp = BS((None, tr, b), lambda k, t, kc: (k, t, 0))
    return _prefetch_call(body, name, (4, a // tr),
                          [BS((None, None, tr, b), lambda k, t, kc: (kc[1], k, t, 0)), sp], [sp, sp],
                          [SDS((4, a, b), F32), SDS((4, a, b), BF16)])(kc, g, got)


def _chip_sum(t32, got3, kc, name):
    _, a, b = t32.shape
    tr = _row_tile(a, b)

    def body(kc_ref, a_ref, b_ref, u_ref):
        u_ref[...] = ((a_ref[...] + b_ref[0].astype(F32)) + b_ref[1].astype(F32)) + b_ref[2].astype(F32)

    return _prefetch_call(body, name, (a // tr,),
                          [BS((None, tr, b), lambda t, kc: (kc[0], t, 0)), BS((3, tr, b), lambda t, kc: (0, t, 0))],
                          BS((None, tr, b), lambda t, kc: (kc[1], t, 0)), SDS((2, a, b), F32))(kc, t32, got3)


def _me():
    return lax.axis_index("x"), lax.axis_index("y"), lax.axis_index("c")


def _other_chips(x, y):
    return [(1 - x, y), (x, 1 - y), (1 - x, 1 - y)]


ANY = BS(memory_space=pl.ANY)
COMM_PARAMS = pltpu.CompilerParams(has_side_effects=True)


def _gather(arrs, name):
    n = len(arrs)

    def body(*refs):
        for phase in _gather_phases(refs[n:2 * n], [a.shape for a in arrs], refs[2 * n], refs[2 * n + 1]):
            phase()

    return pl.pallas_call(body, name=name, out_shape=[SDS(a.shape, a.dtype) for a in arrs],
                          in_specs=[ANY] * n, out_specs=[ANY] * n, input_output_aliases={t: t for t in range(n)},
                          scratch_shapes=[pltpu.SemaphoreType.DMA((7 * n,)), pltpu.SemaphoreType.DMA((7 * n,))],
                          compiler_params=COMM_PARAMS)(*arrs)


def _gather_phases(outs, shapes, send_sems, recv_sems, layer=None):
    n = len(outs)
    split = [s[2] % 32 == 0 for s in shapes]

    def plan():
        x, y, c = _me()
        return (c if layer is None else layer), (x, y), (x, y, c), (x, y, 1 - c), _other_chips(x, y)

    def role(moving, fn):
        if layer is None:
            fn()
        else:
            c = lax.axis_index("c")
            pl.when((c == layer) if moving else (c != layer))(fn)

    def blk(t, chip, layer, half=None):
        r = outs[t].at[layer, 2 * chip[0] + chip[1]]
        if half is None:
            return r
        rows = shapes[t][2] // 2
        return r.at[pl.ds(half * rows, rows)]

    def copy(t, k, ref, to):
        return pltpu.make_async_remote_copy(src_ref=ref, dst_ref=ref, send_sem=send_sems.at[7 * t + k],
                                            recv_sem=recv_sems.at[7 * t + k], device_id=to, device_id_type=MESH)

    def own_sends(t):
        c, chip, me, sib, (xn, yn, dg) = plan()
        cps = [copy(t, 0, blk(t, chip, c), (*xn, c)), copy(t, 1, blk(t, chip, c), (*yn, c))]
        return cps if split[t] else cps + [copy(t, 2, blk(t, chip, c), (*dg, c))]

    def relays(t):
        c, chip, me, sib, (xn, yn, dg) = plan()
        after_x = [copy(t, 4, blk(t, xn, c), sib)]
        after_y = [copy(t, 5, blk(t, yn, c), sib)]
        if split[t]:
            after_x.insert(0, copy(t, 2, blk(t, xn, c, 0), (*yn, c)))
            after_y.insert(0, copy(t, 3, blk(t, yn, c, 1), (*xn, c)))
        return after_x, after_y, [copy(t, 6, blk(t, dg, c), sib)]

    def send_own():
        for t in range(n):
            for cp in own_sends(t):
                cp.start()

    def relay_neighbours():
        c, chip, me, sib, (xn, yn, dg) = plan()
        for t in range(n):
            after_x, after_y, _ = relays(t)
            copy(t, 0, blk(t, xn, c), me).wait_recv()
            for cp in after_x:
                cp.start()
            copy(t, 1, blk(t, yn, c), me).wait_recv()
            for cp in after_y:
                cp.start()

    def relay_diagonal():
        c, chip, me, sib, (xn, yn, dg) = plan()
        for t in range(n):
            if split[t]:
                copy(t, 2, blk(t, dg, c, 0), me).wait_recv()
                copy(t, 3, blk(t, dg, c, 1), me).wait_recv()
            else:
                copy(t, 2, blk(t, dg, c), me).wait_recv()
            relays(t)[2][0].start()

    def take_sibling():
        _, chip, me, sib, (xn, yn, dg) = plan()
        theirs = 1 - lax.axis_index("c") if layer is None else layer
        for t in range(n):
            for k, peer in ((4, xn), (5, yn), (6, dg)):
                copy(t, k, blk(t, peer, theirs), me).wait_recv()

    def drain_sends():
        for t in range(n):
            after_x, after_y, after_d = relays(t)
            for cp in own_sends(t) + after_x + after_y + after_d:
                cp.wait_send()

    def phase3():
        role(True, relay_diagonal)
        role(False, take_sibling)
        role(True, drain_sends)

    return (lambda: role(True, send_own)), (lambda: role(True, relay_neighbours)), phase3


def _swap_copies(ins, outs, send_sems, recv_sems):
    x, y, c = _me()
    return [pltpu.make_async_remote_copy(src_ref=ins[t].at[1 - c], dst_ref=outs[t], send_sem=send_sems.at[t],
                                         recv_sem=recv_sems.at[t], device_id=(x, y, 1 - c), device_id_type=MESH)
            for t in range(len(ins))]


def _exchange_copies(ins, outs, send_sems, recv_sems):
    x, y, c = _me()
    return [pltpu.make_async_remote_copy(src_ref=ins[t].at[2 * cx + cy], dst_ref=outs[t].at[j],
                                         send_sem=send_sems.at[3 * t + j], recv_sem=recv_sems.at[3 * t + j],
                                         device_id=(cx, cy, c), device_id_type=MESH)
            for j, (cx, cy) in enumerate(_other_chips(x, y)) for t in range(len(ins))]


class _Rider:
    def __init__(self, arrs, out_shape, nsem, copies):
        self.arrs, self.out_shape, self.nsem, self.copies = list(arrs), out_shape, nsem, copies
        self.n = len(self.arrs)

    def scratch(self):
        return [pltpu.SemaphoreType.DMA((self.nsem,)), pltpu.SemaphoreType.DMA((self.nsem,))]

    def start(self, ins, outs, sems):
        for cp in self.copies(ins, outs, *sems):
            cp.start()

    def wait(self, ins, outs, sems):
        for cp in self.copies(ins, outs, *sems):
            cp.wait()


def _swap_rider(gs):
    return _Rider(gs, [SDS(g.shape[1:], g.dtype) for g in gs], len(gs), _swap_copies)


def _exchange_rider(ts):
    return _Rider(ts, [SDS((3,) + t.shape[1:], t.dtype) for t in ts], 3 * len(ts), _exchange_copies)


def _ride_alone(rider, name):
    n = rider.n

    def body(*refs):
        rider.start(refs[:n], refs[n:2 * n], refs[2 * n:])
        rider.wait(refs[:n], refs[n:2 * n], refs[2 * n:])

    return pl.pallas_call(body, name=name, out_shape=rider.out_shape, in_specs=[ANY] * n, out_specs=[ANY] * n,
                          scratch_shapes=rider.scratch(), compiler_params=COMM_PARAMS)(*rider.arrs)


def _join_layers(us, name):
    n = len(us)

    def body(*refs):
        outs, send_sems, recv_sems = refs[n:2 * n], refs[2 * n], refs[2 * n + 1]
        x, y, c = _me()
        cps = [pltpu.make_async_remote_copy(src_ref=outs[t].at[c], dst_ref=outs[t].at[c], send_sem=send_sems.at[t],
                                            recv_sem=recv_sems.at[t], device_id=(x, y, 1 - c), device_id_type=MESH)
               for t in range(n)]
        for cp in cps:
            cp.start()
        for cp in cps:
            cp.wait()

    return pl.pallas_call(body, name=name, out_shape=[SDS(u.shape, u.dtype) for u in us],
                          in_specs=[ANY] * n, out_specs=[ANY] * n, input_output_aliases={t: t for t in range(n)},
                          scratch_shapes=[pltpu.SemaphoreType.DMA((n,)), pltpu.SemaphoreType.DMA((n,))],
                          compiler_params=COMM_PARAMS)(*us)


def _allsum_small(v, name):
    M = v.shape[0]

    def body(x_ref, o_ref, all_ref, send_sems, recv_sems, local_sem):
        x, y, c = _me()
        me, sib = (x, y, c), (x, y, 1 - c)
        chips = _other_chips(x, y)

        def rows(px, py, pc):
            return all_ref.at[pl.ds((4 * px + 2 * py + pc) * M, M), :]

        def copy(k, block, to, src=None):
            return pltpu.make_async_remote_copy(src_ref=rows(*block) if src is None else src, dst_ref=rows(*block),
                                                send_sem=send_sems.at[k], recv_sem=recv_sems.at[k],
                                                device_id=to, device_id_type=MESH)

        mine = pltpu.make_async_copy(x_ref, rows(*me), local_sem)
        mine.start()
        first = [copy(0, me, sib, src=x_ref)]
        first += [copy(1 + j, me, (*chip, c), src=x_ref) for j, chip in enumerate(chips)]
        for cp in first:
            cp.start()
        passed = [copy(4 + j, (*chip, c), sib) for j, chip in enumerate(chips)]
        for j, chip in enumerate(chips):
            copy(1 + j, (*chip, c), me).wait_recv()
            passed[j].start()
        copy(0, sib, me).wait_recv()
        for j, chip in enumerate(chips):
            copy(4 + j, (*chip, 1 - c), me).wait_recv()
        for cp in first + passed:
            cp.wait_send()
        mine.wait()
        acc = all_ref[0:M, :]
        for d in range(1, 8):
            acc = acc + all_ref[d * M:(d + 1) * M, :]
        o_ref[...] = acc

    vm = BS(memory_space=pltpu.VMEM)
    return pl.pallas_call(body, name=name, out_shape=SDS((M, LANES), F32), in_specs=[vm], out_specs=vm,
                          scratch_shapes=[pltpu.VMEM((8 * M, LANES), F32), pltpu.SemaphoreType.DMA((7,)),
                                          pltpu.SemaphoreType.DMA((7,)), pltpu.SemaphoreType.DMA],
                          compiler_params=pltpu.CompilerParams(has_side_effects=True, vmem_limit_bytes=VMEM_LIMIT))(v)


FFN = ("w_gate_up", "w_down")
REST = ("w_in", "w_o", "w_uq", "w_ukv")
BIG = FFN + REST
TINY = ("conv_w",)
REPL = ("attn_norm", "mla_q_norm", "mla_kv_norm", "pool_w", "pool_scale", "swa_sinks", "mix_norm", "ffn_norm",
        "final_norm")
ORDER = ("attn_norm", "w_in", "mla_q_norm", "w_uq", "mla_kv_norm", "w_ukv", "conv_w", "pool_w", "pool_scale",
         "swa_sinks", "mix_norm", "w_o", "ffn_norm", "w_gate_up", "w_down", "final_norm")


def _rows8(shape):
    return -(-int(np.prod(shape)) // (8 * LANES)) * 8


def _pack(arrs):
    parts = []
    for a in arrs:
        r = _rows8(a.shape)
        parts.append(jnp.pad(a.reshape(-1), (0, r * LANES - a.size)).reshape(r, LANES))
    return jnp.concatenate(parts, axis=0)


def _unpack(buf, shapes):
    out, r0 = [], 0
    for s in shapes:
        n, r = int(np.prod(s)), _rows8(s)
        rows = buf[r0:r0 + r]
        out.append(rows.reshape(s) if n == r * LANES else rows.reshape(-1)[:n].reshape(s))
        r0 += r
    return out


def _cols_joined(g):
    return jnp.transpose(g, (0, 2, 1, 3)).reshape(g.shape[0], g.shape[2], 4 * g.shape[3])


def _cols_split(w):
    n, a, b4 = w.shape
    return jnp.transpose(w.reshape(n, a, 4, b4 // 4), (0, 2, 1, 3))


def _rope_tables(S):
    inv = 1.0 / (10000.0 ** (jnp.arange(0, 32, 2, dtype=F32) / 32))
    ang = jnp.arange(S, dtype=F32)[:, None] * inv[None, :]
    cos, sin = jnp.cos(ang), jnp.sin(ang)
    z = lambda w: jnp.zeros((S, w), F32)
    tc = jnp.concatenate([jnp.ones((S, 64), F32), cos, cos, jnp.ones((S, 32), F32)], axis=1)
    ts1 = jnp.concatenate([z(64), -sin, z(48)], axis=1)
    ts2 = jnp.concatenate([z(80), sin, z(32)], axis=1)
    return tc, ts1, ts2


def _pad_w_in(w):
    z = lambda n: jnp.zeros(w.shape[:-1] + (n,), w.dtype)
    return jnp.concatenate([w[..., 0:384], z(64), w[..., 384:416], z(32), w[..., 416:1952]], axis=-1)


def _unpad_w_in(d):
    return jnp.concatenate([d[..., 0:384], d[..., 448:480], d[..., 512:2048]], axis=-1)


def _pad_heads(w, src, offs):
    cols = []
    for h in range(HEADS):
        src0, n = src[h]
        z = lambda k: jnp.zeros(w.shape[:-1] + (k,), w.dtype)
        cols += [z(offs[h]), w[..., src0:src0 + n], z(128 - offs[h] - n)]
    return jnp.concatenate(cols, axis=-1)


UQ_SRC = [(h * 96, 96) for h in range(HEADS)]
KN_SRC = [(h * 128, 64) for h in range(HEADS)]
V_SRC = [(h * 128 + 64, 64) for h in range(HEADS)]
ZERO_OFF = [0] * HEADS
V_OFF = [(h % 2) * 64 for h in range(HEADS)]


def _unpad_heads(d, src, offs):
    return [d[..., h * 128 + offs[h]: h * 128 + offs[h] + src[h][1]] for h in range(HEADS)]


def kernel(x, attn_norm, w_in, mla_q_norm, w_uq, mla_kv_norm, w_ukv, conv_w, pool_w, pool_scale, swa_sinks, mix_norm, w_o, ffn_norm, w_gate_up, w_down, final_norm, loss_target, m_attn_norm, m_w_in, m_mla_q_norm, m_w_uq, m_mla_kv_norm, m_w_ukv, m_conv_w, m_pool_w, m_pool_scale, m_swa_sinks, m_mix_norm, m_w_o, m_ffn_norm, m_w_gate_up, m_w_down, m_final_norm, v_attn_norm, v_w_in, v_mla_q_norm, v_w_uq, v_mla_kv_norm, v_w_ukv, v_conv_w, v_pool_w, v_pool_scale, v_swa_sinks, v_mix_norm, v_w_o, v_ffn_norm, v_w_gate_up, v_w_down, v_final_norm):
    W = dict(attn_norm=attn_norm, w_in=w_in, mla_q_norm=mla_q_norm, w_uq=w_uq, mla_kv_norm=mla_kv_norm, w_ukv=w_ukv,
             conv_w=conv_w, pool_w=pool_w, pool_scale=pool_scale, swa_sinks=swa_sinks, mix_norm=mix_norm, w_o=w_o,
             ffn_norm=ffn_norm, w_gate_up=w_gate_up, w_down=w_down, final_norm=final_norm)
    M1 = dict(attn_norm=m_attn_norm, w_in=m_w_in, mla_q_norm=m_mla_q_norm, w_uq=m_w_uq, mla_kv_norm=m_mla_kv_norm,
              w_ukv=m_w_ukv, conv_w=m_conv_w, pool_w=m_pool_w, pool_scale=m_pool_scale, swa_sinks=m_swa_sinks,
              mix_norm=m_mix_norm, w_o=m_w_o, ffn_norm=m_ffn_norm, w_gate_up=m_w_gate_up, w_down=m_w_down,
              final_norm=m_final_norm)
    V2 = dict(attn_norm=v_attn_norm, w_in=v_w_in, mla_q_norm=v_mla_q_norm, w_uq=v_w_uq, mla_kv_norm=v_mla_kv_norm,
              w_ukv=v_w_ukv, conv_w=v_conv_w, pool_w=v_pool_w, pool_scale=v_pool_scale, swa_sinks=v_swa_sinks,
              mix_norm=v_mix_norm, w_o=v_w_o, ffn_norm=v_ffn_norm, w_gate_up=v_w_gate_up, w_down=v_w_down,
              final_norm=v_final_norm)
    S = x.shape[1]
    xc, yc, cc = _me()
    chip = 2 * xc + yc
    kc = jnp.stack([chip, cc]).astype(jnp.int32)

    first, later = ("w_in", "w_uq", "w_ukv", "conv_w"), ("w_o", "w_gate_up", "w_down")
    placed = {n: _place(W[n], kc, F32 if n == "conv_w" else BF16, f"place_{n}") for n in first + later}
    gi, gq, gkv, gcv = _gather([placed[n] for n in first], "gather_weights")
    later_w = [placed[n] for n in later]
    win_p = _pad_w_in(_cols_joined(gi))
    wuq_p = _pad_heads(_cols_joined(gq), UQ_SRC, ZERO_OFF)
    wukv = _cols_joined(gkv)
    wk_p = _pad_heads(wukv, KN_SRC, ZERO_OFF)
    wv_p = _pad_heads(wukv, V_SRC, V_OFF)
    conv8 = jnp.pad(_cols_joined(gcv), ((0, 0), (0, 5), (0, 0)))
    pwd = jnp.concatenate([jnp.concatenate(
        [jnp.pad(pool_w[:, 2 * b], ((0, 0), (0, 0), (0, 64))), jnp.pad(pool_w[:, 2 * b + 1], ((0, 0), (0, 0), (64, 0)))],
        axis=1) for b in range(2)], axis=1).astype(BF16)
    tabs = _rope_tables(S)
    g_attn, g_q, g_kv, g_mix, g_ffn, g_ps = (_g3(W[n]) for n in ("attn_norm", "mla_q_norm", "mla_kv_norm", "mix_norm",
                                                                  "ffn_norm", "pool_scale"))

    xs = [x[0]]
    saved = []
    for l in range(DEPTH):
        x0 = xs[-1]
        proj, h = _norm_mm(x0, g_attn, l, win_p, _wspec_in(l), D_INP, D_INP, F32, f"in_proj{l}")
        q, k, v, kt, vt = _mla_prep(proj, g_q, g_kv, wuq_p, wk_p, wv_p, tabs, l, f"mla_prep{l}")
        ya, lse, later_w = _mla_attn(q, k, vt, later_w, l, f"mla_attn{l}")
        go, gu4, gd = later_w
        wo, wdown = go.reshape(2, D, D), gd.reshape(2, D_FF, D)
        yb = _conv(proj, conv8, l, f"conv{l}")
        ycp = _pool(proj, pwd, g_ps, l, f"pool{l}")
        yd = _swa(proj, swa_sinks, l, f"swa{l}")
        x1, ycat, mixed = _mix_out(x0, ya, yb, ycp, yd, g_mix, wo, l, f"mix_out{l}")
        gu, h2 = _norm_mm(x1, g_ffn, l, gu4, _wspec_gu(l), 2 * D_FF, 2 * D_FF // 4, BF16, f"gate_up{l}")
        x2, act = _swiglu_mm_res(x1, gu, wdown, l, f"down{l}")
        saved.append(dict(x0=x0, proj=proj, h=h, q=q, k=k, kt=kt, v=v, lse=lse, x1=x1, ycat=ycat, mixed=mixed,
                          gu=gu, h2=h2, act=act))
        xs.append(x2)

    dx, dx16, dg_final, loss_tile = _loss_head(xs[-1], final_norm.reshape(1, D), loss_target[0], "loss_head")
    loss = lax.psum(loss_tile[0, 0] * (0.5 / D), ("x", "y", "c"))

    G = {n: [None] * DEPTH for n in ("w_uq", "w_ukv") + TINY + REPL if n != "final_norm"}
    gw_in = gw_o = gw_gu = gw_down = None
    for l in reversed(range(DEPTH)):
        sv = saved[l]
        dgu = _bwd_down(dx16, wdown, sv["gu"], l, f"down_bwd{l}")
        gw_down = _mm_tn(sv["act"], dx16, l, gw_down, f"dw_down{l}")
        gw_gu = _mm_tn(sv["h2"], dgu, l, gw_gu, f"dw_gate_up{l}", split4=True)
        ffn_exchange = None
        if l == 0:
            g_f = [gw_gu, gw_down.reshape(2, 4, D_FF // 4, D)]
            dx1, dx1_16, dg, got_f = _mm_nt_normbwd(dgu, gu4, l, sv["x1"], g_ffn, dx, 1, f"gate_up_bwd{l}",
                                                    rider=_swap_rider(g_f))
            pairs_f = [_pair_sum(g, o, kc, f"rs_pair_sum_{n}") for g, o, n in zip(g_f, got_f, FFN)]
            ffn_exchange = _exchange_rider([p[1] for p in pairs_f])
        else:
            dx1, dx1_16, dg = _mm_nt_normbwd(dgu, gu4, l, sv["x1"], g_ffn, dx, 1, f"gate_up_bwd{l}")
        G["ffn_norm"][l] = dg[0]
        gw_o = _mm_tn(sv["mixed"], dx1_16, l, gw_o, f"dw_o{l}")
        dycat, dg = _mm_nt_normbwd(dx1_16, wo.reshape(2, 1, D, D), l, sv["ycat"], g_mix, None, 4, f"mix_bwd{l}")
        G["mix_norm"][l] = dg[0]

        proj = sv["proj"]
        delta = _mla_delta(dycat, sv["ycat"], f"mla_delta{l}")
        dq, dk, dv, got3_l = _mla_attn_bwd(sv["q"], sv["k"], sv["kt"], sv["v"], dycat, sv["lse"], delta, ffn_exchange,
                                           f"mla_attn_bwd{l}")
        if l == 0:
            got3_f = got3_l
        dcq, dckv, dkr, dwuq, dwk, dwv, dgq, dgkv = _mla_prep_bwd(
            dq, dk, dv, proj, g_q, g_kv, wuq_p, wk_p, wv_p, tabs, l, f"mla_prep_bwd{l}")
        dgb, dgc, duc, dcw = _conv_bwd(proj, conv8, dycat, l, f"conv_bwd{l}")
        dup, dpw, dps = _pool_bwd(proj, pwd, g_ps, dycat, l, f"pool_bwd{l}")
        dqs, dks, dvs, dsink = _swa_bwd(proj, swa_sinks, dycat, l, f"swa_bwd{l}")
        dproj = jnp.concatenate([dcq, dckv, dkr, dgb, dgc, duc, dup, dqs, dks, dvs], axis=1)
        gw_in = _mm_tn(sv["h"], dproj, l, gw_in, f"dw_in{l}")
        dx, dx16, dg = _mm_nt_normbwd(dproj, win_p.reshape(2, 1, D, D_INP), l, sv["x0"], g_attn, dx1, 1, f"in_proj_bwd{l}")
        G["attn_norm"][l] = dg[0]
        G["mla_q_norm"][l] = dgq[0]
        G["mla_kv_norm"][l] = dgkv[0]
        G["w_uq"][l] = jnp.concatenate(_unpad_heads(dwuq, UQ_SRC, ZERO_OFF), axis=1)
        kn, vv = _unpad_heads(dwk, KN_SRC, ZERO_OFF), _unpad_heads(dwv, V_SRC, V_OFF)
        G["w_ukv"][l] = jnp.concatenate([t for h in range(HEADS) for t in (kn[h], vv[h])], axis=1)
        G["conv_w"][l] = dcw[0:3]
        G["pool_w"][l] = jnp.stack([dpw[0:64, 0:64], dpw[64:128, 64:128], dpw[128:192, 0:64], dpw[192:256, 64:128]])
        G["pool_scale"][l] = dps[0]
        G["swa_sinks"][l] = dsink[0, 0:4]
    grad_x = dx[None]
    Gl = {n: jnp.stack(G[n]) for n in G}
    Gl["final_norm"] = dg_final[0]

    us_f = [_chip_sum(p[0], o3, kc, f"rs_chip_sum_{n}") for p, o3, n in zip(pairs_f, got3_f, FFN)]
    g_r = [_cols_split(_unpad_w_in(gw_in)), gw_o.reshape(2, 4, D // 4, D), _cols_split(Gl["w_uq"]),
           _cols_split(Gl["w_ukv"])]
    gsum_f = _join_layers(us_f, "rs_join_cores_ffn")
    got_r = _ride_alone(_swap_rider(g_r), "rs_swap_cores")
    pairs_r = [_pair_sum(g, o, kc, f"rs_pair_sum_{n}") for g, o, n in zip(g_r, got_r, REST)]
    got3_r = _ride_alone(_exchange_rider([p[1] for p in pairs_r]), "rs_exchange_chips")
    us_r = [_chip_sum(p[0], o3, kc, f"rs_chip_sum_{n}") for p, o3, n in zip(pairs_r, got3_r, REST)]
    gsum_r = _join_layers(us_r, "rs_join_cores")
    res = {}
    for n, g in zip(BIG, gsum_f + gsum_r):
        d_, m_, v_ = _adamw(W[n], g, M1[n], V2[n], f"adamw_{n}")
        res["g", n], res["d", n], res["m", n], res["v", n] = g, d_, m_, v_

    small = TINY + REPL
    full_shapes = [Gl[n].shape for n in small]
    summed = _unpack(_allsum_small(_pack([Gl[n] for n in small]), "allsum_small"), full_shapes)
    gs = {}
    for n, g in zip(small, summed):
        if n in TINY:
            wdt = W[n].shape[2]
            g = lax.dynamic_slice_in_dim(g, chip * wdt, wdt, axis=2)
        gs[n] = g
    own_shapes = [W[n].shape for n in small]
    pk = lambda src: _pack([src[n] for n in small])[None]
    d_s, m_s, v_s = _adamw(pk(W), pk(gs), pk(M1), pk(V2), "adamw_small")
    for key, buf in (("d", d_s), ("m", m_s), ("v", v_s)):
        for n, a in zip(small, _unpack(buf[0], own_shapes)):
            res[key, n] = a
    for n in small:
        res["g", n] = gs[n]

    return (loss, grad_x, *[res["g", n] for n in ORDER], *[res["d", n] for n in ORDER],
            *[res["m", n] for n in ORDER], *[res["v", n] for n in ORDER])
```

```python
import math

import numpy as np
import jax
import jax.numpy as jnp
from jax import lax
from jax.experimental import pallas as pl
from jax.experimental.pallas import tpu as pltpu

F32, BF16 = jnp.float32, jnp.bfloat16
SDS = jax.ShapeDtypeStruct
BS = pl.BlockSpec
MESH = pl.DeviceIdType.MESH

D = 1024
DEPTH = 2
HEADS = 4
D_FF = 2816
D_INP = 2048
EPS = 1e-6
SWA_WINDOW = 128
BLK = 128
SLOPES = tuple(2.0 ** (-8.0 * (i + 1) / 4) for i in range(4))
QK_SCALE = 1.0 / math.sqrt(96)
SWA_SCALE = 1.0 / math.sqrt(64)
LR, B1, B2, ADAM_EPS, WD, STEP = 0.001, 0.9, 0.999, 1e-08, 0.01, 10

LANES = 1024
VMEM_LIMIT = 56 * 1024 * 1024
NEG_INF = float("-inf")

C_CQ, C_CKV, C_KR, C_GB, C_GC, C_UC, C_UP, C_QS, C_KS, C_VS = 0, 256, 384, 512, 768, 1024, 1280, 1536, 1792, 1920


def _params(ngrid):
    return pltpu.CompilerParams(dimension_semantics=("arbitrary",) * ngrid, vmem_limit_bytes=VMEM_LIMIT)


def _pc(body, *, name, grid, in_specs, out_specs, out_shape, scratch=(), aliases=None):
    return pl.pallas_call(
        body, name=name, grid=grid, in_specs=in_specs, out_specs=out_specs, out_shape=out_shape,
        scratch_shapes=scratch, input_output_aliases=aliases or {}, compiler_params=_params(len(grid)))


def _dot(a, b):
    return jnp.dot(a, b, preferred_element_type=F32)


def _dot_nt(a, b):
    return lax.dot_general(a, b, (((1,), (1,)), ((), ())), preferred_element_type=F32)


def _dot_tn(a, b):
    return lax.dot_general(a, b, (((0,), (0,)), ((), ())), preferred_element_type=F32)


def _tile(n, cap):
    if n <= cap:
        return n
    t = cap - cap % 128
    while n % t:
        t -= 128
    return t


def _row_tile(a, b, cap=262144):
    bp = -(-b // 128) * 128
    best = None
    for t in range(8, a + 1, 8):
        if a % t == 0 and t * bp <= cap:
            best = t
    return best if best is not None else a


def _g3(a):
    return a.reshape(a.shape[0], 1, a.shape[1])


def _norm_mm(x, g3, l, w, wspec, N, tn, out_dtype, name):
    S, K = x.shape
    tm = min(1024 if out_dtype == BF16 else 512, S)

    def body(x_ref, g_ref, w_ref, y_ref, h_ref):
        @pl.when(pl.program_id(1) == 0)
        def _():
            xv = x_ref[...]
            r = lax.rsqrt(jnp.mean(xv * xv, axis=-1, keepdims=True) + EPS)
            h_ref[...] = (xv * r * g_ref[...]).astype(BF16)

        y_ref[...] = _dot(h_ref[...], w_ref[...]).astype(out_dtype)

    return _pc(body, name=name, grid=(S // tm, N // tn),
               in_specs=[BS((tm, K), lambda i, j: (i, 0)), BS((None, 1, K), lambda i, j: (l, 0, 0)), wspec],
               out_specs=[BS((tm, tn), lambda i, j: (i, j)), BS((tm, K), lambda i, j: (i, 0))],
               out_shape=[SDS((S, N), out_dtype), SDS((S, K), BF16)])(x, g3, w)


def _wspec_in(l):
    return BS((None, D, D_INP), lambda i, j: (l, 0, j))


def _wspec_gu(l):
    return BS((None, None, D, 2 * D_FF // 4), lambda i, j: (l, j, 0, 0))


def _mix_out(x0, ya, yb, yc, yd, gmix3, wo, l, name):
    S = x0.shape[0]
    tm = min(512, S)

    def body(x_ref, ya_ref, yb_ref, yc_ref, yd_ref, g_ref, w_ref, x1_ref, ycat_ref, mixed_ref):
        groups = [ya_ref[...], yb_ref[...], yc_ref[...], yd_ref[...]]
        for gi, yg in enumerate(groups):
            sl = slice(gi * 256, (gi + 1) * 256)
            r = lax.rsqrt(jnp.mean(yg * yg, axis=-1, keepdims=True) + EPS)
            ycat_ref[:, sl] = yg
            mixed_ref[:, sl] = (yg * r * g_ref[:, sl]).astype(BF16)
        x1_ref[...] = x_ref[...] + _dot(mixed_ref[...], w_ref[...])

    row = lambda w: BS((tm, w), lambda i: (i, 0))
    return _pc(body, name=name, grid=(S // tm,),
               in_specs=[row(D), row(256), row(256), row(256), row(256), BS((None, 1, D), lambda i: (l, 0, 0)),
                         BS((None, D, D), lambda i: (l, 0, 0))],
               out_specs=[row(D), row(D), row(D)],
               out_shape=[SDS((S, D), F32), SDS((S, D), F32), SDS((S, D), BF16)])(x0, ya, yb, yc, yd, gmix3, wo)


def _swiglu_mm_res(x1, gu, wdown, l, name):
    S = x1.shape[0]
    tm = min(256, S)

    def body(x_ref, gate_ref, up_ref, w_ref, x2_ref, act_ref):
        gt = gate_ref[...].astype(F32)
        act = (gt * pl.reciprocal(1.0 + jnp.exp(-gt), approx=True) * up_ref[...].astype(F32)).astype(BF16)
        act_ref[...] = act
        x2_ref[...] = x_ref[...] + _dot(act, w_ref[...])

    return _pc(body, name=name, grid=(S // tm,),
               in_specs=[BS((tm, D), lambda i: (i, 0)), BS((tm, D_FF), lambda i: (i, 0)),
                         BS((tm, D_FF), lambda i: (i, 1)), BS((None, D_FF, D), lambda i: (l, 0, 0))],
               out_specs=[BS((tm, D), lambda i: (i, 0)), BS((tm, D_FF), lambda i: (i, 0))],
               out_shape=[SDS((S, D), F32), SDS((S, D_FF), BF16)])(x1, gu, gu, wdown)


def _loss_head(x, g, tgt, name):
    S = x.shape[0]
    tm = min(512, S)

    def body(x_ref, g_ref, t_ref, dx_ref, dx16_ref, dg_ref, loss_ref):
        @pl.when(pl.program_id(0) == 0)
        def _():
            dg_ref[...] = jnp.zeros_like(dg_ref)
            loss_ref[...] = jnp.zeros_like(loss_ref)

        xv = x_ref[...]
        r = lax.rsqrt(jnp.mean(xv * xv, axis=-1, keepdims=True) + EPS)
        xh = xv * r
        gv = g_ref[...]
        diff = xh * gv - t_ref[...]
        loss_ref[...] += jnp.sum(diff * diff)
        dy = diff * (1.0 / D)
        dg_ref[...] += jnp.sum(dy * xh, axis=0, keepdims=True)
        dxh = dy * gv
        dx = r * (dxh - xh * jnp.mean(dxh * xh, axis=-1, keepdims=True))
        dx_ref[...] = dx
        dx16_ref[...] = dx.astype(BF16)

    row = BS((tm, D), lambda i: (i, 0))
    return _pc(body, name=name, grid=(S // tm,),
               in_specs=[row, BS((1, D), lambda i: (0, 0)), row],
               out_specs=[row, row, BS((8, D), lambda i: (0, 0)), BS((8, 128), lambda i: (0, 0))],
               out_shape=[SDS((S, D), F32), SDS((S, D), BF16), SDS((8, D), F32), SDS((8, 128), F32)])(x, g, tgt)


def _mm_tn(a, b, l, prev, name, split4=False):
    S, Ka = a.shape
    N = b.shape[1]
    if split4:
        ta, tn = _tile(Ka, 256), N // 4
        out_shape = SDS((2, 4, Ka, tn), F32)
        out_spec = BS((None, None, ta, tn), lambda j, i: (l, j, i, 0))
    else:
        ta, tn = _tile(Ka, 512), _tile(N, 1024)
        out_shape = SDS((2, Ka, N), F32)
        out_spec = BS((None, ta, tn), lambda j, i: (l, i, j))

    def body(a_ref, b_ref, *rest):
        rest[-1][...] = _dot_tn(a_ref[...], b_ref[...])

    in_specs = [BS((S, ta), lambda j, i: (0, i)), BS((S, tn), lambda j, i: (0, j))]
    args = [a, b]
    if prev is not None:
        in_specs.append(BS(memory_space=pl.ANY))
        args.append(prev)
    return _pc(body, name=name, grid=(N // tn, Ka // ta), in_specs=in_specs, out_specs=out_spec, out_shape=out_shape,
               aliases={2: 0} if prev is not None else None)(*args)


def _bwd_down(dx16, wdown, gu, l, name):
    S = dx16.shape[0]
    tm = min(256, S)

    def body(dx_ref, w_ref, gate_ref, up_ref, dgu_ref):
        dxv = dx_ref[...]
        for c0 in range(0, D_FF, 256):
            cs = slice(c0, c0 + 256)
            dact = _dot_nt(dxv, w_ref[cs, :])
            gt = gate_ref[:, cs].astype(F32)
            sg = pl.reciprocal(1.0 + jnp.exp(-gt), approx=True)
            dgu_ref[:, cs] = (dact * up_ref[:, cs].astype(F32) * (sg * (1.0 + gt * (1.0 - sg)))).astype(BF16)
            dgu_ref[:, D_FF + c0:D_FF + c0 + 256] = (dact * (gt * sg)).astype(BF16)

    return _pc(body, name=name, grid=(S // tm,),
               in_specs=[BS((tm, D), lambda i: (i, 0)), BS((None, D_FF, D), lambda i: (l, 0, 0)),
                         BS((tm, D_FF), lambda i: (i, 0)), BS((tm, D_FF), lambda i: (i, 1))],
               out_specs=BS((tm, 2 * D_FF), lambda i: (i, 0)),
               out_shape=SDS((S, 2 * D_FF), BF16))(dx16, wdown, gu, gu)


def _mm_nt_normbwd(dy, w4, l, x, g3, dres, ngroups, name, rider=None):
    S, K = dy.shape
    _, nk, _, kc = w4.shape
    tm = min(512, S)
    gw = D // ngroups
    has_res = dres is not None
    nr = rider.n if rider else 0
    n_in, n_out = 4 + has_res, 2 + has_res

    def body(*refs):
        dy_ref, w_ref, x_ref, g_ref = refs[:4]
        res_ref = refs[4] if has_res else None
        outs = refs[n_in + nr:n_in + nr + n_out]
        dx_ref, dg_ref = outs[0], outs[-1]
        dx16_ref = outs[1] if has_res else None
        r_io = (refs[n_in:n_in + nr], refs[n_in + nr + n_out:n_in + 2 * nr + n_out], refs[n_in + 2 * nr + n_out:])
        if rider:
            pl.when(pl.program_id(0) == 0)(lambda: rider.start(*r_io))

        @pl.when(pl.program_id(0) == 0)
        def _():
            dg_ref[...] = jnp.zeros_like(dg_ref)

        dh = _dot_nt(dy_ref[:, 0:kc], w_ref[0])
        for k in range(1, nk):
            dh = dh + _dot_nt(dy_ref[:, k * kc:(k + 1) * kc], w_ref[k])
        for gi in range(ngroups):
            sl = slice(gi * gw, (gi + 1) * gw)
            xg = x_ref[:, sl]
            r = lax.rsqrt(jnp.mean(xg * xg, axis=-1, keepdims=True) + EPS)
            xh = xg * r
            dhg = dh[:, sl]
            dg_ref[:, sl] += jnp.sum(dhg * xh, axis=0, keepdims=True)
            dxh = dhg * g_ref[:, sl]
            dxg = r * (dxh - xh * jnp.mean(dxh * xh, axis=-1, keepdims=True))
            if has_res:
                dxg = dxg + res_ref[:, sl]
                dx16_ref[:, sl] = dxg.astype(BF16)
            dx_ref[:, sl] = dxg
        if rider:
            pl.when(pl.program_id(0) == S // tm - 1)(lambda: rider.wait(*r_io))

    row = BS((tm, D), lambda i: (i, 0))
    in_specs = [BS((tm, K), lambda i: (i, 0)),
                BS((None, nk, D, kc), lambda i: (l, 0, 0, 0), pipeline_mode=pl.Buffered(1)), row,
                BS((None, 1, D), lambda i: (l, 0, 0))]
    args = [dy, w4, x, g3]
    out_specs, out_shape = [row], [SDS((S, D), F32)]
    if has_res:
        in_specs.append(row)
        args.append(dres)
        out_specs.append(row)
        out_shape.append(SDS((S, D), BF16))
    out_specs.append(BS((8, D), lambda i: (0, 0)))
    out_shape.append(SDS((8, D), F32))
    if not rider:
        return _pc(body, name=name, grid=(S // tm,), in_specs=in_specs, out_specs=out_specs, out_shape=out_shape)(*args)
    out = pl.pallas_call(body, name=name, grid=(S // tm,), in_specs=in_specs + [ANY] * nr,
                         out_specs=out_specs + [ANY] * nr, out_shape=out_shape + rider.out_shape,
                         scratch_shapes=rider.scratch(),
                         compiler_params=pltpu.CompilerParams(dimension_semantics=("arbitrary",),
                                                              vmem_limit_bytes=VMEM_LIMIT, has_side_effects=True))(
        *args, *rider.arrs)
    return (*out[:n_out], list(out[n_out:]))


def _rope(x, c, s1, s2):
    return x * c + pltpu.roll(x, 112, axis=1) * s1 + pltpu.roll(x, 16, axis=1) * s2


def _rope_t(dy, c, s1, s2):
    return dy * c + pltpu.roll(dy * s1, 16, axis=1) + pltpu.roll(dy * s2, 112, axis=1)


def _mla_prep(proj, gq3, gkv3, wuq, wk, wv, tabs, l, name):
    S = proj.shape[0]
    tm = min(512, S)
    tc, ts1, ts2 = tabs

    def body(cq_ref, ckv_ref, kr_ref, gq_ref, gkv_ref, wuq_ref, wk_ref, wv_ref, c_ref, s1_ref, s2_ref,
             q_ref, k_ref, v_ref, kt_ref, vt_ref):
        c, s1, s2 = c_ref[...], s1_ref[...], s2_ref[...]
        cq = cq_ref[...]
        rq = lax.rsqrt(jnp.mean(cq * cq, axis=-1, keepdims=True) + EPS)
        qa = _dot((cq * rq * gq_ref[...]).astype(BF16), wuq_ref[...])
        ckv = ckv_ref[...]
        rkv = lax.rsqrt(jnp.mean(ckv * ckv, axis=-1, keepdims=True) + EPS)
        ckvn = (ckv * rkv * gkv_ref[...]).astype(BF16)
        ka = _dot(ckvn, wk_ref[...])
        va = _dot(ckvn, wv_ref[...])
        v_ref[...] = va.astype(BF16)
        vt_ref[...] = va.T.astype(BF16)
        krr = _rope(kr_ref[...], c, s1, s2)
        for h in range(HEADS):
            sl = slice(h * 128, (h + 1) * 128)
            q_ref[:, sl] = (_rope(qa[:, sl], c, s1, s2) * QK_SCALE).astype(BF16)
            kh = ka[:, sl] + krr
            k_ref[:, sl] = kh.astype(BF16)
            kt_ref[sl, :] = kh.T.astype(BF16)

    lay = lambda a, b: BS((None, a, b), lambda i: (l, 0, 0))
    tab = BS((tm, 128), lambda i: (i, 0))
    return _pc(body, name=name, grid=(S // tm,),
               in_specs=[BS((tm, 256), lambda i: (i, 0)), BS((tm, 128), lambda i: (i, 2)), BS((tm, 128), lambda i: (i, 3)),
                         lay(1, 256), lay(1, 128), lay(256, 512), lay(128, 512), lay(128, 512), tab, tab, tab],
               out_specs=[BS((tm, 512), lambda i: (i, 0))] * 3 + [BS((512, tm), lambda i: (0, i))] * 2,
               out_shape=[SDS((S, 512), BF16)] * 3 + [SDS((512, S), BF16)] * 2)(
        proj, proj, proj, gq3, gkv3, wuq, wk, wv, tc, ts1, ts2)


def _mla_prep_bwd(dq, dk, dv, proj, gq3, gkv3, wuq, wk, wv, tabs, l, name):
    S = proj.shape[0]
    tm = min(512, S)
    tc, ts1, ts2 = tabs

    def body(dq_ref, dk_ref, dv_ref, cq_ref, ckv_ref, gq_ref, gkv_ref, wuq_ref, wk_ref, wv_ref, c_ref, s1_ref, s2_ref,
             dcq_ref, dckv_ref, dkr_ref, dwuq_ref, dwk_ref, dwv_ref, dgq_ref, dgkv_ref):
        @pl.when(pl.program_id(0) == 0)
        def _():
            for r in (dwuq_ref, dwk_ref, dwv_ref, dgq_ref, dgkv_ref):
                r[...] = jnp.zeros_like(r)

        c, s1, s2 = c_ref[...], s1_ref[...], s2_ref[...]
        dqp = jnp.concatenate(
            [_rope_t(dq_ref[h * 128:(h + 1) * 128, :].T * QK_SCALE, c, s1, s2) for h in range(HEADS)], axis=1).astype(BF16)
        cq = cq_ref[...]
        rq = lax.rsqrt(jnp.mean(cq * cq, axis=-1, keepdims=True) + EPS)
        cqh = cq * rq
        gq_v = gq_ref[...]
        dwuq_ref[...] += _dot_tn((cqh * gq_v).astype(BF16), dqp)
        dcqn = _dot_nt(dqp, wuq_ref[...])
        dgq_ref[...] += jnp.sum(dcqn * cqh, axis=0, keepdims=True)
        dxh = dcqn * gq_v
        dcq_ref[...] = (rq * (dxh - cqh * jnp.mean(dxh * cqh, axis=-1, keepdims=True))).astype(BF16)

        dkb = dk_ref[...].astype(BF16)
        dvb = dv_ref[...].astype(BF16)
        ckv = ckv_ref[...]
        rkv = lax.rsqrt(jnp.mean(ckv * ckv, axis=-1, keepdims=True) + EPS)
        ckh = ckv * rkv
        gkv_v = gkv_ref[...]
        ckvn = (ckh * gkv_v).astype(BF16)
        dwk_ref[...] += _dot_tn(ckvn, dkb)
        dwv_ref[...] += _dot_tn(ckvn, dvb)
        dckvn = _dot_nt(dkb, wk_ref[...]) + _dot_nt(dvb, wv_ref[...])
        dgkv_ref[...] += jnp.sum(dckvn * ckh, axis=0, keepdims=True)
        dyh = dckvn * gkv_v
        dckv_ref[...] = (rkv * (dyh - ckh * jnp.mean(dyh * ckh, axis=-1, keepdims=True))).astype(BF16)
        dks = dk_ref[:, 0:128] + dk_ref[:, 128:256] + dk_ref[:, 256:384] + dk_ref[:, 384:512]
        dkr_ref[...] = _rope_t(dks, c, s1, s2).astype(BF16)

    full = lambda a, b: BS((a, b), lambda i: (0, 0))
    lay = lambda a, b: BS((None, a, b), lambda i: (l, 0, 0))
    tab = BS((tm, 128), lambda i: (i, 0))
    row = lambda w: BS((tm, w), lambda i: (i, 0))
    return _pc(body, name=name, grid=(S // tm,),
               in_specs=[BS((512, tm), lambda i: (0, i)), row(512), row(512), BS((tm, 256), lambda i: (i, 0)),
                         BS((tm, 128), lambda i: (i, 2)),
                         lay(1, 256), lay(1, 128), lay(256, 512), lay(128, 512), lay(128, 512), tab, tab, tab],
               out_specs=[row(256), row(128), row(128), full(256, 512), full(128, 512), full(128, 512),
                          full(8, 256), full(8, 128)],
               out_shape=[SDS((S, 256), BF16), SDS((S, 128), BF16), SDS((S, 128), BF16), SDS((256, 512), F32),
                          SDS((128, 512), F32), SDS((128, 512), F32), SDS((8, 256), F32), SDS((8, 128), F32)])(
        dq, dk, dv, proj, proj, gq3, gkv3, wuq, wk, wv, tc, ts1, ts2)


def _causal_steps(n, q_outer):
    if q_outer:
        pairs = [(i, j) for i in range(n) for j in range(i + 1)]
    else:
        pairs = [(i, j) for j in range(n) for i in range(j, n)]
    return jnp.asarray([p[0] for p in pairs], jnp.int32), jnp.asarray([p[1] for p in pairs], jnp.int32)


def _mla_attn(q, k, vt, gts, layer, name):
    S = q.shape[0]
    t = min(512, S)
    n = S // t
    ng = len(gts)

    qi, kj = _causal_steps(n, True)
    last = qi.shape[0] - 1

    def body(qi_ref, kj_ref, q_ref, k_ref, vt_ref, *rest):
        (ya_ref, lse_ref), g_refs = rest[ng:ng + 2], rest[ng + 2:2 * ng + 2]
        m_sc, l_sc, acc_sc = rest[2 * ng + 2:2 * ng + 5]
        i, j = qi_ref[pl.program_id(1)], kj_ref[pl.program_id(1)]
        if ng:
            phases = _gather_phases(g_refs, [g.shape for g in gts], rest[2 * ng + 5], rest[2 * ng + 6], layer)
            for ph, (pp, ss) in zip(phases[:2], ((0, 0), (1, 0))):
                pl.when((pl.program_id(0) == pp) & (pl.program_id(1) == ss))(ph)

        @pl.when(j == 0)
        def _():
            m_sc[...] = jnp.full_like(m_sc, NEG_INF)
            l_sc[...] = jnp.zeros_like(l_sc)
            acc_sc[...] = jnp.zeros_like(acc_sc)

        def step(masked):
            for hh in range(2):
                sl = slice(hh * 128, (hh + 1) * 128)
                st = _dot_nt(k_ref[:, sl], q_ref[:, sl])
                if masked:
                    key = lax.broadcasted_iota(jnp.int32, (t, t), 0)
                    qry = lax.broadcasted_iota(jnp.int32, (t, t), 1)
                    st = jnp.where(key <= qry, st, NEG_INF)
                m_prev = m_sc[hh]
                m_new = jnp.maximum(m_prev, jnp.max(st, axis=0, keepdims=True))
                p = jnp.exp(st - m_new)
                alpha = jnp.exp(m_prev - m_new)
                l_sc[hh] = alpha * l_sc[hh] + jnp.sum(p, axis=0, keepdims=True)
                acc_sc[hh] = alpha * acc_sc[hh] + _dot(vt_ref[sl, :], p.astype(BF16))
                m_sc[hh] = m_new

        @pl.when(j < i)
        def _():
            step(False)

        @pl.when(j == i)
        def _():
            step(True)
            ya_ref[...] = (acc_sc[0] / l_sc[0] + acc_sc[1] / l_sc[1]).T
            for hh in range(2):
                lse_ref[hh] = m_sc[hh] + jnp.log(l_sc[hh])

        if ng:
            pl.when((pl.program_id(0) == 1) & (pl.program_id(1) == last))(phases[2])

    gs = pltpu.PrefetchScalarGridSpec(
        num_scalar_prefetch=2, grid=(2, qi.shape[0]),
        in_specs=[BS((t, 256), lambda p, s, qi, kj: (qi[s], p)), BS((t, 256), lambda p, s, qi, kj: (kj[s], p)),
                  BS((256, t), lambda p, s, qi, kj: (p, kj[s]))] + [ANY] * ng,
        out_specs=[BS((t, 128), lambda p, s, qi, kj: (qi[s], p)), BS((2, 1, t), lambda p, s, qi, kj: (p, 0, qi[s]))]
        + [ANY] * ng,
        scratch_shapes=[pltpu.VMEM((2, 1, t), F32), pltpu.VMEM((2, 1, t), F32), pltpu.VMEM((2, 128, t), F32)]
        + ([pltpu.SemaphoreType.DMA((7 * ng,)), pltpu.SemaphoreType.DMA((7 * ng,))] if ng else []))
    out = pl.pallas_call(body, name=name, grid_spec=gs,
                         out_shape=[SDS((S, 256), F32), SDS((HEADS, 1, S), F32)] + [SDS(g.shape, g.dtype) for g in gts],
                         input_output_aliases={5 + m: 2 + m for m in range(ng)},
                         compiler_params=pltpu.CompilerParams(dimension_semantics=("arbitrary",) * 2,
                                                              vmem_limit_bytes=VMEM_LIMIT, has_side_effects=bool(ng)))(
        qi, kj, q, k, vt, *gts)
    return out[0], out[1], list(out[2:])


def _mla_delta(dycat, ya, name):
    S = ya.shape[0]
    t = min(512, S)

    def body(do_ref, ya_ref, d_ref):
        prod = do_ref[...] * ya_ref[...]
        for p in range(2):
            pt = prod[:, p * 128:(p + 1) * 128].T
            d_ref[2 * p] = jnp.sum(pt[0:64, :], axis=0, keepdims=True)
            d_ref[2 * p + 1] = jnp.sum(pt[64:128, :], axis=0, keepdims=True)

    return _pc(body, name=name, grid=(S // t,),
               in_specs=[BS((t, 256), lambda i: (i, 0)), BS((t, 256), lambda i: (i, 0))],
               out_specs=BS((HEADS, 1, t), lambda i: (0, 0, i)), out_shape=SDS((HEADS, 1, S), F32))(dycat, ya)


def _mla_attn_bwd(q, k, kt, v, dya, lse, delta, rider, name):
    S = q.shape[0]
    t = min(512, S)
    n = S // t
    nr = rider.n if rider else 0

    qi, kj = _causal_steps(n, False)
    last = qi.shape[0] - 1

    def body(qi_ref, kj_ref, q_ref, k_ref, kt_ref, v_ref, do_ref, lse_ref, delta_ref, *rest):
        dqt_ref, dk_ref, dv_ref = rest[nr:nr + 3]
        r_io = (rest[:nr], rest[nr + 3:2 * nr + 3], rest[2 * nr + 3:])
        i, j = qi_ref[pl.program_id(1)], kj_ref[pl.program_id(1)]
        if rider:
            pl.when((pl.program_id(0) == 0) & (pl.program_id(1) == 0))(lambda: rider.start(*r_io))

        @pl.when(pl.program_id(1) == 0)
        def _():
            dqt_ref[...] = jnp.zeros_like(dqt_ref)

        @pl.when(i == j)
        def _():
            dk_ref[...] = jnp.zeros_like(dk_ref)
            dv_ref[...] = jnp.zeros_like(dv_ref)

        def step(masked):
            dob = do_ref[...].astype(BF16)
            cols = pl.ds(pl.multiple_of(i * t, t), t)
            for hh in range(2):
                sl = slice(hh * 128, (hh + 1) * 128)
                qv = q_ref[:, sl]
                p = jnp.exp(_dot_nt(k_ref[:, sl], qv) - lse_ref[hh])
                if masked:
                    key = lax.broadcasted_iota(jnp.int32, (t, t), 0)
                    qry = lax.broadcasted_iota(jnp.int32, (t, t), 1)
                    p = jnp.where(key <= qry, p, 0.0)
                dv_ref[:, sl] += _dot(p.astype(BF16), dob)
                ds = (p * (_dot_nt(v_ref[:, sl], dob) - delta_ref[hh])).astype(BF16)
                dk_ref[:, sl] += _dot(ds, qv)
                dqt_ref[sl, cols] += _dot(kt_ref[sl, :], ds)

        @pl.when(i > j)
        def _():
            step(False)

        @pl.when(i == j)
        def _():
            step(True)

        if rider:
            pl.when((pl.program_id(0) == 1) & (pl.program_id(1) == last))(lambda: rider.wait(*r_io))

    qs = BS((t, 256), lambda p, s, qi, kj: (qi[s], p))
    ks = BS((t, 256), lambda p, s, qi, kj: (kj[s], p))
    rowv = BS((2, 1, t), lambda p, s, qi, kj: (p, 0, qi[s]))
    gs = pltpu.PrefetchScalarGridSpec(
        num_scalar_prefetch=2, grid=(2, qi.shape[0]),
        in_specs=[qs, ks, BS((256, t), lambda p, s, qi, kj: (p, kj[s])), ks,
                  BS((t, 128), lambda p, s, qi, kj: (qi[s], p)), rowv, rowv] + [ANY] * nr,
        out_specs=[BS((256, S), lambda p, s, qi, kj: (p, 0)), ks, ks] + [ANY] * nr,
        scratch_shapes=rider.scratch() if rider else [])
    out = pl.pallas_call(body, name=name, grid_spec=gs,
                         out_shape=[SDS((512, S), F32), SDS((S, 512), F32), SDS((S, 512), F32)]
                         + (rider.out_shape if rider else []),
                         compiler_params=pltpu.CompilerParams(dimension_semantics=("arbitrary",) * 2,
                                                              vmem_limit_bytes=VMEM_LIMIT, has_side_effects=bool(rider)))(
        qi, kj, q, k, kt, v, dya, lse, delta, *(rider.arrs if rider else []))
    return out[0], out[1], out[2], list(out[3:])


def _swa_scores(qm, kk, valid, distf, slope, sink):
    sc = _dot_nt(qm, kk) * SWA_SCALE
    sc = jnp.where(valid, sc - slope * distf, NEG_INF)
    m = jnp.maximum(jnp.max(sc, axis=-1, keepdims=True), sink)
    e = jnp.exp(sc - m)
    esink = jnp.exp(sink - m)
    den = jnp.sum(e, axis=-1, keepdims=True) + esink
    return e / den, esink / den


def _swa_masks():
    r = lax.broadcasted_iota(jnp.int32, (BLK, 2 * BLK), 0)
    c = lax.broadcasted_iota(jnp.int32, (BLK, 2 * BLK), 1)
    dist = r + BLK - c
    return (dist >= 0) & (dist < SWA_WINDOW), c >= BLK, dist.astype(F32)


def _to_half(xb, pos, b):
    return xb if pos == b else pltpu.roll(xb, 64, axis=1)


def _swa(proj, sinks, l, name):
    S = proj.shape[0]
    nb = S // BLK

    def body(q_ref, k_ref, v_ref, sink_ref, o_ref, kp, vp):
        kp[0:BLK, :] = jnp.zeros((BLK, 128), BF16)
        vp[0:BLK, :] = jnp.zeros((BLK, 128), BF16)
        kp[BLK:, :] = k_ref[...].astype(BF16)
        vp[BLK:, :] = v_ref[...].astype(BF16)
        lo = lax.broadcasted_iota(jnp.int32, (BLK, 128), 1) < 64
        band, cur, distf = _swa_masks()

        def blk(i, carry):
            st = pl.multiple_of(i * BLK, BLK)
            kk = kp[pl.ds(st, 2 * BLK), :]
            vv = vp[pl.ds(st, 2 * BLK), :]
            valid = band & (cur | (i > 0))
            for b in range(2):
                half = lo if b == 0 else ~lo
                qb = q_ref[pl.ds(st, BLK), b * 128:(b + 1) * 128]
                outs = []
                for pos in range(2):
                    h = 2 * b + pos
                    qm = jnp.where(half, _to_half(qb, pos, b), 0.0).astype(BF16)
                    p, _ = _swa_scores(qm, kk, valid, distf, SLOPES[h], sink_ref[l, h])
                    outs.append(_to_half(_dot(p.astype(BF16), vv), pos, b))
                o_ref[pl.ds(st, BLK), b * 128:(b + 1) * 128] = jnp.where(lo, outs[0], outs[1])
            return carry

        lax.fori_loop(0, nb, blk, 0, unroll=2)

    return _pc(body, name=name, grid=(1,),
               in_specs=[BS((S, 256), lambda i: (0, C_QS // 256)), BS((S, 128), lambda i: (0, C_KS // 128)),
                         BS((S, 128), lambda i: (0, C_VS // 128)), BS(memory_space=pltpu.SMEM)],
               out_specs=BS((S, 256), lambda i: (0, 0)),
               out_shape=SDS((S, 256), F32),
               scratch=[pltpu.VMEM((S + BLK, 128), BF16), pltpu.VMEM((S + BLK, 128), BF16)])(proj, proj, proj, sinks)


def _swa_bwd(proj, sinks, dyd, l, name, rider=None):
    S = proj.shape[0]
    nb = S // BLK
    nr = rider.n if rider else 0

    def body(q_ref, k_ref, v_ref, sink_ref, do_ref, *rest):
        dq_ref, dk_ref, dv_ref, dsink_ref = rest[nr:nr + 4]
        kp, vp, dkp, dvp = rest[2 * nr + 4:2 * nr + 8]
        r_io = (rest[:nr], rest[nr + 4:2 * nr + 4], rest[2 * nr + 8:])
        if rider:
            rider.start(*r_io)
        kp[0:BLK, :] = jnp.zeros((BLK, 128), BF16)
        vp[0:BLK, :] = jnp.zeros((BLK, 128), BF16)
        kp[BLK:, :] = k_ref[...].astype(BF16)
        vp[BLK:, :] = v_ref[...].astype(BF16)
        dkp[...] = jnp.zeros_like(dkp)
        dvp[...] = jnp.zeros_like(dvp)
        lo = lax.broadcasted_iota(jnp.int32, (BLK, 128), 1) < 64
        lane8 = lax.broadcasted_iota(jnp.int32, (8, 128), 1)
        band, cur, distf = _swa_masks()

        def blk(i, dsink):
            st = pl.multiple_of(i * BLK, BLK)
            kk = kp[pl.ds(st, 2 * BLK), :]
            vv = vp[pl.ds(st, 2 * BLK), :]
            valid = band & (cur | (i > 0))
            dkk = jnp.zeros((2 * BLK, 128), F32)
            dvv = jnp.zeros((2 * BLK, 128), F32)
            for b in range(2):
                half = lo if b == 0 else ~lo
                qb = q_ref[pl.ds(st, BLK), b * 128:(b + 1) * 128]
                dob = do_ref[pl.ds(st, BLK), b * 128:(b + 1) * 128]
                dqs = []
                for pos in range(2):
                    h = 2 * b + pos
                    qm = jnp.where(half, _to_half(qb, pos, b), 0.0).astype(BF16)
                    dom = jnp.where(half, _to_half(dob, pos, b), 0.0).astype(BF16)
                    p, psink = _swa_scores(qm, kk, valid, distf, SLOPES[h], sink_ref[l, h])
                    dp = _dot_nt(dom, vv)
                    dvv = dvv + _dot_tn(p.astype(BF16), dom)
                    delta = jnp.sum(p * dp, axis=-1, keepdims=True)
                    dsink = dsink + jnp.where(lane8 == h, -jnp.sum(psink * delta), 0.0)
                    dsc = (p * (dp - delta) * SWA_SCALE).astype(BF16)
                    dqs.append(_to_half(_dot(dsc, kk), pos, b))
                    dkk = dkk + _dot_tn(dsc, qm)
                dq_ref[pl.ds(st, BLK), b * 128:(b + 1) * 128] = jnp.where(lo, dqs[0], dqs[1]).astype(BF16)
            dkp[pl.ds(st, 2 * BLK), :] += dkk
            dvp[pl.ds(st, 2 * BLK), :] += dvv
            return dsink

        dsink_ref[...] = lax.fori_loop(0, nb, blk, jnp.zeros((8, 128), F32), unroll=2)
        dk_ref[...] = dkp[BLK:, :].astype(BF16)
        dv_ref[...] = dvp[BLK:, :].astype(BF16)
        if rider:
            rider.wait(*r_io)

    in_specs = [BS((S, 256), lambda i: (0, C_QS // 256)), BS((S, 128), lambda i: (0, C_KS // 128)),
                BS((S, 128), lambda i: (0, C_VS // 128)), BS(memory_space=pltpu.SMEM), BS((S, 256), lambda i: (0, 3))]
    out_specs = [BS((S, 256), lambda i: (0, 0)), BS((S, 128), lambda i: (0, 0)), BS((S, 128), lambda i: (0, 0)),
                 BS((8, 128), lambda i: (0, 0))]
    out_shape = [SDS((S, 256), BF16), SDS((S, 128), BF16), SDS((S, 128), BF16), SDS((8, 128), F32)]
    scratch = [pltpu.VMEM((S + BLK, 128), BF16), pltpu.VMEM((S + BLK, 128), BF16),
               pltpu.VMEM((S + BLK, 128), F32), pltpu.VMEM((S + BLK, 128), F32)]
    if not rider:
        return _pc(body, name=name, grid=(1,), in_specs=in_specs, out_specs=out_specs, out_shape=out_shape,
                   scratch=scratch)(proj, proj, proj, sinks, dyd)
    out = pl.pallas_call(body, name=name, grid=(1,), in_specs=in_specs + [ANY] * nr, out_specs=out_specs + [ANY] * nr,
                         out_shape=out_shape + rider.out_shape, scratch_shapes=scratch + rider.scratch(),
                         compiler_params=pltpu.CompilerParams(dimension_semantics=("arbitrary",),
                                                              vmem_limit_bytes=VMEM_LIMIT, has_side_effects=True))(
        proj, proj, proj, sinks, dyd, *rider.arrs)
    return (*out[:4], list(out[4:]))


def _down(x, k, t):
    return jnp.where(t >= k, pltpu.roll(x, k, axis=0), 0.0)


def _up(x, k, t):
    n = x.shape[0]
    return jnp.where(t < n - k, pltpu.roll(x, n - k, axis=0), 0.0)


def _conv(proj, w8, l, name):
    S = proj.shape[0]

    def body(gb_ref, gc_ref, u_ref, w_ref, y_ref):
        t = lax.broadcasted_iota(jnp.int32, (S, 128), 0)
        z = gc_ref[...] * u_ref[...]
        c = w_ref[2:3, :] * z + w_ref[1:2, :] * _down(z, 1, t) + w_ref[0:1, :] * _down(z, 2, t)
        y_ref[...] = gb_ref[...] * c

    col = lambda c0: BS((S, 128), lambda i: (0, c0 // 128 + i))
    return _pc(body, name=name, grid=(2,),
               in_specs=[col(C_GB), col(C_GC), col(C_UC), BS((None, 8, 128), lambda i: (l, 0, i))],
               out_specs=BS((S, 128), lambda i: (0, i)), out_shape=SDS((S, 256), F32))(proj, proj, proj, w8)


def _conv_bwd(proj, w8, dycat, l, name):
    S = proj.shape[0]

    def body(gb_ref, gc_ref, u_ref, w_ref, dy_ref, dgb_ref, dgc_ref, du_ref, dw_ref):
        t = lax.broadcasted_iota(jnp.int32, (S, 128), 0)
        gc, u = gc_ref[...], u_ref[...]
        z = gc * u
        z1, z2 = _down(z, 1, t), _down(z, 2, t)
        w0, w1, w2 = w_ref[0:1, :], w_ref[1:2, :], w_ref[2:3, :]
        dy = dy_ref[...]
        dgb_ref[...] = (dy * (w2 * z + w1 * z1 + w0 * z2)).astype(BF16)
        dc = dy * gb_ref[...]
        dz = w2 * dc + w1 * _up(dc, 1, t) + w0 * _up(dc, 2, t)
        dgc_ref[...] = (dz * u).astype(BF16)
        du_ref[...] = (dz * gc).astype(BF16)
        row = lax.broadcasted_iota(jnp.int32, (8, 128), 0)
        sums = [jnp.sum(dc * zz, axis=0, keepdims=True) for zz in (z2, z1, z)]
        dw_ref[...] = jnp.where(row == 0, sums[0], jnp.where(row == 1, sums[1], jnp.where(row == 2, sums[2], 0.0)))

    col = lambda c0: BS((S, 128), lambda i: (0, c0 // 128 + i))
    out = BS((S, 128), lambda i: (0, i))
    return _pc(body, name=name, grid=(2,),
               in_specs=[col(C_GB), col(C_GC), col(C_UC), BS((None, 8, 128), lambda i: (l, 0, i)), col(256)],
               out_specs=[out, out, out, BS((8, 128), lambda i: (0, i))],
               out_shape=[SDS((S, 256), BF16)] * 3 + [SDS((8, 256), F32)])(proj, proj, proj, w8, dycat)


def _pool_parts(u, t, first):
    lo = lax.broadcasted_iota(jnp.int32, u.shape, 1) < 64
    s2 = u + _down(u, 1, t)
    s4 = s2 + _down(s2, 2, t)
    s8 = s4 + _down(s4, 4, t)
    s16 = s8 + _down(s8, 8, t)
    win = jnp.where(lo, jnp.where(first, s2, s8), jnp.where(first, s4, s16))
    wv = jnp.where(lo, jnp.where(first, 2, 8), jnp.where(first, 4, 16))
    cnt = jnp.minimum(t + 1, wv).astype(F32)
    return win, cnt, lo


def _pool(proj, pwd, scale3, l, name):
    S = proj.shape[0]

    def body(u_ref, pw_ref, sc_ref, y_ref):
        t = lax.broadcasted_iota(jnp.int32, (S, 128), 0)
        u = u_ref[...]
        win, cnt, _ = _pool_parts(u, t, pl.program_id(0) == 0)
        pooled = win / cnt - u
        y_ref[...] = _dot(pooled.astype(BF16), pw_ref[...]) * sc_ref[...]

    return _pc(body, name=name, grid=(2,),
               in_specs=[BS((S, 128), lambda i: (0, C_UP // 128 + i)), BS((None, 128, 128), lambda i: (l, i, 0)),
                         BS((None, 1, 128), lambda i: (l, 0, i))],
               out_specs=BS((S, 128), lambda i: (0, i)), out_shape=SDS((S, 256), F32))(proj, pwd, scale3)


def _pool_bwd(proj, pwd, scale3, dycat, l, name):
    S = proj.shape[0]

    def body(u_ref, pw_ref, sc_ref, dy_ref, du_ref, dpw_ref, dsc_ref):
        t = lax.broadcasted_iota(jnp.int32, (S, 128), 0)
        first = pl.program_id(0) == 0
        u = u_ref[...]
        win, cnt, lo = _pool_parts(u, t, first)
        pooled = (win / cnt - u).astype(BF16)
        pw = pw_ref[...]
        dy = dy_ref[...]
        dsc_ref[...] = jnp.broadcast_to(jnp.sum(dy * _dot(pooled, pw), axis=0, keepdims=True), (8, 128))
        dmb = (dy * sc_ref[...]).astype(BF16)
        dpw_ref[...] = _dot_tn(pooled, dmb)
        dpooled = _dot_nt(dmb, pw)
        a1 = dpooled / cnt
        a2 = a1 + _up(a1, 1, t)
        a4 = a2 + _up(a2, 2, t)
        a8 = a4 + _up(a4, 4, t)
        a16 = a8 + _up(a8, 8, t)
        dwin = jnp.where(lo, jnp.where(first, a2, a8), jnp.where(first, a4, a16))
        du_ref[...] = (dwin - dpooled).astype(BF16)

    return _pc(body, name=name, grid=(2,),
               in_specs=[BS((S, 128), lambda i: (0, C_UP // 128 + i)), BS((None, 128, 128), lambda i: (l, i, 0)),
                         BS((None, 1, 128), lambda i: (l, 0, i)), BS((S, 128), lambda i: (0, 4 + i))],
               out_specs=[BS((S, 128), lambda i: (0, i)), BS((128, 128), lambda i: (i, 0)), BS((8, 128), lambda i: (0, i))],
               out_shape=[SDS((S, 256), BF16), SDS((256, 128), F32), SDS((8, 256), F32)])(proj, pwd, scale3, dycat)


def _adamw(w, g, m, v, name):
    n, a, b = w.shape
    tr = _row_tile(a, b)

    def body(w_ref, g_ref, m_ref, v_ref, d_ref, nm_ref, nv_ref):
        gv = g_ref[...]
        m_new = B1 * m_ref[...] + (1.0 - B1) * gv
        v_new = B2 * v_ref[...] + (1.0 - B2) * (gv * gv)
        m_hat = m_new / (1.0 - B1 ** STEP)
        v_hat = v_new / (1.0 - B2 ** STEP)
        d_ref[...] = -LR * (m_hat / (jnp.sqrt(v_hat) + ADAM_EPS) + WD * w_ref[...])
        nm_ref[...] = m_new
        nv_ref[...] = v_new

    sp = BS((None, tr, b), lambda i, t: (i, t, 0))
    return _pc(body, name=name, grid=(n, a // tr), in_specs=[sp] * 4, out_specs=[sp] * 3,
               out_shape=[SDS((n, a, b), F32)] * 3)(w, g, m, v)


def _prefetch_call(body, name, grid, in_specs, out_specs, out_shape):
    gs = pltpu.PrefetchScalarGridSpec(num_scalar_prefetch=1, grid=grid, in_specs=in_specs, out_specs=out_specs)
    return pl.pallas_call(body, name=name, grid_spec=gs, out_shape=out_shape, compiler_params=_params(len(grid)))


def _place(w, kc, dtype, name):
    _, a, b = w.shape

    def body(kc_ref, w_ref, o_ref):
        o_ref[...] = w_ref[...].astype(dtype)

    return _prefetch_call(body, name, (2,), [BS((None, a, b), lambda l, kc: (l, 0, 0))],
                          BS((None, None, a, b), lambda l, kc: (l, kc[0], 0, 0)), SDS((2, 4, a, b), dtype))(kc, w)


def _pair_sum(g, got, kc, name):
    _, _, a, b = g.shape
    tr = _row_tile(a, b)

    def body(kc_ref, a_ref, b_ref, t32_ref, t16_ref):
        s = a_ref[...] + b_ref[...]
        t32_ref[...] = s
        t16_ref[...] = s.astype(BF16)

    sp = BS((None, tr, b), lambda k, t, kc: (k, t, 0))
    return _prefetch_call(body, name, (4, a // tr),
                          [BS((None, None, tr, b), lambda k, t, kc: (kc[1], k, t, 0)), sp], [sp, sp],
                          [SDS((4, a, b), F32), SDS((4, a, b), BF16)])(kc, g, got)


def _chip_sum(t32, got3, kc, name):
    _, a, b = t32.shape
    tr = _row_tile(a, b)

    def body(kc_ref, a_ref, b_ref, u_ref):
        u_ref[...] = ((a_ref[...] + b_ref[0].astype(F32)) + b_ref[1].astype(F32)) + b_ref[2].astype(F32)

    return _prefetch_call(body, name, (a // tr,),
                          [BS((None, tr, b), lambda t, kc: (kc[0], t, 0)), BS((3, tr, b), lambda t, kc: (0, t, 0))],
                          BS((None, tr, b), lambda t, kc: (kc[1], t, 0)), SDS((2, a, b), F32))(kc, t32, got3)


def _me():
    return lax.axis_index("x"), lax.axis_index("y"), lax.axis_index("c")


def _other_chips(x, y):
    return [(1 - x, y), (x, 1 - y), (1 - x, 1 - y)]


ANY = BS(memory_space=pl.ANY)
COMM_PARAMS = pltpu.CompilerParams(has_side_effects=True)


def _gather(arrs, name):
    n = len(arrs)

    def body(*refs):
        for phase in _gather_phases(refs[n:2 * n], [a.shape for a in arrs], refs[2 * n], refs[2 * n + 1]):
            phase()

    return pl.pallas_call(body, name=name, out_shape=[SDS(a.shape, a.dtype) for a in arrs],
                          in_specs=[ANY] * n, out_specs=[ANY] * n, input_output_aliases={t: t for t in range(n)},
                          scratch_shapes=[pltpu.SemaphoreType.DMA((7 * n,)), pltpu.SemaphoreType.DMA((7 * n,))],
                          compiler_params=COMM_PARAMS)(*arrs)


def _gather_phases(outs, shapes, send_sems, recv_sems, layer=None):
    n = len(outs)
    split = [s[2] % 32 == 0 for s in shapes]

    def plan():
        x, y, c = _me()
        return (c if layer is None else layer), (x, y), (x, y, c), (x, y, 1 - c), _other_chips(x, y)

    def role(moving, fn):
        if layer is None:
            fn()
        else:
            c = lax.axis_index("c")
            pl.when((c == layer) if moving else (c != layer))(fn)

    def blk(t, chip, layer, half=None):
        r = outs[t].at[layer, 2 * chip[0] + chip[1]]
        if half is None:
            return r
        rows = shapes[t][2] // 2
        return r.at[pl.ds(half * rows, rows)]

    def copy(t, k, ref, to):
        return pltpu.make_async_remote_copy(src_ref=ref, dst_ref=ref, send_sem=send_sems.at[7 * t + k],
                                            recv_sem=recv_sems.at[7 * t + k], device_id=to, device_id_type=MESH)

    def own_sends(t):
        c, chip, me, sib, (xn, yn, dg) = plan()
        cps = [copy(t, 0, blk(t, chip, c), (*xn, c)), copy(t, 1, blk(t, chip, c), (*yn, c))]
        return cps if split[t] else cps + [copy(t, 2, blk(t, chip, c), (*dg, c))]

    def relays(t):
        c, chip, me, sib, (xn, yn, dg) = plan()
        after_x = [copy(t, 4, blk(t, xn, c), sib)]
        after_y = [copy(t, 5, blk(t, yn, c), sib)]
        if split[t]:
            after_x.insert(0, copy(t, 2, blk(t, xn, c, 0), (*yn, c)))
            after_y.insert(0, copy(t, 3, blk(t, yn, c, 1), (*xn, c)))
        return after_x, after_y, [copy(t, 6, blk(t, dg, c), sib)]

    def send_own():
        for t in range(n):
            for cp in own_sends(t):
                cp.start()

    def relay_neighbours():
        c, chip, me, sib, (xn, yn, dg) = plan()
        for t in range(n):
            after_x, after_y, _ = relays(t)
            copy(t, 0, blk(t, xn, c), me).wait_recv()
            for cp in after_x:
                cp.start()
            copy(t, 1, blk(t, yn, c), me).wait_recv()
            for cp in after_y:
                cp.start()

    def relay_diagonal():
        c, chip, me, sib, (xn, yn, dg) = plan()
        for t in range(n):
            if split[t]:
                copy(t, 2, blk(t, dg, c, 0), me).wait_recv()
                copy(t, 3, blk(t, dg, c, 1), me).wait_recv()
            else:
                copy(t, 2, blk(t, dg, c), me).wait_recv()
            relays(t)[2][0].start()

    def take_sibling():
        _, chip, me, sib, (xn, yn, dg) = plan()
        theirs = 1 - lax.axis_index("c") if layer is None else layer
        for t in range(n):
            for k, peer in ((4, xn), (5, yn), (6, dg)):
                copy(t, k, blk(t, peer, theirs), me).wait_recv()

    def drain_sends():
        for t in range(n):
            after_x, after_y, after_d = relays(t)
            for cp in own_sends(t) + after_x + after_y + after_d:
                cp.wait_send()

    def phase3():
        role(True, relay_diagonal)
        role(False, take_sibling)
        role(True, drain_sends)

    return (lambda: role(True, send_own)), (lambda: role(True, relay_neighbours)), phase3


def _swap_copies(ins, outs, send_sems, recv_sems):
    x, y, c = _me()
    return [pltpu.make_async_remote_copy(src_ref=ins[t].at[1 - c], dst_ref=outs[t], send_sem=send_sems.at[t],
                                         recv_sem=recv_sems.at[t], device_id=(x, y, 1 - c), device_id_type=MESH)
            for t in range(len(ins))]


def _exchange_copies(ins, outs, send_sems, recv_sems):
    x, y, c = _me()
    return [pltpu.make_async_remote_copy(src_ref=ins[t].at[2 * cx + cy], dst_ref=outs[t].at[j],
                                         send_sem=send_sems.at[3 * t + j], recv_sem=recv_sems.at[3 * t + j],
                                         device_id=(cx, cy, c), device_id_type=MESH)
            for j, (cx, cy) in enumerate(_other_chips(x, y)) for t in range(len(ins))]


class _Rider:
    def __init__(self, arrs, out_shape, nsem, copies):
        self.arrs, self.out_shape, self.nsem, self.copies = list(arrs), out_shape, nsem, copies
        self.n = len(self.arrs)

    def scratch(self):
        return [pltpu.SemaphoreType.DMA((self.nsem,)), pltpu.SemaphoreType.DMA((self.nsem,))]

    def start(self, ins, outs, sems):
        for cp in self.copies(ins, outs, *sems):
            cp.start()

    def wait(self, ins, outs, sems):
        for cp in self.copies(ins, outs, *sems):
            cp.wait()


def _swap_rider(gs):
    return _Rider(gs, [SDS(g.shape[1:], g.dtype) for g in gs], len(gs), _swap_copies)


def _exchange_rider(ts):
    return _Rider(ts, [SDS((3,) + t.shape[1:], t.dtype) for t in ts], 3 * len(ts), _exchange_copies)


def _ride_alone(rider, name):
    n = rider.n

    def body(*refs):
        rider.start(refs[:n], refs[n:2 * n], refs[2 * n:])
        rider.wait(refs[:n], refs[n:2 * n], refs[2 * n:])

    return pl.pallas_call(body, name=name, out_shape=rider.out_shape, in_specs=[ANY] * n, out_specs=[ANY] * n,
                          scratch_shapes=rider.scratch(), compiler_params=COMM_PARAMS)(*rider.arrs)


def _join_layers(us, name):
    n = len(us)

    def body(*refs):
        outs, send_sems, recv_sems = refs[n:2 * n], refs[2 * n], refs[2 * n + 1]
        x, y, c = _me()
        cps = [pltpu.make_async_remote_copy(src_ref=outs[t].at[c], dst_ref=outs[t].at[c], send_sem=send_sems.at[t],
                                            recv_sem=recv_sems.at[t], device_id=(x, y, 1 - c), device_id_type=MESH)
               for t in range(n)]
        for cp in cps:
            cp.start()
        for cp in cps:
            cp.wait()

    return pl.pallas_call(body, name=name, out_shape=[SDS(u.shape, u.dtype) for u in us],
                          in_specs=[ANY] * n, out_specs=[ANY] * n, input_output_aliases={t: t for t in range(n)},
                          scratch_shapes=[pltpu.SemaphoreType.DMA((n,)), pltpu.SemaphoreType.DMA((n,))],
                          compiler_params=COMM_PARAMS)(*us)


def _allsum_small(v, name):
    M = v.shape[0]

    def body(x_ref, o_ref, all_ref, send_sems, recv_sems, local_sem):
        x, y, c = _me()
        me, sib = (x, y, c), (x, y, 1 - c)
        chips = _other_chips(x, y)

        def rows(px, py, pc):
            return all_ref.at[pl.ds((4 * px + 2 * py + pc) * M, M), :]

        def copy(k, block, to, src=None):
            return pltpu.make_async_remote_copy(src_ref=rows(*block) if src is None else src, dst_ref=rows(*block),
                                                send_sem=send_sems.at[k], recv_sem=recv_sems.at[k],
                                                device_id=to, device_id_type=MESH)

        mine = pltpu.make_async_copy(x_ref, rows(*me), local_sem)
        mine.start()
        first = [copy(0, me, sib, src=x_ref)]
        first += [copy(1 + j, me, (*chip, c), src=x_ref) for j, chip in enumerate(chips)]
        for cp in first:
            cp.start()
        passed = [copy(4 + j, (*chip, c), sib) for j, chip in enumerate(chips)]
        for j, chip in enumerate(chips):
            copy(1 + j, (*chip, c), me).wait_recv()
            passed[j].start()
        copy(0, sib, me).wait_recv()
        for j, chip in enumerate(chips):
            copy(4 + j, (*chip, 1 - c), me).wait_recv()
        for cp in first + passed:
            cp.wait_send()
        mine.wait()
        acc = all_ref[0:M, :]
        for d in range(1, 8):
            acc = acc + all_ref[d * M:(d + 1) * M, :]
        o_ref[...] = acc

    vm = BS(memory_space=pltpu.VMEM)
    return pl.pallas_call(body, name=name, out_shape=SDS((M, LANES), F32), in_specs=[vm], out_specs=vm,
                          scratch_shapes=[pltpu.VMEM((8 * M, LANES), F32), pltpu.SemaphoreType.DMA((7,)),
                                          pltpu.SemaphoreType.DMA((7,)), pltpu.SemaphoreType.DMA],
                          compiler_params=pltpu.CompilerParams(has_side_effects=True, vmem_limit_bytes=VMEM_LIMIT))(v)


FFN = ("w_gate_up", "w_down")
REST = ("w_in", "w_o", "w_uq", "w_ukv")
BIG = FFN + REST
TINY = ("conv_w",)
REPL = ("attn_norm", "mla_q_norm", "mla_kv_norm", "pool_w", "pool_scale", "swa_sinks", "mix_norm", "ffn_norm",
        "final_norm")
ORDER = ("attn_norm", "w_in", "mla_q_norm", "w_uq", "mla_kv_norm", "w_ukv", "conv_w", "pool_w", "pool_scale",
         "swa_sinks", "mix_norm", "w_o", "ffn_norm", "w_gate_up", "w_down", "final_norm")


def _rows8(shape):
    return -(-int(np.prod(shape)) // (8 * LANES)) * 8


def _pack(arrs):
    parts = []
    for a in arrs:
        r = _rows8(a.shape)
        parts.append(jnp.pad(a.reshape(-1), (0, r * LANES - a.size)).reshape(r, LANES))
    return jnp.concatenate(parts, axis=0)


def _unpack(buf, shapes):
    out, r0 = [], 0
    for s in shapes:
        n, r = int(np.prod(s)), _rows8(s)
        rows = buf[r0:r0 + r]
        out.append(rows.reshape(s) if n == r * LANES else rows.reshape(-1)[:n].reshape(s))
        r0 += r
    return out


def _cols_joined(g):
    return jnp.transpose(g, (0, 2, 1, 3)).reshape(g.shape[0], g.shape[2], 4 * g.shape[3])


def _cols_split(w):
    n, a, b4 = w.shape
    return jnp.transpose(w.reshape(n, a, 4, b4 // 4), (0, 2, 1, 3))


def _rope_tables(S):
    inv = 1.0 / (10000.0 ** (jnp.arange(0, 32, 2, dtype=F32) / 32))
    ang = jnp.arange(S, dtype=F32)[:, None] * inv[None, :]
    cos, sin = jnp.cos(ang), jnp.sin(ang)
    z = lambda w: jnp.zeros((S, w), F32)
    tc = jnp.concatenate([jnp.ones((S, 64), F32), cos, cos, jnp.ones((S, 32), F32)], axis=1)
    ts1 = jnp.concatenate([z(64), -sin, z(48)], axis=1)
    ts2 = jnp.concatenate([z(80), sin, z(32)], axis=1)
    return tc, ts1, ts2


def _pad_w_in(w):
    z = lambda n: jnp.zeros(w.shape[:-1] + (n,), w.dtype)
    return jnp.concatenate([w[..., 0:384], z(64), w[..., 384:416], z(32), w[..., 416:1952]], axis=-1)


def _unpad_w_in(d):
    return jnp.concatenate([d[..., 0:384], d[..., 448:480], d[..., 512:2048]], axis=-1)


def _pad_heads(w, src, offs):
    cols = []
    for h in range(HEADS):
        src0, n = src[h]
        z = lambda k: jnp.zeros(w.shape[:-1] + (k,), w.dtype)
        cols += [z(offs[h]), w[..., src0:src0 + n], z(128 - offs[h] - n)]
    return jnp.concatenate(cols, axis=-1)


UQ_SRC = [(h * 96, 96) for h in range(HEADS)]
KN_SRC = [(h * 128, 64) for h in range(HEADS)]
V_SRC = [(h * 128 + 64, 64) for h in range(HEADS)]
ZERO_OFF = [0] * HEADS
V_OFF = [(h % 2) * 64 for h in range(HEADS)]


def _unpad_heads(d, src, offs):
    return [d[..., h * 128 + offs[h]: h * 128 + offs[h] + src[h][1]] for h in range(HEADS)]


def kernel(x, attn_norm, w_in, mla_q_norm, w_uq, mla_kv_norm, w_ukv, conv_w, pool_w, pool_scale, swa_sinks, mix_norm, w_o, ffn_norm, w_gate_up, w_down, final_norm, loss_target, m_attn_norm, m_w_in, m_mla_q_norm, m_w_uq, m_mla_kv_norm, m_w_ukv, m_conv_w, m_pool_w, m_pool_scale, m_swa_sinks, m_mix_norm, m_w_o, m_ffn_norm, m_w_gate_up, m_w_down, m_final_norm, v_attn_norm, v_w_in, v_mla_q_norm, v_w_uq, v_mla_kv_norm, v_w_ukv, v_conv_w, v_pool_w, v_pool_scale, v_swa_sinks, v_mix_norm, v_w_o, v_ffn_norm, v_w_gate_up, v_w_down, v_final_norm):
    W = dict(attn_norm=attn_norm, w_in=w_in, mla_q_norm=mla_q_norm, w_uq=w_uq, mla_kv_norm=mla_kv_norm, w_ukv=w_ukv,
             conv_w=conv_w, pool_w=pool_w, pool_scale=pool_scale, swa_sinks=swa_sinks, mix_norm=mix_norm, w_o=w_o,
             ffn_norm=ffn_norm, w_gate_up=w_gate_up, w_down=w_down, final_norm=final_norm)
    M1 = dict(attn_norm=m_attn_norm, w_in=m_w_in, mla_q_norm=m_mla_q_norm, w_uq=m_w_uq, mla_kv_norm=m_mla_kv_norm,
              w_ukv=m_w_ukv, conv_w=m_conv_w, pool_w=m_pool_w, pool_scale=m_pool_scale, swa_sinks=m_swa_sinks,
              mix_norm=m_mix_norm, w_o=m_w_o, ffn_norm=m_ffn_norm, w_gate_up=m_w_gate_up, w_down=m_w_down,
              final_norm=m_final_norm)
    V2 = dict(attn_norm=v_attn_norm, w_in=v_w_in, mla_q_norm=v_mla_q_norm, w_uq=v_w_uq, mla_kv_norm=v_mla_kv_norm,
              w_ukv=v_w_ukv, conv_w=v_conv_w, pool_w=v_pool_w, pool_scale=v_pool_scale, swa_sinks=v_swa_sinks,
              mix_norm=v_mix_norm, w_o=v_w_o, ffn_norm=v_ffn_norm, w_gate_up=v_w_gate_up, w_down=v_w_down,
              final_norm=v_final_norm)
    S = x.shape[1]
    xc, yc, cc = _me()
    chip = 2 * xc + yc
    kc = jnp.stack([chip, cc]).astype(jnp.int32)

    first, later = ("w_in", "w_uq", "w_ukv", "conv_w"), ("w_o", "w_gate_up", "w_down")
    placed = {n: _place(W[n], kc, F32 if n == "conv_w" else BF16, f"place_{n}") for n in first + later}
    gi, gq, gkv, gcv = _gather([placed[n] for n in first], "gather_weights")
    later_w = [placed[n] for n in later]
    win_p = _pad_w_in(_cols_joined(gi))
    wuq_p = _pad_heads(_cols_joined(gq), UQ_SRC, ZERO_OFF)
    wukv = _cols_joined(gkv)
    wk_p = _pad_heads(wukv, KN_SRC, ZERO_OFF)
    wv_p = _pad_heads(wukv, V_SRC, V_OFF)
    conv8 = jnp.pad(_cols_joined(gcv), ((0, 0), (0, 5), (0, 0)))
    pwd = jnp.concatenate([jnp.concatenate(
        [jnp.pad(pool_w[:, 2 * b], ((0, 0), (0, 0), (0, 64))), jnp.pad(pool_w[:, 2 * b + 1], ((0, 0), (0, 0), (64, 0)))],
        axis=1) for b in range(2)], axis=1).astype(BF16)
    tabs = _rope_tables(S)
    g_attn, g_q, g_kv, g_mix, g_ffn, g_ps = (_g3(W[n]) for n in ("attn_norm", "mla_q_norm", "mla_kv_norm", "mix_norm",
                                                                  "ffn_norm", "pool_scale"))

    xs = [x[0]]
    saved = []
    for l in range(DEPTH):
        x0 = xs[-1]
        proj, h = _norm_mm(x0, g_attn, l, win_p, _wspec_in(l), D_INP, D_INP, F32, f"in_proj{l}")
        q, k, v, kt, vt = _mla_prep(proj, g_q, g_kv, wuq_p, wk_p, wv_p, tabs, l, f"mla_prep{l}")
        ya, lse, later_w = _mla_attn(q, k, vt, later_w, l, f"mla_attn{l}")
        go, gu4, gd = later_w
        wo, wdown = go.reshape(2, D, D), gd.reshape(2, D_FF, D)
        yb = _conv(proj, conv8, l, f"conv{l}")
        ycp = _pool(proj, pwd, g_ps, l, f"pool{l}")
        yd = _swa(proj, swa_sinks, l, f"swa{l}")
        x1, ycat, mixed = _mix_out(x0, ya, yb, ycp, yd, g_mix, wo, l, f"mix_out{l}")
        gu, h2 = _norm_mm(x1, g_ffn, l, gu4, _wspec_gu(l), 2 * D_FF, 2 * D_FF // 4, BF16, f"gate_up{l}")
        x2, act = _swiglu_mm_res(x1, gu, wdown, l, f"down{l}")
        saved.append(dict(x0=x0, proj=proj, h=h, q=q, k=k, kt=kt, v=v, lse=lse, x1=x1, ycat=ycat, mixed=mixed,
                          gu=gu, h2=h2, act=act))
        xs.append(x2)

    dx, dx16, dg_final, loss_tile = _loss_head(xs[-1], final_norm.reshape(1, D), loss_target[0], "loss_head")
    loss = lax.psum(loss_tile[0, 0] * (0.5 / D), ("x", "y", "c"))

    G = {n: [None] * DEPTH for n in ("w_uq", "w_ukv") + TINY + REPL if n != "final_norm"}
    gw_in = gw_o = gw_gu = gw_down = None
    for l in reversed(range(DEPTH)):
        sv = saved[l]
        dgu = _bwd_down(dx16, wdown, sv["gu"], l, f"down_bwd{l}")
        gw_down = _mm_tn(sv["act"], dx16, l, gw_down, f"dw_down{l}")
        gw_gu = _mm_tn(sv["h2"], dgu, l, gw_gu, f"dw_gate_up{l}", split4=True)
        exchange_gu = exchange_down = None
        if l == 0:
            g_f = [gw_gu, gw_down.reshape(2, 4, D_FF // 4, D)]
            dx1, dx1_16, dg, got_f = _mm_nt_normbwd(dgu, gu4, l, sv["x1"], g_ffn, dx, 1, f"gate_up_bwd{l}",
                                                    rider=_swap_rider(g_f))
            pairs_f = [_pair_sum(g, o, kc, f"rs_pair_sum_{n}") for g, o, n in zip(g_f, got_f, FFN)]
            exchange_gu, exchange_down = _exchange_rider([pairs_f[0][1]]), _exchange_rider([pairs_f[1][1]])
        else:
            dx1, dx1_16, dg = _mm_nt_normbwd(dgu, gu4, l, sv["x1"], g_ffn, dx, 1, f"gate_up_bwd{l}")
        G["ffn_norm"][l] = dg[0]
        gw_o = _mm_tn(sv["mixed"], dx1_16, l, gw_o, f"dw_o{l}")
        dycat, dg = _mm_nt_normbwd(dx1_16, wo.reshape(2, 1, D, D), l, sv["ycat"], g_mix, None, 4, f"mix_bwd{l}")
        G["mix_norm"][l] = dg[0]

        proj = sv["proj"]
        delta = _mla_delta(dycat, sv["ycat"], f"mla_delta{l}")
        dq, dk, dv, got3_gu = _mla_attn_bwd(sv["q"], sv["k"], sv["kt"], sv["v"], dycat, sv["lse"], delta, exchange_gu,
                                            f"mla_attn_bwd{l}")
        dcq, dckv, dkr, dwuq, dwk, dwv, dgq, dgkv = _mla_prep_bwd(
            dq, dk, dv, proj, g_q, g_kv, wuq_p, wk_p, wv_p, tabs, l, f"mla_prep_bwd{l}")
        dgb, dgc, duc, dcw = _conv_bwd(proj, conv8, dycat, l, f"conv_bwd{l}")
        dup, dpw, dps = _pool_bwd(proj, pwd, g_ps, dycat, l, f"pool_bwd{l}")
        if l == 0:
            dqs, dks, dvs, dsink, got3_down = _swa_bwd(proj, swa_sinks, dycat, l, f"swa_bwd{l}", rider=exchange_down)
            got3_f = got3_gu + got3_down
        else:
            dqs, dks, dvs, dsink = _swa_bwd(proj, swa_sinks, dycat, l, f"swa_bwd{l}")
        dproj = jnp.concatenate([dcq, dckv, dkr, dgb, dgc, duc, dup, dqs, dks, dvs], axis=1)
        gw_in = _mm_tn(sv["h"], dproj, l, gw_in, f"dw_in{l}")
        G["w_uq"][l] = jnp.concatenate(_unpad_heads(dwuq, UQ_SRC, ZERO_OFF), axis=1)
        kn, vv = _unpad_heads(dwk, KN_SRC, ZERO_OFF), _unpad_heads(dwv, V_SRC, V_OFF)
        G["w_ukv"][l] = jnp.concatenate([t for h in range(HEADS) for t in (kn[h], vv[h])], axis=1)
        if l == 0:
            g_r = [_cols_split(_unpad_w_in(gw_in)), gw_o.reshape(2, 4, D // 4, D), _cols_split(jnp.stack(G["w_uq"])),
                   _cols_split(jnp.stack(G["w_ukv"]))]
            dx, dx16, dg, got_r = _mm_nt_normbwd(dproj, win_p.reshape(2, 1, D, D_INP), l, sv["x0"], g_attn, dx1, 1,
                                                 f"in_proj_bwd{l}", rider=_swap_rider(g_r))
        else:
            dx, dx16, dg = _mm_nt_normbwd(dproj, win_p.reshape(2, 1, D, D_INP), l, sv["x0"], g_attn, dx1, 1,
                                          f"in_proj_bwd{l}")
        G["attn_norm"][l] = dg[0]
        G["mla_q_norm"][l] = dgq[0]
        G["mla_kv_norm"][l] = dgkv[0]
        G["conv_w"][l] = dcw[0:3]
        G["pool_w"][l] = jnp.stack([dpw[0:64, 0:64], dpw[64:128, 64:128], dpw[128:192, 0:64], dpw[192:256, 64:128]])
        G["pool_scale"][l] = dps[0]
        G["swa_sinks"][l] = dsink[0, 0:4]
    grad_x = dx[None]
    Gl = {n: jnp.stack(G[n]) for n in TINY + REPL if n != "final_norm"}
    Gl["final_norm"] = dg_final[0]

    us_f = [_chip_sum(p[0], o3, kc, f"rs_chip_sum_{n}") for p, o3, n in zip(pairs_f, got3_f, FFN)]
    gsum_f = _join_layers(us_f, "rs_join_cores_ffn")
    pairs_r = [_pair_sum(g, o, kc, f"rs_pair_sum_{n}") for g, o, n in zip(g_r, got_r, REST)]
    got3_r = _ride_alone(_exchange_rider([p[1] for p in pairs_r]), "rs_exchange_chips")
    us_r = [_chip_sum(p[0], o3, kc, f"rs_chip_sum_{n}") for p, o3, n in zip(pairs_r, got3_r, REST)]
    gsum_r = _join_layers(us_r, "rs_join_cores")
    res = {}
    for n, g in zip(BIG, gsum_f + gsum_r):
        d_, m_, v_ = _adamw(W[n], g, M1[n], V2[n], f"adamw_{n}")
        res["g", n], res["d", n], res["m", n], res["v", n] = g, d_, m_, v_

    small = TINY + REPL
    full_shapes = [Gl[n].shape for n in small]
    summed = _unpack(_allsum_small(_pack([Gl[n] for n in small]), "allsum_small"), full_shapes)
    gs = {}
    for n, g in zip(small, summed):
        if n in TINY:
            wdt = W[n].shape[2]
            g = lax.dynamic_slice_in_dim(g, chip * wdt, wdt, axis=2)
        gs[n] = g
    own_shapes = [W[n].shape for n in small]
    pk = lambda src: _pack([src[n] for n in small])[None]
    d_s, m_s, v_s = _adamw(pk(W), pk(gs), pk(M1), pk(V2), "adamw_small")
    for key, buf in (("d", d_s), ("m", m_s), ("v", v_s)):
        for n, a in zip(small, _unpack(buf[0], own_shapes)):
            res[key, n] = a
    for n in small:
        res["g", n] = gs[n]

    return (loss, grad_x, *[res["g", n] for n in ORDER], *[res["d", n] for n in ORDER],
            *[res["m", n] for n in ORDER], *[res["v", n] for n in ORDER])
```

```python
import math

import numpy as np
import jax
import jax.numpy as jnp
from jax import lax
from jax.experimental import pallas as pl
from jax.experimental.pallas import tpu as pltpu

F32, BF16 = jnp.float32, jnp.bfloat16
SDS = jax.ShapeDtypeStruct
BS = pl.BlockSpec
MESH = pl.DeviceIdType.MESH

D = 1024
DEPTH = 2
HEADS = 4
D_FF = 2816
D_INP = 2048
EPS = 1e-6
SWA_WINDOW = 128
BLK = 128
SLOPES = tuple(2.0 ** (-8.0 * (i + 1) / 4) for i in range(4))
QK_SCALE = 1.0 / math.sqrt(96)
SWA_SCALE = 1.0 / math.sqrt(64)
LR, B1, B2, ADAM_EPS, WD, STEP = 0.001, 0.9, 0.999, 1e-08, 0.01, 10

LANES = 1024
VMEM_LIMIT = 56 * 1024 * 1024
NEG_INF = float("-inf")

C_CQ, C_CKV, C_KR, C_GB, C_GC, C_UC, C_UP, C_QS, C_KS, C_VS = 0, 256, 384, 512, 768, 1024, 1280, 1536, 1792, 1920


def _params(ngrid):
    return pltpu.CompilerParams(dimension_semantics=("arbitrary",) * ngrid, vmem_limit_bytes=VMEM_LIMIT)


def _pc(body, *, name, grid, in_specs, out_specs, out_shape, scratch=(), aliases=None):
    return pl.pallas_call(
        body, name=name, grid=grid, in_specs=in_specs, out_specs=out_specs, out_shape=out_shape,
        scratch_shapes=scratch, input_output_aliases=aliases or {}, compiler_params=_params(len(grid)))


def _dot(a, b):
    return jnp.dot(a, b, preferred_element_type=F32)


def _dot_nt(a, b):
    return lax.dot_general(a, b, (((1,), (1,)), ((), ())), preferred_element_type=F32)


def _dot_tn(a, b):
    return lax.dot_general(a, b, (((0,), (0,)), ((), ())), preferred_element_type=F32)


def _tile(n, cap):
    if n <= cap:
        return n
    t = cap - cap % 128
    while n % t:
        t -= 128
    return t


def _row_tile(a, b, cap=262144):
    bp = -(-b // 128) * 128
    best = None
    for t in range(8, a + 1, 8):
        if a % t == 0 and t * bp <= cap:
            best = t
    return best if best is not None else a


def _g3(a):
    return a.reshape(a.shape[0], 1, a.shape[1])


def _norm_mm(x, g3, l, w, wspec, N, tn, out_dtype, name):
    S, K = x.shape
    tm = min(1024 if out_dtype == BF16 else 512, S)

    def body(x_ref, g_ref, w_ref, y_ref, h_ref):
        @pl.when(pl.program_id(1) == 0)
        def _():
            xv = x_ref[...]
            r = lax.rsqrt(jnp.mean(xv * xv, axis=-1, keepdims=True) + EPS)
            h_ref[...] = (xv * r * g_ref[...]).astype(BF16)

        y_ref[...] = _dot(h_ref[...], w_ref[...]).astype(out_dtype)

    return _pc(body, name=name, grid=(S // tm, N // tn),
               in_specs=[BS((tm, K), lambda i, j: (i, 0)), BS((None, 1, K), lambda i, j: (l, 0, 0)), wspec],
               out_specs=[BS((tm, tn), lambda i, j: (i, j)), BS((tm, K), lambda i, j: (i, 0))],
               out_shape=[SDS((S, N), out_dtype), SDS((S, K), BF16)])(x, g3, w)


def _wspec_in(l):
    return BS((None, D, D_INP), lambda i, j: (l, 0, j))


def _wspec_gu(l):
    return BS((None, None, D, 2 * D_FF // 4), lambda i, j: (l, j, 0, 0))


def _mix_out(x0, ya, yb, yc, yd, gmix3, wo, l, name):
    S = x0.shape[0]
    tm = min(512, S)

    def body(x_ref, ya_ref, yb_ref, yc_ref, yd_ref, g_ref, w_ref, x1_ref, ycat_ref, mixed_ref):
        groups = [ya_ref[...], yb_ref[...], yc_ref[...], yd_ref[...]]
        for gi, yg in enumerate(groups):
            sl = slice(gi * 256, (gi + 1) * 256)
            r = lax.rsqrt(jnp.mean(yg * yg, axis=-1, keepdims=True) + EPS)
            ycat_ref[:, sl] = yg
            mixed_ref[:, sl] = (yg * r * g_ref[:, sl]).astype(BF16)
        x1_ref[...] = x_ref[...] + _dot(mixed_ref[...], w_ref[...])

    row = lambda w: BS((tm, w), lambda i: (i, 0))
    return _pc(body, name=name, grid=(S // tm,),
               in_specs=[row(D), row(256), row(256), row(256), row(256), BS((None, 1, D), lambda i: (l, 0, 0)),
                         BS((None, D, D), lambda i: (l, 0, 0))],
               out_specs=[row(D), row(D), row(D)],
               out_shape=[SDS((S, D), F32), SDS((S, D), F32), SDS((S, D), BF16)])(x0, ya, yb, yc, yd, gmix3, wo)


def _swiglu_mm_res(x1, gu, wdown, l, name):
    S = x1.shape[0]
    tm = min(256, S)

    def body(x_ref, gate_ref, up_ref, w_ref, x2_ref, act_ref):
        gt = gate_ref[...].astype(F32)
        act = (gt * pl.reciprocal(1.0 + jnp.exp(-gt), approx=True) * up_ref[...].astype(F32)).astype(BF16)
        act_ref[...] = act
        x2_ref[...] = x_ref[...] + _dot(act, w_ref[...])

    return _pc(body, name=name, grid=(S // tm,),
               in_specs=[BS((tm, D), lambda i: (i, 0)), BS((tm, D_FF), lambda i: (i, 0)),
                         BS((tm, D_FF), lambda i: (i, 1)), BS((None, D_FF, D), lambda i: (l, 0, 0))],
               out_specs=[BS((tm, D), lambda i: (i, 0)), BS((tm, D_FF), lambda i: (i, 0))],
               out_shape=[SDS((S, D), F32), SDS((S, D_FF), BF16)])(x1, gu, gu, wdown)


def _loss_head(x, g, tgt, name):
    S = x.shape[0]
    tm = min(512, S)

    def body(x_ref, g_ref, t_ref, dx_ref, dx16_ref, dg_ref, loss_ref):
        @pl.when(pl.program_id(0) == 0)
        def _():
            dg_ref[...] = jnp.zeros_like(dg_ref)
            loss_ref[...] = jnp.zeros_like(loss_ref)

        xv = x_ref[...]
        r = lax.rsqrt(jnp.mean(xv * xv, axis=-1, keepdims=True) + EPS)
        xh = xv * r
        gv = g_ref[...]
        diff = xh * gv - t_ref[...]
        loss_ref[...] += jnp.sum(diff * diff)
        dy = diff * (1.0 / D)
        dg_ref[...] += jnp.sum(dy * xh, axis=0, keepdims=True)
        dxh = dy * gv
        dx = r * (dxh - xh * jnp.mean(dxh * xh, axis=-1, keepdims=True))
        dx_ref[...] = dx
        dx16_ref[...] = dx.astype(BF16)

    row = BS((tm, D), lambda i: (i, 0))
    return _pc(body, name=name, grid=(S // tm,),
               in_specs=[row, BS((1, D), lambda i: (0, 0)), row],
               out_specs=[row, row, BS((8, D), lambda i: (0, 0)), BS((8, 128), lambda i: (0, 0))],
               out_shape=[SDS((S, D), F32), SDS((S, D), BF16), SDS((8, D), F32), SDS((8, 128), F32)])(x, g, tgt)


def _mm_tn(a, b, l, prev, name, split4=False):
    S, Ka = a.shape
    N = b.shape[1]
    if split4:
        ta, tn = _tile(Ka, 256), N // 4
        out_shape = SDS((2, 4, Ka, tn), F32)
        out_spec = BS((None, None, ta, tn), lambda j, i: (l, j, i, 0))
    else:
        ta, tn = _tile(Ka, 512), _tile(N, 1024)
        out_shape = SDS((2, Ka, N), F32)
        out_spec = BS((None, ta, tn), lambda j, i: (l, i, j))

    def body(a_ref, b_ref, *rest):
        rest[-1][...] = _dot_tn(a_ref[...], b_ref[...])

    in_specs = [BS((S, ta), lambda j, i: (0, i)), BS((S, tn), lambda j, i: (0, j))]
    args = [a, b]
    if prev is not None:
        in_specs.append(BS(memory_space=pl.ANY))
        args.append(prev)
    return _pc(body, name=name, grid=(N // tn, Ka // ta), in_specs=in_specs, out_specs=out_spec, out_shape=out_shape,
               aliases={2: 0} if prev is not None else None)(*args)


def _bwd_down(dx16, wdown, gu, l, name):
    S = dx16.shape[0]
    tm = min(256, S)

    def body(dx_ref, w_ref, gate_ref, up_ref, dgu_ref):
        dxv = dx_ref[...]
        for c0 in range(0, D_FF, 256):
            cs = slice(c0, c0 + 256)
            dact = _dot_nt(dxv, w_ref[cs, :])
            gt = gate_ref[:, cs].astype(F32)
            sg = pl.reciprocal(1.0 + jnp.exp(-gt), approx=True)
            dgu_ref[:, cs] = (dact * up_ref[:, cs].astype(F32) * (sg * (1.0 + gt * (1.0 - sg)))).astype(BF16)
            dgu_ref[:, D_FF + c0:D_FF + c0 + 256] = (dact * (gt * sg)).astype(BF16)

    return _pc(body, name=name, grid=(S // tm,),
               in_specs=[BS((tm, D), lambda i: (i, 0)), BS((None, D_FF, D), lambda i: (l, 0, 0)),
                         BS((tm, D_FF), lambda i: (i, 0)), BS((tm, D_FF), lambda i: (i, 1))],
               out_specs=BS((tm, 2 * D_FF), lambda i: (i, 0)),
               out_shape=SDS((S, 2 * D_FF), BF16))(dx16, wdown, gu, gu)


def _mm_nt_normbwd(dy, w4, l, x, g3, dres, ngroups, name, rider=None):
    S, K = dy.shape
    _, nk, _, kc = w4.shape
    tm = min(512, S)
    gw = D // ngroups
    has_res = dres is not None
    nr = rider.n if rider else 0
    n_in, n_out = 4 + has_res, 2 + has_res

    def body(*refs):
        dy_ref, w_ref, x_ref, g_ref = refs[:4]
        res_ref = refs[4] if has_res else None
        outs = refs[n_in + nr:n_in + nr + n_out]
        dx_ref, dg_ref = outs[0], outs[-1]
        dx16_ref = outs[1] if has_res else None
        r_io = (refs[n_in:n_in + nr], refs[n_in + nr + n_out:n_in + 2 * nr + n_out], refs[n_in + 2 * nr + n_out:])
        if rider:
            pl.when(pl.program_id(0) == 0)(lambda: rider.start(*r_io))

        @pl.when(pl.program_id(0) == 0)
        def _():
            dg_ref[...] = jnp.zeros_like(dg_ref)

        dh = _dot_nt(dy_ref[:, 0:kc], w_ref[0])
        for k in range(1, nk):
            dh = dh + _dot_nt(dy_ref[:, k * kc:(k + 1) * kc], w_ref[k])
        for gi in range(ngroups):
            sl = slice(gi * gw, (gi + 1) * gw)
            xg = x_ref[:, sl]
            r = lax.rsqrt(jnp.mean(xg * xg, axis=-1, keepdims=True) + EPS)
            xh = xg * r
            dhg = dh[:, sl]
            dg_ref[:, sl] += jnp.sum(dhg * xh, axis=0, keepdims=True)
            dxh = dhg * g_ref[:, sl]
            dxg = r * (dxh - xh * jnp.mean(dxh * xh, axis=-1, keepdims=True))
            if has_res:
                dxg = dxg + res_ref[:, sl]
                dx16_ref[:, sl] = dxg.astype(BF16)
            dx_ref[:, sl] = dxg
        if rider:
            pl.when(pl.program_id(0) == S // tm - 1)(lambda: rider.wait(*r_io))

    row = BS((tm, D), lambda i: (i, 0))
    in_specs = [BS((tm, K), lambda i: (i, 0)),
                BS((None, nk, D, kc), lambda i: (l, 0, 0, 0), pipeline_mode=pl.Buffered(1)), row,
                BS((None, 1, D), lambda i: (l, 0, 0))]
    args = [dy, w4, x, g3]
    out_specs, out_shape = [row], [SDS((S, D), F32)]
    if has_res:
        in_specs.append(row)
        args.append(dres)
        out_specs.append(row)
        out_shape.append(SDS((S, D), BF16))
    out_specs.append(BS((8, D), lambda i: (0, 0)))
    out_shape.append(SDS((8, D), F32))
    if not rider:
        return _pc(body, name=name, grid=(S // tm,), in_specs=in_specs, out_specs=out_specs, out_shape=out_shape)(*args)
    out = pl.pallas_call(body, name=name, grid=(S // tm,), in_specs=in_specs + [ANY] * nr,
                         out_specs=out_specs + [ANY] * nr, out_shape=out_shape + rider.out_shape,
                         scratch_shapes=rider.scratch(),
                         compiler_params=pltpu.CompilerParams(dimension_semantics=("arbitrary",),
                                                              vmem_limit_bytes=VMEM_LIMIT, has_side_effects=True))(
        *args, *rider.arrs)
    return (*out[:n_out], list(out[n_out:]))


def _rope(x, c, s1, s2):
    return x * c + pltpu.roll(x, 112, axis=1) * s1 + pltpu.roll(x, 16, axis=1) * s2


def _rope_t(dy, c, s1, s2):
    return dy * c + pltpu.roll(dy * s1, 16, axis=1) + pltpu.roll(dy * s2, 112, axis=1)


def _mla_prep(proj, gq3, gkv3, wuq, wk, wv, tabs, l, name):
    S = proj.shape[0]
    tm = min(512, S)
    tc, ts1, ts2 = tabs

    def body(cq_ref, ckv_ref, kr_ref, gq_ref, gkv_ref, wuq_ref, wk_ref, wv_ref, c_ref, s1_ref, s2_ref,
             q_ref, k_ref, v_ref, kt_ref, vt_ref):
        c, s1, s2 = c_ref[...], s1_ref[...], s2_ref[...]
        cq = cq_ref[...]
        rq = lax.rsqrt(jnp.mean(cq * cq, axis=-1, keepdims=True) + EPS)
        qa = _dot((cq * rq * gq_ref[...]).astype(BF16), wuq_ref[...])
        ckv = ckv_ref[...]
        rkv = lax.rsqrt(jnp.mean(ckv * ckv, axis=-1, keepdims=True) + EPS)
        ckvn = (ckv * rkv * gkv_ref[...]).astype(BF16)
        ka = _dot(ckvn, wk_ref[...])
        va = _dot(ckvn, wv_ref[...])
        v_ref[...] = va.astype(BF16)
        vt_ref[...] = va.T.astype(BF16)
        krr = _rope(kr_ref[...], c, s1, s2)
        for h in range(HEADS):
            sl = slice(h * 128, (h + 1) * 128)
            q_ref[:, sl] = (_rope(qa[:, sl], c, s1, s2) * QK_SCALE).astype(BF16)
            kh = ka[:, sl] + krr
            k_ref[:, sl] = kh.astype(BF16)
            kt_ref[sl, :] = kh.T.astype(BF16)

    lay = lambda a, b: BS((None, a, b), lambda i: (l, 0, 0))
    tab = BS((tm, 128), lambda i: (i, 0))
    return _pc(body, name=name, grid=(S // tm,),
               in_specs=[BS((tm, 256), lambda i: (i, 0)), BS((tm, 128), lambda i: (i, 2)), BS((tm, 128), lambda i: (i, 3)),
                         lay(1, 256), lay(1, 128), lay(256, 512), lay(128, 512), lay(128, 512), tab, tab, tab],
               out_specs=[BS((tm, 512), lambda i: (i, 0))] * 3 + [BS((512, tm), lambda i: (0, i))] * 2,
               out_shape=[SDS((S, 512), BF16)] * 3 + [SDS((512, S), BF16)] * 2)(
        proj, proj, proj, gq3, gkv3, wuq, wk, wv, tc, ts1, ts2)


def _mla_prep_bwd(dq, dk, dv, proj, gq3, gkv3, wuq, wk, wv, tabs, l, name):
    S = proj.shape[0]
    tm = min(512, S)
    tc, ts1, ts2 = tabs

    def body(dq_ref, dk_ref, dv_ref, cq_ref, ckv_ref, gq_ref, gkv_ref, wuq_ref, wk_ref, wv_ref, c_ref, s1_ref, s2_ref,
             dcq_ref, dckv_ref, dkr_ref, dwuq_ref, dwk_ref, dwv_ref, dgq_ref, dgkv_ref):
        @pl.when(pl.program_id(0) == 0)
        def _():
            for r in (dwuq_ref, dwk_ref, dwv_ref, dgq_ref, dgkv_ref):
                r[...] = jnp.zeros_like(r)

        c, s1, s2 = c_ref[...], s1_ref[...], s2_ref[...]
        dqp = jnp.concatenate(
            [_rope_t(dq_ref[h * 128:(h + 1) * 128, :].T * QK_SCALE, c, s1, s2) for h in range(HEADS)], axis=1).astype(BF16)
        cq = cq_ref[...]
        rq = lax.rsqrt(jnp.mean(cq * cq, axis=-1, keepdims=True) + EPS)
        cqh = cq * rq
        gq_v = gq_ref[...]
        dwuq_ref[...] += _dot_tn((cqh * gq_v).astype(BF16), dqp)
        dcqn = _dot_nt(dqp, wuq_ref[...])
        dgq_ref[...] += jnp.sum(dcqn * cqh, axis=0, keepdims=True)
        dxh = dcqn * gq_v
        dcq_ref[...] = (rq * (dxh - cqh * jnp.mean(dxh * cqh, axis=-1, keepdims=True))).astype(BF16)

        dkb = dk_ref[...].astype(BF16)
        dvb = dv_ref[...].astype(BF16)
        ckv = ckv_ref[...]
        rkv = lax.rsqrt(jnp.mean(ckv * ckv, axis=-1, keepdims=True) + EPS)
        ckh = ckv * rkv
        gkv_v = gkv_ref[...]
        ckvn = (ckh * gkv_v).astype(BF16)
        dwk_ref[...] += _dot_tn(ckvn, dkb)
        dwv_ref[...] += _dot_tn(ckvn, dvb)
        dckvn = _dot_nt(dkb, wk_ref[...]) + _dot_nt(dvb, wv_ref[...])
        dgkv_ref[...] += jnp.sum(dckvn * ckh, axis=0, keepdims=True)
        dyh = dckvn * gkv_v
        dckv_ref[...] = (rkv * (dyh - ckh * jnp.mean(dyh * ckh, axis=-1, keepdims=True))).astype(BF16)
        dks = dk_ref[:, 0:128] + dk_ref[:, 128:256] + dk_ref[:, 256:384] + dk_ref[:, 384:512]
        dkr_ref[...] = _rope_t(dks, c, s1, s2).astype(BF16)

    full = lambda a, b: BS((a, b), lambda i: (0, 0))
    lay = lambda a, b: BS((None, a, b), lambda i: (l, 0, 0))
    tab = BS((tm, 128), lambda i: (i, 0))
    row = lambda w: BS((tm, w), lambda i: (i, 0))
    return _pc(body, name=name, grid=(S // tm,),
               in_specs=[BS((512, tm), lambda i: (0, i)), row(512), row(512), BS((tm, 256), lambda i: (i, 0)),
                         BS((tm, 128), lambda i: (i, 2)),
                         lay(1, 256), lay(1, 128), lay(256, 512), lay(128, 512), lay(128, 512), tab, tab, tab],
               out_specs=[row(256), row(128), row(128), full(256, 512), full(128, 512), full(128, 512),
                          full(8, 256), full(8, 128)],
               out_shape=[SDS((S, 256), BF16), SDS((S, 128), BF16), SDS((S, 128), BF16), SDS((256, 512), F32),
                          SDS((128, 512), F32), SDS((128, 512), F32), SDS((8, 256), F32), SDS((8, 128), F32)])(
        dq, dk, dv, proj, proj, gq3, gkv3, wuq, wk, wv, tc, ts1, ts2)


def _causal_steps(n, q_outer):
    if q_outer:
        pairs = [(i, j) for i in range(n) for j in range(i + 1)]
    else:
        pairs = [(i, j) for j in range(n) for i in range(j, n)]
    return jnp.asarray([p[0] for p in pairs], jnp.int32), jnp.asarray([p[1] for p in pairs], jnp.int32)


def _mla_attn(q, k, vt, gts, layer, name):
    S = q.shape[0]
    t = min(512, S)
    n = S // t
    ng = len(gts)

    qi, kj = _causal_steps(n, True)
    last = qi.shape[0] - 1

    def body(qi_ref, kj_ref, q_ref, k_ref, vt_ref, *rest):
        (ya_ref, lse_ref), g_refs = rest[ng:ng + 2], rest[ng + 2:2 * ng + 2]
        m_sc, l_sc, acc_sc = rest[2 * ng + 2:2 * ng + 5]
        i, j = qi_ref[pl.program_id(1)], kj_ref[pl.program_id(1)]
        if ng:
            phases = _gather_phases(g_refs, [g.shape for g in gts], rest[2 * ng + 5], rest[2 * ng + 6], layer)
            for ph, (pp, ss) in zip(phases[:2], ((0, 0), (1, 0))):
                pl.when((pl.program_id(0) == pp) & (pl.program_id(1) == ss))(ph)

        @pl.when(j == 0)
        def _():
            m_sc[...] = jnp.full_like(m_sc, NEG_INF)
            l_sc[...] = jnp.zeros_like(l_sc)
            acc_sc[...] = jnp.zeros_like(acc_sc)

        def step(masked):
            for hh in range(2):
                sl = slice(hh * 128, (hh + 1) * 128)
                st = _dot_nt(k_ref[:, sl], q_ref[:, sl])
                if masked:
                    key = lax.broadcasted_iota(jnp.int32, (t, t), 0)
                    qry = lax.broadcasted_iota(jnp.int32, (t, t), 1)
                    st = jnp.where(key <= qry, st, NEG_INF)
                m_prev = m_sc[hh]
                m_new = jnp.maximum(m_prev, jnp.max(st, axis=0, keepdims=True))
                p = jnp.exp(st - m_new)
                alpha = jnp.exp(m_prev - m_new)
                l_sc[hh] = alpha * l_sc[hh] + jnp.sum(p, axis=0, keepdims=True)
                acc_sc[hh] = alpha * acc_sc[hh] + _dot(vt_ref[sl, :], p.astype(BF16))
                m_sc[hh] = m_new

        @pl.when(j < i)
        def _():
            step(False)

        @pl.when(j == i)
        def _():
            step(True)
            ya_ref[...] = (acc_sc[0] / l_sc[0] + acc_sc[1] / l_sc[1]).T
            for hh in range(2):
                lse_ref[hh] = m_sc[hh] + jnp.log(l_sc[hh])

        if ng:
            pl.when((pl.program_id(0) == 1) & (pl.program_id(1) == last))(phases[2])

    gs = pltpu.PrefetchScalarGridSpec(
        num_scalar_prefetch=2, grid=(2, qi.shape[0]),
        in_specs=[BS((t, 256), lambda p, s, qi, kj: (qi[s], p)), BS((t, 256), lambda p, s, qi, kj: (kj[s], p)),
                  BS((256, t), lambda p, s, qi, kj: (p, kj[s]))] + [ANY] * ng,
        out_specs=[BS((t, 128), lambda p, s, qi, kj: (qi[s], p)), BS((2, 1, t), lambda p, s, qi, kj: (p, 0, qi[s]))]
        + [ANY] * ng,
        scratch_shapes=[pltpu.VMEM((2, 1, t), F32), pltpu.VMEM((2, 1, t), F32), pltpu.VMEM((2, 128, t), F32)]
        + ([pltpu.SemaphoreType.DMA((7 * ng,)), pltpu.SemaphoreType.DMA((7 * ng,))] if ng else []))
    out = pl.pallas_call(body, name=name, grid_spec=gs,
                         out_shape=[SDS((S, 256), F32), SDS((HEADS, 1, S), F32)] + [SDS(g.shape, g.dtype) for g in gts],
                         input_output_aliases={5 + m: 2 + m for m in range(ng)},
                         compiler_params=pltpu.CompilerParams(dimension_semantics=("arbitrary",) * 2,
                                                              vmem_limit_bytes=VMEM_LIMIT, has_side_effects=bool(ng)))(
        qi, kj, q, k, vt, *gts)
    return out[0], out[1], list(out[2:])


def _mla_delta(dycat, ya, name):
    S = ya.shape[0]
    t = min(512, S)

    def body(do_ref, ya_ref, d_ref):
        prod = do_ref[...] * ya_ref[...]
        for p in range(2):
            pt = prod[:, p * 128:(p + 1) * 128].T
            d_ref[2 * p] = jnp.sum(pt[0:64, :], axis=0, keepdims=True)
            d_ref[2 * p + 1] = jnp.sum(pt[64:128, :], axis=0, keepdims=True)

    return _pc(body, name=name, grid=(S // t,),
               in_specs=[BS((t, 256), lambda i: (i, 0)), BS((t, 256), lambda i: (i, 0))],
               out_specs=BS((HEADS, 1, t), lambda i: (0, 0, i)), out_shape=SDS((HEADS, 1, S), F32))(dycat, ya)


def _mla_attn_bwd(q, k, kt, v, dya, lse, delta, rider, name):
    S = q.shape[0]
    t = min(512, S)
    n = S // t
    nr = rider.n if rider else 0

    qi, kj = _causal_steps(n, False)
    last = qi.shape[0] - 1

    def body(qi_ref, kj_ref, q_ref, k_ref, kt_ref, v_ref, do_ref, lse_ref, delta_ref, *rest):
        dqt_ref, dk_ref, dv_ref = rest[nr:nr + 3]
        r_io = (rest[:nr], rest[nr + 3:2 * nr + 3], rest[2 * nr + 3:])
        i, j = qi_ref[pl.program_id(1)], kj_ref[pl.program_id(1)]
        if rider:
            pl.when((pl.program_id(0) == 0) & (pl.program_id(1) == 0))(lambda: rider.start(*r_io))

        @pl.when(pl.program_id(1) == 0)
        def _():
            dqt_ref[...] = jnp.zeros_like(dqt_ref)

        @pl.when(i == j)
        def _():
            dk_ref[...] = jnp.zeros_like(dk_ref)
            dv_ref[...] = jnp.zeros_like(dv_ref)

        def step(masked):
            dob = do_ref[...].astype(BF16)
            cols = pl.ds(pl.multiple_of(i * t, t), t)
            for hh in range(2):
                sl = slice(hh * 128, (hh + 1) * 128)
                qv = q_ref[:, sl]
                p = jnp.exp(_dot_nt(k_ref[:, sl], qv) - lse_ref[hh])
                if masked:
                    key = lax.broadcasted_iota(jnp.int32, (t, t), 0)
                    qry = lax.broadcasted_iota(jnp.int32, (t, t), 1)
                    p = jnp.where(key <= qry, p, 0.0)
                dv_ref[:, sl] += _dot(p.astype(BF16), dob)
                ds = (p * (_dot_nt(v_ref[:, sl], dob) - delta_ref[hh])).astype(BF16)
                dk_ref[:, sl] += _dot(ds, qv)
                dqt_ref[sl, cols] += _dot(kt_ref[sl, :], ds)

        @pl.when(i > j)
        def _():
            step(False)

        @pl.when(i == j)
        def _():
            step(True)

        if rider:
            pl.when((pl.program_id(0) == 1) & (pl.program_id(1) == last))(lambda: rider.wait(*r_io))

    qs = BS((t, 256), lambda p, s, qi, kj: (qi[s], p))
    ks = BS((t, 256), lambda p, s, qi, kj: (kj[s], p))
    rowv = BS((2, 1, t), lambda p, s, qi, kj: (p, 0, qi[s]))
    gs = pltpu.PrefetchScalarGridSpec(
        num_scalar_prefetch=2, grid=(2, qi.shape[0]),
        in_specs=[qs, ks, BS((256, t), lambda p, s, qi, kj: (p, kj[s])), ks,
                  BS((t, 128), lambda p, s, qi, kj: (qi[s], p)), rowv, rowv] + [ANY] * nr,
        out_specs=[BS((256, S), lambda p, s, qi, kj: (p, 0)), ks, ks] + [ANY] * nr,
        scratch_shapes=rider.scratch() if rider else [])
    out = pl.pallas_call(body, name=name, grid_spec=gs,
                         out_shape=[SDS((512, S), F32), SDS((S, 512), F32), SDS((S, 512), F32)]
                         + (rider.out_shape if rider else []),
                         compiler_params=pltpu.CompilerParams(dimension_semantics=("arbitrary",) * 2,
                                                              vmem_limit_bytes=VMEM_LIMIT, has_side_effects=bool(rider)))(
        qi, kj, q, k, kt, v, dya, lse, delta, *(rider.arrs if rider else []))
    return out[0], out[1], out[2], list(out[3:])


def _swa_scores(qm, kk, valid, distf, slope, sink):
    sc = _dot_nt(qm, kk) * SWA_SCALE
    sc = jnp.where(valid, sc - slope * distf, NEG_INF)
    m = jnp.maximum(jnp.max(sc, axis=-1, keepdims=True), sink)
    e = jnp.exp(sc - m)
    esink = jnp.exp(sink - m)
    den = jnp.sum(e, axis=-1, keepdims=True) + esink
    return e / den, esink / den


def _swa_masks():
    r = lax.broadcasted_iota(jnp.int32, (BLK, 2 * BLK), 0)
    c = lax.broadcasted_iota(jnp.int32, (BLK, 2 * BLK), 1)
    dist = r + BLK - c
    return (dist >= 0) & (dist < SWA_WINDOW), c >= BLK, dist.astype(F32)


def _to_half(xb, pos, b):
    return xb if pos == b else pltpu.roll(xb, 64, axis=1)


def _swa(proj, sinks, l, name):
    S = proj.shape[0]
    nb = S // BLK

    def body(q_ref, k_ref, v_ref, sink_ref, o_ref, kp, vp):
        kp[0:BLK, :] = jnp.zeros((BLK, 128), BF16)
        vp[0:BLK, :] = jnp.zeros((BLK, 128), BF16)
        kp[BLK:, :] = k_ref[...].astype(BF16)
        vp[BLK:, :] = v_ref[...].astype(BF16)
        lo = lax.broadcasted_iota(jnp.int32, (BLK, 128), 1) < 64
        band, cur, distf = _swa_masks()

        def blk(i, carry):
            st = pl.multiple_of(i * BLK, BLK)
            kk = kp[pl.ds(st, 2 * BLK), :]
            vv = vp[pl.ds(st, 2 * BLK), :]
            valid = band & (cur | (i > 0))
            for b in range(2):
                half = lo if b == 0 else ~lo
                qb = q_ref[pl.ds(st, BLK), b * 128:(b + 1) * 128]
                outs = []
                for pos in range(2):
                    h = 2 * b + pos
                    qm = jnp.where(half, _to_half(qb, pos, b), 0.0).astype(BF16)
                    p, _ = _swa_scores(qm, kk, valid, distf, SLOPES[h], sink_ref[l, h])
                    outs.append(_to_half(_dot(p.astype(BF16), vv), pos, b))
                o_ref[pl.ds(st, BLK), b * 128:(b + 1) * 128] = jnp.where(lo, outs[0], outs[1])
            return carry

        lax.fori_loop(0, nb, blk, 0, unroll=2)

    return _pc(body, name=name, grid=(1,),
               in_specs=[BS((S, 256), lambda i: (0, C_QS // 256)), BS((S, 128), lambda i: (0, C_KS // 128)),
                         BS((S, 128), lambda i: (0, C_VS // 128)), BS(memory_space=pltpu.SMEM)],
               out_specs=BS((S, 256), lambda i: (0, 0)),
               out_shape=SDS((S, 256), F32),
               scratch=[pltpu.VMEM((S + BLK, 128), BF16), pltpu.VMEM((S + BLK, 128), BF16)])(proj, proj, proj, sinks)


def _swa_bwd(proj, sinks, dyd, l, name, rider=None):
    S = proj.shape[0]
    nb = S // BLK
    nr = rider.n if rider else 0

    def body(q_ref, k_ref, v_ref, sink_ref, do_ref, *rest):
        dq_ref, dk_ref, dv_ref, dsink_ref = rest[nr:nr + 4]
        kp, vp, dkp, dvp = rest[2 * nr + 4:2 * nr + 8]
        r_io = (rest[:nr], rest[nr + 4:2 * nr + 4], rest[2 * nr + 8:])
        if rider:
            rider.start(*r_io)
        kp[0:BLK, :] = jnp.zeros((BLK, 128), BF16)
        vp[0:BLK, :] = jnp.zeros((BLK, 128), BF16)
        kp[BLK:, :] = k_ref[...].astype(BF16)
        vp[BLK:, :] = v_ref[...].astype(BF16)
        dkp[...] = jnp.zeros_like(dkp)
        dvp[...] = jnp.zeros_like(dvp)
        lo = lax.broadcasted_iota(jnp.int32, (BLK, 128), 1) < 64
        lane8 = lax.broadcasted_iota(jnp.int32, (8, 128), 1)
        band, cur, distf = _swa_masks()

        def blk(i, dsink):
            st = pl.multiple_of(i * BLK, BLK)
            kk = kp[pl.ds(st, 2 * BLK), :]
            vv = vp[pl.ds(st, 2 * BLK), :]
            valid = band & (cur | (i > 0))
            dkk = jnp.zeros((2 * BLK, 128), F32)
            dvv = jnp.zeros((2 * BLK, 128), F32)
            for b in range(2):
                half = lo if b == 0 else ~lo
                qb = q_ref[pl.ds(st, BLK), b * 128:(b + 1) * 128]
                dob = do_ref[pl.ds(st, BLK), b * 128:(b + 1) * 128]
                dqs = []
                for pos in range(2):
                    h = 2 * b + pos
                    qm = jnp.where(half, _to_half(qb, pos, b), 0.0).astype(BF16)
                    dom = jnp.where(half, _to_half(dob, pos, b), 0.0).astype(BF16)
                    p, psink = _swa_scores(qm, kk, valid, distf, SLOPES[h], sink_ref[l, h])
                    dp = _dot_nt(dom, vv)
                    dvv = dvv + _dot_tn(p.astype(BF16), dom)
                    delta = jnp.sum(p * dp, axis=-1, keepdims=True)
                    dsink = dsink + jnp.where(lane8 == h, -jnp.sum(psink * delta), 0.0)
                    dsc = (p * (dp - delta) * SWA_SCALE).astype(BF16)
                    dqs.append(_to_half(_dot(dsc, kk), pos, b))
                    dkk = dkk + _dot_tn(dsc, qm)
                dq_ref[pl.ds(st, BLK), b * 128:(b + 1) * 128] = jnp.where(lo, dqs[0], dqs[1]).astype(BF16)
            dkp[pl.ds(st, 2 * BLK), :] += dkk
            dvp[pl.ds(st, 2 * BLK), :] += dvv
            return dsink

        dsink_ref[...] = lax.fori_loop(0, nb, blk, jnp.zeros((8, 128), F32), unroll=2)
        dk_ref[...] = dkp[BLK:, :].astype(BF16)
        dv_ref[...] = dvp[BLK:, :].astype(BF16)
        if rider:
            rider.wait(*r_io)

    in_specs = [BS((S, 256), lambda i: (0, C_QS // 256)), BS((S, 128), lambda i: (0, C_KS // 128)),
                BS((S, 128), lambda i: (0, C_VS // 128)), BS(memory_space=pltpu.SMEM), BS((S, 256), lambda i: (0, 3))]
    out_specs = [BS((S, 256), lambda i: (0, 0)), BS((S, 128), lambda i: (0, 0)), BS((S, 128), lambda i: (0, 0)),
                 BS((8, 128), lambda i: (0, 0))]
    out_shape = [SDS((S, 256), BF16), SDS((S, 128), BF16), SDS((S, 128), BF16), SDS((8, 128), F32)]
    scratch = [pltpu.VMEM((S + BLK, 128), BF16), pltpu.VMEM((S + BLK, 128), BF16),
               pltpu.VMEM((S + BLK, 128), F32), pltpu.VMEM((S + BLK, 128), F32)]
    if not rider:
        return _pc(body, name=name, grid=(1,), in_specs=in_specs, out_specs=out_specs, out_shape=out_shape,
                   scratch=scratch)(proj, proj, proj, sinks, dyd)
    out = pl.pallas_call(body, name=name, grid=(1,), in_specs=in_specs + [ANY] * nr, out_specs=out_specs + [ANY] * nr,
                         out_shape=out_shape + rider.out_shape, scratch_shapes=scratch + rider.scratch(),
                         compiler_params=pltpu.CompilerParams(dimension_semantics=("arbitrary",),
                                                              vmem_limit_bytes=VMEM_LIMIT, has_side_effects=True))(
        proj, proj, proj, sinks, dyd, *rider.arrs)
    return (*out[:4], list(out[4:]))


def _down(x, k, t):
    return jnp.where(t >= k, pltpu.roll(x, k, axis=0), 0.0)


def _up(x, k, t):
    n = x.shape[0]
    return jnp.where(t < n - k, pltpu.roll(x, n - k, axis=0), 0.0)


def _conv(proj, w8, l, name):
    S = proj.shape[0]

    def body(gb_ref, gc_ref, u_ref, w_ref, y_ref):
        t = lax.broadcasted_iota(jnp.int32, (S, 128), 0)
        z = gc_ref[...] * u_ref[...]
        c = w_ref[2:3, :] * z + w_ref[1:2, :] * _down(z, 1, t) + w_ref[0:1, :] * _down(z, 2, t)
        y_ref[...] = gb_ref[...] * c

    col = lambda c0: BS((S, 128), lambda i: (0, c0 // 128 + i))
    return _pc(body, name=name, grid=(2,),
               in_specs=[col(C_GB), col(C_GC), col(C_UC), BS((None, 8, 128), lambda i: (l, 0, i))],
               out_specs=BS((S, 128), lambda i: (0, i)), out_shape=SDS((S, 256), F32))(proj, proj, proj, w8)


def _conv_bwd(proj, w8, dycat, l, name):
    S = proj.shape[0]

    def body(gb_ref, gc_ref, u_ref, w_ref, dy_ref, dgb_ref, dgc_ref, du_ref, dw_ref):
        t = lax.broadcasted_iota(jnp.int32, (S, 128), 0)
        gc, u = gc_ref[...], u_ref[...]
        z = gc * u
        z1, z2 = _down(z, 1, t), _down(z, 2, t)
        w0, w1, w2 = w_ref[0:1, :], w_ref[1:2, :], w_ref[2:3, :]
        dy = dy_ref[...]
        dgb_ref[...] = (dy * (w2 * z + w1 * z1 + w0 * z2)).astype(BF16)
        dc = dy * gb_ref[...]
        dz = w2 * dc + w1 * _up(dc, 1, t) + w0 * _up(dc, 2, t)
        dgc_ref[...] = (dz * u).astype(BF16)
        du_ref[...] = (dz * gc).astype(BF16)
        row = lax.broadcasted_iota(jnp.int32, (8, 128), 0)
        sums = [jnp.sum(dc * zz, axis=0, keepdims=True) for zz in (z2, z1, z)]
        dw_ref[...] = jnp.where(row == 0, sums[0], jnp.where(row == 1, sums[1], jnp.where(row == 2, sums[2], 0.0)))

    col = lambda c0: BS((S, 128), lambda i: (0, c0 // 128 + i))
    out = BS((S, 128), lambda i: (0, i))
    return _pc(body, name=name, grid=(2,),
               in_specs=[col(C_GB), col(C_GC), col(C_UC), BS((None, 8, 128), lambda i: (l, 0, i)), col(256)],
               out_specs=[out, out, out, BS((8, 128), lambda i: (0, i))],
               out_shape=[SDS((S, 256), BF16)] * 3 + [SDS((8, 256), F32)])(proj, proj, proj, w8, dycat)


def _pool_parts(u, t, first):
    lo = lax.broadcasted_iota(jnp.int32, u.shape, 1) < 64
    s2 = u + _down(u, 1, t)
    s4 = s2 + _down(s2, 2, t)
    s8 = s4 + _down(s4, 4, t)
    s16 = s8 + _down(s8, 8, t)
    win = jnp.where(lo, jnp.where(first, s2, s8), jnp.where(first, s4, s16))
    wv = jnp.where(lo, jnp.where(first, 2, 8), jnp.where(first, 4, 16))
    cnt = jnp.minimum(t + 1, wv).astype(F32)
    return win, cnt, lo


def _pool(proj, pwd, scale3, l, name):
    S = proj.shape[0]

    def body(u_ref, pw_ref, sc_ref, y_ref):
        t = lax.broadcasted_iota(jnp.int32, (S, 128), 0)
        u = u_ref[...]
        win, cnt, _ = _pool_parts(u, t, pl.program_id(0) == 0)
        pooled = win / cnt - u
        y_ref[...] = _dot(pooled.astype(BF16), pw_ref[...]) * sc_ref[...]

    return _pc(body, name=name, grid=(2,),
               in_specs=[BS((S, 128), lambda i: (0, C_UP // 128 + i)), BS((None, 128, 128), lambda i: (l, i, 0)),
                         BS((None, 1, 128), lambda i: (l, 0, i))],
               out_specs=BS((S, 128), lambda i: (0, i)), out_shape=SDS((S, 256), F32))(proj, pwd, scale3)


def _pool_bwd(proj, pwd, scale3, dycat, l, name):
    S = proj.shape[0]

    def body(u_ref, pw_ref, sc_ref, dy_ref, du_ref, dpw_ref, dsc_ref):
        t = lax.broadcasted_iota(jnp.int32, (S, 128), 0)
        first = pl.program_id(0) == 0
        u = u_ref[...]
        win, cnt, lo = _pool_parts(u, t, first)
        pooled = (win / cnt - u).astype(BF16)
        pw = pw_ref[...]
        dy = dy_ref[...]
        dsc_ref[...] = jnp.broadcast_to(jnp.sum(dy * _dot(pooled, pw), axis=0, keepdims=True), (8, 128))
        dmb = (dy * sc_ref[...]).astype(BF16)
        dpw_ref[...] = _dot_tn(pooled, dmb)
        dpooled = _dot_nt(dmb, pw)
        a1 = dpooled / cnt
        a2 = a1 + _up(a1, 1, t)
        a4 = a2 + _up(a2, 2, t)
        a8 = a4 + _up(a4, 4, t)
        a16 = a8 + _up(a8, 8, t)
        dwin = jnp.where(lo, jnp.where(first, a2, a8), jnp.where(first, a4, a16))
        du_ref[...] = (dwin - dpooled).astype(BF16)

    return _pc(body, name=name, grid=(2,),
               in_specs=[BS((S, 128), lambda i: (0, C_UP // 128 + i)), BS((None, 128, 128), lambda i: (l, i, 0)),
                         BS((None, 1, 128), lambda i: (l, 0, i)), BS((S, 128), lambda i: (0, 4 + i))],
               out_specs=[BS((S, 128), lambda i: (0, i)), BS((128, 128), lambda i: (i, 0)), BS((8, 128), lambda i: (0, i))],
               out_shape=[SDS((S, 256), BF16), SDS((256, 128), F32), SDS((8, 256), F32)])(proj, pwd, scale3, dycat)


def _adamw(w, g, m, v, name):
    n, a, b = w.shape
    tr = _row_tile(a, b)

    def body(w_ref, g_ref, m_ref, v_ref, d_ref, nm_ref, nv_ref):
        gv = g_ref[...]
        m_new = B1 * m_ref[...] + (1.0 - B1) * gv
        v_new = B2 * v_ref[...] + (1.0 - B2) * (gv * gv)
        m_hat = m_new / (1.0 - B1 ** STEP)
        v_hat = v_new / (1.0 - B2 ** STEP)
        d_ref[...] = -LR * (m_hat / (jnp.sqrt(v_hat) + ADAM_EPS) + WD * w_ref[...])
        nm_ref[...] = m_new
        nv_ref[...] = v_new

    sp = BS((None, tr, b), lambda i, t: (i, t, 0))
    return _pc(body, name=name, grid=(n, a // tr), in_specs=[sp] * 4, out_specs=[sp] * 3,
               out_shape=[SDS((n, a, b), F32)] * 3)(w, g, m, v)


def _prefetch_call(body, name, grid, in_specs, out_specs, out_shape):
    gs = pltpu.PrefetchScalarGridSpec(num_scalar_prefetch=1, grid=grid, in_specs=in_specs, out_specs=out_specs)
    return pl.pallas_call(body, name=name, grid_spec=gs, out_shape=out_shape, compiler_params=_params(len(grid)))


def _place(w, kc, dtype, name):
    _, a, b = w.shape

    def body(kc_ref, w_ref, o_ref):
        o_ref[...] = w_ref[...].astype(dtype)

    return _prefetch_call(body, name, (2,), [BS((None, a, b), lambda l, kc: (l, 0, 0))],
                          BS((None, None, a, b), lambda l, kc: (l, kc[0], 0, 0)), SDS((2, 4, a, b), dtype))(kc, w)


def _pair_sum(g, got, kc, name):
    _, _, a, b = g.shape
    tr = _row_tile(a, b)

    def body(kc_ref, a_ref, b_ref, t32_ref, t16_ref):
        s = a_ref[...] + b_ref[...]
        t32_ref[...] = s
        t16_ref[...] = s.astype(BF16)

    sp = BS((None, tr, b), lambda k, t, kc: (k, t, 0))
    return _prefetch_call(body, name, (4, a // tr),
                          [BS((None, None, tr, b), lambda k, t, kc: (kc[1], k, t, 0)), sp], [sp, sp],
                          [SDS((4, a, b), F32), SDS((4, a, b), BF16)])(kc, g, got)


def _chip_sum(t32, got3, kc, name):
    _, a, b = t32.shape
    tr = _row_tile(a, b)

    def body(kc_ref, a_ref, b_ref, u_ref):
        u_ref[...] = ((a_ref[...] + b_ref[0].astype(F32)) + b_ref[1].astype(F32)) + b_ref[2].astype(F32)

    return _prefetch_call(body, name, (a // tr,),
                          [BS((None, tr, b), lambda t, kc: (kc[0], t, 0)), BS((3, tr, b), lambda t, kc: (0, t, 0))],
                          BS((None, tr, b), lambda t, kc: (kc[1], t, 0)), SDS((2, a, b), F32))(kc, t32, got3)


def _me():
    return lax.axis_index("x"), lax.axis_index("y"), lax.axis_index("c")


def _other_chips(x, y):
    return [(1 - x, y), (x, 1 - y), (1 - x, 1 - y)]


ANY = BS(memory_space=pl.ANY)
COMM_PARAMS = pltpu.CompilerParams(has_side_effects=True)


def _gather(arrs, name):
    n = len(arrs)

    def body(*refs):
        for phase in _gather_phases(refs[n:2 * n], [a.shape for a in arrs], refs[2 * n], refs[2 * n + 1]):
            phase()

    return pl.pallas_call(body, name=name, out_shape=[SDS(a.shape, a.dtype) for a in arrs],
                          in_specs=[ANY] * n, out_specs=[ANY] * n, input_output_aliases={t: t for t in range(n)},
                          scratch_shapes=[pltpu.SemaphoreType.DMA((7 * n,)), pltpu.SemaphoreType.DMA((7 * n,))],
                          compiler_params=COMM_PARAMS)(*arrs)


def _gather_phases(outs, shapes, send_sems, recv_sems, layer=None):
    n = len(outs)
    split = [s[2] % 32 == 0 for s in shapes]

    def plan():
        x, y, c = _me()
        return (c if layer is None else layer), (x, y), (x, y, c), (x, y, 1 - c), _other_chips(x, y)

    def role(moving, fn):
        if layer is None:
            fn()
        else:
            c = lax.axis_index("c")
            pl.when((c == layer) if moving else (c != layer))(fn)

    def blk(t, chip, layer, half=None):
        r = outs[t].at[layer, 2 * chip[0] + chip[1]]
        if half is None:
            return r
        rows = shapes[t][2] // 2
        return r.at[pl.ds(half * rows, rows)]

    def copy(t, k, ref, to):
        return pltpu.make_async_remote_copy(src_ref=ref, dst_ref=ref, send_sem=send_sems.at[7 * t + k],
                                            recv_sem=recv_sems.at[7 * t + k], device_id=to, device_id_type=MESH)

    def own_sends(t):
        c, chip, me, sib, (xn, yn, dg) = plan()
        cps = [copy(t, 0, blk(t, chip, c), (*xn, c)), copy(t, 1, blk(t, chip, c), (*yn, c))]
        return cps if split[t] else cps + [copy(t, 2, blk(t, chip, c), (*dg, c))]

    def relays(t):
        c, chip, me, sib, (xn, yn, dg) = plan()
        after_x = [copy(t, 4, blk(t, xn, c), sib)]
        after_y = [copy(t, 5, blk(t, yn, c), sib)]
        if split[t]:
            after_x.insert(0, copy(t, 2, blk(t, xn, c, 0), (*yn, c)))
            after_y.insert(0, copy(t, 3, blk(t, yn, c, 1), (*xn, c)))
        return after_x, after_y, [copy(t, 6, blk(t, dg, c), sib)]

    def send_own():
        for t in range(n):
            for cp in own_sends(t):
                cp.start()

    def relay_neighbours():
        c, chip, me, sib, (xn, yn, dg) = plan()
        for t in range(n):
            after_x, after_y, _ = relays(t)
            copy(t, 0, blk(t, xn, c), me).wait_recv()
            for cp in after_x:
                cp.start()
            copy(t, 1, blk(t, yn, c), me).wait_recv()
            for cp in after_y:
                cp.start()

    def relay_diagonal():
        c, chip, me, sib, (xn, yn, dg) = plan()
        for t in range(n):
            if split[t]:
                copy(t, 2, blk(t, dg, c, 0), me).wait_recv()
                copy(t, 3, blk(t, dg, c, 1), me).wait_recv()
            else:
                copy(t, 2, blk(t, dg, c), me).wait_recv()
            relays(t)[2][0].start()

    def take_sibling():
        _, chip, me, sib, (xn, yn, dg) = plan()
        theirs = 1 - lax.axis_index("c") if layer is None else layer
        for t in range(n):
            for k, peer in ((4, xn), (5, yn), (6, dg)):
                copy(t, k, blk(t, peer, theirs), me).wait_recv()

    def drain_sends():
        for t in range(n):
            after_x, after_y, after_d = relays(t)
            for cp in own_sends(t) + after_x + after_y + after_d:
                cp.wait_send()

    def phase3():
        role(True, relay_diagonal)
        role(False, take_sibling)
        role(True, drain_sends)

    return (lambda: role(True, send_own)), (lambda: role(True, relay_neighbours)), phase3


def _swap_copies(ins, outs, send_sems, recv_sems):
    x, y, c = _me()
    return [pltpu.make_async_remote_copy(src_ref=ins[t].at[1 - c], dst_ref=outs[t], send_sem=send_sems.at[t],
                                         recv_sem=recv_sems.at[t], device_id=(x, y, 1 - c), device_id_type=MESH)
            for t in range(len(ins))]


def _exchange_copies(ins, outs, send_sems, recv_sems):
    x, y, c = _me()
    return [pltpu.make_async_remote_copy(src_ref=ins[t].at[2 * cx + cy], dst_ref=outs[t].at[j],
                                         send_sem=send_sems.at[3 * t + j], recv_sem=recv_sems.at[3 * t + j],
                                         device_id=(cx, cy, c), device_id_type=MESH)
            for j, (cx, cy) in enumerate(_other_chips(x, y)) for t in range(len(ins))]


class _Rider:
    def __init__(self, arrs, out_shape, nsem, copies):
        self.arrs, self.out_shape, self.nsem, self.copies = list(arrs), out_shape, nsem, copies
        self.n = len(self.arrs)

    def scratch(self):
        return [pltpu.SemaphoreType.DMA((self.nsem,)), pltpu.SemaphoreType.DMA((self.nsem,))]

    def start(self, ins, outs, sems):
        for cp in self.copies(ins, outs, *sems):
            cp.start()

    def wait(self, ins, outs, sems):
        for cp in self.copies(ins, outs, *sems):
            cp.wait()


def _swap_rider(gs):
    return _Rider(gs, [SDS(g.shape[1:], g.dtype) for g in gs], len(gs), _swap_copies)


def _exchange_rider(ts):
    return _Rider(ts, [SDS((3,) + t.shape[1:], t.dtype) for t in ts], 3 * len(ts), _exchange_copies)


def _ride_alone(rider, name):
    n = rider.n

    def body(*refs):
        rider.start(refs[:n], refs[n:2 * n], refs[2 * n:])
        rider.wait(refs[:n], refs[n:2 * n], refs[2 * n:])

    return pl.pallas_call(body, name=name, out_shape=rider.out_shape, in_specs=[ANY] * n, out_specs=[ANY] * n,
                          scratch_shapes=rider.scratch(), compiler_params=COMM_PARAMS)(*rider.arrs)


def _join_layers(us, name):
    n = len(us)

    def body(*refs):
        outs, send_sems, recv_sems = refs[n:2 * n], refs[2 * n], refs[2 * n + 1]
        x, y, c = _me()
        cps = [pltpu.make_async_remote_copy(src_ref=outs[t].at[c], dst_ref=outs[t].at[c], send_sem=send_sems.at[t],
                                            recv_sem=recv_sems.at[t], device_id=(x, y, 1 - c), device_id_type=MESH)
               for t in range(n)]
        for cp in cps:
            cp.start()
        for cp in cps:
            cp.wait()

    return pl.pallas_call(body, name=name, out_shape=[SDS(u.shape, u.dtype) for u in us],
                          in_specs=[ANY] * n, out_specs=[ANY] * n, input_output_aliases={t: t for t in range(n)},
                          scratch_shapes=[pltpu.SemaphoreType.DMA((n,)), pltpu.SemaphoreType.DMA((n,))],
                          compiler_params=COMM_PARAMS)(*us)


def _allsum_small(v, name, rider=None):
    M = v.shape[0]
    nr = rider.n if rider else 0

    def body(x_ref, *rest):
        o_ref = rest[nr]
        all_ref, send_sems, recv_sems, local_sem = rest[2 * nr + 1:2 * nr + 5]
        r_io = (rest[:nr], rest[nr + 1:2 * nr + 1], rest[2 * nr + 5:])
        if rider:
            rider.start(*r_io)
        x, y, c = _me()
        me, sib = (x, y, c), (x, y, 1 - c)
        chips = _other_chips(x, y)

        def rows(px, py, pc):
            return all_ref.at[pl.ds((4 * px + 2 * py + pc) * M, M), :]

        def copy(k, block, to, src=None):
            return pltpu.make_async_remote_copy(src_ref=rows(*block) if src is None else src, dst_ref=rows(*block),
                                                send_sem=send_sems.at[k], recv_sem=recv_sems.at[k],
                                                device_id=to, device_id_type=MESH)

        mine = pltpu.make_async_copy(x_ref, rows(*me), local_sem)
        mine.start()
        first = [copy(0, me, sib, src=x_ref)]
        first += [copy(1 + j, me, (*chip, c), src=x_ref) for j, chip in enumerate(chips)]
        for cp in first:
            cp.start()
        passed = [copy(4 + j, (*chip, c), sib) for j, chip in enumerate(chips)]
        for j, chip in enumerate(chips):
            copy(1 + j, (*chip, c), me).wait_recv()
            passed[j].start()
        copy(0, sib, me).wait_recv()
        for j, chip in enumerate(chips):
            copy(4 + j, (*chip, 1 - c), me).wait_recv()
        for cp in first + passed:
            cp.wait_send()
        mine.wait()
        acc = all_ref[0:M, :]
        for d in range(1, 8):
            acc = acc + all_ref[d * M:(d + 1) * M, :]
        o_ref[...] = acc
        if rider:
            rider.wait(*r_io)

    vm = BS(memory_space=pltpu.VMEM)
    out = pl.pallas_call(body, name=name, out_shape=[SDS((M, LANES), F32)] + (rider.out_shape if rider else []),
                         in_specs=[vm] + [ANY] * nr, out_specs=[vm] + [ANY] * nr,
                         scratch_shapes=[pltpu.VMEM((8 * M, LANES), F32), pltpu.SemaphoreType.DMA((7,)),
                                         pltpu.SemaphoreType.DMA((7,)), pltpu.SemaphoreType.DMA]
                         + (rider.scratch() if rider else []),
                         compiler_params=pltpu.CompilerParams(has_side_effects=True, vmem_limit_bytes=VMEM_LIMIT))(
        v, *(rider.arrs if rider else []))
    return out[0], list(out[1:])


FFN = ("w_gate_up", "w_down")
REST = ("w_in", "w_o", "w_uq", "w_ukv")
BIG = FFN + REST
TINY = ("conv_w",)
REPL = ("attn_norm", "mla_q_norm", "mla_kv_norm", "pool_w", "pool_scale", "swa_sinks", "mix_norm", "ffn_norm",
        "final_norm")
ORDER = ("attn_norm", "w_in", "mla_q_norm", "w_uq", "mla_kv_norm", "w_ukv", "conv_w", "pool_w", "pool_scale",
         "swa_sinks", "mix_norm", "w_o", "ffn_norm", "w_gate_up", "w_down", "final_norm")


def _rows8(shape):
    return -(-int(np.prod(shape)) // (8 * LANES)) * 8


def _pack(arrs):
    parts = []
    for a in arrs:
        r = _rows8(a.shape)
        parts.append(jnp.pad(a.reshape(-1), (0, r * LANES - a.size)).reshape(r, LANES))
    return jnp.concatenate(parts, axis=0)


def _unpack(buf, shapes):
    out, r0 = [], 0
    for s in shapes:
        n, r = int(np.prod(s)), _rows8(s)
        rows = buf[r0:r0 + r]
        out.append(rows.reshape(s) if n == r * LANES else rows.reshape(-1)[:n].reshape(s))
        r0 += r
    return out


def _cols_joined(g):
    return jnp.transpose(g, (0, 2, 1, 3)).reshape(g.shape[0], g.shape[2], 4 * g.shape[3])


def _cols_split(w):
    n, a, b4 = w.shape
    return jnp.transpose(w.reshape(n, a, 4, b4 // 4), (0, 2, 1, 3))


def _rope_tables(S):
    inv = 1.0 / (10000.0 ** (jnp.arange(0, 32, 2, dtype=F32) / 32))
    ang = jnp.arange(S, dtype=F32)[:, None] * inv[None, :]
    cos, sin = jnp.cos(ang), jnp.sin(ang)
    z = lambda w: jnp.zeros((S, w), F32)
    tc = jnp.concatenate([jnp.ones((S, 64), F32), cos, cos, jnp.ones((S, 32), F32)], axis=1)
    ts1 = jnp.concatenate([z(64), -sin, z(48)], axis=1)
    ts2 = jnp.concatenate([z(80), sin, z(32)], axis=1)
    return tc, ts1, ts2


def _pad_w_in(w):
    z = lambda n: jnp.zeros(w.shape[:-1] + (n,), w.dtype)
    return jnp.concatenate([w[..., 0:384], z(64), w[..., 384:416], z(32), w[..., 416:1952]], axis=-1)


def _unpad_w_in(d):
    return jnp.concatenate([d[..., 0:384], d[..., 448:480], d[..., 512:2048]], axis=-1)


def _pad_heads(w, src, offs):
    cols = []
    for h in range(HEADS):
        src0, n = src[h]
        z = lambda k: jnp.zeros(w.shape[:-1] + (k,), w.dtype)
        cols += [z(offs[h]), w[..., src0:src0 + n], z(128 - offs[h] - n)]
    return jnp.concatenate(cols, axis=-1)


UQ_SRC = [(h * 96, 96) for h in range(HEADS)]
KN_SRC = [(h * 128, 64) for h in range(HEADS)]
V_SRC = [(h * 128 + 64, 64) for h in range(HEADS)]
ZERO_OFF = [0] * HEADS
V_OFF = [(h % 2) * 64 for h in range(HEADS)]


def _unpad_heads(d, src, offs):
    return [d[..., h * 128 + offs[h]: h * 128 + offs[h] + src[h][1]] for h in range(HEADS)]


def kernel(x, attn_norm, w_in, mla_q_norm, w_uq, mla_kv_norm, w_ukv, conv_w, pool_w, pool_scale, swa_sinks, mix_norm, w_o, ffn_norm, w_gate_up, w_down, final_norm, loss_target, m_attn_norm, m_w_in, m_mla_q_norm, m_w_uq, m_mla_kv_norm, m_w_ukv, m_conv_w, m_pool_w, m_pool_scale, m_swa_sinks, m_mix_norm, m_w_o, m_ffn_norm, m_w_gate_up, m_w_down, m_final_norm, v_attn_norm, v_w_in, v_mla_q_norm, v_w_uq, v_mla_kv_norm, v_w_ukv, v_conv_w, v_pool_w, v_pool_scale, v_swa_sinks, v_mix_norm, v_w_o, v_ffn_norm, v_w_gate_up, v_w_down, v_final_norm):
    W = dict(attn_norm=attn_norm, w_in=w_in, mla_q_norm=mla_q_norm, w_uq=w_uq, mla_kv_norm=mla_kv_norm, w_ukv=w_ukv,
             conv_w=conv_w, pool_w=pool_w, pool_scale=pool_scale, swa_sinks=swa_sinks, mix_norm=mix_norm, w_o=w_o,
             ffn_norm=ffn_norm, w_gate_up=w_gate_up, w_down=w_down, final_norm=final_norm)
    M1 = dict(attn_norm=m_attn_norm, w_in=m_w_in, mla_q_norm=m_mla_q_norm, w_uq=m_w_uq, mla_kv_norm=m_mla_kv_norm,
              w_ukv=m_w_ukv, conv_w=m_conv_w, pool_w=m_pool_w, pool_scale=m_pool_scale, swa_sinks=m_swa_sinks,
              mix_norm=m_mix_norm, w_o=m_w_o, ffn_norm=m_ffn_norm, w_gate_up=m_w_gate_up, w_down=m_w_down,
              final_norm=m_final_norm)
    V2 = dict(attn_norm=v_attn_norm, w_in=v_w_in, mla_q_norm=v_mla_q_norm, w_uq=v_w_uq, mla_kv_norm=v_mla_kv_norm,
              w_ukv=v_w_ukv, conv_w=v_conv_w, pool_w=v_pool_w, pool_scale=v_pool_scale, swa_sinks=v_swa_sinks,
              mix_norm=v_mix_norm, w_o=v_w_o, ffn_norm=v_ffn_norm, w_gate_up=v_w_gate_up, w_down=v_w_down,
              final_norm=v_final_norm)
    S = x.shape[1]
    xc, yc, cc = _me()
    chip = 2 * xc + yc
    kc = jnp.stack([chip, cc]).astype(jnp.int32)

    first, later = ("w_in", "w_uq", "w_ukv", "conv_w"), ("w_o", "w_gate_up", "w_down")
    placed = {n: _place(W[n], kc, F32 if n == "conv_w" else BF16, f"place_{n}") for n in first + later}
    gi, gq, gkv, gcv = _gather([placed[n] for n in first], "gather_weights")
    later_w = [placed[n] for n in later]
    win_p = _pad_w_in(_cols_joined(gi))
    wuq_p = _pad_heads(_cols_joined(gq), UQ_SRC, ZERO_OFF)
    wukv = _cols_joined(gkv)
    wk_p = _pad_heads(wukv, KN_SRC, ZERO_OFF)
    wv_p = _pad_heads(wukv, V_SRC, V_OFF)
    conv8 = jnp.pad(_cols_joined(gcv), ((0, 0), (0, 5), (0, 0)))
    pwd = jnp.concatenate([jnp.concatenate(
        [jnp.pad(pool_w[:, 2 * b], ((0, 0), (0, 0), (0, 64))), jnp.pad(pool_w[:, 2 * b + 1], ((0, 0), (0, 0), (64, 0)))],
        axis=1) for b in range(2)], axis=1).astype(BF16)
    tabs = _rope_tables(S)
    g_attn, g_q, g_kv, g_mix, g_ffn, g_ps = (_g3(W[n]) for n in ("attn_norm", "mla_q_norm", "mla_kv_norm", "mix_norm",
                                                                  "ffn_norm", "pool_scale"))

    xs = [x[0]]
    saved = []
    for l in range(DEPTH):
        x0 = xs[-1]
        proj, h = _norm_mm(x0, g_attn, l, win_p, _wspec_in(l), D_INP, D_INP, F32, f"in_proj{l}")
        q, k, v, kt, vt = _mla_prep(proj, g_q, g_kv, wuq_p, wk_p, wv_p, tabs, l, f"mla_prep{l}")
        ya, lse, later_w = _mla_attn(q, k, vt, later_w, l, f"mla_attn{l}")
        go, gu4, gd = later_w
        wo, wdown = go.reshape(2, D, D), gd.reshape(2, D_FF, D)
        yb = _conv(proj, conv8, l, f"conv{l}")
        ycp = _pool(proj, pwd, g_ps, l, f"pool{l}")
        yd = _swa(proj, swa_sinks, l, f"swa{l}")
        x1, ycat, mixed = _mix_out(x0, ya, yb, ycp, yd, g_mix, wo, l, f"mix_out{l}")
        gu, h2 = _norm_mm(x1, g_ffn, l, gu4, _wspec_gu(l), 2 * D_FF, 2 * D_FF // 4, BF16, f"gate_up{l}")
        x2, act = _swiglu_mm_res(x1, gu, wdown, l, f"down{l}")
        saved.append(dict(x0=x0, proj=proj, h=h, q=q, k=k, kt=kt, v=v, lse=lse, x1=x1, ycat=ycat, mixed=mixed,
                          gu=gu, h2=h2, act=act))
        xs.append(x2)

    dx, dx16, dg_final, loss_tile = _loss_head(xs[-1], final_norm.reshape(1, D), loss_target[0], "loss_head")
    loss = lax.psum(loss_tile[0, 0] * (0.5 / D), ("x", "y", "c"))

    G = {n: [None] * DEPTH for n in ("w_uq", "w_ukv") + TINY + REPL if n != "final_norm"}
    gw_in = gw_o = gw_gu = gw_down = None
    for l in reversed(range(DEPTH)):
        sv = saved[l]
        dgu = _bwd_down(dx16, wdown, sv["gu"], l, f"down_bwd{l}")
        gw_down = _mm_tn(sv["act"], dx16, l, gw_down, f"dw_down{l}")
        gw_gu = _mm_tn(sv["h2"], dgu, l, gw_gu, f"dw_gate_up{l}", split4=True)
        exchange_gu = exchange_down = None
        if l == 0:
            g_f = [gw_gu, gw_down.reshape(2, 4, D_FF // 4, D)]
            dx1, dx1_16, dg, got_f = _mm_nt_normbwd(dgu, gu4, l, sv["x1"], g_ffn, dx, 1, f"gate_up_bwd{l}",
                                                    rider=_swap_rider(g_f))
            pairs_f = [_pair_sum(g, o, kc, f"rs_pair_sum_{n}") for g, o, n in zip(g_f, got_f, FFN)]
            exchange_gu, exchange_down = _exchange_rider([pairs_f[0][1]]), _exchange_rider([pairs_f[1][1]])
        else:
            dx1, dx1_16, dg = _mm_nt_normbwd(dgu, gu4, l, sv["x1"], g_ffn, dx, 1, f"gate_up_bwd{l}")
        G["ffn_norm"][l] = dg[0]
        gw_o = _mm_tn(sv["mixed"], dx1_16, l, gw_o, f"dw_o{l}")
        dycat, dg = _mm_nt_normbwd(dx1_16, wo.reshape(2, 1, D, D), l, sv["ycat"], g_mix, None, 4, f"mix_bwd{l}")
        G["mix_norm"][l] = dg[0]

        proj = sv["proj"]
        delta = _mla_delta(dycat, sv["ycat"], f"mla_delta{l}")
        dq, dk, dv, got3_gu = _mla_attn_bwd(sv["q"], sv["k"], sv["kt"], sv["v"], dycat, sv["lse"], delta, exchange_gu,
                                            f"mla_attn_bwd{l}")
        dcq, dckv, dkr, dwuq, dwk, dwv, dgq, dgkv = _mla_prep_bwd(
            dq, dk, dv, proj, g_q, g_kv, wuq_p, wk_p, wv_p, tabs, l, f"mla_prep_bwd{l}")
        dgb, dgc, duc, dcw = _conv_bwd(proj, conv8, dycat, l, f"conv_bwd{l}")
        dup, dpw, dps = _pool_bwd(proj, pwd, g_ps, dycat, l, f"pool_bwd{l}")
        if l == 0:
            dqs, dks, dvs, dsink, got3_down = _swa_bwd(proj, swa_sinks, dycat, l, f"swa_bwd{l}", rider=exchange_down)
            got3_f = got3_gu + got3_down
        else:
            dqs, dks, dvs, dsink = _swa_bwd(proj, swa_sinks, dycat, l, f"swa_bwd{l}")
        dproj = jnp.concatenate([dcq, dckv, dkr, dgb, dgc, duc, dup, dqs, dks, dvs], axis=1)
        gw_in = _mm_tn(sv["h"], dproj, l, gw_in, f"dw_in{l}")
        G["w_uq"][l] = jnp.concatenate(_unpad_heads(dwuq, UQ_SRC, ZERO_OFF), axis=1)
        kn, vv = _unpad_heads(dwk, KN_SRC, ZERO_OFF), _unpad_heads(dwv, V_SRC, V_OFF)
        G["w_ukv"][l] = jnp.concatenate([t for h in range(HEADS) for t in (kn[h], vv[h])], axis=1)
        dx, dx16, dg = _mm_nt_normbwd(dproj, win_p.reshape(2, 1, D, D_INP), l, sv["x0"], g_attn, dx1, 1, f"in_proj_bwd{l}")
        G["attn_norm"][l] = dg[0]
        G["mla_q_norm"][l] = dgq[0]
        G["mla_kv_norm"][l] = dgkv[0]
        G["conv_w"][l] = dcw[0:3]
        G["pool_w"][l] = jnp.stack([dpw[0:64, 0:64], dpw[64:128, 64:128], dpw[128:192, 0:64], dpw[192:256, 64:128]])
        G["pool_scale"][l] = dps[0]
        G["swa_sinks"][l] = dsink[0, 0:4]
    grad_x = dx[None]
    Gl = {n: jnp.stack(G[n]) for n in TINY + REPL if n != "final_norm"}
    Gl["final_norm"] = dg_final[0]

    us_f = [_chip_sum(p[0], o3, kc, f"rs_chip_sum_{n}") for p, o3, n in zip(pairs_f, got3_f, FFN)]
    gsum_f = _join_layers(us_f, "rs_join_cores_ffn")
    g_r = [_cols_split(_unpad_w_in(gw_in)), gw_o.reshape(2, 4, D // 4, D), _cols_split(jnp.stack(G["w_uq"])),
           _cols_split(jnp.stack(G["w_ukv"]))]
    got_r = _ride_alone(_swap_rider(g_r), "rs_swap_cores")
    pairs_r = [_pair_sum(g, o, kc, f"rs_pair_sum_{n}") for g, o, n in zip(g_r, got_r, REST)]
    small = TINY + REPL
    full_shapes = [Gl[n].shape for n in small]
    summed, got3_r = _allsum_small(_pack([Gl[n] for n in small]), "allsum_small",
                                   rider=_exchange_rider([p[1] for p in pairs_r]))
    summed = _unpack(summed, full_shapes)
    us_r = [_chip_sum(p[0], o3, kc, f"rs_chip_sum_{n}") for p, o3, n in zip(pairs_r, got3_r, REST)]
    gsum_r = _join_layers(us_r, "rs_join_cores")
    res = {}
    for n, g in zip(BIG, gsum_f + gsum_r):
        d_, m_, v_ = _adamw(W[n], g, M1[n], V2[n], f"adamw_{n}")
        res["g", n], res["d", n], res["m", n], res["v", n] = g, d_, m_, v_

    gs = {}
    for n, g in zip(small, summed):
        if n in TINY:
            wdt = W[n].shape[2]
            g = lax.dynamic_slice_in_dim(g, chip * wdt, wdt, axis=2)
        gs[n] = g
    own_shapes = [W[n].shape for n in small]
    pk = lambda src: _pack([src[n] for n in small])[None]
    d_s, m_s, v_s = _adamw(pk(W), pk(gs), pk(M1), pk(V2), "adamw_small")
    for key, buf in (("d", d_s), ("m", m_s), ("v", v_s)):
        for n, a in zip(small, _unpack(buf[0], own_shapes)):
            res[key, n] = a
    for n in small:
        res["g", n] = gs[n]

    return (loss, grad_x, *[res["g", n] for n in ORDER], *[res["d", n] for n in ORDER],
            *[res["m", n] for n in ORDER], *[res["v", n] for n in ORDER])
```

```python
import math

import numpy as np
import jax
import jax.numpy as jnp
from jax import lax
from jax.experimental import pallas as pl
from jax.experimental.pallas import tpu as pltpu

F32, BF16 = jnp.float32, jnp.bfloat16
SDS = jax.ShapeDtypeStruct
BS = pl.BlockSpec
MESH = pl.DeviceIdType.MESH

D = 1024
DEPTH = 2
HEADS = 4
D_FF = 2816
D_INP = 2048
EPS = 1e-6
SWA_WINDOW = 128
BLK = 128
SLOPES = tuple(2.0 ** (-8.0 * (i + 1) / 4) for i in range(4))
QK_SCALE = 1.0 / math.sqrt(96)
SWA_SCALE = 1.0 / math.sqrt(64)
LR, B1, B2, ADAM_EPS, WD, STEP = 0.001, 0.9, 0.999, 1e-08, 0.01, 10

LANES = 1024
VMEM_LIMIT = 56 * 1024 * 1024
NEG_INF = float("-inf")

C_CQ, C_CKV, C_KR, C_GB, C_GC, C_UC, C_UP, C_QS, C_KS, C_VS = 0, 256, 384, 512, 768, 1024, 1280, 1536, 1792, 1920


def _params(ngrid):
    return pltpu.CompilerParams(dimension_semantics=("arbitrary",) * ngrid, vmem_limit_bytes=VMEM_LIMIT)


def _pc(body, *, name, grid, in_specs, out_specs, out_shape, scratch=(), aliases=None):
    return pl.pallas_call(
        body, name=name, grid=grid, in_specs=in_specs, out_specs=out_specs, out_shape=out_shape,
        scratch_shapes=scratch, input_output_aliases=aliases or {}, compiler_params=_params(len(grid)))


def _dot(a, b):
    return jnp.dot(a, b, preferred_element_type=F32)


def _dot_nt(a, b):
    return lax.dot_general(a, b, (((1,), (1,)), ((), ())), preferred_element_type=F32)


def _dot_tn(a, b):
    return lax.dot_general(a, b, (((0,), (0,)), ((), ())), preferred_element_type=F32)


def _tile(n, cap):
    if n <= cap:
        return n
    t = cap - cap % 128
    while n % t:
        t -= 128
    return t


def _row_tile(a, b, cap=262144):
    bp = -(-b // 128) * 128
    best = None
    for t in range(8, a + 1, 8):
        if a % t == 0 and t * bp <= cap:
            best = t
    if best is None or (best < 64 and a * bp <= 2 * cap):
        return a
    return best


def _g3(a):
    return a.reshape(a.shape[0], 1, a.shape[1])


def _norm_mm(x, g3, l, w, wspec, N, tn, out_dtype, name, w_t=False):
    S, K = x.shape
    tm = min(1024 if out_dtype == BF16 else 512, S)

    def body(x_ref, g_ref, w_ref, y_ref, h_ref):
        @pl.when(pl.program_id(1) == 0)
        def _():
            xv = x_ref[...]
            r = lax.rsqrt(jnp.mean(xv * xv, axis=-1, keepdims=True) + EPS)
            h_ref[...] = (xv * r * g_ref[...]).astype(BF16)

        y_ref[...] = (_dot_nt if w_t else _dot)(h_ref[...], w_ref[...]).astype(out_dtype)

    return _pc(body, name=name, grid=(S // tm, N // tn),
               in_specs=[BS((tm, K), lambda i, j: (i, 0)), BS((None, 1, K), lambda i, j: (l, 0, 0)), wspec],
               out_specs=[BS((tm, tn), lambda i, j: (i, j)), BS((tm, K), lambda i, j: (i, 0))],
               out_shape=[SDS((S, N), out_dtype), SDS((S, K), BF16)])(x, g3, w)


def _wspec_in(l):
    return BS((None, D_INP, D), lambda i, j: (l, j, 0))


def _wspec_gu(l):
    return BS((None, None, D, 2 * D_FF // 4), lambda i, j: (l, j, 0, 0))


def _mix_out(x0, ya, yb, yc, yd, gmix3, wo, l, name):
    S = x0.shape[0]
    tm = min(512, S)

    def body(x_ref, ya_ref, yb_ref, yc_ref, yd_ref, g_ref, w_ref, x1_ref, ycat_ref, mixed_ref):
        groups = [ya_ref[...], yb_ref[...], yc_ref[...], yd_ref[...]]
        for gi, yg in enumerate(groups):
            sl = slice(gi * 256, (gi + 1) * 256)
            r = lax.rsqrt(jnp.mean(yg * yg, axis=-1, keepdims=True) + EPS)
            ycat_ref[:, sl] = yg
            mixed_ref[:, sl] = (yg * r * g_ref[:, sl]).astype(BF16)
        x1_ref[...] = x_ref[...] + _dot(mixed_ref[...], w_ref[...])

    row = lambda w: BS((tm, w), lambda i: (i, 0))
    return _pc(body, name=name, grid=(S // tm,),
               in_specs=[row(D), row(256), row(256), row(256), row(256), BS((None, 1, D), lambda i: (l, 0, 0)),
                         BS((None, D, D), lambda i: (l, 0, 0))],
               out_specs=[row(D), row(D), row(D)],
               out_shape=[SDS((S, D), F32), SDS((S, D), F32), SDS((S, D), BF16)])(x0, ya, yb, yc, yd, gmix3, wo)


def _swiglu_mm_res(x1, gu, wdown, l, name):
    S = x1.shape[0]
    tm = min(256, S)

    def body(x_ref, gate_ref, up_ref, w_ref, x2_ref, act_ref):
        gt = gate_ref[...].astype(F32)
        act = (gt * pl.reciprocal(1.0 + jnp.exp(-gt), approx=True) * up_ref[...].astype(F32)).astype(BF16)
        act_ref[...] = act
        x2_ref[...] = x_ref[...] + _dot(act, w_ref[...])

    return _pc(body, name=name, grid=(S // tm,),
               in_specs=[BS((tm, D), lambda i: (i, 0)), BS((tm, D_FF), lambda i: (i, 0)),
                         BS((tm, D_FF), lambda i: (i, 1)), BS((None, D_FF, D), lambda i: (l, 0, 0))],
               out_specs=[BS((tm, D), lambda i: (i, 0)), BS((tm, D_FF), lambda i: (i, 0))],
               out_shape=[SDS((S, D), F32), SDS((S, D_FF), BF16)])(x1, gu, gu, wdown)


def _loss_head(x, g, tgt, name):
    S = x.shape[0]
    tm = min(512, S)

    def body(x_ref, g_ref, t_ref, dx_ref, dx16_ref, dg_ref, loss_ref):
        @pl.when(pl.program_id(0) == 0)
        def _():
            dg_ref[...] = jnp.zeros_like(dg_ref)
            loss_ref[...] = jnp.zeros_like(loss_ref)

        xv = x_ref[...]
        r = lax.rsqrt(jnp.mean(xv * xv, axis=-1, keepdims=True) + EPS)
        xh = xv * r
        gv = g_ref[...]
        diff = xh * gv - t_ref[...]
        loss_ref[...] += jnp.sum(diff * diff)
        dy = diff * (1.0 / D)
        dg_ref[...] += jnp.sum(dy * xh, axis=0, keepdims=True)
        dxh = dy * gv
        dx = r * (dxh - xh * jnp.mean(dxh * xh, axis=-1, keepdims=True))
        dx_ref[...] = dx
        dx16_ref[...] = dx.astype(BF16)

    row = BS((tm, D), lambda i: (i, 0))
    return _pc(body, name=name, grid=(S // tm,),
               in_specs=[row, BS((1, D), lambda i: (0, 0)), row],
               out_specs=[row, row, BS((8, D), lambda i: (0, 0)), BS((8, 128), lambda i: (0, 0))],
               out_shape=[SDS((S, D), F32), SDS((S, D), BF16), SDS((8, D), F32), SDS((8, 128), F32)])(x, g, tgt)


def _mm_tn(a, b, l, prev, name, split4=False):
    S, Ka = a.shape
    N = b.shape[1]
    if split4:
        ta, tn = _tile(Ka, 256), N // 4
        out_shape = SDS((2, 4, Ka, tn), F32)
        out_spec = BS((None, None, ta, tn), lambda j, i: (l, j, i, 0))
    else:
        ta, tn = _tile(Ka, 512), _tile(N, 1024)
        out_shape = SDS((2, Ka, N), F32)
        out_spec = BS((None, ta, tn), lambda j, i: (l, i, j))

    def body(a_ref, b_ref, *rest):
        rest[-1][...] = _dot_tn(a_ref[...], b_ref[...])

    in_specs = [BS((S, ta), lambda j, i: (0, i)), BS((S, tn), lambda j, i: (0, j))]
    args = [a, b]
    if prev is not None:
        in_specs.append(BS(memory_space=pl.ANY))
        args.append(prev)
    return _pc(body, name=name, grid=(N // tn, Ka // ta), in_specs=in_specs, out_specs=out_spec, out_shape=out_shape,
               aliases={2: 0} if prev is not None else None)(*args)


def _bwd_down(dx16, wdown, gu, l, name):
    S = dx16.shape[0]
    tm = min(256, S)

    def body(dx_ref, w_ref, gate_ref, up_ref, dgu_ref):
        dxv = dx_ref[...]
        for c0 in range(0, D_FF, 256):
            cs = slice(c0, c0 + 256)
            dact = _dot_nt(dxv, w_ref[cs, :])
            gt = gate_ref[:, cs].astype(F32)
            sg = pl.reciprocal(1.0 + jnp.exp(-gt), approx=True)
            dgu_ref[:, cs] = (dact * up_ref[:, cs].astype(F32) * (sg * (1.0 + gt * (1.0 - sg)))).astype(BF16)
            dgu_ref[:, D_FF + c0:D_FF + c0 + 256] = (dact * (gt * sg)).astype(BF16)

    return _pc(body, name=name, grid=(S // tm,),
               in_specs=[BS((tm, D), lambda i: (i, 0)), BS((None, D_FF, D), lambda i: (l, 0, 0)),
                         BS((tm, D_FF), lambda i: (i, 0)), BS((tm, D_FF), lambda i: (i, 1))],
               out_specs=BS((tm, 2 * D_FF), lambda i: (i, 0)),
               out_shape=SDS((S, 2 * D_FF), BF16))(dx16, wdown, gu, gu)


def _mm_nt_normbwd(dy, w4, l, x, g3, dres, ngroups, name, rider=None, w_t=False):
    S, K = dy.shape
    nk, kc = w4.shape[1], w4.shape[2 if w_t else 3]
    mm = _dot if w_t else _dot_nt
    tm = min(512, S)
    gw = D // ngroups
    has_res = dres is not None
    nr = rider.n if rider else 0
    n_in, n_out = 4 + has_res, 2 + has_res

    def body(*refs):
        dy_ref, w_ref, x_ref, g_ref = refs[:4]
        res_ref = refs[4] if has_res else None
        outs = refs[n_in + nr:n_in + nr + n_out]
        dx_ref, dg_ref = outs[0], outs[-1]
        dx16_ref = outs[1] if has_res else None
        r_io = (refs[n_in:n_in + nr], refs[n_in + nr + n_out:n_in + 2 * nr + n_out], refs[n_in + 2 * nr + n_out:])
        if rider:
            pl.when(pl.program_id(0) == 0)(lambda: rider.start(*r_io))

        @pl.when(pl.program_id(0) == 0)
        def _():
            dg_ref[...] = jnp.zeros_like(dg_ref)

        dh = mm(dy_ref[:, 0:kc], w_ref[0])
        for k in range(1, nk):
            dh = dh + mm(dy_ref[:, k * kc:(k + 1) * kc], w_ref[k])
        for gi in range(ngroups):
            sl = slice(gi * gw, (gi + 1) * gw)
            xg = x_ref[:, sl]
            r = lax.rsqrt(jnp.mean(xg * xg, axis=-1, keepdims=True) + EPS)
            xh = xg * r
            dhg = dh[:, sl]
            dg_ref[:, sl] += jnp.sum(dhg * xh, axis=0, keepdims=True)
            dxh = dhg * g_ref[:, sl]
            dxg = r * (dxh - xh * jnp.mean(dxh * xh, axis=-1, keepdims=True))
            if has_res:
                dxg = dxg + res_ref[:, sl]
                dx16_ref[:, sl] = dxg.astype(BF16)
            dx_ref[:, sl] = dxg
        if rider:
            pl.when(pl.program_id(0) == S // tm - 1)(lambda: rider.wait(*r_io))

    row = BS((tm, D), lambda i: (i, 0))
    in_specs = [BS((tm, K), lambda i: (i, 0)),
                BS((None,) + tuple(w4.shape[1:]), lambda i: (l, 0, 0, 0), pipeline_mode=pl.Buffered(1)), row,
                BS((None, 1, D), lambda i: (l, 0, 0))]
    args = [dy, w4, x, g3]
    out_specs, out_shape = [row], [SDS((S, D), F32)]
    if has_res:
        in_specs.append(row)
        args.append(dres)
        out_specs.append(row)
        out_shape.append(SDS((S, D), BF16))
    out_specs.append(BS((8, D), lambda i: (0, 0)))
    out_shape.append(SDS((8, D), F32))
    if not rider:
        return _pc(body, name=name, grid=(S // tm,), in_specs=in_specs, out_specs=out_specs, out_shape=out_shape)(*args)
    out = pl.pallas_call(body, name=name, grid=(S // tm,), in_specs=in_specs + [ANY] * nr,
                         out_specs=out_specs + [ANY] * nr, out_shape=out_shape + rider.out_shape,
                         scratch_shapes=rider.scratch(),
                         compiler_params=pltpu.CompilerParams(dimension_semantics=("arbitrary",),
                                                              vmem_limit_bytes=VMEM_LIMIT, has_side_effects=True))(
        *args, *rider.arrs)
    return (*out[:n_out], list(out[n_out:]))


def _rope(x, c, s1, s2):
    return x * c + pltpu.roll(x, 112, axis=1) * s1 + pltpu.roll(x, 16, axis=1) * s2


def _rope_t(dy, c, s1, s2):
    return dy * c + pltpu.roll(dy * s1, 16, axis=1) + pltpu.roll(dy * s2, 112, axis=1)


def _mla_prep(proj, gq3, gkv3, wuq, wk, wv, tabs, l, name):
    S = proj.shape[0]
    tm = min(512, S)
    tc, ts1, ts2 = tabs

    def body(cq_ref, ckv_ref, kr_ref, gq_ref, gkv_ref, wuq_ref, wk_ref, wv_ref, c_ref, s1_ref, s2_ref,
             q_ref, k_ref, v_ref, kt_ref, vt_ref):
        c, s1, s2 = c_ref[...], s1_ref[...], s2_ref[...]
        cq = cq_ref[...]
        rq = lax.rsqrt(jnp.mean(cq * cq, axis=-1, keepdims=True) + EPS)
        qa = _dot((cq * rq * gq_ref[...]).astype(BF16), wuq_ref[...])
        ckv = ckv_ref[...]
        rkv = lax.rsqrt(jnp.mean(ckv * ckv, axis=-1, keepdims=True) + EPS)
        ckvn = (ckv * rkv * gkv_ref[...]).astype(BF16)
        ka = _dot(ckvn, wk_ref[...])
        va = _dot(ckvn, wv_ref[...])
        v_ref[...] = va.astype(BF16)
        vt_ref[...] = va.T.astype(BF16)
        krr = _rope(kr_ref[...], c, s1, s2)
        for h in range(HEADS):
            sl = slice(h * 128, (h + 1) * 128)
            q_ref[:, sl] = (_rope(qa[:, sl], c, s1, s2) * QK_SCALE).astype(BF16)
            kh = ka[:, sl] + krr
            k_ref[:, sl] = kh.astype(BF16)
            kt_ref[sl, :] = kh.T.astype(BF16)

    lay = lambda a, b: BS((None, a, b), lambda i: (l, 0, 0))
    tab = BS((tm, 128), lambda i: (i, 0))
    return _pc(body, name=name, grid=(S // tm,),
               in_specs=[BS((tm, 256), lambda i: (i, 0)), BS((tm, 128), lambda i: (i, 2)), BS((tm, 128), lambda i: (i, 3)),
                         lay(1, 256), lay(1, 128), lay(256, 512), lay(128, 512), lay(128, 512), tab, tab, tab],
               out_specs=[BS((tm, 512), lambda i: (i, 0))] * 3 + [BS((512, tm), lambda i: (0, i))] * 2,
               out_shape=[SDS((S, 512), BF16)] * 3 + [SDS((512, S), BF16)] * 2)(
        proj, proj, proj, gq3, gkv3, wuq, wk, wv, tc, ts1, ts2)


def _mla_prep_bwd(dq, dk, dv, proj, gq3, gkv3, wuq, wk, wv, tabs, l, name):
    S = proj.shape[0]
    tm = min(512, S)
    tc, ts1, ts2 = tabs

    def body(dq_ref, dk_ref, dv_ref, cq_ref, ckv_ref, gq_ref, gkv_ref, wuq_ref, wk_ref, wv_ref, c_ref, s1_ref, s2_ref,
             dcq_ref, dckv_ref, dkr_ref, dwuq_ref, dwk_ref, dwv_ref, dgq_ref, dgkv_ref):
        @pl.when(pl.program_id(0) == 0)
        def _():
            for r in (dwuq_ref, dwk_ref, dwv_ref, dgq_ref, dgkv_ref):
                r[...] = jnp.zeros_like(r)

        c, s1, s2 = c_ref[...], s1_ref[...], s2_ref[...]
        dqp = jnp.concatenate(
            [_rope_t(dq_ref[h * 128:(h + 1) * 128, :].T * QK_SCALE, c, s1, s2) for h in range(HEADS)], axis=1).astype(BF16)
        cq = cq_ref[...]
        rq = lax.rsqrt(jnp.mean(cq * cq, axis=-1, keepdims=True) + EPS)
        cqh = cq * rq
        gq_v = gq_ref[...]
        dwuq_ref[...] += _dot_tn((cqh * gq_v).astype(BF16), dqp)
        dcqn = _dot_nt(dqp, wuq_ref[...])
        dgq_ref[...] += jnp.sum(dcqn * cqh, axis=0, keepdims=True)
        dxh = dcqn * gq_v
        dcq_ref[...] = (rq * (dxh - cqh * jnp.mean(dxh * cqh, axis=-1, keepdims=True))).astype(BF16)

        dkb = dk_ref[...].astype(BF16)
        dvb = dv_ref[...].astype(BF16)
        ckv = ckv_ref[...]
        rkv = lax.rsqrt(jnp.mean(ckv * ckv, axis=-1, keepdims=True) + EPS)
        ckh = ckv * rkv
        gkv_v = gkv_ref[...]
        ckvn = (ckh * gkv_v).astype(BF16)
        dwk_ref[...] += _dot_tn(ckvn, dkb)
        dwv_ref[...] += _dot_tn(ckvn, dvb)
        dckvn = _dot_nt(dkb, wk_ref[...]) + _dot_nt(dvb, wv_ref[...])
        dgkv_ref[...] += jnp.sum(dckvn * ckh, axis=0, keepdims=True)
        dyh = dckvn * gkv_v
        dckv_ref[...] = (rkv * (dyh - ckh * jnp.mean(dyh * ckh, axis=-1, keepdims=True))).astype(BF16)
        dks = dk_ref[:, 0:128] + dk_ref[:, 128:256] + dk_ref[:, 256:384] + dk_ref[:, 384:512]
        dkr_ref[...] = _rope_t(dks, c, s1, s2).astype(BF16)

    full = lambda a, b: BS((a, b), lambda i: (0, 0))
    lay = lambda a, b: BS((None, a, b), lambda i: (l, 0, 0))
    tab = BS((tm, 128), lambda i: (i, 0))
    row = lambda w: BS((tm, w), lambda i: (i, 0))
    return _pc(body, name=name, grid=(S // tm,),
               in_specs=[BS((512, tm), lambda i: (0, i)), row(512), row(512), BS((tm, 256), lambda i: (i, 0)),
                         BS((tm, 128), lambda i: (i, 2)),
                         lay(1, 256), lay(1, 128), lay(256, 512), lay(128, 512), lay(128, 512), tab, tab, tab],
               out_specs=[row(256), row(128), row(128), full(256, 512), full(128, 512), full(128, 512),
                          full(8, 256), full(8, 128)],
               out_shape=[SDS((S, 256), BF16), SDS((S, 128), BF16), SDS((S, 128), BF16), SDS((256, 512), F32),
                          SDS((128, 512), F32), SDS((128, 512), F32), SDS((8, 256), F32), SDS((8, 128), F32)])(
        dq, dk, dv, proj, proj, gq3, gkv3, wuq, wk, wv, tc, ts1, ts2)


def _causal_steps(n, q_outer):
    if q_outer:
        pairs = [(i, j) for i in range(n) for j in range(i + 1)]
    else:
        pairs = [(i, j) for j in range(n) for i in range(j, n)]
    return jnp.asarray([p[0] for p in pairs], jnp.int32), jnp.asarray([p[1] for p in pairs], jnp.int32)


def _mla_attn(q, k, vt, gts, layer, name):
    S = q.shape[0]
    t = min(512, S)
    n = S // t
    ng = len(gts)

    qi, kj = _causal_steps(n, True)
    last = qi.shape[0] - 1

    def body(qi_ref, kj_ref, q_ref, k_ref, vt_ref, *rest):
        (ya_ref, lse_ref), g_refs = rest[ng:ng + 2], rest[ng + 2:2 * ng + 2]
        m_sc, l_sc, acc_sc = rest[2 * ng + 2:2 * ng + 5]
        i, j = qi_ref[pl.program_id(1)], kj_ref[pl.program_id(1)]
        if ng:
            phases = _gather_phases(g_refs, [g.shape for g in gts], rest[2 * ng + 5], rest[2 * ng + 6], layer)
            for ph, (pp, ss) in zip(phases[:2], ((0, 0), (1, 0))):
                pl.when((pl.program_id(0) == pp) & (pl.program_id(1) == ss))(ph)

        @pl.when(j == 0)
        def _():
            m_sc[...] = jnp.full_like(m_sc, NEG_INF)
            l_sc[...] = jnp.zeros_like(l_sc)
            acc_sc[...] = jnp.zeros_like(acc_sc)

        def step(masked):
            for hh in range(2):
                sl = slice(hh * 128, (hh + 1) * 128)
                st = _dot_nt(k_ref[:, sl], q_ref[:, sl])
                if masked:
                    key = lax.broadcasted_iota(jnp.int32, (t, t), 0)
                    qry = lax.broadcasted_iota(jnp.int32, (t, t), 1)
                    st = jnp.where(key <= qry, st, NEG_INF)
                m_prev = m_sc[hh]
                m_new = jnp.maximum(m_prev, jnp.max(st, axis=0, keepdims=True))
                p = jnp.exp(st - m_new)
                alpha = jnp.exp(m_prev - m_new)
                l_sc[hh] = alpha * l_sc[hh] + jnp.sum(p, axis=0, keepdims=True)
                acc_sc[hh] = alpha * acc_sc[hh] + _dot(vt_ref[sl, :], p.astype(BF16))
                m_sc[hh] = m_new

        @pl.when(j < i)
        def _():
            step(False)

        @pl.when(j == i)
        def _():
            step(True)
            ya_ref[...] = (acc_sc[0] / l_sc[0] + acc_sc[1] / l_sc[1]).T
            for hh in range(2):
                lse_ref[hh] = m_sc[hh] + jnp.log(l_sc[hh])

        if ng:
            pl.when((pl.program_id(0) == 1) & (pl.program_id(1) == last))(phases[2])

    gs = pltpu.PrefetchScalarGridSpec(
        num_scalar_prefetch=2, grid=(2, qi.shape[0]),
        in_specs=[BS((t, 256), lambda p, s, qi, kj: (qi[s], p)), BS((t, 256), lambda p, s, qi, kj: (kj[s], p)),
                  BS((256, t), lambda p, s, qi, kj: (p, kj[s]))] + [ANY] * ng,
        out_specs=[BS((t, 128), lambda p, s, qi, kj: (qi[s], p)), BS((2, 1, t), lambda p, s, qi, kj: (p, 0, qi[s]))]
        + [ANY] * ng,
        scratch_shapes=[pltpu.VMEM((2, 1, t), F32), pltpu.VMEM((2, 1, t), F32), pltpu.VMEM((2, 128, t), F32)]
        + ([pltpu.SemaphoreType.DMA((7 * ng,)), pltpu.SemaphoreType.DMA((7 * ng,))] if ng else []))
    out = pl.pallas_call(body, name=name, grid_spec=gs,
                         out_shape=[SDS((S, 256), F32), SDS((HEADS, 1, S), F32)] + [SDS(g.shape, g.dtype) for g in gts],
                         input_output_aliases={5 + m: 2 + m for m in range(ng)},
                         compiler_params=pltpu.CompilerParams(dimension_semantics=("arbitrary",) * 2,
                                                              vmem_limit_bytes=VMEM_LIMIT, has_side_effects=bool(ng)))(
        qi, kj, q, k, vt, *gts)
    return out[0], out[1], list(out[2:])


def _mla_delta(dycat, ya, name):
    S = ya.shape[0]
    t = min(512, S)

    def body(do_ref, ya_ref, d_ref):
        prod = do_ref[...] * ya_ref[...]
        for p in range(2):
            pt = prod[:, p * 128:(p + 1) * 128].T
            d_ref[2 * p] = jnp.sum(pt[0:64, :], axis=0, keepdims=True)
            d_ref[2 * p + 1] = jnp.sum(pt[64:128, :], axis=0, keepdims=True)

    return _pc(body, name=name, grid=(S // t,),
               in_specs=[BS((t, 256), lambda i: (i, 0)), BS((t, 256), lambda i: (i, 0))],
               out_specs=BS((HEADS, 1, t), lambda i: (0, 0, i)), out_shape=SDS((HEADS, 1, S), F32))(dycat, ya)


def _mla_attn_bwd(q, k, kt, v, dya, lse, delta, rider, name):
    S = q.shape[0]
    t = min(512, S)
    n = S // t
    nr = rider.n if rider else 0

    qi, kj = _causal_steps(n, False)
    last = qi.shape[0] - 1

    def body(qi_ref, kj_ref, q_ref, k_ref, kt_ref, v_ref, do_ref, lse_ref, delta_ref, *rest):
        dqt_ref, dk_ref, dv_ref = rest[nr:nr + 3]
        r_io = (rest[:nr], rest[nr + 3:2 * nr + 3], rest[2 * nr + 3:])
        i, j = qi_ref[pl.program_id(1)], kj_ref[pl.program_id(1)]
        if rider:
            pl.when((pl.program_id(0) == 0) & (pl.program_id(1) == 0))(lambda: rider.start(*r_io))

        @pl.when(pl.program_id(1) == 0)
        def _():
            dqt_ref[...] = jnp.zeros_like(dqt_ref)

        @pl.when(i == j)
        def _():
            dk_ref[...] = jnp.zeros_like(dk_ref)
            dv_ref[...] = jnp.zeros_like(dv_ref)

        def step(masked):
            dob = do_ref[...].astype(BF16)
            cols = pl.ds(pl.multiple_of(i * t, t), t)
            for hh in range(2):
                sl = slice(hh * 128, (hh + 1) * 128)
                qv = q_ref[:, sl]
                p = jnp.exp(_dot_nt(k_ref[:, sl], qv) - lse_ref[hh])
                if masked:
                    key = lax.broadcasted_iota(jnp.int32, (t, t), 0)
                    qry = lax.broadcasted_iota(jnp.int32, (t, t), 1)
                    p = jnp.where(key <= qry, p, 0.0)
                dv_ref[:, sl] += _dot(p.astype(BF16), dob)
                ds = (p * (_dot_nt(v_ref[:, sl], dob) - delta_ref[hh])).astype(BF16)
                dk_ref[:, sl] += _dot(ds, qv)
                dqt_ref[sl, cols] += _dot(kt_ref[sl, :], ds)

        @pl.when(i > j)
        def _():
            step(False)

        @pl.when(i == j)
        def _():
            step(True)

        if rider:
            pl.when((pl.program_id(0) == 1) & (pl.program_id(1) == last))(lambda: rider.wait(*r_io))

    qs = BS((t, 256), lambda p, s, qi, kj: (qi[s], p))
    ks = BS((t, 256), lambda p, s, qi, kj: (kj[s], p))
    rowv = BS((2, 1, t), lambda p, s, qi, kj: (p, 0, qi[s]))
    gs = pltpu.PrefetchScalarGridSpec(
        num_scalar_prefetch=2, grid=(2, qi.shape[0]),
        in_specs=[qs, ks, BS((256, t), lambda p, s, qi, kj: (p, kj[s])), ks,
                  BS((t, 128), lambda p, s, qi, kj: (qi[s], p)), rowv, rowv] + [ANY] * nr,
        out_specs=[BS((256, S), lambda p, s, qi, kj: (p, 0)), ks, ks] + [ANY] * nr,
        scratch_shapes=rider.scratch() if rider else [])
    out = pl.pallas_call(body, name=name, grid_spec=gs,
                         out_shape=[SDS((512, S), F32), SDS((S, 512), F32), SDS((S, 512), F32)]
                         + (rider.out_shape if rider else []),
                         compiler_params=pltpu.CompilerParams(dimension_semantics=("arbitrary",) * 2,
                                                              vmem_limit_bytes=VMEM_LIMIT, has_side_effects=bool(rider)))(
        qi, kj, q, k, kt, v, dya, lse, delta, *(rider.arrs if rider else []))
    return out[0], out[1], out[2], list(out[3:])


def _swa_scores(qm, kk, valid, distf, slope, sink):
    sc = _dot_nt(qm, kk) * SWA_SCALE
    sc = jnp.where(valid, sc - slope * distf, NEG_INF)
    m = jnp.maximum(jnp.max(sc, axis=-1, keepdims=True), sink)
    e = jnp.exp(sc - m)
    esink = jnp.exp(sink - m)
    den = jnp.sum(e, axis=-1, keepdims=True) + esink
    return e / den, esink / den


def _swa_masks():
    r = lax.broadcasted_iota(jnp.int32, (BLK, 2 * BLK), 0)
    c = lax.broadcasted_iota(jnp.int32, (BLK, 2 * BLK), 1)
    dist = r + BLK - c
    return (dist >= 0) & (dist < SWA_WINDOW), c >= BLK, dist.astype(F32)


def _to_half(xb, pos, b):
    return xb if pos == b else pltpu.roll(xb, 64, axis=1)


def _swa(proj, sinks, l, name):
    S = proj.shape[0]
    nb = S // BLK

    def body(q_ref, k_ref, v_ref, sink_ref, o_ref, kp, vp):
        kp[0:BLK, :] = jnp.zeros((BLK, 128), BF16)
        vp[0:BLK, :] = jnp.zeros((BLK, 128), BF16)
        kp[BLK:, :] = k_ref[...].astype(BF16)
        vp[BLK:, :] = v_ref[...].astype(BF16)
        lo = lax.broadcasted_iota(jnp.int32, (BLK, 128), 1) < 64
        band, cur, distf = _swa_masks()

        def blk(i, carry):
            st = pl.multiple_of(i * BLK, BLK)
            kk = kp[pl.ds(st, 2 * BLK), :]
            vv = vp[pl.ds(st, 2 * BLK), :]
            valid = band & (cur | (i > 0))
            for b in range(2):
                half = lo if b == 0 else ~lo
                qb = q_ref[pl.ds(st, BLK), b * 128:(b + 1) * 128]
                outs = []
                for pos in range(2):
                    h = 2 * b + pos
                    qm = jnp.where(half, _to_half(qb, pos, b), 0.0).astype(BF16)
                    p, _ = _swa_scores(qm, kk, valid, distf, SLOPES[h], sink_ref[l, h])
                    outs.append(_to_half(_dot(p.astype(BF16), vv), pos, b))
                o_ref[pl.ds(st, BLK), b * 128:(b + 1) * 128] = jnp.where(lo, outs[0], outs[1])
            return carry

        lax.fori_loop(0, nb, blk, 0, unroll=2)

    return _pc(body, name=name, grid=(1,),
               in_specs=[BS((S, 256), lambda i: (0, C_QS // 256)), BS((S, 128), lambda i: (0, C_KS // 128)),
                         BS((S, 128), lambda i: (0, C_VS // 128)), BS(memory_space=pltpu.SMEM)],
               out_specs=BS((S, 256), lambda i: (0, 0)),
               out_shape=SDS((S, 256), F32),
               scratch=[pltpu.VMEM((S + BLK, 128), BF16), pltpu.VMEM((S + BLK, 128), BF16)])(proj, proj, proj, sinks)


def _swa_bwd(proj, sinks, dyd, l, name, rider=None):
    S = proj.shape[0]
    nb = S // BLK
    nr = rider.n if rider else 0

    def body(q_ref, k_ref, v_ref, sink_ref, do_ref, *rest):
        dq_ref, dk_ref, dv_ref, dsink_ref = rest[nr:nr + 4]
        kp, vp, dkp, dvp = rest[2 * nr + 4:2 * nr + 8]
        r_io = (rest[:nr], rest[nr + 4:2 * nr + 4], rest[2 * nr + 8:])
        if rider:
            rider.start(*r_io)
        kp[0:BLK, :] = jnp.zeros((BLK, 128), BF16)
        vp[0:BLK, :] = jnp.zeros((BLK, 128), BF16)
        kp[BLK:, :] = k_ref[...].astype(BF16)
        vp[BLK:, :] = v_ref[...].astype(BF16)
        dkp[...] = jnp.zeros_like(dkp)
        dvp[...] = jnp.zeros_like(dvp)
        lo = lax.broadcasted_iota(jnp.int32, (BLK, 128), 1) < 64
        lane8 = lax.broadcasted_iota(jnp.int32, (8, 128), 1)
        band, cur, distf = _swa_masks()

        def blk(i, dsink):
            st = pl.multiple_of(i * BLK, BLK)
            kk = kp[pl.ds(st, 2 * BLK), :]
            vv = vp[pl.ds(st, 2 * BLK), :]
            valid = band & (cur | (i > 0))
            dkk = jnp.zeros((2 * BLK, 128), F32)
            dvv = jnp.zeros((2 * BLK, 128), F32)
            for b in range(2):
                half = lo if b == 0 else ~lo
                qb = q_ref[pl.ds(st, BLK), b * 128:(b + 1) * 128]
                dob = do_ref[pl.ds(st, BLK), b * 128:(b + 1) * 128]
                dqs = []
                for pos in range(2):
                    h = 2 * b + pos
                    qm = jnp.where(half, _to_half(qb, pos, b), 0.0).astype(BF16)
                    dom = jnp.where(half, _to_half(dob, pos, b), 0.0).astype(BF16)
                    p, psink = _swa_scores(qm, kk, valid, distf, SLOPES[h], sink_ref[l, h])
                    dp = _dot_nt(dom, vv)
                    dvv = dvv + _dot_tn(p.astype(BF16), dom)
                    delta = jnp.sum(p * dp, axis=-1, keepdims=True)
                    dsink = dsink + jnp.where(lane8 == h, -jnp.sum(psink * delta), 0.0)
                    dsc = (p * (dp - delta) * SWA_SCALE).astype(BF16)
                    dqs.append(_to_half(_dot(dsc, kk), pos, b))
                    dkk = dkk + _dot_tn(dsc, qm)
                dq_ref[pl.ds(st, BLK), b * 128:(b + 1) * 128] = jnp.where(lo, dqs[0], dqs[1]).astype(BF16)
            dkp[pl.ds(st, 2 * BLK), :] += dkk
            dvp[pl.ds(st, 2 * BLK), :] += dvv
            return dsink

        dsink_ref[...] = lax.fori_loop(0, nb, blk, jnp.zeros((8, 128), F32), unroll=2)
        dk_ref[...] = dkp[BLK:, :].astype(BF16)
        dv_ref[...] = dvp[BLK:, :].astype(BF16)
        if rider:
            rider.wait(*r_io)

    in_specs = [BS((S, 256), lambda i: (0, C_QS // 256)), BS((S, 128), lambda i: (0, C_KS // 128)),
                BS((S, 128), lambda i: (0, C_VS // 128)), BS(memory_space=pltpu.SMEM), BS((S, 256), lambda i: (0, 3))]
    out_specs = [BS((S, 256), lambda i: (0, 0)), BS((S, 128), lambda i: (0, 0)), BS((S, 128), lambda i: (0, 0)),
                 BS((8, 128), lambda i: (0, 0))]
    out_shape = [SDS((S, 256), BF16), SDS((S, 128), BF16), SDS((S, 128), BF16), SDS((8, 128), F32)]
    scratch = [pltpu.VMEM((S + BLK, 128), BF16), pltpu.VMEM((S + BLK, 128), BF16),
               pltpu.VMEM((S + BLK, 128), F32), pltpu.VMEM((S + BLK, 128), F32)]
    if not rider:
        return _pc(body, name=name, grid=(1,), in_specs=in_specs, out_specs=out_specs, out_shape=out_shape,
                   scratch=scratch)(proj, proj, proj, sinks, dyd)
    out = pl.pallas_call(body, name=name, grid=(1,), in_specs=in_specs + [ANY] * nr, out_specs=out_specs + [ANY] * nr,
                         out_shape=out_shape + rider.out_shape, scratch_shapes=scratch + rider.scratch(),
                         compiler_params=pltpu.CompilerParams(dimension_semantics=("arbitrary",),
                                                              vmem_limit_bytes=VMEM_LIMIT, has_side_effects=True))(
        proj, proj, proj, sinks, dyd, *rider.arrs)
    return (*out[:4], list(out[4:]))


def _down(x, k, t):
    return jnp.where(t >= k, pltpu.roll(x, k, axis=0), 0.0)


def _up(x, k, t):
    n = x.shape[0]
    return jnp.where(t < n - k, pltpu.roll(x, n - k, axis=0), 0.0)


def _conv(proj, w8, l, name):
    S = proj.shape[0]

    def body(gb_ref, gc_ref, u_ref, w_ref, y_ref):
        t = lax.broadcasted_iota(jnp.int32, (S, 128), 0)
        z = gc_ref[...] * u_ref[...]
        c = w_ref[2:3, :] * z + w_ref[1:2, :] * _down(z, 1, t) + w_ref[0:1, :] * _down(z, 2, t)
        y_ref[...] = gb_ref[...] * c

    col = lambda c0: BS((S, 128), lambda i: (0, c0 // 128 + i))
    return _pc(body, name=name, grid=(2,),
               in_specs=[col(C_GB), col(C_GC), col(C_UC), BS((None, 8, 128), lambda i: (l, 0, i))],
               out_specs=BS((S, 128), lambda i: (0, i)), out_shape=SDS((S, 256), F32))(proj, proj, proj, w8)


def _conv_bwd(proj, w8, dycat, l, name):
    S = proj.shape[0]

    def body(gb_ref, gc_ref, u_ref, w_ref, dy_ref, dgb_ref, dgc_ref, du_ref, dw_ref):
        t = lax.broadcasted_iota(jnp.int32, (S, 128), 0)
        gc, u = gc_ref[...], u_ref[...]
        z = gc * u
        z1, z2 = _down(z, 1, t), _down(z, 2, t)
        w0, w1, w2 = w_ref[0:1, :], w_ref[1:2, :], w_ref[2:3, :]
        dy = dy_ref[...]
        dgb_ref[...] = (dy * (w2 * z + w1 * z1 + w0 * z2)).astype(BF16)
        dc = dy * gb_ref[...]
        dz = w2 * dc + w1 * _up(dc, 1, t) + w0 * _up(dc, 2, t)
        dgc_ref[...] = (dz * u).astype(BF16)
        du_ref[...] = (dz * gc).astype(BF16)
        row = lax.broadcasted_iota(jnp.int32, (8, 128), 0)
        sums = [jnp.sum(dc * zz, axis=0, keepdims=True) for zz in (z2, z1, z)]
        dw_ref[...] = jnp.where(row == 0, sums[0], jnp.where(row == 1, sums[1], jnp.where(row == 2, sums[2], 0.0)))

    col = lambda c0: BS((S, 128), lambda i: (0, c0 // 128 + i))
    out = BS((S, 128), lambda i: (0, i))
    return _pc(body, name=name, grid=(2,),
               in_specs=[col(C_GB), col(C_GC), col(C_UC), BS((None, 8, 128), lambda i: (l, 0, i)), col(256)],
               out_specs=[out, out, out, BS((8, 128), lambda i: (0, i))],
               out_shape=[SDS((S, 256), BF16)] * 3 + [SDS((8, 256), F32)])(proj, proj, proj, w8, dycat)


def _pool_parts(u, t, first):
    lo = lax.broadcasted_iota(jnp.int32, u.shape, 1) < 64
    s2 = u + _down(u, 1, t)
    s4 = s2 + _down(s2, 2, t)
    s8 = s4 + _down(s4, 4, t)
    s16 = s8 + _down(s8, 8, t)
    win = jnp.where(lo, jnp.where(first, s2, s8), jnp.where(first, s4, s16))
    wv = jnp.where(lo, jnp.where(first, 2, 8), jnp.where(first, 4, 16))
    cnt = jnp.minimum(t + 1, wv).astype(F32)
    return win, cnt, lo


def _pool(proj, pwd, scale3, l, name):
    S = proj.shape[0]

    def body(u_ref, pw_ref, sc_ref, y_ref):
        t = lax.broadcasted_iota(jnp.int32, (S, 128), 0)
        u = u_ref[...]
        win, cnt, _ = _pool_parts(u, t, pl.program_id(0) == 0)
        pooled = win / cnt - u
        y_ref[...] = _dot(pooled.astype(BF16), pw_ref[...]) * sc_ref[...]

    return _pc(body, name=name, grid=(2,),
               in_specs=[BS((S, 128), lambda i: (0, C_UP // 128 + i)), BS((None, 128, 128), lambda i: (l, i, 0)),
                         BS((None, 1, 128), lambda i: (l, 0, i))],
               out_specs=BS((S, 128), lambda i: (0, i)), out_shape=SDS((S, 256), F32))(proj, pwd, scale3)


def _pool_bwd(proj, pwd, scale3, dycat, l, name):
    S = proj.shape[0]

    def body(u_ref, pw_ref, sc_ref, dy_ref, du_ref, dpw_ref, dsc_ref):
        t = lax.broadcasted_iota(jnp.int32, (S, 128), 0)
        first = pl.program_id(0) == 0
        u = u_ref[...]
        win, cnt, lo = _pool_parts(u, t, first)
        pooled = (win / cnt - u).astype(BF16)
        pw = pw_ref[...]
        dy = dy_ref[...]
        dsc_ref[...] = jnp.broadcast_to(jnp.sum(dy * _dot(pooled, pw), axis=0, keepdims=True), (8, 128))
        dmb = (dy * sc_ref[...]).astype(BF16)
        dpw_ref[...] = _dot_tn(pooled, dmb)
        dpooled = _dot_nt(dmb, pw)
        a1 = dpooled / cnt
        a2 = a1 + _up(a1, 1, t)
        a4 = a2 + _up(a2, 2, t)
        a8 = a4 + _up(a4, 4, t)
        a16 = a8 + _up(a8, 8, t)
        dwin = jnp.where(lo, jnp.where(first, a2, a8), jnp.where(first, a4, a16))
        du_ref[...] = (dwin - dpooled).astype(BF16)

    return _pc(body, name=name, grid=(2,),
               in_specs=[BS((S, 128), lambda i: (0, C_UP // 128 + i)), BS((None, 128, 128), lambda i: (l, i, 0)),
                         BS((None, 1, 128), lambda i: (l, 0, i)), BS((S, 128), lambda i: (0, 4 + i))],
               out_specs=[BS((S, 128), lambda i: (0, i)), BS((128, 128), lambda i: (i, 0)), BS((8, 128), lambda i: (0, i))],
               out_shape=[SDS((S, 256), BF16), SDS((256, 128), F32), SDS((8, 256), F32)])(proj, pwd, scale3, dycat)


def _adamw(w, g, m, v, name):
    n, a, b = w.shape
    tr = _row_tile(a, b)

    def body(w_ref, g_ref, m_ref, v_ref, d_ref, nm_ref, nv_ref):
        gv = g_ref[...]
        m_new = B1 * m_ref[...] + (1.0 - B1) * gv
        v_new = B2 * v_ref[...] + (1.0 - B2) * (gv * gv)
        m_hat = m_new / (1.0 - B1 ** STEP)
        v_hat = v_new / (1.0 - B2 ** STEP)
        d_ref[...] = -LR * (m_hat / (jnp.sqrt(v_hat) + ADAM_EPS) + WD * w_ref[...])
        nm_ref[...] = m_new
        nv_ref[...] = v_new

    sp = BS((None, tr, b), lambda i, t: (i, t, 0))
    return _pc(body, name=name, grid=(n, a // tr), in_specs=[sp] * 4, out_specs=[sp] * 3,
               out_shape=[SDS((n, a, b), F32)] * 3)(w, g, m, v)


def _prefetch_call(body, name, grid, in_specs, out_specs, out_shape):
    gs = pltpu.PrefetchScalarGridSpec(num_scalar_prefetch=1, grid=grid, in_specs=in_specs, out_specs=out_specs)
    return pl.pallas_call(body, name=name, grid_spec=gs, out_shape=out_shape, compiler_params=_params(len(grid)))


def _place(w, kc, dtype, name):
    _, a, b = w.shape

    def body(kc_ref, w_ref, o_ref):
        o_ref[...] = w_ref[...].astype(dtype)

    return _prefetch_call(body, name, (2,), [BS((None, a, b), lambda l, kc: (l, 0, 0))],
                          BS((None, None, a, b), lambda l, kc: (l, kc[0], 0, 0)), SDS((2, 4, a, b), dtype))(kc, w)


def _pair_sum(g, got, kc, name):
    _, _, a, b = g.shape
    tr = _row_tile(a, b)

    def body(kc_ref, a_ref, b_ref, t32_ref, t16_ref):
        s = a_ref[...] + b_ref[...]
        t32_ref[...] = s
        t16_ref[...] = s.astype(BF16)

    sp = BS((None, tr, b), lambda k, t, kc: (k, t, 0))
    return _prefetch_call(body, name, (4, a // tr),
                          [BS((None, None, tr, b), lambda k, t, kc: (kc[1], k, t, 0)), sp], [sp, sp],
                          [SDS((4, a, b), F32), SDS((4, a, b), BF16)])(kc, g, got)


def _chip_sum(t32, got3, kc, name):
    _, a, b = t32.shape
    tr = _row_tile(a, b)

    def body(kc_ref, a_ref, b_ref, u_ref):
        u_ref[...] = ((a_ref[...] + b_ref[0].astype(F32)) + b_ref[1].astype(F32)) + b_ref[2].astype(F32)

    return _prefetch_call(body, name, (a // tr,),
                          [BS((None, tr, b), lambda t, kc: (kc[0], t, 0)), BS((3, tr, b), lambda t, kc: (0, t, 0))],
                          BS((None, tr, b), lambda t, kc: (kc[1], t, 0)), SDS((2, a, b), F32))(kc, t32, got3)


def _me():
    return lax.axis_index("x"), lax.axis_index("y"), lax.axis_index("c")


def _other_chips(x, y):
    return [(1 - x, y), (x, 1 - y), (1 - x, 1 - y)]


ANY = BS(memory_space=pl.ANY)
COMM_PARAMS = pltpu.CompilerParams(has_side_effects=True)


def _gather(arrs, name):
    n = len(arrs)

    def body(*refs):
        for phase in _gather_phases(refs[n:2 * n], [a.shape for a in arrs], refs[2 * n], refs[2 * n + 1]):
            phase()

    return pl.pallas_call(body, name=name, out_shape=[SDS(a.shape, a.dtype) for a in arrs],
                          in_specs=[ANY] * n, out_specs=[ANY] * n, input_output_aliases={t: t for t in range(n)},
                          scratch_shapes=[pltpu.SemaphoreType.DMA((7 * n,)), pltpu.SemaphoreType.DMA((7 * n,))],
                          compiler_params=COMM_PARAMS)(*arrs)


def _gather_phases(outs, shapes, send_sems, recv_sems, layer=None):
    n = len(outs)
    split = [s[2] % 32 == 0 for s in shapes]

    def plan():
        x, y, c = _me()
        return (c if layer is None else layer), (x, y), (x, y, c), (x, y, 1 - c), _other_chips(x, y)

    def role(moving, fn):
        if layer is None:
            fn()
        else:
            c = lax.axis_index("c")
            pl.when((c == layer) if moving else (c != layer))(fn)

    def blk(t, chip, layer, half=None):
        r = outs[t].at[layer, 2 * chip[0] + chip[1]]
        if half is None:
            return r
        rows = shapes[t][2] // 2
        return r.at[pl.ds(half * rows, rows)]

    def copy(t, k, ref, to):
        return pltpu.make_async_remote_copy(src_ref=ref, dst_ref=ref, send_sem=send_sems.at[7 * t + k],
                                            recv_sem=recv_sems.at[7 * t + k], device_id=to, device_id_type=MESH)

    def own_sends(t):
        c, chip, me, sib, (xn, yn, dg) = plan()
        cps = [copy(t, 0, blk(t, chip, c), (*xn, c)), copy(t, 1, blk(t, chip, c), (*yn, c))]
        return cps if split[t] else cps + [copy(t, 2, blk(t, chip, c), (*dg, c))]

    def relays(t):
        c, chip, me, sib, (xn, yn, dg) = plan()
        after_x = [copy(t, 4, blk(t, xn, c), sib)]
        after_y = [copy(t, 5, blk(t, yn, c), sib)]
        if split[t]:
            after_x.insert(0, copy(t, 2, blk(t, xn, c, 0), (*yn, c)))
            after_y.insert(0, copy(t, 3, blk(t, yn, c, 1), (*xn, c)))
        return after_x, after_y, [copy(t, 6, blk(t, dg, c), sib)]

    def send_own():
        for t in range(n):
            for cp in own_sends(t):
                cp.start()

    def relay_neighbours():
        c, chip, me, sib, (xn, yn, dg) = plan()
        for t in range(n):
            after_x, after_y, _ = relays(t)
            copy(t, 0, blk(t, xn, c), me).wait_recv()
            for cp in after_x:
                cp.start()
            copy(t, 1, blk(t, yn, c), me).wait_recv()
            for cp in after_y:
                cp.start()

    def relay_diagonal():
        c, chip, me, sib, (xn, yn, dg) = plan()
        for t in range(n):
            if split[t]:
                copy(t, 2, blk(t, dg, c, 0), me).wait_recv()
                copy(t, 3, blk(t, dg, c, 1), me).wait_recv()
            else:
                copy(t, 2, blk(t, dg, c), me).wait_recv()
            relays(t)[2][0].start()

    def take_sibling():
        _, chip, me, sib, (xn, yn, dg) = plan()
        theirs = 1 - lax.axis_index("c") if layer is None else layer
        for t in range(n):
            for k, peer in ((4, xn), (5, yn), (6, dg)):
                copy(t, k, blk(t, peer, theirs), me).wait_recv()

    def drain_sends():
        for t in range(n):
            after_x, after_y, after_d = relays(t)
            for cp in own_sends(t) + after_x + after_y + after_d:
                cp.wait_send()

    def phase3():
        role(True, relay_diagonal)
        role(False, take_sibling)
        role(True, drain_sends)

    return (lambda: role(True, send_own)), (lambda: role(True, relay_neighbours)), phase3


def _swap_copies(ins, outs, send_sems, recv_sems):
    x, y, c = _me()
    return [pltpu.make_async_remote_copy(src_ref=ins[t].at[1 - c], dst_ref=outs[t], send_sem=send_sems.at[t],
                                         recv_sem=recv_sems.at[t], device_id=(x, y, 1 - c), device_id_type=MESH)
            for t in range(len(ins))]


def _exchange_copies(ins, outs, send_sems, recv_sems):
    x, y, c = _me()
    return [pltpu.make_async_remote_copy(src_ref=ins[t].at[2 * cx + cy], dst_ref=outs[t].at[j],
                                         send_sem=send_sems.at[3 * t + j], recv_sem=recv_sems.at[3 * t + j],
                                         device_id=(cx, cy, c), device_id_type=MESH)
            for j, (cx, cy) in enumerate(_other_chips(x, y)) for t in range(len(ins))]


class _Rider:
    def __init__(self, arrs, out_shape, nsem, copies):
        self.arrs, self.out_shape, self.nsem, self.copies = list(arrs), out_shape, nsem, copies
        self.n = len(self.arrs)

    def scratch(self):
        return [pltpu.SemaphoreType.DMA((self.nsem,)), pltpu.SemaphoreType.DMA((self.nsem,))]

    def start(self, ins, outs, sems):
        for cp in self.copies(ins, outs, *sems):
            cp.start()

    def wait(self, ins, outs, sems):
        for cp in self.copies(ins, outs, *sems):
            cp.wait()


def _swap_rider(gs):
    return _Rider(gs, [SDS(g.shape[1:], g.dtype) for g in gs], len(gs), _swap_copies)


def _exchange_rider(ts):
    return _Rider(ts, [SDS((3,) + t.shape[1:], t.dtype) for t in ts], 3 * len(ts), _exchange_copies)


def _ride_alone(rider, name):
    n = rider.n

    def body(*refs):
        rider.start(refs[:n], refs[n:2 * n], refs[2 * n:])
        rider.wait(refs[:n], refs[n:2 * n], refs[2 * n:])

    return pl.pallas_call(body, name=name, out_shape=rider.out_shape, in_specs=[ANY] * n, out_specs=[ANY] * n,
                          scratch_shapes=rider.scratch(), compiler_params=COMM_PARAMS)(*rider.arrs)


def _join_layers(us, name):
    n = len(us)

    def body(*refs):
        outs, send_sems, recv_sems = refs[n:2 * n], refs[2 * n], refs[2 * n + 1]
        x, y, c = _me()
        cps = [pltpu.make_async_remote_copy(src_ref=outs[t].at[c], dst_ref=outs[t].at[c], send_sem=send_sems.at[t],
                                            recv_sem=recv_sems.at[t], device_id=(x, y, 1 - c), device_id_type=MESH)
               for t in range(n)]
        for cp in cps:
            cp.start()
        for cp in cps:
            cp.wait()

    return pl.pallas_call(body, name=name, out_shape=[SDS(u.shape, u.dtype) for u in us],
                          in_specs=[ANY] * n, out_specs=[ANY] * n, input_output_aliases={t: t for t in range(n)},
                          scratch_shapes=[pltpu.SemaphoreType.DMA((n,)), pltpu.SemaphoreType.DMA((n,))],
                          compiler_params=COMM_PARAMS)(*us)


def _allsum_small(v, name, rider=None):
    M = v.shape[0]
    nr = rider.n if rider else 0

    def body(x_ref, *rest):
        o_ref = rest[nr]
        all_ref, send_sems, recv_sems, local_sem = rest[2 * nr + 1:2 * nr + 5]
        r_io = (rest[:nr], rest[nr + 1:2 * nr + 1], rest[2 * nr + 5:])
        x, y, c = _me()
        me, sib = (x, y, c), (x, y, 1 - c)
        chips = _other_chips(x, y)

        def rows(px, py, pc):
            return all_ref.at[pl.ds((4 * px + 2 * py + pc) * M, M), :]

        def copy(k, block, to, src=None):
            return pltpu.make_async_remote_copy(src_ref=rows(*block) if src is None else src, dst_ref=rows(*block),
                                                send_sem=send_sems.at[k], recv_sem=recv_sems.at[k],
                                                device_id=to, device_id_type=MESH)

        mine = pltpu.make_async_copy(x_ref, rows(*me), local_sem)
        mine.start()
        first = [copy(0, me, sib, src=x_ref)]
        first += [copy(1 + j, me, (*chip, c), src=x_ref) for j, chip in enumerate(chips)]
        for cp in first:
            cp.start()
        if rider:
            rider.start(*r_io)
        passed = [copy(4 + j, (*chip, c), sib) for j, chip in enumerate(chips)]
        for j, chip in enumerate(chips):
            copy(1 + j, (*chip, c), me).wait_recv()
            passed[j].start()
        copy(0, sib, me).wait_recv()
        for j, chip in enumerate(chips):
            copy(4 + j, (*chip, 1 - c), me).wait_recv()
        for cp in first + passed:
            cp.wait_send()
        mine.wait()
        acc = all_ref[0:M, :]
        for d in range(1, 8):
            acc = acc + all_ref[d * M:(d + 1) * M, :]
        o_ref[...] = acc
        if rider:
            rider.wait(*r_io)

    vm = BS(memory_space=pltpu.VMEM)
    out = pl.pallas_call(body, name=name, out_shape=[SDS((M, LANES), F32)] + (rider.out_shape if rider else []),
                         in_specs=[vm] + [ANY] * nr, out_specs=[vm] + [ANY] * nr,
                         scratch_shapes=[pltpu.VMEM((8 * M, LANES), F32), pltpu.SemaphoreType.DMA((7,)),
                                         pltpu.SemaphoreType.DMA((7,)), pltpu.SemaphoreType.DMA]
                         + (rider.scratch() if rider else []),
                         compiler_params=pltpu.CompilerParams(has_side_effects=True, vmem_limit_bytes=VMEM_LIMIT))(
        v, *(rider.arrs if rider else []))
    return out[0], list(out[1:])


FFN = ("w_gate_up", "w_down")
REST = ("w_in", "w_o", "w_uq", "w_ukv")
BIG = FFN + REST
TINY = ("conv_w",)
REPL = ("attn_norm", "mla_q_norm", "mla_kv_norm", "pool_w", "pool_scale", "swa_sinks", "mix_norm", "ffn_norm",
        "final_norm")
ORDER = ("attn_norm", "w_in", "mla_q_norm", "w_uq", "mla_kv_norm", "w_ukv", "conv_w", "pool_w", "pool_scale",
         "swa_sinks", "mix_norm", "w_o", "ffn_norm", "w_gate_up", "w_down", "final_norm")


def _rows8(shape):
    return -(-int(np.prod(shape)) // (8 * LANES)) * 8


def _pack(arrs):
    parts = []
    for a in arrs:
        r = _rows8(a.shape)
        parts.append(jnp.pad(a.reshape(-1), (0, r * LANES - a.size)).reshape(r, LANES))
    return jnp.concatenate(parts, axis=0)


def _unpack(buf, shapes):
    out, r0 = [], 0
    for s in shapes:
        n, r = int(np.prod(s)), _rows8(s)
        rows = buf[r0:r0 + r]
        out.append(rows.reshape(s) if n == r * LANES else rows.reshape(-1)[:n].reshape(s))
        r0 += r
    return out


def _cols_joined(g):
    return jnp.transpose(g, (0, 2, 1, 3)).reshape(g.shape[0], g.shape[2], 4 * g.shape[3])


def _cols_split(w):
    n, a, b4 = w.shape
    return jnp.transpose(w.reshape(n, a, 4, b4 // 4), (0, 2, 1, 3))


def _rope_tables(S):
    inv = 1.0 / (10000.0 ** (jnp.arange(0, 32, 2, dtype=F32) / 32))
    ang = jnp.arange(S, dtype=F32)[:, None] * inv[None, :]
    cos, sin = jnp.cos(ang), jnp.sin(ang)
    z = lambda w: jnp.zeros((S, w), F32)
    tc = jnp.concatenate([jnp.ones((S, 64), F32), cos, cos, jnp.ones((S, 32), F32)], axis=1)
    ts1 = jnp.concatenate([z(64), -sin, z(48)], axis=1)
    ts2 = jnp.concatenate([z(80), sin, z(32)], axis=1)
    return tc, ts1, ts2


def _pad_w_in(wt):
    z = lambda n: jnp.zeros((wt.shape[0], n, wt.shape[2]), wt.dtype)
    return jnp.concatenate([wt[:, 0:384], z(64), wt[:, 384:416], z(32), wt[:, 416:1952]], axis=1)


def _unpad_w_in(dt):
    return jnp.concatenate([dt[:, 0:384], dt[:, 448:480], dt[:, 512:2048]], axis=1)


def _pad_heads(w, src, offs):
    cols = []
    for h in range(HEADS):
        src0, n = src[h]
        z = lambda k: jnp.zeros(w.shape[:-1] + (k,), w.dtype)
        cols += [z(offs[h]), w[..., src0:src0 + n], z(128 - offs[h] - n)]
    return jnp.concatenate(cols, axis=-1)


UQ_SRC = [(h * 96, 96) for h in range(HEADS)]
KN_SRC = [(h * 128, 64) for h in range(HEADS)]
V_SRC = [(h * 128 + 64, 64) for h in range(HEADS)]
ZERO_OFF = [0] * HEADS
V_OFF = [(h % 2) * 64 for h in range(HEADS)]


def _unpad_heads(d, src, offs):
    return [d[..., h * 128 + offs[h]: h * 128 + offs[h] + src[h][1]] for h in range(HEADS)]


def kernel(x, attn_norm, w_in, mla_q_norm, w_uq, mla_kv_norm, w_ukv, conv_w, pool_w, pool_scale, swa_sinks, mix_norm, w_o, ffn_norm, w_gate_up, w_down, final_norm, loss_target, m_attn_norm, m_w_in, m_mla_q_norm, m_w_uq, m_mla_kv_norm, m_w_ukv, m_conv_w, m_pool_w, m_pool_scale, m_swa_sinks, m_mix_norm, m_w_o, m_ffn_norm, m_w_gate_up, m_w_down, m_final_norm, v_attn_norm, v_w_in, v_mla_q_norm, v_w_uq, v_mla_kv_norm, v_w_ukv, v_conv_w, v_pool_w, v_pool_scale, v_swa_sinks, v_mix_norm, v_w_o, v_ffn_norm, v_w_gate_up, v_w_down, v_final_norm):
    W = dict(attn_norm=attn_norm, w_in=w_in, mla_q_norm=mla_q_norm, w_uq=w_uq, mla_kv_norm=mla_kv_norm, w_ukv=w_ukv,
             conv_w=conv_w, pool_w=pool_w, pool_scale=pool_scale, swa_sinks=swa_sinks, mix_norm=mix_norm, w_o=w_o,
             ffn_norm=ffn_norm, w_gate_up=w_gate_up, w_down=w_down, final_norm=final_norm)
    M1 = dict(attn_norm=m_attn_norm, w_in=m_w_in, mla_q_norm=m_mla_q_norm, w_uq=m_w_uq, mla_kv_norm=m_mla_kv_norm,
              w_ukv=m_w_ukv, conv_w=m_conv_w, pool_w=m_pool_w, pool_scale=m_pool_scale, swa_sinks=m_swa_sinks,
              mix_norm=m_mix_norm, w_o=m_w_o, ffn_norm=m_ffn_norm, w_gate_up=m_w_gate_up, w_down=m_w_down,
              final_norm=m_final_norm)
    V2 = dict(attn_norm=v_attn_norm, w_in=v_w_in, mla_q_norm=v_mla_q_norm, w_uq=v_w_uq, mla_kv_norm=v_mla_kv_norm,
              w_ukv=v_w_ukv, conv_w=v_conv_w, pool_w=v_pool_w, pool_scale=v_pool_scale, swa_sinks=v_swa_sinks,
              mix_norm=v_mix_norm, w_o=v_w_o, ffn_norm=v_ffn_norm, w_gate_up=v_w_gate_up, w_down=v_w_down,
              final_norm=v_final_norm)
    S = x.shape[1]
    xc, yc, cc = _me()
    chip = 2 * xc + yc
    kc = jnp.stack([chip, cc]).astype(jnp.int32)

    first, later = ("w_in", "w_uq", "w_ukv", "conv_w"), ("w_o", "w_gate_up", "w_down")
    T = lambda a: jnp.swapaxes(a, 1, 2)
    W["w_in"], M1["w_in"], V2["w_in"] = T(w_in), T(m_w_in), T(v_w_in)
    placed = {n: _place(W[n], kc, F32 if n == "conv_w" else BF16, f"place_{n}") for n in first + later}
    gi, gq, gkv, gcv = _gather([placed[n] for n in first], "gather_weights")
    later_w = [placed[n] for n in later]
    win_p = _pad_w_in(gi.reshape(2, 4 * gi.shape[2], D))
    wuq_p = _pad_heads(_cols_joined(gq), UQ_SRC, ZERO_OFF)
    wukv = _cols_joined(gkv)
    wk_p = _pad_heads(wukv, KN_SRC, ZERO_OFF)
    wv_p = _pad_heads(wukv, V_SRC, V_OFF)
    conv8 = jnp.pad(_cols_joined(gcv), ((0, 0), (0, 5), (0, 0)))
    pwd = jnp.concatenate([jnp.concatenate(
        [jnp.pad(pool_w[:, 2 * b], ((0, 0), (0, 0), (0, 64))), jnp.pad(pool_w[:, 2 * b + 1], ((0, 0), (0, 0), (64, 0)))],
        axis=1) for b in range(2)], axis=1).astype(BF16)
    tabs = _rope_tables(S)
    g_attn, g_q, g_kv, g_mix, g_ffn, g_ps = (_g3(W[n]) for n in ("attn_norm", "mla_q_norm", "mla_kv_norm", "mix_norm",
                                                                  "ffn_norm", "pool_scale"))

    xs = [x[0]]
    saved = []
    for l in range(DEPTH):
        x0 = xs[-1]
        proj, h = _norm_mm(x0, g_attn, l, win_p, _wspec_in(l), D_INP, D_INP, F32, f"in_proj{l}", w_t=True)
        q, k, v, kt, vt = _mla_prep(proj, g_q, g_kv, wuq_p, wk_p, wv_p, tabs, l, f"mla_prep{l}")
        ya, lse, later_w = _mla_attn(q, k, vt, later_w, l, f"mla_attn{l}")
        go, gu4, gd = later_w
        wo, wdown = go.reshape(2, D, D), gd.reshape(2, D_FF, D)
        yb = _conv(proj, conv8, l, f"conv{l}")
        ycp = _pool(proj, pwd, g_ps, l, f"pool{l}")
        yd = _swa(proj, swa_sinks, l, f"swa{l}")
        x1, ycat, mixed = _mix_out(x0, ya, yb, ycp, yd, g_mix, wo, l, f"mix_out{l}")
        gu, h2 = _norm_mm(x1, g_ffn, l, gu4, _wspec_gu(l), 2 * D_FF, 2 * D_FF // 4, BF16, f"gate_up{l}")
        x2, act = _swiglu_mm_res(x1, gu, wdown, l, f"down{l}")
        saved.append(dict(x0=x0, proj=proj, h=h, q=q, k=k, kt=kt, v=v, lse=lse, x1=x1, ycat=ycat, mixed=mixed,
                          gu=gu, h2=h2, act=act))
        xs.append(x2)

    dx, dx16, dg_final, loss_tile = _loss_head(xs[-1], final_norm.reshape(1, D), loss_target[0], "loss_head")
    loss = lax.psum(loss_tile[0, 0] * (0.5 / D), ("x", "y", "c"))

    G = {n: [None] * DEPTH for n in ("w_uq", "w_ukv") + TINY + REPL if n != "final_norm"}
    gw_in = gw_o = gw_gu = gw_down = None
    for l in reversed(range(DEPTH)):
        sv = saved[l]
        dgu = _bwd_down(dx16, wdown, sv["gu"], l, f"down_bwd{l}")
        gw_down = _mm_tn(sv["act"], dx16, l, gw_down, f"dw_down{l}")
        gw_gu = _mm_tn(sv["h2"], dgu, l, gw_gu, f"dw_gate_up{l}", split4=True)
        exchange_gu = exchange_down = None
        if l == 0:
            g_f = [gw_gu, gw_down.reshape(2, 4, D_FF // 4, D)]
            dx1, dx1_16, dg, got_f = _mm_nt_normbwd(dgu, gu4, l, sv["x1"], g_ffn, dx, 1, f"gate_up_bwd{l}",
                                                    rider=_swap_rider(g_f))
            pairs_f = [_pair_sum(g, o, kc, f"rs_pair_sum_{n}") for g, o, n in zip(g_f, got_f, FFN)]
            exchange_gu, exchange_down = _exchange_rider([pairs_f[0][1]]), _exchange_rider([pairs_f[1][1]])
        else:
            dx1, dx1_16, dg = _mm_nt_normbwd(dgu, gu4, l, sv["x1"], g_ffn, dx, 1, f"gate_up_bwd{l}")
        G["ffn_norm"][l] = dg[0]
        gw_o = _mm_tn(sv["mixed"], dx1_16, l, gw_o, f"dw_o{l}")
        dycat, dg = _mm_nt_normbwd(dx1_16, wo.reshape(2, 1, D, D), l, sv["ycat"], g_mix, None, 4, f"mix_bwd{l}")
        G["mix_norm"][l] = dg[0]

        proj = sv["proj"]
        delta = _mla_delta(dycat, sv["ycat"], f"mla_delta{l}")
        dq, dk, dv, got3_gu = _mla_attn_bwd(sv["q"], sv["k"], sv["kt"], sv["v"], dycat, sv["lse"], delta, exchange_gu,
                                            f"mla_attn_bwd{l}")
        dcq, dckv, dkr, dwuq, dwk, dwv, dgq, dgkv = _mla_prep_bwd(
            dq, dk, dv, proj, g_q, g_kv, wuq_p, wk_p, wv_p, tabs, l, f"mla_prep_bwd{l}")
        dgb, dgc, duc, dcw = _conv_bwd(proj, conv8, dycat, l, f"conv_bwd{l}")
        dup, dpw, dps = _pool_bwd(proj, pwd, g_ps, dycat, l, f"pool_bwd{l}")
        if l == 0:
            dqs, dks, dvs, dsink, got3_down = _swa_bwd(proj, swa_sinks, dycat, l, f"swa_bwd{l}", rider=exchange_down)
            got3_f = got3_gu + got3_down
        else:
            dqs, dks, dvs, dsink = _swa_bwd(proj, swa_sinks, dycat, l, f"swa_bwd{l}")
        dproj = jnp.concatenate([dcq, dckv, dkr, dgb, dgc, duc, dup, dqs, dks, dvs], axis=1)
        gw_in = _mm_tn(dproj, sv["h"], l, gw_in, f"dw_in{l}")
        G["w_uq"][l] = jnp.concatenate(_unpad_heads(dwuq, UQ_SRC, ZERO_OFF), axis=1)
        kn, vv = _unpad_heads(dwk, KN_SRC, ZERO_OFF), _unpad_heads(dwv, V_SRC, V_OFF)
        G["w_ukv"][l] = jnp.concatenate([t for h in range(HEADS) for t in (kn[h], vv[h])], axis=1)
        dx, dx16, dg = _mm_nt_normbwd(dproj, win_p.reshape(2, 1, D_INP, D), l, sv["x0"], g_attn, dx1, 1, f"in_proj_bwd{l}",
                                      w_t=True)
        G["attn_norm"][l] = dg[0]
        G["mla_q_norm"][l] = dgq[0]
        G["mla_kv_norm"][l] = dgkv[0]
        G["conv_w"][l] = dcw[0:3]
        G["pool_w"][l] = jnp.stack([dpw[0:64, 0:64], dpw[64:128, 64:128], dpw[128:192, 0:64], dpw[192:256, 64:128]])
        G["pool_scale"][l] = dps[0]
        G["swa_sinks"][l] = dsink[0, 0:4]
    grad_x = dx[None]
    Gl = {n: jnp.stack(G[n]) for n in TINY + REPL if n != "final_norm"}
    Gl["final_norm"] = dg_final[0]

    us_f = [_chip_sum(p[0], o3, kc, f"rs_chip_sum_{n}") for p, o3, n in zip(pairs_f, got3_f, FFN)]
    gsum_f = _join_layers(us_f, "rs_join_cores_ffn")
    g_r = [_unpad_w_in(gw_in).reshape(2, 4, -1, D), gw_o.reshape(2, 4, D // 4, D), _cols_split(jnp.stack(G["w_uq"])),
           _cols_split(jnp.stack(G["w_ukv"]))]
    got_r = _ride_alone(_swap_rider(g_r), "rs_swap_cores")
    pairs_r = [_pair_sum(g, o, kc, f"rs_pair_sum_{n}") for g, o, n in zip(g_r, got_r, REST)]
    small = TINY + REPL
    full_shapes = [Gl[n].shape for n in small]
    summed, got3_r = _allsum_small(_pack([Gl[n] for n in small]), "allsum_small",
                                   rider=_exchange_rider([p[1] for p in pairs_r]))
    summed = _unpack(summed, full_shapes)
    us_r = [_chip_sum(p[0], o3, kc, f"rs_chip_sum_{n}") for p, o3, n in zip(pairs_r, got3_r, REST)]
    gsum_r = _join_layers(us_r, "rs_join_cores")
    res = {}
    for n, g in zip(BIG, gsum_f + gsum_r):
        d_, m_, v_ = _adamw(W[n], g, M1[n], V2[n], f"adamw_{n}")
        back = T if n == "w_in" else (lambda a: a)
        res["g", n], res["d", n], res["m", n], res["v", n] = back(g), back(d_), back(m_), back(v_)

    gs = {}
    for n, g in zip(small, summed):
        if n in TINY:
            wdt = W[n].shape[2]
            g = lax.dynamic_slice_in_dim(g, chip * wdt, wdt, axis=2)
        gs[n] = g
    own_shapes = [W[n].shape for n in small]
    pk = lambda src: _pack([src[n] for n in small])[None]
    d_s, m_s, v_s = _adamw(pk(W), pk(gs), pk(M1), pk(V2), "adamw_small")
    for key, buf in (("d", d_s), ("m", m_s), ("v", v_s)):
        for n, a in zip(small, _unpack(buf[0], own_shapes)):
            res[key, n] = a
    for n in small:
        res["g", n] = gs[n]

    return (loss, grad_x, *[res["g", n] for n in ORDER], *[res["d", n] for n in ORDER],
            *[res["m", n] for n in ORDER], *[res["v", n] for n in ORDER])
```

```python
import math

import numpy as np
import jax
import jax.numpy as jnp
from jax import lax
from jax.experimental import pallas as pl
from jax.experimental.pallas import tpu as pltpu

F32, BF16 = jnp.float32, jnp.bfloat16
SDS = jax.ShapeDtypeStruct
BS = pl.BlockSpec
MESH = pl.DeviceIdType.MESH

D = 1024
DEPTH = 2
HEADS = 4
D_FF = 2816
D_INP = 2048
EPS = 1e-6
SWA_WINDOW = 128
BLK = 128
SLOPES = tuple(2.0 ** (-8.0 * (i + 1) / 4) for i in range(4))
QK_SCALE = 1.0 / math.sqrt(96)
SWA_SCALE = 1.0 / math.sqrt(64)
LR, B1, B2, ADAM_EPS, WD, STEP = 0.001, 0.9, 0.999, 1e-08, 0.01, 10

LANES = 1024
VMEM_LIMIT = 56 * 1024 * 1024
NEG_INF = float("-inf")

C_CQ, C_CKV, C_KR, C_GB, C_GC, C_UC, C_UP, C_QS, C_KS, C_VS = 0, 256, 384, 512, 768, 1024, 1280, 1536, 1792, 1920


def _params(ngrid):
    return pltpu.CompilerParams(dimension_semantics=("arbitrary",) * ngrid, vmem_limit_bytes=VMEM_LIMIT)


def _pc(body, *, name, grid, in_specs, out_specs, out_shape, scratch=(), aliases=None):
    return pl.pallas_call(
        body, name=name, grid=grid, in_specs=in_specs, out_specs=out_specs, out_shape=out_shape,
        scratch_shapes=scratch, input_output_aliases=aliases or {}, compiler_params=_params(len(grid)))


def _dot(a, b):
    return jnp.dot(a, b, preferred_element_type=F32)


def _dot_nt(a, b):
    return lax.dot_general(a, b, (((1,), (1,)), ((), ())), preferred_element_type=F32)


def _dot_tn(a, b):
    return lax.dot_general(a, b, (((0,), (0,)), ((), ())), preferred_element_type=F32)


def _tile(n, cap):
    if n <= cap:
        return n
    t = cap - cap % 128
    while n % t:
        t -= 128
    return t


def _row_tile(a, b, cap=262144):
    bp = -(-b // 128) * 128
    best = None
    for t in range(8, a + 1, 8):
        if a % t == 0 and t * bp <= cap:
            best = t
    if best is None or (best < 64 and a * bp <= 2 * cap):
        return a
    return best


def _g3(a):
    return a.reshape(a.shape[0], 1, a.shape[1])


def _norm_mm(x, g3, l, w, wspec, N, tn, out_dtype, name, w_t=False):
    S, K = x.shape
    tm = min(1024 if out_dtype == BF16 else 512, S)

    def body(x_ref, g_ref, w_ref, y_ref, h_ref):
        @pl.when(pl.program_id(1) == 0)
        def _():
            xv = x_ref[...]
            r = lax.rsqrt(jnp.mean(xv * xv, axis=-1, keepdims=True) + EPS)
            h_ref[...] = (xv * r * g_ref[...]).astype(BF16)

        y_ref[...] = (_dot_nt if w_t else _dot)(h_ref[...], w_ref[...]).astype(out_dtype)

    return _pc(body, name=name, grid=(S // tm, N // tn),
               in_specs=[BS((tm, K), lambda i, j: (i, 0)), BS((None, 1, K), lambda i, j: (l, 0, 0)), wspec],
               out_specs=[BS((tm, tn), lambda i, j: (i, j)), BS((tm, K), lambda i, j: (i, 0))],
               out_shape=[SDS((S, N), out_dtype), SDS((S, K), BF16)])(x, g3, w)


def _wspec_in(l):
    return BS((None, D_INP, D), lambda i, j: (l, j, 0))


def _wspec_gu(l):
    return BS((None, None, D, 2 * D_FF // 4), lambda i, j: (l, j, 0, 0))


def _mix_out(x0, ya, yb, yc, yd, gmix3, wo, l, name):
    S = x0.shape[0]
    tm = min(512, S)

    def body(x_ref, ya_ref, yb_ref, yc_ref, yd_ref, g_ref, w_ref, x1_ref, ycat_ref, mixed_ref):
        groups = [ya_ref[...], yb_ref[...], yc_ref[...], yd_ref[...]]
        for gi, yg in enumerate(groups):
            sl = slice(gi * 256, (gi + 1) * 256)
            r = lax.rsqrt(jnp.mean(yg * yg, axis=-1, keepdims=True) + EPS)
            ycat_ref[:, sl] = yg
            mixed_ref[:, sl] = (yg * r * g_ref[:, sl]).astype(BF16)
        x1_ref[...] = x_ref[...] + _dot(mixed_ref[...], w_ref[...])

    row = lambda w: BS((tm, w), lambda i: (i, 0))
    return _pc(body, name=name, grid=(S // tm,),
               in_specs=[row(D), row(256), row(256), row(256), row(256), BS((None, 1, D), lambda i: (l, 0, 0)),
                         BS((None, D, D), lambda i: (l, 0, 0))],
               out_specs=[row(D), row(D), row(D)],
               out_shape=[SDS((S, D), F32), SDS((S, D), F32), SDS((S, D), BF16)])(x0, ya, yb, yc, yd, gmix3, wo)


def _swiglu_mm_res(x1, gu, wdown, l, name):
    S = x1.shape[0]
    tm = min(256, S)

    def body(x_ref, gate_ref, up_ref, w_ref, x2_ref, act_ref):
        gt = gate_ref[...].astype(F32)
        act = (gt * pl.reciprocal(1.0 + jnp.exp(-gt), approx=True) * up_ref[...].astype(F32)).astype(BF16)
        act_ref[...] = act
        x2_ref[...] = x_ref[...] + _dot(act, w_ref[...])

    return _pc(body, name=name, grid=(S // tm,),
               in_specs=[BS((tm, D), lambda i: (i, 0)), BS((tm, D_FF), lambda i: (i, 0)),
                         BS((tm, D_FF), lambda i: (i, 1)), BS((None, D_FF, D), lambda i: (l, 0, 0))],
               out_specs=[BS((tm, D), lambda i: (i, 0)), BS((tm, D_FF), lambda i: (i, 0))],
               out_shape=[SDS((S, D), F32), SDS((S, D_FF), BF16)])(x1, gu, gu, wdown)


def _loss_head(x, g, tgt, name):
    S = x.shape[0]
    tm = min(512, S)

    def body(x_ref, g_ref, t_ref, dx_ref, dx16_ref, dg_ref, loss_ref):
        @pl.when(pl.program_id(0) == 0)
        def _():
            dg_ref[...] = jnp.zeros_like(dg_ref)
            loss_ref[...] = jnp.zeros_like(loss_ref)

        xv = x_ref[...]
        r = lax.rsqrt(jnp.mean(xv * xv, axis=-1, keepdims=True) + EPS)
        xh = xv * r
        gv = g_ref[...]
        diff = xh * gv - t_ref[...]
        loss_ref[...] += jnp.sum(diff * diff)
        dy = diff * (1.0 / D)
        dg_ref[...] += jnp.sum(dy * xh, axis=0, keepdims=True)
        dxh = dy * gv
        dx = r * (dxh - xh * jnp.mean(dxh * xh, axis=-1, keepdims=True))
        dx_ref[...] = dx
        dx16_ref[...] = dx.astype(BF16)

    row = BS((tm, D), lambda i: (i, 0))
    return _pc(body, name=name, grid=(S // tm,),
               in_specs=[row, BS((1, D), lambda i: (0, 0)), row],
               out_specs=[row, row, BS((8, D), lambda i: (0, 0)), BS((8, 128), lambda i: (0, 0))],
               out_shape=[SDS((S, D), F32), SDS((S, D), BF16), SDS((8, D), F32), SDS((8, 128), F32)])(x, g, tgt)


def _mm_tn(a, b, l, prev, name, split4=False):
    S, Ka = a.shape
    N = b.shape[1]
    if split4:
        ta, tn = _tile(Ka, 256), N // 4
        out_shape = SDS((2, 4, Ka, tn), F32)
        out_spec = BS((None, None, ta, tn), lambda j, i: (l, j, i, 0))
    else:
        ta, tn = _tile(Ka, 512), _tile(N, 1024)
        out_shape = SDS((2, Ka, N), F32)
        out_spec = BS((None, ta, tn), lambda j, i: (l, i, j))

    def body(a_ref, b_ref, *rest):
        rest[-1][...] = _dot_tn(a_ref[...], b_ref[...])

    in_specs = [BS((S, ta), lambda j, i: (0, i)), BS((S, tn), lambda j, i: (0, j))]
    args = [a, b]
    if prev is not None:
        in_specs.append(BS(memory_space=pl.ANY))
        args.append(prev)
    return _pc(body, name=name, grid=(N // tn, Ka // ta), in_specs=in_specs, out_specs=out_spec, out_shape=out_shape,
               aliases={2: 0} if prev is not None else None)(*args)


def _bwd_down(dx16, wdown, gu, l, name):
    S = dx16.shape[0]
    tm = min(256, S)

    def body(dx_ref, w_ref, gate_ref, up_ref, dgu_ref):
        dxv = dx_ref[...]
        for c0 in range(0, D_FF, 256):
            cs = slice(c0, c0 + 256)
            dact = _dot_nt(dxv, w_ref[cs, :])
            gt = gate_ref[:, cs].astype(F32)
            sg = pl.reciprocal(1.0 + jnp.exp(-gt), approx=True)
            dgu_ref[:, cs] = (dact * up_ref[:, cs].astype(F32) * (sg * (1.0 + gt * (1.0 - sg)))).astype(BF16)
            dgu_ref[:, D_FF + c0:D_FF + c0 + 256] = (dact * (gt * sg)).astype(BF16)

    return _pc(body, name=name, grid=(S // tm,),
               in_specs=[BS((tm, D), lambda i: (i, 0)), BS((None, D_FF, D), lambda i: (l, 0, 0)),
                         BS((tm, D_FF), lambda i: (i, 0)), BS((tm, D_FF), lambda i: (i, 1))],
               out_specs=BS((tm, 2 * D_FF), lambda i: (i, 0)),
               out_shape=SDS((S, 2 * D_FF), BF16))(dx16, wdown, gu, gu)


def _mm_nt_normbwd(dy, w4, l, x, g3, dres, ngroups, name, rider=None, w_t=False):
    S, K = dy.shape
    nk, kc = w4.shape[1], w4.shape[2 if w_t else 3]
    mm = _dot if w_t else _dot_nt
    tm = min(512, S)
    gw = D // ngroups
    has_res = dres is not None
    nr = rider.n if rider else 0
    n_in, n_out = 4 + has_res, 2 + has_res

    def body(*refs):
        dy_ref, w_ref, x_ref, g_ref = refs[:4]
        res_ref = refs[4] if has_res else None
        outs = refs[n_in + nr:n_in + nr + n_out]
        dx_ref, dg_ref = outs[0], outs[-1]
        dx16_ref = outs[1] if has_res else None
        r_io = (refs[n_in:n_in + nr], refs[n_in + nr + n_out:n_in + 2 * nr + n_out], refs[n_in + 2 * nr + n_out:])
        if rider:
            pl.when(pl.program_id(0) == 0)(lambda: rider.start(*r_io))

        @pl.when(pl.program_id(0) == 0)
        def _():
            dg_ref[...] = jnp.zeros_like(dg_ref)

        dh = mm(dy_ref[:, 0:kc], w_ref[0])
        for k in range(1, nk):
            dh = dh + mm(dy_ref[:, k * kc:(k + 1) * kc], w_ref[k])
        for gi in range(ngroups):
            sl = slice(gi * gw, (gi + 1) * gw)
            xg = x_ref[:, sl]
            r = lax.rsqrt(jnp.mean(xg * xg, axis=-1, keepdims=True) + EPS)
            xh = xg * r
            dhg = dh[:, sl]
            dg_ref[:, sl] += jnp.sum(dhg * xh, axis=0, keepdims=True)
            dxh = dhg * g_ref[:, sl]
            dxg = r * (dxh - xh * jnp.mean(dxh * xh, axis=-1, keepdims=True))
            if has_res:
                dxg = dxg + res_ref[:, sl]
                dx16_ref[:, sl] = dxg.astype(BF16)
            dx_ref[:, sl] = dxg
        if rider:
            pl.when(pl.program_id(0) == S // tm - 1)(lambda: rider.wait(*r_io))

    row = BS((tm, D), lambda i: (i, 0))
    in_specs = [BS((tm, K), lambda i: (i, 0)),
                BS((None,) + tuple(w4.shape[1:]), lambda i: (l, 0, 0, 0), pipeline_mode=pl.Buffered(1)), row,
                BS((None, 1, D), lambda i: (l, 0, 0))]
    args = [dy, w4, x, g3]
    out_specs, out_shape = [row], [SDS((S, D), F32)]
    if has_res:
        in_specs.append(row)
        args.append(dres)
        out_specs.append(row)
        out_shape.append(SDS((S, D), BF16))
    out_specs.append(BS((8, D), lambda i: (0, 0)))
    out_shape.append(SDS((8, D), F32))
    if not rider:
        return _pc(body, name=name, grid=(S // tm,), in_specs=in_specs, out_specs=out_specs, out_shape=out_shape)(*args)
    out = pl.pallas_call(body, name=name, grid=(S // tm,), in_specs=in_specs + [ANY] * nr,
                         out_specs=out_specs + [ANY] * nr, out_shape=out_shape + rider.out_shape,
                         scratch_shapes=rider.scratch(),
                         compiler_params=pltpu.CompilerParams(dimension_semantics=("arbitrary",),
                                                              vmem_limit_bytes=VMEM_LIMIT, has_side_effects=True))(
        *args, *rider.arrs)
    return (*out[:n_out], list(out[n_out:]))


def _rope(x, c, s1, s2):
    return x * c + pltpu.roll(x, 112, axis=1) * s1 + pltpu.roll(x, 16, axis=1) * s2


def _rope_t(dy, c, s1, s2):
    return dy * c + pltpu.roll(dy * s1, 16, axis=1) + pltpu.roll(dy * s2, 112, axis=1)


def _mla_prep(proj, gq3, gkv3, wuq, wk, wv, tabs, l, name):
    S = proj.shape[0]
    tm = min(512, S)
    tc, ts1, ts2 = tabs

    def body(cq_ref, ckv_ref, kr_ref, gq_ref, gkv_ref, wuq_ref, wk_ref, wv_ref, c_ref, s1_ref, s2_ref,
             q_ref, k_ref, v_ref, kt_ref, vt_ref):
        c, s1, s2 = c_ref[...], s1_ref[...], s2_ref[...]
        cq = cq_ref[...]
        rq = lax.rsqrt(jnp.mean(cq * cq, axis=-1, keepdims=True) + EPS)
        qa = _dot((cq * rq * gq_ref[...]).astype(BF16), wuq_ref[...])
        ckv = ckv_ref[...]
        rkv = lax.rsqrt(jnp.mean(ckv * ckv, axis=-1, keepdims=True) + EPS)
        ckvn = (ckv * rkv * gkv_ref[...]).astype(BF16)
        ka = _dot(ckvn, wk_ref[...])
        va = _dot(ckvn, wv_ref[...])
        v_ref[...] = va.astype(BF16)
        vt_ref[...] = va.T.astype(BF16)
        krr = _rope(kr_ref[...], c, s1, s2)
        for h in range(HEADS):
            sl = slice(h * 128, (h + 1) * 128)
            q_ref[:, sl] = (_rope(qa[:, sl], c, s1, s2) * QK_SCALE).astype(BF16)
            kh = ka[:, sl] + krr
            k_ref[:, sl] = kh.astype(BF16)
            kt_ref[sl, :] = kh.T.astype(BF16)

    lay = lambda a, b: BS((None, a, b), lambda i: (l, 0, 0))
    tab = BS((tm, 128), lambda i: (i, 0))
    return _pc(body, name=name, grid=(S // tm,),
               in_specs=[BS((tm, 256), lambda i: (i, 0)), BS((tm, 128), lambda i: (i, 2)), BS((tm, 128), lambda i: (i, 3)),
                         lay(1, 256), lay(1, 128), lay(256, 512), lay(128, 512), lay(128, 512), tab, tab, tab],
               out_specs=[BS((tm, 512), lambda i: (i, 0))] * 3 + [BS((512, tm), lambda i: (0, i))] * 2,
               out_shape=[SDS((S, 512), BF16)] * 3 + [SDS((512, S), BF16)] * 2)(
        proj, proj, proj, gq3, gkv3, wuq, wk, wv, tc, ts1, ts2)


def _mla_prep_bwd(dq, dk, dv, proj, gq3, gkv3, wuq, wk, wv, tabs, l, name):
    S = proj.shape[0]
    tm = min(512, S)
    tc, ts1, ts2 = tabs

    def body(dq_ref, dk_ref, dv_ref, cq_ref, ckv_ref, gq_ref, gkv_ref, wuq_ref, wk_ref, wv_ref, c_ref, s1_ref, s2_ref,
             dcq_ref, dckv_ref, dkr_ref, dwuq_ref, dwk_ref, dwv_ref, dgq_ref, dgkv_ref):
        @pl.when(pl.program_id(0) == 0)
        def _():
            for r in (dwuq_ref, dwk_ref, dwv_ref, dgq_ref, dgkv_ref):
                r[...] = jnp.zeros_like(r)

        c, s1, s2 = c_ref[...], s1_ref[...], s2_ref[...]
        dqp = jnp.concatenate(
            [_rope_t(dq_ref[h * 128:(h + 1) * 128, :].T * QK_SCALE, c, s1, s2) for h in range(HEADS)], axis=1).astype(BF16)
        cq = cq_ref[...]
        rq = lax.rsqrt(jnp.mean(cq * cq, axis=-1, keepdims=True) + EPS)
        cqh = cq * rq
        gq_v = gq_ref[...]
        dwuq_ref[...] += _dot_tn((cqh * gq_v).astype(BF16), dqp)
        dcqn = _dot_nt(dqp, wuq_ref[...])
        dgq_ref[...] += jnp.sum(dcqn * cqh, axis=0, keepdims=True)
        dxh = dcqn * gq_v
        dcq_ref[...] = (rq * (dxh - cqh * jnp.mean(dxh * cqh, axis=-1, keepdims=True))).astype(BF16)

        dkb = dk_ref[...].astype(BF16)
        dvb = dv_ref[...].astype(BF16)
        ckv = ckv_ref[...]
        rkv = lax.rsqrt(jnp.mean(ckv * ckv, axis=-1, keepdims=True) + EPS)
        ckh = ckv * rkv
        gkv_v = gkv_ref[...]
        ckvn = (ckh * gkv_v).astype(BF16)
        dwk_ref[...] += _dot_tn(ckvn, dkb)
        dwv_ref[...] += _dot_tn(ckvn, dvb)
        dckvn = _dot_nt(dkb, wk_ref[...]) + _dot_nt(dvb, wv_ref[...])
        dgkv_ref[...] += jnp.sum(dckvn * ckh, axis=0, keepdims=True)
        dyh = dckvn * gkv_v
        dckv_ref[...] = (rkv * (dyh - ckh * jnp.mean(dyh * ckh, axis=-1, keepdims=True))).astype(BF16)
        dks = dk_ref[:, 0:128] + dk_ref[:, 128:256] + dk_ref[:, 256:384] + dk_ref[:, 384:512]
        dkr_ref[...] = _rope_t(dks, c, s1, s2).astype(BF16)

    full = lambda a, b: BS((a, b), lambda i: (0, 0))
    lay = lambda a, b: BS((None, a, b), lambda i: (l, 0, 0))
    tab = BS((tm, 128), lambda i: (i, 0))
    row = lambda w: BS((tm, w), lambda i: (i, 0))
    return _pc(body, name=name, grid=(S // tm,),
               in_specs=[BS((512, tm), lambda i: (0, i)), row(512), row(512), BS((tm, 256), lambda i: (i, 0)),
                         BS((tm, 128), lambda i: (i, 2)),
                         lay(1, 256), lay(1, 128), lay(256, 512), lay(128, 512), lay(128, 512), tab, tab, tab],
               out_specs=[row(256), row(128), row(128), full(256, 512), full(128, 512), full(128, 512),
                          full(8, 256), full(8, 128)],
               out_shape=[SDS((S, 256), BF16), SDS((S, 128), BF16), SDS((S, 128), BF16), SDS((256, 512), F32),
                          SDS((128, 512), F32), SDS((128, 512), F32), SDS((8, 256), F32), SDS((8, 128), F32)])(
        dq, dk, dv, proj, proj, gq3, gkv3, wuq, wk, wv, tc, ts1, ts2)


def _causal_steps(n, q_outer):
    if q_outer:
        pairs = [(i, j) for i in range(n) for j in range(i + 1)]
    else:
        pairs = [(i, j) for j in range(n) for i in range(j, n)]
    return jnp.asarray([p[0] for p in pairs], jnp.int32), jnp.asarray([p[1] for p in pairs], jnp.int32)


def _mla_attn(q, k, vt, gts, layer, name):
    S = q.shape[0]
    t = min(512, S)
    n = S // t
    ng = len(gts)

    qi, kj = _causal_steps(n, True)
    last = qi.shape[0] - 1

    def body(qi_ref, kj_ref, q_ref, k_ref, vt_ref, *rest):
        (ya_ref, lse_ref), g_refs = rest[ng:ng + 2], rest[ng + 2:2 * ng + 2]
        m_sc, l_sc, acc_sc = rest[2 * ng + 2:2 * ng + 5]
        i, j = qi_ref[pl.program_id(1)], kj_ref[pl.program_id(1)]
        if ng:
            phases = _gather_phases(g_refs, [g.shape for g in gts], rest[2 * ng + 5], rest[2 * ng + 6], layer)
            for ph, (pp, ss) in zip(phases[:2], ((0, 0), (1, 0))):
                pl.when((pl.program_id(0) == pp) & (pl.program_id(1) == ss))(ph)

        @pl.when(j == 0)
        def _():
            m_sc[...] = jnp.full_like(m_sc, NEG_INF)
            l_sc[...] = jnp.zeros_like(l_sc)
            acc_sc[...] = jnp.zeros_like(acc_sc)

        def step(masked):
            for hh in range(2):
                sl = slice(hh * 128, (hh + 1) * 128)
                st = _dot_nt(k_ref[:, sl], q_ref[:, sl])
                if masked:
                    key = lax.broadcasted_iota(jnp.int32, (t, t), 0)
                    qry = lax.broadcasted_iota(jnp.int32, (t, t), 1)
                    st = jnp.where(key <= qry, st, NEG_INF)
                m_prev = m_sc[hh]
                m_new = jnp.maximum(m_prev, jnp.max(st, axis=0, keepdims=True))
                p = jnp.exp(st - m_new)
                alpha = jnp.exp(m_prev - m_new)
                l_sc[hh] = alpha * l_sc[hh] + jnp.sum(p, axis=0, keepdims=True)
                acc_sc[hh] = alpha * acc_sc[hh] + _dot(vt_ref[sl, :], p.astype(BF16))
                m_sc[hh] = m_new

        @pl.when(j < i)
        def _():
            step(False)

        @pl.when(j == i)
        def _():
            step(True)
            ya_ref[...] = (acc_sc[0] / l_sc[0] + acc_sc[1] / l_sc[1]).T
            for hh in range(2):
                lse_ref[hh] = m_sc[hh] + jnp.log(l_sc[hh])

        if ng:
            pl.when((pl.program_id(0) == 1) & (pl.program_id(1) == last))(phases[2])

    gs = pltpu.PrefetchScalarGridSpec(
        num_scalar_prefetch=2, grid=(2, qi.shape[0]),
        in_specs=[BS((t, 256), lambda p, s, qi, kj: (qi[s], p)), BS((t, 256), lambda p, s, qi, kj: (kj[s], p)),
                  BS((256, t), lambda p, s, qi, kj: (p, kj[s]))] + [ANY] * ng,
        out_specs=[BS((t, 128), lambda p, s, qi, kj: (qi[s], p)), BS((2, 1, t), lambda p, s, qi, kj: (p, 0, qi[s]))]
        + [ANY] * ng,
        scratch_shapes=[pltpu.VMEM((2, 1, t), F32), pltpu.VMEM((2, 1, t), F32), pltpu.VMEM((2, 128, t), F32)]
        + ([pltpu.SemaphoreType.DMA((7 * ng,)), pltpu.SemaphoreType.DMA((7 * ng,))] if ng else []))
    out = pl.pallas_call(body, name=name, grid_spec=gs,
                         out_shape=[SDS((S, 256), F32), SDS((HEADS, 1, S), F32)] + [SDS(g.shape, g.dtype) for g in gts],
                         input_output_aliases={5 + m: 2 + m for m in range(ng)},
                         compiler_params=pltpu.CompilerParams(dimension_semantics=("arbitrary",) * 2,
                                                              vmem_limit_bytes=VMEM_LIMIT, has_side_effects=bool(ng)))(
        qi, kj, q, k, vt, *gts)
    return out[0], out[1], list(out[2:])


def _mla_delta(dycat, ya, name):
    S = ya.shape[0]
    t = min(512, S)

    def body(do_ref, ya_ref, d_ref):
        prod = do_ref[...] * ya_ref[...]
        for p in range(2):
            pt = prod[:, p * 128:(p + 1) * 128].T
            d_ref[2 * p] = jnp.sum(pt[0:64, :], axis=0, keepdims=True)
            d_ref[2 * p + 1] = jnp.sum(pt[64:128, :], axis=0, keepdims=True)

    return _pc(body, name=name, grid=(S // t,),
               in_specs=[BS((t, 256), lambda i: (i, 0)), BS((t, 256), lambda i: (i, 0))],
               out_specs=BS((HEADS, 1, t), lambda i: (0, 0, i)), out_shape=SDS((HEADS, 1, S), F32))(dycat, ya)


def _mla_attn_bwd(q, k, kt, v, dya, lse, delta, rider, name):
    S = q.shape[0]
    t = min(512, S)
    n = S // t
    nr = rider.n if rider else 0

    qi, kj = _causal_steps(n, False)
    last = qi.shape[0] - 1

    def body(qi_ref, kj_ref, q_ref, k_ref, kt_ref, v_ref, do_ref, lse_ref, delta_ref, *rest):
        dqt_ref, dk_ref, dv_ref = rest[nr:nr + 3]
        r_io = (rest[:nr], rest[nr + 3:2 * nr + 3], rest[2 * nr + 3:])
        i, j = qi_ref[pl.program_id(1)], kj_ref[pl.program_id(1)]
        if rider:
            pl.when((pl.program_id(0) == 0) & (pl.program_id(1) == 0))(lambda: rider.start(*r_io))

        @pl.when(pl.program_id(1) == 0)
        def _():
            dqt_ref[...] = jnp.zeros_like(dqt_ref)

        @pl.when(i == j)
        def _():
            dk_ref[...] = jnp.zeros_like(dk_ref)
            dv_ref[...] = jnp.zeros_like(dv_ref)

        def step(masked):
            dob = do_ref[...].astype(BF16)
            cols = pl.ds(pl.multiple_of(i * t, t), t)
            for hh in range(2):
                sl = slice(hh * 128, (hh + 1) * 128)
                qv = q_ref[:, sl]
                p = jnp.exp(_dot_nt(k_ref[:, sl], qv) - lse_ref[hh])
                if masked:
                    key = lax.broadcasted_iota(jnp.int32, (t, t), 0)
                    qry = lax.broadcasted_iota(jnp.int32, (t, t), 1)
                    p = jnp.where(key <= qry, p, 0.0)
                dv_ref[:, sl] += _dot(p.astype(BF16), dob)
                ds = (p * (_dot_nt(v_ref[:, sl], dob) - delta_ref[hh])).astype(BF16)
                dk_ref[:, sl] += _dot(ds, qv)
                dqt_ref[sl, cols] += _dot(kt_ref[sl, :], ds)

        @pl.when(i > j)
        def _():
            step(False)

        @pl.when(i == j)
        def _():
            step(True)

        if rider:
            pl.when((pl.program_id(0) == 1) & (pl.program_id(1) == last))(lambda: rider.wait(*r_io))

    qs = BS((t, 256), lambda p, s, qi, kj: (qi[s], p))
    ks = BS((t, 256), lambda p, s, qi, kj: (kj[s], p))
    rowv = BS((2, 1, t), lambda p, s, qi, kj: (p, 0, qi[s]))
    gs = pltpu.PrefetchScalarGridSpec(
        num_scalar_prefetch=2, grid=(2, qi.shape[0]),
        in_specs=[qs, ks, BS((256, t), lambda p, s, qi, kj: (p, kj[s])), ks,
                  BS((t, 128), lambda p, s, qi, kj: (qi[s], p)), rowv, rowv] + [ANY] * nr,
        out_specs=[BS((256, S), lambda p, s, qi, kj: (p, 0)), ks, ks] + [ANY] * nr,
        scratch_shapes=rider.scratch() if rider else [])
    out = pl.pallas_call(body, name=name, grid_spec=gs,
                         out_shape=[SDS((512, S), F32), SDS((S, 512), F32), SDS((S, 512), F32)]
                         + (rider.out_shape if rider else []),
                         compiler_params=pltpu.CompilerParams(dimension_semantics=("arbitrary",) * 2,
                                                              vmem_limit_bytes=VMEM_LIMIT, has_side_effects=bool(rider)))(
        qi, kj, q, k, kt, v, dya, lse, delta, *(rider.arrs if rider else []))
    return out[0], out[1], out[2], list(out[3:])


def _swa_scores(qm, kk, valid, distf, slope, sink):
    sc = _dot_nt(qm, kk) * SWA_SCALE
    sc = jnp.where(valid, sc - slope * distf, NEG_INF)
    m = jnp.maximum(jnp.max(sc, axis=-1, keepdims=True), sink)
    e = jnp.exp(sc - m)
    esink = jnp.exp(sink - m)
    den = jnp.sum(e, axis=-1, keepdims=True) + esink
    return e / den, esink / den


def _swa_masks():
    r = lax.broadcasted_iota(jnp.int32, (BLK, 2 * BLK), 0)
    c = lax.broadcasted_iota(jnp.int32, (BLK, 2 * BLK), 1)
    dist = r + BLK - c
    return (dist >= 0) & (dist < SWA_WINDOW), c >= BLK, dist.astype(F32)


def _to_half(xb, pos, b):
    return xb if pos == b else pltpu.roll(xb, 64, axis=1)


def _swa(proj, sinks, l, name):
    S = proj.shape[0]
    nb = S // BLK

    def body(q_ref, k_ref, v_ref, sink_ref, o_ref, kp, vp):
        kp[0:BLK, :] = jnp.zeros((BLK, 128), BF16)
        vp[0:BLK, :] = jnp.zeros((BLK, 128), BF16)
        kp[BLK:, :] = k_ref[...].astype(BF16)
        vp[BLK:, :] = v_ref[...].astype(BF16)
        lo = lax.broadcasted_iota(jnp.int32, (BLK, 128), 1) < 64
        band, cur, distf = _swa_masks()

        def blk(i, carry):
            st = pl.multiple_of(i * BLK, BLK)
            kk = kp[pl.ds(st, 2 * BLK), :]
            vv = vp[pl.ds(st, 2 * BLK), :]
            valid = band & (cur | (i > 0))
            for b in range(2):
                half = lo if b == 0 else ~lo
                qb = q_ref[pl.ds(st, BLK), b * 128:(b + 1) * 128]
                outs = []
                for pos in range(2):
                    h = 2 * b + pos
                    qm = jnp.where(half, _to_half(qb, pos, b), 0.0).astype(BF16)
                    p, _ = _swa_scores(qm, kk, valid, distf, SLOPES[h], sink_ref[l, h])
                    outs.append(_to_half(_dot(p.astype(BF16), vv), pos, b))
                o_ref[pl.ds(st, BLK), b * 128:(b + 1) * 128] = jnp.where(lo, outs[0], outs[1])
            return carry

        lax.fori_loop(0, nb, blk, 0, unroll=2)

    return _pc(body, name=name, grid=(1,),
               in_specs=[BS((S, 256), lambda i: (0, C_QS // 256)), BS((S, 128), lambda i: (0, C_KS // 128)),
                         BS((S, 128), lambda i: (0, C_VS // 128)), BS(memory_space=pltpu.SMEM)],
               out_specs=BS((S, 256), lambda i: (0, 0)),
               out_shape=SDS((S, 256), F32),
               scratch=[pltpu.VMEM((S + BLK, 128), BF16), pltpu.VMEM((S + BLK, 128), BF16)])(proj, proj, proj, sinks)


def _swa_bwd(proj, sinks, dyd, l, name, rider=None):
    S = proj.shape[0]
    nb = S // BLK
    nr = rider.n if rider else 0

    def body(q_ref, k_ref, v_ref, sink_ref, do_ref, *rest):
        dq_ref, dk_ref, dv_ref, dsink_ref = rest[nr:nr + 4]
        kp, vp, dkp, dvp = rest[2 * nr + 4:2 * nr + 8]
        r_io = (rest[:nr], rest[nr + 4:2 * nr + 4], rest[2 * nr + 8:])
        if rider:
            rider.start(*r_io)
        kp[0:BLK, :] = jnp.zeros((BLK, 128), BF16)
        vp[0:BLK, :] = jnp.zeros((BLK, 128), BF16)
        kp[BLK:, :] = k_ref[...].astype(BF16)
        vp[BLK:, :] = v_ref[...].astype(BF16)
        dkp[...] = jnp.zeros_like(dkp)
        dvp[...] = jnp.zeros_like(dvp)
        lo = lax.broadcasted_iota(jnp.int32, (BLK, 128), 1) < 64
        lane8 = lax.broadcasted_iota(jnp.int32, (8, 128), 1)
        band, cur, distf = _swa_masks()

        def blk(i, dsink):
            st = pl.multiple_of(i * BLK, BLK)
            kk = kp[pl.ds(st, 2 * BLK), :]
            vv = vp[pl.ds(st, 2 * BLK), :]
            valid = band & (cur | (i > 0))
            dkk = jnp.zeros((2 * BLK, 128), F32)
            dvv = jnp.zeros((2 * BLK, 128), F32)
            for b in range(2):
                half = lo if b == 0 else ~lo
                qb = q_ref[pl.ds(st, BLK), b * 128:(b + 1) * 128]
                dob = do_ref[pl.ds(st, BLK), b * 128:(b + 1) * 128]
                dqs = []
                for pos in range(2):
                    h = 2 * b + pos
                    qm = jnp.where(half, _to_half(qb, pos, b), 0.0).astype(BF16)
                    dom = jnp.where(half, _to_half(dob, pos, b), 0.0).astype(BF16)
                    p, psink = _swa_scores(qm, kk, valid, distf, SLOPES[h], sink_ref[l, h])
                    dp = _dot_nt(dom, vv)
                    dvv = dvv + _dot_tn(p.astype(BF16), dom)
                    delta = jnp.sum(p * dp, axis=-1, keepdims=True)
                    dsink = dsink + jnp.where(lane8 == h, -jnp.sum(psink * delta), 0.0)
                    dsc = (p * (dp - delta) * SWA_SCALE).astype(BF16)
                    dqs.append(_to_half(_dot(dsc, kk), pos, b))
                    dkk = dkk + _dot_tn(dsc, qm)
                dq_ref[pl.ds(st, BLK), b * 128:(b + 1) * 128] = jnp.where(lo, dqs[0], dqs[1]).astype(BF16)
            dkp[pl.ds(st, 2 * BLK), :] += dkk
            dvp[pl.ds(st, 2 * BLK), :] += dvv
            return dsink

        dsink_ref[...] = lax.fori_loop(0, nb, blk, jnp.zeros((8, 128), F32), unroll=2)
        dk_ref[...] = dkp[BLK:, :].astype(BF16)
        dv_ref[...] = dvp[BLK:, :].astype(BF16)
        if rider:
            rider.wait(*r_io)

    in_specs = [BS((S, 256), lambda i: (0, C_QS // 256)), BS((S, 128), lambda i: (0, C_KS // 128)),
                BS((S, 128), lambda i: (0, C_VS // 128)), BS(memory_space=pltpu.SMEM), BS((S, 256), lambda i: (0, 3))]
    out_specs = [BS((S, 256), lambda i: (0, 0)), BS((S, 128), lambda i: (0, 0)), BS((S, 128), lambda i: (0, 0)),
                 BS((8, 128), lambda i: (0, 0))]
    out_shape = [SDS((S, 256), BF16), SDS((S, 128), BF16), SDS((S, 128), BF16), SDS((8, 128), F32)]
    scratch = [pltpu.VMEM((S + BLK, 128), BF16), pltpu.VMEM((S + BLK, 128), BF16),
               pltpu.VMEM((S + BLK, 128), F32), pltpu.VMEM((S + BLK, 128), F32)]
    if not rider:
        return _pc(body, name=name, grid=(1,), in_specs=in_specs, out_specs=out_specs, out_shape=out_shape,
                   scratch=scratch)(proj, proj, proj, sinks, dyd)
    out = pl.pallas_call(body, name=name, grid=(1,), in_specs=in_specs + [ANY] * nr, out_specs=out_specs + [ANY] * nr,
                         out_shape=out_shape + rider.out_shape, scratch_shapes=scratch + rider.scratch(),
                         compiler_params=pltpu.CompilerParams(dimension_semantics=("arbitrary",),
                                                              vmem_limit_bytes=VMEM_LIMIT, has_side_effects=True))(
        proj, proj, proj, sinks, dyd, *rider.arrs)
    return (*out[:4], list(out[4:]))


def _down(x, k, t):
    return jnp.where(t >= k, pltpu.roll(x, k, axis=0), 0.0)


def _up(x, k, t):
    n = x.shape[0]
    return jnp.where(t < n - k, pltpu.roll(x, n - k, axis=0), 0.0)


def _conv(proj, w8, l, name):
    S = proj.shape[0]

    def body(gb_ref, gc_ref, u_ref, w_ref, y_ref):
        t = lax.broadcasted_iota(jnp.int32, (S, 128), 0)
        z = gc_ref[...] * u_ref[...]
        c = w_ref[2:3, :] * z + w_ref[1:2, :] * _down(z, 1, t) + w_ref[0:1, :] * _down(z, 2, t)
        y_ref[...] = gb_ref[...] * c

    col = lambda c0: BS((S, 128), lambda i: (0, c0 // 128 + i))
    return _pc(body, name=name, grid=(2,),
               in_specs=[col(C_GB), col(C_GC), col(C_UC), BS((None, 8, 128), lambda i: (l, 0, i))],
               out_specs=BS((S, 128), lambda i: (0, i)), out_shape=SDS((S, 256), F32))(proj, proj, proj, w8)


def _conv_bwd(proj, w8, dycat, l, name):
    S = proj.shape[0]

    def body(gb_ref, gc_ref, u_ref, w_ref, dy_ref, dgb_ref, dgc_ref, du_ref, dw_ref):
        t = lax.broadcasted_iota(jnp.int32, (S, 128), 0)
        gc, u = gc_ref[...], u_ref[...]
        z = gc * u
        z1, z2 = _down(z, 1, t), _down(z, 2, t)
        w0, w1, w2 = w_ref[0:1, :], w_ref[1:2, :], w_ref[2:3, :]
        dy = dy_ref[...]
        dgb_ref[...] = (dy * (w2 * z + w1 * z1 + w0 * z2)).astype(BF16)
        dc = dy * gb_ref[...]
        dz = w2 * dc + w1 * _up(dc, 1, t) + w0 * _up(dc, 2, t)
        dgc_ref[...] = (dz * u).astype(BF16)
        du_ref[...] = (dz * gc).astype(BF16)
        row = lax.broadcasted_iota(jnp.int32, (8, 128), 0)
        sums = [jnp.sum(dc * zz, axis=0, keepdims=True) for zz in (z2, z1, z)]
        dw_ref[...] = jnp.where(row == 0, sums[0], jnp.where(row == 1, sums[1], jnp.where(row == 2, sums[2], 0.0)))

    col = lambda c0: BS((S, 128), lambda i: (0, c0 // 128 + i))
    out = BS((S, 128), lambda i: (0, i))
    return _pc(body, name=name, grid=(2,),
               in_specs=[col(C_GB), col(C_GC), col(C_UC), BS((None, 8, 128), lambda i: (l, 0, i)), col(256)],
               out_specs=[out, out, out, BS((8, 128), lambda i: (0, i))],
               out_shape=[SDS((S, 256), BF16)] * 3 + [SDS((8, 256), F32)])(proj, proj, proj, w8, dycat)


def _pool_parts(u, t, first):
    lo = lax.broadcasted_iota(jnp.int32, u.shape, 1) < 64
    s2 = u + _down(u, 1, t)
    s4 = s2 + _down(s2, 2, t)
    s8 = s4 + _down(s4, 4, t)
    s16 = s8 + _down(s8, 8, t)
    win = jnp.where(lo, jnp.where(first, s2, s8), jnp.where(first, s4, s16))
    wv = jnp.where(lo, jnp.where(first, 2, 8), jnp.where(first, 4, 16))
    cnt = jnp.minimum(t + 1, wv).astype(F32)
    return win, cnt, lo


def _pool(proj, pwd, scale3, l, name):
    S = proj.shape[0]

    def body(u_ref, pw_ref, sc_ref, y_ref):
        t = lax.broadcasted_iota(jnp.int32, (S, 128), 0)
        u = u_ref[...]
        win, cnt, _ = _pool_parts(u, t, pl.program_id(0) == 0)
        pooled = win / cnt - u
        y_ref[...] = _dot(pooled.astype(BF16), pw_ref[...]) * sc_ref[...]

    return _pc(body, name=name, grid=(2,),
               in_specs=[BS((S, 128), lambda i: (0, C_UP // 128 + i)), BS((None, 128, 128), lambda i: (l, i, 0)),
                         BS((None, 1, 128), lambda i: (l, 0, i))],
               out_specs=BS((S, 128), lambda i: (0, i)), out_shape=SDS((S, 256), F32))(proj, pwd, scale3)


def _pool_bwd(proj, pwd, scale3, dycat, l, name):
    S = proj.shape[0]

    def body(u_ref, pw_ref, sc_ref, dy_ref, du_ref, dpw_ref, dsc_ref):
        t = lax.broadcasted_iota(jnp.int32, (S, 128), 0)
        first = pl.program_id(0) == 0
        u = u_ref[...]
        win, cnt, lo = _pool_parts(u, t, first)
        pooled = (win / cnt - u).astype(BF16)
        pw = pw_ref[...]
        dy = dy_ref[...]
        dsc_ref[...] = jnp.broadcast_to(jnp.sum(dy * _dot(pooled, pw), axis=0, keepdims=True), (8, 128))
        dmb = (dy * sc_ref[...]).astype(BF16)
        dpw_ref[...] = _dot_tn(pooled, dmb)
        dpooled = _dot_nt(dmb, pw)
        a1 = dpooled / cnt
        a2 = a1 + _up(a1, 1, t)
        a4 = a2 + _up(a2, 2, t)
        a8 = a4 + _up(a4, 4, t)
        a16 = a8 + _up(a8, 8, t)
        dwin = jnp.where(lo, jnp.where(first, a2, a8), jnp.where(first, a4, a16))
        du_ref[...] = (dwin - dpooled).astype(BF16)

    return _pc(body, name=name, grid=(2,),
               in_specs=[BS((S, 128), lambda i: (0, C_UP // 128 + i)), BS((None, 128, 128), lambda i: (l, i, 0)),
                         BS((None, 1, 128), lambda i: (l, 0, i)), BS((S, 128), lambda i: (0, 4 + i))],
               out_specs=[BS((S, 128), lambda i: (0, i)), BS((128, 128), lambda i: (i, 0)), BS((8, 128), lambda i: (0, i))],
               out_shape=[SDS((S, 256), BF16), SDS((256, 128), F32), SDS((8, 256), F32)])(proj, pwd, scale3, dycat)


def _adamw(w, g, m, v, name):
    n, a, b = w.shape
    tr = _row_tile(a, b)

    def body(w_ref, g_ref, m_ref, v_ref, d_ref, nm_ref, nv_ref):
        gv = g_ref[...]
        m_new = B1 * m_ref[...] + (1.0 - B1) * gv
        v_new = B2 * v_ref[...] + (1.0 - B2) * (gv * gv)
        m_hat = m_new / (1.0 - B1 ** STEP)
        v_hat = v_new / (1.0 - B2 ** STEP)
        d_ref[...] = -LR * (m_hat / (jnp.sqrt(v_hat) + ADAM_EPS) + WD * w_ref[...])
        nm_ref[...] = m_new
        nv_ref[...] = v_new

    sp = BS((None, tr, b), lambda i, t: (i, t, 0))
    return _pc(body, name=name, grid=(n, a // tr), in_specs=[sp] * 4, out_specs=[sp] * 3,
               out_shape=[SDS((n, a, b), F32)] * 3)(w, g, m, v)


def _prefetch_call(body, name, grid, in_specs, out_specs, out_shape):
    gs = pltpu.PrefetchScalarGridSpec(num_scalar_prefetch=1, grid=grid, in_specs=in_specs, out_specs=out_specs)
    return pl.pallas_call(body, name=name, grid_spec=gs, out_shape=out_shape, compiler_params=_params(len(grid)))


def _place(w, kc, dtype, name):
    _, a, b = w.shape

    def body(kc_ref, w_ref, o_ref):
        o_ref[...] = w_ref[...].astype(dtype)

    return _prefetch_call(body, name, (2,), [BS((None, a, b), lambda l, kc: (l, 0, 0))],
                          BS((None, None, a, b), lambda l, kc: (l, kc[0], 0, 0)), SDS((2, 4, a, b), dtype))(kc, w)


def _pair_sum(g, got, kc, name):
    _, _, a, b = g.shape
    tr = _row_tile(a, b)

    def body(kc_ref, a_ref, b_ref, t32_ref, t16_ref):
        s = a_ref[...] + b_ref[...]
        t16_ref[...] = s.astype(BF16)

        @pl.when(pl.program_id(1) == kc_ref[0])
        def _():
            t32_ref[...] = s

    sp = BS((None, tr, b), lambda t, k, kc: (k, t, 0))
    return _prefetch_call(body, name, (a // tr, 4),
                          [BS((None, None, tr, b), lambda t, k, kc: (kc[1], k, t, 0)), sp],
                          [BS((tr, b), lambda t, k, kc: (t, 0)), sp],
                          [SDS((a, b), F32), SDS((4, a, b), BF16)])(kc, g, got)


def _chip_sum(t32, got3, kc, name):
    a, b = t32.shape
    tr = _row_tile(a, b)

    def body(kc_ref, a_ref, b_ref, u_ref):
        u_ref[...] = ((a_ref[...] + b_ref[0].astype(F32)) + b_ref[1].astype(F32)) + b_ref[2].astype(F32)

    return _prefetch_call(body, name, (a // tr,),
                          [BS((tr, b), lambda t, kc: (t, 0)), BS((3, tr, b), lambda t, kc: (0, t, 0))],
                          BS((None, tr, b), lambda t, kc: (kc[1], t, 0)), SDS((2, a, b), F32))(kc, t32, got3)


def _me():
    return lax.axis_index("x"), lax.axis_index("y"), lax.axis_index("c")


def _other_chips(x, y):
    return [(1 - x, y), (x, 1 - y), (1 - x, 1 - y)]


ANY = BS(memory_space=pl.ANY)
COMM_PARAMS = pltpu.CompilerParams(has_side_effects=True)


def _gather(arrs, name):
    n = len(arrs)

    def body(*refs):
        for phase in _gather_phases(refs[n:2 * n], [a.shape for a in arrs], refs[2 * n], refs[2 * n + 1]):
            phase()

    return pl.pallas_call(body, name=name, out_shape=[SDS(a.shape, a.dtype) for a in arrs],
                          in_specs=[ANY] * n, out_specs=[ANY] * n, input_output_aliases={t: t for t in range(n)},
                          scratch_shapes=[pltpu.SemaphoreType.DMA((7 * n,)), pltpu.SemaphoreType.DMA((7 * n,))],
                          compiler_params=COMM_PARAMS)(*arrs)


def _gather_phases(outs, shapes, send_sems, recv_sems, layer=None):
    n = len(outs)
    cut = [s[2] // 2 // 16 * 16 for s in shapes]
    split = [r > 0 for r in cut]

    def plan():
        x, y, c = _me()
        return (c if layer is None else layer), (x, y), (x, y, c), (x, y, 1 - c), _other_chips(x, y)

    def role(moving, fn):
        if layer is None:
            fn()
        else:
            c = lax.axis_index("c")
            pl.when((c == layer) if moving else (c != layer))(fn)

    def blk(t, chip, layer, half=None):
        r = outs[t].at[layer, 2 * chip[0] + chip[1]]
        if half is None:
            return r
        return r.at[pl.ds(0, cut[t])] if half == 0 else r.at[pl.ds(cut[t], shapes[t][2] - cut[t])]

    def copy(t, k, ref, to):
        return pltpu.make_async_remote_copy(src_ref=ref, dst_ref=ref, send_sem=send_sems.at[7 * t + k],
                                            recv_sem=recv_sems.at[7 * t + k], device_id=to, device_id_type=MESH)

    def own_sends(t):
        c, chip, me, sib, (xn, yn, dg) = plan()
        cps = [copy(t, 0, blk(t, chip, c), (*xn, c)), copy(t, 1, blk(t, chip, c), (*yn, c))]
        return cps if split[t] else cps + [copy(t, 2, blk(t, chip, c), (*dg, c))]

    def relays(t):
        c, chip, me, sib, (xn, yn, dg) = plan()
        after_x = [copy(t, 4, blk(t, xn, c), sib)]
        after_y = [copy(t, 5, blk(t, yn, c), sib)]
        if split[t]:
            after_x.insert(0, copy(t, 2, blk(t, xn, c, 0), (*yn, c)))
            after_y.insert(0, copy(t, 3, blk(t, yn, c, 1), (*xn, c)))
        return after_x, after_y, [copy(t, 6, blk(t, dg, c), sib)]

    def send_own():
        for t in range(n):
            for cp in own_sends(t):
                cp.start()

    def relay_neighbours():
        c, chip, me, sib, (xn, yn, dg) = plan()
        for t in range(n):
            after_x, after_y, _ = relays(t)
            copy(t, 0, blk(t, xn, c), me).wait_recv()
            for cp in after_x:
                cp.start()
            copy(t, 1, blk(t, yn, c), me).wait_recv()
            for cp in after_y:
                cp.start()

    def relay_diagonal():
        c, chip, me, sib, (xn, yn, dg) = plan()
        for t in range(n):
            if split[t]:
                copy(t, 2, blk(t, dg, c, 0), me).wait_recv()
                copy(t, 3, blk(t, dg, c, 1), me).wait_recv()
            else:
                copy(t, 2, blk(t, dg, c), me).wait_recv()
            relays(t)[2][0].start()

    def take_sibling():
        _, chip, me, sib, (xn, yn, dg) = plan()
        theirs = 1 - lax.axis_index("c") if layer is None else layer
        for t in range(n):
            for k, peer in ((4, xn), (5, yn), (6, dg)):
                copy(t, k, blk(t, peer, theirs), me).wait_recv()

    def drain_sends():
        for t in range(n):
            after_x, after_y, after_d = relays(t)
            for cp in own_sends(t) + after_x + after_y + after_d:
                cp.wait_send()

    def phase3():
        role(True, relay_diagonal)
        role(False, take_sibling)
        role(True, drain_sends)

    return (lambda: role(True, send_own)), (lambda: role(True, relay_neighbours)), phase3


def _swap_copies(ins, outs, send_sems, recv_sems):
    x, y, c = _me()
    return [pltpu.make_async_remote_copy(src_ref=ins[t].at[1 - c], dst_ref=outs[t], send_sem=send_sems.at[t],
                                         recv_sem=recv_sems.at[t], device_id=(x, y, 1 - c), device_id_type=MESH)
            for t in range(len(ins))]


def _exchange_copies(ins, outs, send_sems, recv_sems):
    x, y, c = _me()
    return [pltpu.make_async_remote_copy(src_ref=ins[t].at[2 * cx + cy], dst_ref=outs[t].at[j],
                                         send_sem=send_sems.at[3 * t + j], recv_sem=recv_sems.at[3 * t + j],
                                         device_id=(cx, cy, c), device_id_type=MESH)
            for j, (cx, cy) in enumerate(_other_chips(x, y)) for t in range(len(ins))]


class _Rider:
    def __init__(self, arrs, out_shape, nsem, copies):
        self.arrs, self.out_shape, self.nsem, self.copies = list(arrs), out_shape, nsem, copies
        self.n = len(self.arrs)

    def scratch(self):
        return [pltpu.SemaphoreType.DMA((self.nsem,)), pltpu.SemaphoreType.DMA((self.nsem,))]

    def start(self, ins, outs, sems):
        for cp in self.copies(ins, outs, *sems):
            cp.start()

    def wait(self, ins, outs, sems):
        for cp in self.copies(ins, outs, *sems):
            cp.wait()


def _swap_rider(gs):
    return _Rider(gs, [SDS(g.shape[1:], g.dtype) for g in gs], len(gs), _swap_copies)


def _exchange_rider(ts):
    return _Rider(ts, [SDS((3,) + t.shape[1:], t.dtype) for t in ts], 3 * len(ts), _exchange_copies)


def _ride_alone(rider, name):
    n = rider.n

    def body(*refs):
        rider.start(refs[:n], refs[n:2 * n], refs[2 * n:])
        rider.wait(refs[:n], refs[n:2 * n], refs[2 * n:])

    return pl.pallas_call(body, name=name, out_shape=rider.out_shape, in_specs=[ANY] * n, out_specs=[ANY] * n,
                          scratch_shapes=rider.scratch(), compiler_params=COMM_PARAMS)(*rider.arrs)


def _join_layers(us, name):
    n = len(us)

    def body(*refs):
        outs, send_sems, recv_sems = refs[n:2 * n], refs[2 * n], refs[2 * n + 1]
        x, y, c = _me()
        cps = [pltpu.make_async_remote_copy(src_ref=outs[t].at[c], dst_ref=outs[t].at[c], send_sem=send_sems.at[t],
                                            recv_sem=recv_sems.at[t], device_id=(x, y, 1 - c), device_id_type=MESH)
               for t in range(n)]
        for cp in cps:
            cp.start()
        for cp in cps:
            cp.wait()

    return pl.pallas_call(body, name=name, out_shape=[SDS(u.shape, u.dtype) for u in us],
                          in_specs=[ANY] * n, out_specs=[ANY] * n, input_output_aliases={t: t for t in range(n)},
                          scratch_shapes=[pltpu.SemaphoreType.DMA((n,)), pltpu.SemaphoreType.DMA((n,))],
                          compiler_params=COMM_PARAMS)(*us)


def _allsum_small(v, name, rider=None):
    M = v.shape[0]
    nr = rider.n if rider else 0

    def body(x_ref, *rest):
        o_ref = rest[nr]
        all_ref, send_sems, recv_sems, local_sem = rest[2 * nr + 1:2 * nr + 5]
        r_io = (rest[:nr], rest[nr + 1:2 * nr + 1], rest[2 * nr + 5:])
        x, y, c = _me()
        me, sib = (x, y, c), (x, y, 1 - c)
        chips = _other_chips(x, y)

        def rows(px, py, pc):
            return all_ref.at[pl.ds((4 * px + 2 * py + pc) * M, M), :]

        def copy(k, block, to, src=None):
            return pltpu.make_async_remote_copy(src_ref=rows(*block) if src is None else src, dst_ref=rows(*block),
                                                send_sem=send_sems.at[k], recv_sem=recv_sems.at[k],
                                                device_id=to, device_id_type=MESH)

        mine = pltpu.make_async_copy(x_ref, rows(*me), local_sem)
        mine.start()
        first = [copy(0, me, sib, src=x_ref)]
        first += [copy(1 + j, me, (*chip, c), src=x_ref) for j, chip in enumerate(chips)]
        for cp in first:
            cp.start()
        if rider:
            rider.start(*r_io)
        passed = [copy(4 + j, (*chip, c), sib) for j, chip in enumerate(chips)]
        for j, chip in enumerate(chips):
            copy(1 + j, (*chip, c), me).wait_recv()
            passed[j].start()
        copy(0, sib, me).wait_recv()
        for j, chip in enumerate(chips):
            copy(4 + j, (*chip, 1 - c), me).wait_recv()
        for cp in first + passed:
            cp.wait_send()
        mine.wait()
        acc = all_ref[0:M, :]
        for d in range(1, 8):
            acc = acc + all_ref[d * M:(d + 1) * M, :]
        o_ref[...] = acc
        if rider:
            rider.wait(*r_io)

    vm = BS(memory_space=pltpu.VMEM)
    out = pl.pallas_call(body, name=name, out_shape=[SDS((M, LANES), F32)] + (rider.out_shape if rider else []),
                         in_specs=[vm] + [ANY] * nr, out_specs=[vm] + [ANY] * nr,
                         scratch_shapes=[pltpu.VMEM((8 * M, LANES), F32), pltpu.SemaphoreType.DMA((7,)),
                                         pltpu.SemaphoreType.DMA((7,)), pltpu.SemaphoreType.DMA]
                         + (rider.scratch() if rider else []),
                         compiler_params=pltpu.CompilerParams(has_side_effects=True, vmem_limit_bytes=VMEM_LIMIT))(
        v, *(rider.arrs if rider else []))
    return out[0], list(out[1:])


FFN = ("w_gate_up", "w_down")
REST = ("w_in", "w_o", "w_uq", "w_ukv")
BIG = FFN + REST
TINY = ("conv_w",)
REPL = ("attn_norm", "mla_q_norm", "mla_kv_norm", "pool_w", "pool_scale", "swa_sinks", "mix_norm", "ffn_norm",
        "final_norm")
ORDER = ("attn_norm", "w_in", "mla_q_norm", "w_uq", "mla_kv_norm", "w_ukv", "conv_w", "pool_w", "pool_scale",
         "swa_sinks", "mix_norm", "w_o", "ffn_norm", "w_gate_up", "w_down", "final_norm")


def _rows8(shape):
    return -(-int(np.prod(shape)) // (8 * LANES)) * 8


def _pack(arrs):
    parts = []
    for a in arrs:
        r = _rows8(a.shape)
        parts.append(jnp.pad(a.reshape(-1), (0, r * LANES - a.size)).reshape(r, LANES))
    return jnp.concatenate(parts, axis=0)


def _unpack(buf, shapes):
    out, r0 = [], 0
    for s in shapes:
        n, r = int(np.prod(s)), _rows8(s)
        rows = buf[r0:r0 + r]
        out.append(rows.reshape(s) if n == r * LANES else rows.reshape(-1)[:n].reshape(s))
        r0 += r
    return out


def _cols_joined(g):
    return jnp.transpose(g, (0, 2, 1, 3)).reshape(g.shape[0], g.shape[2], 4 * g.shape[3])


def _cols_split(w):
    n, a, b4 = w.shape
    return jnp.transpose(w.reshape(n, a, 4, b4 // 4), (0, 2, 1, 3))


def _rope_tables(S):
    inv = 1.0 / (10000.0 ** (jnp.arange(0, 32, 2, dtype=F32) / 32))
    ang = jnp.arange(S, dtype=F32)[:, None] * inv[None, :]
    cos, sin = jnp.cos(ang), jnp.sin(ang)
    z = lambda w: jnp.zeros((S, w), F32)
    tc = jnp.concatenate([jnp.ones((S, 64), F32), cos, cos, jnp.ones((S, 32), F32)], axis=1)
    ts1 = jnp.concatenate([z(64), -sin, z(48)], axis=1)
    ts2 = jnp.concatenate([z(80), sin, z(32)], axis=1)
    return tc, ts1, ts2


def _pad_w_in(wt):
    z = lambda n: jnp.zeros((wt.shape[0], n, wt.shape[2]), wt.dtype)
    return jnp.concatenate([wt[:, 0:384], z(64), wt[:, 384:416], z(32), wt[:, 416:1952]], axis=1)


def _unpad_w_in(dt):
    return jnp.concatenate([dt[:, 0:384], dt[:, 448:480], dt[:, 512:2048]], axis=1)


def _pad_heads(w, src, offs):
    cols = []
    for h in range(HEADS):
        src0, n = src[h]
        z = lambda k: jnp.zeros(w.shape[:-1] + (k,), w.dtype)
        cols += [z(offs[h]), w[..., src0:src0 + n], z(128 - offs[h] - n)]
    return jnp.concatenate(cols, axis=-1)


UQ_SRC = [(h * 96, 96) for h in range(HEADS)]
KN_SRC = [(h * 128, 64) for h in range(HEADS)]
V_SRC = [(h * 128 + 64, 64) for h in range(HEADS)]
ZERO_OFF = [0] * HEADS
V_OFF = [(h % 2) * 64 for h in range(HEADS)]


def _unpad_heads(d, src, offs):
    return [d[..., h * 128 + offs[h]: h * 128 + offs[h] + src[h][1]] for h in range(HEADS)]


def kernel(x, attn_norm, w_in, mla_q_norm, w_uq, mla_kv_norm, w_ukv, conv_w, pool_w, pool_scale, swa_sinks, mix_norm, w_o, ffn_norm, w_gate_up, w_down, final_norm, loss_target, m_attn_norm, m_w_in, m_mla_q_norm, m_w_uq, m_mla_kv_norm, m_w_ukv, m_conv_w, m_pool_w, m_pool_scale, m_swa_sinks, m_mix_norm, m_w_o, m_ffn_norm, m_w_gate_up, m_w_down, m_final_norm, v_attn_norm, v_w_in, v_mla_q_norm, v_w_uq, v_mla_kv_norm, v_w_ukv, v_conv_w, v_pool_w, v_pool_scale, v_swa_sinks, v_mix_norm, v_w_o, v_ffn_norm, v_w_gate_up, v_w_down, v_final_norm):
    W = dict(attn_norm=attn_norm, w_in=w_in, mla_q_norm=mla_q_norm, w_uq=w_uq, mla_kv_norm=mla_kv_norm, w_ukv=w_ukv,
             conv_w=conv_w, pool_w=pool_w, pool_scale=pool_scale, swa_sinks=swa_sinks, mix_norm=mix_norm, w_o=w_o,
             ffn_norm=ffn_norm, w_gate_up=w_gate_up, w_down=w_down, final_norm=final_norm)
    M1 = dict(attn_norm=m_attn_norm, w_in=m_w_in, mla_q_norm=m_mla_q_norm, w_uq=m_w_uq, mla_kv_norm=m_mla_kv_norm,
              w_ukv=m_w_ukv, conv_w=m_conv_w, pool_w=m_pool_w, pool_scale=m_pool_scale, swa_sinks=m_swa_sinks,
              mix_norm=m_mix_norm, w_o=m_w_o, ffn_norm=m_ffn_norm, w_gate_up=m_w_gate_up, w_down=m_w_down,
              final_norm=m_final_norm)
    V2 = dict(attn_norm=v_attn_norm, w_in=v_w_in, mla_q_norm=v_mla_q_norm, w_uq=v_w_uq, mla_kv_norm=v_mla_kv_norm,
              w_ukv=v_w_ukv, conv_w=v_conv_w, pool_w=v_pool_w, pool_scale=v_pool_scale, swa_sinks=v_swa_sinks,
              mix_norm=v_mix_norm, w_o=v_w_o, ffn_norm=v_ffn_norm, w_gate_up=v_w_gate_up, w_down=v_w_down,
              final_norm=v_final_norm)
    S = x.shape[1]
    xc, yc, cc = _me()
    chip = 2 * xc + yc
    kc = jnp.stack([chip, cc]).astype(jnp.int32)

    first, later = ("w_in", "w_uq", "w_ukv", "conv_w"), ("w_o", "w_gate_up", "w_down")
    T = lambda a: jnp.swapaxes(a, 1, 2)
    W["w_in"], M1["w_in"], V2["w_in"] = T(w_in), T(m_w_in), T(v_w_in)
    placed = {n: _place(W[n], kc, F32 if n == "conv_w" else BF16, f"place_{n}") for n in first + later}
    gi, gq, gkv, gcv = _gather([placed[n] for n in first], "gather_weights")
    later_w = [placed[n] for n in later]
    win_p = _pad_w_in(gi.reshape(2, 4 * gi.shape[2], D))
    wuq_p = _pad_heads(_cols_joined(gq), UQ_SRC, ZERO_OFF)
    wukv = _cols_joined(gkv)
    wk_p = _pad_heads(wukv, KN_SRC, ZERO_OFF)
    wv_p = _pad_heads(wukv, V_SRC, V_OFF)
    conv8 = jnp.pad(_cols_joined(gcv), ((0, 0), (0, 5), (0, 0)))
    pwd = jnp.concatenate([jnp.concatenate(
        [jnp.pad(pool_w[:, 2 * b], ((0, 0), (0, 0), (0, 64))), jnp.pad(pool_w[:, 2 * b + 1], ((0, 0), (0, 0), (64, 0)))],
        axis=1) for b in range(2)], axis=1).astype(BF16)
    tabs = _rope_tables(S)
    g_attn, g_q, g_kv, g_mix, g_ffn, g_ps = (_g3(W[n]) for n in ("attn_norm", "mla_q_norm", "mla_kv_norm", "mix_norm",
                                                                  "ffn_norm", "pool_scale"))

    xs = [x[0]]
    saved = []
    for l in range(DEPTH):
        x0 = xs[-1]
        proj, h = _norm_mm(x0, g_attn, l, win_p, _wspec_in(l), D_INP, D_INP, F32, f"in_proj{l}", w_t=True)
        q, k, v, kt, vt = _mla_prep(proj, g_q, g_kv, wuq_p, wk_p, wv_p, tabs, l, f"mla_prep{l}")
        ya, lse, later_w = _mla_attn(q, k, vt, later_w, l, f"mla_attn{l}")
        go, gu4, gd = later_w
        wo, wdown = go.reshape(2, D, D), gd.reshape(2, D_FF, D)
        yb = _conv(proj, conv8, l, f"conv{l}")
        ycp = _pool(proj, pwd, g_ps, l, f"pool{l}")
        yd = _swa(proj, swa_sinks, l, f"swa{l}")
        x1, ycat, mixed = _mix_out(x0, ya, yb, ycp, yd, g_mix, wo, l, f"mix_out{l}")
        gu, h2 = _norm_mm(x1, g_ffn, l, gu4, _wspec_gu(l), 2 * D_FF, 2 * D_FF // 4, BF16, f"gate_up{l}")
        x2, act = _swiglu_mm_res(x1, gu, wdown, l, f"down{l}")
        saved.append(dict(x0=x0, proj=proj, h=h, q=q, k=k, kt=kt, v=v, lse=lse, x1=x1, ycat=ycat, mixed=mixed,
                          gu=gu, h2=h2, act=act))
        xs.append(x2)

    dx, dx16, dg_final, loss_tile = _loss_head(xs[-1], final_norm.reshape(1, D), loss_target[0], "loss_head")
    loss = lax.psum(loss_tile[0, 0] * (0.5 / D), ("x", "y", "c"))

    G = {n: [None] * DEPTH for n in ("w_uq", "w_ukv") + TINY + REPL if n != "final_norm"}
    gw_in = gw_o = gw_gu = gw_down = None
    for l in reversed(range(DEPTH)):
        sv = saved[l]
        dgu = _bwd_down(dx16, wdown, sv["gu"], l, f"down_bwd{l}")
        gw_down = _mm_tn(sv["act"], dx16, l, gw_down, f"dw_down{l}")
        gw_gu = _mm_tn(sv["h2"], dgu, l, gw_gu, f"dw_gate_up{l}", split4=True)
        exchange_gu = exchange_down = None
        if l == 0:
            g_f = [gw_gu, gw_down.reshape(2, 4, D_FF // 4, D)]
            dx1, dx1_16, dg, got_f = _mm_nt_normbwd(dgu, gu4, l, sv["x1"], g_ffn, dx, 1, f"gate_up_bwd{l}",
                                                    rider=_swap_rider(g_f))
            pairs_f = [_pair_sum(g, o, kc, f"rs_pair_sum_{n}") for g, o, n in zip(g_f, got_f, FFN)]
            exchange_gu, exchange_down = _exchange_rider([pairs_f[0][1]]), _exchange_rider([pairs_f[1][1]])
        else:
            dx1, dx1_16, dg = _mm_nt_normbwd(dgu, gu4, l, sv["x1"], g_ffn, dx, 1, f"gate_up_bwd{l}")
        G["ffn_norm"][l] = dg[0]
        gw_o = _mm_tn(sv["mixed"], dx1_16, l, gw_o, f"dw_o{l}")
        dycat, dg = _mm_nt_normbwd(dx1_16, wo.reshape(2, 1, D, D), l, sv["ycat"], g_mix, None, 4, f"mix_bwd{l}")
        G["mix_norm"][l] = dg[0]

        proj = sv["proj"]
        delta = _mla_delta(dycat, sv["ycat"], f"mla_delta{l}")
        dq, dk, dv, got3_gu = _mla_attn_bwd(sv["q"], sv["k"], sv["kt"], sv["v"], dycat, sv["lse"], delta, exchange_gu,
                                            f"mla_attn_bwd{l}")
        dcq, dckv, dkr, dwuq, dwk, dwv, dgq, dgkv = _mla_prep_bwd(
            dq, dk, dv, proj, g_q, g_kv, wuq_p, wk_p, wv_p, tabs, l, f"mla_prep_bwd{l}")
        dgb, dgc, duc, dcw = _conv_bwd(proj, conv8, dycat, l, f"conv_bwd{l}")
        dup, dpw, dps = _pool_bwd(proj, pwd, g_ps, dycat, l, f"pool_bwd{l}")
        if l == 0:
            dqs, dks, dvs, dsink, got3_down = _swa_bwd(proj, swa_sinks, dycat, l, f"swa_bwd{l}", rider=exchange_down)
            got3_f = got3_gu + got3_down
        else:
            dqs, dks, dvs, dsink = _swa_bwd(proj, swa_sinks, dycat, l, f"swa_bwd{l}")
        dproj = jnp.concatenate([dcq, dckv, dkr, dgb, dgc, duc, dup, dqs, dks, dvs], axis=1)
        gw_in = _mm_tn(dproj, sv["h"], l, gw_in, f"dw_in{l}")
        G["w_uq"][l] = jnp.concatenate(_unpad_heads(dwuq, UQ_SRC, ZERO_OFF), axis=1)
        kn, vv = _unpad_heads(dwk, KN_SRC, ZERO_OFF), _unpad_heads(dwv, V_SRC, V_OFF)
        G["w_ukv"][l] = jnp.concatenate([t for h in range(HEADS) for t in (kn[h], vv[h])], axis=1)
        dx, dx16, dg = _mm_nt_normbwd(dproj, win_p.reshape(2, 1, D_INP, D), l, sv["x0"], g_attn, dx1, 1, f"in_proj_bwd{l}",
                                      w_t=True)
        G["attn_norm"][l] = dg[0]
        G["mla_q_norm"][l] = dgq[0]
        G["mla_kv_norm"][l] = dgkv[0]
        G["conv_w"][l] = dcw[0:3]
        G["pool_w"][l] = jnp.stack([dpw[0:64, 0:64], dpw[64:128, 64:128], dpw[128:192, 0:64], dpw[192:256, 64:128]])
        G["pool_scale"][l] = dps[0]
        G["swa_sinks"][l] = dsink[0, 0:4]
    grad_x = dx[None]
    Gl = {n: jnp.stack(G[n]) for n in TINY + REPL if n != "final_norm"}
    Gl["final_norm"] = dg_final[0]

    us_f = [_chip_sum(p[0], o3, kc, f"rs_chip_sum_{n}") for p, o3, n in zip(pairs_f, got3_f, FFN)]
    gsum_f = _join_layers(us_f, "rs_join_cores_ffn")
    g_r = [_unpad_w_in(gw_in).reshape(2, 4, -1, D), gw_o.reshape(2, 4, D // 4, D), _cols_split(jnp.stack(G["w_uq"])),
           _cols_split(jnp.stack(G["w_ukv"]))]
    got_r = _ride_alone(_swap_rider(g_r), "rs_swap_cores")
    pairs_r = [_pair_sum(g, o, kc, f"rs_pair_sum_{n}") for g, o, n in zip(g_r, got_r, REST)]
    small = TINY + REPL
    full_shapes = [Gl[n].shape for n in small]
    summed, got3_r = _allsum_small(_pack([Gl[n] for n in small]), "allsum_small",
                                   rider=_exchange_rider([p[1] for p in pairs_r]))
    summed = _unpack(summed, full_shapes)
    us_r = [_chip_sum(p[0], o3, kc, f"rs_chip_sum_{n}") for p, o3, n in zip(pairs_r, got3_r, REST)]
    gsum_r = _join_layers(us_r, "rs_join_cores")
    res = {}
    for n, g in zip(BIG, gsum_f + gsum_r):
        d_, m_, v_ = _adamw(W[n], g, M1[n], V2[n], f"adamw_{n}")
        back = T if n == "w_in" else (lambda a: a)
        res["g", n], res["d", n], res["m", n], res["v", n] = back(g), back(d_), back(m_), back(v_)

    def as3(a):
        if a.ndim <= 2:
            return a.reshape((1,) * (3 - a.ndim) + a.shape)
        return a.reshape(a.shape[0], -1, a.shape[-1])

    for n, g in zip(small, summed):
        if n in TINY:
            wdt = W[n].shape[2]
            g = lax.dynamic_slice_in_dim(g, chip * wdt, wdt, axis=2)
        out = _adamw(as3(W[n]), as3(g), as3(M1[n]), as3(V2[n]), f"adamw_{n}")
        res["g", n] = g
        res["d", n], res["m", n], res["v", n] = (o.reshape(W[n].shape) for o in out)

    return (loss, grad_x, *[res["g", n] for n in ORDER], *[res["d", n] for n in ORDER],
            *[res["m", n] for n in ORDER], *[res["v", n] for n in ORDER])
```

```python
import math

import numpy as np
import jax
import jax.numpy as jnp
from jax import lax
from jax.experimental import pallas as pl
from jax.experimental.pallas import tpu as pltpu

F32, BF16 = jnp.float32, jnp.bfloat16
SDS = jax.ShapeDtypeStruct
BS = pl.BlockSpec
MESH = pl.DeviceIdType.MESH

D = 1024
DEPTH = 2
HEADS = 4
D_FF = 2816
D_INP = 2048
EPS = 1e-6
SWA_WINDOW = 128
BLK = 128
SLOPES = tuple(2.0 ** (-8.0 * (i + 1) / 4) for i in range(4))
QK_SCALE = 1.0 / math.sqrt(96)
SWA_SCALE = 1.0 / math.sqrt(64)
LR, B1, B2, ADAM_EPS, WD, STEP = 0.001, 0.9, 0.999, 1e-08, 0.01, 10

LANES = 1024
VMEM_LIMIT = 56 * 1024 * 1024
NEG_INF = float("-inf")

C_CQ, C_CKV, C_KR, C_GB, C_GC, C_UC, C_UP, C_QS, C_KS, C_VS = 0, 256, 384, 512, 768, 1024, 1280, 1536, 1792, 1920


def _params(ngrid):
    return pltpu.CompilerParams(dimension_semantics=("arbitrary",) * ngrid, vmem_limit_bytes=VMEM_LIMIT)


def _pc(body, *, name, grid, in_specs, out_specs, out_shape, scratch=(), aliases=None):
    return pl.pallas_call(
        body, name=name, grid=grid, in_specs=in_specs, out_specs=out_specs, out_shape=out_shape,
        scratch_shapes=scratch, input_output_aliases=aliases or {}, compiler_params=_params(len(grid)))


def _dot(a, b):
    return jnp.dot(a, b, preferred_element_type=F32)


def _dot_nt(a, b):
    return lax.dot_general(a, b, (((1,), (1,)), ((), ())), preferred_element_type=F32)


def _dot_tn(a, b):
    return lax.dot_general(a, b, (((0,), (0,)), ((), ())), preferred_element_type=F32)


def _tile(n, cap):
    if n <= cap:
        return n
    t = cap - cap % 128
    while n % t:
        t -= 128
    return t


def _row_tile(a, b, cap=262144):
    bp = -(-b // 128) * 128
    best = None
    for t in range(8, a + 1, 8):
        if a % t == 0 and t * bp <= cap:
            best = t
    if best is None or (best < 64 and a * bp <= 2 * cap):
        return a
    return best


def _g3(a):
    return a.reshape(a.shape[0], 1, a.shape[1])


def _norm_mm(x, g3, l, w, wspec, N, tn, out_dtype, name, w_t=False):
    S, K = x.shape
    tm = min(1024 if out_dtype == BF16 else 512, S)

    def body(x_ref, g_ref, w_ref, y_ref, h_ref):
        @pl.when(pl.program_id(1) == 0)
        def _():
            xv = x_ref[...]
            r = lax.rsqrt(jnp.mean(xv * xv, axis=-1, keepdims=True) + EPS)
            h_ref[...] = (xv * r * g_ref[...]).astype(BF16)

        y_ref[...] = (_dot_nt if w_t else _dot)(h_ref[...], w_ref[...]).astype(out_dtype)

    return _pc(body, name=name, grid=(S // tm, N // tn),
               in_specs=[BS((tm, K), lambda i, j: (i, 0)), BS((None, 1, K), lambda i, j: (l, 0, 0)), wspec],
               out_specs=[BS((tm, tn), lambda i, j: (i, j)), BS((tm, K), lambda i, j: (i, 0))],
               out_shape=[SDS((S, N), out_dtype), SDS((S, K), BF16)])(x, g3, w)


def _wspec_in(l):
    return BS((None, D_INP, D), lambda i, j: (l, j, 0))


def _wspec_gu(l):
    return BS((None, None, D, 2 * D_FF // 4), lambda i, j: (l, j, 0, 0))


def _mix_out(x0, ya, yb, yc, yd, gmix3, wo, l, name):
    S = x0.shape[0]
    tm = min(512, S)

    def body(x_ref, ya_ref, yb_ref, yc_ref, yd_ref, g_ref, w_ref, x1_ref, ycat_ref, mixed_ref):
        groups = [ya_ref[...], yb_ref[...], yc_ref[...], yd_ref[...]]
        for gi, yg in enumerate(groups):
            sl = slice(gi * 256, (gi + 1) * 256)
            r = lax.rsqrt(jnp.mean(yg * yg, axis=-1, keepdims=True) + EPS)
            ycat_ref[:, sl] = yg
            mixed_ref[:, sl] = (yg * r * g_ref[:, sl]).astype(BF16)
        x1_ref[...] = x_ref[...] + _dot(mixed_ref[...], w_ref[...])

    row = lambda w: BS((tm, w), lambda i: (i, 0))
    return _pc(body, name=name, grid=(S // tm,),
               in_specs=[row(D), row(256), row(256), row(256), row(256), BS((None, 1, D), lambda i: (l, 0, 0)),
                         BS((None, D, D), lambda i: (l, 0, 0))],
               out_specs=[row(D), row(D), row(D)],
               out_shape=[SDS((S, D), F32), SDS((S, D), F32), SDS((S, D), BF16)])(x0, ya, yb, yc, yd, gmix3, wo)


def _swiglu_mm_res(x1, gu, wdown, l, name):
    S = x1.shape[0]
    tm = min(256, S)

    def body(x_ref, gate_ref, up_ref, w_ref, x2_ref, act_ref):
        acc = x_ref[...]
        for c0 in range(0, D_FF, D_FF // 2):
            cs = slice(c0, c0 + D_FF // 2)
            gt = gate_ref[:, cs].astype(F32)
            act = (gt * pl.reciprocal(1.0 + jnp.exp(-gt), approx=True) * up_ref[:, cs].astype(F32)).astype(BF16)
            act_ref[:, cs] = act
            acc = acc + _dot(act, w_ref[cs, :])
        x2_ref[...] = acc

    return _pc(body, name=name, grid=(S // tm,),
               in_specs=[BS((tm, D), lambda i: (i, 0)), BS((tm, D_FF), lambda i: (i, 0)),
                         BS((tm, D_FF), lambda i: (i, 1)), BS((None, D_FF, D), lambda i: (l, 0, 0))],
               out_specs=[BS((tm, D), lambda i: (i, 0)), BS((tm, D_FF), lambda i: (i, 0))],
               out_shape=[SDS((S, D), F32), SDS((S, D_FF), BF16)])(x1, gu, gu, wdown)


def _loss_head(x, g, tgt, name):
    S = x.shape[0]
    tm = min(512, S)

    def body(x_ref, g_ref, t_ref, dx_ref, dx16_ref, dg_ref, loss_ref):
        @pl.when(pl.program_id(0) == 0)
        def _():
            dg_ref[...] = jnp.zeros_like(dg_ref)
            loss_ref[...] = jnp.zeros_like(loss_ref)

        xv = x_ref[...]
        r = lax.rsqrt(jnp.mean(xv * xv, axis=-1, keepdims=True) + EPS)
        xh = xv * r
        gv = g_ref[...]
        diff = xh * gv - t_ref[...]
        loss_ref[...] += jnp.sum(diff * diff)
        dy = diff * (1.0 / D)
        dg_ref[...] += jnp.sum(dy * xh, axis=0, keepdims=True)
        dxh = dy * gv
        dx = r * (dxh - xh * jnp.mean(dxh * xh, axis=-1, keepdims=True))
        dx_ref[...] = dx
        dx16_ref[...] = dx.astype(BF16)

    row = BS((tm, D), lambda i: (i, 0))
    return _pc(body, name=name, grid=(S // tm,),
               in_specs=[row, BS((1, D), lambda i: (0, 0)), row],
               out_specs=[row, row, BS((8, D), lambda i: (0, 0)), BS((8, 128), lambda i: (0, 0))],
               out_shape=[SDS((S, D), F32), SDS((S, D), BF16), SDS((8, D), F32), SDS((8, 128), F32)])(x, g, tgt)


def _mm_tn(a, b, l, prev, name, split4=False):
    S, Ka = a.shape
    N = b.shape[1]
    if split4:
        ta, tn = _tile(Ka, 256), N // 4
        out_shape = SDS((2, 4, Ka, tn), F32)
        out_spec = BS((None, None, ta, tn), lambda j, i: (l, j, i, 0))
    else:
        ta, tn = _tile(Ka, 512), _tile(N, 1024)
        out_shape = SDS((2, Ka, N), F32)
        out_spec = BS((None, ta, tn), lambda j, i: (l, i, j))

    def body(a_ref, b_ref, *rest):
        rest[-1][...] = _dot_tn(a_ref[...], b_ref[...])

    in_specs = [BS((S, ta), lambda j, i: (0, i)), BS((S, tn), lambda j, i: (0, j))]
    args = [a, b]
    if prev is not None:
        in_specs.append(BS(memory_space=pl.ANY))
        args.append(prev)
    return _pc(body, name=name, grid=(N // tn, Ka // ta), in_specs=in_specs, out_specs=out_spec, out_shape=out_shape,
               aliases={2: 0} if prev is not None else None)(*args)


def _bwd_down(dx16, wdown, gu, l, name):
    S = dx16.shape[0]
    tm = min(256, S)

    def body(dx_ref, w_ref, gate_ref, up_ref, dgu_ref):
        dxv = dx_ref[...]
        for c0 in range(0, D_FF, 256):
            cs = slice(c0, c0 + 256)
            dact = _dot_nt(dxv, w_ref[cs, :])
            gt = gate_ref[:, cs].astype(F32)
            sg = pl.reciprocal(1.0 + jnp.exp(-gt), approx=True)
            dgu_ref[:, cs] = (dact * up_ref[:, cs].astype(F32) * (sg * (1.0 + gt * (1.0 - sg)))).astype(BF16)
            dgu_ref[:, D_FF + c0:D_FF + c0 + 256] = (dact * (gt * sg)).astype(BF16)

    return _pc(body, name=name, grid=(S // tm,),
               in_specs=[BS((tm, D), lambda i: (i, 0)), BS((None, D_FF, D), lambda i: (l, 0, 0)),
                         BS((tm, D_FF), lambda i: (i, 0)), BS((tm, D_FF), lambda i: (i, 1))],
               out_specs=BS((tm, 2 * D_FF), lambda i: (i, 0)),
               out_shape=SDS((S, 2 * D_FF), BF16))(dx16, wdown, gu, gu)


def _mm_nt_normbwd(dy, w4, l, x, g3, dres, ngroups, name, rider=None, w_t=False):
    S, K = dy.shape
    nk, kc = w4.shape[1], w4.shape[2 if w_t else 3]
    mm = _dot if w_t else _dot_nt
    tm = min(512, S)
    gw = D // ngroups
    has_res = dres is not None
    nr = rider.n if rider else 0
    n_in, n_out = 4 + has_res, 2 + has_res

    def body(*refs):
        dy_ref, w_ref, x_ref, g_ref = refs[:4]
        res_ref = refs[4] if has_res else None
        outs = refs[n_in + nr:n_in + nr + n_out]
        dx_ref, dg_ref = outs[0], outs[-1]
        dx16_ref = outs[1] if has_res else None
        r_io = (refs[n_in:n_in + nr], refs[n_in + nr + n_out:n_in + 2 * nr + n_out], refs[n_in + 2 * nr + n_out:])
        if rider:
            pl.when(pl.program_id(0) == 0)(lambda: rider.start(*r_io))

        @pl.when(pl.program_id(0) == 0)
        def _():
            dg_ref[...] = jnp.zeros_like(dg_ref)

        dh = mm(dy_ref[:, 0:kc], w_ref[0])
        for k in range(1, nk):
            dh = dh + mm(dy_ref[:, k * kc:(k + 1) * kc], w_ref[k])
        for gi in range(ngroups):
            sl = slice(gi * gw, (gi + 1) * gw)
            xg = x_ref[:, sl]
            r = lax.rsqrt(jnp.mean(xg * xg, axis=-1, keepdims=True) + EPS)
            xh = xg * r
            dhg = dh[:, sl]
            dg_ref[:, sl] += jnp.sum(dhg * xh, axis=0, keepdims=True)
            dxh = dhg * g_ref[:, sl]
            dxg = r * (dxh - xh * jnp.mean(dxh * xh, axis=-1, keepdims=True))
            if has_res:
                dxg = dxg + res_ref[:, sl]
                dx16_ref[:, sl] = dxg.astype(BF16)
            dx_ref[:, sl] = dxg
        if rider:
            pl.when(pl.program_id(0) == S // tm - 1)(lambda: rider.wait(*r_io))

    row = BS((tm, D), lambda i: (i, 0))
    in_specs = [BS((tm, K), lambda i: (i, 0)),
                BS((None,) + tuple(w4.shape[1:]), lambda i: (l, 0, 0, 0), pipeline_mode=pl.Buffered(1)), row,
                BS((None, 1, D), lambda i: (l, 0, 0))]
    args = [dy, w4, x, g3]
    out_specs, out_shape = [row], [SDS((S, D), F32)]
    if has_res:
        in_specs.append(row)
        args.append(dres)
        out_specs.append(row)
        out_shape.append(SDS((S, D), BF16))
    out_specs.append(BS((8, D), lambda i: (0, 0)))
    out_shape.append(SDS((8, D), F32))
    if not rider:
        return _pc(body, name=name, grid=(S // tm,), in_specs=in_specs, out_specs=out_specs, out_shape=out_shape)(*args)
    out = pl.pallas_call(body, name=name, grid=(S // tm,), in_specs=in_specs + [ANY] * nr,
                         out_specs=out_specs + [ANY] * nr, out_shape=out_shape + rider.out_shape,
                         scratch_shapes=rider.scratch(),
                         compiler_params=pltpu.CompilerParams(dimension_semantics=("arbitrary",),
                                                              vmem_limit_bytes=VMEM_LIMIT, has_side_effects=True))(
        *args, *rider.arrs)
    return (*out[:n_out], list(out[n_out:]))


def _rope(x, c, s1, s2):
    return x * c + pltpu.roll(x, 112, axis=1) * s1 + pltpu.roll(x, 16, axis=1) * s2


def _rope_t(dy, c, s1, s2):
    return dy * c + pltpu.roll(dy * s1, 16, axis=1) + pltpu.roll(dy * s2, 112, axis=1)


def _mla_prep(proj, gq3, gkv3, wuq, wk, wv, tabs, l, name):
    S = proj.shape[0]
    tm = min(512, S)
    tc, ts1, ts2 = tabs

    def body(cq_ref, ckv_ref, kr_ref, gq_ref, gkv_ref, wuq_ref, wk_ref, wv_ref, c_ref, s1_ref, s2_ref,
             q_ref, k_ref, v_ref, kt_ref, vt_ref):
        c, s1, s2 = c_ref[...], s1_ref[...], s2_ref[...]
        cq = cq_ref[...]
        rq = lax.rsqrt(jnp.mean(cq * cq, axis=-1, keepdims=True) + EPS)
        qa = _dot((cq * rq * gq_ref[...]).astype(BF16), wuq_ref[...])
        ckv = ckv_ref[...]
        rkv = lax.rsqrt(jnp.mean(ckv * ckv, axis=-1, keepdims=True) + EPS)
        ckvn = (ckv * rkv * gkv_ref[...]).astype(BF16)
        ka = _dot(ckvn, wk_ref[...])
        va = _dot(ckvn, wv_ref[...])
        v_ref[...] = va.astype(BF16)
        vt_ref[...] = va.T.astype(BF16)
        krr = _rope(kr_ref[...], c, s1, s2)
        for h in range(HEADS):
            sl = slice(h * 128, (h + 1) * 128)
            q_ref[:, sl] = (_rope(qa[:, sl], c, s1, s2) * QK_SCALE).astype(BF16)
            kh = ka[:, sl] + krr
            k_ref[:, sl] = kh.astype(BF16)
            kt_ref[sl, :] = kh.T.astype(BF16)

    lay = lambda a, b: BS((None, a, b), lambda i: (l, 0, 0))
    tab = BS((tm, 128), lambda i: (i, 0))
    return _pc(body, name=name, grid=(S // tm,),
               in_specs=[BS((tm, 256), lambda i: (i, 0)), BS((tm, 128), lambda i: (i, 2)), BS((tm, 128), lambda i: (i, 3)),
                         lay(1, 256), lay(1, 128), lay(256, 512), lay(128, 512), lay(128, 512), tab, tab, tab],
               out_specs=[BS((tm, 512), lambda i: (i, 0))] * 3 + [BS((512, tm), lambda i: (0, i))] * 2,
               out_shape=[SDS((S, 512), BF16)] * 3 + [SDS((512, S), BF16)] * 2)(
        proj, proj, proj, gq3, gkv3, wuq, wk, wv, tc, ts1, ts2)


def _mla_prep_bwd(dq, dk, dv, proj, gq3, gkv3, wuq, wk, wv, tabs, l, name):
    S = proj.shape[0]
    tm = min(512, S)
    tc, ts1, ts2 = tabs

    def body(dq_ref, dk_ref, dv_ref, cq_ref, ckv_ref, gq_ref, gkv_ref, wuq_ref, wk_ref, wv_ref, c_ref, s1_ref, s2_ref,
             dcq_ref, dckv_ref, dkr_ref, dwuq_ref, dwk_ref, dwv_ref, dgq_ref, dgkv_ref):
        @pl.when(pl.program_id(0) == 0)
        def _():
            for r in (dwuq_ref, dwk_ref, dwv_ref, dgq_ref, dgkv_ref):
                r[...] = jnp.zeros_like(r)

        c, s1, s2 = c_ref[...], s1_ref[...], s2_ref[...]
        dqp = jnp.concatenate(
            [_rope_t(dq_ref[h * 128:(h + 1) * 128, :].T * QK_SCALE, c, s1, s2) for h in range(HEADS)], axis=1).astype(BF16)
        cq = cq_ref[...]
        rq = lax.rsqrt(jnp.mean(cq * cq, axis=-1, keepdims=True) + EPS)
        cqh = cq * rq
        gq_v = gq_ref[...]
        dwuq_ref[...] += _dot_tn((cqh * gq_v).astype(BF16), dqp)
        dcqn = _dot_nt(dqp, wuq_ref[...])
        dgq_ref[...] += jnp.sum(dcqn * cqh, axis=0, keepdims=True)
        dxh = dcqn * gq_v
        dcq_ref[...] = (rq * (dxh - cqh * jnp.mean(dxh * cqh, axis=-1, keepdims=True))).astype(BF16)

        dkb = dk_ref[...].astype(BF16)
        dvb = dv_ref[...].astype(BF16)
        ckv = ckv_ref[...]
        rkv = lax.rsqrt(jnp.mean(ckv * ckv, axis=-1, keepdims=True) + EPS)
        ckh = ckv * rkv
        gkv_v = gkv_ref[...]
        ckvn = (ckh * gkv_v).astype(BF16)
        dwk_ref[...] += _dot_tn(ckvn, dkb)
        dwv_ref[...] += _dot_tn(ckvn, dvb)
        dckvn = _dot_nt(dkb, wk_ref[...]) + _dot_nt(dvb, wv_ref[...])
        dgkv_ref[...] += jnp.sum(dckvn * ckh, axis=0, keepdims=True)
        dyh = dckvn * gkv_v
        dckv_ref[...] = (rkv * (dyh - ckh * jnp.mean(dyh * ckh, axis=-1, keepdims=True))).astype(BF16)
        dks = dk_ref[:, 0:128] + dk_ref[:, 128:256] + dk_ref[:, 256:384] + dk_ref[:, 384:512]
        dkr_ref[...] = _rope_t(dks, c, s1, s2).astype(BF16)

    full = lambda a, b: BS((a, b), lambda i: (0, 0))
    lay = lambda a, b: BS((None, a, b), lambda i: (l, 0, 0))
    tab = BS((tm, 128), lambda i: (i, 0))
    row = lambda w: BS((tm, w), lambda i: (i, 0))
    return _pc(body, name=name, grid=(S // tm,),
               in_specs=[BS((512, tm), lambda i: (0, i)), row(512), row(512), BS((tm, 256), lambda i: (i, 0)),
                         BS((tm, 128), lambda i: (i, 2)),
                         lay(1, 256), lay(1, 128), lay(256, 512), lay(128, 512), lay(128, 512), tab, tab, tab],
               out_specs=[row(256), row(128), row(128), full(256, 512), full(128, 512), full(128, 512),
                          full(8, 256), full(8, 128)],
               out_shape=[SDS((S, 256), BF16), SDS((S, 128), BF16), SDS((S, 128), BF16), SDS((256, 512), F32),
                          SDS((128, 512), F32), SDS((128, 512), F32), SDS((8, 256), F32), SDS((8, 128), F32)])(
        dq, dk, dv, proj, proj, gq3, gkv3, wuq, wk, wv, tc, ts1, ts2)


def _causal_steps(n, q_outer):
    if q_outer:
        pairs = [(i, j) for i in range(n) for j in range(i + 1)]
    else:
        pairs = [(i, j) for j in range(n) for i in range(j, n)]
    return jnp.asarray([p[0] for p in pairs], jnp.int32), jnp.asarray([p[1] for p in pairs], jnp.int32)


def _mla_attn(q, k, vt, gts, layer, name):
    S = q.shape[0]
    t = min(512, S)
    n = S // t
    ng = len(gts)

    qi, kj = _causal_steps(n, True)
    last = qi.shape[0] - 1

    def body(qi_ref, kj_ref, q_ref, k_ref, vt_ref, *rest):
        (ya_ref, lse_ref), g_refs = rest[ng:ng + 2], rest[ng + 2:2 * ng + 2]
        m_sc, l_sc, acc_sc = rest[2 * ng + 2:2 * ng + 5]
        i, j = qi_ref[pl.program_id(1)], kj_ref[pl.program_id(1)]
        if ng:
            phases = _gather_phases(g_refs, [g.shape for g in gts], rest[2 * ng + 5], rest[2 * ng + 6], layer)
            for ph, (pp, ss) in zip(phases[:2], ((0, 0), (1, 0))):
                pl.when((pl.program_id(0) == pp) & (pl.program_id(1) == ss))(ph)

        @pl.when(j == 0)
        def _():
            m_sc[...] = jnp.full_like(m_sc, NEG_INF)
            l_sc[...] = jnp.zeros_like(l_sc)
            acc_sc[...] = jnp.zeros_like(acc_sc)

        def step(masked):
            for hh in range(2):
                sl = slice(hh * 128, (hh + 1) * 128)
                st = _dot_nt(k_ref[:, sl], q_ref[:, sl])
                if masked:
                    key = lax.broadcasted_iota(jnp.int32, (t, t), 0)
                    qry = lax.broadcasted_iota(jnp.int32, (t, t), 1)
                    st = jnp.where(key <= qry, st, NEG_INF)
                m_prev = m_sc[hh]
                m_new = jnp.maximum(m_prev, jnp.max(st, axis=0, keepdims=True))
                p = jnp.exp(st - m_new)
                alpha = jnp.exp(m_prev - m_new)
                l_sc[hh] = alpha * l_sc[hh] + jnp.sum(p, axis=0, keepdims=True)
                acc_sc[hh] = alpha * acc_sc[hh] + _dot(vt_ref[sl, :], p.astype(BF16))
                m_sc[hh] = m_new

        @pl.when(j < i)
        def _():
            step(False)

        @pl.when(j == i)
        def _():
            step(True)
            ya_ref[...] = (acc_sc[0] / l_sc[0] + acc_sc[1] / l_sc[1]).T
            for hh in range(2):
                lse_ref[hh] = m_sc[hh] + jnp.log(l_sc[hh])

        if ng:
            pl.when((pl.program_id(0) == 1) & (pl.program_id(1) == last))(phases[2])

    gs = pltpu.PrefetchScalarGridSpec(
        num_scalar_prefetch=2, grid=(2, qi.shape[0]),
        in_specs=[BS((t, 256), lambda p, s, qi, kj: (qi[s], p)), BS((t, 256), lambda p, s, qi, kj: (kj[s], p)),
                  BS((256, t), lambda p, s, qi, kj: (p, kj[s]))] + [ANY] * ng,
        out_specs=[BS((t, 128), lambda p, s, qi, kj: (qi[s], p)), BS((2, 1, t), lambda p, s, qi, kj: (p, 0, qi[s]))]
        + [ANY] * ng,
        scratch_shapes=[pltpu.VMEM((2, 1, t), F32), pltpu.VMEM((2, 1, t), F32), pltpu.VMEM((2, 128, t), F32)]
        + ([pltpu.SemaphoreType.DMA((7 * ng,)), pltpu.SemaphoreType.DMA((7 * ng,))] if ng else []))
    out = pl.pallas_call(body, name=name, grid_spec=gs,
                         out_shape=[SDS((S, 256), F32), SDS((HEADS, 1, S), F32)] + [SDS(g.shape, g.dtype) for g in gts],
                         input_output_aliases={5 + m: 2 + m for m in range(ng)},
                         compiler_params=pltpu.CompilerParams(dimension_semantics=("arbitrary",) * 2,
                                                              vmem_limit_bytes=VMEM_LIMIT, has_side_effects=bool(ng)))(
        qi, kj, q, k, vt, *gts)
    return out[0], out[1], list(out[2:])


def _mla_delta(dycat, ya, name):
    S = ya.shape[0]
    t = min(512, S)

    def body(do_ref, ya_ref, d_ref):
        prod = do_ref[...] * ya_ref[...]
        for p in range(2):
            pt = prod[:, p * 128:(p + 1) * 128].T
            d_ref[2 * p] = jnp.sum(pt[0:64, :], axis=0, keepdims=True)
            d_ref[2 * p + 1] = jnp.sum(pt[64:128, :], axis=0, keepdims=True)

    return _pc(body, name=name, grid=(S // t,),
               in_specs=[BS((t, 256), lambda i: (i, 0)), BS((t, 256), lambda i: (i, 0))],
               out_specs=BS((HEADS, 1, t), lambda i: (0, 0, i)), out_shape=SDS((HEADS, 1, S), F32))(dycat, ya)


def _mla_attn_bwd(q, k, kt, v, dya, lse, delta, rider, name):
    S = q.shape[0]
    t = min(512, S)
    n = S // t
    nr = rider.n if rider else 0

    qi, kj = _causal_steps(n, False)
    last = qi.shape[0] - 1

    def body(qi_ref, kj_ref, q_ref, k_ref, kt_ref, v_ref, do_ref, lse_ref, delta_ref, *rest):
        dqt_ref, dk_ref, dv_ref = rest[nr:nr + 3]
        r_io = (rest[:nr], rest[nr + 3:2 * nr + 3], rest[2 * nr + 3:])
        i, j = qi_ref[pl.program_id(1)], kj_ref[pl.program_id(1)]
        if rider:
            pl.when((pl.program_id(0) == 0) & (pl.program_id(1) == 0))(lambda: rider.start(*r_io))

        @pl.when(pl.program_id(1) == 0)
        def _():
            dqt_ref[...] = jnp.zeros_like(dqt_ref)

        @pl.when(i == j)
        def _():
            dk_ref[...] = jnp.zeros_like(dk_ref)
            dv_ref[...] = jnp.zeros_like(dv_ref)

        def step(masked):
            dob = do_ref[...].astype(BF16)
            cols = pl.ds(pl.multiple_of(i * t, t), t)
            for hh in range(2):
                sl = slice(hh * 128, (hh + 1) * 128)
                qv = q_ref[:, sl]
                p = jnp.exp(_dot_nt(k_ref[:, sl], qv) - lse_ref[hh])
                if masked:
                    key = lax.broadcasted_iota(jnp.int32, (t, t), 0)
                    qry = lax.broadcasted_iota(jnp.int32, (t, t), 1)
                    p = jnp.where(key <= qry, p, 0.0)
                dv_ref[:, sl] += _dot(p.astype(BF16), dob)
                ds = (p * (_dot_nt(v_ref[:, sl], dob) - delta_ref[hh])).astype(BF16)
                dk_ref[:, sl] += _dot(ds, qv)
                dqt_ref[sl, cols] += _dot(kt_ref[sl, :], ds)

        @pl.when(i > j)
        def _():
            step(False)

        @pl.when(i == j)
        def _():
            step(True)

        if rider:
            pl.when((pl.program_id(0) == 1) & (pl.program_id(1) == last))(lambda: rider.wait(*r_io))

    qs = BS((t, 256), lambda p, s, qi, kj: (qi[s], p))
    ks = BS((t, 256), lambda p, s, qi, kj: (kj[s], p))
    rowv = BS((2, 1, t), lambda p, s, qi, kj: (p, 0, qi[s]))
    gs = pltpu.PrefetchScalarGridSpec(
        num_scalar_prefetch=2, grid=(2, qi.shape[0]),
        in_specs=[qs, ks, BS((256, t), lambda p, s, qi, kj: (p, kj[s])), ks,
                  BS((t, 128), lambda p, s, qi, kj: (qi[s], p)), rowv, rowv] + [ANY] * nr,
        out_specs=[BS((256, S), lambda p, s, qi, kj: (p, 0)), ks, ks] + [ANY] * nr,
        scratch_shapes=rider.scratch() if rider else [])
    out = pl.pallas_call(body, name=name, grid_spec=gs,
                         out_shape=[SDS((512, S), F32), SDS((S, 512), F32), SDS((S, 512), F32)]
                         + (rider.out_shape if rider else []),
                         compiler_params=pltpu.CompilerParams(dimension_semantics=("arbitrary",) * 2,
                                                              vmem_limit_bytes=VMEM_LIMIT, has_side_effects=bool(rider)))(
        qi, kj, q, k, kt, v, dya, lse, delta, *(rider.arrs if rider else []))
    return out[0], out[1], out[2], list(out[3:])


def _swa_scores(qm, kk, valid, distf, slope, sink):
    sc = _dot_nt(qm, kk) * SWA_SCALE
    sc = jnp.where(valid, sc - slope * distf, NEG_INF)
    m = jnp.maximum(jnp.max(sc, axis=-1, keepdims=True), sink)
    e = jnp.exp(sc - m)
    esink = jnp.exp(sink - m)
    den = jnp.sum(e, axis=-1, keepdims=True) + esink
    return e / den, esink / den


def _swa_masks():
    r = lax.broadcasted_iota(jnp.int32, (BLK, 2 * BLK), 0)
    c = lax.broadcasted_iota(jnp.int32, (BLK, 2 * BLK), 1)
    dist = r + BLK - c
    return (dist >= 0) & (dist < SWA_WINDOW), c >= BLK, dist.astype(F32)


def _to_half(xb, pos, b):
    return xb if pos == b else pltpu.roll(xb, 64, axis=1)


def _swa(proj, sinks, l, name):
    S = proj.shape[0]
    nb = S // BLK

    def body(q_ref, k_ref, v_ref, sink_ref, o_ref, kp, vp):
        kp[0:BLK, :] = jnp.zeros((BLK, 128), BF16)
        vp[0:BLK, :] = jnp.zeros((BLK, 128), BF16)
        kp[BLK:, :] = k_ref[...].astype(BF16)
        vp[BLK:, :] = v_ref[...].astype(BF16)
        lo = lax.broadcasted_iota(jnp.int32, (BLK, 128), 1) < 64
        band, cur, distf = _swa_masks()

        def blk(i, carry):
            st = pl.multiple_of(i * BLK, BLK)
            kk = kp[pl.ds(st, 2 * BLK), :]
            vv = vp[pl.ds(st, 2 * BLK), :]
            valid = band & (cur | (i > 0))
            for b in range(2):
                half = lo if b == 0 else ~lo
                qb = q_ref[pl.ds(st, BLK), b * 128:(b + 1) * 128]
                outs = []
                for pos in range(2):
                    h = 2 * b + pos
                    qm = jnp.where(half, _to_half(qb, pos, b), 0.0).astype(BF16)
                    p, _ = _swa_scores(qm, kk, valid, distf, SLOPES[h], sink_ref[l, h])
                    outs.append(_to_half(_dot(p.astype(BF16), vv), pos, b))
                o_ref[pl.ds(st, BLK), b * 128:(b + 1) * 128] = jnp.where(lo, outs[0], outs[1])
            return carry

        lax.fori_loop(0, nb, blk, 0, unroll=2)

    return _pc(body, name=name, grid=(1,),
               in_specs=[BS((S, 256), lambda i: (0, C_QS // 256)), BS((S, 128), lambda i: (0, C_KS // 128)),
                         BS((S, 128), lambda i: (0, C_VS // 128)), BS(memory_space=pltpu.SMEM)],
               out_specs=BS((S, 256), lambda i: (0, 0)),
               out_shape=SDS((S, 256), F32),
               scratch=[pltpu.VMEM((S + BLK, 128), BF16), pltpu.VMEM((S + BLK, 128), BF16)])(proj, proj, proj, sinks)


def _swa_bwd(proj, sinks, dyd, l, name, rider=None):
    S = proj.shape[0]
    nb = S // BLK
    nr = rider.n if rider else 0

    def body(q_ref, k_ref, v_ref, sink_ref, do_ref, *rest):
        dq_ref, dk_ref, dv_ref, dsink_ref = rest[nr:nr + 4]
        kp, vp, dkp, dvp = rest[2 * nr + 4:2 * nr + 8]
        r_io = (rest[:nr], rest[nr + 4:2 * nr + 4], rest[2 * nr + 8:])
        if rider:
            rider.start(*r_io)
        kp[0:BLK, :] = jnp.zeros((BLK, 128), BF16)
        vp[0:BLK, :] = jnp.zeros((BLK, 128), BF16)
        kp[BLK:, :] = k_ref[...].astype(BF16)
        vp[BLK:, :] = v_ref[...].astype(BF16)
        dkp[...] = jnp.zeros_like(dkp)
        dvp[...] = jnp.zeros_like(dvp)
        lo = lax.broadcasted_iota(jnp.int32, (BLK, 128), 1) < 64
        lane8 = lax.broadcasted_iota(jnp.int32, (8, 128), 1)
        band, cur, distf = _swa_masks()

        def blk2(i2, dsink):
            return blk(2 * i2 + 1, blk(2 * i2, dsink))

        def blk(i, dsink):
            st = pl.multiple_of(i * BLK, BLK)
            kk = kp[pl.ds(st, 2 * BLK), :]
            vv = vp[pl.ds(st, 2 * BLK), :]
            valid = band & (cur | (i > 0))
            dkk = jnp.zeros((2 * BLK, 128), F32)
            dvv = jnp.zeros((2 * BLK, 128), F32)
            for b in range(2):
                half = lo if b == 0 else ~lo
                qb = q_ref[pl.ds(st, BLK), b * 128:(b + 1) * 128]
                dob = do_ref[pl.ds(st, BLK), b * 128:(b + 1) * 128]
                dqs = []
                for pos in range(2):
                    h = 2 * b + pos
                    qm = jnp.where(half, _to_half(qb, pos, b), 0.0).astype(BF16)
                    dom = jnp.where(half, _to_half(dob, pos, b), 0.0).astype(BF16)
                    p, psink = _swa_scores(qm, kk, valid, distf, SLOPES[h], sink_ref[l, h])
                    dp = _dot_nt(dom, vv)
                    dvv = dvv + _dot_tn(p.astype(BF16), dom)
                    delta = jnp.sum(p * dp, axis=-1, keepdims=True)
                    dsink = dsink + jnp.where(lane8 == h, -jnp.sum(psink * delta), 0.0)
                    dsc = (p * (dp - delta) * SWA_SCALE).astype(BF16)
                    dqs.append(_to_half(_dot(dsc, kk), pos, b))
                    dkk = dkk + _dot_tn(dsc, qm)
                dq_ref[pl.ds(st, BLK), b * 128:(b + 1) * 128] = jnp.where(lo, dqs[0], dqs[1]).astype(BF16)
            dkp[pl.ds(st, 2 * BLK), :] += dkk
            dvp[pl.ds(st, 2 * BLK), :] += dvv
            return dsink

        dsink_ref[...] = lax.fori_loop(0, nb // 2, blk2, jnp.zeros((8, 128), F32))
        dk_ref[...] = dkp[BLK:, :].astype(BF16)
        dv_ref[...] = dvp[BLK:, :].astype(BF16)
        if rider:
            rider.wait(*r_io)

    in_specs = [BS((S, 256), lambda i: (0, C_QS // 256)), BS((S, 128), lambda i: (0, C_KS // 128)),
                BS((S, 128), lambda i: (0, C_VS // 128)), BS(memory_space=pltpu.SMEM), BS((S, 256), lambda i: (0, 3))]
    out_specs = [BS((S, 256), lambda i: (0, 0)), BS((S, 128), lambda i: (0, 0)), BS((S, 128), lambda i: (0, 0)),
                 BS((8, 128), lambda i: (0, 0))]
    out_shape = [SDS((S, 256), BF16), SDS((S, 128), BF16), SDS((S, 128), BF16), SDS((8, 128), F32)]
    scratch = [pltpu.VMEM((S + BLK, 128), BF16), pltpu.VMEM((S + BLK, 128), BF16),
               pltpu.VMEM((S + BLK, 128), F32), pltpu.VMEM((S + BLK, 128), F32)]
    if not rider:
        return _pc(body, name=name, grid=(1,), in_specs=in_specs, out_specs=out_specs, out_shape=out_shape,
                   scratch=scratch)(proj, proj, proj, sinks, dyd)
    out = pl.pallas_call(body, name=name, grid=(1,), in_specs=in_specs + [ANY] * nr, out_specs=out_specs + [ANY] * nr,
                         out_shape=out_shape + rider.out_shape, scratch_shapes=scratch + rider.scratch(),
                         compiler_params=pltpu.CompilerParams(dimension_semantics=("arbitrary",),
                                                              vmem_limit_bytes=VMEM_LIMIT, has_side_effects=True))(
        proj, proj, proj, sinks, dyd, *rider.arrs)
    return (*out[:4], list(out[4:]))


def _down(x, k, t):
    return jnp.where(t >= k, pltpu.roll(x, k, axis=0), 0.0)


def _up(x, k, t):
    n = x.shape[0]
    return jnp.where(t < n - k, pltpu.roll(x, n - k, axis=0), 0.0)


def _conv(proj, w8, l, name):
    S = proj.shape[0]

    def body(gb_ref, gc_ref, u_ref, w_ref, y_ref):
        t = lax.broadcasted_iota(jnp.int32, (S, 128), 0)
        z = gc_ref[...] * u_ref[...]
        c = w_ref[2:3, :] * z + w_ref[1:2, :] * _down(z, 1, t) + w_ref[0:1, :] * _down(z, 2, t)
        y_ref[...] = gb_ref[...] * c

    col = lambda c0: BS((S, 128), lambda i: (0, c0 // 128 + i))
    return _pc(body, name=name, grid=(2,),
               in_specs=[col(C_GB), col(C_GC), col(C_UC), BS((None, 8, 128), lambda i: (l, 0, i))],
               out_specs=BS((S, 128), lambda i: (0, i)), out_shape=SDS((S, 256), F32))(proj, proj, proj, w8)


def _conv_bwd(proj, w8, dycat, l, name):
    S = proj.shape[0]

    def body(gb_ref, gc_ref, u_ref, w_ref, dy_ref, dgb_ref, dgc_ref, du_ref, dw_ref):
        t = lax.broadcasted_iota(jnp.int32, (S, 128), 0)
        gc, u = gc_ref[...], u_ref[...]
        z = gc * u
        z1, z2 = _down(z, 1, t), _down(z, 2, t)
        w0, w1, w2 = w_ref[0:1, :], w_ref[1:2, :], w_ref[2:3, :]
        dy = dy_ref[...]
        dgb_ref[...] = (dy * (w2 * z + w1 * z1 + w0 * z2)).astype(BF16)
        dc = dy * gb_ref[...]
        dz = w2 * dc + w1 * _up(dc, 1, t) + w0 * _up(dc, 2, t)
        dgc_ref[...] = (dz * u).astype(BF16)
        du_ref[...] = (dz * gc).astype(BF16)
        row = lax.broadcasted_iota(jnp.int32, (8, 128), 0)
        sums = [jnp.sum(dc * zz, axis=0, keepdims=True) for zz in (z2, z1, z)]
        dw_ref[...] = jnp.where(row == 0, sums[0], jnp.where(row == 1, sums[1], jnp.where(row == 2, sums[2], 0.0)))

    col = lambda c0: BS((S, 128), lambda i: (0, c0 // 128 + i))
    out = BS((S, 128), lambda i: (0, i))
    return _pc(body, name=name, grid=(2,),
               in_specs=[col(C_GB), col(C_GC), col(C_UC), BS((None, 8, 128), lambda i: (l, 0, i)), col(256)],
               out_specs=[out, out, out, BS((8, 128), lambda i: (0, i))],
               out_shape=[SDS((S, 256), BF16)] * 3 + [SDS((8, 256), F32)])(proj, proj, proj, w8, dycat)


def _pool_parts(u, t, first):
    lo = lax.broadcasted_iota(jnp.int32, u.shape, 1) < 64
    s2 = u + _down(u, 1, t)
    s4 = s2 + _down(s2, 2, t)
    s8 = s4 + _down(s4, 4, t)
    s16 = s8 + _down(s8, 8, t)
    win = jnp.where(lo, jnp.where(first, s2, s8), jnp.where(first, s4, s16))
    wv = jnp.where(lo, jnp.where(first, 2, 8), jnp.where(first, 4, 16))
    cnt = jnp.minimum(t + 1, wv).astype(F32)
    return win, cnt, lo


def _pool(proj, pwd, scale3, l, name):
    S = proj.shape[0]

    def body(u_ref, pw_ref, sc_ref, y_ref):
        t = lax.broadcasted_iota(jnp.int32, (S, 128), 0)
        u = u_ref[...]
        win, cnt, _ = _pool_parts(u, t, pl.program_id(0) == 0)
        pooled = win / cnt - u
        y_ref[...] = _dot(pooled.astype(BF16), pw_ref[...]) * sc_ref[...]

    return _pc(body, name=name, grid=(2,),
               in_specs=[BS((S, 128), lambda i: (0, C_UP // 128 + i)), BS((None, 128, 128), lambda i: (l, i, 0)),
                         BS((None, 1, 128), lambda i: (l, 0, i))],
               out_specs=BS((S, 128), lambda i: (0, i)), out_shape=SDS((S, 256), F32))(proj, pwd, scale3)


def _pool_bwd(proj, pwd, scale3, dycat, l, name):
    S = proj.shape[0]

    def body(u_ref, pw_ref, sc_ref, dy_ref, du_ref, dpw_ref, dsc_ref):
        t = lax.broadcasted_iota(jnp.int32, (S, 128), 0)
        first = pl.program_id(0) == 0
        u = u_ref[...]
        win, cnt, lo = _pool_parts(u, t, first)
        pooled = (win / cnt - u).astype(BF16)
        pw = pw_ref[...]
        dy = dy_ref[...]
        dsc_ref[...] = jnp.broadcast_to(jnp.sum(dy * _dot(pooled, pw), axis=0, keepdims=True), (8, 128))
        dmb = (dy * sc_ref[...]).astype(BF16)
        dpw_ref[...] = _dot_tn(pooled, dmb)
        dpooled = _dot_nt(dmb, pw)
        a1 = dpooled / cnt
        a2 = a1 + _up(a1, 1, t)
        a4 = a2 + _up(a2, 2, t)
        a8 = a4 + _up(a4, 4, t)
        a16 = a8 + _up(a8, 8, t)
        dwin = jnp.where(lo, jnp.where(first, a2, a8), jnp.where(first, a4, a16))
        du_ref[...] = (dwin - dpooled).astype(BF16)

    return _pc(body, name=name, grid=(2,),
               in_specs=[BS((S, 128), lambda i: (0, C_UP // 128 + i)), BS((None, 128, 128), lambda i: (l, i, 0)),
                         BS((None, 1, 128), lambda i: (l, 0, i)), BS((S, 128), lambda i: (0, 4 + i))],
               out_specs=[BS((S, 128), lambda i: (0, i)), BS((128, 128), lambda i: (i, 0)), BS((8, 128), lambda i: (0, i))],
               out_shape=[SDS((S, 256), BF16), SDS((256, 128), F32), SDS((8, 256), F32)])(proj, pwd, scale3, dycat)


def _adamw(w, g, m, v, name, echo=False):
    n, a, b = w.shape
    tr = _row_tile(a, b)

    def body(w_ref, g_ref, m_ref, v_ref, d_ref, nm_ref, nv_ref, *g_out):
        gv = g_ref[...]
        if echo:
            g_out[0][...] = gv
        m_new = B1 * m_ref[...] + (1.0 - B1) * gv
        v_new = B2 * v_ref[...] + (1.0 - B2) * (gv * gv)
        m_hat = m_new / (1.0 - B1 ** STEP)
        v_hat = v_new / (1.0 - B2 ** STEP)
        d_ref[...] = -LR * (m_hat / (jnp.sqrt(v_hat) + ADAM_EPS) + WD * w_ref[...])
        nm_ref[...] = m_new
        nv_ref[...] = v_new

    sp = BS((None, tr, b), lambda i, t: (i, t, 0))
    return _pc(body, name=name, grid=(n, a // tr), in_specs=[sp] * 4, out_specs=[sp] * (3 + echo),
               out_shape=[SDS((n, a, b), F32)] * (3 + echo))(w, g, m, v)


def _prefetch_call(body, name, grid, in_specs, out_specs, out_shape):
    gs = pltpu.PrefetchScalarGridSpec(num_scalar_prefetch=1, grid=grid, in_specs=in_specs, out_specs=out_specs)
    return pl.pallas_call(body, name=name, grid_spec=gs, out_shape=out_shape, compiler_params=_params(len(grid)))


def _place(w, kc, dtype, name):
    _, a, b = w.shape

    def body(kc_ref, w_ref, o_ref):
        o_ref[...] = w_ref[...].astype(dtype)

    return _prefetch_call(body, name, (2,), [BS((None, a, b), lambda l, kc: (l, 0, 0))],
                          BS((None, None, a, b), lambda l, kc: (l, kc[0], 0, 0)), SDS((2, 4, a, b), dtype))(kc, w)


def _pair_sum(g, got, kc, name):
    _, _, a, b = g.shape
    tr = _row_tile(a, b)

    def body(kc_ref, a_ref, b_ref, t32_ref, t16_ref):
        s = a_ref[...] + b_ref[...]
        t16_ref[...] = s.astype(BF16)

        @pl.when(pl.program_id(1) == kc_ref[0])
        def _():
            t32_ref[...] = s

    sp = BS((None, tr, b), lambda t, k, kc: (k, t, 0))
    return _prefetch_call(body, name, (a // tr, 4),
                          [BS((None, None, tr, b), lambda t, k, kc: (kc[1], k, t, 0)), sp],
                          [BS((tr, b), lambda t, k, kc: (t, 0)), sp],
                          [SDS((a, b), F32), SDS((4, a, b), BF16)])(kc, g, got)


def _chip_sum(t32, got3, kc, name):
    a, b = t32.shape
    tr = _row_tile(a, b)

    def body(kc_ref, a_ref, b_ref, u_ref):
        u_ref[...] = ((a_ref[...] + b_ref[0].astype(F32)) + b_ref[1].astype(F32)) + b_ref[2].astype(F32)

    return _prefetch_call(body, name, (a // tr,),
                          [BS((tr, b), lambda t, kc: (t, 0)), BS((3, tr, b), lambda t, kc: (0, t, 0))],
                          BS((None, tr, b), lambda t, kc: (kc[1], t, 0)), SDS((2, a, b), F32))(kc, t32, got3)


def _me():
    return lax.axis_index("x"), lax.axis_index("y"), lax.axis_index("c")


def _other_chips(x, y):
    return [(1 - x, y), (x, 1 - y), (1 - x, 1 - y)]


ANY = BS(memory_space=pl.ANY)
COMM_PARAMS = pltpu.CompilerParams(has_side_effects=True)


def _gather(arrs, name):
    n = len(arrs)

    def body(*refs):
        for phase in _gather_phases(refs[n:2 * n], [a.shape for a in arrs], refs[2 * n], refs[2 * n + 1]):
            phase()

    return pl.pallas_call(body, name=name, out_shape=[SDS(a.shape, a.dtype) for a in arrs],
                          in_specs=[ANY] * n, out_specs=[ANY] * n, input_output_aliases={t: t for t in range(n)},
                          scratch_shapes=[pltpu.SemaphoreType.DMA((7 * n,)), pltpu.SemaphoreType.DMA((7 * n,))],
                          compiler_params=COMM_PARAMS)(*arrs)


def _gather_phases(outs, shapes, send_sems, recv_sems, layer=None):
    n = len(outs)
    cut = [s[2] // 2 // 16 * 16 for s in shapes]
    split = [r > 0 for r in cut]

    def plan():
        x, y, c = _me()
        return (c if layer is None else layer), (x, y), (x, y, c), (x, y, 1 - c), _other_chips(x, y)

    def role(moving, fn):
        if layer is None:
            fn()
        else:
            c = lax.axis_index("c")
            pl.when((c == layer) if moving else (c != layer))(fn)

    def blk(t, chip, layer, half=None):
        r = outs[t].at[layer, 2 * chip[0] + chip[1]]
        if half is None:
            return r
        return r.at[pl.ds(0, cut[t])] if half == 0 else r.at[pl.ds(cut[t], shapes[t][2] - cut[t])]

    def copy(t, k, ref, to):
        return pltpu.make_async_remote_copy(src_ref=ref, dst_ref=ref, send_sem=send_sems.at[7 * t + k],
                                            recv_sem=recv_sems.at[7 * t + k], device_id=to, device_id_type=MESH)

    def own_sends(t):
        c, chip, me, sib, (xn, yn, dg) = plan()
        cps = [copy(t, 0, blk(t, chip, c), (*xn, c)), copy(t, 1, blk(t, chip, c), (*yn, c))]
        return cps if split[t] else cps + [copy(t, 2, blk(t, chip, c), (*dg, c))]

    def relays(t):
        c, chip, me, sib, (xn, yn, dg) = plan()
        after_x = [copy(t, 4, blk(t, xn, c), sib)]
        after_y = [copy(t, 5, blk(t, yn, c), sib)]
        if split[t]:
            after_x.insert(0, copy(t, 2, blk(t, xn, c, 0), (*yn, c)))
            after_y.insert(0, copy(t, 3, blk(t, yn, c, 1), (*xn, c)))
        return after_x, after_y, [copy(t, 6, blk(t, dg, c), sib)]

    def send_own():
        for t in range(n):
            for cp in own_sends(t):
                cp.start()

    def relay_neighbours():
        c, chip, me, sib, (xn, yn, dg) = plan()
        for t in range(n):
            after_x, after_y, _ = relays(t)
            copy(t, 0, blk(t, xn, c), me).wait_recv()
            for cp in after_x:
                cp.start()
            copy(t, 1, blk(t, yn, c), me).wait_recv()
            for cp in after_y:
                cp.start()

    def relay_diagonal():
        c, chip, me, sib, (xn, yn, dg) = plan()
        for t in range(n):
            if split[t]:
                copy(t, 2, blk(t, dg, c, 0), me).wait_recv()
                copy(t, 3, blk(t, dg, c, 1), me).wait_recv()
            else:
                copy(t, 2, blk(t, dg, c), me).wait_recv()
            relays(t)[2][0].start()

    def take_sibling():
        _, chip, me, sib, (xn, yn, dg) = plan()
        theirs = 1 - lax.axis_index("c") if layer is None else layer
        for t in range(n):
            for k, peer in ((4, xn), (5, yn), (6, dg)):
                copy(t, k, blk(t, peer, theirs), me).wait_recv()

    def drain_sends():
        for t in range(n):
            after_x, after_y, after_d = relays(t)
            for cp in own_sends(t) + after_x + after_y + after_d:
                cp.wait_send()

    def phase3():
        role(True, relay_diagonal)
        role(False, take_sibling)
        role(True, drain_sends)

    return (lambda: role(True, send_own)), (lambda: role(True, relay_neighbours)), phase3


def _swap_copies(ins, outs, send_sems, recv_sems):
    x, y, c = _me()
    return [pltpu.make_async_remote_copy(src_ref=ins[t].at[1 - c], dst_ref=outs[t], send_sem=send_sems.at[t],
                                         recv_sem=recv_sems.at[t], device_id=(x, y, 1 - c), device_id_type=MESH)
            for t in range(len(ins))]


def _exchange_copies(ins, outs, send_sems, recv_sems):
    x, y, c = _me()
    return [pltpu.make_async_remote_copy(src_ref=ins[t].at[2 * cx + cy], dst_ref=outs[t].at[j],
                                         send_sem=send_sems.at[3 * t + j], recv_sem=recv_sems.at[3 * t + j],
                                         device_id=(cx, cy, c), device_id_type=MESH)
            for j, (cx, cy) in enumerate(_other_chips(x, y)) for t in range(len(ins))]


class _Rider:
    def __init__(self, arrs, out_shape, nsem, copies):
        self.arrs, self.out_shape, self.nsem, self.copies = list(arrs), out_shape, nsem, copies
        self.n = len(self.arrs)

    def scratch(self):
        return [pltpu.SemaphoreType.DMA((self.nsem,)), pltpu.SemaphoreType.DMA((self.nsem,))]

    def start(self, ins, outs, sems):
        for cp in self.copies(ins, outs, *sems):
            cp.start()

    def wait(self, ins, outs, sems):
        for cp in self.copies(ins, outs, *sems):
            cp.wait()


def _swap_rider(gs):
    return _Rider(gs, [SDS(g.shape[1:], g.dtype) for g in gs], len(gs), _swap_copies)


def _exchange_rider(ts):
    return _Rider(ts, [SDS((3,) + t.shape[1:], t.dtype) for t in ts], 3 * len(ts), _exchange_copies)


def _ride_alone(rider, name):
    n = rider.n

    def body(*refs):
        rider.start(refs[:n], refs[n:2 * n], refs[2 * n:])
        rider.wait(refs[:n], refs[n:2 * n], refs[2 * n:])

    return pl.pallas_call(body, name=name, out_shape=rider.out_shape, in_specs=[ANY] * n, out_specs=[ANY] * n,
                          scratch_shapes=rider.scratch(), compiler_params=COMM_PARAMS)(*rider.arrs)


def _join_layers(us, name):
    n = len(us)

    def body(*refs):
        outs, send_sems, recv_sems = refs[n:2 * n], refs[2 * n], refs[2 * n + 1]
        x, y, c = _me()
        cps = [pltpu.make_async_remote_copy(src_ref=outs[t].at[c], dst_ref=outs[t].at[c], send_sem=send_sems.at[t],
                                            recv_sem=recv_sems.at[t], device_id=(x, y, 1 - c), device_id_type=MESH)
               for t in range(n)]
        for cp in cps:
            cp.start()
        for cp in cps:
            cp.wait()

    return pl.pallas_call(body, name=name, out_shape=[SDS(u.shape, u.dtype) for u in us],
                          in_specs=[ANY] * n, out_specs=[ANY] * n, input_output_aliases={t: t for t in range(n)},
                          scratch_shapes=[pltpu.SemaphoreType.DMA((n,)), pltpu.SemaphoreType.DMA((n,))],
                          compiler_params=COMM_PARAMS)(*us)


def _allsum_small(v, name, rider=None):
    M = v.shape[0]
    nr = rider.n if rider else 0

    def body(x_ref, *rest):
        o_ref = rest[nr]
        all_ref, send_sems, recv_sems, local_sem = rest[2 * nr + 1:2 * nr + 5]
        r_io = (rest[:nr], rest[nr + 1:2 * nr + 1], rest[2 * nr + 5:])
        x, y, c = _me()
        me, sib = (x, y, c), (x, y, 1 - c)
        chips = _other_chips(x, y)

        def rows(px, py, pc):
            return all_ref.at[pl.ds((4 * px + 2 * py + pc) * M, M), :]

        def copy(k, block, to, src=None):
            return pltpu.make_async_remote_copy(src_ref=rows(*block) if src is None else src, dst_ref=rows(*block),
                                                send_sem=send_sems.at[k], recv_sem=recv_sems.at[k],
                                                device_id=to, device_id_type=MESH)

        mine = pltpu.make_async_copy(x_ref, rows(*me), local_sem)
        mine.start()
        first = [copy(0, me, sib, src=x_ref)]
        first += [copy(1 + j, me, (*chip, c), src=x_ref) for j, chip in enumerate(chips)]
        for cp in first:
            cp.start()
        if rider:
            rider.start(*r_io)
        passed = [copy(4 + j, (*chip, c), sib) for j, chip in enumerate(chips)]
        for j, chip in enumerate(chips):
            copy(1 + j, (*chip, c), me).wait_recv()
            passed[j].start()
        copy(0, sib, me).wait_recv()
        for j, chip in enumerate(chips):
            copy(4 + j, (*chip, 1 - c), me).wait_recv()
        for cp in first + passed:
            cp.wait_send()
        mine.wait()
        acc = all_ref[0:M, :]
        for d in range(1, 8):
            acc = acc + all_ref[d * M:(d + 1) * M, :]
        o_ref[...] = acc
        if rider:
            rider.wait(*r_io)

    vm = BS(memory_space=pltpu.VMEM)
    out = pl.pallas_call(body, name=name, out_shape=[SDS((M, LANES), F32)] + (rider.out_shape if rider else []),
                         in_specs=[vm] + [ANY] * nr, out_specs=[vm] + [ANY] * nr,
                         scratch_shapes=[pltpu.VMEM((8 * M, LANES), F32), pltpu.SemaphoreType.DMA((7,)),
                                         pltpu.SemaphoreType.DMA((7,)), pltpu.SemaphoreType.DMA]
                         + (rider.scratch() if rider else []),
                         compiler_params=pltpu.CompilerParams(has_side_effects=True, vmem_limit_bytes=VMEM_LIMIT))(
        v, *(rider.arrs if rider else []))
    return out[0], list(out[1:])


FFN = ("w_gate_up", "w_down")
REST = ("w_in", "w_o", "w_uq", "w_ukv")
BIG = FFN + REST
TINY = ("conv_w",)
REPL = ("attn_norm", "mla_q_norm", "mla_kv_norm", "pool_w", "pool_scale", "swa_sinks", "mix_norm", "ffn_norm",
        "final_norm")
ORDER = ("attn_norm", "w_in", "mla_q_norm", "w_uq", "mla_kv_norm", "w_ukv", "conv_w", "pool_w", "pool_scale",
         "swa_sinks", "mix_norm", "w_o", "ffn_norm", "w_gate_up", "w_down", "final_norm")


def _rows8(shape):
    return -(-int(np.prod(shape)) // (8 * LANES)) * 8


def _pack(arrs):
    parts = []
    for a in arrs:
        r = _rows8(a.shape)
        parts.append(jnp.pad(a.reshape(-1), (0, r * LANES - a.size)).reshape(r, LANES))
    return jnp.concatenate(parts, axis=0)


def _unpack(buf, shapes):
    out, r0 = [], 0
    for s in shapes:
        n, r = int(np.prod(s)), _rows8(s)
        rows = buf[r0:r0 + r]
        out.append(rows.reshape(s) if n == r * LANES else rows.reshape(-1)[:n].reshape(s))
        r0 += r
    return out


def _cols_joined(g):
    return jnp.transpose(g, (0, 2, 1, 3)).reshape(g.shape[0], g.shape[2], 4 * g.shape[3])


def _cols_split(w):
    n, a, b4 = w.shape
    return jnp.transpose(w.reshape(n, a, 4, b4 // 4), (0, 2, 1, 3))


def _rope_tables(S):
    inv = 1.0 / (10000.0 ** (jnp.arange(0, 32, 2, dtype=F32) / 32))
    ang = jnp.arange(S, dtype=F32)[:, None] * inv[None, :]
    cos, sin = jnp.cos(ang), jnp.sin(ang)
    z = lambda w: jnp.zeros((S, w), F32)
    tc = jnp.concatenate([jnp.ones((S, 64), F32), cos, cos, jnp.ones((S, 32), F32)], axis=1)
    ts1 = jnp.concatenate([z(64), -sin, z(48)], axis=1)
    ts2 = jnp.concatenate([z(80), sin, z(32)], axis=1)
    return tc, ts1, ts2


def _pad_w_in(wt):
    z = lambda n: jnp.zeros((wt.shape[0], n, wt.shape[2]), wt.dtype)
    return jnp.concatenate([wt[:, 0:384], z(64), wt[:, 384:416], z(32), wt[:, 416:1952]], axis=1)


def _unpad_w_in(dt):
    return jnp.concatenate([dt[:, 0:384], dt[:, 448:480], dt[:, 512:2048]], axis=1)


def _pad_heads(w, src, offs):
    cols = []
    for h in range(HEADS):
        src0, n = src[h]
        z = lambda k: jnp.zeros(w.shape[:-1] + (k,), w.dtype)
        cols += [z(offs[h]), w[..., src0:src0 + n], z(128 - offs[h] - n)]
    return jnp.concatenate(cols, axis=-1)


UQ_SRC = [(h * 96, 96) for h in range(HEADS)]
KN_SRC = [(h * 128, 64) for h in range(HEADS)]
V_SRC = [(h * 128 + 64, 64) for h in range(HEADS)]
ZERO_OFF = [0] * HEADS
V_OFF = [(h % 2) * 64 for h in range(HEADS)]


def _unpad_heads(d, src, offs):
    return [d[..., h * 128 + offs[h]: h * 128 + offs[h] + src[h][1]] for h in range(HEADS)]


def kernel(x, attn_norm, w_in, mla_q_norm, w_uq, mla_kv_norm, w_ukv, conv_w, pool_w, pool_scale, swa_sinks, mix_norm, w_o, ffn_norm, w_gate_up, w_down, final_norm, loss_target, m_attn_norm, m_w_in, m_mla_q_norm, m_w_uq, m_mla_kv_norm, m_w_ukv, m_conv_w, m_pool_w, m_pool_scale, m_swa_sinks, m_mix_norm, m_w_o, m_ffn_norm, m_w_gate_up, m_w_down, m_final_norm, v_attn_norm, v_w_in, v_mla_q_norm, v_w_uq, v_mla_kv_norm, v_w_ukv, v_conv_w, v_pool_w, v_pool_scale, v_swa_sinks, v_mix_norm, v_w_o, v_ffn_norm, v_w_gate_up, v_w_down, v_final_norm):
    W = dict(attn_norm=attn_norm, w_in=w_in, mla_q_norm=mla_q_norm, w_uq=w_uq, mla_kv_norm=mla_kv_norm, w_ukv=w_ukv,
             conv_w=conv_w, pool_w=pool_w, pool_scale=pool_scale, swa_sinks=swa_sinks, mix_norm=mix_norm, w_o=w_o,
             ffn_norm=ffn_norm, w_gate_up=w_gate_up, w_down=w_down, final_norm=final_norm)
    M1 = dict(attn_norm=m_attn_norm, w_in=m_w_in, mla_q_norm=m_mla_q_norm, w_uq=m_w_uq, mla_kv_norm=m_mla_kv_norm,
              w_ukv=m_w_ukv, conv_w=m_conv_w, pool_w=m_pool_w, pool_scale=m_pool_scale, swa_sinks=m_swa_sinks,
              mix_norm=m_mix_norm, w_o=m_w_o, ffn_norm=m_ffn_norm, w_gate_up=m_w_gate_up, w_down=m_w_down,
              final_norm=m_final_norm)
    V2 = dict(attn_norm=v_attn_norm, w_in=v_w_in, mla_q_norm=v_mla_q_norm, w_uq=v_w_uq, mla_kv_norm=v_mla_kv_norm,
              w_ukv=v_w_ukv, conv_w=v_conv_w, pool_w=v_pool_w, pool_scale=v_pool_scale, swa_sinks=v_swa_sinks,
              mix_norm=v_mix_norm, w_o=v_w_o, ffn_norm=v_ffn_norm, w_gate_up=v_w_gate_up, w_down=v_w_down,
              final_norm=v_final_norm)
    S = x.shape[1]
    xc, yc, cc = _me()
    chip = 2 * xc + yc
    kc = jnp.stack([chip, cc]).astype(jnp.int32)

    first, later = ("w_in", "w_uq", "w_ukv", "conv_w"), ("w_o", "w_gate_up", "w_down")
    T = lambda a: jnp.swapaxes(a, 1, 2)
    W["w_in"], M1["w_in"], V2["w_in"] = T(w_in), T(m_w_in), T(v_w_in)
    placed = {n: _place(W[n], kc, F32 if n == "conv_w" else BF16, f"place_{n}") for n in first + later}
    gi, gq, gkv, gcv = _gather([placed[n] for n in first], "gather_weights")
    later_w = [placed[n] for n in later]
    win_p = _pad_w_in(gi.reshape(2, 4 * gi.shape[2], D))
    wuq_p = _pad_heads(_cols_joined(gq), UQ_SRC, ZERO_OFF)
    wukv = _cols_joined(gkv)
    wk_p = _pad_heads(wukv, KN_SRC, ZERO_OFF)
    wv_p = _pad_heads(wukv, V_SRC, V_OFF)
    conv8 = jnp.pad(_cols_joined(gcv), ((0, 0), (0, 5), (0, 0)))
    pwd = jnp.concatenate([jnp.concatenate(
        [jnp.pad(pool_w[:, 2 * b], ((0, 0), (0, 0), (0, 64))), jnp.pad(pool_w[:, 2 * b + 1], ((0, 0), (0, 0), (64, 0)))],
        axis=1) for b in range(2)], axis=1).astype(BF16)
    tabs = _rope_tables(S)
    g_attn, g_q, g_kv, g_mix, g_ffn, g_ps = (_g3(W[n]) for n in ("attn_norm", "mla_q_norm", "mla_kv_norm", "mix_norm",
                                                                  "ffn_norm", "pool_scale"))

    xs = [x[0]]
    saved = []
    for l in range(DEPTH):
        x0 = xs[-1]
        proj, h = _norm_mm(x0, g_attn, l, win_p, _wspec_in(l), D_INP, D_INP, F32, f"in_proj{l}", w_t=True)
        q, k, v, kt, vt = _mla_prep(proj, g_q, g_kv, wuq_p, wk_p, wv_p, tabs, l, f"mla_prep{l}")
        ya, lse, later_w = _mla_attn(q, k, vt, later_w, l, f"mla_attn{l}")
        go, gu4, gd = later_w
        wo, wdown = go.reshape(2, D, D), gd.reshape(2, D_FF, D)
        yb = _conv(proj, conv8, l, f"conv{l}")
        ycp = _pool(proj, pwd, g_ps, l, f"pool{l}")
        yd = _swa(proj, swa_sinks, l, f"swa{l}")
        x1, ycat, mixed = _mix_out(x0, ya, yb, ycp, yd, g_mix, wo, l, f"mix_out{l}")
        gu, h2 = _norm_mm(x1, g_ffn, l, gu4, _wspec_gu(l), 2 * D_FF, 2 * D_FF // 4, BF16, f"gate_up{l}")
        x2, act = _swiglu_mm_res(x1, gu, wdown, l, f"down{l}")
        saved.append(dict(x0=x0, proj=proj, h=h, q=q, k=k, kt=kt, v=v, lse=lse, x1=x1, ycat=ycat, mixed=mixed,
                          gu=gu, h2=h2, act=act))
        xs.append(x2)

    dx, dx16, dg_final, loss_tile = _loss_head(xs[-1], final_norm.reshape(1, D), loss_target[0], "loss_head")
    loss_here = (loss_tile[0, 0] * (0.5 / D)).reshape(1)

    G = {n: [None] * DEPTH for n in ("w_uq", "w_ukv") + TINY + REPL if n != "final_norm"}
    gw_in = gw_o = gw_gu = gw_down = None
    for l in reversed(range(DEPTH)):
        sv = saved[l]
        dgu = _bwd_down(dx16, wdown, sv["gu"], l, f"down_bwd{l}")
        gw_down = _mm_tn(sv["act"], dx16, l, gw_down, f"dw_down{l}")
        gw_gu = _mm_tn(sv["h2"], dgu, l, gw_gu, f"dw_gate_up{l}", split4=True)
        exchange_gu = exchange_down = None
        if l == 0:
            g_f = [gw_gu, gw_down.reshape(2, 4, D_FF // 4, D)]
            dx1, dx1_16, dg, got_f = _mm_nt_normbwd(dgu, gu4, l, sv["x1"], g_ffn, dx, 1, f"gate_up_bwd{l}",
                                                    rider=_swap_rider(g_f))
            pairs_f = [_pair_sum(g, o, kc, f"rs_pair_sum_{n}") for g, o, n in zip(g_f, got_f, FFN)]
            exchange_gu, exchange_down = _exchange_rider([pairs_f[0][1]]), _exchange_rider([pairs_f[1][1]])
        else:
            dx1, dx1_16, dg = _mm_nt_normbwd(dgu, gu4, l, sv["x1"], g_ffn, dx, 1, f"gate_up_bwd{l}")
        G["ffn_norm"][l] = dg[0]
        gw_o = _mm_tn(sv["mixed"], dx1_16, l, gw_o, f"dw_o{l}")
        dycat, dg = _mm_nt_normbwd(dx1_16, wo.reshape(2, 1, D, D), l, sv["ycat"], g_mix, None, 4, f"mix_bwd{l}")
        G["mix_norm"][l] = dg[0]

        proj = sv["proj"]
        delta = _mla_delta(dycat, sv["ycat"], f"mla_delta{l}")
        dq, dk, dv, got3_gu = _mla_attn_bwd(sv["q"], sv["k"], sv["kt"], sv["v"], dycat, sv["lse"], delta, exchange_gu,
                                            f"mla_attn_bwd{l}")
        dcq, dckv, dkr, dwuq, dwk, dwv, dgq, dgkv = _mla_prep_bwd(
            dq, dk, dv, proj, g_q, g_kv, wuq_p, wk_p, wv_p, tabs, l, f"mla_prep_bwd{l}")
        dgb, dgc, duc, dcw = _conv_bwd(proj, conv8, dycat, l, f"conv_bwd{l}")
        dup, dpw, dps = _pool_bwd(proj, pwd, g_ps, dycat, l, f"pool_bwd{l}")
        if l == 0:
            dqs, dks, dvs, dsink, got3_down = _swa_bwd(proj, swa_sinks, dycat, l, f"swa_bwd{l}", rider=exchange_down)
            got3_f = got3_gu + got3_down
        else:
            dqs, dks, dvs, dsink = _swa_bwd(proj, swa_sinks, dycat, l, f"swa_bwd{l}")
        dproj = jnp.concatenate([dcq, dckv, dkr, dgb, dgc, duc, dup, dqs, dks, dvs], axis=1)
        gw_in = _mm_tn(dproj, sv["h"], l, gw_in, f"dw_in{l}")
        G["w_uq"][l] = jnp.concatenate(_unpad_heads(dwuq, UQ_SRC, ZERO_OFF), axis=1)
        kn, vv = _unpad_heads(dwk, KN_SRC, ZERO_OFF), _unpad_heads(dwv, V_SRC, V_OFF)
        G["w_ukv"][l] = jnp.concatenate([t for h in range(HEADS) for t in (kn[h], vv[h])], axis=1)
        dx, dx16, dg = _mm_nt_normbwd(dproj, win_p.reshape(2, 1, D_INP, D), l, sv["x0"], g_attn, dx1, 1, f"in_proj_bwd{l}",
                                      w_t=True)
        G["attn_norm"][l] = dg[0]
        G["mla_q_norm"][l] = dgq[0]
        G["mla_kv_norm"][l] = dgkv[0]
        G["conv_w"][l] = dcw[0:3]
        G["pool_w"][l] = jnp.stack([dpw[0:64, 0:64], dpw[64:128, 64:128], dpw[128:192, 0:64], dpw[192:256, 64:128]])
        G["pool_scale"][l] = dps[0]
        G["swa_sinks"][l] = dsink[0, 0:4]
    grad_x = dx[None]
    Gl = {n: jnp.stack(G[n]) for n in TINY + REPL if n != "final_norm"}
    Gl["final_norm"] = dg_final[0]

    us_f = [_chip_sum(p[0], o3, kc, f"rs_chip_sum_{n}") for p, o3, n in zip(pairs_f, got3_f, FFN)]
    gsum_f = _join_layers(us_f, "rs_join_cores_ffn")
    g_r = [_unpad_w_in(gw_in).reshape(2, 4, -1, D), gw_o.reshape(2, 4, D // 4, D), _cols_split(jnp.stack(G["w_uq"])),
           _cols_split(jnp.stack(G["w_ukv"]))]
    got_r = _ride_alone(_swap_rider(g_r), "rs_swap_cores")
    pairs_r = [_pair_sum(g, o, kc, f"rs_pair_sum_{n}") for g, o, n in zip(g_r, got_r, REST)]
    small = TINY + REPL
    full_shapes = [Gl[n].shape for n in small] + [(1,)]
    summed, got3_r = _allsum_small(_pack([Gl[n] for n in small] + [loss_here]), "allsum_small",
                                   rider=_exchange_rider([p[1] for p in pairs_r]))
    summed = _unpack(summed, full_shapes)
    loss = summed.pop().reshape(())
    us_r = [_chip_sum(p[0], o3, kc, f"rs_chip_sum_{n}") for p, o3, n in zip(pairs_r, got3_r, REST)]
    gsum_r = _join_layers(us_r, "rs_join_cores")
    res = {}
    for n, g in zip(BIG, gsum_f + gsum_r):
        d_, m_, v_, g_ = _adamw(W[n], g, M1[n], V2[n], f"adamw_{n}", echo=True)
        back = T if n == "w_in" else (lambda a: a)
        res["g", n], res["d", n], res["m", n], res["v", n] = back(g_), back(d_), back(m_), back(v_)

    def as3(a):
        if a.ndim <= 2:
            return a.reshape((1,) * (3 - a.ndim) + a.shape)
        return a.reshape(a.shape[0], -1, a.shape[-1])

    for n, g in zip(small, summed):
        if n in TINY:
            wdt = W[n].shape[2]
            g = lax.dynamic_slice_in_dim(g, chip * wdt, wdt, axis=2)
        out = _adamw(as3(W[n]), as3(g), as3(M1[n]), as3(V2[n]), f"adamw_{n}")
        res["g", n] = g
        res["d", n], res["m", n], res["v", n] = (o.reshape(W[n].shape) for o in out)

    return (loss, grad_x, *[res["g", n] for n in ORDER], *[res["d", n] for n in ORDER],
            *[res["m", n] for n in ORDER], *[res["v", n] for n in ORDER])
```

```python
import math

import numpy as np
import jax
import jax.numpy as jnp
from jax import lax
from jax.experimental import pallas as pl
from jax.experimental.pallas import tpu as pltpu

F32, BF16 = jnp.float32, jnp.bfloat16
SDS = jax.ShapeDtypeStruct
BS = pl.BlockSpec
MESH = pl.DeviceIdType.MESH

D = 1024
DEPTH = 2
HEADS = 4
D_FF = 2816
D_INP = 2048
EPS = 1e-6
SWA_WINDOW = 128
BLK = 128
SLOPES = tuple(2.0 ** (-8.0 * (i + 1) / 4) for i in range(4))
QK_SCALE = 1.0 / math.sqrt(96)
SWA_SCALE = 1.0 / math.sqrt(64)
LR, B1, B2, ADAM_EPS, WD, STEP = 0.001, 0.9, 0.999, 1e-08, 0.01, 10

LANES = 1024
VMEM_LIMIT = 56 * 1024 * 1024
NEG_INF = float("-inf")

C_CQ, C_CKV, C_KR, C_GB, C_GC, C_UC, C_UP, C_QS, C_KS, C_VS = 0, 256, 384, 512, 768, 1024, 1280, 1536, 1792, 1920


def _params(ngrid):
    return pltpu.CompilerParams(dimension_semantics=("arbitrary",) * ngrid, vmem_limit_bytes=VMEM_LIMIT)


def _pc(body, *, name, grid, in_specs, out_specs, out_shape, scratch=(), aliases=None):
    return pl.pallas_call(
        body, name=name, grid=grid, in_specs=in_specs, out_specs=out_specs, out_shape=out_shape,
        scratch_shapes=scratch, input_output_aliases=aliases or {}, compiler_params=_params(len(grid)))


def _dot(a, b):
    return jnp.dot(a, b, preferred_element_type=F32)


def _dot_nt(a, b):
    return lax.dot_general(a, b, (((1,), (1,)), ((), ())), preferred_element_type=F32)


def _dot_tn(a, b):
    return lax.dot_general(a, b, (((0,), (0,)), ((), ())), preferred_element_type=F32)


def _tile(n, cap):
    if n <= cap:
        return n
    t = cap - cap % 128
    while n % t:
        t -= 128
    return t


def _row_tile(a, b, cap=262144):
    bp = -(-b // 128) * 128
    best = None
    for t in range(8, a + 1, 8):
        if a % t == 0 and t * bp <= cap:
            best = t
    if best is None or (best < 64 and a * bp <= 2 * cap):
        return a
    return best


def _g3(a):
    return a.reshape(a.shape[0], 1, a.shape[1])


def _norm_mm(x, g3, l, w, wspec, N, tn, out_dtype, name, w_t=False):
    S, K = x.shape
    tm = min(1024 if out_dtype == BF16 else 512, S)

    def body(x_ref, g_ref, w_ref, y_ref, h_ref):
        @pl.when(pl.program_id(1) == 0)
        def _():
            xv = x_ref[...]
            r = lax.rsqrt(jnp.mean(xv * xv, axis=-1, keepdims=True) + EPS)
            h_ref[...] = (xv * r * g_ref[...]).astype(BF16)

        y_ref[...] = (_dot_nt if w_t else _dot)(h_ref[...], w_ref[...]).astype(out_dtype)

    return _pc(body, name=name, grid=(S // tm, N // tn),
               in_specs=[BS((tm, K), lambda i, j: (i, 0)), BS((None, 1, K), lambda i, j: (l, 0, 0)), wspec],
               out_specs=[BS((tm, tn), lambda i, j: (i, j)), BS((tm, K), lambda i, j: (i, 0))],
               out_shape=[SDS((S, N), out_dtype), SDS((S, K), BF16)])(x, g3, w)


def _wspec_in(l):
    return BS((None, D_INP, D), lambda i, j: (l, j, 0))


def _wspec_gu(l):
    return BS((None, None, D, 2 * D_FF // 4), lambda i, j: (l, j, 0, 0))


def _mix_out(x0, ya, yb, yc, yd, gmix3, wo, l, name):
    S = x0.shape[0]
    tm = min(512, S)

    def body(x_ref, ya_ref, yb_ref, yc_ref, yd_ref, g_ref, w_ref, x1_ref, ycat_ref, mixed_ref):
        groups = [ya_ref[...], yb_ref[...], yc_ref[...], yd_ref[...]]
        for gi, yg in enumerate(groups):
            sl = slice(gi * 256, (gi + 1) * 256)
            r = lax.rsqrt(jnp.mean(yg * yg, axis=-1, keepdims=True) + EPS)
            ycat_ref[:, sl] = yg
            mixed_ref[:, sl] = (yg * r * g_ref[:, sl]).astype(BF16)
        x1_ref[...] = x_ref[...] + _dot(mixed_ref[...], w_ref[...])

    row = lambda w: BS((tm, w), lambda i: (i, 0))
    return _pc(body, name=name, grid=(S // tm,),
               in_specs=[row(D), row(256), row(256), row(256), row(256), BS((None, 1, D), lambda i: (l, 0, 0)),
                         BS((None, D, D), lambda i: (l, 0, 0))],
               out_specs=[row(D), row(D), row(D)],
               out_shape=[SDS((S, D), F32), SDS((S, D), F32), SDS((S, D), BF16)])(x0, ya, yb, yc, yd, gmix3, wo)


def _swiglu_mm_res(x1, gu, wdown, l, name):
    S = x1.shape[0]
    tm = min(256, S)

    def body(x_ref, gate_ref, up_ref, w_ref, x2_ref, act_ref):
        acc = x_ref[...]
        for c0 in range(0, D_FF, D_FF // 2):
            cs = slice(c0, c0 + D_FF // 2)
            gt = gate_ref[:, cs].astype(F32)
            act = (gt * pl.reciprocal(1.0 + jnp.exp(-gt), approx=True) * up_ref[:, cs].astype(F32)).astype(BF16)
            act_ref[:, cs] = act
            acc = acc + _dot(act, w_ref[cs, :])
        x2_ref[...] = acc

    return _pc(body, name=name, grid=(S // tm,),
               in_specs=[BS((tm, D), lambda i: (i, 0)), BS((tm, D_FF), lambda i: (i, 0)),
                         BS((tm, D_FF), lambda i: (i, 1)), BS((None, D_FF, D), lambda i: (l, 0, 0))],
               out_specs=[BS((tm, D), lambda i: (i, 0)), BS((tm, D_FF), lambda i: (i, 0))],
               out_shape=[SDS((S, D), F32), SDS((S, D_FF), BF16)])(x1, gu, gu, wdown)


def _loss_head(x, g, tgt, name):
    S = x.shape[0]
    tm = min(512, S)

    def body(x_ref, g_ref, t_ref, dx_ref, dx16_ref, dg_ref, loss_ref):
        @pl.when(pl.program_id(0) == 0)
        def _():
            dg_ref[...] = jnp.zeros_like(dg_ref)
            loss_ref[...] = jnp.zeros_like(loss_ref)

        xv = x_ref[...]
        r = lax.rsqrt(jnp.mean(xv * xv, axis=-1, keepdims=True) + EPS)
        xh = xv * r
        gv = g_ref[...]
        diff = xh * gv - t_ref[...]
        loss_ref[...] += jnp.sum(diff * diff)
        dy = diff * (1.0 / D)
        dg_ref[...] += jnp.sum(dy * xh, axis=0, keepdims=True)
        dxh = dy * gv
        dx = r * (dxh - xh * jnp.mean(dxh * xh, axis=-1, keepdims=True))
        dx_ref[...] = dx
        dx16_ref[...] = dx.astype(BF16)

    row = BS((tm, D), lambda i: (i, 0))
    return _pc(body, name=name, grid=(S // tm,),
               in_specs=[row, BS((1, D), lambda i: (0, 0)), row],
               out_specs=[row, row, BS((8, D), lambda i: (0, 0)), BS((8, 128), lambda i: (0, 0))],
               out_shape=[SDS((S, D), F32), SDS((S, D), BF16), SDS((8, D), F32), SDS((8, 128), F32)])(x, g, tgt)


def _mm_tn(a, b, l, prev, name, split4=False):
    S, Ka = a.shape
    N = b.shape[1]
    if split4:
        ta, tn = _tile(Ka, 256), N // 4
        out_shape = SDS((2, 4, Ka, tn), F32)
        out_spec = BS((None, None, ta, tn), lambda j, i: (l, j, i, 0))
    else:
        ta, tn = _tile(Ka, 512), _tile(N, 1024)
        out_shape = SDS((2, Ka, N), F32)
        out_spec = BS((None, ta, tn), lambda j, i: (l, i, j))

    def body(a_ref, b_ref, *rest):
        rest[-1][...] = _dot_tn(a_ref[...], b_ref[...])

    in_specs = [BS((S, ta), lambda j, i: (0, i)), BS((S, tn), lambda j, i: (0, j))]
    args = [a, b]
    if prev is not None:
        in_specs.append(BS(memory_space=pl.ANY))
        args.append(prev)
    return _pc(body, name=name, grid=(N // tn, Ka // ta), in_specs=in_specs, out_specs=out_spec, out_shape=out_shape,
               aliases={2: 0} if prev is not None else None)(*args)


def _bwd_down(dx16, wdown, gu, l, name):
    S = dx16.shape[0]
    tm = min(256, S)

    def body(dx_ref, w_ref, gate_ref, up_ref, dgu_ref):
        dxv = dx_ref[...]
        for c0 in range(0, D_FF, 256):
            cs = slice(c0, c0 + 256)
            dact = _dot_nt(dxv, w_ref[cs, :])
            gt = gate_ref[:, cs].astype(F32)
            sg = pl.reciprocal(1.0 + jnp.exp(-gt), approx=True)
            dgu_ref[:, cs] = (dact * up_ref[:, cs].astype(F32) * (sg * (1.0 + gt * (1.0 - sg)))).astype(BF16)
            dgu_ref[:, D_FF + c0:D_FF + c0 + 256] = (dact * (gt * sg)).astype(BF16)

    return _pc(body, name=name, grid=(S // tm,),
               in_specs=[BS((tm, D), lambda i: (i, 0)), BS((None, D_FF, D), lambda i: (l, 0, 0)),
                         BS((tm, D_FF), lambda i: (i, 0)), BS((tm, D_FF), lambda i: (i, 1))],
               out_specs=BS((tm, 2 * D_FF), lambda i: (i, 0)),
               out_shape=SDS((S, 2 * D_FF), BF16))(dx16, wdown, gu, gu)


def _mm_nt_normbwd(dy, w4, l, x, g3, dres, ngroups, name, rider=None, w_t=False):
    S, K = dy.shape
    nk, kc = w4.shape[1], w4.shape[2 if w_t else 3]
    mm = _dot if w_t else _dot_nt
    tm = min(512, S)
    gw = D // ngroups
    has_res = dres is not None
    nr = rider.n if rider else 0
    n_in, n_out = 4 + has_res, 2 + has_res

    def body(*refs):
        dy_ref, w_ref, x_ref, g_ref = refs[:4]
        res_ref = refs[4] if has_res else None
        outs = refs[n_in + nr:n_in + nr + n_out]
        dx_ref, dg_ref = outs[0], outs[-1]
        dx16_ref = outs[1] if has_res else None
        r_io = (refs[n_in:n_in + nr], refs[n_in + nr + n_out:n_in + 2 * nr + n_out], refs[n_in + 2 * nr + n_out:])
        if rider:
            pl.when(pl.program_id(0) == 0)(lambda: rider.start(*r_io))

        @pl.when(pl.program_id(0) == 0)
        def _():
            dg_ref[...] = jnp.zeros_like(dg_ref)

        dh = mm(dy_ref[:, 0:kc], w_ref[0])
        for k in range(1, nk):
            dh = dh + mm(dy_ref[:, k * kc:(k + 1) * kc], w_ref[k])
        for gi in range(ngroups):
            sl = slice(gi * gw, (gi + 1) * gw)
            xg = x_ref[:, sl]
            r = lax.rsqrt(jnp.mean(xg * xg, axis=-1, keepdims=True) + EPS)
            xh = xg * r
            dhg = dh[:, sl]
            dg_ref[:, sl] += jnp.sum(dhg * xh, axis=0, keepdims=True)
            dxh = dhg * g_ref[:, sl]
            dxg = r * (dxh - xh * jnp.mean(dxh * xh, axis=-1, keepdims=True))
            if has_res:
                dxg = dxg + res_ref[:, sl]
                dx16_ref[:, sl] = dxg.astype(BF16)
            dx_ref[:, sl] = dxg
        if rider:
            pl.when(pl.program_id(0) == S // tm - 1)(lambda: rider.wait(*r_io))

    row = BS((tm, D), lambda i: (i, 0))
    in_specs = [BS((tm, K), lambda i: (i, 0)),
                BS((None,) + tuple(w4.shape[1:]), lambda i: (l, 0, 0, 0), pipeline_mode=pl.Buffered(1)), row,
                BS((None, 1, D), lambda i: (l, 0, 0))]
    args = [dy, w4, x, g3]
    out_specs, out_shape = [row], [SDS((S, D), F32)]
    if has_res:
        in_specs.append(row)
        args.append(dres)
        out_specs.append(row)
        out_shape.append(SDS((S, D), BF16))
    out_specs.append(BS((8, D), lambda i: (0, 0)))
    out_shape.append(SDS((8, D), F32))
    if not rider:
        return _pc(body, name=name, grid=(S // tm,), in_specs=in_specs, out_specs=out_specs, out_shape=out_shape)(*args)
    out = pl.pallas_call(body, name=name, grid=(S // tm,), in_specs=in_specs + [ANY] * nr,
                         out_specs=out_specs + [ANY] * nr, out_shape=out_shape + rider.out_shape,
                         scratch_shapes=rider.scratch(),
                         compiler_params=pltpu.CompilerParams(dimension_semantics=("arbitrary",),
                                                              vmem_limit_bytes=VMEM_LIMIT, has_side_effects=True))(
        *args, *rider.arrs)
    return (*out[:n_out], list(out[n_out:]))


def _rope(x, c, s1, s2):
    return x * c + pltpu.roll(x, 112, axis=1) * s1 + pltpu.roll(x, 16, axis=1) * s2


def _rope_t(dy, c, s1, s2):
    return dy * c + pltpu.roll(dy * s1, 16, axis=1) + pltpu.roll(dy * s2, 112, axis=1)


def _mla_prep(proj, gq3, gkv3, wuq, wk, wv, tabs, l, name):
    S = proj.shape[0]
    tm = min(512, S)
    tc, ts1, ts2 = tabs

    def body(cq_ref, ckv_ref, kr_ref, gq_ref, gkv_ref, wuq_ref, wk_ref, wv_ref, c_ref, s1_ref, s2_ref,
             q_ref, k_ref, v_ref, kt_ref, vt_ref):
        c, s1, s2 = c_ref[...], s1_ref[...], s2_ref[...]
        cq = cq_ref[...]
        rq = lax.rsqrt(jnp.mean(cq * cq, axis=-1, keepdims=True) + EPS)
        qa = _dot((cq * rq * gq_ref[...]).astype(BF16), wuq_ref[...])
        ckv = ckv_ref[...]
        rkv = lax.rsqrt(jnp.mean(ckv * ckv, axis=-1, keepdims=True) + EPS)
        ckvn = (ckv * rkv * gkv_ref[...]).astype(BF16)
        ka = _dot(ckvn, wk_ref[...])
        va = _dot(ckvn, wv_ref[...])
        v_ref[...] = va.astype(BF16)
        vt_ref[...] = va.T.astype(BF16)
        krr = _rope(kr_ref[...], c, s1, s2)
        for h in range(HEADS):
            sl = slice(h * 128, (h + 1) * 128)
            q_ref[:, sl] = (_rope(qa[:, sl], c, s1, s2) * QK_SCALE).astype(BF16)
            kh = ka[:, sl] + krr
            k_ref[:, sl] = kh.astype(BF16)
            kt_ref[sl, :] = kh.T.astype(BF16)

    lay = lambda a, b: BS((None, a, b), lambda i: (l, 0, 0))
    tab = BS((tm, 128), lambda i: (i, 0))
    return _pc(body, name=name, grid=(S // tm,),
               in_specs=[BS((tm, 256), lambda i: (i, 0)), BS((tm, 128), lambda i: (i, 2)), BS((tm, 128), lambda i: (i, 3)),
                         lay(1, 256), lay(1, 128), lay(256, 512), lay(128, 512), lay(128, 512), tab, tab, tab],
               out_specs=[BS((tm, 512), lambda i: (i, 0))] * 3 + [BS((512, tm), lambda i: (0, i))] * 2,
               out_shape=[SDS((S, 512), BF16)] * 3 + [SDS((512, S), BF16)] * 2)(
        proj, proj, proj, gq3, gkv3, wuq, wk, wv, tc, ts1, ts2)


def _mla_prep_bwd(dq, dk, dv, proj, gq3, gkv3, wuq, wk, wv, tabs, l, name):
    S = proj.shape[0]
    tm = min(512, S)
    tc, ts1, ts2 = tabs

    def body(dq_ref, dk_ref, dv_ref, cq_ref, ckv_ref, gq_ref, gkv_ref, wuq_ref, wk_ref, wv_ref, c_ref, s1_ref, s2_ref,
             dcq_ref, dckv_ref, dkr_ref, dwuq_ref, dwk_ref, dwv_ref, dgq_ref, dgkv_ref):
        @pl.when(pl.program_id(0) == 0)
        def _():
            for r in (dwuq_ref, dwk_ref, dwv_ref, dgq_ref, dgkv_ref):
                r[...] = jnp.zeros_like(r)

        c, s1, s2 = c_ref[...], s1_ref[...], s2_ref[...]
        dqp = jnp.concatenate(
            [_rope_t(dq_ref[h * 128:(h + 1) * 128, :].T * QK_SCALE, c, s1, s2) for h in range(HEADS)], axis=1).astype(BF16)
        cq = cq_ref[...]
        rq = lax.rsqrt(jnp.mean(cq * cq, axis=-1, keepdims=True) + EPS)
        cqh = cq * rq
        gq_v = gq_ref[...]
        dwuq_ref[...] += _dot_tn((cqh * gq_v).astype(BF16), dqp)
        dcqn = _dot_nt(dqp, wuq_ref[...])
        dgq_ref[...] += jnp.sum(dcqn * cqh, axis=0, keepdims=True)
        dxh = dcqn * gq_v
        dcq_ref[...] = (rq * (dxh - cqh * jnp.mean(dxh * cqh, axis=-1, keepdims=True))).astype(BF16)

        dkb = dk_ref[...].astype(BF16)
        dvb = dv_ref[...].astype(BF16)
        ckv = ckv_ref[...]
        rkv = lax.rsqrt(jnp.mean(ckv * ckv, axis=-1, keepdims=True) + EPS)
        ckh = ckv * rkv
        gkv_v = gkv_ref[...]
        ckvn = (ckh * gkv_v).astype(BF16)
        dwk_ref[...] += _dot_tn(ckvn, dkb)
        dwv_ref[...] += _dot_tn(ckvn, dvb)
        dckvn = _dot_nt(dkb, wk_ref[...]) + _dot_nt(dvb, wv_ref[...])
        dgkv_ref[...] += jnp.sum(dckvn * ckh, axis=0, keepdims=True)
        dyh = dckvn * gkv_v
        dckv_ref[...] = (rkv * (dyh - ckh * jnp.mean(dyh * ckh, axis=-1, keepdims=True))).astype(BF16)
        dks = dk_ref[:, 0:128] + dk_ref[:, 128:256] + dk_ref[:, 256:384] + dk_ref[:, 384:512]
        dkr_ref[...] = _rope_t(dks, c, s1, s2).astype(BF16)

    full = lambda a, b: BS((a, b), lambda i: (0, 0))
    lay = lambda a, b: BS((None, a, b), lambda i: (l, 0, 0))
    tab = BS((tm, 128), lambda i: (i, 0))
    row = lambda w: BS((tm, w), lambda i: (i, 0))
    return _pc(body, name=name, grid=(S // tm,),
               in_specs=[BS((512, tm), lambda i: (0, i)), row(512), row(512), BS((tm, 256), lambda i: (i, 0)),
                         BS((tm, 128), lambda i: (i, 2)),
                         lay(1, 256), lay(1, 128), lay(256, 512), lay(128, 512), lay(128, 512), tab, tab, tab],
               out_specs=[row(256), row(128), row(128), full(256, 512), full(128, 512), full(128, 512),
                          full(8, 256), full(8, 128)],
               out_shape=[SDS((S, 256), BF16), SDS((S, 128), BF16), SDS((S, 128), BF16), SDS((256, 512), F32),
                          SDS((128, 512), F32), SDS((128, 512), F32), SDS((8, 256), F32), SDS((8, 128), F32)])(
        dq, dk, dv, proj, proj, gq3, gkv3, wuq, wk, wv, tc, ts1, ts2)


def _causal_steps(n, q_outer):
    if q_outer:
        pairs = [(i, j) for i in range(n) for j in range(i + 1)]
    else:
        pairs = [(i, j) for j in range(n) for i in range(j, n)]
    return jnp.asarray([p[0] for p in pairs], jnp.int32), jnp.asarray([p[1] for p in pairs], jnp.int32)


def _mla_attn(q, k, vt, gts, layer, name):
    S = q.shape[0]
    t = min(512, S)
    n = S // t
    ng = len(gts)

    qi, kj = _causal_steps(n, True)
    last = qi.shape[0] - 1

    def body(qi_ref, kj_ref, q_ref, k_ref, vt_ref, *rest):
        (ya_ref, lse_ref), g_refs = rest[ng:ng + 2], rest[ng + 2:2 * ng + 2]
        m_sc, l_sc, acc_sc = rest[2 * ng + 2:2 * ng + 5]
        i, j = qi_ref[pl.program_id(1)], kj_ref[pl.program_id(1)]
        if ng:
            phases = _gather_phases(g_refs, [g.shape for g in gts], rest[2 * ng + 5], rest[2 * ng + 6], layer)
            for ph, (pp, ss) in zip(phases[:2], ((0, 0), (1, 0))):
                pl.when((pl.program_id(0) == pp) & (pl.program_id(1) == ss))(ph)

        @pl.when(j == 0)
        def _():
            m_sc[...] = jnp.full_like(m_sc, NEG_INF)
            l_sc[...] = jnp.zeros_like(l_sc)
            acc_sc[...] = jnp.zeros_like(acc_sc)

        def step(masked):
            for hh in range(2):
                sl = slice(hh * 128, (hh + 1) * 128)
                st = _dot_nt(k_ref[:, sl], q_ref[:, sl])
                if masked:
                    key = lax.broadcasted_iota(jnp.int32, (t, t), 0)
                    qry = lax.broadcasted_iota(jnp.int32, (t, t), 1)
                    st = jnp.where(key <= qry, st, NEG_INF)
                m_prev = m_sc[hh]
                m_new = jnp.maximum(m_prev, jnp.max(st, axis=0, keepdims=True))
                p = jnp.exp(st - m_new)
                alpha = jnp.exp(m_prev - m_new)
                l_sc[hh] = alpha * l_sc[hh] + jnp.sum(p, axis=0, keepdims=True)
                acc_sc[hh] = alpha * acc_sc[hh] + _dot(vt_ref[sl, :], p.astype(BF16))
                m_sc[hh] = m_new

        @pl.when(j < i)
        def _():
            step(False)

        @pl.when(j == i)
        def _():
            step(True)
            ya_ref[...] = (acc_sc[0] / l_sc[0] + acc_sc[1] / l_sc[1]).T
            for hh in range(2):
                lse_ref[hh] = m_sc[hh] + jnp.log(l_sc[hh])

        if ng:
            pl.when((pl.program_id(0) == 1) & (pl.program_id(1) == last))(phases[2])

    gs = pltpu.PrefetchScalarGridSpec(
        num_scalar_prefetch=2, grid=(2, qi.shape[0]),
        in_specs=[BS((t, 256), lambda p, s, qi, kj: (qi[s], p)), BS((t, 256), lambda p, s, qi, kj: (kj[s], p)),
                  BS((256, t), lambda p, s, qi, kj: (p, kj[s]))] + [ANY] * ng,
        out_specs=[BS((t, 128), lambda p, s, qi, kj: (qi[s], p)), BS((2, 1, t), lambda p, s, qi, kj: (p, 0, qi[s]))]
        + [ANY] * ng,
        scratch_shapes=[pltpu.VMEM((2, 1, t), F32), pltpu.VMEM((2, 1, t), F32), pltpu.VMEM((2, 128, t), F32)]
        + ([pltpu.SemaphoreType.DMA((7 * ng,)), pltpu.SemaphoreType.DMA((7 * ng,))] if ng else []))
    out = pl.pallas_call(body, name=name, grid_spec=gs,
                         out_shape=[SDS((S, 256), F32), SDS((HEADS, 1, S), F32)] + [SDS(g.shape, g.dtype) for g in gts],
                         input_output_aliases={5 + m: 2 + m for m in range(ng)},
                         compiler_params=pltpu.CompilerParams(dimension_semantics=("arbitrary",) * 2,
                                                              vmem_limit_bytes=VMEM_LIMIT, has_side_effects=bool(ng)))(
        qi, kj, q, k, vt, *gts)
    return out[0], out[1], list(out[2:])


def _mla_delta(dycat, ya, name):
    S = ya.shape[0]
    t = min(512, S)

    def body(do_ref, ya_ref, d_ref):
        prod = do_ref[...] * ya_ref[...]
        for p in range(2):
            pt = prod[:, p * 128:(p + 1) * 128].T
            d_ref[2 * p] = jnp.sum(pt[0:64, :], axis=0, keepdims=True)
            d_ref[2 * p + 1] = jnp.sum(pt[64:128, :], axis=0, keepdims=True)

    return _pc(body, name=name, grid=(S // t,),
               in_specs=[BS((t, 256), lambda i: (i, 0)), BS((t, 256), lambda i: (i, 0))],
               out_specs=BS((HEADS, 1, t), lambda i: (0, 0, i)), out_shape=SDS((HEADS, 1, S), F32))(dycat, ya)


def _mla_attn_bwd(q, k, kt, v, dya, lse, delta, rider, name):
    S = q.shape[0]
    t = min(512, S)
    n = S // t
    nr = rider.n if rider else 0

    qi, kj = _causal_steps(n, False)
    last = qi.shape[0] - 1

    def body(qi_ref, kj_ref, q_ref, k_ref, kt_ref, v_ref, do_ref, lse_ref, delta_ref, *rest):
        dqt_ref, dk_ref, dv_ref = rest[nr:nr + 3]
        r_io = (rest[:nr], rest[nr + 3:2 * nr + 3], rest[2 * nr + 3:])
        i, j = qi_ref[pl.program_id(1)], kj_ref[pl.program_id(1)]
        if rider:
            pl.when((pl.program_id(0) == 0) & (pl.program_id(1) == 0))(lambda: rider.start(*r_io))

        @pl.when(pl.program_id(1) == 0)
        def _():
            dqt_ref[...] = jnp.zeros_like(dqt_ref)

        @pl.when(i == j)
        def _():
            dk_ref[...] = jnp.zeros_like(dk_ref)
            dv_ref[...] = jnp.zeros_like(dv_ref)

        def step(masked):
            dob = do_ref[...].astype(BF16)
            cols = pl.ds(pl.multiple_of(i * t, t), t)
            for hh in range(2):
                sl = slice(hh * 128, (hh + 1) * 128)
                qv = q_ref[:, sl]
                p = jnp.exp(_dot_nt(k_ref[:, sl], qv) - lse_ref[hh])
                if masked:
                    key = lax.broadcasted_iota(jnp.int32, (t, t), 0)
                    qry = lax.broadcasted_iota(jnp.int32, (t, t), 1)
                    p = jnp.where(key <= qry, p, 0.0)
                dv_ref[:, sl] += _dot(p.astype(BF16), dob)
                ds = (p * (_dot_nt(v_ref[:, sl], dob) - delta_ref[hh])).astype(BF16)
                dk_ref[:, sl] += _dot(ds, qv)
                dqt_ref[sl, cols] += _dot(kt_ref[sl, :], ds)

        @pl.when(i > j)
        def _():
            step(False)

        @pl.when(i == j)
        def _():
            step(True)

        if rider:
            pl.when((pl.program_id(0) == 1) & (pl.program_id(1) == last))(lambda: rider.wait(*r_io))

    qs = BS((t, 256), lambda p, s, qi, kj: (qi[s], p))
    ks = BS((t, 256), lambda p, s, qi, kj: (kj[s], p))
    rowv = BS((2, 1, t), lambda p, s, qi, kj: (p, 0, qi[s]))
    gs = pltpu.PrefetchScalarGridSpec(
        num_scalar_prefetch=2, grid=(2, qi.shape[0]),
        in_specs=[qs, ks, BS((256, t), lambda p, s, qi, kj: (p, kj[s])), ks,
                  BS((t, 128), lambda p, s, qi, kj: (qi[s], p)), rowv, rowv] + [ANY] * nr,
        out_specs=[BS((256, S), lambda p, s, qi, kj: (p, 0)), ks, ks] + [ANY] * nr,
        scratch_shapes=rider.scratch() if rider else [])
    out = pl.pallas_call(body, name=name, grid_spec=gs,
                         out_shape=[SDS((512, S), F32), SDS((S, 512), F32), SDS((S, 512), F32)]
                         + (rider.out_shape if rider else []),
                         compiler_params=pltpu.CompilerParams(dimension_semantics=("arbitrary",) * 2,
                                                              vmem_limit_bytes=VMEM_LIMIT, has_side_effects=bool(rider)))(
        qi, kj, q, k, kt, v, dya, lse, delta, *(rider.arrs if rider else []))
    return out[0], out[1], out[2], list(out[3:])


def _swa_scores(qm, kk, valid, bias, sink):
    sc = jnp.where(valid, _dot_nt(qm, kk) * SWA_SCALE + bias, NEG_INF)
    m = jnp.maximum(jnp.max(sc, axis=-1, keepdims=True), sink)
    e = jnp.exp(sc - m)
    esink = jnp.exp(sink - m)
    den = jnp.sum(e, axis=-1, keepdims=True) + esink
    return e / den, esink / den


def _swa_consts(sink_ref, l):
    rows = HEADS * BLK
    r = lax.broadcasted_iota(jnp.int32, (rows, 2 * BLK), 0)
    c = lax.broadcasted_iota(jnp.int32, (rows, 2 * BLK), 1)
    dist = (r & (BLK - 1)) + BLK - c
    head = lax.broadcasted_iota(jnp.int32, (rows, 1), 0) // BLK

    def per_head(vals):
        return jnp.where(head == 0, vals[0], jnp.where(head == 1, vals[1], jnp.where(head == 2, vals[2], vals[3])))

    bias = -per_head(SLOPES) * dist.astype(F32)
    sink = per_head([sink_ref[l, h] for h in range(HEADS)])
    return (dist >= 0) & (dist < SWA_WINDOW), c >= BLK, bias, sink, head


def _to_half(xb, pos, b):
    return xb if pos == b else pltpu.roll(xb, 64, axis=1)


def _swa_stack(ref, st, lo):
    parts = []
    for b in range(2):
        xb = ref[pl.ds(st, BLK), b * 128:(b + 1) * 128]
        half = lo if b == 0 else ~lo
        parts += [jnp.where(half, _to_half(xb, pos, b), 0.0).astype(BF16) for pos in range(2)]
    return jnp.concatenate(parts, axis=0)


def _swa_unstack(x_all, lo):
    blocks = []
    for b in range(2):
        h0, h1 = (_to_half(x_all[(2 * b + pos) * BLK:(2 * b + pos + 1) * BLK], pos, b) for pos in range(2))
        blocks.append(jnp.where(lo, h0, h1))
    return blocks


def _swa(proj, sinks, l, name):
    S = proj.shape[0]
    nb = S // BLK

    def body(q_ref, k_ref, v_ref, sink_ref, o_ref, kp, vp):
        kp[0:BLK, :] = jnp.zeros((BLK, 128), BF16)
        vp[0:BLK, :] = jnp.zeros((BLK, 128), BF16)
        kp[BLK:, :] = k_ref[...].astype(BF16)
        vp[BLK:, :] = v_ref[...].astype(BF16)
        lo = lax.broadcasted_iota(jnp.int32, (BLK, 128), 1) < 64
        band, cur, bias, sink, _ = _swa_consts(sink_ref, l)

        def blk(i, carry):
            st = pl.multiple_of(i * BLK, BLK)
            kk = kp[pl.ds(st, 2 * BLK), :]
            vv = vp[pl.ds(st, 2 * BLK), :]
            p, _ = _swa_scores(_swa_stack(q_ref, st, lo), kk, band & (cur | (i > 0)), bias, sink)
            for b, ob in enumerate(_swa_unstack(_dot(p.astype(BF16), vv), lo)):
                o_ref[pl.ds(st, BLK), b * 128:(b + 1) * 128] = ob
            return carry

        lax.fori_loop(0, nb, blk, 0, unroll=2)

    return _pc(body, name=name, grid=(1,),
               in_specs=[BS((S, 256), lambda i: (0, C_QS // 256)), BS((S, 128), lambda i: (0, C_KS // 128)),
                         BS((S, 128), lambda i: (0, C_VS // 128)), BS(memory_space=pltpu.SMEM)],
               out_specs=BS((S, 256), lambda i: (0, 0)),
               out_shape=SDS((S, 256), F32),
               scratch=[pltpu.VMEM((S + BLK, 128), BF16), pltpu.VMEM((S + BLK, 128), BF16)])(proj, proj, proj, sinks)


def _swa_bwd(proj, sinks, dyd, l, name, rider=None):
    S = proj.shape[0]
    nb = S // BLK
    nr = rider.n if rider else 0

    def body(q_ref, k_ref, v_ref, sink_ref, do_ref, *rest):
        dq_ref, dk_ref, dv_ref, dsink_ref = rest[nr:nr + 4]
        kp, vp, dkp, dvp = rest[2 * nr + 4:2 * nr + 8]
        r_io = (rest[:nr], rest[nr + 4:2 * nr + 4], rest[2 * nr + 8:])
        if rider:
            rider.start(*r_io)
        kp[0:BLK, :] = jnp.zeros((BLK, 128), BF16)
        vp[0:BLK, :] = jnp.zeros((BLK, 128), BF16)
        kp[BLK:, :] = k_ref[...].astype(BF16)
        vp[BLK:, :] = v_ref[...].astype(BF16)
        dkp[...] = jnp.zeros_like(dkp)
        dvp[...] = jnp.zeros_like(dvp)
        lo = lax.broadcasted_iota(jnp.int32, (BLK, 128), 1) < 64
        lane8 = lax.broadcasted_iota(jnp.int32, (8, 128), 1)
        band, cur, bias, sink, head = _swa_consts(sink_ref, l)

        def blk(i, dsink):
            st = pl.multiple_of(i * BLK, BLK)
            kk = kp[pl.ds(st, 2 * BLK), :]
            vv = vp[pl.ds(st, 2 * BLK), :]
            qm, dom = _swa_stack(q_ref, st, lo), _swa_stack(do_ref, st, lo)
            p, psink = _swa_scores(qm, kk, band & (cur | (i > 0)), bias, sink)
            dp = _dot_nt(dom, vv)
            dvp[pl.ds(st, 2 * BLK), :] += _dot_tn(p.astype(BF16), dom)
            delta = jnp.sum(p * dp, axis=-1, keepdims=True)
            dsk = -psink * delta
            for h in range(HEADS):
                dsink = dsink + jnp.where(lane8 == h, jnp.sum(jnp.where(head == h, dsk, 0.0)), 0.0)
            dsc = (p * (dp - delta) * SWA_SCALE).astype(BF16)
            for b, dqb in enumerate(_swa_unstack(_dot(dsc, kk), lo)):
                dq_ref[pl.ds(st, BLK), b * 128:(b + 1) * 128] = dqb.astype(BF16)
            dkp[pl.ds(st, 2 * BLK), :] += _dot_tn(dsc, qm)
            return dsink

        dsink_ref[...] = lax.fori_loop(0, nb, blk, jnp.zeros((8, 128), F32), unroll=2)
        dk_ref[...] = dkp[BLK:, :].astype(BF16)
        dv_ref[...] = dvp[BLK:, :].astype(BF16)
        if rider:
            rider.wait(*r_io)

    in_specs = [BS((S, 256), lambda i: (0, C_QS // 256)), BS((S, 128), lambda i: (0, C_KS // 128)),
                BS((S, 128), lambda i: (0, C_VS // 128)), BS(memory_space=pltpu.SMEM), BS((S, 256), lambda i: (0, 3))]
    out_specs = [BS((S, 256), lambda i: (0, 0)), BS((S, 128), lambda i: (0, 0)), BS((S, 128), lambda i: (0, 0)),
                 BS((8, 128), lambda i: (0, 0))]
    out_shape = [SDS((S, 256), BF16), SDS((S, 128), BF16), SDS((S, 128), BF16), SDS((8, 128), F32)]
    scratch = [pltpu.VMEM((S + BLK, 128), BF16), pltpu.VMEM((S + BLK, 128), BF16),
               pltpu.VMEM((S + BLK, 128), F32), pltpu.VMEM((S + BLK, 128), F32)]
    if not rider:
        return _pc(body, name=name, grid=(1,), in_specs=in_specs, out_specs=out_specs, out_shape=out_shape,
                   scratch=scratch)(proj, proj, proj, sinks, dyd)
    out = pl.pallas_call(body, name=name, grid=(1,), in_specs=in_specs + [ANY] * nr, out_specs=out_specs + [ANY] * nr,
                         out_shape=out_shape + rider.out_shape, scratch_shapes=scratch + rider.scratch(),
                         compiler_params=pltpu.CompilerParams(dimension_semantics=("arbitrary",),
                                                              vmem_limit_bytes=VMEM_LIMIT, has_side_effects=True))(
        proj, proj, proj, sinks, dyd, *rider.arrs)
    return (*out[:4], list(out[4:]))


def _down(x, k, t):
    return jnp.where(t >= k, pltpu.roll(x, k, axis=0), 0.0)


def _up(x, k, t):
    n = x.shape[0]
    return jnp.where(t < n - k, pltpu.roll(x, n - k, axis=0), 0.0)


def _conv(proj, w8, l, name):
    S = proj.shape[0]

    def body(gb_ref, gc_ref, u_ref, w_ref, y_ref):
        t = lax.broadcasted_iota(jnp.int32, (S, 128), 0)
        z = gc_ref[...] * u_ref[...]
        c = w_ref[2:3, :] * z + w_ref[1:2, :] * _down(z, 1, t) + w_ref[0:1, :] * _down(z, 2, t)
        y_ref[...] = gb_ref[...] * c

    col = lambda c0: BS((S, 128), lambda i: (0, c0 // 128 + i))
    return _pc(body, name=name, grid=(2,),
               in_specs=[col(C_GB), col(C_GC), col(C_UC), BS((None, 8, 128), lambda i: (l, 0, i))],
               out_specs=BS((S, 128), lambda i: (0, i)), out_shape=SDS((S, 256), F32))(proj, proj, proj, w8)


def _conv_bwd(proj, w8, dycat, l, name):
    S = proj.shape[0]

    def body(gb_ref, gc_ref, u_ref, w_ref, dy_ref, dgb_ref, dgc_ref, du_ref, dw_ref):
        t = lax.broadcasted_iota(jnp.int32, (S, 128), 0)
        gc, u = gc_ref[...], u_ref[...]
        z = gc * u
        z1, z2 = _down(z, 1, t), _down(z, 2, t)
        w0, w1, w2 = w_ref[0:1, :], w_ref[1:2, :], w_ref[2:3, :]
        dy = dy_ref[...]
        dgb_ref[...] = (dy * (w2 * z + w1 * z1 + w0 * z2)).astype(BF16)
        dc = dy * gb_ref[...]
        dz = w2 * dc + w1 * _up(dc, 1, t) + w0 * _up(dc, 2, t)
        dgc_ref[...] = (dz * u).astype(BF16)
        du_ref[...] = (dz * gc).astype(BF16)
        row = lax.broadcasted_iota(jnp.int32, (8, 128), 0)
        sums = [jnp.sum(dc * zz, axis=0, keepdims=True) for zz in (z2, z1, z)]
        dw_ref[...] = jnp.where(row == 0, sums[0], jnp.where(row == 1, sums[1], jnp.where(row == 2, sums[2], 0.0)))

    col = lambda c0: BS((S, 128), lambda i: (0, c0 // 128 + i))
    out = BS((S, 128), lambda i: (0, i))
    return _pc(body, name=name, grid=(2,),
               in_specs=[col(C_GB), col(C_GC), col(C_UC), BS((None, 8, 128), lambda i: (l, 0, i)), col(256)],
               out_specs=[out, out, out, BS((8, 128), lambda i: (0, i))],
               out_shape=[SDS((S, 256), BF16)] * 3 + [SDS((8, 256), F32)])(proj, proj, proj, w8, dycat)


def _pool_parts(u, t, first):
    lo = lax.broadcasted_iota(jnp.int32, u.shape, 1) < 64
    s2 = u + _down(u, 1, t)
    s4 = s2 + _down(s2, 2, t)
    s8 = s4 + _down(s4, 4, t)
    s16 = s8 + _down(s8, 8, t)
    win = jnp.where(lo, jnp.where(first, s2, s8), jnp.where(first, s4, s16))
    wv = jnp.where(lo, jnp.where(first, 2, 8), jnp.where(first, 4, 16))
    cnt = jnp.minimum(t + 1, wv).astype(F32)
    return win, cnt, lo


def _pool(proj, pwd, scale3, l, name):
    S = proj.shape[0]

    def body(u_ref, pw_ref, sc_ref, y_ref):
        t = lax.broadcasted_iota(jnp.int32, (S, 128), 0)
        u = u_ref[...]
        win, cnt, _ = _pool_parts(u, t, pl.program_id(0) == 0)
        pooled = win / cnt - u
        y_ref[...] = _dot(pooled.astype(BF16), pw_ref[...]) * sc_ref[...]

    return _pc(body, name=name, grid=(2,),
               in_specs=[BS((S, 128), lambda i: (0, C_UP // 128 + i)), BS((None, 128, 128), lambda i: (l, i, 0)),
                         BS((None, 1, 128), lambda i: (l, 0, i))],
               out_specs=BS((S, 128), lambda i: (0, i)), out_shape=SDS((S, 256), F32))(proj, pwd, scale3)


def _pool_bwd(proj, pwd, scale3, dycat, l, name):
    S = proj.shape[0]

    def body(u_ref, pw_ref, sc_ref, dy_ref, du_ref, dpw_ref, dsc_ref):
        t = lax.broadcasted_iota(jnp.int32, (S, 128), 0)
        first = pl.program_id(0) == 0
        u = u_ref[...]
        win, cnt, lo = _pool_parts(u, t, first)
        pooled = (win / cnt - u).astype(BF16)
        pw = pw_ref[...]
        dy = dy_ref[...]
        dsc_ref[...] = jnp.broadcast_to(jnp.sum(dy * _dot(pooled, pw), axis=0, keepdims=True), (8, 128))
        dmb = (dy * sc_ref[...]).astype(BF16)
        dpw_ref[...] = _dot_tn(pooled, dmb)
        dpooled = _dot_nt(dmb, pw)
        a1 = dpooled / cnt
        a2 = a1 + _up(a1, 1, t)
        a4 = a2 + _up(a2, 2, t)
        a8 = a4 + _up(a4, 4, t)
        a16 = a8 + _up(a8, 8, t)
        dwin = jnp.where(lo, jnp.where(first, a2, a8), jnp.where(first, a4, a16))
        du_ref[...] = (dwin - dpooled).astype(BF16)

    return _pc(body, name=name, grid=(2,),
               in_specs=[BS((S, 128), lambda i: (0, C_UP // 128 + i)), BS((None, 128, 128), lambda i: (l, i, 0)),
                         BS((None, 1, 128), lambda i: (l, 0, i)), BS((S, 128), lambda i: (0, 4 + i))],
               out_specs=[BS((S, 128), lambda i: (0, i)), BS((128, 128), lambda i: (i, 0)), BS((8, 128), lambda i: (0, i))],
               out_shape=[SDS((S, 256), BF16), SDS((256, 128), F32), SDS((8, 256), F32)])(proj, pwd, scale3, dycat)


def _adamw(w, g, m, v, name, echo=False):
    n, a, b = w.shape
    tr = _row_tile(a, b)

    def body(w_ref, g_ref, m_ref, v_ref, d_ref, nm_ref, nv_ref, *g_out):
        gv = g_ref[...]
        if echo:
            g_out[0][...] = gv
        m_new = B1 * m_ref[...] + (1.0 - B1) * gv
        v_new = B2 * v_ref[...] + (1.0 - B2) * (gv * gv)
        m_hat = m_new / (1.0 - B1 ** STEP)
        v_hat = v_new / (1.0 - B2 ** STEP)
        d_ref[...] = -LR * (m_hat / (jnp.sqrt(v_hat) + ADAM_EPS) + WD * w_ref[...])
        nm_ref[...] = m_new
        nv_ref[...] = v_new

    sp = BS((None, tr, b), lambda i, t: (i, t, 0))
    return _pc(body, name=name, grid=(n, a // tr), in_specs=[sp] * 4, out_specs=[sp] * (3 + echo),
               out_shape=[SDS((n, a, b), F32)] * (3 + echo))(w, g, m, v)


def _prefetch_call(body, name, grid, in_specs, out_specs, out_shape):
    gs = pltpu.PrefetchScalarGridSpec(num_scalar_prefetch=1, grid=grid, in_specs=in_specs, out_specs=out_specs)
    return pl.pallas_call(body, name=name, grid_spec=gs, out_shape=out_shape, compiler_params=_params(len(grid)))


def _place(w, kc, dtype, name):
    _, a, b = w.shape

    def body(kc_ref, w_ref, o_ref):
        o_ref[...] = w_ref[...].astype(dtype)

    return _prefetch_call(body, name, (2,), [BS((None, a, b), lambda l, kc: (l, 0, 0))],
                          BS((None, None, a, b), lambda l, kc: (l, kc[0], 0, 0)), SDS((2, 4, a, b), dtype))(kc, w)


def _pair_sum(g, got, kc, name):
    _, _, a, b = g.shape
    tr = _row_tile(a, b)

    def body(kc_ref, a_ref, b_ref, t32_ref, t16_ref):
        s = a_ref[...] + b_ref[...]
        t16_ref[...] = s.astype(BF16)

        @pl.when(pl.program_id(1) == kc_ref[0])
        def _():
            t32_ref[...] = s

    sp = BS((None, tr, b), lambda t, k, kc: (k, t, 0))
    return _prefetch_call(body, name, (a // tr, 4),
                          [BS((None, None, tr, b), lambda t, k, kc: (kc[1], k, t, 0)), sp],
                          [BS((tr, b), lambda t, k, kc: (t, 0)), sp],
                          [SDS((a, b), F32), SDS((4, a, b), BF16)])(kc, g, got)


def _chip_sum(t32, got3, kc, name):
    a, b = t32.shape
    tr = _row_tile(a, b)

    def body(kc_ref, a_ref, b_ref, u_ref):
        u_ref[...] = ((a_ref[...] + b_ref[0].astype(F32)) + b_ref[1].astype(F32)) + b_ref[2].astype(F32)

    return _prefetch_call(body, name, (a // tr,),
                          [BS((tr, b), lambda t, kc: (t, 0)), BS((3, tr, b), lambda t, kc: (0, t, 0))],
                          BS((None, tr, b), lambda t, kc: (kc[1], t, 0)), SDS((2, a, b), F32))(kc, t32, got3)


def _me():
    return lax.axis_index("x"), lax.axis_index("y"), lax.axis_index("c")


def _other_chips(x, y):
    return [(1 - x, y), (x, 1 - y), (1 - x, 1 - y)]


ANY = BS(memory_space=pl.ANY)
COMM_PARAMS = pltpu.CompilerParams(has_side_effects=True)


def _gather(arrs, name):
    n = len(arrs)

    def body(*refs):
        for phase in _gather_phases(refs[n:2 * n], [a.shape for a in arrs], refs[2 * n], refs[2 * n + 1]):
            phase()

    return pl.pallas_call(body, name=name, out_shape=[SDS(a.shape, a.dtype) for a in arrs],
                          in_specs=[ANY] * n, out_specs=[ANY] * n, input_output_aliases={t: t for t in range(n)},
                          scratch_shapes=[pltpu.SemaphoreType.DMA((7 * n,)), pltpu.SemaphoreType.DMA((7 * n,))],
                          compiler_params=COMM_PARAMS)(*arrs)


def _gather_phases(outs, shapes, send_sems, recv_sems, layer=None):
    n = len(outs)
    cut = [s[2] // 2 // 16 * 16 for s in shapes]
    split = [r > 0 for r in cut]

    def plan():
        x, y, c = _me()
        return (c if layer is None else layer), (x, y), (x, y, c), (x, y, 1 - c), _other_chips(x, y)

    def role(moving, fn):
        if layer is None:
            fn()
        else:
            c = lax.axis_index("c")
            pl.when((c == layer) if moving else (c != layer))(fn)

    def blk(t, chip, layer, half=None):
        r = outs[t].at[layer, 2 * chip[0] + chip[1]]
        if half is None:
            return r
        return r.at[pl.ds(0, cut[t])] if half == 0 else r.at[pl.ds(cut[t], shapes[t][2] - cut[t])]

    def copy(t, k, ref, to):
        return pltpu.make_async_remote_copy(src_ref=ref, dst_ref=ref, send_sem=send_sems.at[7 * t + k],
                                            recv_sem=recv_sems.at[7 * t + k], device_id=to, device_id_type=MESH)

    def own_sends(t):
        c, chip, me, sib, (xn, yn, dg) = plan()
        cps = [copy(t, 0, blk(t, chip, c), (*xn, c)), copy(t, 1, blk(t, chip, c), (*yn, c))]
        return cps if split[t] else cps + [copy(t, 2, blk(t, chip, c), (*dg, c))]

    def relays(t):
        c, chip, me, sib, (xn, yn, dg) = plan()
        after_x = [copy(t, 4, blk(t, xn, c), sib)]
        after_y = [copy(t, 5, blk(t, yn, c), sib)]
        if split[t]:
            after_x.insert(0, copy(t, 2, blk(t, xn, c, 0), (*yn, c)))
            after_y.insert(0, copy(t, 3, blk(t, yn, c, 1), (*xn, c)))
        return after_x, after_y, [copy(t, 6, blk(t, dg, c), sib)]

    def send_own():
        for t in range(n):
            for cp in own_sends(t):
                cp.start()

    def relay_neighbours():
        c, chip, me, sib, (xn, yn, dg) = plan()
        for t in range(n):
            after_x, after_y, _ = relays(t)
            copy(t, 0, blk(t, xn, c), me).wait_recv()
            for cp in after_x:
                cp.start()
            copy(t, 1, blk(t, yn, c), me).wait_recv()
            for cp in after_y:
                cp.start()

    def relay_diagonal():
        c, chip, me, sib, (xn, yn, dg) = plan()
        for t in range(n):
            if split[t]:
                copy(t, 2, blk(t, dg, c, 0), me).wait_recv()
                copy(t, 3, blk(t, dg, c, 1), me).wait_recv()
            else:
                copy(t, 2, blk(t, dg, c), me).wait_recv()
            relays(t)[2][0].start()

    def take_sibling():
        _, chip, me, sib, (xn, yn, dg) = plan()
        theirs = 1 - lax.axis_index("c") if layer is None else layer
        for t in range(n):
            for k, peer in ((4, xn), (5, yn), (6, dg)):
                copy(t, k, blk(t, peer, theirs), me).wait_recv()

    def drain_sends():
        for t in range(n):
            after_x, after_y, after_d = relays(t)
            for cp in own_sends(t) + after_x + after_y + after_d:
                cp.wait_send()

    def phase3():
        role(True, relay_diagonal)
        role(False, take_sibling)
        role(True, drain_sends)

    return (lambda: role(True, send_own)), (lambda: role(True, relay_neighbours)), phase3


def _swap_copies(ins, outs, send_sems, recv_sems):
    x, y, c = _me()
    return [pltpu.make_async_remote_copy(src_ref=ins[t].at[1 - c], dst_ref=outs[t], send_sem=send_sems.at[t],
                                         recv_sem=recv_sems.at[t], device_id=(x, y, 1 - c), device_id_type=MESH)
            for t in range(len(ins))]


def _exchange_copies(ins, outs, send_sems, recv_sems):
    x, y, c = _me()
    return [pltpu.make_async_remote_copy(src_ref=ins[t].at[2 * cx + cy], dst_ref=outs[t].at[j],
                                         send_sem=send_sems.at[3 * t + j], recv_sem=recv_sems.at[3 * t + j],
                                         device_id=(cx, cy, c), device_id_type=MESH)
            for j, (cx, cy) in enumerate(_other_chips(x, y)) for t in range(len(ins))]


class _Rider:
    def __init__(self, arrs, out_shape, nsem, copies):
        self.arrs, self.out_shape, self.nsem, self.copies = list(arrs), out_shape, nsem, copies
        self.n = len(self.arrs)

    def scratch(self):
        return [pltpu.SemaphoreType.DMA((self.nsem,)), pltpu.SemaphoreType.DMA((self.nsem,))]

    def start(self, ins, outs, sems):
        for cp in self.copies(ins, outs, *sems):
            cp.start()

    def wait(self, ins, outs, sems):
        for cp in self.copies(ins, outs, *sems):
            cp.wait()


def _swap_rider(gs):
    return _Rider(gs, [SDS(g.shape[1:], g.dtype) for g in gs], len(gs), _swap_copies)


def _exchange_rider(ts):
    return _Rider(ts, [SDS((3,) + t.shape[1:], t.dtype) for t in ts], 3 * len(ts), _exchange_copies)


def _ride_alone(rider, name):
    n = rider.n

    def body(*refs):
        rider.start(refs[:n], refs[n:2 * n], refs[2 * n:])
        rider.wait(refs[:n], refs[n:2 * n], refs[2 * n:])

    return pl.pallas_call(body, name=name, out_shape=rider.out_shape, in_specs=[ANY] * n, out_specs=[ANY] * n,
                          scratch_shapes=rider.scratch(), compiler_params=COMM_PARAMS)(*rider.arrs)


def _join_layers(us, name):
    n = len(us)

    def body(*refs):
        outs, send_sems, recv_sems = refs[n:2 * n], refs[2 * n], refs[2 * n + 1]
        x, y, c = _me()
        cps = [pltpu.make_async_remote_copy(src_ref=outs[t].at[c], dst_ref=outs[t].at[c], send_sem=send_sems.at[t],
                                            recv_sem=recv_sems.at[t], device_id=(x, y, 1 - c), device_id_type=MESH)
               for t in range(n)]
        for cp in cps:
            cp.start()
        for cp in cps:
            cp.wait()

    return pl.pallas_call(body, name=name, out_shape=[SDS(u.shape, u.dtype) for u in us],
                          in_specs=[ANY] * n, out_specs=[ANY] * n, input_output_aliases={t: t for t in range(n)},
                          scratch_shapes=[pltpu.SemaphoreType.DMA((n,)), pltpu.SemaphoreType.DMA((n,))],
                          compiler_params=COMM_PARAMS)(*us)


def _allsum_small(v, name, rider=None):
    M = v.shape[0]
    nr = rider.n if rider else 0

    def body(x_ref, *rest):
        o_ref = rest[nr]
        all_ref, send_sems, recv_sems, local_sem = rest[2 * nr + 1:2 * nr + 5]
        r_io = (rest[:nr], rest[nr + 1:2 * nr + 1], rest[2 * nr + 5:])
        x, y, c = _me()
        me, sib = (x, y, c), (x, y, 1 - c)
        chips = _other_chips(x, y)

        def rows(px, py, pc):
            return all_ref.at[pl.ds((4 * px + 2 * py + pc) * M, M), :]

        def copy(k, block, to, src=None):
            return pltpu.make_async_remote_copy(src_ref=rows(*block) if src is None else src, dst_ref=rows(*block),
                                                send_sem=send_sems.at[k], recv_sem=recv_sems.at[k],
                                                device_id=to, device_id_type=MESH)

        mine = pltpu.make_async_copy(x_ref, rows(*me), local_sem)
        mine.start()
        first = [copy(0, me, sib, src=x_ref)]
        first += [copy(1 + j, me, (*chip, c), src=x_ref) for j, chip in enumerate(chips)]
        for cp in first:
            cp.start()
        if rider:
            rider.start(*r_io)
        passed = [copy(4 + j, (*chip, c), sib) for j, chip in enumerate(chips)]
        for j, chip in enumerate(chips):
            copy(1 + j, (*chip, c), me).wait_recv()
            passed[j].start()
        copy(0, sib, me).wait_recv()
        for j, chip in enumerate(chips):
            copy(4 + j, (*chip, 1 - c), me).wait_recv()
        for cp in first + passed:
            cp.wait_send()
        mine.wait()
        acc = all_ref[0:M, :]
        for d in range(1, 8):
            acc = acc + all_ref[d * M:(d + 1) * M, :]
        o_ref[...] = acc
        if rider:
            rider.wait(*r_io)

    vm = BS(memory_space=pltpu.VMEM)
    out = pl.pallas_call(body, name=name, out_shape=[SDS((M, LANES), F32)] + (rider.out_shape if rider else []),
                         in_specs=[vm] + [ANY] * nr, out_specs=[vm] + [ANY] * nr,
                         scratch_shapes=[pltpu.VMEM((8 * M, LANES), F32), pltpu.SemaphoreType.DMA((7,)),
                                         pltpu.SemaphoreType.DMA((7,)), pltpu.SemaphoreType.DMA]
                         + (rider.scratch() if rider else []),
                         compiler_params=pltpu.CompilerParams(has_side_effects=True, vmem_limit_bytes=VMEM_LIMIT))(
        v, *(rider.arrs if rider else []))
    return out[0], list(out[1:])


FFN = ("w_gate_up", "w_down")
REST = ("w_in", "w_o", "w_uq", "w_ukv")
BIG = FFN + REST
TINY = ("conv_w",)
REPL = ("attn_norm", "mla_q_norm", "mla_kv_norm", "pool_w", "pool_scale", "swa_sinks", "mix_norm", "ffn_norm",
        "final_norm")
ORDER = ("attn_norm", "w_in", "mla_q_norm", "w_uq", "mla_kv_norm", "w_ukv", "conv_w", "pool_w", "pool_scale",
         "swa_sinks", "mix_norm", "w_o", "ffn_norm", "w_gate_up", "w_down", "final_norm")


def _rows8(shape):
    return -(-int(np.prod(shape)) // (8 * LANES)) * 8


def _pack(arrs):
    parts = []
    for a in arrs:
        r = _rows8(a.shape)
        parts.append(jnp.pad(a.reshape(-1), (0, r * LANES - a.size)).reshape(r, LANES))
    return jnp.concatenate(parts, axis=0)


def _unpack(buf, shapes):
    out, r0 = [], 0
    for s in shapes:
        n, r = int(np.prod(s)), _rows8(s)
        rows = buf[r0:r0 + r]
        out.append(rows.reshape(s) if n == r * LANES else rows.reshape(-1)[:n].reshape(s))
        r0 += r
    return out


def _cols_joined(g):
    return jnp.transpose(g, (0, 2, 1, 3)).reshape(g.shape[0], g.shape[2], 4 * g.shape[3])


def _cols_split(w):
    n, a, b4 = w.shape
    return jnp.transpose(w.reshape(n, a, 4, b4 // 4), (0, 2, 1, 3))


def _rope_tables(S):
    inv = 1.0 / (10000.0 ** (jnp.arange(0, 32, 2, dtype=F32) / 32))
    ang = jnp.arange(S, dtype=F32)[:, None] * inv[None, :]
    cos, sin = jnp.cos(ang), jnp.sin(ang)
    z = lambda w: jnp.zeros((S, w), F32)
    tc = jnp.concatenate([jnp.ones((S, 64), F32), cos, cos, jnp.ones((S, 32), F32)], axis=1)
    ts1 = jnp.concatenate([z(64), -sin, z(48)], axis=1)
    ts2 = jnp.concatenate([z(80), sin, z(32)], axis=1)
    return tc, ts1, ts2


def _pad_w_in(wt):
    z = lambda n: jnp.zeros((wt.shape[0], n, wt.shape[2]), wt.dtype)
    return jnp.concatenate([wt[:, 0:384], z(64), wt[:, 384:416], z(32), wt[:, 416:1952]], axis=1)


def _unpad_w_in(dt):
    return jnp.concatenate([dt[:, 0:384], dt[:, 448:480], dt[:, 512:2048]], axis=1)


def _pad_heads(w, src, offs):
    cols = []
    for h in range(HEADS):
        src0, n = src[h]
        z = lambda k: jnp.zeros(w.shape[:-1] + (k,), w.dtype)
        cols += [z(offs[h]), w[..., src0:src0 + n], z(128 - offs[h] - n)]
    return jnp.concatenate(cols, axis=-1)


UQ_SRC = [(h * 96, 96) for h in range(HEADS)]
KN_SRC = [(h * 128, 64) for h in range(HEADS)]
V_SRC = [(h * 128 + 64, 64) for h in range(HEADS)]
ZERO_OFF = [0] * HEADS
V_OFF = [(h % 2) * 64 for h in range(HEADS)]


def _unpad_heads(d, src, offs):
    return [d[..., h * 128 + offs[h]: h * 128 + offs[h] + src[h][1]] for h in range(HEADS)]


def kernel(x, attn_norm, w_in, mla_q_norm, w_uq, mla_kv_norm, w_ukv, conv_w, pool_w, pool_scale, swa_sinks, mix_norm, w_o, ffn_norm, w_gate_up, w_down, final_norm, loss_target, m_attn_norm, m_w_in, m_mla_q_norm, m_w_uq, m_mla_kv_norm, m_w_ukv, m_conv_w, m_pool_w, m_pool_scale, m_swa_sinks, m_mix_norm, m_w_o, m_ffn_norm, m_w_gate_up, m_w_down, m_final_norm, v_attn_norm, v_w_in, v_mla_q_norm, v_w_uq, v_mla_kv_norm, v_w_ukv, v_conv_w, v_pool_w, v_pool_scale, v_swa_sinks, v_mix_norm, v_w_o, v_ffn_norm, v_w_gate_up, v_w_down, v_final_norm):
    W = dict(attn_norm=attn_norm, w_in=w_in, mla_q_norm=mla_q_norm, w_uq=w_uq, mla_kv_norm=mla_kv_norm, w_ukv=w_ukv,
             conv_w=conv_w, pool_w=pool_w, pool_scale=pool_scale, swa_sinks=swa_sinks, mix_norm=mix_norm, w_o=w_o,
             ffn_norm=ffn_norm, w_gate_up=w_gate_up, w_down=w_down, final_norm=final_norm)
    M1 = dict(attn_norm=m_attn_norm, w_in=m_w_in, mla_q_norm=m_mla_q_norm, w_uq=m_w_uq, mla_kv_norm=m_mla_kv_norm,
              w_ukv=m_w_ukv, conv_w=m_conv_w, pool_w=m_pool_w, pool_scale=m_pool_scale, swa_sinks=m_swa_sinks,
              mix_norm=m_mix_norm, w_o=m_w_o, ffn_norm=m_ffn_norm, w_gate_up=m_w_gate_up, w_down=m_w_down,
              final_norm=m_final_norm)
    V2 = dict(attn_norm=v_attn_norm, w_in=v_w_in, mla_q_norm=v_mla_q_norm, w_uq=v_w_uq, mla_kv_norm=v_mla_kv_norm,
              w_ukv=v_w_ukv, conv_w=v_conv_w, pool_w=v_pool_w, pool_scale=v_pool_scale, swa_sinks=v_swa_sinks,
              mix_norm=v_mix_norm, w_o=v_w_o, ffn_norm=v_ffn_norm, w_gate_up=v_w_gate_up, w_down=v_w_down,
              final_norm=v_final_norm)
    S = x.shape[1]
    xc, yc, cc = _me()
    chip = 2 * xc + yc
    kc = jnp.stack([chip, cc]).astype(jnp.int32)

    first, later = ("w_in", "w_uq", "w_ukv", "conv_w"), ("w_o", "w_gate_up", "w_down")
    T = lambda a: jnp.swapaxes(a, 1, 2)
    W["w_in"], M1["w_in"], V2["w_in"] = T(w_in), T(m_w_in), T(v_w_in)
    placed = {n: _place(W[n], kc, F32 if n == "conv_w" else BF16, f"place_{n}") for n in first + later}
    gi, gq, gkv, gcv = _gather([placed[n] for n in first], "gather_weights")
    later_w = [placed[n] for n in later]
    win_p = _pad_w_in(gi.reshape(2, 4 * gi.shape[2], D))
    wuq_p = _pad_heads(_cols_joined(gq), UQ_SRC, ZERO_OFF)
    wukv = _cols_joined(gkv)
    wk_p = _pad_heads(wukv, KN_SRC, ZERO_OFF)
    wv_p = _pad_heads(wukv, V_SRC, V_OFF)
    conv8 = jnp.pad(_cols_joined(gcv), ((0, 0), (0, 5), (0, 0)))
    pwd = jnp.concatenate([jnp.concatenate(
        [jnp.pad(pool_w[:, 2 * b], ((0, 0), (0, 0), (0, 64))), jnp.pad(pool_w[:, 2 * b + 1], ((0, 0), (0, 0), (64, 0)))],
        axis=1) for b in range(2)], axis=1).astype(BF16)
    tabs = _rope_tables(S)
    g_attn, g_q, g_kv, g_mix, g_ffn, g_ps = (_g3(W[n]) for n in ("attn_norm", "mla_q_norm", "mla_kv_norm", "mix_norm",
                                                                  "ffn_norm", "pool_scale"))

    xs = [x[0]]
    saved = []
    for l in range(DEPTH):
        x0 = xs[-1]
        proj, h = _norm_mm(x0, g_attn, l, win_p, _wspec_in(l), D_INP, D_INP, F32, f"in_proj{l}", w_t=True)
        q, k, v, kt, vt = _mla_prep(proj, g_q, g_kv, wuq_p, wk_p, wv_p, tabs, l, f"mla_prep{l}")
        ya, lse, later_w = _mla_attn(q, k, vt, later_w, l, f"mla_attn{l}")
        go, gu4, gd = later_w
        wo, wdown = go.reshape(2, D, D), gd.reshape(2, D_FF, D)
        yb = _conv(proj, conv8, l, f"conv{l}")
        ycp = _pool(proj, pwd, g_ps, l, f"pool{l}")
        yd = _swa(proj, swa_sinks, l, f"swa{l}")
        x1, ycat, mixed = _mix_out(x0, ya, yb, ycp, yd, g_mix, wo, l, f"mix_out{l}")
        gu, h2 = _norm_mm(x1, g_ffn, l, gu4, _wspec_gu(l), 2 * D_FF, 2 * D_FF // 4, BF16, f"gate_up{l}")
        x2, act = _swiglu_mm_res(x1, gu, wdown, l, f"down{l}")
        saved.append(dict(x0=x0, proj=proj, h=h, q=q, k=k, kt=kt, v=v, lse=lse, x1=x1, ycat=ycat, mixed=mixed,
                          gu=gu, h2=h2, act=act))
        xs.append(x2)

    dx, dx16, dg_final, loss_tile = _loss_head(xs[-1], final_norm.reshape(1, D), loss_target[0], "loss_head")
    loss_here = (loss_tile[0, 0] * (0.5 / D)).reshape(1)

    G = {n: [None] * DEPTH for n in ("w_uq", "w_ukv") + TINY + REPL if n != "final_norm"}
    gw_in = gw_o = gw_gu = gw_down = None
    for l in reversed(range(DEPTH)):
        sv = saved[l]
        dgu = _bwd_down(dx16, wdown, sv["gu"], l, f"down_bwd{l}")
        gw_down = _mm_tn(sv["act"], dx16, l, gw_down, f"dw_down{l}")
        gw_gu = _mm_tn(sv["h2"], dgu, l, gw_gu, f"dw_gate_up{l}", split4=True)
        exchange_gu = exchange_down = None
        if l == 0:
            g_f = [gw_gu, gw_down.reshape(2, 4, D_FF // 4, D)]
            dx1, dx1_16, dg, got_f = _mm_nt_normbwd(dgu, gu4, l, sv["x1"], g_ffn, dx, 1, f"gate_up_bwd{l}",
                                                    rider=_swap_rider(g_f))
            pairs_f = [_pair_sum(g, o, kc, f"rs_pair_sum_{n}") for g, o, n in zip(g_f, got_f, FFN)]
            exchange_gu, exchange_down = _exchange_rider([pairs_f[0][1]]), _exchange_rider([pairs_f[1][1]])
        else:
            dx1, dx1_16, dg = _mm_nt_normbwd(dgu, gu4, l, sv["x1"], g_ffn, dx, 1, f"gate_up_bwd{l}")
        G["ffn_norm"][l] = dg[0]
        gw_o = _mm_tn(sv["mixed"], dx1_16, l, gw_o, f"dw_o{l}")
        dycat, dg = _mm_nt_normbwd(dx1_16, wo.reshape(2, 1, D, D), l, sv["ycat"], g_mix, None, 4, f"mix_bwd{l}")
        G["mix_norm"][l] = dg[0]

        proj = sv["proj"]
        delta = _mla_delta(dycat, sv["ycat"], f"mla_delta{l}")
        dq, dk, dv, got3_gu = _mla_attn_bwd(sv["q"], sv["k"], sv["kt"], sv["v"], dycat, sv["lse"], delta, exchange_gu,
                                            f"mla_attn_bwd{l}")
        dcq, dckv, dkr, dwuq, dwk, dwv, dgq, dgkv = _mla_prep_bwd(
            dq, dk, dv, proj, g_q, g_kv, wuq_p, wk_p, wv_p, tabs, l, f"mla_prep_bwd{l}")
        dgb, dgc, duc, dcw = _conv_bwd(proj, conv8, dycat, l, f"conv_bwd{l}")
        dup, dpw, dps = _pool_bwd(proj, pwd, g_ps, dycat, l, f"pool_bwd{l}")
        if l == 0:
            dqs, dks, dvs, dsink, got3_down = _swa_bwd(proj, swa_sinks, dycat, l, f"swa_bwd{l}", rider=exchange_down)
            got3_f = got3_gu + got3_down
        else:
            dqs, dks, dvs, dsink = _swa_bwd(proj, swa_sinks, dycat, l, f"swa_bwd{l}")
        dproj = jnp.concatenate([dcq, dckv, dkr, dgb, dgc, duc, dup, dqs, dks, dvs], axis=1)
        gw_in = _mm_tn(dproj, sv["h"], l, gw_in, f"dw_in{l}")
        G["w_uq"][l] = jnp.concatenate(_unpad_heads(dwuq, UQ_SRC, ZERO_OFF), axis=1)
        kn, vv = _unpad_heads(dwk, KN_SRC, ZERO_OFF), _unpad_heads(dwv, V_SRC, V_OFF)
        G["w_ukv"][l] = jnp.concatenate([t for h in range(HEADS) for t in (kn[h], vv[h])], axis=1)
        dx, dx16, dg = _mm_nt_normbwd(dproj, win_p.reshape(2, 1, D_INP, D), l, sv["x0"], g_attn, dx1, 1, f"in_proj_bwd{l}",
                                      w_t=True)
        G["attn_norm"][l] = dg[0]
        G["mla_q_norm"][l] = dgq[0]
        G["mla_kv_norm"][l] = dgkv[0]
        G["conv_w"][l] = dcw[0:3]
        G["pool_w"][l] = jnp.stack([dpw[0:64, 0:64], dpw[64:128, 64:128], dpw[128:192, 0:64], dpw[192:256, 64:128]])
        G["pool_scale"][l] = dps[0]
        G["swa_sinks"][l] = dsink[0, 0:4]
    grad_x = dx[None]
    Gl = {n: jnp.stack(G[n]) for n in TINY + REPL if n != "final_norm"}
    Gl["final_norm"] = dg_final[0]

    us_f = [_chip_sum(p[0], o3, kc, f"rs_chip_sum_{n}") for p, o3, n in zip(pairs_f, got3_f, FFN)]
    gsum_f = _join_layers(us_f, "rs_join_cores_ffn")
    g_r = [_unpad_w_in(gw_in).reshape(2, 4, -1, D), gw_o.reshape(2, 4, D // 4, D), _cols_split(jnp.stack(G["w_uq"])),
           _cols_split(jnp.stack(G["w_ukv"]))]
    got_r = _ride_alone(_swap_rider(g_r), "rs_swap_cores")
    pairs_r = [_pair_sum(g, o, kc, f"rs_pair_sum_{n}") for g, o, n in zip(g_r, got_r, REST)]
    small = TINY + REPL
    full_shapes = [Gl[n].shape for n in small] + [(1,)]
    summed, got3_r = _allsum_small(_pack([Gl[n] for n in small] + [loss_here]), "allsum_small",
                                   rider=_exchange_rider([p[1] for p in pairs_r]))
    summed = _unpack(summed, full_shapes)
    loss = summed.pop().reshape(())
    us_r = [_chip_sum(p[0], o3, kc, f"rs_chip_sum_{n}") for p, o3, n in zip(pairs_r, got3_r, REST)]
    gsum_r = _join_layers(us_r, "rs_join_cores")
    res = {}
    for n, g in zip(BIG, gsum_f + gsum_r):
        d_, m_, v_, g_ = _adamw(W[n], g, M1[n], V2[n], f"adamw_{n}", echo=True)
        back = T if n == "w_in" else (lambda a: a)
        res["g", n], res["d", n], res["m", n], res["v", n] = back(g_), back(d_), back(m_), back(v_)

    def as3(a):
        if a.ndim <= 2:
            return a.reshape((1,) * (3 - a.ndim) + a.shape)
        return a.reshape(a.shape[0], -1, a.shape[-1])

    for n, g in zip(small, summed):
        if n in TINY:
            wdt = W[n].shape[2]
            g = lax.dynamic_slice_in_dim(g, chip * wdt, wdt, axis=2)
        out = _adamw(as3(W[n]), as3(g), as3(M1[n]), as3(V2[n]), f"adamw_{n}")
        res["g", n] = g
        res["d", n], res["m", n], res["v", n] = (o.reshape(W[n].shape) for o in out)

    return (loss, grad_x, *[res["g", n] for n in ORDER], *[res["d", n] for n in ORDER],
            *[res["m", n] for n in ORDER], *[res["v", n] for n in ORDER])
```

```python
import functools
import math

import numpy as np
import jax
import jax.numpy as jnp
from jax import lax
from jax.experimental import pallas as pl
from jax.experimental.pallas import tpu as pltpu

F32, BF16 = jnp.float32, jnp.bfloat16
SDS = jax.ShapeDtypeStruct
BS = pl.BlockSpec
MESH = pl.DeviceIdType.MESH

D = 1024
DEPTH = 2
HEADS = 4
D_FF = 2816
D_INP = 2048
EPS = 1e-6
SWA_WINDOW = 128
BLK = 128
SLOPES = tuple(2.0 ** (-8.0 * (i + 1) / 4) for i in range(4))
QK_SCALE = 1.0 / math.sqrt(96)
SWA_SCALE = 1.0 / math.sqrt(64)
LR, B1, B2, ADAM_EPS, WD, STEP = 0.001, 0.9, 0.999, 1e-08, 0.01, 10

LANES = 1024
VMEM_LIMIT = 56 * 1024 * 1024
NEG_INF = float("-inf")

C_CQ, C_CKV, C_KR, C_GB, C_GC, C_UC, C_UP, C_QS, C_KS, C_VS = 0, 256, 384, 512, 768, 1024, 1280, 1536, 1792, 1920


def _params(ngrid):
    return pltpu.CompilerParams(dimension_semantics=("arbitrary",) * ngrid, vmem_limit_bytes=VMEM_LIMIT)


def _pc(body, *, name, grid, in_specs, out_specs, out_shape, scratch=(), aliases=None):
    return pl.pallas_call(
        body, name=name, grid=grid, in_specs=in_specs, out_specs=out_specs, out_shape=out_shape,
        scratch_shapes=scratch, input_output_aliases=aliases or {}, compiler_params=_params(len(grid)))


def _dot(a, b):
    return jnp.dot(a, b, preferred_element_type=F32)


def _dot_nt(a, b):
    return lax.dot_general(a, b, (((1,), (1,)), ((), ())), preferred_element_type=F32)


def _dot_tn(a, b):
    return lax.dot_general(a, b, (((0,), (0,)), ((), ())), preferred_element_type=F32)


def _tile(n, cap):
    if n <= cap:
        return n
    t = cap - cap % 128
    while n % t:
        t -= 128
    return t


def _row_tile(a, b, cap=262144):
    bp = -(-b // 128) * 128
    best = None
    for t in range(8, a + 1, 8):
        if a % t == 0 and t * bp <= cap:
            best = t
    if best is None or (best < 64 and a * bp <= 2 * cap):
        return a
    return best


def _g3(a):
    return a.reshape(a.shape[0], 1, a.shape[1])


def _norm_mm(x, g3, l, w, wspec, N, tn, out_dtype, name, w_t=False):
    S, K = x.shape
    tm = min(1024 if out_dtype == BF16 else 512, S)

    def body(x_ref, g_ref, w_ref, y_ref, h_ref):
        @pl.when(pl.program_id(1) == 0)
        def _():
            xv = x_ref[...]
            r = lax.rsqrt(jnp.mean(xv * xv, axis=-1, keepdims=True) + EPS)
            h_ref[...] = (xv * r * g_ref[...]).astype(BF16)

        y_ref[...] = (_dot_nt if w_t else _dot)(h_ref[...], w_ref[...]).astype(out_dtype)

    return _pc(body, name=name, grid=(S // tm, N // tn),
               in_specs=[BS((tm, K), lambda i, j: (i, 0)), BS((None, 1, K), lambda i, j: (l, 0, 0)), wspec],
               out_specs=[BS((tm, tn), lambda i, j: (i, j)), BS((tm, K), lambda i, j: (i, 0))],
               out_shape=[SDS((S, N), out_dtype), SDS((S, K), BF16)])(x, g3, w)


def _wspec_in(l):
    return BS((None, D_INP, D), lambda i, j: (l, j, 0))


def _wspec_gu(l):
    return BS((None, None, D, 2 * D_FF // 4), lambda i, j: (l, j, 0, 0))


def _mix_out(x0, ya, yb, yc, yd, gmix3, wo, l, name):
    S = x0.shape[0]
    tm = min(512, S)

    def body(x_ref, ya_ref, yb_ref, yc_ref, yd_ref, g_ref, w_ref, x1_ref, ycat_ref, mixed_ref):
        groups = [ya_ref[...], yb_ref[...], yc_ref[...], yd_ref[...]]
        for gi, yg in enumerate(groups):
            sl = slice(gi * 256, (gi + 1) * 256)
            r = lax.rsqrt(jnp.mean(yg * yg, axis=-1, keepdims=True) + EPS)
            ycat_ref[:, sl] = yg
            mixed_ref[:, sl] = (yg * r * g_ref[:, sl]).astype(BF16)
        x1_ref[...] = x_ref[...] + _dot(mixed_ref[...], w_ref[...])

    row = lambda w: BS((tm, w), lambda i: (i, 0))
    return _pc(body, name=name, grid=(S // tm,),
               in_specs=[row(D), row(256), row(256), row(256), row(256), BS((None, 1, D), lambda i: (l, 0, 0)),
                         BS((None, D, D), lambda i: (l, 0, 0))],
               out_specs=[row(D), row(D), row(D)],
               out_shape=[SDS((S, D), F32), SDS((S, D), F32), SDS((S, D), BF16)])(x0, ya, yb, yc, yd, gmix3, wo)


def _swiglu_mm_res(x1, gu, wdown, l, name):
    S = x1.shape[0]
    tm = min(256, S)

    def body(x_ref, gate_ref, up_ref, w_ref, x2_ref, act_ref):
        acc = x_ref[...]
        for c0 in range(0, D_FF, D_FF // 2):
            cs = slice(c0, c0 + D_FF // 2)
            gt = gate_ref[:, cs].astype(F32)
            act = (gt * pl.reciprocal(1.0 + jnp.exp(-gt), approx=True) * up_ref[:, cs].astype(F32)).astype(BF16)
            act_ref[:, cs] = act
            acc = acc + _dot(act, w_ref[cs, :])
        x2_ref[...] = acc

    return _pc(body, name=name, grid=(S // tm,),
               in_specs=[BS((tm, D), lambda i: (i, 0)), BS((tm, D_FF), lambda i: (i, 0)),
                         BS((tm, D_FF), lambda i: (i, 1)), BS((None, D_FF, D), lambda i: (l, 0, 0))],
               out_specs=[BS((tm, D), lambda i: (i, 0)), BS((tm, D_FF), lambda i: (i, 0))],
               out_shape=[SDS((S, D), F32), SDS((S, D_FF), BF16)])(x1, gu, gu, wdown)


def _loss_head(x, g, tgt, name):
    S = x.shape[0]
    tm = min(512, S)

    def body(x_ref, g_ref, t_ref, dx_ref, dx16_ref, dg_ref, loss_ref):
        @pl.when(pl.program_id(0) == 0)
        def _():
            dg_ref[...] = jnp.zeros_like(dg_ref)
            loss_ref[...] = jnp.zeros_like(loss_ref)

        xv = x_ref[...]
        r = lax.rsqrt(jnp.mean(xv * xv, axis=-1, keepdims=True) + EPS)
        xh = xv * r
        gv = g_ref[...]
        diff = xh * gv - t_ref[...]
        loss_ref[...] += jnp.sum(diff * diff)
        dy = diff * (1.0 / D)
        dg_ref[...] += jnp.sum(dy * xh, axis=0, keepdims=True)
        dxh = dy * gv
        dx = r * (dxh - xh * jnp.mean(dxh * xh, axis=-1, keepdims=True))
        dx_ref[...] = dx
        dx16_ref[...] = dx.astype(BF16)

    row = BS((tm, D), lambda i: (i, 0))
    return _pc(body, name=name, grid=(S // tm,),
               in_specs=[row, BS((1, D), lambda i: (0, 0)), row],
               out_specs=[row, row, BS((8, D), lambda i: (0, 0)), BS((8, 128), lambda i: (0, 0))],
               out_shape=[SDS((S, D), F32), SDS((S, D), BF16), SDS((8, D), F32), SDS((8, 128), F32)])(x, g, tgt)


def _mm_tn(a, b, l, prev, name, split4=False):
    S, Ka = a.shape
    N = b.shape[1]
    if split4:
        ta, tn = _tile(Ka, 256), N // 4
        out_shape = SDS((2, 4, Ka, tn), F32)
        out_spec = BS((None, None, ta, tn), lambda j, i: (l, j, i, 0))
    else:
        ta, tn = _tile(Ka, 512), _tile(N, 1024)
        out_shape = SDS((2, Ka, N), F32)
        out_spec = BS((None, ta, tn), lambda j, i: (l, i, j))

    def body(a_ref, b_ref, *rest):
        rest[-1][...] = _dot_tn(a_ref[...], b_ref[...])

    in_specs = [BS((S, ta), lambda j, i: (0, i)), BS((S, tn), lambda j, i: (0, j))]
    args = [a, b]
    if prev is not None:
        in_specs.append(BS(memory_space=pl.ANY))
        args.append(prev)
    return _pc(body, name=name, grid=(N // tn, Ka // ta), in_specs=in_specs, out_specs=out_spec, out_shape=out_shape,
               aliases={2: 0} if prev is not None else None)(*args)


def _bwd_down(dx16, wdown, gu, l, name):
    S = dx16.shape[0]
    tm = min(256, S)

    def body(dx_ref, w_ref, gate_ref, up_ref, dgu_ref):
        dxv = dx_ref[...]
        for c0 in range(0, D_FF, 256):
            cs = slice(c0, c0 + 256)
            dact = _dot_nt(dxv, w_ref[cs, :])
            gt = gate_ref[:, cs].astype(F32)
            sg = pl.reciprocal(1.0 + jnp.exp(-gt), approx=True)
            dgu_ref[:, cs] = (dact * up_ref[:, cs].astype(F32) * (sg * (1.0 + gt * (1.0 - sg)))).astype(BF16)
            dgu_ref[:, D_FF + c0:D_FF + c0 + 256] = (dact * (gt * sg)).astype(BF16)

    return _pc(body, name=name, grid=(S // tm,),
               in_specs=[BS((tm, D), lambda i: (i, 0)), BS((None, D_FF, D), lambda i: (l, 0, 0)),
                         BS((tm, D_FF), lambda i: (i, 0)), BS((tm, D_FF), lambda i: (i, 1))],
               out_specs=BS((tm, 2 * D_FF), lambda i: (i, 0)),
               out_shape=SDS((S, 2 * D_FF), BF16))(dx16, wdown, gu, gu)


def _mm_nt_normbwd(dy, w4, l, x, g3, dres, ngroups, name, rider=None, w_t=False):
    S, K = dy.shape
    nk, kc = w4.shape[1], w4.shape[2 if w_t else 3]
    mm = _dot if w_t else _dot_nt
    tm = min(512, S)
    gw = D // ngroups
    has_res = dres is not None
    nr = rider.n if rider else 0
    n_in, n_out = 4 + has_res, 2 + has_res

    def body(*refs):
        dy_ref, w_ref, x_ref, g_ref = refs[:4]
        res_ref = refs[4] if has_res else None
        outs = refs[n_in + nr:n_in + nr + n_out]
        dx_ref, dg_ref = outs[0], outs[-1]
        dx16_ref = outs[1] if has_res else None
        r_io = (refs[n_in:n_in + nr], refs[n_in + nr + n_out:n_in + 2 * nr + n_out], refs[n_in + 2 * nr + n_out:])
        if rider:
            pl.when(pl.program_id(0) == 0)(lambda: rider.start(*r_io))

        @pl.when(pl.program_id(0) == 0)
        def _():
            dg_ref[...] = jnp.zeros_like(dg_ref)

        dh = mm(dy_ref[:, 0:kc], w_ref[0])
        for k in range(1, nk):
            dh = dh + mm(dy_ref[:, k * kc:(k + 1) * kc], w_ref[k])
        for gi in range(ngroups):
            sl = slice(gi * gw, (gi + 1) * gw)
            xg = x_ref[:, sl]
            r = lax.rsqrt(jnp.mean(xg * xg, axis=-1, keepdims=True) + EPS)
            xh = xg * r
            dhg = dh[:, sl]
            dg_ref[:, sl] += jnp.sum(dhg * xh, axis=0, keepdims=True)
            dxh = dhg * g_ref[:, sl]
            dxg = r * (dxh - xh * jnp.mean(dxh * xh, axis=-1, keepdims=True))
            if has_res:
                dxg = dxg + res_ref[:, sl]
                dx16_ref[:, sl] = dxg.astype(BF16)
            dx_ref[:, sl] = dxg
        if rider:
            pl.when(pl.program_id(0) == S // tm - 1)(lambda: rider.wait(*r_io))

    row = BS((tm, D), lambda i: (i, 0))
    in_specs = [BS((tm, K), lambda i: (i, 0)),
                BS((None,) + tuple(w4.shape[1:]), lambda i: (l, 0, 0, 0), pipeline_mode=pl.Buffered(1)), row,
                BS((None, 1, D), lambda i: (l, 0, 0))]
    args = [dy, w4, x, g3]
    out_specs, out_shape = [row], [SDS((S, D), F32)]
    if has_res:
        in_specs.append(row)
        args.append(dres)
        out_specs.append(row)
        out_shape.append(SDS((S, D), BF16))
    out_specs.append(BS((8, D), lambda i: (0, 0)))
    out_shape.append(SDS((8, D), F32))
    if not rider:
        return _pc(body, name=name, grid=(S // tm,), in_specs=in_specs, out_specs=out_specs, out_shape=out_shape)(*args)
    out = pl.pallas_call(body, name=name, grid=(S // tm,), in_specs=in_specs + [ANY] * nr,
                         out_specs=out_specs + [ANY] * nr, out_shape=out_shape + rider.out_shape,
                         scratch_shapes=rider.scratch(),
                         compiler_params=pltpu.CompilerParams(dimension_semantics=("arbitrary",),
                                                              vmem_limit_bytes=VMEM_LIMIT, has_side_effects=True))(
        *args, *rider.arrs)
    return (*out[:n_out], list(out[n_out:]))


def _rope(x, c, s1, s2):
    return x * c + pltpu.roll(x, 112, axis=1) * s1 + pltpu.roll(x, 16, axis=1) * s2


def _rope_t(dy, c, s1, s2):
    return dy * c + pltpu.roll(dy * s1, 16, axis=1) + pltpu.roll(dy * s2, 112, axis=1)


def _mla_prep(proj, gq3, gkv3, wuq, wk, wv, tabs, l, name):
    S = proj.shape[0]
    tm = min(512, S)
    tc, ts1, ts2 = tabs

    def body(cq_ref, ckv_ref, kr_ref, gq_ref, gkv_ref, wuq_ref, wk_ref, wv_ref, c_ref, s1_ref, s2_ref,
             q_ref, k_ref, v_ref, kt_ref, vt_ref):
        c, s1, s2 = c_ref[...], s1_ref[...], s2_ref[...]
        cq = cq_ref[...]
        rq = lax.rsqrt(jnp.mean(cq * cq, axis=-1, keepdims=True) + EPS)
        qa = _dot((cq * rq * gq_ref[...]).astype(BF16), wuq_ref[...])
        ckv = ckv_ref[...]
        rkv = lax.rsqrt(jnp.mean(ckv * ckv, axis=-1, keepdims=True) + EPS)
        ckvn = (ckv * rkv * gkv_ref[...]).astype(BF16)
        ka = _dot(ckvn, wk_ref[...])
        va = _dot(ckvn, wv_ref[...])
        v_ref[...] = va.astype(BF16)
        vt_ref[...] = va.T.astype(BF16)
        krr = _rope(kr_ref[...], c, s1, s2)
        for h in range(HEADS):
            sl = slice(h * 128, (h + 1) * 128)
            q_ref[:, sl] = (_rope(qa[:, sl], c, s1, s2) * QK_SCALE).astype(BF16)
            kh = ka[:, sl] + krr
            k_ref[:, sl] = kh.astype(BF16)
            kt_ref[sl, :] = kh.T.astype(BF16)

    lay = lambda a, b: BS((None, a, b), lambda i: (l, 0, 0))
    tab = BS((tm, 128), lambda i: (i, 0))
    return _pc(body, name=name, grid=(S // tm,),
               in_specs=[BS((tm, 256), lambda i: (i, 0)), BS((tm, 128), lambda i: (i, 2)), BS((tm, 128), lambda i: (i, 3)),
                         lay(1, 256), lay(1, 128), lay(256, 512), lay(128, 512), lay(128, 512), tab, tab, tab],
               out_specs=[BS((tm, 512), lambda i: (i, 0))] * 3 + [BS((512, tm), lambda i: (0, i))] * 2,
               out_shape=[SDS((S, 512), BF16)] * 3 + [SDS((512, S), BF16)] * 2)(
        proj, proj, proj, gq3, gkv3, wuq, wk, wv, tc, ts1, ts2)


def _mla_prep_bwd(dq, dk, dv, proj, gq3, gkv3, wuq, wk, wv, tabs, l, name):
    S = proj.shape[0]
    tm = min(512, S)
    tc, ts1, ts2 = tabs

    def body(dq_ref, dk_ref, dv_ref, cq_ref, ckv_ref, gq_ref, gkv_ref, wuq_ref, wk_ref, wv_ref, c_ref, s1_ref, s2_ref,
             dcq_ref, dckv_ref, dkr_ref, dwuq_ref, dwk_ref, dwv_ref, dgq_ref, dgkv_ref):
        @pl.when(pl.program_id(0) == 0)
        def _():
            for r in (dwuq_ref, dwk_ref, dwv_ref, dgq_ref, dgkv_ref):
                r[...] = jnp.zeros_like(r)

        c, s1, s2 = c_ref[...], s1_ref[...], s2_ref[...]
        dqp = jnp.concatenate(
            [_rope_t(dq_ref[h * 128:(h + 1) * 128, :].T * QK_SCALE, c, s1, s2) for h in range(HEADS)], axis=1).astype(BF16)
        cq = cq_ref[...]
        rq = lax.rsqrt(jnp.mean(cq * cq, axis=-1, keepdims=True) + EPS)
        cqh = cq * rq
        gq_v = gq_ref[...]
        dwuq_ref[...] += _dot_tn((cqh * gq_v).astype(BF16), dqp)
        dcqn = _dot_nt(dqp, wuq_ref[...])
        dgq_ref[...] += jnp.sum(dcqn * cqh, axis=0, keepdims=True)
        dxh = dcqn * gq_v
        dcq_ref[...] = (rq * (dxh - cqh * jnp.mean(dxh * cqh, axis=-1, keepdims=True))).astype(BF16)

        dkb = dk_ref[...].astype(BF16)
        dvb = dv_ref[...].astype(BF16)
        ckv = ckv_ref[...]
        rkv = lax.rsqrt(jnp.mean(ckv * ckv, axis=-1, keepdims=True) + EPS)
        ckh = ckv * rkv
        gkv_v = gkv_ref[...]
        ckvn = (ckh * gkv_v).astype(BF16)
        dwk_ref[...] += _dot_tn(ckvn, dkb)
        dwv_ref[...] += _dot_tn(ckvn, dvb)
        dckvn = _dot_nt(dkb, wk_ref[...]) + _dot_nt(dvb, wv_ref[...])
        dgkv_ref[...] += jnp.sum(dckvn * ckh, axis=0, keepdims=True)
        dyh = dckvn * gkv_v
        dckv_ref[...] = (rkv * (dyh - ckh * jnp.mean(dyh * ckh, axis=-1, keepdims=True))).astype(BF16)
        dks = dk_ref[:, 0:128] + dk_ref[:, 128:256] + dk_ref[:, 256:384] + dk_ref[:, 384:512]
        dkr_ref[...] = _rope_t(dks, c, s1, s2).astype(BF16)

    full = lambda a, b: BS((a, b), lambda i: (0, 0))
    lay = lambda a, b: BS((None, a, b), lambda i: (l, 0, 0))
    tab = BS((tm, 128), lambda i: (i, 0))
    row = lambda w: BS((tm, w), lambda i: (i, 0))
    return _pc(body, name=name, grid=(S // tm,),
               in_specs=[BS((512, tm), lambda i: (0, i)), row(512), row(512), BS((tm, 256), lambda i: (i, 0)),
                         BS((tm, 128), lambda i: (i, 2)),
                         lay(1, 256), lay(1, 128), lay(256, 512), lay(128, 512), lay(128, 512), tab, tab, tab],
               out_specs=[row(256), row(128), row(128), full(256, 512), full(128, 512), full(128, 512),
                          full(8, 256), full(8, 128)],
               out_shape=[SDS((S, 256), BF16), SDS((S, 128), BF16), SDS((S, 128), BF16), SDS((256, 512), F32),
                          SDS((128, 512), F32), SDS((128, 512), F32), SDS((8, 256), F32), SDS((8, 128), F32)])(
        dq, dk, dv, proj, proj, gq3, gkv3, wuq, wk, wv, tc, ts1, ts2)


def _causal_steps(n, q_outer):
    if q_outer:
        pairs = [(i, j) for i in range(n) for j in range(i + 1)]
    else:
        pairs = [(i, j) for j in range(n) for i in range(j, n)]
    return jnp.asarray([p[0] for p in pairs], jnp.int32), jnp.asarray([p[1] for p in pairs], jnp.int32)


def _mla_attn(q, k, vt, gts, layer, name):
    S = q.shape[0]
    t = min(512, S)
    n = S // t
    ng = len(gts)

    qi, kj = _causal_steps(n, True)
    last = qi.shape[0] - 1

    def body(qi_ref, kj_ref, q_ref, k_ref, vt_ref, *rest):
        (ya_ref, lse_ref), g_refs = rest[ng:ng + 2], rest[ng + 2:2 * ng + 2]
        m_sc, l_sc, acc_sc = rest[2 * ng + 2:2 * ng + 5]
        i, j = qi_ref[pl.program_id(1)], kj_ref[pl.program_id(1)]
        if ng:
            phases = _gather_phases(g_refs, [g.shape for g in gts], rest[2 * ng + 5], rest[2 * ng + 6], layer)
            for ph, (pp, ss) in zip(phases[:3], ((0, 0), (1, 0), (1, 2 * last // 3))):
                pl.when((pl.program_id(0) == pp) & (pl.program_id(1) == ss))(ph)

        @pl.when(j == 0)
        def _():
            m_sc[...] = jnp.full_like(m_sc, NEG_INF)
            l_sc[...] = jnp.zeros_like(l_sc)
            acc_sc[...] = jnp.zeros_like(acc_sc)

        def step(masked):
            for hh in range(2):
                sl = slice(hh * 128, (hh + 1) * 128)
                st = _dot_nt(k_ref[:, sl], q_ref[:, sl])
                if masked:
                    key = lax.broadcasted_iota(jnp.int32, (t, t), 0)
                    qry = lax.broadcasted_iota(jnp.int32, (t, t), 1)
                    st = jnp.where(key <= qry, st, NEG_INF)
                m_prev = m_sc[hh]
                m_new = jnp.maximum(m_prev, jnp.max(st, axis=0, keepdims=True))
                p = jnp.exp(st - m_new)
                alpha = jnp.exp(m_prev - m_new)
                l_sc[hh] = alpha * l_sc[hh] + jnp.sum(p, axis=0, keepdims=True)
                acc_sc[hh] = alpha * acc_sc[hh] + _dot(vt_ref[sl, :], p.astype(BF16))
                m_sc[hh] = m_new

        @pl.when(j < i)
        def _():
            step(False)

        @pl.when(j == i)
        def _():
            step(True)
            ya_ref[...] = (acc_sc[0] / l_sc[0] + acc_sc[1] / l_sc[1]).T
            for hh in range(2):
                lse_ref[hh] = m_sc[hh] + jnp.log(l_sc[hh])

        if ng:
            pl.when((pl.program_id(0) == 1) & (pl.program_id(1) == last))(phases[3])

    gs = pltpu.PrefetchScalarGridSpec(
        num_scalar_prefetch=2, grid=(2, qi.shape[0]),
        in_specs=[BS((t, 256), lambda p, s, qi, kj: (qi[s], p)), BS((t, 256), lambda p, s, qi, kj: (kj[s], p)),
                  BS((256, t), lambda p, s, qi, kj: (p, kj[s]))] + [ANY] * ng,
        out_specs=[BS((t, 128), lambda p, s, qi, kj: (qi[s], p)), BS((2, 1, t), lambda p, s, qi, kj: (p, 0, qi[s]))]
        + [ANY] * ng,
        scratch_shapes=[pltpu.VMEM((2, 1, t), F32), pltpu.VMEM((2, 1, t), F32), pltpu.VMEM((2, 128, t), F32)]
        + ([pltpu.SemaphoreType.DMA((7 * ng,)), pltpu.SemaphoreType.DMA((7 * ng,))] if ng else []))
    out = pl.pallas_call(body, name=name, grid_spec=gs,
                         out_shape=[SDS((S, 256), F32), SDS((HEADS, 1, S), F32)] + [SDS(g.shape, g.dtype) for g in gts],
                         input_output_aliases={5 + m: 2 + m for m in range(ng)},
                         compiler_params=pltpu.CompilerParams(dimension_semantics=("arbitrary",) * 2,
                                                              vmem_limit_bytes=VMEM_LIMIT, has_side_effects=bool(ng)))(
        qi, kj, q, k, vt, *gts)
    return out[0], out[1], list(out[2:])


def _mla_delta(dycat, ya, name):
    S = ya.shape[0]
    t = min(512, S)

    def body(do_ref, ya_ref, d_ref):
        prod = do_ref[...] * ya_ref[...]
        for p in range(2):
            pt = prod[:, p * 128:(p + 1) * 128].T
            d_ref[2 * p] = jnp.sum(pt[0:64, :], axis=0, keepdims=True)
            d_ref[2 * p + 1] = jnp.sum(pt[64:128, :], axis=0, keepdims=True)

    return _pc(body, name=name, grid=(S // t,),
               in_specs=[BS((t, 256), lambda i: (i, 0)), BS((t, 256), lambda i: (i, 0))],
               out_specs=BS((HEADS, 1, t), lambda i: (0, 0, i)), out_shape=SDS((HEADS, 1, S), F32))(dycat, ya)


def _mla_attn_bwd(q, k, kt, v, dya, lse, delta, rider, name):
    S = q.shape[0]
    t = min(512, S)
    n = S // t
    nr = rider.n if rider else 0

    qi, kj = _causal_steps(n, False)
    last = qi.shape[0] - 1

    def body(qi_ref, kj_ref, q_ref, k_ref, kt_ref, v_ref, do_ref, lse_ref, delta_ref, *rest):
        dqt_ref, dk_ref, dv_ref = rest[nr:nr + 3]
        r_io = (rest[:nr], rest[nr + 3:2 * nr + 3], rest[2 * nr + 3:])
        i, j = qi_ref[pl.program_id(1)], kj_ref[pl.program_id(1)]
        if rider:
            pl.when((pl.program_id(0) == 0) & (pl.program_id(1) == 0))(lambda: rider.start(*r_io))

        @pl.when(pl.program_id(1) == 0)
        def _():
            dqt_ref[...] = jnp.zeros_like(dqt_ref)

        @pl.when(i == j)
        def _():
            dk_ref[...] = jnp.zeros_like(dk_ref)
            dv_ref[...] = jnp.zeros_like(dv_ref)

        def step(masked):
            dob = do_ref[...].astype(BF16)
            cols = pl.ds(pl.multiple_of(i * t, t), t)
            for hh in range(2):
                sl = slice(hh * 128, (hh + 1) * 128)
                qv = q_ref[:, sl]
                p = jnp.exp(_dot_nt(k_ref[:, sl], qv) - lse_ref[hh])
                if masked:
                    key = lax.broadcasted_iota(jnp.int32, (t, t), 0)
                    qry = lax.broadcasted_iota(jnp.int32, (t, t), 1)
                    p = jnp.where(key <= qry, p, 0.0)
                dv_ref[:, sl] += _dot(p.astype(BF16), dob)
                ds = (p * (_dot_nt(v_ref[:, sl], dob) - delta_ref[hh])).astype(BF16)
                dk_ref[:, sl] += _dot(ds, qv)
                dqt_ref[sl, cols] += _dot(kt_ref[sl, :], ds)

        @pl.when(i > j)
        def _():
            step(False)

        @pl.when(i == j)
        def _():
            step(True)

        if rider:
            pl.when((pl.program_id(0) == 1) & (pl.program_id(1) == last))(lambda: rider.wait(*r_io))

    qs = BS((t, 256), lambda p, s, qi, kj: (qi[s], p))
    ks = BS((t, 256), lambda p, s, qi, kj: (kj[s], p))
    rowv = BS((2, 1, t), lambda p, s, qi, kj: (p, 0, qi[s]))
    gs = pltpu.PrefetchScalarGridSpec(
        num_scalar_prefetch=2, grid=(2, qi.shape[0]),
        in_specs=[qs, ks, BS((256, t), lambda p, s, qi, kj: (p, kj[s])), ks,
                  BS((t, 128), lambda p, s, qi, kj: (qi[s], p)), rowv, rowv] + [ANY] * nr,
        out_specs=[BS((256, S), lambda p, s, qi, kj: (p, 0)), ks, ks] + [ANY] * nr,
        scratch_shapes=rider.scratch() if rider else [])
    out = pl.pallas_call(body, name=name, grid_spec=gs,
                         out_shape=[SDS((512, S), F32), SDS((S, 512), F32), SDS((S, 512), F32)]
                         + (rider.out_shape if rider else []),
                         compiler_params=pltpu.CompilerParams(dimension_semantics=("arbitrary",) * 2,
                                                              vmem_limit_bytes=VMEM_LIMIT, has_side_effects=bool(rider)))(
        qi, kj, q, k, kt, v, dya, lse, delta, *(rider.arrs if rider else []))
    return out[0], out[1], out[2], list(out[3:])


def _swa_scores(qm, kk, valid, bias, sink):
    sc = jnp.where(valid, _dot_nt(qm, kk) * SWA_SCALE + bias, NEG_INF)
    m = jnp.maximum(jnp.max(sc, axis=-1, keepdims=True), sink)
    e = jnp.exp(sc - m)
    esink = jnp.exp(sink - m)
    den = jnp.sum(e, axis=-1, keepdims=True) + esink
    return e / den, esink / den


def _swa_consts(sink_ref, l):
    rows = HEADS * BLK
    r = lax.broadcasted_iota(jnp.int32, (rows, 2 * BLK), 0)
    c = lax.broadcasted_iota(jnp.int32, (rows, 2 * BLK), 1)
    dist = (r & (BLK - 1)) + BLK - c
    head = lax.broadcasted_iota(jnp.int32, (rows, 1), 0) // BLK

    def per_head(vals):
        return jnp.where(head == 0, vals[0], jnp.where(head == 1, vals[1], jnp.where(head == 2, vals[2], vals[3])))

    bias = -per_head(SLOPES) * dist.astype(F32)
    sink = per_head([sink_ref[l, h] for h in range(HEADS)])
    return (dist >= 0) & (dist < SWA_WINDOW), c >= BLK, bias, sink, head


def _to_half(xb, pos, b):
    return xb if pos == b else pltpu.roll(xb, 64, axis=1)


def _swa_stack(ref, st, lo):
    parts = []
    for b in range(2):
        xb = ref[pl.ds(st, BLK), b * 128:(b + 1) * 128]
        half = lo if b == 0 else ~lo
        parts += [jnp.where(half, _to_half(xb, pos, b), 0.0).astype(BF16) for pos in range(2)]
    return jnp.concatenate(parts, axis=0)


def _swa_unstack(x_all, lo):
    blocks = []
    for b in range(2):
        h0, h1 = (_to_half(x_all[(2 * b + pos) * BLK:(2 * b + pos + 1) * BLK], pos, b) for pos in range(2))
        blocks.append(jnp.where(lo, h0, h1))
    return blocks


def _swa(proj, sinks, l, name):
    S = proj.shape[0]
    nb = S // BLK

    def body(q_ref, k_ref, v_ref, sink_ref, o_ref, kp, vp):
        kp[0:BLK, :] = jnp.zeros((BLK, 128), BF16)
        vp[0:BLK, :] = jnp.zeros((BLK, 128), BF16)
        kp[BLK:, :] = k_ref[...].astype(BF16)
        vp[BLK:, :] = v_ref[...].astype(BF16)
        lo = lax.broadcasted_iota(jnp.int32, (BLK, 128), 1) < 64
        band, cur, bias, sink, _ = _swa_consts(sink_ref, l)

        def blk(i, carry):
            st = pl.multiple_of(i * BLK, BLK)
            kk = kp[pl.ds(st, 2 * BLK), :]
            vv = vp[pl.ds(st, 2 * BLK), :]
            p, _ = _swa_scores(_swa_stack(q_ref, st, lo), kk, band & (cur | (i > 0)), bias, sink)
            for b, ob in enumerate(_swa_unstack(_dot(p.astype(BF16), vv), lo)):
                o_ref[pl.ds(st, BLK), b * 128:(b + 1) * 128] = ob
            return carry

        lax.fori_loop(0, nb, blk, 0, unroll=2)

    return _pc(body, name=name, grid=(1,),
               in_specs=[BS((S, 256), lambda i: (0, C_QS // 256)), BS((S, 128), lambda i: (0, C_KS // 128)),
                         BS((S, 128), lambda i: (0, C_VS // 128)), BS(memory_space=pltpu.SMEM)],
               out_specs=BS((S, 256), lambda i: (0, 0)),
               out_shape=SDS((S, 256), F32),
               scratch=[pltpu.VMEM((S + BLK, 128), BF16), pltpu.VMEM((S + BLK, 128), BF16)])(proj, proj, proj, sinks)


def _swa_bwd(proj, sinks, dyd, l, name, rider=None):
    S = proj.shape[0]
    nb = S // BLK
    nr = rider.n if rider else 0

    def body(q_ref, k_ref, v_ref, sink_ref, do_ref, *rest):
        dq_ref, dk_ref, dv_ref, dsink_ref = rest[nr:nr + 4]
        kp, vp, dkp, dvp = rest[2 * nr + 4:2 * nr + 8]
        r_io = (rest[:nr], rest[nr + 4:2 * nr + 4], rest[2 * nr + 8:])
        if rider:
            rider.start(*r_io)
        kp[0:BLK, :] = jnp.zeros((BLK, 128), BF16)
        vp[0:BLK, :] = jnp.zeros((BLK, 128), BF16)
        kp[BLK:, :] = k_ref[...].astype(BF16)
        vp[BLK:, :] = v_ref[...].astype(BF16)
        dkp[...] = jnp.zeros_like(dkp)
        dvp[...] = jnp.zeros_like(dvp)
        lo = lax.broadcasted_iota(jnp.int32, (BLK, 128), 1) < 64
        lane8 = lax.broadcasted_iota(jnp.int32, (8, 128), 1)
        band, cur, bias, sink, head = _swa_consts(sink_ref, l)

        def blk(i, dsink):
            st = pl.multiple_of(i * BLK, BLK)
            kk = kp[pl.ds(st, 2 * BLK), :]
            vv = vp[pl.ds(st, 2 * BLK), :]
            qm, dom = _swa_stack(q_ref, st, lo), _swa_stack(do_ref, st, lo)
            p, psink = _swa_scores(qm, kk, band & (cur | (i > 0)), bias, sink)
            dp = _dot_nt(dom, vv)
            dvp[pl.ds(st, 2 * BLK), :] += _dot_tn(p.astype(BF16), dom)
            delta = jnp.sum(p * dp, axis=-1, keepdims=True)
            dsk = -psink * delta
            for h in range(HEADS):
                dsink = dsink + jnp.where(lane8 == h, jnp.sum(jnp.where(head == h, dsk, 0.0)), 0.0)
            dsc = (p * (dp - delta) * SWA_SCALE).astype(BF16)
            for b, dqb in enumerate(_swa_unstack(_dot(dsc, kk), lo)):
                dq_ref[pl.ds(st, BLK), b * 128:(b + 1) * 128] = dqb.astype(BF16)
            dkp[pl.ds(st, 2 * BLK), :] += _dot_tn(dsc, qm)
            return dsink

        dsink_ref[...] = lax.fori_loop(0, nb, blk, jnp.zeros((8, 128), F32), unroll=2)
        dk_ref[...] = dkp[BLK:, :].astype(BF16)
        dv_ref[...] = dvp[BLK:, :].astype(BF16)
        if rider:
            rider.wait(*r_io)

    in_specs = [BS((S, 256), lambda i: (0, C_QS // 256)), BS((S, 128), lambda i: (0, C_KS // 128)),
                BS((S, 128), lambda i: (0, C_VS // 128)), BS(memory_space=pltpu.SMEM), BS((S, 256), lambda i: (0, 3))]
    out_specs = [BS((S, 256), lambda i: (0, 0)), BS((S, 128), lambda i: (0, 0)), BS((S, 128), lambda i: (0, 0)),
                 BS((8, 128), lambda i: (0, 0))]
    out_shape = [SDS((S, 256), BF16), SDS((S, 128), BF16), SDS((S, 128), BF16), SDS((8, 128), F32)]
    scratch = [pltpu.VMEM((S + BLK, 128), BF16), pltpu.VMEM((S + BLK, 128), BF16),
               pltpu.VMEM((S + BLK, 128), F32), pltpu.VMEM((S + BLK, 128), F32)]
    if not rider:
        return _pc(body, name=name, grid=(1,), in_specs=in_specs, out_specs=out_specs, out_shape=out_shape,
                   scratch=scratch)(proj, proj, proj, sinks, dyd)
    out = pl.pallas_call(body, name=name, grid=(1,), in_specs=in_specs + [ANY] * nr, out_specs=out_specs + [ANY] * nr,
                         out_shape=out_shape + rider.out_shape, scratch_shapes=scratch + rider.scratch(),
                         compiler_params=pltpu.CompilerParams(dimension_semantics=("arbitrary",),
                                                              vmem_limit_bytes=VMEM_LIMIT, has_side_effects=True))(
        proj, proj, proj, sinks, dyd, *rider.arrs)
    return (*out[:4], list(out[4:]))


def _down(x, k, t):
    return jnp.where(t >= k, pltpu.roll(x, k, axis=0), 0.0)


def _up(x, k, t):
    n = x.shape[0]
    return jnp.where(t < n - k, pltpu.roll(x, n - k, axis=0), 0.0)


def _conv(proj, w8, l, name):
    S = proj.shape[0]

    def body(gb_ref, gc_ref, u_ref, w_ref, y_ref):
        t = lax.broadcasted_iota(jnp.int32, (S, 128), 0)
        z = gc_ref[...] * u_ref[...]
        c = w_ref[2:3, :] * z + w_ref[1:2, :] * _down(z, 1, t) + w_ref[0:1, :] * _down(z, 2, t)
        y_ref[...] = gb_ref[...] * c

    col = lambda c0: BS((S, 128), lambda i: (0, c0 // 128 + i))
    return _pc(body, name=name, grid=(2,),
               in_specs=[col(C_GB), col(C_GC), col(C_UC), BS((None, 8, 128), lambda i: (l, 0, i))],
               out_specs=BS((S, 128), lambda i: (0, i)), out_shape=SDS((S, 256), F32))(proj, proj, proj, w8)


def _conv_bwd(proj, w8, dycat, l, name):
    S = proj.shape[0]

    def body(gb_ref, gc_ref, u_ref, w_ref, dy_ref, dgb_ref, dgc_ref, du_ref, dw_ref):
        t = lax.broadcasted_iota(jnp.int32, (S, 128), 0)
        gc, u = gc_ref[...], u_ref[...]
        z = gc * u
        z1, z2 = _down(z, 1, t), _down(z, 2, t)
        w0, w1, w2 = w_ref[0:1, :], w_ref[1:2, :], w_ref[2:3, :]
        dy = dy_ref[...]
        dgb_ref[...] = (dy * (w2 * z + w1 * z1 + w0 * z2)).astype(BF16)
        dc = dy * gb_ref[...]
        dz = w2 * dc + w1 * _up(dc, 1, t) + w0 * _up(dc, 2, t)
        dgc_ref[...] = (dz * u).astype(BF16)
        du_ref[...] = (dz * gc).astype(BF16)
        row = lax.broadcasted_iota(jnp.int32, (8, 128), 0)
        sums = [jnp.sum(dc * zz, axis=0, keepdims=True) for zz in (z2, z1, z)]
        dw_ref[...] = jnp.where(row == 0, sums[0], jnp.where(row == 1, sums[1], jnp.where(row == 2, sums[2], 0.0)))

    col = lambda c0: BS((S, 128), lambda i: (0, c0 // 128 + i))
    out = BS((S, 128), lambda i: (0, i))
    return _pc(body, name=name, grid=(2,),
               in_specs=[col(C_GB), col(C_GC), col(C_UC), BS((None, 8, 128), lambda i: (l, 0, i)), col(256)],
               out_specs=[out, out, out, BS((8, 128), lambda i: (0, i))],
               out_shape=[SDS((S, 256), BF16)] * 3 + [SDS((8, 256), F32)])(proj, proj, proj, w8, dycat)


def _pool_parts(u, t, first):
    lo = lax.broadcasted_iota(jnp.int32, u.shape, 1) < 64
    s2 = u + _down(u, 1, t)
    s4 = s2 + _down(s2, 2, t)
    s8 = s4 + _down(s4, 4, t)
    s16 = s8 + _down(s8, 8, t)
    win = jnp.where(lo, jnp.where(first, s2, s8), jnp.where(first, s4, s16))
    wv = jnp.where(lo, jnp.where(first, 2, 8), jnp.where(first, 4, 16))
    cnt = jnp.minimum(t + 1, wv).astype(F32)
    return win, cnt, lo


def _pool(proj, pwd, scale3, l, name):
    S = proj.shape[0]

    def body(u_ref, pw_ref, sc_ref, y_ref):
        t = lax.broadcasted_iota(jnp.int32, (S, 128), 0)
        u = u_ref[...]
        win, cnt, _ = _pool_parts(u, t, pl.program_id(0) == 0)
        pooled = win / cnt - u
        y_ref[...] = _dot(pooled.astype(BF16), pw_ref[...]) * sc_ref[...]

    return _pc(body, name=name, grid=(2,),
               in_specs=[BS((S, 128), lambda i: (0, C_UP // 128 + i)), BS((None, 128, 128), lambda i: (l, i, 0)),
                         BS((None, 1, 128), lambda i: (l, 0, i))],
               out_specs=BS((S, 128), lambda i: (0, i)), out_shape=SDS((S, 256), F32))(proj, pwd, scale3)


def _pool_bwd(proj, pwd, scale3, dycat, l, name):
    S = proj.shape[0]

    def body(u_ref, pw_ref, sc_ref, dy_ref, du_ref, dpw_ref, dsc_ref):
        t = lax.broadcasted_iota(jnp.int32, (S, 128), 0)
        first = pl.program_id(0) == 0
        u = u_ref[...]
        win, cnt, lo = _pool_parts(u, t, first)
        pooled = (win / cnt - u).astype(BF16)
        pw = pw_ref[...]
        dy = dy_ref[...]
        dsc_ref[...] = jnp.broadcast_to(jnp.sum(dy * _dot(pooled, pw), axis=0, keepdims=True), (8, 128))
        dmb = (dy * sc_ref[...]).astype(BF16)
        dpw_ref[...] = _dot_tn(pooled, dmb)
        dpooled = _dot_nt(dmb, pw)
        a1 = dpooled / cnt
        a2 = a1 + _up(a1, 1, t)
        a4 = a2 + _up(a2, 2, t)
        a8 = a4 + _up(a4, 4, t)
        a16 = a8 + _up(a8, 8, t)
        dwin = jnp.where(lo, jnp.where(first, a2, a8), jnp.where(first, a4, a16))
        du_ref[...] = (dwin - dpooled).astype(BF16)

    return _pc(body, name=name, grid=(2,),
               in_specs=[BS((S, 128), lambda i: (0, C_UP // 128 + i)), BS((None, 128, 128), lambda i: (l, i, 0)),
                         BS((None, 1, 128), lambda i: (l, 0, i)), BS((S, 128), lambda i: (0, 4 + i))],
               out_specs=[BS((S, 128), lambda i: (0, i)), BS((128, 128), lambda i: (i, 0)), BS((8, 128), lambda i: (0, i))],
               out_shape=[SDS((S, 256), BF16), SDS((256, 128), F32), SDS((8, 256), F32)])(proj, pwd, scale3, dycat)


def _adamw(w, g, m, v, name, echo=False):
    n, a, b = w.shape
    tr = _row_tile(a, b)

    def body(w_ref, g_ref, m_ref, v_ref, d_ref, nm_ref, nv_ref, *g_out):
        gv = g_ref[...]
        if echo:
            g_out[0][...] = gv
        m_new = B1 * m_ref[...] + (1.0 - B1) * gv
        v_new = B2 * v_ref[...] + (1.0 - B2) * (gv * gv)
        m_hat = m_new / (1.0 - B1 ** STEP)
        v_hat = v_new / (1.0 - B2 ** STEP)
        d_ref[...] = -LR * (m_hat / (jnp.sqrt(v_hat) + ADAM_EPS) + WD * w_ref[...])
        nm_ref[...] = m_new
        nv_ref[...] = v_new

    sp = BS((None, tr, b), lambda i, t: (i, t, 0))
    return _pc(body, name=name, grid=(n, a // tr), in_specs=[sp] * 4, out_specs=[sp] * (3 + echo),
               out_shape=[SDS((n, a, b), F32)] * (3 + echo))(w, g, m, v)


def _prefetch_call(body, name, grid, in_specs, out_specs, out_shape):
    gs = pltpu.PrefetchScalarGridSpec(num_scalar_prefetch=1, grid=grid, in_specs=in_specs, out_specs=out_specs)
    return pl.pallas_call(body, name=name, grid_spec=gs, out_shape=out_shape, compiler_params=_params(len(grid)))


def _place(w, kc, dtype, name):
    _, a, b = w.shape

    def body(kc_ref, w_ref, o_ref):
        o_ref[...] = w_ref[...].astype(dtype)

    return _prefetch_call(body, name, (2,), [BS((None, a, b), lambda l, kc: (l, 0, 0))],
                          BS((None, None, a, b), lambda l, kc: (l, kc[0], 0, 0)), SDS((2, 4, a, b), dtype))(kc, w)


def _pair_sum(g, got, kc, name):
    _, _, a, b = g.shape
    tr = _row_tile(a, b)

    def body(kc_ref, a_ref, b_ref, t32_ref, t16_ref):
        s = a_ref[...] + b_ref[...]
        t16_ref[...] = s.astype(BF16)

        @pl.when(pl.program_id(1) == kc_ref[0])
        def _():
            t32_ref[...] = s

    sp = BS((None, tr, b), lambda t, k, kc: (k, t, 0))
    return _prefetch_call(body, name, (a // tr, 4),
                          [BS((None, None, tr, b), lambda t, k, kc: (kc[1], k, t, 0)), sp],
                          [BS((tr, b), lambda t, k, kc: (t, 0)), sp],
                          [SDS((a, b), F32), SDS((4, a, b), BF16)])(kc, g, got)


def _chip_sum(t32, gots, kc, name):
    a, b = t32.shape
    tr = _row_tile(a, b)

    def body(kc_ref, a_ref, *refs):
        acc = a_ref[...]
        for g_ref in refs[:-1]:
            for i in range(g_ref.shape[0]):
                acc = acc + g_ref[i].astype(F32)
        refs[-1][...] = acc

    return _prefetch_call(body, name, (a // tr,),
                          [BS((tr, b), lambda t, kc: (t, 0))] + [BS((g.shape[0], tr, b), lambda t, kc: (0, t, 0))
                                                                   for g in gots],
                          BS((None, tr, b), lambda t, kc: (kc[1], t, 0)), SDS((2, a, b), F32))(kc, t32, *gots)


def _me():
    return lax.axis_index("x"), lax.axis_index("y"), lax.axis_index("c")


def _other_chips(x, y):
    return [(1 - x, y), (x, 1 - y), (1 - x, 1 - y)]


ANY = BS(memory_space=pl.ANY)
COMM_PARAMS = pltpu.CompilerParams(has_side_effects=True)


def _gather(arrs, name):
    n = len(arrs)

    def body(*refs):
        for phase in _gather_phases(refs[n:2 * n], [a.shape for a in arrs], refs[2 * n], refs[2 * n + 1]):
            phase()

    return pl.pallas_call(body, name=name, out_shape=[SDS(a.shape, a.dtype) for a in arrs],
                          in_specs=[ANY] * n, out_specs=[ANY] * n, input_output_aliases={t: t for t in range(n)},
                          scratch_shapes=[pltpu.SemaphoreType.DMA((7 * n,)), pltpu.SemaphoreType.DMA((7 * n,))],
                          compiler_params=COMM_PARAMS)(*arrs)


def _gather_phases(outs, shapes, send_sems, recv_sems, layer=None):
    n = len(outs)
    cut = [s[2] // 2 // 16 * 16 for s in shapes]
    split = [r > 0 for r in cut]

    def plan():
        x, y, c = _me()
        return (c if layer is None else layer), (x, y), (x, y, c), (x, y, 1 - c), _other_chips(x, y)

    def role(moving, fn):
        if layer is None:
            fn()
        else:
            c = lax.axis_index("c")
            pl.when((c == layer) if moving else (c != layer))(fn)

    def blk(t, chip, layer, half=None):
        r = outs[t].at[layer, 2 * chip[0] + chip[1]]
        if half is None:
            return r
        return r.at[pl.ds(0, cut[t])] if half == 0 else r.at[pl.ds(cut[t], shapes[t][2] - cut[t])]

    def copy(t, k, ref, to):
        return pltpu.make_async_remote_copy(src_ref=ref, dst_ref=ref, send_sem=send_sems.at[7 * t + k],
                                            recv_sem=recv_sems.at[7 * t + k], device_id=to, device_id_type=MESH)

    def own_sends(t):
        c, chip, me, sib, (xn, yn, dg) = plan()
        cps = [copy(t, 0, blk(t, chip, c), (*xn, c)), copy(t, 1, blk(t, chip, c), (*yn, c))]
        return cps if split[t] else cps + [copy(t, 2, blk(t, chip, c), (*dg, c))]

    def relays(t):
        c, chip, me, sib, (xn, yn, dg) = plan()
        after_x = [copy(t, 4, blk(t, xn, c), sib)]
        after_y = [copy(t, 5, blk(t, yn, c), sib)]
        if split[t]:
            after_x.insert(0, copy(t, 2, blk(t, xn, c, 0), (*yn, c)))
            after_y.insert(0, copy(t, 3, blk(t, yn, c, 1), (*xn, c)))
        return after_x, after_y, [copy(t, 6, blk(t, dg, c), sib)]

    def send_own():
        for t in range(n):
            for cp in own_sends(t):
                cp.start()

    def relay_neighbours():
        c, chip, me, sib, (xn, yn, dg) = plan()
        for t in range(n):
            after_x, after_y, _ = relays(t)
            copy(t, 0, blk(t, xn, c), me).wait_recv()
            for cp in after_x:
                cp.start()
            copy(t, 1, blk(t, yn, c), me).wait_recv()
            for cp in after_y:
                cp.start()

    def relay_diagonal():
        c, chip, me, sib, (xn, yn, dg) = plan()
        for t in range(n):
            if split[t]:
                copy(t, 2, blk(t, dg, c, 0), me).wait_recv()
                copy(t, 3, blk(t, dg, c, 1), me).wait_recv()
            else:
                copy(t, 2, blk(t, dg, c), me).wait_recv()
            relays(t)[2][0].start()

    def take_sibling():
        _, chip, me, sib, (xn, yn, dg) = plan()
        theirs = 1 - lax.axis_index("c") if layer is None else layer
        for t in range(n):
            for k, peer in ((4, xn), (5, yn), (6, dg)):
                copy(t, k, blk(t, peer, theirs), me).wait_recv()

    def drain_sends():
        for t in range(n):
            after_x, after_y, after_d = relays(t)
            for cp in own_sends(t) + after_x + after_y + after_d:
                cp.wait_send()

    def finish():
        role(False, take_sibling)
        role(True, drain_sends)

    return ((lambda: role(True, send_own)), (lambda: role(True, relay_neighbours)),
            (lambda: role(True, relay_diagonal)), finish)


def _swap_copies(ins, outs, send_sems, recv_sems):
    x, y, c = _me()
    return [pltpu.make_async_remote_copy(src_ref=ins[t].at[1 - c], dst_ref=outs[t], send_sem=send_sems.at[t],
                                         recv_sem=recv_sems.at[t], device_id=(x, y, 1 - c), device_id_type=MESH)
            for t in range(len(ins))]


def _exchange_copies(peers, ins, outs, send_sems, recv_sems):
    x, y, c = _me()
    chips = _other_chips(x, y)
    n = len(peers)
    return [pltpu.make_async_remote_copy(src_ref=ins[t].at[2 * chips[j][0] + chips[j][1]], dst_ref=outs[t].at[i],
                                         send_sem=send_sems.at[n * t + i], recv_sem=recv_sems.at[n * t + i],
                                         device_id=(*chips[j], c), device_id_type=MESH)
            for i, j in enumerate(peers) for t in range(len(ins))]


class _Rider:
    def __init__(self, arrs, out_shape, nsem, copies):
        self.arrs, self.out_shape, self.nsem, self.copies = list(arrs), out_shape, nsem, copies
        self.n = len(self.arrs)

    def scratch(self):
        return [pltpu.SemaphoreType.DMA((self.nsem,)), pltpu.SemaphoreType.DMA((self.nsem,))]

    def start(self, ins, outs, sems):
        for cp in self.copies(ins, outs, *sems):
            cp.start()

    def wait(self, ins, outs, sems):
        for cp in self.copies(ins, outs, *sems):
            cp.wait()


def _swap_rider(gs):
    return _Rider(gs, [SDS(g.shape[1:], g.dtype) for g in gs], len(gs), _swap_copies)


def _exchange_rider(ts, peers=(0, 1, 2)):
    return _Rider(ts, [SDS((len(peers),) + t.shape[1:], t.dtype) for t in ts], len(peers) * len(ts),
                  functools.partial(_exchange_copies, peers))


def _ride_alone(rider, name):
    n = rider.n

    def body(*refs):
        rider.start(refs[:n], refs[n:2 * n], refs[2 * n:])
        rider.wait(refs[:n], refs[n:2 * n], refs[2 * n:])

    return pl.pallas_call(body, name=name, out_shape=rider.out_shape, in_specs=[ANY] * n, out_specs=[ANY] * n,
                          scratch_shapes=rider.scratch(), compiler_params=COMM_PARAMS)(*rider.arrs)


def _join_layers(us, name):
    n = len(us)

    def body(*refs):
        outs, send_sems, recv_sems = refs[n:2 * n], refs[2 * n], refs[2 * n + 1]
        x, y, c = _me()
        cps = [pltpu.make_async_remote_copy(src_ref=outs[t].at[c], dst_ref=outs[t].at[c], send_sem=send_sems.at[t],
                                            recv_sem=recv_sems.at[t], device_id=(x, y, 1 - c), device_id_type=MESH)
               for t in range(n)]
        for cp in cps:
            cp.start()
        for cp in cps:
            cp.wait()

    return pl.pallas_call(body, name=name, out_shape=[SDS(u.shape, u.dtype) for u in us],
                          in_specs=[ANY] * n, out_specs=[ANY] * n, input_output_aliases={t: t for t in range(n)},
                          scratch_shapes=[pltpu.SemaphoreType.DMA((n,)), pltpu.SemaphoreType.DMA((n,))],
                          compiler_params=COMM_PARAMS)(*us)


def _allsum_small(v, name, rider=None):
    M = v.shape[0]
    nr = rider.n if rider else 0

    def body(x_ref, *rest):
        o_ref = rest[nr]
        all_ref, send_sems, recv_sems, local_sem = rest[2 * nr + 1:2 * nr + 5]
        r_io = (rest[:nr], rest[nr + 1:2 * nr + 1], rest[2 * nr + 5:])
        x, y, c = _me()
        me, sib = (x, y, c), (x, y, 1 - c)
        chips = _other_chips(x, y)

        def rows(px, py, pc):
            return all_ref.at[pl.ds((4 * px + 2 * py + pc) * M, M), :]

        def copy(k, block, to, src=None):
            return pltpu.make_async_remote_copy(src_ref=rows(*block) if src is None else src, dst_ref=rows(*block),
                                                send_sem=send_sems.at[k], recv_sem=recv_sems.at[k],
                                                device_id=to, device_id_type=MESH)

        mine = pltpu.make_async_copy(x_ref, rows(*me), local_sem)
        mine.start()
        first = [copy(0, me, sib, src=x_ref)]
        first += [copy(1 + j, me, (*chip, c), src=x_ref) for j, chip in enumerate(chips)]
        for cp in first:
            cp.start()
        if rider:
            rider.start(*r_io)
        passed = [copy(4 + j, (*chip, c), sib) for j, chip in enumerate(chips)]
        for j, chip in enumerate(chips):
            copy(1 + j, (*chip, c), me).wait_recv()
            passed[j].start()
        copy(0, sib, me).wait_recv()
        for j, chip in enumerate(chips):
            copy(4 + j, (*chip, 1 - c), me).wait_recv()
        for cp in first + passed:
            cp.wait_send()
        mine.wait()
        acc = all_ref[0:M, :]
        for d in range(1, 8):
            acc = acc + all_ref[d * M:(d + 1) * M, :]
        o_ref[...] = acc
        if rider:
            rider.wait(*r_io)

    vm = BS(memory_space=pltpu.VMEM)
    out = pl.pallas_call(body, name=name, out_shape=[SDS((M, LANES), F32)] + (rider.out_shape if rider else []),
                         in_specs=[vm] + [ANY] * nr, out_specs=[vm] + [ANY] * nr,
                         scratch_shapes=[pltpu.VMEM((8 * M, LANES), F32), pltpu.SemaphoreType.DMA((7,)),
                                         pltpu.SemaphoreType.DMA((7,)), pltpu.SemaphoreType.DMA]
                         + (rider.scratch() if rider else []),
                         compiler_params=pltpu.CompilerParams(has_side_effects=True, vmem_limit_bytes=VMEM_LIMIT))(
        v, *(rider.arrs if rider else []))
    return out[0], list(out[1:])


FFN = ("w_gate_up", "w_down")
REST = ("w_in", "w_o", "w_uq", "w_ukv")
BIG = FFN + REST
TINY = ("conv_w",)
REPL = ("attn_norm", "mla_q_norm", "mla_kv_norm", "pool_w", "pool_scale", "swa_sinks", "mix_norm", "ffn_norm",
        "final_norm")
ORDER = ("attn_norm", "w_in", "mla_q_norm", "w_uq", "mla_kv_norm", "w_ukv", "conv_w", "pool_w", "pool_scale",
         "swa_sinks", "mix_norm", "w_o", "ffn_norm", "w_gate_up", "w_down", "final_norm")


def _rows8(shape):
    return -(-int(np.prod(shape)) // (8 * LANES)) * 8


def _pack(arrs):
    parts = []
    for a in arrs:
        r = _rows8(a.shape)
        parts.append(jnp.pad(a.reshape(-1), (0, r * LANES - a.size)).reshape(r, LANES))
    return jnp.concatenate(parts, axis=0)


def _unpack(buf, shapes):
    out, r0 = [], 0
    for s in shapes:
        n, r = int(np.prod(s)), _rows8(s)
        rows = buf[r0:r0 + r]
        out.append(rows.reshape(s) if n == r * LANES else rows.reshape(-1)[:n].reshape(s))
        r0 += r
    return out


def _cols_joined(g):
    return jnp.transpose(g, (0, 2, 1, 3)).reshape(g.shape[0], g.shape[2], 4 * g.shape[3])


def _cols_split(w):
    n, a, b4 = w.shape
    return jnp.transpose(w.reshape(n, a, 4, b4 // 4), (0, 2, 1, 3))


def _rope_tables(S):
    inv = 1.0 / (10000.0 ** (jnp.arange(0, 32, 2, dtype=F32) / 32))
    ang = jnp.arange(S, dtype=F32)[:, None] * inv[None, :]
    cos, sin = jnp.cos(ang), jnp.sin(ang)
    z = lambda w: jnp.zeros((S, w), F32)
    tc = jnp.concatenate([jnp.ones((S, 64), F32), cos, cos, jnp.ones((S, 32), F32)], axis=1)
    ts1 = jnp.concatenate([z(64), -sin, z(48)], axis=1)
    ts2 = jnp.concatenate([z(80), sin, z(32)], axis=1)
    return tc, ts1, ts2


def _pad_w_in(wt):
    z = lambda n: jnp.zeros((wt.shape[0], n, wt.shape[2]), wt.dtype)
    return jnp.concatenate([wt[:, 0:384], z(64), wt[:, 384:416], z(32), wt[:, 416:1952]], axis=1)


def _unpad_w_in(dt):
    return jnp.concatenate([dt[:, 0:384], dt[:, 448:480], dt[:, 512:2048]], axis=1)


def _pad_heads(w, src, offs):
    cols = []
    for h in range(HEADS):
        src0, n = src[h]
        z = lambda k: jnp.zeros(w.shape[:-1] + (k,), w.dtype)
        cols += [z(offs[h]), w[..., src0:src0 + n], z(128 - offs[h] - n)]
    return jnp.concatenate(cols, axis=-1)


UQ_SRC = [(h * 96, 96) for h in range(HEADS)]
KN_SRC = [(h * 128, 64) for h in range(HEADS)]
V_SRC = [(h * 128 + 64, 64) for h in range(HEADS)]
ZERO_OFF = [0] * HEADS
V_OFF = [(h % 2) * 64 for h in range(HEADS)]


def _unpad_heads(d, src, offs):
    return [d[..., h * 128 + offs[h]: h * 128 + offs[h] + src[h][1]] for h in range(HEADS)]


def kernel(x, attn_norm, w_in, mla_q_norm, w_uq, mla_kv_norm, w_ukv, conv_w, pool_w, pool_scale, swa_sinks, mix_norm, w_o, ffn_norm, w_gate_up, w_down, final_norm, loss_target, m_attn_norm, m_w_in, m_mla_q_norm, m_w_uq, m_mla_kv_norm, m_w_ukv, m_conv_w, m_pool_w, m_pool_scale, m_swa_sinks, m_mix_norm, m_w_o, m_ffn_norm, m_w_gate_up, m_w_down, m_final_norm, v_attn_norm, v_w_in, v_mla_q_norm, v_w_uq, v_mla_kv_norm, v_w_ukv, v_conv_w, v_pool_w, v_pool_scale, v_swa_sinks, v_mix_norm, v_w_o, v_ffn_norm, v_w_gate_up, v_w_down, v_final_norm):
    W = dict(attn_norm=attn_norm, w_in=w_in, mla_q_norm=mla_q_norm, w_uq=w_uq, mla_kv_norm=mla_kv_norm, w_ukv=w_ukv,
             conv_w=conv_w, pool_w=pool_w, pool_scale=pool_scale, swa_sinks=swa_sinks, mix_norm=mix_norm, w_o=w_o,
             ffn_norm=ffn_norm, w_gate_up=w_gate_up, w_down=w_down, final_norm=final_norm)
    M1 = dict(attn_norm=m_attn_norm, w_in=m_w_in, mla_q_norm=m_mla_q_norm, w_uq=m_w_uq, mla_kv_norm=m_mla_kv_norm,
              w_ukv=m_w_ukv, conv_w=m_conv_w, pool_w=m_pool_w, pool_scale=m_pool_scale, swa_sinks=m_swa_sinks,
              mix_norm=m_mix_norm, w_o=m_w_o, ffn_norm=m_ffn_norm, w_gate_up=m_w_gate_up, w_down=m_w_down,
              final_norm=m_final_norm)
    V2 = dict(attn_norm=v_attn_norm, w_in=v_w_in, mla_q_norm=v_mla_q_norm, w_uq=v_w_uq, mla_kv_norm=v_mla_kv_norm,
              w_ukv=v_w_ukv, conv_w=v_conv_w, pool_w=v_pool_w, pool_scale=v_pool_scale, swa_sinks=v_swa_sinks,
              mix_norm=v_mix_norm, w_o=v_w_o, ffn_norm=v_ffn_norm, w_gate_up=v_w_gate_up, w_down=v_w_down,
              final_norm=v_final_norm)
    S = x.shape[1]
    xc, yc, cc = _me()
    chip = 2 * xc + yc
    kc = jnp.stack([chip, cc]).astype(jnp.int32)

    first, later = ("w_in", "w_uq", "w_ukv", "conv_w"), ("w_o", "w_gate_up", "w_down")
    T = lambda a: jnp.swapaxes(a, 1, 2)
    W["w_in"], M1["w_in"], V2["w_in"] = T(w_in), T(m_w_in), T(v_w_in)
    placed = {n: _place(W[n], kc, F32 if n == "conv_w" else BF16, f"place_{n}") for n in first + later}
    gi, gq, gkv, gcv = _gather([placed[n] for n in first], "gather_weights")
    later_w = [placed[n] for n in later]
    win_p = _pad_w_in(gi.reshape(2, 4 * gi.shape[2], D))
    wuq_p = _pad_heads(_cols_joined(gq), UQ_SRC, ZERO_OFF)
    wukv = _cols_joined(gkv)
    wk_p = _pad_heads(wukv, KN_SRC, ZERO_OFF)
    wv_p = _pad_heads(wukv, V_SRC, V_OFF)
    conv8 = jnp.pad(_cols_joined(gcv), ((0, 0), (0, 5), (0, 0)))
    pwd = jnp.concatenate([jnp.concatenate(
        [jnp.pad(pool_w[:, 2 * b], ((0, 0), (0, 0), (0, 64))), jnp.pad(pool_w[:, 2 * b + 1], ((0, 0), (0, 0), (64, 0)))],
        axis=1) for b in range(2)], axis=1).astype(BF16)
    tabs = _rope_tables(S)
    g_attn, g_q, g_kv, g_mix, g_ffn, g_ps = (_g3(W[n]) for n in ("attn_norm", "mla_q_norm", "mla_kv_norm", "mix_norm",
                                                                  "ffn_norm", "pool_scale"))

    xs = [x[0]]
    saved = []
    for l in range(DEPTH):
        x0 = xs[-1]
        proj, h = _norm_mm(x0, g_attn, l, win_p, _wspec_in(l), D_INP, D_INP, F32, f"in_proj{l}", w_t=True)
        q, k, v, kt, vt = _mla_prep(proj, g_q, g_kv, wuq_p, wk_p, wv_p, tabs, l, f"mla_prep{l}")
        ya, lse, later_w = _mla_attn(q, k, vt, later_w, l, f"mla_attn{l}")
        go, gu4, gd = later_w
        wo, wdown = go.reshape(2, D, D), gd.reshape(2, D_FF, D)
        yb = _conv(proj, conv8, l, f"conv{l}")
        ycp = _pool(proj, pwd, g_ps, l, f"pool{l}")
        yd = _swa(proj, swa_sinks, l, f"swa{l}")
        x1, ycat, mixed = _mix_out(x0, ya, yb, ycp, yd, g_mix, wo, l, f"mix_out{l}")
        gu, h2 = _norm_mm(x1, g_ffn, l, gu4, _wspec_gu(l), 2 * D_FF, 2 * D_FF // 4, BF16, f"gate_up{l}")
        x2, act = _swiglu_mm_res(x1, gu, wdown, l, f"down{l}")
        saved.append(dict(x0=x0, proj=proj, h=h, q=q, k=k, kt=kt, v=v, lse=lse, x1=x1, ycat=ycat, mixed=mixed,
                          gu=gu, h2=h2, act=act))
        xs.append(x2)

    dx, dx16, dg_final, loss_tile = _loss_head(xs[-1], final_norm.reshape(1, D), loss_target[0], "loss_head")
    loss_here = (loss_tile[0, 0] * (0.5 / D)).reshape(1)

    G = {n: [None] * DEPTH for n in ("w_uq", "w_ukv") + TINY + REPL if n != "final_norm"}
    gw_in = gw_o = gw_gu = gw_down = None
    for l in reversed(range(DEPTH)):
        sv = saved[l]
        dgu = _bwd_down(dx16, wdown, sv["gu"], l, f"down_bwd{l}")
        gw_down = _mm_tn(sv["act"], dx16, l, gw_down, f"dw_down{l}")
        gw_gu = _mm_tn(sv["h2"], dgu, l, gw_gu, f"dw_gate_up{l}", split4=True)
        exchange_gu = None
        if l == 0:
            g_f = [gw_gu, gw_down.reshape(2, 4, D_FF // 4, D)]
            dx1, dx1_16, dg, got_f = _mm_nt_normbwd(dgu, gu4, l, sv["x1"], g_ffn, dx, 1, f"gate_up_bwd{l}",
                                                    rider=_swap_rider(g_f))
            pairs_f = [_pair_sum(g, o, kc, f"rs_pair_sum_{n}") for g, o, n in zip(g_f, got_f, FFN)]
            exchange_gu = _exchange_rider([pairs_f[0][1]])
            down_to_diagonal = _exchange_rider([pairs_f[1][1]], peers=(2,))
            down_to_neighbours = _exchange_rider([pairs_f[1][1]], peers=(0, 1))
        else:
            dx1, dx1_16, dg = _mm_nt_normbwd(dgu, gu4, l, sv["x1"], g_ffn, dx, 1, f"gate_up_bwd{l}")
        G["ffn_norm"][l] = dg[0]
        gw_o = _mm_tn(sv["mixed"], dx1_16, l, gw_o, f"dw_o{l}")
        if l == 0:
            dycat, dg, got_dg = _mm_nt_normbwd(dx1_16, wo.reshape(2, 1, D, D), l, sv["ycat"], g_mix, None, 4,
                                               f"mix_bwd{l}", rider=down_to_diagonal)
        else:
            dycat, dg = _mm_nt_normbwd(dx1_16, wo.reshape(2, 1, D, D), l, sv["ycat"], g_mix, None, 4, f"mix_bwd{l}")
        G["mix_norm"][l] = dg[0]

        proj = sv["proj"]
        delta = _mla_delta(dycat, sv["ycat"], f"mla_delta{l}")
        dq, dk, dv, got3_gu = _mla_attn_bwd(sv["q"], sv["k"], sv["kt"], sv["v"], dycat, sv["lse"], delta, exchange_gu,
                                            f"mla_attn_bwd{l}")
        dcq, dckv, dkr, dwuq, dwk, dwv, dgq, dgkv = _mla_prep_bwd(
            dq, dk, dv, proj, g_q, g_kv, wuq_p, wk_p, wv_p, tabs, l, f"mla_prep_bwd{l}")
        dgb, dgc, duc, dcw = _conv_bwd(proj, conv8, dycat, l, f"conv_bwd{l}")
        dup, dpw, dps = _pool_bwd(proj, pwd, g_ps, dycat, l, f"pool_bwd{l}")
        if l == 0:
            dqs, dks, dvs, dsink, got_xy = _swa_bwd(proj, swa_sinks, dycat, l, f"swa_bwd{l}", rider=down_to_neighbours)
            got3_f = [got3_gu, got_xy + got_dg]
        else:
            dqs, dks, dvs, dsink = _swa_bwd(proj, swa_sinks, dycat, l, f"swa_bwd{l}")
        dproj = jnp.concatenate([dcq, dckv, dkr, dgb, dgc, duc, dup, dqs, dks, dvs], axis=1)
        gw_in = _mm_tn(dproj, sv["h"], l, gw_in, f"dw_in{l}")
        G["w_uq"][l] = jnp.concatenate(_unpad_heads(dwuq, UQ_SRC, ZERO_OFF), axis=1)
        kn, vv = _unpad_heads(dwk, KN_SRC, ZERO_OFF), _unpad_heads(dwv, V_SRC, V_OFF)
        G["w_ukv"][l] = jnp.concatenate([t for h in range(HEADS) for t in (kn[h], vv[h])], axis=1)
        dx, dx16, dg = _mm_nt_normbwd(dproj, win_p.reshape(2, 1, D_INP, D), l, sv["x0"], g_attn, dx1, 1, f"in_proj_bwd{l}",
                                      w_t=True)
        G["attn_norm"][l] = dg[0]
        G["mla_q_norm"][l] = dgq[0]
        G["mla_kv_norm"][l] = dgkv[0]
        G["conv_w"][l] = dcw[0:3]
        G["pool_w"][l] = jnp.stack([dpw[0:64, 0:64], dpw[64:128, 64:128], dpw[128:192, 0:64], dpw[192:256, 64:128]])
        G["pool_scale"][l] = dps[0]
        G["swa_sinks"][l] = dsink[0, 0:4]
    grad_x = dx[None]
    Gl = {n: jnp.stack(G[n]) for n in TINY + REPL if n != "final_norm"}
    Gl["final_norm"] = dg_final[0]

    us_f = [_chip_sum(p[0], o3, kc, f"rs_chip_sum_{n}") for p, o3, n in zip(pairs_f, got3_f, FFN)]
    gsum_f = _join_layers(us_f, "rs_join_cores_ffn")
    g_r = [_unpad_w_in(gw_in).reshape(2, 4, -1, D), gw_o.reshape(2, 4, D // 4, D), _cols_split(jnp.stack(G["w_uq"])),
           _cols_split(jnp.stack(G["w_ukv"]))]
    got_r = _ride_alone(_swap_rider(g_r), "rs_swap_cores")
    pairs_r = [_pair_sum(g, o, kc, f"rs_pair_sum_{n}") for g, o, n in zip(g_r, got_r, REST)]
    small = TINY + REPL
    full_shapes = [Gl[n].shape for n in small] + [(1,)]
    summed, got3_r = _allsum_small(_pack([Gl[n] for n in small] + [loss_here]), "allsum_small",
                                   rider=_exchange_rider([p[1] for p in pairs_r]))
    summed = _unpack(summed, full_shapes)
    loss = summed.pop().reshape(())
    us_r = [_chip_sum(p[0], [o3], kc, f"rs_chip_sum_{n}") for p, o3, n in zip(pairs_r, got3_r, REST)]
    gsum_r = _join_layers(us_r, "rs_join_cores")
    res = {}
    for n, g in zip(BIG, gsum_f + gsum_r):
        d_, m_, v_, g_ = _adamw(W[n], g, M1[n], V2[n], f"adamw_{n}", echo=True)
        back = T if n == "w_in" else (lambda a: a)
        res["g", n], res["d", n], res["m", n], res["v", n] = back(g_), back(d_), back(m_), back(v_)

    def as3(a):
        if a.ndim <= 2:
            return a.reshape((1,) * (3 - a.ndim) + a.shape)
        return a.reshape(a.shape[0], -1, a.shape[-1])

    for n, g in zip(small, summed):
        if n in TINY:
            wdt = W[n].shape[2]
            g = lax.dynamic_slice_in_dim(g, chip * wdt, wdt, axis=2)
        out = _adamw(as3(W[n]), as3(g), as3(M1[n]), as3(V2[n]), f"adamw_{n}")
        res["g", n] = g
        res["d", n], res["m", n], res["v", n] = (o.reshape(W[n].shape) for o in out)

    return (loss, grad_x, *[res["g", n] for n in ORDER], *[res["d", n] for n in ORDER],
            *[res["m", n] for n in ORDER], *[res["v", n] for n in ORDER])
```

```python
import functools
import math

import numpy as np
import jax
import jax.numpy as jnp
from jax import lax
from jax.experimental import pallas as pl
from jax.experimental.pallas import tpu as pltpu

F32, BF16 = jnp.float32, jnp.bfloat16
SDS = jax.ShapeDtypeStruct
BS = pl.BlockSpec
MESH = pl.DeviceIdType.MESH

D = 1024
DEPTH = 2
HEADS = 4
D_FF = 2816
D_INP = 2048
EPS = 1e-6
SWA_WINDOW = 128
BLK = 128
SLOPES = tuple(2.0 ** (-8.0 * (i + 1) / 4) for i in range(4))
QK_SCALE = 1.0 / math.sqrt(96)
SWA_SCALE = 1.0 / math.sqrt(64)
LR, B1, B2, ADAM_EPS, WD, STEP = 0.001, 0.9, 0.999, 1e-08, 0.01, 10

LANES = 1024
VMEM_LIMIT = 56 * 1024 * 1024
NEG_INF = float("-inf")

C_CQ, C_CKV, C_KR, C_GB, C_GC, C_UC, C_UP, C_QS, C_KS, C_VS = 0, 256, 384, 512, 768, 1024, 1280, 1536, 1792, 1920


def _params(ngrid):
    return pltpu.CompilerParams(dimension_semantics=("arbitrary",) * ngrid, vmem_limit_bytes=VMEM_LIMIT)


def _pc(body, *, name, grid, in_specs, out_specs, out_shape, scratch=(), aliases=None):
    return pl.pallas_call(
        body, name=name, grid=grid, in_specs=in_specs, out_specs=out_specs, out_shape=out_shape,
        scratch_shapes=scratch, input_output_aliases=aliases or {}, compiler_params=_params(len(grid)))


def _dot(a, b):
    return jnp.dot(a, b, preferred_element_type=F32)


def _dot_nt(a, b):
    return lax.dot_general(a, b, (((1,), (1,)), ((), ())), preferred_element_type=F32)


def _dot_tn(a, b):
    return lax.dot_general(a, b, (((0,), (0,)), ((), ())), preferred_element_type=F32)


def _tile(n, cap):
    if n <= cap:
        return n
    t = cap - cap % 128
    while n % t:
        t -= 128
    return t


def _row_tile(a, b, cap=262144):
    bp = -(-b // 128) * 128
    best = None
    for t in range(8, a + 1, 8):
        if a % t == 0 and t * bp <= cap:
            best = t
    if best is None or (best < 64 and a * bp <= 2 * cap):
        return a
    return best


def _g3(a):
    return a.reshape(a.shape[0], 1, a.shape[1])


def _norm_mm(x, g3, l, w, wspec, N, tn, out_dtype, name, w_t=False):
    S, K = x.shape
    tm = min(1024 if out_dtype == BF16 else 512, S)

    def body(x_ref, g_ref, w_ref, y_ref, h_ref):
        @pl.when(pl.program_id(1) == 0)
        def _():
            xv = x_ref[...]
            r = lax.rsqrt(jnp.mean(xv * xv, axis=-1, keepdims=True) + EPS)
            h_ref[...] = (xv * r * g_ref[...]).astype(BF16)

        y_ref[...] = (_dot_nt if w_t else _dot)(h_ref[...], w_ref[...]).astype(out_dtype)

    return _pc(body, name=name, grid=(S // tm, N // tn),
               in_specs=[BS((tm, K), lambda i, j: (i, 0)), BS((None, 1, K), lambda i, j: (l, 0, 0)), wspec],
               out_specs=[BS((tm, tn), lambda i, j: (i, j)), BS((tm, K), lambda i, j: (i, 0))],
               out_shape=[SDS((S, N), out_dtype), SDS((S, K), BF16)])(x, g3, w)


def _wspec_in(l):
    return BS((None, D_INP, D), lambda i, j: (l, j, 0))


def _wspec_gu(l):
    return BS((None, None, D, 2 * D_FF // 4), lambda i, j: (l, j, 0, 0))


def _mix_out(x0, ya, yb, yc, yd, gmix3, wo, l, name):
    S = x0.shape[0]
    tm = min(512, S)

    def body(x_ref, ya_ref, yb_ref, yc_ref, yd_ref, g_ref, w_ref, x1_ref, ycat_ref, mixed_ref):
        groups = [ya_ref[...], yb_ref[...], yc_ref[...], yd_ref[...]]
        for gi, yg in enumerate(groups):
            sl = slice(gi * 256, (gi + 1) * 256)
            r = lax.rsqrt(jnp.mean(yg * yg, axis=-1, keepdims=True) + EPS)
            ycat_ref[:, sl] = yg
            mixed_ref[:, sl] = (yg * r * g_ref[:, sl]).astype(BF16)
        x1_ref[...] = x_ref[...] + _dot(mixed_ref[...], w_ref[...])

    row = lambda w: BS((tm, w), lambda i: (i, 0))
    return _pc(body, name=name, grid=(S // tm,),
               in_specs=[row(D), row(256), row(256), row(256), row(256), BS((None, 1, D), lambda i: (l, 0, 0)),
                         BS((None, D, D), lambda i: (l, 0, 0))],
               out_specs=[row(D), row(D), row(D)],
               out_shape=[SDS((S, D), F32), SDS((S, D), F32), SDS((S, D), BF16)])(x0, ya, yb, yc, yd, gmix3, wo)


def _swiglu_mm_res(x1, gu, wdown, l, name):
    S = x1.shape[0]
    tm = min(256, S)

    def body(x_ref, gate_ref, up_ref, w_ref, x2_ref, act_ref):
        acc = x_ref[...]
        for c0 in range(0, D_FF, D_FF // 2):
            cs = slice(c0, c0 + D_FF // 2)
            gt = gate_ref[:, cs].astype(F32)
            act = (gt * pl.reciprocal(1.0 + jnp.exp(-gt), approx=True) * up_ref[:, cs].astype(F32)).astype(BF16)
            act_ref[:, cs] = act
            acc = acc + _dot(act, w_ref[cs, :])
        x2_ref[...] = acc

    return _pc(body, name=name, grid=(S // tm,),
               in_specs=[BS((tm, D), lambda i: (i, 0)), BS((tm, D_FF), lambda i: (i, 0)),
                         BS((tm, D_FF), lambda i: (i, 1)), BS((None, D_FF, D), lambda i: (l, 0, 0))],
               out_specs=[BS((tm, D), lambda i: (i, 0)), BS((tm, D_FF), lambda i: (i, 0))],
               out_shape=[SDS((S, D), F32), SDS((S, D_FF), BF16)])(x1, gu, gu, wdown)


def _loss_head(x, g, tgt, name):
    S = x.shape[0]
    tm = min(512, S)

    def body(x_ref, g_ref, t_ref, dx_ref, dx16_ref, dg_ref, loss_ref):
        @pl.when(pl.program_id(0) == 0)
        def _():
            dg_ref[...] = jnp.zeros_like(dg_ref)
            loss_ref[...] = jnp.zeros_like(loss_ref)

        xv = x_ref[...]
        r = lax.rsqrt(jnp.mean(xv * xv, axis=-1, keepdims=True) + EPS)
        xh = xv * r
        gv = g_ref[...]
        diff = xh * gv - t_ref[...]
        loss_ref[...] += jnp.sum(diff * diff)
        dy = diff * (1.0 / D)
        dg_ref[...] += jnp.sum(dy * xh, axis=0, keepdims=True)
        dxh = dy * gv
        dx = r * (dxh - xh * jnp.mean(dxh * xh, axis=-1, keepdims=True))
        dx_ref[...] = dx
        dx16_ref[...] = dx.astype(BF16)

    row = BS((tm, D), lambda i: (i, 0))
    return _pc(body, name=name, grid=(S // tm,),
               in_specs=[row, BS((1, D), lambda i: (0, 0)), row],
               out_specs=[row, row, BS((8, D), lambda i: (0, 0)), BS((8, 128), lambda i: (0, 0))],
               out_shape=[SDS((S, D), F32), SDS((S, D), BF16), SDS((8, D), F32), SDS((8, 128), F32)])(x, g, tgt)


def _mm_tn(a, b, l, prev, name, split4=False):
    S, Ka = a.shape
    N = b.shape[1]
    if split4:
        ta, tn = _tile(Ka, 256), N // 4
        out_shape = SDS((2, 4, Ka, tn), F32)
        out_spec = BS((None, None, ta, tn), lambda j, i: (l, j, i, 0))
    else:
        ta, tn = _tile(Ka, 512), _tile(N, 1024)
        out_shape = SDS((2, Ka, N), F32)
        out_spec = BS((None, ta, tn), lambda j, i: (l, i, j))

    def body(a_ref, b_ref, *rest):
        rest[-1][...] = _dot_tn(a_ref[...], b_ref[...])

    in_specs = [BS((S, ta), lambda j, i: (0, i)), BS((S, tn), lambda j, i: (0, j))]
    args = [a, b]
    if prev is not None:
        in_specs.append(BS(memory_space=pl.ANY))
        args.append(prev)
    return _pc(body, name=name, grid=(N // tn, Ka // ta), in_specs=in_specs, out_specs=out_spec, out_shape=out_shape,
               aliases={2: 0} if prev is not None else None)(*args)


def _bwd_down(dx16, wdown, gu, l, name):
    S = dx16.shape[0]
    tm = min(256, S)

    def body(dx_ref, w_ref, gate_ref, up_ref, dgu_ref):
        dxv = dx_ref[...]
        for c0 in range(0, D_FF, 256):
            cs = slice(c0, c0 + 256)
            dact = _dot_nt(dxv, w_ref[cs, :])
            gt = gate_ref[:, cs].astype(F32)
            sg = pl.reciprocal(1.0 + jnp.exp(-gt), approx=True)
            dgu_ref[:, cs] = (dact * up_ref[:, cs].astype(F32) * (sg * (1.0 + gt * (1.0 - sg)))).astype(BF16)
            dgu_ref[:, D_FF + c0:D_FF + c0 + 256] = (dact * (gt * sg)).astype(BF16)

    return _pc(body, name=name, grid=(S // tm,),
               in_specs=[BS((tm, D), lambda i: (i, 0)), BS((None, D_FF, D), lambda i: (l, 0, 0)),
                         BS((tm, D_FF), lambda i: (i, 0)), BS((tm, D_FF), lambda i: (i, 1))],
               out_specs=BS((tm, 2 * D_FF), lambda i: (i, 0)),
               out_shape=SDS((S, 2 * D_FF), BF16))(dx16, wdown, gu, gu)


def _mm_nt_normbwd(dy, w4, l, x, g3, dres, ngroups, name, rider=None, w_t=False):
    S, K = dy.shape
    nk, kc = w4.shape[1], w4.shape[2 if w_t else 3]
    mm = _dot if w_t else _dot_nt
    tm = min(512, S)
    gw = D // ngroups
    has_res = dres is not None
    nr = rider.n if rider else 0
    n_in, n_out = 4 + has_res, 2 + has_res

    def body(*refs):
        dy_ref, w_ref, x_ref, g_ref = refs[:4]
        res_ref = refs[4] if has_res else None
        outs = refs[n_in + nr:n_in + nr + n_out]
        dx_ref, dg_ref = outs[0], outs[-1]
        dx16_ref = outs[1] if has_res else None
        r_io = (refs[n_in:n_in + nr], refs[n_in + nr + n_out:n_in + 2 * nr + n_out], refs[n_in + 2 * nr + n_out:])
        if rider:
            pl.when(pl.program_id(0) == 0)(lambda: rider.start(*r_io))

        @pl.when(pl.program_id(0) == 0)
        def _():
            dg_ref[...] = jnp.zeros_like(dg_ref)

        dh = mm(dy_ref[:, 0:kc], w_ref[0])
        for k in range(1, nk):
            dh = dh + mm(dy_ref[:, k * kc:(k + 1) * kc], w_ref[k])
        for gi in range(ngroups):
            sl = slice(gi * gw, (gi + 1) * gw)
            xg = x_ref[:, sl]
            r = lax.rsqrt(jnp.mean(xg * xg, axis=-1, keepdims=True) + EPS)
            xh = xg * r
            dhg = dh[:, sl]
            dg_ref[:, sl] += jnp.sum(dhg * xh, axis=0, keepdims=True)
            dxh = dhg * g_ref[:, sl]
            dxg = r * (dxh - xh * jnp.mean(dxh * xh, axis=-1, keepdims=True))
            if has_res:
                dxg = dxg + res_ref[:, sl]
                dx16_ref[:, sl] = dxg.astype(BF16)
            dx_ref[:, sl] = dxg
        if rider:
            pl.when(pl.program_id(0) == S // tm - 1)(lambda: rider.wait(*r_io))

    row = BS((tm, D), lambda i: (i, 0))
    in_specs = [BS((tm, K), lambda i: (i, 0)),
                BS((None,) + tuple(w4.shape[1:]), lambda i: (l, 0, 0, 0), pipeline_mode=pl.Buffered(1)), row,
                BS((None, 1, D), lambda i: (l, 0, 0))]
    args = [dy, w4, x, g3]
    out_specs, out_shape = [row], [SDS((S, D), F32)]
    if has_res:
        in_specs.append(row)
        args.append(dres)
        out_specs.append(row)
        out_shape.append(SDS((S, D), BF16))
    out_specs.append(BS((8, D), lambda i: (0, 0)))
    out_shape.append(SDS((8, D), F32))
    if not rider:
        return _pc(body, name=name, grid=(S // tm,), in_specs=in_specs, out_specs=out_specs, out_shape=out_shape)(*args)
    out = pl.pallas_call(body, name=name, grid=(S // tm,), in_specs=in_specs + [ANY] * nr,
                         out_specs=out_specs + [ANY] * nr, out_shape=out_shape + rider.out_shape,
                         scratch_shapes=rider.scratch(),
                         compiler_params=pltpu.CompilerParams(dimension_semantics=("arbitrary",),
                                                              vmem_limit_bytes=VMEM_LIMIT, has_side_effects=True))(
        *args, *rider.arrs)
    return (*out[:n_out], list(out[n_out:]))


def _rope(x, c, s1, s2):
    return x * c + pltpu.roll(x, 112, axis=1) * s1 + pltpu.roll(x, 16, axis=1) * s2


def _rope_t(dy, c, s1, s2):
    return dy * c + pltpu.roll(dy * s1, 16, axis=1) + pltpu.roll(dy * s2, 112, axis=1)


def _mla_prep(proj, gq3, gkv3, wuq, wk, wv, tabs, l, name):
    S = proj.shape[0]
    tm = min(512, S)
    tc, ts1, ts2 = tabs

    def body(cq_ref, ckv_ref, kr_ref, gq_ref, gkv_ref, wuq_ref, wk_ref, wv_ref, c_ref, s1_ref, s2_ref,
             q_ref, k_ref, v_ref, kt_ref, vt_ref):
        c, s1, s2 = c_ref[...], s1_ref[...], s2_ref[...]
        cq = cq_ref[...]
        rq = lax.rsqrt(jnp.mean(cq * cq, axis=-1, keepdims=True) + EPS)
        qa = _dot((cq * rq * gq_ref[...]).astype(BF16), wuq_ref[...])
        ckv = ckv_ref[...]
        rkv = lax.rsqrt(jnp.mean(ckv * ckv, axis=-1, keepdims=True) + EPS)
        ckvn = (ckv * rkv * gkv_ref[...]).astype(BF16)
        ka = _dot(ckvn, wk_ref[...])
        va = _dot(ckvn, wv_ref[...])
        v_ref[...] = va.astype(BF16)
        vt_ref[...] = va.T.astype(BF16)
        krr = _rope(kr_ref[...], c, s1, s2)
        for h in range(HEADS):
            sl = slice(h * 128, (h + 1) * 128)
            q_ref[:, sl] = (_rope(qa[:, sl], c, s1, s2) * QK_SCALE).astype(BF16)
            kh = ka[:, sl] + krr
            k_ref[:, sl] = kh.astype(BF16)
            kt_ref[sl, :] = kh.T.astype(BF16)

    lay = lambda a, b: BS((None, a, b), lambda i: (l, 0, 0))
    tab = BS((tm, 128), lambda i: (i, 0))
    return _pc(body, name=name, grid=(S // tm,),
               in_specs=[BS((tm, 256), lambda i: (i, 0)), BS((tm, 128), lambda i: (i, 2)), BS((tm, 128), lambda i: (i, 3)),
                         lay(1, 256), lay(1, 128), lay(256, 512), lay(128, 512), lay(128, 512), tab, tab, tab],
               out_specs=[BS((tm, 512), lambda i: (i, 0))] * 3 + [BS((512, tm), lambda i: (0, i))] * 2,
               out_shape=[SDS((S, 512), BF16)] * 3 + [SDS((512, S), BF16)] * 2)(
        proj, proj, proj, gq3, gkv3, wuq, wk, wv, tc, ts1, ts2)


def _mla_prep_bwd(dq, dk, dv, proj, gq3, gkv3, wuq, wk, wv, tabs, l, name):
    S = proj.shape[0]
    tm = min(512, S)
    tc, ts1, ts2 = tabs

    def body(dq_ref, dk_ref, dv_ref, cq_ref, ckv_ref, gq_ref, gkv_ref, wuq_ref, wk_ref, wv_ref, c_ref, s1_ref, s2_ref,
             dcq_ref, dckv_ref, dkr_ref, dwuq_ref, dwk_ref, dwv_ref, dgq_ref, dgkv_ref):
        @pl.when(pl.program_id(0) == 0)
        def _():
            for r in (dwuq_ref, dwk_ref, dwv_ref, dgq_ref, dgkv_ref):
                r[...] = jnp.zeros_like(r)

        c, s1, s2 = c_ref[...], s1_ref[...], s2_ref[...]
        dqp = jnp.concatenate(
            [_rope_t(dq_ref[h * 128:(h + 1) * 128, :].T * QK_SCALE, c, s1, s2) for h in range(HEADS)], axis=1).astype(BF16)
        cq = cq_ref[...]
        rq = lax.rsqrt(jnp.mean(cq * cq, axis=-1, keepdims=True) + EPS)
        cqh = cq * rq
        gq_v = gq_ref[...]
        dwuq_ref[...] += _dot_tn((cqh * gq_v).astype(BF16), dqp)
        dcqn = _dot_nt(dqp, wuq_ref[...])
        dgq_ref[...] += jnp.sum(dcqn * cqh, axis=0, keepdims=True)
        dxh = dcqn * gq_v
        dcq_ref[...] = (rq * (dxh - cqh * jnp.mean(dxh * cqh, axis=-1, keepdims=True))).astype(BF16)

        dkb = dk_ref[...].astype(BF16)
        dvb = dv_ref[...].astype(BF16)
        ckv = ckv_ref[...]
        rkv = lax.rsqrt(jnp.mean(ckv * ckv, axis=-1, keepdims=True) + EPS)
        ckh = ckv * rkv
        gkv_v = gkv_ref[...]
        ckvn = (ckh * gkv_v).astype(BF16)
        dwk_ref[...] += _dot_tn(ckvn, dkb)
        dwv_ref[...] += _dot_tn(ckvn, dvb)
        dckvn = _dot_nt(dkb, wk_ref[...]) + _dot_nt(dvb, wv_ref[...])
        dgkv_ref[...] += jnp.sum(dckvn * ckh, axis=0, keepdims=True)
        dyh = dckvn * gkv_v
        dckv_ref[...] = (rkv * (dyh - ckh * jnp.mean(dyh * ckh, axis=-1, keepdims=True))).astype(BF16)
        dks = dk_ref[:, 0:128] + dk_ref[:, 128:256] + dk_ref[:, 256:384] + dk_ref[:, 384:512]
        dkr_ref[...] = _rope_t(dks, c, s1, s2).astype(BF16)

    full = lambda a, b: BS((a, b), lambda i: (0, 0))
    lay = lambda a, b: BS((None, a, b), lambda i: (l, 0, 0))
    tab = BS((tm, 128), lambda i: (i, 0))
    row = lambda w: BS((tm, w), lambda i: (i, 0))
    return _pc(body, name=name, grid=(S // tm,),
               in_specs=[BS((512, tm), lambda i: (0, i)), row(512), row(512), BS((tm, 256), lambda i: (i, 0)),
                         BS((tm, 128), lambda i: (i, 2)),
                         lay(1, 256), lay(1, 128), lay(256, 512), lay(128, 512), lay(128, 512), tab, tab, tab],
               out_specs=[row(256), row(128), row(128), full(256, 512), full(128, 512), full(128, 512),
                          full(8, 256), full(8, 128)],
               out_shape=[SDS((S, 256), BF16), SDS((S, 128), BF16), SDS((S, 128), BF16), SDS((256, 512), F32),
                          SDS((128, 512), F32), SDS((128, 512), F32), SDS((8, 256), F32), SDS((8, 128), F32)])(
        dq, dk, dv, proj, proj, gq3, gkv3, wuq, wk, wv, tc, ts1, ts2)


def _causal_steps(n, q_outer):
    if q_outer:
        pairs = [(i, j) for i in range(n) for j in range(i + 1)]
    else:
        pairs = [(i, j) for j in range(n) for i in range(j, n)]
    return jnp.asarray([p[0] for p in pairs], jnp.int32), jnp.asarray([p[1] for p in pairs], jnp.int32)


def _mla_attn(q, k, vt, gts, layer, name):
    S = q.shape[0]
    t = min(512, S)
    n = S // t
    ng = len(gts)

    qi, kj = _causal_steps(n, True)
    last = qi.shape[0] - 1

    def body(qi_ref, kj_ref, q_ref, k_ref, vt_ref, *rest):
        (ya_ref, lse_ref), g_refs = rest[ng:ng + 2], rest[ng + 2:2 * ng + 2]
        m_sc, l_sc, acc_sc = rest[2 * ng + 2:2 * ng + 5]
        i, j = qi_ref[pl.program_id(1)], kj_ref[pl.program_id(1)]
        if ng:
            phases = _gather_phases(g_refs, [g.shape for g in gts], rest[2 * ng + 5], rest[2 * ng + 6], layer)
            for ph, (pp, ss) in zip(phases[:3], ((0, 0), (1, 0), (1, 2 * last // 3))):
                pl.when((pl.program_id(0) == pp) & (pl.program_id(1) == ss))(ph)

        @pl.when(j == 0)
        def _():
            m_sc[...] = jnp.full_like(m_sc, NEG_INF)
            l_sc[...] = jnp.zeros_like(l_sc)
            acc_sc[...] = jnp.zeros_like(acc_sc)

        def step(masked):
            for hh in range(2):
                sl = slice(hh * 128, (hh + 1) * 128)
                st = _dot_nt(k_ref[:, sl], q_ref[:, sl])
                if masked:
                    key = lax.broadcasted_iota(jnp.int32, (t, t), 0)
                    qry = lax.broadcasted_iota(jnp.int32, (t, t), 1)
                    st = jnp.where(key <= qry, st, NEG_INF)
                m_prev = m_sc[hh]
                m_new = jnp.maximum(m_prev, jnp.max(st, axis=0, keepdims=True))
                p = jnp.exp(st - m_new)
                alpha = jnp.exp(m_prev - m_new)
                l_sc[hh] = alpha * l_sc[hh] + jnp.sum(p, axis=0, keepdims=True)
                acc_sc[hh] = alpha * acc_sc[hh] + _dot(vt_ref[sl, :], p.astype(BF16))
                m_sc[hh] = m_new

        @pl.when(j < i)
        def _():
            step(False)

        @pl.when(j == i)
        def _():
            step(True)
            ya_ref[...] = (acc_sc[0] / l_sc[0] + acc_sc[1] / l_sc[1]).T
            for hh in range(2):
                lse_ref[hh] = m_sc[hh] + jnp.log(l_sc[hh])

        if ng:
            pl.when((pl.program_id(0) == 1) & (pl.program_id(1) == last))(phases[3])

    gs = pltpu.PrefetchScalarGridSpec(
        num_scalar_prefetch=2, grid=(2, qi.shape[0]),
        in_specs=[BS((t, 256), lambda p, s, qi, kj: (qi[s], p)), BS((t, 256), lambda p, s, qi, kj: (kj[s], p)),
                  BS((256, t), lambda p, s, qi, kj: (p, kj[s]))] + [ANY] * ng,
        out_specs=[BS((t, 128), lambda p, s, qi, kj: (qi[s], p)), BS((2, 1, t), lambda p, s, qi, kj: (p, 0, qi[s]))]
        + [ANY] * ng,
        scratch_shapes=[pltpu.VMEM((2, 1, t), F32), pltpu.VMEM((2, 1, t), F32), pltpu.VMEM((2, 128, t), F32)]
        + ([pltpu.SemaphoreType.DMA((7 * ng,)), pltpu.SemaphoreType.DMA((7 * ng,))] if ng else []))
    out = pl.pallas_call(body, name=name, grid_spec=gs,
                         out_shape=[SDS((S, 256), F32), SDS((HEADS, 1, S), F32)] + [SDS(g.shape, g.dtype) for g in gts],
                         input_output_aliases={5 + m: 2 + m for m in range(ng)},
                         compiler_params=pltpu.CompilerParams(dimension_semantics=("arbitrary",) * 2,
                                                              vmem_limit_bytes=VMEM_LIMIT, has_side_effects=bool(ng)))(
        qi, kj, q, k, vt, *gts)
    return out[0], out[1], list(out[2:])


def _mla_delta(dycat, ya, name):
    S = ya.shape[0]
    t = min(512, S)

    def body(do_ref, ya_ref, d_ref):
        prod = do_ref[...] * ya_ref[...]
        for p in range(2):
            pt = prod[:, p * 128:(p + 1) * 128].T
            d_ref[2 * p] = jnp.sum(pt[0:64, :], axis=0, keepdims=True)
            d_ref[2 * p + 1] = jnp.sum(pt[64:128, :], axis=0, keepdims=True)

    return _pc(body, name=name, grid=(S // t,),
               in_specs=[BS((t, 256), lambda i: (i, 0)), BS((t, 256), lambda i: (i, 0))],
               out_specs=BS((HEADS, 1, t), lambda i: (0, 0, i)), out_shape=SDS((HEADS, 1, S), F32))(dycat, ya)


def _mla_attn_bwd(q, k, kt, v, dya, lse, delta, rider, name):
    S = q.shape[0]
    t = min(512, S)
    n = S // t
    nr = rider.n if rider else 0

    qi, kj = _causal_steps(n, False)
    last = qi.shape[0] - 1

    def body(qi_ref, kj_ref, q_ref, k_ref, kt_ref, v_ref, do_ref, lse_ref, delta_ref, *rest):
        dqt_ref, dk_ref, dv_ref = rest[nr:nr + 3]
        r_io = (rest[:nr], rest[nr + 3:2 * nr + 3], rest[2 * nr + 3:])
        i, j = qi_ref[pl.program_id(1)], kj_ref[pl.program_id(1)]
        if rider:
            pl.when((pl.program_id(0) == 0) & (pl.program_id(1) == 0))(lambda: rider.start(*r_io))

        @pl.when(pl.program_id(1) == 0)
        def _():
            dqt_ref[...] = jnp.zeros_like(dqt_ref)

        @pl.when(i == j)
        def _():
            dk_ref[...] = jnp.zeros_like(dk_ref)
            dv_ref[...] = jnp.zeros_like(dv_ref)

        def step(masked):
            dob = do_ref[...].astype(BF16)
            cols = pl.ds(pl.multiple_of(i * t, t), t)
            for hh in range(2):
                sl = slice(hh * 128, (hh + 1) * 128)
                qv = q_ref[:, sl]
                p = jnp.exp(_dot_nt(k_ref[:, sl], qv) - lse_ref[hh])
                if masked:
                    key = lax.broadcasted_iota(jnp.int32, (t, t), 0)
                    qry = lax.broadcasted_iota(jnp.int32, (t, t), 1)
                    p = jnp.where(key <= qry, p, 0.0)
                dv_ref[:, sl] += _dot(p.astype(BF16), dob)
                ds = (p * (_dot_nt(v_ref[:, sl], dob) - delta_ref[hh])).astype(BF16)
                dk_ref[:, sl] += _dot(ds, qv)
                dqt_ref[sl, cols] += _dot(kt_ref[sl, :], ds)

        @pl.when(i > j)
        def _():
            step(False)

        @pl.when(i == j)
        def _():
            step(True)

        if rider:
            pl.when((pl.program_id(0) == 1) & (pl.program_id(1) == last))(lambda: rider.wait(*r_io))

    qs = BS((t, 256), lambda p, s, qi, kj: (qi[s], p))
    ks = BS((t, 256), lambda p, s, qi, kj: (kj[s], p))
    rowv = BS((2, 1, t), lambda p, s, qi, kj: (p, 0, qi[s]))
    gs = pltpu.PrefetchScalarGridSpec(
        num_scalar_prefetch=2, grid=(2, qi.shape[0]),
        in_specs=[qs, ks, BS((256, t), lambda p, s, qi, kj: (p, kj[s])), ks,
                  BS((t, 128), lambda p, s, qi, kj: (qi[s], p)), rowv, rowv] + [ANY] * nr,
        out_specs=[BS((256, S), lambda p, s, qi, kj: (p, 0)), ks, ks] + [ANY] * nr,
        scratch_shapes=rider.scratch() if rider else [])
    out = pl.pallas_call(body, name=name, grid_spec=gs,
                         out_shape=[SDS((512, S), F32), SDS((S, 512), F32), SDS((S, 512), F32)]
                         + (rider.out_shape if rider else []),
                         compiler_params=pltpu.CompilerParams(dimension_semantics=("arbitrary",) * 2,
                                                              vmem_limit_bytes=VMEM_LIMIT, has_side_effects=bool(rider)))(
        qi, kj, q, k, kt, v, dya, lse, delta, *(rider.arrs if rider else []))
    return out[0], out[1], out[2], list(out[3:])


def _swa_scores(qm, kk, valid, bias, sink):
    sc = jnp.where(valid, _dot_nt(qm, kk) * SWA_SCALE + bias, NEG_INF)
    m = jnp.maximum(jnp.max(sc, axis=-1, keepdims=True), sink)
    e = jnp.exp(sc - m)
    esink = jnp.exp(sink - m)
    den = jnp.sum(e, axis=-1, keepdims=True) + esink
    return e / den, esink / den


def _swa_consts(sink_ref, l):
    rows = HEADS * BLK
    r = lax.broadcasted_iota(jnp.int32, (rows, 2 * BLK), 0)
    c = lax.broadcasted_iota(jnp.int32, (rows, 2 * BLK), 1)
    dist = (r & (BLK - 1)) + BLK - c
    head = lax.broadcasted_iota(jnp.int32, (rows, 1), 0) // BLK

    def per_head(vals):
        return jnp.where(head == 0, vals[0], jnp.where(head == 1, vals[1], jnp.where(head == 2, vals[2], vals[3])))

    bias = -per_head(SLOPES) * dist.astype(F32)
    sink = per_head([sink_ref[l, h] for h in range(HEADS)])
    return (dist >= 0) & (dist < SWA_WINDOW), c >= BLK, bias, sink, head


def _to_half(xb, pos, b):
    return xb if pos == b else pltpu.roll(xb, 64, axis=1)


def _swa_stack(ref, st, lo):
    parts = []
    for b in range(2):
        xb = ref[pl.ds(st, BLK), b * 128:(b + 1) * 128]
        half = lo if b == 0 else ~lo
        parts += [jnp.where(half, _to_half(xb, pos, b), 0.0).astype(BF16) for pos in range(2)]
    return jnp.concatenate(parts, axis=0)


def _swa_unstack(x_all, lo):
    blocks = []
    for b in range(2):
        h0, h1 = (_to_half(x_all[(2 * b + pos) * BLK:(2 * b + pos + 1) * BLK], pos, b) for pos in range(2))
        blocks.append(jnp.where(lo, h0, h1))
    return blocks


def _swa(proj, sinks, l, name):
    S = proj.shape[0]
    nb = S // BLK

    def body(q_ref, k_ref, v_ref, sink_ref, o_ref, kp, vp):
        kp[0:BLK, :] = jnp.zeros((BLK, 128), BF16)
        vp[0:BLK, :] = jnp.zeros((BLK, 128), BF16)
        kp[BLK:, :] = k_ref[...].astype(BF16)
        vp[BLK:, :] = v_ref[...].astype(BF16)
        lo = lax.broadcasted_iota(jnp.int32, (BLK, 128), 1) < 64
        band, cur, bias, sink, _ = _swa_consts(sink_ref, l)

        def blk(i, carry):
            st = pl.multiple_of(i * BLK, BLK)
            kk = kp[pl.ds(st, 2 * BLK), :]
            vv = vp[pl.ds(st, 2 * BLK), :]
            p, _ = _swa_scores(_swa_stack(q_ref, st, lo), kk, band & (cur | (i > 0)), bias, sink)
            for b, ob in enumerate(_swa_unstack(_dot(p.astype(BF16), vv), lo)):
                o_ref[pl.ds(st, BLK), b * 128:(b + 1) * 128] = ob
            return carry

        lax.fori_loop(0, nb, blk, 0, unroll=2)

    return _pc(body, name=name, grid=(1,),
               in_specs=[BS((S, 256), lambda i: (0, C_QS // 256)), BS((S, 128), lambda i: (0, C_KS // 128)),
                         BS((S, 128), lambda i: (0, C_VS // 128)), BS(memory_space=pltpu.SMEM)],
               out_specs=BS((S, 256), lambda i: (0, 0)),
               out_shape=SDS((S, 256), F32),
               scratch=[pltpu.VMEM((S + BLK, 128), BF16), pltpu.VMEM((S + BLK, 128), BF16)])(proj, proj, proj, sinks)


def _swa_bwd(proj, sinks, dyd, l, name, rider=None):
    S = proj.shape[0]
    nb = S // BLK
    nr = rider.n if rider else 0

    def body(q_ref, k_ref, v_ref, sink_ref, do_ref, *rest):
        dq_ref, dk_ref, dv_ref, dsink_ref = rest[nr:nr + 4]
        kp, vp, dkp, dvp = rest[2 * nr + 4:2 * nr + 8]
        r_io = (rest[:nr], rest[nr + 4:2 * nr + 4], rest[2 * nr + 8:])
        if rider:
            rider.start(*r_io)
        kp[0:BLK, :] = jnp.zeros((BLK, 128), BF16)
        vp[0:BLK, :] = jnp.zeros((BLK, 128), BF16)
        kp[BLK:, :] = k_ref[...].astype(BF16)
        vp[BLK:, :] = v_ref[...].astype(BF16)
        dkp[...] = jnp.zeros_like(dkp)
        dvp[...] = jnp.zeros_like(dvp)
        lo = lax.broadcasted_iota(jnp.int32, (BLK, 128), 1) < 64
        lane8 = lax.broadcasted_iota(jnp.int32, (8, 128), 1)
        band, cur, bias, sink, head = _swa_consts(sink_ref, l)

        def blk(i, dsink):
            st = pl.multiple_of(i * BLK, BLK)
            kk = kp[pl.ds(st, 2 * BLK), :]
            vv = vp[pl.ds(st, 2 * BLK), :]
            qm, dom = _swa_stack(q_ref, st, lo), _swa_stack(do_ref, st, lo)
            p, psink = _swa_scores(qm, kk, band & (cur | (i > 0)), bias, sink)
            dp = _dot_nt(dom, vv)
            dvp[pl.ds(st, 2 * BLK), :] += _dot_tn(p.astype(BF16), dom)
            delta = jnp.sum(p * dp, axis=-1, keepdims=True)
            dsk = -psink * delta
            for h in range(HEADS):
                dsink = dsink + jnp.where(lane8 == h, jnp.sum(jnp.where(head == h, dsk, 0.0)), 0.0)
            dsc = (p * (dp - delta) * SWA_SCALE).astype(BF16)
            for b, dqb in enumerate(_swa_unstack(_dot(dsc, kk), lo)):
                dq_ref[pl.ds(st, BLK), b * 128:(b + 1) * 128] = dqb.astype(BF16)
            dkp[pl.ds(st, 2 * BLK), :] += _dot_tn(dsc, qm)
            return dsink

        dsink_ref[...] = lax.fori_loop(0, nb, blk, jnp.zeros((8, 128), F32), unroll=2)
        dk_ref[...] = dkp[BLK:, :].astype(BF16)
        dv_ref[...] = dvp[BLK:, :].astype(BF16)
        if rider:
            rider.wait(*r_io)

    in_specs = [BS((S, 256), lambda i: (0, C_QS // 256)), BS((S, 128), lambda i: (0, C_KS // 128)),
                BS((S, 128), lambda i: (0, C_VS // 128)), BS(memory_space=pltpu.SMEM), BS((S, 256), lambda i: (0, 3))]
    out_specs = [BS((S, 256), lambda i: (0, 0)), BS((S, 128), lambda i: (0, 0)), BS((S, 128), lambda i: (0, 0)),
                 BS((8, 128), lambda i: (0, 0))]
    out_shape = [SDS((S, 256), BF16), SDS((S, 128), BF16), SDS((S, 128), BF16), SDS((8, 128), F32)]
    scratch = [pltpu.VMEM((S + BLK, 128), BF16), pltpu.VMEM((S + BLK, 128), BF16),
               pltpu.VMEM((S + BLK, 128), F32), pltpu.VMEM((S + BLK, 128), F32)]
    if not rider:
        return _pc(body, name=name, grid=(1,), in_specs=in_specs, out_specs=out_specs, out_shape=out_shape,
                   scratch=scratch)(proj, proj, proj, sinks, dyd)
    out = pl.pallas_call(body, name=name, grid=(1,), in_specs=in_specs + [ANY] * nr, out_specs=out_specs + [ANY] * nr,
                         out_shape=out_shape + rider.out_shape, scratch_shapes=scratch + rider.scratch(),
                         compiler_params=pltpu.CompilerParams(dimension_semantics=("arbitrary",),
                                                              vmem_limit_bytes=VMEM_LIMIT, has_side_effects=True))(
        proj, proj, proj, sinks, dyd, *rider.arrs)
    return (*out[:4], list(out[4:]))


def _down(x, k, t):
    return jnp.where(t >= k, pltpu.roll(x, k, axis=0), 0.0)


def _up(x, k, t):
    n = x.shape[0]
    return jnp.where(t < n - k, pltpu.roll(x, n - k, axis=0), 0.0)


def _conv(proj, w8, l, name):
    S = proj.shape[0]

    def body(gb_ref, gc_ref, u_ref, w_ref, y_ref):
        t = lax.broadcasted_iota(jnp.int32, (S, 128), 0)
        z = gc_ref[...] * u_ref[...]
        c = w_ref[2:3, :] * z + w_ref[1:2, :] * _down(z, 1, t) + w_ref[0:1, :] * _down(z, 2, t)
        y_ref[...] = gb_ref[...] * c

    col = lambda c0: BS((S, 128), lambda i: (0, c0 // 128 + i))
    return _pc(body, name=name, grid=(2,),
               in_specs=[col(C_GB), col(C_GC), col(C_UC), BS((None, 8, 128), lambda i: (l, 0, i))],
               out_specs=BS((S, 128), lambda i: (0, i)), out_shape=SDS((S, 256), F32))(proj, proj, proj, w8)


def _conv_bwd(proj, w8, dycat, l, name):
    S = proj.shape[0]

    def body(gb_ref, gc_ref, u_ref, w_ref, dy_ref, dgb_ref, dgc_ref, du_ref, dw_ref):
        t = lax.broadcasted_iota(jnp.int32, (S, 128), 0)
        gc, u = gc_ref[...], u_ref[...]
        z = gc * u
        z1, z2 = _down(z, 1, t), _down(z, 2, t)
        w0, w1, w2 = w_ref[0:1, :], w_ref[1:2, :], w_ref[2:3, :]
        dy = dy_ref[...]
        dgb_ref[...] = (dy * (w2 * z + w1 * z1 + w0 * z2)).astype(BF16)
        dc = dy * gb_ref[...]
        dz = w2 * dc + w1 * _up(dc, 1, t) + w0 * _up(dc, 2, t)
        dgc_ref[...] = (dz * u).astype(BF16)
        du_ref[...] = (dz * gc).astype(BF16)
        row = lax.broadcasted_iota(jnp.int32, (8, 128), 0)
        sums = [jnp.sum(dc * zz, axis=0, keepdims=True) for zz in (z2, z1, z)]
        dw_ref[...] = jnp.where(row == 0, sums[0], jnp.where(row == 1, sums[1], jnp.where(row == 2, sums[2], 0.0)))

    col = lambda c0: BS((S, 128), lambda i: (0, c0 // 128 + i))
    out = BS((S, 128), lambda i: (0, i))
    return _pc(body, name=name, grid=(2,),
               in_specs=[col(C_GB), col(C_GC), col(C_UC), BS((None, 8, 128), lambda i: (l, 0, i)), col(256)],
               out_specs=[out, out, out, BS((8, 128), lambda i: (0, i))],
               out_shape=[SDS((S, 256), BF16)] * 3 + [SDS((8, 256), F32)])(proj, proj, proj, w8, dycat)


def _pool_parts(u, t, first):
    lo = lax.broadcasted_iota(jnp.int32, u.shape, 1) < 64
    s2 = u + _down(u, 1, t)
    s4 = s2 + _down(s2, 2, t)
    s8 = s4 + _down(s4, 4, t)
    s16 = s8 + _down(s8, 8, t)
    win = jnp.where(lo, jnp.where(first, s2, s8), jnp.where(first, s4, s16))
    wv = jnp.where(lo, jnp.where(first, 2, 8), jnp.where(first, 4, 16))
    cnt = jnp.minimum(t + 1, wv).astype(F32)
    return win, cnt, lo


def _pool(proj, pwd, scale3, l, name):
    S = proj.shape[0]

    def body(u_ref, pw_ref, sc_ref, y_ref):
        t = lax.broadcasted_iota(jnp.int32, (S, 128), 0)
        u = u_ref[...]
        win, cnt, _ = _pool_parts(u, t, pl.program_id(0) == 0)
        pooled = win / cnt - u
        y_ref[...] = _dot(pooled.astype(BF16), pw_ref[...]) * sc_ref[...]

    return _pc(body, name=name, grid=(2,),
               in_specs=[BS((S, 128), lambda i: (0, C_UP // 128 + i)), BS((None, 128, 128), lambda i: (l, i, 0)),
                         BS((None, 1, 128), lambda i: (l, 0, i))],
               out_specs=BS((S, 128), lambda i: (0, i)), out_shape=SDS((S, 256), F32))(proj, pwd, scale3)


def _pool_bwd(proj, pwd, scale3, dycat, l, name):
    S = proj.shape[0]

    def body(u_ref, pw_ref, sc_ref, dy_ref, du_ref, dpw_ref, dsc_ref):
        t = lax.broadcasted_iota(jnp.int32, (S, 128), 0)
        first = pl.program_id(0) == 0
        u = u_ref[...]
        win, cnt, lo = _pool_parts(u, t, first)
        pooled = (win / cnt - u).astype(BF16)
        pw = pw_ref[...]
        dy = dy_ref[...]
        dsc_ref[...] = jnp.broadcast_to(jnp.sum(dy * _dot(pooled, pw), axis=0, keepdims=True), (8, 128))
        dmb = (dy * sc_ref[...]).astype(BF16)
        dpw_ref[...] = _dot_tn(pooled, dmb)
        dpooled = _dot_nt(dmb, pw)
        a1 = dpooled / cnt
        a2 = a1 + _up(a1, 1, t)
        a4 = a2 + _up(a2, 2, t)
        a8 = a4 + _up(a4, 4, t)
        a16 = a8 + _up(a8, 8, t)
        dwin = jnp.where(lo, jnp.where(first, a2, a8), jnp.where(first, a4, a16))
        du_ref[...] = (dwin - dpooled).astype(BF16)

    return _pc(body, name=name, grid=(2,),
               in_specs=[BS((S, 128), lambda i: (0, C_UP // 128 + i)), BS((None, 128, 128), lambda i: (l, i, 0)),
                         BS((None, 1, 128), lambda i: (l, 0, i)), BS((S, 128), lambda i: (0, 4 + i))],
               out_specs=[BS((S, 128), lambda i: (0, i)), BS((128, 128), lambda i: (i, 0)), BS((8, 128), lambda i: (0, i))],
               out_shape=[SDS((S, 256), BF16), SDS((256, 128), F32), SDS((8, 256), F32)])(proj, pwd, scale3, dycat)


def _adamw(w, g, m, v, name, echo=False):
    n, a, b = w.shape
    tr = _row_tile(a, b)

    def body(w_ref, g_ref, m_ref, v_ref, d_ref, nm_ref, nv_ref, *g_out):
        gv = g_ref[...]
        if echo:
            g_out[0][...] = gv
        m_new = B1 * m_ref[...] + (1.0 - B1) * gv
        v_new = B2 * v_ref[...] + (1.0 - B2) * (gv * gv)
        m_hat = m_new / (1.0 - B1 ** STEP)
        v_hat = v_new / (1.0 - B2 ** STEP)
        d_ref[...] = -LR * (m_hat / (jnp.sqrt(v_hat) + ADAM_EPS) + WD * w_ref[...])
        nm_ref[...] = m_new
        nv_ref[...] = v_new

    sp = BS((None, tr, b), lambda i, t: (i, t, 0))
    return _pc(body, name=name, grid=(n, a // tr), in_specs=[sp] * 4, out_specs=[sp] * (3 + echo),
               out_shape=[SDS((n, a, b), F32)] * (3 + echo))(w, g, m, v)


def _prefetch_call(body, name, grid, in_specs, out_specs, out_shape):
    gs = pltpu.PrefetchScalarGridSpec(num_scalar_prefetch=1, grid=grid, in_specs=in_specs, out_specs=out_specs)
    return pl.pallas_call(body, name=name, grid_spec=gs, out_shape=out_shape, compiler_params=_params(len(grid)))


def _place(w, kc, dtype, name):
    _, a, b = w.shape

    def body(kc_ref, w_ref, o_ref):
        o_ref[...] = w_ref[...].astype(dtype)

    return _prefetch_call(body, name, (2,), [BS((None, a, b), lambda l, kc: (l, 0, 0))],
                          BS((None, None, a, b), lambda l, kc: (l, kc[0], 0, 0)), SDS((2, 4, a, b), dtype))(kc, w)


def _pair_sum(g, got, kc, name):
    _, _, a, b = g.shape
    tr = _row_tile(a, b)

    def body(kc_ref, a_ref, b_ref, t32_ref, t16_ref):
        s = a_ref[...] + b_ref[...]
        t16_ref[...] = s.astype(BF16)

        @pl.when(pl.program_id(1) == kc_ref[0])
        def _():
            t32_ref[...] = s

    sp = BS((None, tr, b), lambda t, k, kc: (k, t, 0))
    return _prefetch_call(body, name, (a // tr, 4),
                          [BS((None, None, tr, b), lambda t, k, kc: (kc[1], k, t, 0)), sp],
                          [BS((tr, b), lambda t, k, kc: (t, 0)), sp],
                          [SDS((a, b), F32), SDS((4, a, b), BF16)])(kc, g, got)


def _chip_sum(t32, gots, kc, name, after=None):
    a, b = t32.shape
    tr = _row_tile(a, b)
    ng = len(gots)

    def body(kc_ref, a_ref, *refs):
        acc = a_ref[...]
        for g_ref in refs[:ng]:
            for i in range(g_ref.shape[0]):
                acc = acc + g_ref[i].astype(F32)
        refs[-1][...] = acc

    extra = [] if after is None else [after]
    return _prefetch_call(body, name, (a // tr,),
                          [BS((tr, b), lambda t, kc: (t, 0))]
                          + [BS((g.shape[0], tr, b), lambda t, kc: (0, t, 0)) for g in gots]
                          + [BS((8, 128), lambda t, kc: (0, 0)) for _ in extra],
                          BS((None, tr, b), lambda t, kc: (kc[1], t, 0)), SDS((2, a, b), F32))(kc, t32, *gots, *extra)


def _me():
    return lax.axis_index("x"), lax.axis_index("y"), lax.axis_index("c")


def _other_chips(x, y):
    return [(1 - x, y), (x, 1 - y), (1 - x, 1 - y)]


ANY = BS(memory_space=pl.ANY)
COMM_PARAMS = pltpu.CompilerParams(has_side_effects=True)


def _gather(arrs, name):
    n = len(arrs)

    def body(*refs):
        for phase in _gather_phases(refs[n:2 * n], [a.shape for a in arrs], refs[2 * n], refs[2 * n + 1]):
            phase()

    return pl.pallas_call(body, name=name, out_shape=[SDS(a.shape, a.dtype) for a in arrs],
                          in_specs=[ANY] * n, out_specs=[ANY] * n, input_output_aliases={t: t for t in range(n)},
                          scratch_shapes=[pltpu.SemaphoreType.DMA((7 * n,)), pltpu.SemaphoreType.DMA((7 * n,))],
                          compiler_params=COMM_PARAMS)(*arrs)


def _gather_phases(outs, shapes, send_sems, recv_sems, layer=None):
    n = len(outs)
    cut = [s[2] // 2 // 16 * 16 for s in shapes]
    split = [r > 0 for r in cut]

    def plan():
        x, y, c = _me()
        return (c if layer is None else layer), (x, y), (x, y, c), (x, y, 1 - c), _other_chips(x, y)

    def role(moving, fn):
        if layer is None:
            fn()
        else:
            c = lax.axis_index("c")
            pl.when((c == layer) if moving else (c != layer))(fn)

    def blk(t, chip, layer, half=None):
        r = outs[t].at[layer, 2 * chip[0] + chip[1]]
        if half is None:
            return r
        return r.at[pl.ds(0, cut[t])] if half == 0 else r.at[pl.ds(cut[t], shapes[t][2] - cut[t])]

    def copy(t, k, ref, to):
        return pltpu.make_async_remote_copy(src_ref=ref, dst_ref=ref, send_sem=send_sems.at[7 * t + k],
                                            recv_sem=recv_sems.at[7 * t + k], device_id=to, device_id_type=MESH)

    def own_sends(t):
        c, chip, me, sib, (xn, yn, dg) = plan()
        cps = [copy(t, 0, blk(t, chip, c), (*xn, c)), copy(t, 1, blk(t, chip, c), (*yn, c))]
        return cps if split[t] else cps + [copy(t, 2, blk(t, chip, c), (*dg, c))]

    def relays(t):
        c, chip, me, sib, (xn, yn, dg) = plan()
        after_x = [copy(t, 4, blk(t, xn, c), sib)]
        after_y = [copy(t, 5, blk(t, yn, c), sib)]
        if split[t]:
            after_x.insert(0, copy(t, 2, blk(t, xn, c, 0), (*yn, c)))
            after_y.insert(0, copy(t, 3, blk(t, yn, c, 1), (*xn, c)))
        return after_x, after_y, [copy(t, 6, blk(t, dg, c), sib)]

    def send_own():
        for t in range(n):
            for cp in own_sends(t):
                cp.start()

    def relay_neighbours():
        c, chip, me, sib, (xn, yn, dg) = plan()
        for t in range(n):
            after_x, after_y, _ = relays(t)
            copy(t, 0, blk(t, xn, c), me).wait_recv()
            for cp in after_x:
                cp.start()
            copy(t, 1, blk(t, yn, c), me).wait_recv()
            for cp in after_y:
                cp.start()

    def relay_diagonal():
        c, chip, me, sib, (xn, yn, dg) = plan()
        for t in range(n):
            if split[t]:
                copy(t, 2, blk(t, dg, c, 0), me).wait_recv()
                copy(t, 3, blk(t, dg, c, 1), me).wait_recv()
            else:
                copy(t, 2, blk(t, dg, c), me).wait_recv()
            relays(t)[2][0].start()

    def take_sibling():
        _, chip, me, sib, (xn, yn, dg) = plan()
        theirs = 1 - lax.axis_index("c") if layer is None else layer
        for t in range(n):
            for k, peer in ((4, xn), (5, yn), (6, dg)):
                copy(t, k, blk(t, peer, theirs), me).wait_recv()

    def drain_sends():
        for t in range(n):
            after_x, after_y, after_d = relays(t)
            for cp in own_sends(t) + after_x + after_y + after_d:
                cp.wait_send()

    def finish():
        role(False, take_sibling)
        role(True, drain_sends)

    return ((lambda: role(True, send_own)), (lambda: role(True, relay_neighbours)),
            (lambda: role(True, relay_diagonal)), finish)


def _swap_copies(ins, outs, send_sems, recv_sems):
    x, y, c = _me()
    return [pltpu.make_async_remote_copy(src_ref=ins[t].at[1 - c], dst_ref=outs[t], send_sem=send_sems.at[t],
                                         recv_sem=recv_sems.at[t], device_id=(x, y, 1 - c), device_id_type=MESH)
            for t in range(len(ins))]


def _exchange_copies(peers, ins, outs, send_sems, recv_sems):
    x, y, c = _me()
    chips = _other_chips(x, y)
    n = len(peers)
    return [pltpu.make_async_remote_copy(src_ref=ins[t].at[2 * chips[j][0] + chips[j][1]], dst_ref=outs[t].at[i],
                                         send_sem=send_sems.at[n * t + i], recv_sem=recv_sems.at[n * t + i],
                                         device_id=(*chips[j], c), device_id_type=MESH)
            for i, j in enumerate(peers) for t in range(len(ins))]


class _Rider:
    def __init__(self, arrs, out_shape, nsem, copies):
        self.arrs, self.out_shape, self.nsem, self.copies = list(arrs), out_shape, nsem, copies
        self.n = len(self.arrs)

    def scratch(self):
        return [pltpu.SemaphoreType.DMA((self.nsem,)), pltpu.SemaphoreType.DMA((self.nsem,))]

    def start(self, ins, outs, sems):
        for cp in self.copies(ins, outs, *sems):
            cp.start()

    def wait(self, ins, outs, sems):
        for cp in self.copies(ins, outs, *sems):
            cp.wait()


def _swap_rider(gs):
    return _Rider(gs, [SDS(g.shape[1:], g.dtype) for g in gs], len(gs), _swap_copies)


def _exchange_rider(ts, peers=(0, 1, 2)):
    return _Rider(ts, [SDS((len(peers),) + t.shape[1:], t.dtype) for t in ts], len(peers) * len(ts),
                  functools.partial(_exchange_copies, peers))


HBM = BS(memory_space=pltpu.HBM)
SEM = BS(memory_space=pltpu.SEMAPHORE)
SPLIT_PARAMS = pltpu.CompilerParams(has_side_effects=pltpu.SideEffectType.DATAFLOW_SIDE_EFFECTING)


def _exchange_start(ts, name):
    n = len(ts)
    lands = [lax.empty((3,) + t.shape[1:], t.dtype) for t in ts]

    def body(*refs):
        for cp in _exchange_copies((0, 1, 2), refs[:n], refs[n:2 * n], refs[2 * n], refs[2 * n + 1]):
            cp.start()
        refs[-1][...] = jnp.zeros_like(refs[-1])

    held = [pltpu.with_memory_space_constraint(a, pltpu.HBM) for a in list(ts) + lands]
    out = pl.pallas_call(
        body, name=name,
        out_shape=(pltpu.SemaphoreType.DMA((3 * n,)), pltpu.SemaphoreType.DMA((3 * n,)),
                   *[pltpu.HBM(a.shape, a.dtype) for a in held], SDS((8, 128), F32)),
        in_specs=[HBM] * (2 * n), out_specs=(SEM, SEM, *[HBM] * (2 * n), BS(memory_space=pltpu.VMEM)),
        input_output_aliases={i: 2 + i for i in range(2 * n)}, compiler_params=SPLIT_PARAMS)(*held)
    return out[0], out[1], list(out[2:2 + n]), list(out[2 + n:2 + 2 * n]), out[-1]


def _exchange_wait(send_sems, recv_sems, ts, lands, after, name):
    n = len(ts)

    def body(*refs):
        for cp in _exchange_copies((0, 1, 2), refs[:n], refs[n:2 * n], refs[2 * n], refs[2 * n + 1]):
            cp.wait_send()
            cp.wait_recv()

    out = pl.pallas_call(
        body, name=name, out_shape=tuple(pltpu.HBM(a.shape, a.dtype) for a in ts + lands),
        in_specs=[HBM] * (2 * n) + [SEM, SEM, ANY], out_specs=[HBM] * (2 * n),
        input_output_aliases={i: i for i in range(2 * n)}, compiler_params=SPLIT_PARAMS)(
        *ts, *lands, send_sems, recv_sems, after)
    return list(out[n:2 * n])


def _ride_alone(rider, name):
    n = rider.n

    def body(*refs):
        rider.start(refs[:n], refs[n:2 * n], refs[2 * n:])
        rider.wait(refs[:n], refs[n:2 * n], refs[2 * n:])

    return pl.pallas_call(body, name=name, out_shape=rider.out_shape, in_specs=[ANY] * n, out_specs=[ANY] * n,
                          scratch_shapes=rider.scratch(), compiler_params=COMM_PARAMS)(*rider.arrs)


def _join_layers(us, name):
    n = len(us)

    def body(*refs):
        outs, send_sems, recv_sems = refs[n:2 * n], refs[2 * n], refs[2 * n + 1]
        x, y, c = _me()
        cps = [pltpu.make_async_remote_copy(src_ref=outs[t].at[c], dst_ref=outs[t].at[c], send_sem=send_sems.at[t],
                                            recv_sem=recv_sems.at[t], device_id=(x, y, 1 - c), device_id_type=MESH)
               for t in range(n)]
        for cp in cps:
            cp.start()
        for cp in cps:
            cp.wait()

    return pl.pallas_call(body, name=name, out_shape=[SDS(u.shape, u.dtype) for u in us],
                          in_specs=[ANY] * n, out_specs=[ANY] * n, input_output_aliases={t: t for t in range(n)},
                          scratch_shapes=[pltpu.SemaphoreType.DMA((n,)), pltpu.SemaphoreType.DMA((n,))],
                          compiler_params=COMM_PARAMS)(*us)


def _allsum_small(v, name, rider=None):
    M = v.shape[0]
    nr = rider.n if rider else 0

    def body(x_ref, *rest):
        o_ref = rest[nr]
        all_ref, send_sems, recv_sems, local_sem = rest[2 * nr + 1:2 * nr + 5]
        r_io = (rest[:nr], rest[nr + 1:2 * nr + 1], rest[2 * nr + 5:])
        x, y, c = _me()
        me, sib = (x, y, c), (x, y, 1 - c)
        chips = _other_chips(x, y)

        def rows(px, py, pc):
            return all_ref.at[pl.ds((4 * px + 2 * py + pc) * M, M), :]

        def copy(k, block, to, src=None):
            return pltpu.make_async_remote_copy(src_ref=rows(*block) if src is None else src, dst_ref=rows(*block),
                                                send_sem=send_sems.at[k], recv_sem=recv_sems.at[k],
                                                device_id=to, device_id_type=MESH)

        mine = pltpu.make_async_copy(x_ref, rows(*me), local_sem)
        mine.start()
        first = [copy(0, me, sib, src=x_ref)]
        first += [copy(1 + j, me, (*chip, c), src=x_ref) for j, chip in enumerate(chips)]
        for cp in first:
            cp.start()
        if rider:
            rider.start(*r_io)
        passed = [copy(4 + j, (*chip, c), sib) for j, chip in enumerate(chips)]
        for j, chip in enumerate(chips):
            copy(1 + j, (*chip, c), me).wait_recv()
            passed[j].start()
        copy(0, sib, me).wait_recv()
        for j, chip in enumerate(chips):
            copy(4 + j, (*chip, 1 - c), me).wait_recv()
        for cp in first + passed:
            cp.wait_send()
        mine.wait()
        acc = all_ref[0:M, :]
        for d in range(1, 8):
            acc = acc + all_ref[d * M:(d + 1) * M, :]
        o_ref[...] = acc
        if rider:
            rider.wait(*r_io)

    vm = BS(memory_space=pltpu.VMEM)
    out = pl.pallas_call(body, name=name, out_shape=[SDS((M, LANES), F32)] + (rider.out_shape if rider else []),
                         in_specs=[vm] + [ANY] * nr, out_specs=[vm] + [ANY] * nr,
                         scratch_shapes=[pltpu.VMEM((8 * M, LANES), F32), pltpu.SemaphoreType.DMA((7,)),
                                         pltpu.SemaphoreType.DMA((7,)), pltpu.SemaphoreType.DMA]
                         + (rider.scratch() if rider else []),
                         compiler_params=pltpu.CompilerParams(has_side_effects=True, vmem_limit_bytes=VMEM_LIMIT))(
        v, *(rider.arrs if rider else []))
    return out[0], list(out[1:])


FFN = ("w_gate_up", "w_down")
REST = ("w_in", "w_o", "w_uq", "w_ukv")
BIG = FFN + REST
TINY = ("conv_w",)
REPL = ("attn_norm", "mla_q_norm", "mla_kv_norm", "pool_w", "pool_scale", "swa_sinks", "mix_norm", "ffn_norm",
        "final_norm")
ORDER = ("attn_norm", "w_in", "mla_q_norm", "w_uq", "mla_kv_norm", "w_ukv", "conv_w", "pool_w", "pool_scale",
         "swa_sinks", "mix_norm", "w_o", "ffn_norm", "w_gate_up", "w_down", "final_norm")


def _rows8(shape):
    return -(-int(np.prod(shape)) // (8 * LANES)) * 8


def _pack(arrs):
    parts = []
    for a in arrs:
        r = _rows8(a.shape)
        parts.append(jnp.pad(a.reshape(-1), (0, r * LANES - a.size)).reshape(r, LANES))
    return jnp.concatenate(parts, axis=0)


def _unpack(buf, shapes):
    out, r0 = [], 0
    for s in shapes:
        n, r = int(np.prod(s)), _rows8(s)
        rows = buf[r0:r0 + r]
        out.append(rows.reshape(s) if n == r * LANES else rows.reshape(-1)[:n].reshape(s))
        r0 += r
    return out


def _cols_joined(g):
    return jnp.transpose(g, (0, 2, 1, 3)).reshape(g.shape[0], g.shape[2], 4 * g.shape[3])


def _cols_split(w):
    n, a, b4 = w.shape
    return jnp.transpose(w.reshape(n, a, 4, b4 // 4), (0, 2, 1, 3))


def _rope_tables(S):
    inv = 1.0 / (10000.0 ** (jnp.arange(0, 32, 2, dtype=F32) / 32))
    ang = jnp.arange(S, dtype=F32)[:, None] * inv[None, :]
    cos, sin = jnp.cos(ang), jnp.sin(ang)
    z = lambda w: jnp.zeros((S, w), F32)
    tc = jnp.concatenate([jnp.ones((S, 64), F32), cos, cos, jnp.ones((S, 32), F32)], axis=1)
    ts1 = jnp.concatenate([z(64), -sin, z(48)], axis=1)
    ts2 = jnp.concatenate([z(80), sin, z(32)], axis=1)
    return tc, ts1, ts2


def _pad_w_in(wt):
    z = lambda n: jnp.zeros((wt.shape[0], n, wt.shape[2]), wt.dtype)
    return jnp.concatenate([wt[:, 0:384], z(64), wt[:, 384:416], z(32), wt[:, 416:1952]], axis=1)


def _unpad_w_in(dt):
    return jnp.concatenate([dt[:, 0:384], dt[:, 448:480], dt[:, 512:2048]], axis=1)


def _pad_heads(w, src, offs):
    cols = []
    for h in range(HEADS):
        src0, n = src[h]
        z = lambda k: jnp.zeros(w.shape[:-1] + (k,), w.dtype)
        cols += [z(offs[h]), w[..., src0:src0 + n], z(128 - offs[h] - n)]
    return jnp.concatenate(cols, axis=-1)


UQ_SRC = [(h * 96, 96) for h in range(HEADS)]
KN_SRC = [(h * 128, 64) for h in range(HEADS)]
V_SRC = [(h * 128 + 64, 64) for h in range(HEADS)]
ZERO_OFF = [0] * HEADS
V_OFF = [(h % 2) * 64 for h in range(HEADS)]


def _unpad_heads(d, src, offs):
    return [d[..., h * 128 + offs[h]: h * 128 + offs[h] + src[h][1]] for h in range(HEADS)]


def kernel(x, attn_norm, w_in, mla_q_norm, w_uq, mla_kv_norm, w_ukv, conv_w, pool_w, pool_scale, swa_sinks, mix_norm, w_o, ffn_norm, w_gate_up, w_down, final_norm, loss_target, m_attn_norm, m_w_in, m_mla_q_norm, m_w_uq, m_mla_kv_norm, m_w_ukv, m_conv_w, m_pool_w, m_pool_scale, m_swa_sinks, m_mix_norm, m_w_o, m_ffn_norm, m_w_gate_up, m_w_down, m_final_norm, v_attn_norm, v_w_in, v_mla_q_norm, v_w_uq, v_mla_kv_norm, v_w_ukv, v_conv_w, v_pool_w, v_pool_scale, v_swa_sinks, v_mix_norm, v_w_o, v_ffn_norm, v_w_gate_up, v_w_down, v_final_norm):
    W = dict(attn_norm=attn_norm, w_in=w_in, mla_q_norm=mla_q_norm, w_uq=w_uq, mla_kv_norm=mla_kv_norm, w_ukv=w_ukv,
             conv_w=conv_w, pool_w=pool_w, pool_scale=pool_scale, swa_sinks=swa_sinks, mix_norm=mix_norm, w_o=w_o,
             ffn_norm=ffn_norm, w_gate_up=w_gate_up, w_down=w_down, final_norm=final_norm)
    M1 = dict(attn_norm=m_attn_norm, w_in=m_w_in, mla_q_norm=m_mla_q_norm, w_uq=m_w_uq, mla_kv_norm=m_mla_kv_norm,
              w_ukv=m_w_ukv, conv_w=m_conv_w, pool_w=m_pool_w, pool_scale=m_pool_scale, swa_sinks=m_swa_sinks,
              mix_norm=m_mix_norm, w_o=m_w_o, ffn_norm=m_ffn_norm, w_gate_up=m_w_gate_up, w_down=m_w_down,
              final_norm=m_final_norm)
    V2 = dict(attn_norm=v_attn_norm, w_in=v_w_in, mla_q_norm=v_mla_q_norm, w_uq=v_w_uq, mla_kv_norm=v_mla_kv_norm,
              w_ukv=v_w_ukv, conv_w=v_conv_w, pool_w=v_pool_w, pool_scale=v_pool_scale, swa_sinks=v_swa_sinks,
              mix_norm=v_mix_norm, w_o=v_w_o, ffn_norm=v_ffn_norm, w_gate_up=v_w_gate_up, w_down=v_w_down,
              final_norm=v_final_norm)
    S = x.shape[1]
    xc, yc, cc = _me()
    chip = 2 * xc + yc
    kc = jnp.stack([chip, cc]).astype(jnp.int32)

    first, later = ("w_in", "w_uq", "w_ukv", "conv_w"), ("w_o", "w_gate_up", "w_down")
    T = lambda a: jnp.swapaxes(a, 1, 2)
    W["w_in"], M1["w_in"], V2["w_in"] = T(w_in), T(m_w_in), T(v_w_in)
    placed = {n: _place(W[n], kc, F32 if n == "conv_w" else BF16, f"place_{n}") for n in first + later}
    gi, gq, gkv, gcv = _gather([placed[n] for n in first], "gather_weights")
    later_w = [placed[n] for n in later]
    win_p = _pad_w_in(gi.reshape(2, 4 * gi.shape[2], D))
    wuq_p = _pad_heads(_cols_joined(gq), UQ_SRC, ZERO_OFF)
    wukv = _cols_joined(gkv)
    wk_p = _pad_heads(wukv, KN_SRC, ZERO_OFF)
    wv_p = _pad_heads(wukv, V_SRC, V_OFF)
    conv8 = jnp.pad(_cols_joined(gcv), ((0, 0), (0, 5), (0, 0)))
    pwd = jnp.concatenate([jnp.concatenate(
        [jnp.pad(pool_w[:, 2 * b], ((0, 0), (0, 0), (0, 64))), jnp.pad(pool_w[:, 2 * b + 1], ((0, 0), (0, 0), (64, 0)))],
        axis=1) for b in range(2)], axis=1).astype(BF16)
    tabs = _rope_tables(S)
    g_attn, g_q, g_kv, g_mix, g_ffn, g_ps = (_g3(W[n]) for n in ("attn_norm", "mla_q_norm", "mla_kv_norm", "mix_norm",
                                                                  "ffn_norm", "pool_scale"))

    xs = [x[0]]
    saved = []
    for l in range(DEPTH):
        x0 = xs[-1]
        proj, h = _norm_mm(x0, g_attn, l, win_p, _wspec_in(l), D_INP, D_INP, F32, f"in_proj{l}", w_t=True)
        q, k, v, kt, vt = _mla_prep(proj, g_q, g_kv, wuq_p, wk_p, wv_p, tabs, l, f"mla_prep{l}")
        ya, lse, later_w = _mla_attn(q, k, vt, later_w, l, f"mla_attn{l}")
        go, gu4, gd = later_w
        wo, wdown = go.reshape(2, D, D), gd.reshape(2, D_FF, D)
        yb = _conv(proj, conv8, l, f"conv{l}")
        ycp = _pool(proj, pwd, g_ps, l, f"pool{l}")
        yd = _swa(proj, swa_sinks, l, f"swa{l}")
        x1, ycat, mixed = _mix_out(x0, ya, yb, ycp, yd, g_mix, wo, l, f"mix_out{l}")
        gu, h2 = _norm_mm(x1, g_ffn, l, gu4, _wspec_gu(l), 2 * D_FF, 2 * D_FF // 4, BF16, f"gate_up{l}")
        x2, act = _swiglu_mm_res(x1, gu, wdown, l, f"down{l}")
        saved.append(dict(x0=x0, proj=proj, h=h, q=q, k=k, kt=kt, v=v, lse=lse, x1=x1, ycat=ycat, mixed=mixed,
                          gu=gu, h2=h2, act=act))
        xs.append(x2)

    dx, dx16, dg_final, loss_tile = _loss_head(xs[-1], final_norm.reshape(1, D), loss_target[0], "loss_head")
    loss_here = (loss_tile[0, 0] * (0.5 / D)).reshape(1)

    G = {n: [None] * DEPTH for n in ("w_uq", "w_ukv") + TINY + REPL if n != "final_norm"}
    gw_in = gw_o = gw_gu = gw_down = None
    for l in reversed(range(DEPTH)):
        sv = saved[l]
        dgu = _bwd_down(dx16, wdown, sv["gu"], l, f"down_bwd{l}")
        gw_down = _mm_tn(sv["act"], dx16, l, gw_down, f"dw_down{l}")
        gw_gu = _mm_tn(sv["h2"], dgu, l, gw_gu, f"dw_gate_up{l}", split4=True)
        exchange_gu = None
        if l == 0:
            g_f = [gw_gu, gw_down.reshape(2, 4, D_FF // 4, D)]
            dx1, dx1_16, dg, got_f = _mm_nt_normbwd(dgu, gu4, l, sv["x1"], g_ffn, dx, 1, f"gate_up_bwd{l}",
                                                    rider=_swap_rider(g_f))
            pairs_f = [_pair_sum(g, o, kc, f"rs_pair_sum_{n}") for g, o, n in zip(g_f, got_f, FFN)]
            exchange_gu = _exchange_rider([pairs_f[0][1]])
            down_to_diagonal = _exchange_rider([pairs_f[1][1]], peers=(2,))
            down_to_neighbours = _exchange_rider([pairs_f[1][1]], peers=(0, 1))
        else:
            dx1, dx1_16, dg = _mm_nt_normbwd(dgu, gu4, l, sv["x1"], g_ffn, dx, 1, f"gate_up_bwd{l}")
        G["ffn_norm"][l] = dg[0]
        gw_o = _mm_tn(sv["mixed"], dx1_16, l, gw_o, f"dw_o{l}")
        if l == 0:
            dycat, dg, got_dg = _mm_nt_normbwd(dx1_16, wo.reshape(2, 1, D, D), l, sv["ycat"], g_mix, None, 4,
                                               f"mix_bwd{l}", rider=down_to_diagonal)
        else:
            dycat, dg = _mm_nt_normbwd(dx1_16, wo.reshape(2, 1, D, D), l, sv["ycat"], g_mix, None, 4, f"mix_bwd{l}")
        G["mix_norm"][l] = dg[0]

        proj = sv["proj"]
        delta = _mla_delta(dycat, sv["ycat"], f"mla_delta{l}")
        dq, dk, dv, got3_gu = _mla_attn_bwd(sv["q"], sv["k"], sv["kt"], sv["v"], dycat, sv["lse"], delta, exchange_gu,
                                            f"mla_attn_bwd{l}")
        dcq, dckv, dkr, dwuq, dwk, dwv, dgq, dgkv = _mla_prep_bwd(
            dq, dk, dv, proj, g_q, g_kv, wuq_p, wk_p, wv_p, tabs, l, f"mla_prep_bwd{l}")
        dgb, dgc, duc, dcw = _conv_bwd(proj, conv8, dycat, l, f"conv_bwd{l}")
        dup, dpw, dps = _pool_bwd(proj, pwd, g_ps, dycat, l, f"pool_bwd{l}")
        if l == 0:
            dqs, dks, dvs, dsink, got_xy = _swa_bwd(proj, swa_sinks, dycat, l, f"swa_bwd{l}", rider=down_to_neighbours)
            got3_f = [got3_gu, got_xy + got_dg]
        else:
            dqs, dks, dvs, dsink = _swa_bwd(proj, swa_sinks, dycat, l, f"swa_bwd{l}")
        dproj = jnp.concatenate([dcq, dckv, dkr, dgb, dgc, duc, dup, dqs, dks, dvs], axis=1)
        gw_in = _mm_tn(dproj, sv["h"], l, gw_in, f"dw_in{l}")
        G["w_uq"][l] = jnp.concatenate(_unpad_heads(dwuq, UQ_SRC, ZERO_OFF), axis=1)
        kn, vv = _unpad_heads(dwk, KN_SRC, ZERO_OFF), _unpad_heads(dwv, V_SRC, V_OFF)
        G["w_ukv"][l] = jnp.concatenate([t for h in range(HEADS) for t in (kn[h], vv[h])], axis=1)
        dx, dx16, dg = _mm_nt_normbwd(dproj, win_p.reshape(2, 1, D_INP, D), l, sv["x0"], g_attn, dx1, 1, f"in_proj_bwd{l}",
                                      w_t=True)
        G["attn_norm"][l] = dg[0]
        G["mla_q_norm"][l] = dgq[0]
        G["mla_kv_norm"][l] = dgkv[0]
        G["conv_w"][l] = dcw[0:3]
        G["pool_w"][l] = jnp.stack([dpw[0:64, 0:64], dpw[64:128, 64:128], dpw[128:192, 0:64], dpw[192:256, 64:128]])
        G["pool_scale"][l] = dps[0]
        G["swa_sinks"][l] = dsink[0, 0:4]
    grad_x = dx[None]
    Gl = {n: jnp.stack(G[n]) for n in TINY + REPL if n != "final_norm"}
    Gl["final_norm"] = dg_final[0]

    g_r = [_unpad_w_in(gw_in).reshape(2, 4, -1, D), gw_o.reshape(2, 4, D // 4, D), _cols_split(jnp.stack(G["w_uq"])),
           _cols_split(jnp.stack(G["w_ukv"]))]
    got_r = _ride_alone(_swap_rider(g_r), "rs_swap_cores")
    pairs_r = [_pair_sum(g, o, kc, f"rs_pair_sum_{n}") for g, o, n in zip(g_r, got_r, REST)]
    in_flight = _exchange_start([p[1] for p in pairs_r], "rs_exchange_start")
    us_f = [_chip_sum(p[0], o3, kc, f"rs_chip_sum_{n}", after=in_flight[-1]) for p, o3, n in zip(pairs_f, got3_f, FFN)]
    gsum_f = _join_layers(us_f, "rs_join_cores_ffn")
    res = {}

    def update(names, grads):
        for n, g in zip(names, grads):
            d_, m_, v_, g_ = _adamw(W[n], g, M1[n], V2[n], f"adamw_{n}", echo=True)
            back = T if n == "w_in" else (lambda a: a)
            res["g", n], res["d", n], res["m", n], res["v", n] = back(g_), back(d_), back(m_), back(v_)

    update(FFN, gsum_f)
    got3_r = _exchange_wait(*in_flight[:-1], res["d", FFN[-1]], "rs_exchange_wait")
    us_r = [_chip_sum(p[0], [o3], kc, f"rs_chip_sum_{n}") for p, o3, n in zip(pairs_r, got3_r, REST)]
    update(REST, _join_layers(us_r, "rs_join_cores"))
    small = TINY + REPL
    full_shapes = [Gl[n].shape for n in small] + [(1,)]
    summed, _ = _allsum_small(_pack([Gl[n] for n in small] + [loss_here]), "allsum_small")
    summed = _unpack(summed, full_shapes)
    loss = summed.pop().reshape(())

    def as3(a):
        if a.ndim <= 2:
            return a.reshape((1,) * (3 - a.ndim) + a.shape)
        return a.reshape(a.shape[0], -1, a.shape[-1])

    for n, g in zip(small, summed):
        if n in TINY:
            wdt = W[n].shape[2]
            g = lax.dynamic_slice_in_dim(g, chip * wdt, wdt, axis=2)
        out = _adamw(as3(W[n]), as3(g), as3(M1[n]), as3(V2[n]), f"adamw_{n}")
        res["g", n] = g
        res["d", n], res["m", n], res["v", n] = (o.reshape(W[n].shape) for o in out)

    return (loss, grad_x, *[res["g", n] for n in ORDER], *[res["d", n] for n in ORDER],
            *[res["m", n] for n in ORDER], *[res["v", n] for n in ORDER])
```

```python
import functools
import math

import numpy as np
import jax
import jax.numpy as jnp
from jax import lax
from jax.experimental import pallas as pl
from jax.experimental.pallas import tpu as pltpu

F32, BF16 = jnp.float32, jnp.bfloat16
SDS = jax.ShapeDtypeStruct
BS = pl.BlockSpec
MESH = pl.DeviceIdType.MESH

D = 1024
DEPTH = 2
HEADS = 4
D_FF = 2816
D_INP = 2048
EPS = 1e-6
SWA_WINDOW = 128
BLK = 128
SLOPES = tuple(2.0 ** (-8.0 * (i + 1) / 4) for i in range(4))
QK_SCALE = 1.0 / math.sqrt(96)
SWA_SCALE = 1.0 / math.sqrt(64)
LR, B1, B2, ADAM_EPS, WD, STEP = 0.001, 0.9, 0.999, 1e-08, 0.01, 10

LANES = 1024
VMEM_LIMIT = 56 * 1024 * 1024
NEG_INF = float("-inf")

C_CQ, C_CKV, C_KR, C_GB, C_GC, C_UC, C_UP, C_QS, C_KS, C_VS = 0, 256, 384, 512, 768, 1024, 1280, 1536, 1792, 1920


def _params(ngrid):
    return pltpu.CompilerParams(dimension_semantics=("arbitrary",) * ngrid, vmem_limit_bytes=VMEM_LIMIT)


def _pc(body, *, name, grid, in_specs, out_specs, out_shape, scratch=(), aliases=None):
    return pl.pallas_call(
        body, name=name, grid=grid, in_specs=in_specs, out_specs=out_specs, out_shape=out_shape,
        scratch_shapes=scratch, input_output_aliases=aliases or {}, compiler_params=_params(len(grid)))


def _dot(a, b):
    return jnp.dot(a, b, preferred_element_type=F32)


def _dot_nt(a, b):
    return lax.dot_general(a, b, (((1,), (1,)), ((), ())), preferred_element_type=F32)


def _dot_tn(a, b):
    return lax.dot_general(a, b, (((0,), (0,)), ((), ())), preferred_element_type=F32)


def _tile(n, cap):
    if n <= cap:
        return n
    t = cap - cap % 128
    while n % t:
        t -= 128
    return t


def _row_tile(a, b, cap=262144):
    bp = -(-b // 128) * 128
    best = None
    for t in range(8, a + 1, 8):
        if a % t == 0 and t * bp <= cap:
            best = t
    if best is None or (best < 64 and a * bp <= 2 * cap):
        return a
    return best


def _g3(a):
    return a.reshape(a.shape[0], 1, a.shape[1])


def _norm_mm(x, g3, l, w, wspec, N, tn, out_dtype, name, w_t=False):
    S, K = x.shape
    tm = min(1024 if out_dtype == BF16 else 512, S)

    def body(x_ref, g_ref, w_ref, y_ref, h_ref):
        @pl.when(pl.program_id(1) == 0)
        def _():
            xv = x_ref[...]
            r = lax.rsqrt(jnp.mean(xv * xv, axis=-1, keepdims=True) + EPS)
            h_ref[...] = (xv * r * g_ref[...]).astype(BF16)

        y_ref[...] = (_dot_nt if w_t else _dot)(h_ref[...], w_ref[...]).astype(out_dtype)

    return _pc(body, name=name, grid=(S // tm, N // tn),
               in_specs=[BS((tm, K), lambda i, j: (i, 0)), BS((None, 1, K), lambda i, j: (l, 0, 0)), wspec],
               out_specs=[BS((tm, tn), lambda i, j: (i, j)), BS((tm, K), lambda i, j: (i, 0))],
               out_shape=[SDS((S, N), out_dtype), SDS((S, K), BF16)])(x, g3, w)


def _wspec_in(l):
    return BS((None, D_INP, D), lambda i, j: (l, j, 0))


def _wspec_gu(l):
    return BS((None, None, D, 2 * D_FF // 4), lambda i, j: (l, j, 0, 0))


def _mix_out(x0, ya, yb, yc, yd, gmix3, wo, l, name):
    S = x0.shape[0]
    tm = min(512, S)

    def body(x_ref, ya_ref, yb_ref, yc_ref, yd_ref, g_ref, w_ref, x1_ref, ycat_ref, mixed_ref):
        groups = [ya_ref[...], yb_ref[...], yc_ref[...], yd_ref[...]]
        for gi, yg in enumerate(groups):
            sl = slice(gi * 256, (gi + 1) * 256)
            r = lax.rsqrt(jnp.mean(yg * yg, axis=-1, keepdims=True) + EPS)
            ycat_ref[:, sl] = yg
            mixed_ref[:, sl] = (yg * r * g_ref[:, sl]).astype(BF16)
        x1_ref[...] = x_ref[...] + _dot(mixed_ref[...], w_ref[...])

    row = lambda w: BS((tm, w), lambda i: (i, 0))
    return _pc(body, name=name, grid=(S // tm,),
               in_specs=[row(D), row(256), row(256), row(256), row(256), BS((None, 1, D), lambda i: (l, 0, 0)),
                         BS((None, D, D), lambda i: (l, 0, 0))],
               out_specs=[row(D), row(D), row(D)],
               out_shape=[SDS((S, D), F32), SDS((S, D), F32), SDS((S, D), BF16)])(x0, ya, yb, yc, yd, gmix3, wo)


def _swiglu_mm_res(x1, gu, wdown, l, name):
    S = x1.shape[0]
    tm = min(256, S)

    def body(x_ref, gate_ref, up_ref, w_ref, x2_ref, act_ref):
        acc = x_ref[...]
        for c0 in range(0, D_FF, D_FF // 2):
            cs = slice(c0, c0 + D_FF // 2)
            gt = gate_ref[:, cs].astype(F32)
            act = (gt * pl.reciprocal(1.0 + jnp.exp(-gt), approx=True) * up_ref[:, cs].astype(F32)).astype(BF16)
            act_ref[:, cs] = act
            acc = acc + _dot(act, w_ref[cs, :])
        x2_ref[...] = acc

    return _pc(body, name=name, grid=(S // tm,),
               in_specs=[BS((tm, D), lambda i: (i, 0)), BS((tm, D_FF), lambda i: (i, 0)),
                         BS((tm, D_FF), lambda i: (i, 1)), BS((None, D_FF, D), lambda i: (l, 0, 0))],
               out_specs=[BS((tm, D), lambda i: (i, 0)), BS((tm, D_FF), lambda i: (i, 0))],
               out_shape=[SDS((S, D), F32), SDS((S, D_FF), BF16)])(x1, gu, gu, wdown)


def _loss_head(x, g, tgt, name):
    S = x.shape[0]
    tm = min(512, S)

    def body(x_ref, g_ref, t_ref, dx_ref, dx16_ref, dg_ref, loss_ref):
        @pl.when(pl.program_id(0) == 0)
        def _():
            dg_ref[...] = jnp.zeros_like(dg_ref)
            loss_ref[...] = jnp.zeros_like(loss_ref)

        xv = x_ref[...]
        r = lax.rsqrt(jnp.mean(xv * xv, axis=-1, keepdims=True) + EPS)
        xh = xv * r
        gv = g_ref[...]
        diff = xh * gv - t_ref[...]
        loss_ref[...] += jnp.sum(diff * diff)
        dy = diff * (1.0 / D)
        dg_ref[...] += jnp.sum(dy * xh, axis=0, keepdims=True)
        dxh = dy * gv
        dx = r * (dxh - xh * jnp.mean(dxh * xh, axis=-1, keepdims=True))
        dx_ref[...] = dx
        dx16_ref[...] = dx.astype(BF16)

    row = BS((tm, D), lambda i: (i, 0))
    return _pc(body, name=name, grid=(S // tm,),
               in_specs=[row, BS((1, D), lambda i: (0, 0)), row],
               out_specs=[row, row, BS((8, D), lambda i: (0, 0)), BS((8, 128), lambda i: (0, 0))],
               out_shape=[SDS((S, D), F32), SDS((S, D), BF16), SDS((8, D), F32), SDS((8, 128), F32)])(x, g, tgt)


def _mm_tn(a, b, l, prev, name, split4=False):
    S, Ka = a.shape
    N = b.shape[1]
    if split4:
        ta, tn = _tile(Ka, 256), N // 4
        out_shape = SDS((2, 4, Ka, tn), F32)
        out_spec = BS((None, None, ta, tn), lambda j, i: (l, j, i, 0))
    else:
        ta, tn = _tile(Ka, 512), _tile(N, 1024)
        out_shape = SDS((2, Ka, N), F32)
        out_spec = BS((None, ta, tn), lambda j, i: (l, i, j))

    def body(a_ref, b_ref, *rest):
        rest[-1][...] = _dot_tn(a_ref[...], b_ref[...])

    in_specs = [BS((S, ta), lambda j, i: (0, i)), BS((S, tn), lambda j, i: (0, j))]
    args = [a, b]
    if prev is not None:
        in_specs.append(BS(memory_space=pl.ANY))
        args.append(prev)
    return _pc(body, name=name, grid=(N // tn, Ka // ta), in_specs=in_specs, out_specs=out_spec, out_shape=out_shape,
               aliases={2: 0} if prev is not None else None)(*args)


def _bwd_down(dx16, wdown, gu, l, name):
    S = dx16.shape[0]
    tm = min(256, S)

    def body(dx_ref, w_ref, gate_ref, up_ref, dgu_ref):
        dxv = dx_ref[...]
        for c0 in range(0, D_FF, 256):
            cs = slice(c0, c0 + 256)
            dact = _dot_nt(dxv, w_ref[cs, :])
            gt = gate_ref[:, cs].astype(F32)
            sg = pl.reciprocal(1.0 + jnp.exp(-gt), approx=True)
            dgu_ref[:, cs] = (dact * up_ref[:, cs].astype(F32) * (sg * (1.0 + gt * (1.0 - sg)))).astype(BF16)
            dgu_ref[:, D_FF + c0:D_FF + c0 + 256] = (dact * (gt * sg)).astype(BF16)

    return _pc(body, name=name, grid=(S // tm,),
               in_specs=[BS((tm, D), lambda i: (i, 0)), BS((None, D_FF, D), lambda i: (l, 0, 0)),
                         BS((tm, D_FF), lambda i: (i, 0)), BS((tm, D_FF), lambda i: (i, 1))],
               out_specs=BS((tm, 2 * D_FF), lambda i: (i, 0)),
               out_shape=SDS((S, 2 * D_FF), BF16))(dx16, wdown, gu, gu)


def _mm_nt_normbwd(dy, w4, l, x, g3, dres, ngroups, name, rider=None, w_t=False):
    S, K = dy.shape
    nk, kc = w4.shape[1], w4.shape[2 if w_t else 3]
    mm = _dot if w_t else _dot_nt
    tm = min(512, S)
    gw = D // ngroups
    has_res = dres is not None
    nr = rider.n if rider else 0
    n_in, n_out = 4 + has_res, 2 + has_res

    def body(*refs):
        dy_ref, w_ref, x_ref, g_ref = refs[:4]
        res_ref = refs[4] if has_res else None
        outs = refs[n_in + nr:n_in + nr + n_out]
        dx_ref, dg_ref = outs[0], outs[-1]
        dx16_ref = outs[1] if has_res else None
        r_io = (refs[n_in:n_in + nr], refs[n_in + nr + n_out:n_in + 2 * nr + n_out], refs[n_in + 2 * nr + n_out:])
        if rider:
            pl.when(pl.program_id(0) == 0)(lambda: rider.start(*r_io))

        @pl.when(pl.program_id(0) == 0)
        def _():
            dg_ref[...] = jnp.zeros_like(dg_ref)

        dh = mm(dy_ref[:, 0:kc], w_ref[0])
        for k in range(1, nk):
            dh = dh + mm(dy_ref[:, k * kc:(k + 1) * kc], w_ref[k])
        for gi in range(ngroups):
            sl = slice(gi * gw, (gi + 1) * gw)
            xg = x_ref[:, sl]
            r = lax.rsqrt(jnp.mean(xg * xg, axis=-1, keepdims=True) + EPS)
            xh = xg * r
            dhg = dh[:, sl]
            dg_ref[:, sl] += jnp.sum(dhg * xh, axis=0, keepdims=True)
            dxh = dhg * g_ref[:, sl]
            dxg = r * (dxh - xh * jnp.mean(dxh * xh, axis=-1, keepdims=True))
            if has_res:
                dxg = dxg + res_ref[:, sl]
                dx16_ref[:, sl] = dxg.astype(BF16)
            dx_ref[:, sl] = dxg
        if rider:
            pl.when(pl.program_id(0) == S // tm - 1)(lambda: rider.wait(*r_io))

    row = BS((tm, D), lambda i: (i, 0))
    in_specs = [BS((tm, K), lambda i: (i, 0)),
                BS((None,) + tuple(w4.shape[1:]), lambda i: (l, 0, 0, 0), pipeline_mode=pl.Buffered(1)), row,
                BS((None, 1, D), lambda i: (l, 0, 0))]
    args = [dy, w4, x, g3]
    out_specs, out_shape = [row], [SDS((S, D), F32)]
    if has_res:
        in_specs.append(row)
        args.append(dres)
        out_specs.append(row)
        out_shape.append(SDS((S, D), BF16))
    out_specs.append(BS((8, D), lambda i: (0, 0)))
    out_shape.append(SDS((8, D), F32))
    if not rider:
        return _pc(body, name=name, grid=(S // tm,), in_specs=in_specs, out_specs=out_specs, out_shape=out_shape)(*args)
    out = pl.pallas_call(body, name=name, grid=(S // tm,), in_specs=in_specs + [ANY] * nr,
                         out_specs=out_specs + [ANY] * nr, out_shape=out_shape + rider.out_shape,
                         scratch_shapes=rider.scratch(),
                         compiler_params=pltpu.CompilerParams(dimension_semantics=("arbitrary",),
                                                              vmem_limit_bytes=VMEM_LIMIT, has_side_effects=True))(
        *args, *rider.arrs)
    return (*out[:n_out], list(out[n_out:]))


def _rope(x, c, s1, s2):
    return x * c + pltpu.roll(x, 112, axis=1) * s1 + pltpu.roll(x, 16, axis=1) * s2


def _rope_t(dy, c, s1, s2):
    return dy * c + pltpu.roll(dy * s1, 16, axis=1) + pltpu.roll(dy * s2, 112, axis=1)


def _mla_prep(proj, gq3, gkv3, wuq, wk, wv, tabs, l, name):
    S = proj.shape[0]
    tm = min(512, S)
    tc, ts1, ts2 = tabs

    def body(cq_ref, ckv_ref, kr_ref, gq_ref, gkv_ref, wuq_ref, wk_ref, wv_ref, c_ref, s1_ref, s2_ref,
             q_ref, k_ref, v_ref, kt_ref, vt_ref):
        c, s1, s2 = c_ref[...], s1_ref[...], s2_ref[...]
        cq = cq_ref[...]
        rq = lax.rsqrt(jnp.mean(cq * cq, axis=-1, keepdims=True) + EPS)
        qa = _dot((cq * rq * gq_ref[...]).astype(BF16), wuq_ref[...])
        ckv = ckv_ref[...]
        rkv = lax.rsqrt(jnp.mean(ckv * ckv, axis=-1, keepdims=True) + EPS)
        ckvn = (ckv * rkv * gkv_ref[...]).astype(BF16)
        ka = _dot(ckvn, wk_ref[...])
        va = _dot(ckvn, wv_ref[...])
        v_ref[...] = va.astype(BF16)
        vt_ref[...] = va.T.astype(BF16)
        krr = _rope(kr_ref[...], c, s1, s2)
        for h in range(HEADS):
            sl = slice(h * 128, (h + 1) * 128)
            q_ref[:, sl] = (_rope(qa[:, sl], c, s1, s2) * QK_SCALE).astype(BF16)
            kh = ka[:, sl] + krr
            k_ref[:, sl] = kh.astype(BF16)
            kt_ref[sl, :] = kh.T.astype(BF16)

    lay = lambda a, b: BS((None, a, b), lambda i: (l, 0, 0))
    tab = BS((tm, 128), lambda i: (i, 0))
    return _pc(body, name=name, grid=(S // tm,),
               in_specs=[BS((tm, 256), lambda i: (i, 0)), BS((tm, 128), lambda i: (i, 2)), BS((tm, 128), lambda i: (i, 3)),
                         lay(1, 256), lay(1, 128), lay(256, 512), lay(128, 512), lay(128, 512), tab, tab, tab],
               out_specs=[BS((tm, 512), lambda i: (i, 0))] * 3 + [BS((512, tm), lambda i: (0, i))] * 2,
               out_shape=[SDS((S, 512), BF16)] * 3 + [SDS((512, S), BF16)] * 2)(
        proj, proj, proj, gq3, gkv3, wuq, wk, wv, tc, ts1, ts2)


def _mla_prep_bwd(dq, dk, dv, proj, gq3, gkv3, wuq, wk, wv, tabs, l, name):
    S = proj.shape[0]
    tm = min(512, S)
    tc, ts1, ts2 = tabs

    def body(dq_ref, dk_ref, dv_ref, cq_ref, ckv_ref, gq_ref, gkv_ref, wuq_ref, wk_ref, wv_ref, c_ref, s1_ref, s2_ref,
             dcq_ref, dckv_ref, dkr_ref, dwuq_ref, dwk_ref, dwv_ref, dgq_ref, dgkv_ref):
        @pl.when(pl.program_id(0) == 0)
        def _():
            for r in (dwuq_ref, dwk_ref, dwv_ref, dgq_ref, dgkv_ref):
                r[...] = jnp.zeros_like(r)

        c, s1, s2 = c_ref[...], s1_ref[...], s2_ref[...]
        dqp = jnp.concatenate(
            [_rope_t(dq_ref[h * 128:(h + 1) * 128, :].T * QK_SCALE, c, s1, s2) for h in range(HEADS)], axis=1).astype(BF16)
        cq = cq_ref[...]
        rq = lax.rsqrt(jnp.mean(cq * cq, axis=-1, keepdims=True) + EPS)
        cqh = cq * rq
        gq_v = gq_ref[...]
        dwuq_ref[...] += _dot_tn((cqh * gq_v).astype(BF16), dqp)
        dcqn = _dot_nt(dqp, wuq_ref[...])
        dgq_ref[...] += jnp.sum(dcqn * cqh, axis=0, keepdims=True)
        dxh = dcqn * gq_v
        dcq_ref[...] = (rq * (dxh - cqh * jnp.mean(dxh * cqh, axis=-1, keepdims=True))).astype(BF16)

        dkb = dk_ref[...].astype(BF16)
        dvb = dv_ref[...].astype(BF16)
        ckv = ckv_ref[...]
        rkv = lax.rsqrt(jnp.mean(ckv * ckv, axis=-1, keepdims=True) + EPS)
        ckh = ckv * rkv
        gkv_v = gkv_ref[...]
        ckvn = (ckh * gkv_v).astype(BF16)
        dwk_ref[...] += _dot_tn(ckvn, dkb)
        dwv_ref[...] += _dot_tn(ckvn, dvb)
        dckvn = _dot_nt(dkb, wk_ref[...]) + _dot_nt(dvb, wv_ref[...])
        dgkv_ref[...] += jnp.sum(dckvn * ckh, axis=0, keepdims=True)
        dyh = dckvn * gkv_v
        dckv_ref[...] = (rkv * (dyh - ckh * jnp.mean(dyh * ckh, axis=-1, keepdims=True))).astype(BF16)
        dks = dk_ref[:, 0:128] + dk_ref[:, 128:256] + dk_ref[:, 256:384] + dk_ref[:, 384:512]
        dkr_ref[...] = _rope_t(dks, c, s1, s2).astype(BF16)

    full = lambda a, b: BS((a, b), lambda i: (0, 0))
    lay = lambda a, b: BS((None, a, b), lambda i: (l, 0, 0))
    tab = BS((tm, 128), lambda i: (i, 0))
    row = lambda w: BS((tm, w), lambda i: (i, 0))
    return _pc(body, name=name, grid=(S // tm,),
               in_specs=[BS((512, tm), lambda i: (0, i)), row(512), row(512), BS((tm, 256), lambda i: (i, 0)),
                         BS((tm, 128), lambda i: (i, 2)),
                         lay(1, 256), lay(1, 128), lay(256, 512), lay(128, 512), lay(128, 512), tab, tab, tab],
               out_specs=[row(256), row(128), row(128), full(256, 512), full(128, 512), full(128, 512),
                          full(8, 256), full(8, 128)],
               out_shape=[SDS((S, 256), BF16), SDS((S, 128), BF16), SDS((S, 128), BF16), SDS((256, 512), F32),
                          SDS((128, 512), F32), SDS((128, 512), F32), SDS((8, 256), F32), SDS((8, 128), F32)])(
        dq, dk, dv, proj, proj, gq3, gkv3, wuq, wk, wv, tc, ts1, ts2)


def _causal_steps(n, q_outer):
    if q_outer:
        pairs = [(i, j) for i in range(n) for j in range(i + 1)]
    else:
        pairs = [(i, j) for j in range(n) for i in range(j, n)]
    return jnp.asarray([p[0] for p in pairs], jnp.int32), jnp.asarray([p[1] for p in pairs], jnp.int32)


def _mla_attn(q, k, vt, gts, layer, name):
    S = q.shape[0]
    t = min(512, S)
    n = S // t
    ng = len(gts)

    qi, kj = _causal_steps(n, True)
    last = qi.shape[0] - 1

    def body(qi_ref, kj_ref, q_ref, k_ref, vt_ref, *rest):
        (ya_ref, lse_ref), g_refs = rest[ng:ng + 2], rest[ng + 2:2 * ng + 2]
        m_sc, l_sc, acc_sc = rest[2 * ng + 2:2 * ng + 5]
        i, j = qi_ref[pl.program_id(1)], kj_ref[pl.program_id(1)]
        if ng:
            phases = _gather_phases(g_refs, [g.shape for g in gts], rest[2 * ng + 5], rest[2 * ng + 6], layer)
            for ph, (pp, ss) in zip(phases[:3], ((0, 0), (1, 0), (1, 2 * last // 3))):
                pl.when((pl.program_id(0) == pp) & (pl.program_id(1) == ss))(ph)

        @pl.when(j == 0)
        def _():
            m_sc[...] = jnp.full_like(m_sc, NEG_INF)
            l_sc[...] = jnp.zeros_like(l_sc)
            acc_sc[...] = jnp.zeros_like(acc_sc)

        def step(masked):
            for hh in range(2):
                sl = slice(hh * 128, (hh + 1) * 128)
                st = _dot_nt(k_ref[:, sl], q_ref[:, sl])
                if masked:
                    key = lax.broadcasted_iota(jnp.int32, (t, t), 0)
                    qry = lax.broadcasted_iota(jnp.int32, (t, t), 1)
                    st = jnp.where(key <= qry, st, NEG_INF)
                m_prev = m_sc[hh]
                m_new = jnp.maximum(m_prev, jnp.max(st, axis=0, keepdims=True))
                p = jnp.exp(st - m_new)
                alpha = jnp.exp(m_prev - m_new)
                l_sc[hh] = alpha * l_sc[hh] + jnp.sum(p, axis=0, keepdims=True)
                acc_sc[hh] = alpha * acc_sc[hh] + _dot(vt_ref[sl, :], p.astype(BF16))
                m_sc[hh] = m_new

        @pl.when(j < i)
        def _():
            step(False)

        @pl.when(j == i)
        def _():
            step(True)
            ya_ref[...] = (acc_sc[0] / l_sc[0] + acc_sc[1] / l_sc[1]).T
            for hh in range(2):
                lse_ref[hh] = m_sc[hh] + jnp.log(l_sc[hh])

        if ng:
            pl.when((pl.program_id(0) == 1) & (pl.program_id(1) == last))(phases[3])

    gs = pltpu.PrefetchScalarGridSpec(
        num_scalar_prefetch=2, grid=(2, qi.shape[0]),
        in_specs=[BS((t, 256), lambda p, s, qi, kj: (qi[s], p)), BS((t, 256), lambda p, s, qi, kj: (kj[s], p)),
                  BS((256, t), lambda p, s, qi, kj: (p, kj[s]))] + [ANY] * ng,
        out_specs=[BS((t, 128), lambda p, s, qi, kj: (qi[s], p)), BS((2, 1, t), lambda p, s, qi, kj: (p, 0, qi[s]))]
        + [ANY] * ng,
        scratch_shapes=[pltpu.VMEM((2, 1, t), F32), pltpu.VMEM((2, 1, t), F32), pltpu.VMEM((2, 128, t), F32)]
        + ([pltpu.SemaphoreType.DMA((7 * ng,)), pltpu.SemaphoreType.DMA((7 * ng,))] if ng else []))
    out = pl.pallas_call(body, name=name, grid_spec=gs,
                         out_shape=[SDS((S, 256), F32), SDS((HEADS, 1, S), F32)] + [SDS(g.shape, g.dtype) for g in gts],
                         input_output_aliases={5 + m: 2 + m for m in range(ng)},
                         compiler_params=pltpu.CompilerParams(dimension_semantics=("arbitrary",) * 2,
                                                              vmem_limit_bytes=VMEM_LIMIT, has_side_effects=bool(ng)))(
        qi, kj, q, k, vt, *gts)
    return out[0], out[1], list(out[2:])


def _mla_delta(dycat, ya, name, after=None):
    S = ya.shape[0]
    t = min(512, S)

    def body(do_ref, ya_ref, *refs):
        d_ref = refs[-1]
        prod = do_ref[...] * ya_ref[...]
        for p in range(2):
            pt = prod[:, p * 128:(p + 1) * 128].T
            d_ref[2 * p] = jnp.sum(pt[0:64, :], axis=0, keepdims=True)
            d_ref[2 * p + 1] = jnp.sum(pt[64:128, :], axis=0, keepdims=True)

    extra = [] if after is None else [after]
    return _pc(body, name=name, grid=(S // t,),
               in_specs=[BS((t, 256), lambda i: (i, 0)), BS((t, 256), lambda i: (i, 0))]
               + [BS((8, 128), lambda i: (0, 0)) for _ in extra],
               out_specs=BS((HEADS, 1, t), lambda i: (0, 0, i)), out_shape=SDS((HEADS, 1, S), F32))(dycat, ya, *extra)


def _mla_attn_bwd(q, k, kt, v, dya, lse, delta, rider, name):
    S = q.shape[0]
    t = min(512, S)
    n = S // t
    nr = rider.n if rider else 0

    qi, kj = _causal_steps(n, False)
    last = qi.shape[0] - 1

    def body(qi_ref, kj_ref, q_ref, k_ref, kt_ref, v_ref, do_ref, lse_ref, delta_ref, *rest):
        dqt_ref, dk_ref, dv_ref = rest[nr:nr + 3]
        r_io = (rest[:nr], rest[nr + 3:2 * nr + 3], rest[2 * nr + 3:])
        i, j = qi_ref[pl.program_id(1)], kj_ref[pl.program_id(1)]
        if rider:
            pl.when((pl.program_id(0) == 0) & (pl.program_id(1) == 0))(lambda: rider.start(*r_io))

        @pl.when(pl.program_id(1) == 0)
        def _():
            dqt_ref[...] = jnp.zeros_like(dqt_ref)

        @pl.when(i == j)
        def _():
            dk_ref[...] = jnp.zeros_like(dk_ref)
            dv_ref[...] = jnp.zeros_like(dv_ref)

        def step(masked):
            dob = do_ref[...].astype(BF16)
            cols = pl.ds(pl.multiple_of(i * t, t), t)
            for hh in range(2):
                sl = slice(hh * 128, (hh + 1) * 128)
                qv = q_ref[:, sl]
                p = jnp.exp(_dot_nt(k_ref[:, sl], qv) - lse_ref[hh])
                if masked:
                    key = lax.broadcasted_iota(jnp.int32, (t, t), 0)
                    qry = lax.broadcasted_iota(jnp.int32, (t, t), 1)
                    p = jnp.where(key <= qry, p, 0.0)
                dv_ref[:, sl] += _dot(p.astype(BF16), dob)
                ds = (p * (_dot_nt(v_ref[:, sl], dob) - delta_ref[hh])).astype(BF16)
                dk_ref[:, sl] += _dot(ds, qv)
                dqt_ref[sl, cols] += _dot(kt_ref[sl, :], ds)

        @pl.when(i > j)
        def _():
            step(False)

        @pl.when(i == j)
        def _():
            step(True)

        if rider:
            pl.when((pl.program_id(0) == 1) & (pl.program_id(1) == last))(lambda: rider.wait(*r_io))

    qs = BS((t, 256), lambda p, s, qi, kj: (qi[s], p))
    ks = BS((t, 256), lambda p, s, qi, kj: (kj[s], p))
    rowv = BS((2, 1, t), lambda p, s, qi, kj: (p, 0, qi[s]))
    gs = pltpu.PrefetchScalarGridSpec(
        num_scalar_prefetch=2, grid=(2, qi.shape[0]),
        in_specs=[qs, ks, BS((256, t), lambda p, s, qi, kj: (p, kj[s])), ks,
                  BS((t, 128), lambda p, s, qi, kj: (qi[s], p)), rowv, rowv] + [ANY] * nr,
        out_specs=[BS((256, S), lambda p, s, qi, kj: (p, 0)), ks, ks] + [ANY] * nr,
        scratch_shapes=rider.scratch() if rider else [])
    out = pl.pallas_call(body, name=name, grid_spec=gs,
                         out_shape=[SDS((512, S), F32), SDS((S, 512), F32), SDS((S, 512), F32)]
                         + (rider.out_shape if rider else []),
                         compiler_params=pltpu.CompilerParams(dimension_semantics=("arbitrary",) * 2,
                                                              vmem_limit_bytes=VMEM_LIMIT, has_side_effects=bool(rider)))(
        qi, kj, q, k, kt, v, dya, lse, delta, *(rider.arrs if rider else []))
    return out[0], out[1], out[2], list(out[3:])


def _swa_scores(qm, kk, valid, bias, sink):
    sc = jnp.where(valid, _dot_nt(qm, kk) * SWA_SCALE + bias, NEG_INF)
    m = jnp.maximum(jnp.max(sc, axis=-1, keepdims=True), sink)
    e = jnp.exp(sc - m)
    esink = jnp.exp(sink - m)
    den = jnp.sum(e, axis=-1, keepdims=True) + esink
    return e / den, esink / den


def _swa_consts(sink_ref, l):
    rows = HEADS * BLK
    r = lax.broadcasted_iota(jnp.int32, (rows, 2 * BLK), 0)
    c = lax.broadcasted_iota(jnp.int32, (rows, 2 * BLK), 1)
    dist = (r & (BLK - 1)) + BLK - c
    head = lax.broadcasted_iota(jnp.int32, (rows, 1), 0) // BLK

    def per_head(vals):
        return jnp.where(head == 0, vals[0], jnp.where(head == 1, vals[1], jnp.where(head == 2, vals[2], vals[3])))

    bias = -per_head(SLOPES) * dist.astype(F32)
    sink = per_head([sink_ref[l, h] for h in range(HEADS)])
    return (dist >= 0) & (dist < SWA_WINDOW), c >= BLK, bias, sink, head


def _to_half(xb, pos, b):
    return xb if pos == b else pltpu.roll(xb, 64, axis=1)


def _swa_stack(ref, st, lo):
    parts = []
    for b in range(2):
        xb = ref[pl.ds(st, BLK), b * 128:(b + 1) * 128]
        half = lo if b == 0 else ~lo
        parts += [jnp.where(half, _to_half(xb, pos, b), 0.0).astype(BF16) for pos in range(2)]
    return jnp.concatenate(parts, axis=0)


def _swa_unstack(x_all, lo):
    blocks = []
    for b in range(2):
        h0, h1 = (_to_half(x_all[(2 * b + pos) * BLK:(2 * b + pos + 1) * BLK], pos, b) for pos in range(2))
        blocks.append(jnp.where(lo, h0, h1))
    return blocks


def _swa(proj, sinks, l, name):
    S = proj.shape[0]
    nb = S // BLK

    def body(q_ref, k_ref, v_ref, sink_ref, o_ref, kp, vp):
        kp[0:BLK, :] = jnp.zeros((BLK, 128), BF16)
        vp[0:BLK, :] = jnp.zeros((BLK, 128), BF16)
        kp[BLK:, :] = k_ref[...].astype(BF16)
        vp[BLK:, :] = v_ref[...].astype(BF16)
        lo = lax.broadcasted_iota(jnp.int32, (BLK, 128), 1) < 64
        band, cur, bias, sink, _ = _swa_consts(sink_ref, l)

        def blk(i, carry):
            st = pl.multiple_of(i * BLK, BLK)
            kk = kp[pl.ds(st, 2 * BLK), :]
            vv = vp[pl.ds(st, 2 * BLK), :]
            p, _ = _swa_scores(_swa_stack(q_ref, st, lo), kk, band & (cur | (i > 0)), bias, sink)
            for b, ob in enumerate(_swa_unstack(_dot(p.astype(BF16), vv), lo)):
                o_ref[pl.ds(st, BLK), b * 128:(b + 1) * 128] = ob
            return carry

        lax.fori_loop(0, nb, blk, 0, unroll=2)

    return _pc(body, name=name, grid=(1,),
               in_specs=[BS((S, 256), lambda i: (0, C_QS // 256)), BS((S, 128), lambda i: (0, C_KS // 128)),
                         BS((S, 128), lambda i: (0, C_VS // 128)), BS(memory_space=pltpu.SMEM)],
               out_specs=BS((S, 256), lambda i: (0, 0)),
               out_shape=SDS((S, 256), F32),
               scratch=[pltpu.VMEM((S + BLK, 128), BF16), pltpu.VMEM((S + BLK, 128), BF16)])(proj, proj, proj, sinks)


def _swa_bwd(proj, sinks, dyd, l, name, rider=None):
    S = proj.shape[0]
    nb = S // BLK
    nr = rider.n if rider else 0

    def body(q_ref, k_ref, v_ref, sink_ref, do_ref, *rest):
        dq_ref, dk_ref, dv_ref, dsink_ref = rest[nr:nr + 4]
        kp, vp, dkp, dvp = rest[2 * nr + 4:2 * nr + 8]
        r_io = (rest[:nr], rest[nr + 4:2 * nr + 4], rest[2 * nr + 8:])
        if rider:
            rider.start(*r_io)
        kp[0:BLK, :] = jnp.zeros((BLK, 128), BF16)
        vp[0:BLK, :] = jnp.zeros((BLK, 128), BF16)
        kp[BLK:, :] = k_ref[...].astype(BF16)
        vp[BLK:, :] = v_ref[...].astype(BF16)
        dkp[...] = jnp.zeros_like(dkp)
        dvp[...] = jnp.zeros_like(dvp)
        lo = lax.broadcasted_iota(jnp.int32, (BLK, 128), 1) < 64
        lane8 = lax.broadcasted_iota(jnp.int32, (8, 128), 1)
        band, cur, bias, sink, head = _swa_consts(sink_ref, l)

        def blk(i, dsink):
            st = pl.multiple_of(i * BLK, BLK)
            kk = kp[pl.ds(st, 2 * BLK), :]
            vv = vp[pl.ds(st, 2 * BLK), :]
            qm, dom = _swa_stack(q_ref, st, lo), _swa_stack(do_ref, st, lo)
            p, psink = _swa_scores(qm, kk, band & (cur | (i > 0)), bias, sink)
            dp = _dot_nt(dom, vv)
            dvp[pl.ds(st, 2 * BLK), :] += _dot_tn(p.astype(BF16), dom)
            delta = jnp.sum(p * dp, axis=-1, keepdims=True)
            dsk = -psink * delta
            for h in range(HEADS):
                dsink = dsink + jnp.where(lane8 == h, jnp.sum(jnp.where(head == h, dsk, 0.0)), 0.0)
            dsc = (p * (dp - delta) * SWA_SCALE).astype(BF16)
            for b, dqb in enumerate(_swa_unstack(_dot(dsc, kk), lo)):
                dq_ref[pl.ds(st, BLK), b * 128:(b + 1) * 128] = dqb.astype(BF16)
            dkp[pl.ds(st, 2 * BLK), :] += _dot_tn(dsc, qm)
            return dsink

        dsink_ref[...] = lax.fori_loop(0, nb, blk, jnp.zeros((8, 128), F32), unroll=2)
        dk_ref[...] = dkp[BLK:, :].astype(BF16)
        dv_ref[...] = dvp[BLK:, :].astype(BF16)
        if rider:
            rider.wait(*r_io)

    in_specs = [BS((S, 256), lambda i: (0, C_QS // 256)), BS((S, 128), lambda i: (0, C_KS // 128)),
                BS((S, 128), lambda i: (0, C_VS // 128)), BS(memory_space=pltpu.SMEM), BS((S, 256), lambda i: (0, 3))]
    out_specs = [BS((S, 256), lambda i: (0, 0)), BS((S, 128), lambda i: (0, 0)), BS((S, 128), lambda i: (0, 0)),
                 BS((8, 128), lambda i: (0, 0))]
    out_shape = [SDS((S, 256), BF16), SDS((S, 128), BF16), SDS((S, 128), BF16), SDS((8, 128), F32)]
    scratch = [pltpu.VMEM((S + BLK, 128), BF16), pltpu.VMEM((S + BLK, 128), BF16),
               pltpu.VMEM((S + BLK, 128), F32), pltpu.VMEM((S + BLK, 128), F32)]
    if not rider:
        return _pc(body, name=name, grid=(1,), in_specs=in_specs, out_specs=out_specs, out_shape=out_shape,
                   scratch=scratch)(proj, proj, proj, sinks, dyd)
    out = pl.pallas_call(body, name=name, grid=(1,), in_specs=in_specs + [ANY] * nr, out_specs=out_specs + [ANY] * nr,
                         out_shape=out_shape + rider.out_shape, scratch_shapes=scratch + rider.scratch(),
                         compiler_params=pltpu.CompilerParams(dimension_semantics=("arbitrary",),
                                                              vmem_limit_bytes=VMEM_LIMIT, has_side_effects=True))(
        proj, proj, proj, sinks, dyd, *rider.arrs)
    return (*out[:4], list(out[4:]))


def _down(x, k, t):
    return jnp.where(t >= k, pltpu.roll(x, k, axis=0), 0.0)


def _up(x, k, t):
    n = x.shape[0]
    return jnp.where(t < n - k, pltpu.roll(x, n - k, axis=0), 0.0)


def _conv(proj, w8, l, name):
    S = proj.shape[0]

    def body(gb_ref, gc_ref, u_ref, w_ref, y_ref):
        t = lax.broadcasted_iota(jnp.int32, (S, 128), 0)
        z = gc_ref[...] * u_ref[...]
        c = w_ref[2:3, :] * z + w_ref[1:2, :] * _down(z, 1, t) + w_ref[0:1, :] * _down(z, 2, t)
        y_ref[...] = gb_ref[...] * c

    col = lambda c0: BS((S, 128), lambda i: (0, c0 // 128 + i))
    return _pc(body, name=name, grid=(2,),
               in_specs=[col(C_GB), col(C_GC), col(C_UC), BS((None, 8, 128), lambda i: (l, 0, i))],
               out_specs=BS((S, 128), lambda i: (0, i)), out_shape=SDS((S, 256), F32))(proj, proj, proj, w8)


def _conv_bwd(proj, w8, dycat, l, name):
    S = proj.shape[0]

    def body(gb_ref, gc_ref, u_ref, w_ref, dy_ref, dgb_ref, dgc_ref, du_ref, dw_ref):
        t = lax.broadcasted_iota(jnp.int32, (S, 128), 0)
        gc, u = gc_ref[...], u_ref[...]
        z = gc * u
        z1, z2 = _down(z, 1, t), _down(z, 2, t)
        w0, w1, w2 = w_ref[0:1, :], w_ref[1:2, :], w_ref[2:3, :]
        dy = dy_ref[...]
        dgb_ref[...] = (dy * (w2 * z + w1 * z1 + w0 * z2)).astype(BF16)
        dc = dy * gb_ref[...]
        dz = w2 * dc + w1 * _up(dc, 1, t) + w0 * _up(dc, 2, t)
        dgc_ref[...] = (dz * u).astype(BF16)
        du_ref[...] = (dz * gc).astype(BF16)
        row = lax.broadcasted_iota(jnp.int32, (8, 128), 0)
        sums = [jnp.sum(dc * zz, axis=0, keepdims=True) for zz in (z2, z1, z)]
        dw_ref[...] = jnp.where(row == 0, sums[0], jnp.where(row == 1, sums[1], jnp.where(row == 2, sums[2], 0.0)))

    col = lambda c0: BS((S, 128), lambda i: (0, c0 // 128 + i))
    out = BS((S, 128), lambda i: (0, i))
    return _pc(body, name=name, grid=(2,),
               in_specs=[col(C_GB), col(C_GC), col(C_UC), BS((None, 8, 128), lambda i: (l, 0, i)), col(256)],
               out_specs=[out, out, out, BS((8, 128), lambda i: (0, i))],
               out_shape=[SDS((S, 256), BF16)] * 3 + [SDS((8, 256), F32)])(proj, proj, proj, w8, dycat)


def _pool_parts(u, t, first):
    lo = lax.broadcasted_iota(jnp.int32, u.shape, 1) < 64
    s2 = u + _down(u, 1, t)
    s4 = s2 + _down(s2, 2, t)
    s8 = s4 + _down(s4, 4, t)
    s16 = s8 + _down(s8, 8, t)
    win = jnp.where(lo, jnp.where(first, s2, s8), jnp.where(first, s4, s16))
    wv = jnp.where(lo, jnp.where(first, 2, 8), jnp.where(first, 4, 16))
    cnt = jnp.minimum(t + 1, wv).astype(F32)
    return win, cnt, lo


def _pool(proj, pwd, scale3, l, name):
    S = proj.shape[0]

    def body(u_ref, pw_ref, sc_ref, y_ref):
        t = lax.broadcasted_iota(jnp.int32, (S, 128), 0)
        u = u_ref[...]
        win, cnt, _ = _pool_parts(u, t, pl.program_id(0) == 0)
        pooled = win / cnt - u
        y_ref[...] = _dot(pooled.astype(BF16), pw_ref[...]) * sc_ref[...]

    return _pc(body, name=name, grid=(2,),
               in_specs=[BS((S, 128), lambda i: (0, C_UP // 128 + i)), BS((None, 128, 128), lambda i: (l, i, 0)),
                         BS((None, 1, 128), lambda i: (l, 0, i))],
               out_specs=BS((S, 128), lambda i: (0, i)), out_shape=SDS((S, 256), F32))(proj, pwd, scale3)


def _pool_bwd(proj, pwd, scale3, dycat, l, name):
    S = proj.shape[0]

    def body(u_ref, pw_ref, sc_ref, dy_ref, du_ref, dpw_ref, dsc_ref):
        t = lax.broadcasted_iota(jnp.int32, (S, 128), 0)
        first = pl.program_id(0) == 0
        u = u_ref[...]
        win, cnt, lo = _pool_parts(u, t, first)
        pooled = (win / cnt - u).astype(BF16)
        pw = pw_ref[...]
        dy = dy_ref[...]
        dsc_ref[...] = jnp.broadcast_to(jnp.sum(dy * _dot(pooled, pw), axis=0, keepdims=True), (8, 128))
        dmb = (dy * sc_ref[...]).astype(BF16)
        dpw_ref[...] = _dot_tn(pooled, dmb)
        dpooled = _dot_nt(dmb, pw)
        a1 = dpooled / cnt
        a2 = a1 + _up(a1, 1, t)
        a4 = a2 + _up(a2, 2, t)
        a8 = a4 + _up(a4, 4, t)
        a16 = a8 + _up(a8, 8, t)
        dwin = jnp.where(lo, jnp.where(first, a2, a8), jnp.where(first, a4, a16))
        du_ref[...] = (dwin - dpooled).astype(BF16)

    return _pc(body, name=name, grid=(2,),
               in_specs=[BS((S, 128), lambda i: (0, C_UP // 128 + i)), BS((None, 128, 128), lambda i: (l, i, 0)),
                         BS((None, 1, 128), lambda i: (l, 0, i)), BS((S, 128), lambda i: (0, 4 + i))],
               out_specs=[BS((S, 128), lambda i: (0, i)), BS((128, 128), lambda i: (i, 0)), BS((8, 128), lambda i: (0, i))],
               out_shape=[SDS((S, 256), BF16), SDS((256, 128), F32), SDS((8, 256), F32)])(proj, pwd, scale3, dycat)


def _adamw(w, g, m, v, name, echo=False):
    n, a, b = w.shape
    tr = _row_tile(a, b)

    def body(w_ref, g_ref, m_ref, v_ref, d_ref, nm_ref, nv_ref, *g_out):
        gv = g_ref[...]
        if echo:
            g_out[0][...] = gv
        m_new = B1 * m_ref[...] + (1.0 - B1) * gv
        v_new = B2 * v_ref[...] + (1.0 - B2) * (gv * gv)
        m_hat = m_new / (1.0 - B1 ** STEP)
        v_hat = v_new / (1.0 - B2 ** STEP)
        d_ref[...] = -LR * (m_hat / (jnp.sqrt(v_hat) + ADAM_EPS) + WD * w_ref[...])
        nm_ref[...] = m_new
        nv_ref[...] = v_new

    sp = BS((None, tr, b), lambda i, t: (i, t, 0))
    return _pc(body, name=name, grid=(n, a // tr), in_specs=[sp] * 4, out_specs=[sp] * (3 + echo),
               out_shape=[SDS((n, a, b), F32)] * (3 + echo))(w, g, m, v)


def _prefetch_call(body, name, grid, in_specs, out_specs, out_shape):
    gs = pltpu.PrefetchScalarGridSpec(num_scalar_prefetch=1, grid=grid, in_specs=in_specs, out_specs=out_specs)
    return pl.pallas_call(body, name=name, grid_spec=gs, out_shape=out_shape, compiler_params=_params(len(grid)))


def _place(w, kc, dtype, name):
    _, a, b = w.shape

    def body(kc_ref, w_ref, o_ref):
        o_ref[...] = w_ref[...].astype(dtype)

    return _prefetch_call(body, name, (2,), [BS((None, a, b), lambda l, kc: (l, 0, 0))],
                          BS((None, None, a, b), lambda l, kc: (l, kc[0], 0, 0)), SDS((2, 4, a, b), dtype))(kc, w)


def _pair_sum(g, got, kc, name):
    _, _, a, b = g.shape
    tr = _row_tile(a, b)

    def body(kc_ref, a_ref, b_ref, t32_ref, t16_ref):
        s = a_ref[...] + b_ref[...]
        t16_ref[...] = s.astype(BF16)

        @pl.when(pl.program_id(1) == kc_ref[0])
        def _():
            t32_ref[...] = s

    sp = BS((None, tr, b), lambda t, k, kc: (k, t, 0))
    return _prefetch_call(body, name, (a // tr, 4),
                          [BS((None, None, tr, b), lambda t, k, kc: (kc[1], k, t, 0)), sp],
                          [BS((tr, b), lambda t, k, kc: (t, 0)), sp],
                          [SDS((a, b), F32), SDS((4, a, b), BF16)])(kc, g, got)


def _chip_sum(t32, gots, kc, name, after=None):
    a, b = t32.shape
    tr = _row_tile(a, b)
    ng = len(gots)

    def body(kc_ref, a_ref, *refs):
        acc = a_ref[...]
        for g_ref in refs[:ng]:
            for i in range(g_ref.shape[0]):
                acc = acc + g_ref[i].astype(F32)
        refs[-1][...] = acc

    extra = [] if after is None else [after]
    return _prefetch_call(body, name, (a // tr,),
                          [BS((tr, b), lambda t, kc: (t, 0))]
                          + [BS((g.shape[0], tr, b), lambda t, kc: (0, t, 0)) for g in gots]
                          + [BS((8, 128), lambda t, kc: (0, 0)) for _ in extra],
                          BS((None, tr, b), lambda t, kc: (kc[1], t, 0)), SDS((2, a, b), F32))(kc, t32, *gots, *extra)


def _me():
    return lax.axis_index("x"), lax.axis_index("y"), lax.axis_index("c")


def _other_chips(x, y):
    return [(1 - x, y), (x, 1 - y), (1 - x, 1 - y)]


ANY = BS(memory_space=pl.ANY)
COMM_PARAMS = pltpu.CompilerParams(has_side_effects=True)


def _gather(arrs, name):
    n = len(arrs)

    def body(*refs):
        for phase in _gather_phases(refs[n:2 * n], [a.shape for a in arrs], refs[2 * n], refs[2 * n + 1]):
            phase()

    return pl.pallas_call(body, name=name, out_shape=[SDS(a.shape, a.dtype) for a in arrs],
                          in_specs=[ANY] * n, out_specs=[ANY] * n, input_output_aliases={t: t for t in range(n)},
                          scratch_shapes=[pltpu.SemaphoreType.DMA((7 * n,)), pltpu.SemaphoreType.DMA((7 * n,))],
                          compiler_params=COMM_PARAMS)(*arrs)


def _gather_phases(outs, shapes, send_sems, recv_sems, layer=None):
    n = len(outs)
    cut = [s[2] // 2 // 16 * 16 for s in shapes]
    split = [r > 0 for r in cut]

    def plan():
        x, y, c = _me()
        return (c if layer is None else layer), (x, y), (x, y, c), (x, y, 1 - c), _other_chips(x, y)

    def role(moving, fn):
        if layer is None:
            fn()
        else:
            c = lax.axis_index("c")
            pl.when((c == layer) if moving else (c != layer))(fn)

    def blk(t, chip, layer, half=None):
        r = outs[t].at[layer, 2 * chip[0] + chip[1]]
        if half is None:
            return r
        return r.at[pl.ds(0, cut[t])] if half == 0 else r.at[pl.ds(cut[t], shapes[t][2] - cut[t])]

    def copy(t, k, ref, to):
        return pltpu.make_async_remote_copy(src_ref=ref, dst_ref=ref, send_sem=send_sems.at[7 * t + k],
                                            recv_sem=recv_sems.at[7 * t + k], device_id=to, device_id_type=MESH)

    def own_sends(t):
        c, chip, me, sib, (xn, yn, dg) = plan()
        cps = [copy(t, 0, blk(t, chip, c), (*xn, c)), copy(t, 1, blk(t, chip, c), (*yn, c))]
        return cps if split[t] else cps + [copy(t, 2, blk(t, chip, c), (*dg, c))]

    def relays(t):
        c, chip, me, sib, (xn, yn, dg) = plan()
        after_x = [copy(t, 4, blk(t, xn, c), sib)]
        after_y = [copy(t, 5, blk(t, yn, c), sib)]
        if split[t]:
            after_x.insert(0, copy(t, 2, blk(t, xn, c, 0), (*yn, c)))
            after_y.insert(0, copy(t, 3, blk(t, yn, c, 1), (*xn, c)))
        return after_x, after_y, [copy(t, 6, blk(t, dg, c), sib)]

    def send_own():
        for t in range(n):
            for cp in own_sends(t):
                cp.start()

    def relay_neighbours():
        c, chip, me, sib, (xn, yn, dg) = plan()
        for t in range(n):
            after_x, after_y, _ = relays(t)
            copy(t, 0, blk(t, xn, c), me).wait_recv()
            for cp in after_x:
                cp.start()
            copy(t, 1, blk(t, yn, c), me).wait_recv()
            for cp in after_y:
                cp.start()

    def relay_diagonal():
        c, chip, me, sib, (xn, yn, dg) = plan()
        for t in range(n):
            if split[t]:
                copy(t, 2, blk(t, dg, c, 0), me).wait_recv()
                copy(t, 3, blk(t, dg, c, 1), me).wait_recv()
            else:
                copy(t, 2, blk(t, dg, c), me).wait_recv()
            relays(t)[2][0].start()

    def take_sibling():
        _, chip, me, sib, (xn, yn, dg) = plan()
        theirs = 1 - lax.axis_index("c") if layer is None else layer
        for t in range(n):
            for k, peer in ((4, xn), (5, yn), (6, dg)):
                copy(t, k, blk(t, peer, theirs), me).wait_recv()

    def drain_sends():
        for t in range(n):
            after_x, after_y, after_d = relays(t)
            for cp in own_sends(t) + after_x + after_y + after_d:
                cp.wait_send()

    def finish():
        role(False, take_sibling)
        role(True, drain_sends)

    return ((lambda: role(True, send_own)), (lambda: role(True, relay_neighbours)),
            (lambda: role(True, relay_diagonal)), finish)


def _swap_copies(ins, outs, send_sems, recv_sems):
    x, y, c = _me()
    return [pltpu.make_async_remote_copy(src_ref=ins[t].at[1 - c], dst_ref=outs[t], send_sem=send_sems.at[t],
                                         recv_sem=recv_sems.at[t], device_id=(x, y, 1 - c), device_id_type=MESH)
            for t in range(len(ins))]


def _exchange_copies(peers, ins, outs, send_sems, recv_sems):
    x, y, c = _me()
    chips = _other_chips(x, y)
    n = len(peers)
    return [pltpu.make_async_remote_copy(src_ref=ins[t].at[2 * chips[j][0] + chips[j][1]], dst_ref=outs[t].at[i],
                                         send_sem=send_sems.at[n * t + i], recv_sem=recv_sems.at[n * t + i],
                                         device_id=(*chips[j], c), device_id_type=MESH)
            for i, j in enumerate(peers) for t in range(len(ins))]


class _Rider:
    def __init__(self, arrs, out_shape, nsem, copies):
        self.arrs, self.out_shape, self.nsem, self.copies = list(arrs), out_shape, nsem, copies
        self.n = len(self.arrs)

    def scratch(self):
        return [pltpu.SemaphoreType.DMA((self.nsem,)), pltpu.SemaphoreType.DMA((self.nsem,))]

    def start(self, ins, outs, sems):
        for cp in self.copies(ins, outs, *sems):
            cp.start()

    def wait(self, ins, outs, sems):
        for cp in self.copies(ins, outs, *sems):
            cp.wait()


def _swap_rider(gs):
    return _Rider(gs, [SDS(g.shape[1:], g.dtype) for g in gs], len(gs), _swap_copies)


def _exchange_rider(ts, peers=(0, 1, 2)):
    return _Rider(ts, [SDS((len(peers),) + t.shape[1:], t.dtype) for t in ts], len(peers) * len(ts),
                  functools.partial(_exchange_copies, peers))


HBM = BS(memory_space=pltpu.HBM)
SEM = BS(memory_space=pltpu.SEMAPHORE)
SPLIT_PARAMS = pltpu.CompilerParams(has_side_effects=pltpu.SideEffectType.DATAFLOW_SIDE_EFFECTING)


def _exchange_start(ts, name):
    n = len(ts)
    lands = [lax.empty((3,) + t.shape[1:], t.dtype) for t in ts]

    def body(*refs):
        for cp in _exchange_copies((0, 1, 2), refs[:n], refs[n:2 * n], refs[2 * n], refs[2 * n + 1]):
            cp.start()
        refs[-1][...] = jnp.zeros_like(refs[-1])

    held = [pltpu.with_memory_space_constraint(a, pltpu.HBM) for a in list(ts) + lands]
    out = pl.pallas_call(
        body, name=name,
        out_shape=(pltpu.SemaphoreType.DMA((3 * n,)), pltpu.SemaphoreType.DMA((3 * n,)),
                   *[pltpu.HBM(a.shape, a.dtype) for a in held], SDS((8, 128), F32)),
        in_specs=[HBM] * (2 * n), out_specs=(SEM, SEM, *[HBM] * (2 * n), BS(memory_space=pltpu.VMEM)),
        input_output_aliases={i: 2 + i for i in range(2 * n)}, compiler_params=SPLIT_PARAMS)(*held)
    return out[0], out[1], list(out[2:2 + n]), list(out[2 + n:2 + 2 * n]), out[-1]


def _exchange_wait(send_sems, recv_sems, ts, lands, after, name):
    n = len(ts)

    def body(*refs):
        for cp in _exchange_copies((0, 1, 2), refs[:n], refs[n:2 * n], refs[2 * n], refs[2 * n + 1]):
            cp.wait_send()
            cp.wait_recv()

    out = pl.pallas_call(
        body, name=name, out_shape=tuple(pltpu.HBM(a.shape, a.dtype) for a in ts + lands),
        in_specs=[HBM] * (2 * n) + [SEM, SEM, ANY], out_specs=[HBM] * (2 * n),
        input_output_aliases={i: i for i in range(2 * n)}, compiler_params=SPLIT_PARAMS)(
        *ts, *lands, send_sems, recv_sems, after)
    return list(out[n:2 * n])


def _ride_alone(rider, name):
    n = rider.n

    def body(*refs):
        rider.start(refs[:n], refs[n:2 * n], refs[2 * n:])
        rider.wait(refs[:n], refs[n:2 * n], refs[2 * n:])

    return pl.pallas_call(body, name=name, out_shape=rider.out_shape, in_specs=[ANY] * n, out_specs=[ANY] * n,
                          scratch_shapes=rider.scratch(), compiler_params=COMM_PARAMS)(*rider.arrs)


def _join_layers(us, name):
    n = len(us)

    def body(*refs):
        outs, send_sems, recv_sems = refs[n:2 * n], refs[2 * n], refs[2 * n + 1]
        x, y, c = _me()
        cps = [pltpu.make_async_remote_copy(src_ref=outs[t].at[c], dst_ref=outs[t].at[c], send_sem=send_sems.at[t],
                                            recv_sem=recv_sems.at[t], device_id=(x, y, 1 - c), device_id_type=MESH)
               for t in range(n)]
        for cp in cps:
            cp.start()
        for cp in cps:
            cp.wait()

    return pl.pallas_call(body, name=name, out_shape=[SDS(u.shape, u.dtype) for u in us],
                          in_specs=[ANY] * n, out_specs=[ANY] * n, input_output_aliases={t: t for t in range(n)},
                          scratch_shapes=[pltpu.SemaphoreType.DMA((n,)), pltpu.SemaphoreType.DMA((n,))],
                          compiler_params=COMM_PARAMS)(*us)


def _allsum_small(v, name, after):
    M = v.shape[0]

    def body(x_ref, after_ref, o_ref, all_ref, send_sems, recv_sems, local_sem):
        x, y, c = _me()
        me, sib = (x, y, c), (x, y, 1 - c)
        chips = _other_chips(x, y)

        def rows(px, py, pc):
            return all_ref.at[pl.ds((4 * px + 2 * py + pc) * M, M), :]

        def copy(k, block, to, src=None):
            return pltpu.make_async_remote_copy(src_ref=rows(*block) if src is None else src, dst_ref=rows(*block),
                                                send_sem=send_sems.at[k], recv_sem=recv_sems.at[k],
                                                device_id=to, device_id_type=MESH)

        mine = pltpu.make_async_copy(x_ref, rows(*me), local_sem)
        mine.start()
        first = [copy(0, me, sib, src=x_ref)]
        first += [copy(1 + j, me, (*chip, c), src=x_ref) for j, chip in enumerate(chips)]
        for cp in first:
            cp.start()
        passed = [copy(4 + j, (*chip, c), sib) for j, chip in enumerate(chips)]
        for j, chip in enumerate(chips):
            copy(1 + j, (*chip, c), me).wait_recv()
            passed[j].start()
        copy(0, sib, me).wait_recv()
        for j, chip in enumerate(chips):
            copy(4 + j, (*chip, 1 - c), me).wait_recv()
        for cp in first + passed:
            cp.wait_send()
        mine.wait()
        acc = all_ref[0:M, :]
        for d in range(1, 8):
            acc = acc + all_ref[d * M:(d + 1) * M, :]
        o_ref[...] = acc

    vm = BS(memory_space=pltpu.VMEM)
    return pl.pallas_call(body, name=name, out_shape=SDS((M, LANES), F32), in_specs=[vm, ANY], out_specs=vm,
                          scratch_shapes=[pltpu.VMEM((8 * M, LANES), F32), pltpu.SemaphoreType.DMA((7,)),
                                          pltpu.SemaphoreType.DMA((7,)), pltpu.SemaphoreType.DMA],
                          compiler_params=pltpu.CompilerParams(has_side_effects=True, vmem_limit_bytes=VMEM_LIMIT))(
        v, after)


FFN = ("w_gate_up", "w_down")
REST = ("w_in", "w_o", "w_uq", "w_ukv")
BIG = FFN + REST
TINY = ("conv_w",)
REPL = ("attn_norm", "mla_q_norm", "mla_kv_norm", "pool_w", "pool_scale", "swa_sinks", "mix_norm", "ffn_norm",
        "final_norm")
ORDER = ("attn_norm", "w_in", "mla_q_norm", "w_uq", "mla_kv_norm", "w_ukv", "conv_w", "pool_w", "pool_scale",
         "swa_sinks", "mix_norm", "w_o", "ffn_norm", "w_gate_up", "w_down", "final_norm")


def _rows8(shape):
    return -(-int(np.prod(shape)) // (8 * LANES)) * 8


def _pack(arrs):
    parts = []
    for a in arrs:
        r = _rows8(a.shape)
        parts.append(jnp.pad(a.reshape(-1), (0, r * LANES - a.size)).reshape(r, LANES))
    return jnp.concatenate(parts, axis=0)


def _unpack(buf, shapes):
    out, r0 = [], 0
    for s in shapes:
        n, r = int(np.prod(s)), _rows8(s)
        rows = buf[r0:r0 + r]
        out.append(rows.reshape(s) if n == r * LANES else rows.reshape(-1)[:n].reshape(s))
        r0 += r
    return out


def _cols_joined(g):
    return jnp.transpose(g, (0, 2, 1, 3)).reshape(g.shape[0], g.shape[2], 4 * g.shape[3])


def _cols_split(w):
    n, a, b4 = w.shape
    return jnp.transpose(w.reshape(n, a, 4, b4 // 4), (0, 2, 1, 3))


def _rope_tables(S):
    inv = 1.0 / (10000.0 ** (jnp.arange(0, 32, 2, dtype=F32) / 32))
    ang = jnp.arange(S, dtype=F32)[:, None] * inv[None, :]
    cos, sin = jnp.cos(ang), jnp.sin(ang)
    z = lambda w: jnp.zeros((S, w), F32)
    tc = jnp.concatenate([jnp.ones((S, 64), F32), cos, cos, jnp.ones((S, 32), F32)], axis=1)
    ts1 = jnp.concatenate([z(64), -sin, z(48)], axis=1)
    ts2 = jnp.concatenate([z(80), sin, z(32)], axis=1)
    return tc, ts1, ts2


def _pad_w_in(wt):
    z = lambda n: jnp.zeros((wt.shape[0], n, wt.shape[2]), wt.dtype)
    return jnp.concatenate([wt[:, 0:384], z(64), wt[:, 384:416], z(32), wt[:, 416:1952]], axis=1)


def _unpad_w_in(dt):
    return jnp.concatenate([dt[:, 0:384], dt[:, 448:480], dt[:, 512:2048]], axis=1)


def _pad_heads(w, src, offs):
    cols = []
    for h in range(HEADS):
        src0, n = src[h]
        z = lambda k: jnp.zeros(w.shape[:-1] + (k,), w.dtype)
        cols += [z(offs[h]), w[..., src0:src0 + n], z(128 - offs[h] - n)]
    return jnp.concatenate(cols, axis=-1)


UQ_SRC = [(h * 96, 96) for h in range(HEADS)]
KN_SRC = [(h * 128, 64) for h in range(HEADS)]
V_SRC = [(h * 128 + 64, 64) for h in range(HEADS)]
ZERO_OFF = [0] * HEADS
V_OFF = [(h % 2) * 64 for h in range(HEADS)]


def _unpad_heads(d, src, offs):
    return [d[..., h * 128 + offs[h]: h * 128 + offs[h] + src[h][1]] for h in range(HEADS)]


def kernel(x, attn_norm, w_in, mla_q_norm, w_uq, mla_kv_norm, w_ukv, conv_w, pool_w, pool_scale, swa_sinks, mix_norm, w_o, ffn_norm, w_gate_up, w_down, final_norm, loss_target, m_attn_norm, m_w_in, m_mla_q_norm, m_w_uq, m_mla_kv_norm, m_w_ukv, m_conv_w, m_pool_w, m_pool_scale, m_swa_sinks, m_mix_norm, m_w_o, m_ffn_norm, m_w_gate_up, m_w_down, m_final_norm, v_attn_norm, v_w_in, v_mla_q_norm, v_w_uq, v_mla_kv_norm, v_w_ukv, v_conv_w, v_pool_w, v_pool_scale, v_swa_sinks, v_mix_norm, v_w_o, v_ffn_norm, v_w_gate_up, v_w_down, v_final_norm):
    W = dict(attn_norm=attn_norm, w_in=w_in, mla_q_norm=mla_q_norm, w_uq=w_uq, mla_kv_norm=mla_kv_norm, w_ukv=w_ukv,
             conv_w=conv_w, pool_w=pool_w, pool_scale=pool_scale, swa_sinks=swa_sinks, mix_norm=mix_norm, w_o=w_o,
             ffn_norm=ffn_norm, w_gate_up=w_gate_up, w_down=w_down, final_norm=final_norm)
    M1 = dict(attn_norm=m_attn_norm, w_in=m_w_in, mla_q_norm=m_mla_q_norm, w_uq=m_w_uq, mla_kv_norm=m_mla_kv_norm,
              w_ukv=m_w_ukv, conv_w=m_conv_w, pool_w=m_pool_w, pool_scale=m_pool_scale, swa_sinks=m_swa_sinks,
              mix_norm=m_mix_norm, w_o=m_w_o, ffn_norm=m_ffn_norm, w_gate_up=m_w_gate_up, w_down=m_w_down,
              final_norm=m_final_norm)
    V2 = dict(attn_norm=v_attn_norm, w_in=v_w_in, mla_q_norm=v_mla_q_norm, w_uq=v_w_uq, mla_kv_norm=v_mla_kv_norm,
              w_ukv=v_w_ukv, conv_w=v_conv_w, pool_w=v_pool_w, pool_scale=v_pool_scale, swa_sinks=v_swa_sinks,
              mix_norm=v_mix_norm, w_o=v_w_o, ffn_norm=v_ffn_norm, w_gate_up=v_w_gate_up, w_down=v_w_down,
              final_norm=v_final_norm)
    S = x.shape[1]
    xc, yc, cc = _me()
    chip = 2 * xc + yc
    kc = jnp.stack([chip, cc]).astype(jnp.int32)

    first, later = ("w_in", "w_uq", "w_ukv", "conv_w"), ("w_o", "w_gate_up", "w_down")
    T = lambda a: jnp.swapaxes(a, 1, 2)
    W["w_in"], M1["w_in"], V2["w_in"] = T(w_in), T(m_w_in), T(v_w_in)
    placed = {n: _place(W[n], kc, F32 if n == "conv_w" else BF16, f"place_{n}") for n in first + later}
    gi, gq, gkv, gcv = _gather([placed[n] for n in first], "gather_weights")
    later_w = [placed[n] for n in later]
    win_p = _pad_w_in(gi.reshape(2, 4 * gi.shape[2], D))
    wuq_p = _pad_heads(_cols_joined(gq), UQ_SRC, ZERO_OFF)
    wukv = _cols_joined(gkv)
    wk_p = _pad_heads(wukv, KN_SRC, ZERO_OFF)
    wv_p = _pad_heads(wukv, V_SRC, V_OFF)
    conv8 = jnp.pad(_cols_joined(gcv), ((0, 0), (0, 5), (0, 0)))
    pwd = jnp.concatenate([jnp.concatenate(
        [jnp.pad(pool_w[:, 2 * b], ((0, 0), (0, 0), (0, 64))), jnp.pad(pool_w[:, 2 * b + 1], ((0, 0), (0, 0), (64, 0)))],
        axis=1) for b in range(2)], axis=1).astype(BF16)
    tabs = _rope_tables(S)
    g_attn, g_q, g_kv, g_mix, g_ffn, g_ps = (_g3(W[n]) for n in ("attn_norm", "mla_q_norm", "mla_kv_norm", "mix_norm",
                                                                  "ffn_norm", "pool_scale"))

    xs = [x[0]]
    saved = []
    for l in range(DEPTH):
        x0 = xs[-1]
        proj, h = _norm_mm(x0, g_attn, l, win_p, _wspec_in(l), D_INP, D_INP, F32, f"in_proj{l}", w_t=True)
        q, k, v, kt, vt = _mla_prep(proj, g_q, g_kv, wuq_p, wk_p, wv_p, tabs, l, f"mla_prep{l}")
        ya, lse, later_w = _mla_attn(q, k, vt, later_w, l, f"mla_attn{l}")
        go, gu4, gd = later_w
        wo, wdown = go.reshape(2, D, D), gd.reshape(2, D_FF, D)
        yb = _conv(proj, conv8, l, f"conv{l}")
        ycp = _pool(proj, pwd, g_ps, l, f"pool{l}")
        yd = _swa(proj, swa_sinks, l, f"swa{l}")
        x1, ycat, mixed = _mix_out(x0, ya, yb, ycp, yd, g_mix, wo, l, f"mix_out{l}")
        gu, h2 = _norm_mm(x1, g_ffn, l, gu4, _wspec_gu(l), 2 * D_FF, 2 * D_FF // 4, BF16, f"gate_up{l}")
        x2, act = _swiglu_mm_res(x1, gu, wdown, l, f"down{l}")
        saved.append(dict(x0=x0, proj=proj, h=h, q=q, k=k, kt=kt, v=v, lse=lse, x1=x1, ycat=ycat, mixed=mixed,
                          gu=gu, h2=h2, act=act))
        xs.append(x2)

    dx, dx16, dg_final, loss_tile = _loss_head(xs[-1], final_norm.reshape(1, D), loss_target[0], "loss_head")
    loss_here = (loss_tile[0, 0] * (0.5 / D)).reshape(1)

    G = {n: [None] * DEPTH for n in ("w_uq", "w_ukv") + TINY + REPL if n != "final_norm"}
    gw_in = gw_o = gw_gu = gw_down = None
    for l in reversed(range(DEPTH)):
        sv = saved[l]
        dgu = _bwd_down(dx16, wdown, sv["gu"], l, f"down_bwd{l}")
        gw_down = _mm_tn(sv["act"], dx16, l, gw_down, f"dw_down{l}")
        gw_gu = _mm_tn(sv["h2"], dgu, l, gw_gu, f"dw_gate_up{l}", split4=True)
        ffn_token = None
        if l == 0:
            g_f = [gw_gu, gw_down.reshape(2, 4, D_FF // 4, D)]
            dx1, dx1_16, dg, got_f = _mm_nt_normbwd(dgu, gu4, l, sv["x1"], g_ffn, dx, 1, f"gate_up_bwd{l}",
                                                    rider=_swap_rider(g_f))
            pairs_f = [_pair_sum(g, o, kc, f"rs_pair_sum_{n}") for g, o, n in zip(g_f, got_f, FFN)]
            ffn_flight = _exchange_start([p[1] for p in pairs_f], "rs_exchange_start_ffn")
            ffn_token = ffn_flight[-1]
        else:
            dx1, dx1_16, dg = _mm_nt_normbwd(dgu, gu4, l, sv["x1"], g_ffn, dx, 1, f"gate_up_bwd{l}")
        G["ffn_norm"][l] = dg[0]
        gw_o = _mm_tn(sv["mixed"], dx1_16, l, gw_o, f"dw_o{l}")
        dycat, dg = _mm_nt_normbwd(dx1_16, wo.reshape(2, 1, D, D), l, sv["ycat"], g_mix, None, 4, f"mix_bwd{l}")
        G["mix_norm"][l] = dg[0]

        proj = sv["proj"]
        delta = _mla_delta(dycat, sv["ycat"], f"mla_delta{l}", after=ffn_token)
        dq, dk, dv, _ = _mla_attn_bwd(sv["q"], sv["k"], sv["kt"], sv["v"], dycat, sv["lse"], delta, None,
                                      f"mla_attn_bwd{l}")
        dcq, dckv, dkr, dwuq, dwk, dwv, dgq, dgkv = _mla_prep_bwd(
            dq, dk, dv, proj, g_q, g_kv, wuq_p, wk_p, wv_p, tabs, l, f"mla_prep_bwd{l}")
        dgb, dgc, duc, dcw = _conv_bwd(proj, conv8, dycat, l, f"conv_bwd{l}")
        dup, dpw, dps = _pool_bwd(proj, pwd, g_ps, dycat, l, f"pool_bwd{l}")
        dqs, dks, dvs, dsink = _swa_bwd(proj, swa_sinks, dycat, l, f"swa_bwd{l}")
        dproj = jnp.concatenate([dcq, dckv, dkr, dgb, dgc, duc, dup, dqs, dks, dvs], axis=1)
        gw_in = _mm_tn(dproj, sv["h"], l, gw_in, f"dw_in{l}")
        G["w_uq"][l] = jnp.concatenate(_unpad_heads(dwuq, UQ_SRC, ZERO_OFF), axis=1)
        kn, vv = _unpad_heads(dwk, KN_SRC, ZERO_OFF), _unpad_heads(dwv, V_SRC, V_OFF)
        G["w_ukv"][l] = jnp.concatenate([t for h in range(HEADS) for t in (kn[h], vv[h])], axis=1)
        dx, dx16, dg = _mm_nt_normbwd(dproj, win_p.reshape(2, 1, D_INP, D), l, sv["x0"], g_attn, dx1, 1, f"in_proj_bwd{l}",
                                      w_t=True)
        G["attn_norm"][l] = dg[0]
        G["mla_q_norm"][l] = dgq[0]
        G["mla_kv_norm"][l] = dgkv[0]
        G["conv_w"][l] = dcw[0:3]
        G["pool_w"][l] = jnp.stack([dpw[0:64, 0:64], dpw[64:128, 64:128], dpw[128:192, 0:64], dpw[192:256, 64:128]])
        G["pool_scale"][l] = dps[0]
        G["swa_sinks"][l] = dsink[0, 0:4]
    grad_x = dx[None]
    Gl = {n: jnp.stack(G[n]) for n in TINY + REPL if n != "final_norm"}
    Gl["final_norm"] = dg_final[0]

    g_r = [_unpad_w_in(gw_in).reshape(2, 4, -1, D), gw_o.reshape(2, 4, D // 4, D), _cols_split(jnp.stack(G["w_uq"])),
           _cols_split(jnp.stack(G["w_ukv"]))]
    got_r = _ride_alone(_swap_rider(g_r), "rs_swap_cores")
    pairs_r = [_pair_sum(g, o, kc, f"rs_pair_sum_{n}") for g, o, n in zip(g_r, got_r, REST)]
    got3_f = _exchange_wait(*ffn_flight[:-1], dx16, "rs_exchange_wait_ffn")
    in_flight = _exchange_start([p[1] for p in pairs_r], "rs_exchange_start")
    us_f = [_chip_sum(p[0], [o3], kc, f"rs_chip_sum_{n}", after=in_flight[-1])
            for p, o3, n in zip(pairs_f, got3_f, FFN)]
    gsum_f = _join_layers(us_f, "rs_join_cores_ffn")
    res = {}

    def update(names, grads):
        for n, g in zip(names, grads):
            d_, m_, v_, g_ = _adamw(W[n], g, M1[n], V2[n], f"adamw_{n}", echo=True)
            back = T if n == "w_in" else (lambda a: a)
            res["g", n], res["d", n], res["m", n], res["v", n] = back(g_), back(d_), back(m_), back(v_)

    update(FFN, gsum_f)
    small = TINY + REPL
    full_shapes = [Gl[n].shape for n in small] + [(1,)]
    summed = _allsum_small(_pack([Gl[n] for n in small] + [loss_here]), "allsum_small", after=res["d", FFN[-1]])
    got3_r = _exchange_wait(*in_flight[:-1], summed, "rs_exchange_wait")
    us_r = [_chip_sum(p[0], [o3], kc, f"rs_chip_sum_{n}") for p, o3, n in zip(pairs_r, got3_r, REST)]
    update(REST, _join_layers(us_r, "rs_join_cores"))
    summed = _unpack(summed, full_shapes)
    loss = summed.pop().reshape(())

    def as3(a):
        if a.ndim <= 2:
            return a.reshape((1,) * (3 - a.ndim) + a.shape)
        return a.reshape(a.shape[0], -1, a.shape[-1])

    for n, g in zip(small, summed):
        if n in TINY:
            wdt = W[n].shape[2]
            g = lax.dynamic_slice_in_dim(g, chip * wdt, wdt, axis=2)
        out = _adamw(as3(W[n]), as3(g), as3(M1[n]), as3(V2[n]), f"adamw_{n}")
        res["g", n] = g
        res["d", n], res["m", n], res["v", n] = (o.reshape(W[n].shape) for o in out)

    return (loss, grad_x, *[res["g", n] for n in ORDER], *[res["d", n] for n in ORDER],
            *[res["m", n] for n in ORDER], *[res["v", n] for n in ORDER])
```

```python
import functools
import math

import numpy as np
import jax
import jax.numpy as jnp
from jax import lax
from jax.experimental import pallas as pl
from jax.experimental.pallas import tpu as pltpu

F32, BF16 = jnp.float32, jnp.bfloat16
SDS = jax.ShapeDtypeStruct
BS = pl.BlockSpec
MESH = pl.DeviceIdType.MESH

D = 1024
DEPTH = 2
HEADS = 4
D_FF = 2816
D_INP = 2048
EPS = 1e-6
SWA_WINDOW = 128
BLK = 128
SLOPES = tuple(2.0 ** (-8.0 * (i + 1) / 4) for i in range(4))
QK_SCALE = 1.0 / math.sqrt(96)
SWA_SCALE = 1.0 / math.sqrt(64)
LR, B1, B2, ADAM_EPS, WD, STEP = 0.001, 0.9, 0.999, 1e-08, 0.01, 10

LANES = 1024
VMEM_LIMIT = 56 * 1024 * 1024
NEG_INF = float("-inf")

C_CQ, C_CKV, C_KR, C_GB, C_GC, C_UC, C_UP, C_QS, C_KS, C_VS = 0, 256, 384, 512, 768, 1024, 1280, 1536, 1792, 1920


def _params(ngrid):
    return pltpu.CompilerParams(dimension_semantics=("arbitrary",) * ngrid, vmem_limit_bytes=VMEM_LIMIT)


def _pc(body, *, name, grid, in_specs, out_specs, out_shape, scratch=(), aliases=None):
    return pl.pallas_call(
        body, name=name, grid=grid, in_specs=in_specs, out_specs=out_specs, out_shape=out_shape,
        scratch_shapes=scratch, input_output_aliases=aliases or {}, compiler_params=_params(len(grid)))


def _dot(a, b):
    return jnp.dot(a, b, preferred_element_type=F32)


def _dot_nt(a, b):
    return lax.dot_general(a, b, (((1,), (1,)), ((), ())), preferred_element_type=F32)


def _dot_tn(a, b):
    return lax.dot_general(a, b, (((0,), (0,)), ((), ())), preferred_element_type=F32)


def _tile(n, cap):
    if n <= cap:
        return n
    t = cap - cap % 128
    while n % t:
        t -= 128
    return t


def _row_tile(a, b, cap=262144):
    bp = -(-b // 128) * 128
    best = None
    for t in range(8, a + 1, 8):
        if a % t == 0 and t * bp <= cap:
            best = t
    if best is None or (best < 64 and a * bp <= 2 * cap):
        return a
    return best


def _g3(a):
    return a.reshape(a.shape[0], 1, a.shape[1])


def _norm_mm(x, g3, l, w, wspec, N, tn, out_dtype, name, w_t=False):
    S, K = x.shape
    tm = min(1024 if out_dtype == BF16 else 512, S)

    def body(x_ref, g_ref, w_ref, y_ref, h_ref):
        @pl.when(pl.program_id(1) == 0)
        def _():
            xv = x_ref[...]
            r = lax.rsqrt(jnp.mean(xv * xv, axis=-1, keepdims=True) + EPS)
            h_ref[...] = (xv * r * g_ref[...]).astype(BF16)

        y_ref[...] = (_dot_nt if w_t else _dot)(h_ref[...], w_ref[...]).astype(out_dtype)

    return _pc(body, name=name, grid=(S // tm, N // tn),
               in_specs=[BS((tm, K), lambda i, j: (i, 0)), BS((None, 1, K), lambda i, j: (l, 0, 0)), wspec],
               out_specs=[BS((tm, tn), lambda i, j: (i, j)), BS((tm, K), lambda i, j: (i, 0))],
               out_shape=[SDS((S, N), out_dtype), SDS((S, K), BF16)])(x, g3, w)


def _wspec_in(l):
    return BS((None, D_INP, D), lambda i, j: (l, j, 0))


def _wspec_gu(l):
    return BS((None, None, D, 2 * D_FF // 4), lambda i, j: (l, j, 0, 0))


def _mix_out(x0, ya, yb, yc, yd, gmix3, wo, l, name):
    S = x0.shape[0]
    tm = min(512, S)

    def body(x_ref, ya_ref, yb_ref, yc_ref, yd_ref, g_ref, w_ref, x1_ref, ycat_ref, mixed_ref):
        groups = [ya_ref[...], yb_ref[...], yc_ref[...], yd_ref[...]]
        for gi, yg in enumerate(groups):
            sl = slice(gi * 256, (gi + 1) * 256)
            r = lax.rsqrt(jnp.mean(yg * yg, axis=-1, keepdims=True) + EPS)
            ycat_ref[:, sl] = yg
            mixed_ref[:, sl] = (yg * r * g_ref[:, sl]).astype(BF16)
        x1_ref[...] = x_ref[...] + _dot(mixed_ref[...], w_ref[...])

    row = lambda w: BS((tm, w), lambda i: (i, 0))
    return _pc(body, name=name, grid=(S // tm,),
               in_specs=[row(D), row(256), row(256), row(256), row(256), BS((None, 1, D), lambda i: (l, 0, 0)),
                         BS((None, D, D), lambda i: (l, 0, 0))],
               out_specs=[row(D), row(D), row(D)],
               out_shape=[SDS((S, D), F32), SDS((S, D), F32), SDS((S, D), BF16)])(x0, ya, yb, yc, yd, gmix3, wo)


def _swiglu_mm_res(x1, gu, wdown, l, name):
    S = x1.shape[0]
    tm = min(256, S)

    def body(x_ref, gate_ref, up_ref, w_ref, x2_ref, act_ref):
        acc = x_ref[...]
        for c0 in range(0, D_FF, D_FF // 2):
            cs = slice(c0, c0 + D_FF // 2)
            gt = gate_ref[:, cs].astype(F32)
            act = (gt * pl.reciprocal(1.0 + jnp.exp(-gt), approx=True) * up_ref[:, cs].astype(F32)).astype(BF16)
            act_ref[:, cs] = act
            acc = acc + _dot(act, w_ref[cs, :])
        x2_ref[...] = acc

    return _pc(body, name=name, grid=(S // tm,),
               in_specs=[BS((tm, D), lambda i: (i, 0)), BS((tm, D_FF), lambda i: (i, 0)),
                         BS((tm, D_FF), lambda i: (i, 1)), BS((None, D_FF, D), lambda i: (l, 0, 0))],
               out_specs=[BS((tm, D), lambda i: (i, 0)), BS((tm, D_FF), lambda i: (i, 0))],
               out_shape=[SDS((S, D), F32), SDS((S, D_FF), BF16)])(x1, gu, gu, wdown)


def _loss_head(x, g, tgt, name):
    S = x.shape[0]
    tm = min(512, S)

    def body(x_ref, g_ref, t_ref, dx_ref, dx16_ref, dg_ref, loss_ref):
        @pl.when(pl.program_id(0) == 0)
        def _():
            dg_ref[...] = jnp.zeros_like(dg_ref)
            loss_ref[...] = jnp.zeros_like(loss_ref)

        xv = x_ref[...]
        r = lax.rsqrt(jnp.mean(xv * xv, axis=-1, keepdims=True) + EPS)
        xh = xv * r
        gv = g_ref[...]
        diff = xh * gv - t_ref[...]
        loss_ref[...] += jnp.sum(diff * diff)
        dy = diff * (1.0 / D)
        dg_ref[...] += jnp.sum(dy * xh, axis=0, keepdims=True)
        dxh = dy * gv
        dx = r * (dxh - xh * jnp.mean(dxh * xh, axis=-1, keepdims=True))
        dx_ref[...] = dx
        dx16_ref[...] = dx.astype(BF16)

    row = BS((tm, D), lambda i: (i, 0))
    return _pc(body, name=name, grid=(S // tm,),
               in_specs=[row, BS((1, D), lambda i: (0, 0)), row],
               out_specs=[row, row, BS((8, D), lambda i: (0, 0)), BS((8, 128), lambda i: (0, 0))],
               out_shape=[SDS((S, D), F32), SDS((S, D), BF16), SDS((8, D), F32), SDS((8, 128), F32)])(x, g, tgt)


def _mm_tn(a, b, l, prev, name, split4=False):
    S, Ka = a.shape
    N = b.shape[1]
    if split4:
        ta, tn = _tile(Ka, 256), N // 4
        out_shape = SDS((2, 4, Ka, tn), F32)
        out_spec = BS((None, None, ta, tn), lambda j, i: (l, j, i, 0))
    else:
        ta, tn = _tile(Ka, 512), _tile(N, 1024)
        out_shape = SDS((2, Ka, N), F32)
        out_spec = BS((None, ta, tn), lambda j, i: (l, i, j))

    def body(a_ref, b_ref, *rest):
        rest[-1][...] = _dot_tn(a_ref[...], b_ref[...])

    in_specs = [BS((S, ta), lambda j, i: (0, i)), BS((S, tn), lambda j, i: (0, j))]
    args = [a, b]
    if prev is not None:
        in_specs.append(BS(memory_space=pl.ANY))
        args.append(prev)
    return _pc(body, name=name, grid=(N // tn, Ka // ta), in_specs=in_specs, out_specs=out_spec, out_shape=out_shape,
               aliases={2: 0} if prev is not None else None)(*args)


def _bwd_down(dx16, wdown, gu, l, name):
    S = dx16.shape[0]
    tm = min(256, S)

    def body(dx_ref, w_ref, gate_ref, up_ref, dgu_ref):
        dxv = dx_ref[...]
        for c0 in range(0, D_FF, 256):
            cs = slice(c0, c0 + 256)
            dact = _dot_nt(dxv, w_ref[cs, :])
            gt = gate_ref[:, cs].astype(F32)
            sg = pl.reciprocal(1.0 + jnp.exp(-gt), approx=True)
            dgu_ref[:, cs] = (dact * up_ref[:, cs].astype(F32) * (sg * (1.0 + gt * (1.0 - sg)))).astype(BF16)
            dgu_ref[:, D_FF + c0:D_FF + c0 + 256] = (dact * (gt * sg)).astype(BF16)

    return _pc(body, name=name, grid=(S // tm,),
               in_specs=[BS((tm, D), lambda i: (i, 0)), BS((None, D_FF, D), lambda i: (l, 0, 0)),
                         BS((tm, D_FF), lambda i: (i, 0)), BS((tm, D_FF), lambda i: (i, 1))],
               out_specs=BS((tm, 2 * D_FF), lambda i: (i, 0)),
               out_shape=SDS((S, 2 * D_FF), BF16))(dx16, wdown, gu, gu)


def _mm_nt_normbwd(dy, w4, l, x, g3, dres, ngroups, name, rider=None, w_t=False, after=None):
    S, K = dy.shape
    nk, kc = w4.shape[1], w4.shape[2 if w_t else 3]
    mm = _dot if w_t else _dot_nt
    tm = min(512, S)
    gw = D // ngroups
    has_res = dres is not None
    nr = rider.n if rider else 0
    n_in, n_out = 4 + has_res + (after is not None), 2 + has_res

    def body(*refs):
        dy_ref, w_ref, x_ref, g_ref = refs[:4]
        res_ref = refs[4] if has_res else None
        outs = refs[n_in + nr:n_in + nr + n_out]
        dx_ref, dg_ref = outs[0], outs[-1]
        dx16_ref = outs[1] if has_res else None
        r_io = (refs[n_in:n_in + nr], refs[n_in + nr + n_out:n_in + 2 * nr + n_out], refs[n_in + 2 * nr + n_out:])
        if rider:
            pl.when(pl.program_id(0) == 0)(lambda: rider.start(*r_io))

        @pl.when(pl.program_id(0) == 0)
        def _():
            dg_ref[...] = jnp.zeros_like(dg_ref)

        dh = mm(dy_ref[:, 0:kc], w_ref[0])
        for k in range(1, nk):
            dh = dh + mm(dy_ref[:, k * kc:(k + 1) * kc], w_ref[k])
        for gi in range(ngroups):
            sl = slice(gi * gw, (gi + 1) * gw)
            xg = x_ref[:, sl]
            r = lax.rsqrt(jnp.mean(xg * xg, axis=-1, keepdims=True) + EPS)
            xh = xg * r
            dhg = dh[:, sl]
            dg_ref[:, sl] += jnp.sum(dhg * xh, axis=0, keepdims=True)
            dxh = dhg * g_ref[:, sl]
            dxg = r * (dxh - xh * jnp.mean(dxh * xh, axis=-1, keepdims=True))
            if has_res:
                dxg = dxg + res_ref[:, sl]
                dx16_ref[:, sl] = dxg.astype(BF16)
            dx_ref[:, sl] = dxg
        if rider:
            pl.when(pl.program_id(0) == S // tm - 1)(lambda: rider.wait(*r_io))

    row = BS((tm, D), lambda i: (i, 0))
    in_specs = [BS((tm, K), lambda i: (i, 0)),
                BS((None,) + tuple(w4.shape[1:]), lambda i: (l, 0, 0, 0), pipeline_mode=pl.Buffered(1)), row,
                BS((None, 1, D), lambda i: (l, 0, 0))]
    args = [dy, w4, x, g3]
    out_specs, out_shape = [row], [SDS((S, D), F32)]
    if has_res:
        in_specs.append(row)
        args.append(dres)
        out_specs.append(row)
        out_shape.append(SDS((S, D), BF16))
    if after is not None:
        in_specs.append(BS((8, 128), lambda i: (0, 0)))
        args.append(after)
    out_specs.append(BS((8, D), lambda i: (0, 0)))
    out_shape.append(SDS((8, D), F32))
    if not rider:
        return _pc(body, name=name, grid=(S // tm,), in_specs=in_specs, out_specs=out_specs, out_shape=out_shape)(*args)
    out = pl.pallas_call(body, name=name, grid=(S // tm,), in_specs=in_specs + [ANY] * nr,
                         out_specs=out_specs + [ANY] * nr, out_shape=out_shape + rider.out_shape,
                         scratch_shapes=rider.scratch(),
                         compiler_params=pltpu.CompilerParams(dimension_semantics=("arbitrary",),
                                                              vmem_limit_bytes=VMEM_LIMIT, has_side_effects=True))(
        *args, *rider.arrs)
    return (*out[:n_out], list(out[n_out:]))


def _rope(x, c, s1, s2):
    return x * c + pltpu.roll(x, 112, axis=1) * s1 + pltpu.roll(x, 16, axis=1) * s2


def _rope_t(dy, c, s1, s2):
    return dy * c + pltpu.roll(dy * s1, 16, axis=1) + pltpu.roll(dy * s2, 112, axis=1)


def _mla_prep(proj, gq3, gkv3, wuq, wk, wv, tabs, l, name):
    S = proj.shape[0]
    tm = min(512, S)
    tc, ts1, ts2 = tabs

    def body(cq_ref, ckv_ref, kr_ref, gq_ref, gkv_ref, wuq_ref, wk_ref, wv_ref, c_ref, s1_ref, s2_ref,
             q_ref, k_ref, v_ref, kt_ref, vt_ref):
        c, s1, s2 = c_ref[...], s1_ref[...], s2_ref[...]
        cq = cq_ref[...]
        rq = lax.rsqrt(jnp.mean(cq * cq, axis=-1, keepdims=True) + EPS)
        qa = _dot((cq * rq * gq_ref[...]).astype(BF16), wuq_ref[...])
        ckv = ckv_ref[...]
        rkv = lax.rsqrt(jnp.mean(ckv * ckv, axis=-1, keepdims=True) + EPS)
        ckvn = (ckv * rkv * gkv_ref[...]).astype(BF16)
        ka = _dot(ckvn, wk_ref[...])
        va = _dot(ckvn, wv_ref[...])
        v_ref[...] = va.astype(BF16)
        vt_ref[...] = va.T.astype(BF16)
        krr = _rope(kr_ref[...], c, s1, s2)
        for h in range(HEADS):
            sl = slice(h * 128, (h + 1) * 128)
            q_ref[:, sl] = (_rope(qa[:, sl], c, s1, s2) * QK_SCALE).astype(BF16)
            kh = ka[:, sl] + krr
            k_ref[:, sl] = kh.astype(BF16)
            kt_ref[sl, :] = kh.T.astype(BF16)

    lay = lambda a, b: BS((None, a, b), lambda i: (l, 0, 0))
    tab = BS((tm, 128), lambda i: (i, 0))
    return _pc(body, name=name, grid=(S // tm,),
               in_specs=[BS((tm, 256), lambda i: (i, 0)), BS((tm, 128), lambda i: (i, 2)), BS((tm, 128), lambda i: (i, 3)),
                         lay(1, 256), lay(1, 128), lay(256, 512), lay(128, 512), lay(128, 512), tab, tab, tab],
               out_specs=[BS((tm, 512), lambda i: (i, 0))] * 3 + [BS((512, tm), lambda i: (0, i))] * 2,
               out_shape=[SDS((S, 512), BF16)] * 3 + [SDS((512, S), BF16)] * 2)(
        proj, proj, proj, gq3, gkv3, wuq, wk, wv, tc, ts1, ts2)


def _mla_prep_bwd(dq, dk, dv, proj, gq3, gkv3, wuq, wk, wv, tabs, l, name):
    S = proj.shape[0]
    tm = min(512, S)
    tc, ts1, ts2 = tabs

    def body(dq_ref, dk_ref, dv_ref, cq_ref, ckv_ref, gq_ref, gkv_ref, wuq_ref, wk_ref, wv_ref, c_ref, s1_ref, s2_ref,
             dcq_ref, dckv_ref, dkr_ref, dwuq_ref, dwk_ref, dwv_ref, dgq_ref, dgkv_ref):
        @pl.when(pl.program_id(0) == 0)
        def _():
            for r in (dwuq_ref, dwk_ref, dwv_ref, dgq_ref, dgkv_ref):
                r[...] = jnp.zeros_like(r)

        c, s1, s2 = c_ref[...], s1_ref[...], s2_ref[...]
        dqp = jnp.concatenate(
            [_rope_t(dq_ref[h * 128:(h + 1) * 128, :].T * QK_SCALE, c, s1, s2) for h in range(HEADS)], axis=1).astype(BF16)
        cq = cq_ref[...]
        rq = lax.rsqrt(jnp.mean(cq * cq, axis=-1, keepdims=True) + EPS)
        cqh = cq * rq
        gq_v = gq_ref[...]
        dwuq_ref[...] += _dot_tn((cqh * gq_v).astype(BF16), dqp)
        dcqn = _dot_nt(dqp, wuq_ref[...])
        dgq_ref[...] += jnp.sum(dcqn * cqh, axis=0, keepdims=True)
        dxh = dcqn * gq_v
        dcq_ref[...] = (rq * (dxh - cqh * jnp.mean(dxh * cqh, axis=-1, keepdims=True))).astype(BF16)

        dkb = dk_ref[...].astype(BF16)
        dvb = dv_ref[...].astype(BF16)
        ckv = ckv_ref[...]
        rkv = lax.rsqrt(jnp.mean(ckv * ckv, axis=-1, keepdims=True) + EPS)
        ckh = ckv * rkv
        gkv_v = gkv_ref[...]
        ckvn = (ckh * gkv_v).astype(BF16)
        dwk_ref[...] += _dot_tn(ckvn, dkb)
        dwv_ref[...] += _dot_tn(ckvn, dvb)
        dckvn = _dot_nt(dkb, wk_ref[...]) + _dot_nt(dvb, wv_ref[...])
        dgkv_ref[...] += jnp.sum(dckvn * ckh, axis=0, keepdims=True)
        dyh = dckvn * gkv_v
        dckv_ref[...] = (rkv * (dyh - ckh * jnp.mean(dyh * ckh, axis=-1, keepdims=True))).astype(BF16)
        dks = dk_ref[:, 0:128] + dk_ref[:, 128:256] + dk_ref[:, 256:384] + dk_ref[:, 384:512]
        dkr_ref[...] = _rope_t(dks, c, s1, s2).astype(BF16)

    full = lambda a, b: BS((a, b), lambda i: (0, 0))
    lay = lambda a, b: BS((None, a, b), lambda i: (l, 0, 0))
    tab = BS((tm, 128), lambda i: (i, 0))
    row = lambda w: BS((tm, w), lambda i: (i, 0))
    return _pc(body, name=name, grid=(S // tm,),
               in_specs=[BS((512, tm), lambda i: (0, i)), row(512), row(512), BS((tm, 256), lambda i: (i, 0)),
                         BS((tm, 128), lambda i: (i, 2)),
                         lay(1, 256), lay(1, 128), lay(256, 512), lay(128, 512), lay(128, 512), tab, tab, tab],
               out_specs=[row(256), row(128), row(128), full(256, 512), full(128, 512), full(128, 512),
                          full(8, 256), full(8, 128)],
               out_shape=[SDS((S, 256), BF16), SDS((S, 128), BF16), SDS((S, 128), BF16), SDS((256, 512), F32),
                          SDS((128, 512), F32), SDS((128, 512), F32), SDS((8, 256), F32), SDS((8, 128), F32)])(
        dq, dk, dv, proj, proj, gq3, gkv3, wuq, wk, wv, tc, ts1, ts2)


def _causal_steps(n, q_outer):
    if q_outer:
        pairs = [(i, j) for i in range(n) for j in range(i + 1)]
    else:
        pairs = [(i, j) for j in range(n) for i in range(j, n)]
    return jnp.asarray([p[0] for p in pairs], jnp.int32), jnp.asarray([p[1] for p in pairs], jnp.int32)


def _mla_attn(q, k, vt, gts, layer, name):
    S = q.shape[0]
    t = min(512, S)
    n = S // t
    ng = len(gts)

    qi, kj = _causal_steps(n, True)
    last = qi.shape[0] - 1

    def body(qi_ref, kj_ref, q_ref, k_ref, vt_ref, *rest):
        (ya_ref, lse_ref), g_refs = rest[ng:ng + 2], rest[ng + 2:2 * ng + 2]
        m_sc, l_sc, acc_sc = rest[2 * ng + 2:2 * ng + 5]
        i, j = qi_ref[pl.program_id(1)], kj_ref[pl.program_id(1)]
        if ng:
            phases = _gather_phases(g_refs, [g.shape for g in gts], rest[2 * ng + 5], rest[2 * ng + 6], layer)
            for ph, (pp, ss) in zip(phases[:3], ((0, 0), (1, 0), (1, 2 * last // 3))):
                pl.when((pl.program_id(0) == pp) & (pl.program_id(1) == ss))(ph)

        @pl.when(j == 0)
        def _():
            m_sc[...] = jnp.full_like(m_sc, NEG_INF)
            l_sc[...] = jnp.zeros_like(l_sc)
            acc_sc[...] = jnp.zeros_like(acc_sc)

        def step(masked):
            for hh in range(2):
                sl = slice(hh * 128, (hh + 1) * 128)
                st = _dot_nt(k_ref[:, sl], q_ref[:, sl])
                if masked:
                    key = lax.broadcasted_iota(jnp.int32, (t, t), 0)
                    qry = lax.broadcasted_iota(jnp.int32, (t, t), 1)
                    st = jnp.where(key <= qry, st, NEG_INF)
                m_prev = m_sc[hh]
                m_new = jnp.maximum(m_prev, jnp.max(st, axis=0, keepdims=True))
                p = jnp.exp(st - m_new)
                alpha = jnp.exp(m_prev - m_new)
                l_sc[hh] = alpha * l_sc[hh] + jnp.sum(p, axis=0, keepdims=True)
                acc_sc[hh] = alpha * acc_sc[hh] + _dot(vt_ref[sl, :], p.astype(BF16))
                m_sc[hh] = m_new

        @pl.when(j < i)
        def _():
            step(False)

        @pl.when(j == i)
        def _():
            step(True)
            ya_ref[...] = (acc_sc[0] / l_sc[0] + acc_sc[1] / l_sc[1]).T
            for hh in range(2):
                lse_ref[hh] = m_sc[hh] + jnp.log(l_sc[hh])

        if ng:
            pl.when((pl.program_id(0) == 1) & (pl.program_id(1) == last))(phases[3])

    gs = pltpu.PrefetchScalarGridSpec(
        num_scalar_prefetch=2, grid=(2, qi.shape[0]),
        in_specs=[BS((t, 256), lambda p, s, qi, kj: (qi[s], p)), BS((t, 256), lambda p, s, qi, kj: (kj[s], p)),
                  BS((256, t), lambda p, s, qi, kj: (p, kj[s]))] + [ANY] * ng,
        out_specs=[BS((t, 128), lambda p, s, qi, kj: (qi[s], p)), BS((2, 1, t), lambda p, s, qi, kj: (p, 0, qi[s]))]
        + [ANY] * ng,
        scratch_shapes=[pltpu.VMEM((2, 1, t), F32), pltpu.VMEM((2, 1, t), F32), pltpu.VMEM((2, 128, t), F32)]
        + ([pltpu.SemaphoreType.DMA((7 * ng,)), pltpu.SemaphoreType.DMA((7 * ng,))] if ng else []))
    out = pl.pallas_call(body, name=name, grid_spec=gs,
                         out_shape=[SDS((S, 256), F32), SDS((HEADS, 1, S), F32)] + [SDS(g.shape, g.dtype) for g in gts],
                         input_output_aliases={5 + m: 2 + m for m in range(ng)},
                         compiler_params=pltpu.CompilerParams(dimension_semantics=("arbitrary",) * 2,
                                                              vmem_limit_bytes=VMEM_LIMIT, has_side_effects=bool(ng)))(
        qi, kj, q, k, vt, *gts)
    return out[0], out[1], list(out[2:])


def _mla_delta(dycat, ya, name, after=None):
    S = ya.shape[0]
    t = min(512, S)

    def body(do_ref, ya_ref, *refs):
        d_ref = refs[-1]
        prod = do_ref[...] * ya_ref[...]
        for p in range(2):
            pt = prod[:, p * 128:(p + 1) * 128].T
            d_ref[2 * p] = jnp.sum(pt[0:64, :], axis=0, keepdims=True)
            d_ref[2 * p + 1] = jnp.sum(pt[64:128, :], axis=0, keepdims=True)

    extra = [] if after is None else [after]
    return _pc(body, name=name, grid=(S // t,),
               in_specs=[BS((t, 256), lambda i: (i, 0)), BS((t, 256), lambda i: (i, 0))]
               + [BS((8, 128), lambda i: (0, 0)) for _ in extra],
               out_specs=BS((HEADS, 1, t), lambda i: (0, 0, i)), out_shape=SDS((HEADS, 1, S), F32))(dycat, ya, *extra)


def _mla_attn_bwd(q, k, kt, v, dya, lse, delta, rider, name):
    S = q.shape[0]
    t = min(512, S)
    n = S // t
    nr = rider.n if rider else 0

    qi, kj = _causal_steps(n, False)
    last = qi.shape[0] - 1

    def body(qi_ref, kj_ref, q_ref, k_ref, kt_ref, v_ref, do_ref, lse_ref, delta_ref, *rest):
        dqt_ref, dk_ref, dv_ref = rest[nr:nr + 3]
        r_io = (rest[:nr], rest[nr + 3:2 * nr + 3], rest[2 * nr + 3:])
        i, j = qi_ref[pl.program_id(1)], kj_ref[pl.program_id(1)]
        if rider:
            pl.when((pl.program_id(0) == 0) & (pl.program_id(1) == 0))(lambda: rider.start(*r_io))

        @pl.when(pl.program_id(1) == 0)
        def _():
            dqt_ref[...] = jnp.zeros_like(dqt_ref)

        @pl.when(i == j)
        def _():
            dk_ref[...] = jnp.zeros_like(dk_ref)
            dv_ref[...] = jnp.zeros_like(dv_ref)

        def step(masked):
            dob = do_ref[...].astype(BF16)
            cols = pl.ds(pl.multiple_of(i * t, t), t)
            for hh in range(2):
                sl = slice(hh * 128, (hh + 1) * 128)
                qv = q_ref[:, sl]
                p = jnp.exp(_dot_nt(k_ref[:, sl], qv) - lse_ref[hh])
                if masked:
                    key = lax.broadcasted_iota(jnp.int32, (t, t), 0)
                    qry = lax.broadcasted_iota(jnp.int32, (t, t), 1)
                    p = jnp.where(key <= qry, p, 0.0)
                dv_ref[:, sl] += _dot(p.astype(BF16), dob)
                ds = (p * (_dot_nt(v_ref[:, sl], dob) - delta_ref[hh])).astype(BF16)
                dk_ref[:, sl] += _dot(ds, qv)
                dqt_ref[sl, cols] += _dot(kt_ref[sl, :], ds)

        @pl.when(i > j)
        def _():
            step(False)

        @pl.when(i == j)
        def _():
            step(True)

        if rider:
            pl.when((pl.program_id(0) == 1) & (pl.program_id(1) == last))(lambda: rider.wait(*r_io))

    qs = BS((t, 256), lambda p, s, qi, kj: (qi[s], p))
    ks = BS((t, 256), lambda p, s, qi, kj: (kj[s], p))
    rowv = BS((2, 1, t), lambda p, s, qi, kj: (p, 0, qi[s]))
    gs = pltpu.PrefetchScalarGridSpec(
        num_scalar_prefetch=2, grid=(2, qi.shape[0]),
        in_specs=[qs, ks, BS((256, t), lambda p, s, qi, kj: (p, kj[s])), ks,
                  BS((t, 128), lambda p, s, qi, kj: (qi[s], p)), rowv, rowv] + [ANY] * nr,
        out_specs=[BS((256, S), lambda p, s, qi, kj: (p, 0)), ks, ks] + [ANY] * nr,
        scratch_shapes=rider.scratch() if rider else [])
    out = pl.pallas_call(body, name=name, grid_spec=gs,
                         out_shape=[SDS((512, S), F32), SDS((S, 512), F32), SDS((S, 512), F32)]
                         + (rider.out_shape if rider else []),
                         compiler_params=pltpu.CompilerParams(dimension_semantics=("arbitrary",) * 2,
                                                              vmem_limit_bytes=VMEM_LIMIT, has_side_effects=bool(rider)))(
        qi, kj, q, k, kt, v, dya, lse, delta, *(rider.arrs if rider else []))
    return out[0], out[1], out[2], list(out[3:])


def _swa_scores(qm, kk, valid, bias, sink):
    sc = jnp.where(valid, _dot_nt(qm, kk) * SWA_SCALE + bias, NEG_INF)
    m = jnp.maximum(jnp.max(sc, axis=-1, keepdims=True), sink)
    e = jnp.exp(sc - m)
    esink = jnp.exp(sink - m)
    den = jnp.sum(e, axis=-1, keepdims=True) + esink
    return e / den, esink / den


def _swa_consts(sink_ref, l):
    rows = HEADS * BLK
    r = lax.broadcasted_iota(jnp.int32, (rows, 2 * BLK), 0)
    c = lax.broadcasted_iota(jnp.int32, (rows, 2 * BLK), 1)
    dist = (r & (BLK - 1)) + BLK - c
    head = lax.broadcasted_iota(jnp.int32, (rows, 1), 0) // BLK

    def per_head(vals):
        return jnp.where(head == 0, vals[0], jnp.where(head == 1, vals[1], jnp.where(head == 2, vals[2], vals[3])))

    bias = -per_head(SLOPES) * dist.astype(F32)
    sink = per_head([sink_ref[l, h] for h in range(HEADS)])
    return (dist >= 0) & (dist < SWA_WINDOW), c >= BLK, bias, sink, head


def _to_half(xb, pos, b):
    return xb if pos == b else pltpu.roll(xb, 64, axis=1)


def _swa_stack(ref, st, lo):
    parts = []
    for b in range(2):
        xb = ref[pl.ds(st, BLK), b * 128:(b + 1) * 128]
        half = lo if b == 0 else ~lo
        parts += [jnp.where(half, _to_half(xb, pos, b), 0.0).astype(BF16) for pos in range(2)]
    return jnp.concatenate(parts, axis=0)


def _swa_unstack(x_all, lo):
    blocks = []
    for b in range(2):
        h0, h1 = (_to_half(x_all[(2 * b + pos) * BLK:(2 * b + pos + 1) * BLK], pos, b) for pos in range(2))
        blocks.append(jnp.where(lo, h0, h1))
    return blocks


def _swa(proj, sinks, l, name):
    S = proj.shape[0]
    nb = S // BLK

    def body(q_ref, k_ref, v_ref, sink_ref, o_ref, kp, vp):
        kp[0:BLK, :] = jnp.zeros((BLK, 128), BF16)
        vp[0:BLK, :] = jnp.zeros((BLK, 128), BF16)
        kp[BLK:, :] = k_ref[...].astype(BF16)
        vp[BLK:, :] = v_ref[...].astype(BF16)
        lo = lax.broadcasted_iota(jnp.int32, (BLK, 128), 1) < 64
        band, cur, bias, sink, _ = _swa_consts(sink_ref, l)

        def blk(i, carry):
            st = pl.multiple_of(i * BLK, BLK)
            kk = kp[pl.ds(st, 2 * BLK), :]
            vv = vp[pl.ds(st, 2 * BLK), :]
            p, _ = _swa_scores(_swa_stack(q_ref, st, lo), kk, band & (cur | (i > 0)), bias, sink)
            for b, ob in enumerate(_swa_unstack(_dot(p.astype(BF16), vv), lo)):
                o_ref[pl.ds(st, BLK), b * 128:(b + 1) * 128] = ob
            return carry

        lax.fori_loop(0, nb, blk, 0, unroll=2)

    return _pc(body, name=name, grid=(1,),
               in_specs=[BS((S, 256), lambda i: (0, C_QS // 256)), BS((S, 128), lambda i: (0, C_KS // 128)),
                         BS((S, 128), lambda i: (0, C_VS // 128)), BS(memory_space=pltpu.SMEM)],
               out_specs=BS((S, 256), lambda i: (0, 0)),
               out_shape=SDS((S, 256), F32),
               scratch=[pltpu.VMEM((S + BLK, 128), BF16), pltpu.VMEM((S + BLK, 128), BF16)])(proj, proj, proj, sinks)


def _swa_bwd(proj, sinks, dyd, l, name, rider=None):
    S = proj.shape[0]
    nb = S // BLK
    nr = rider.n if rider else 0

    def body(q_ref, k_ref, v_ref, sink_ref, do_ref, *rest):
        dq_ref, dk_ref, dv_ref, dsink_ref = rest[nr:nr + 4]
        kp, vp, dkp, dvp = rest[2 * nr + 4:2 * nr + 8]
        r_io = (rest[:nr], rest[nr + 4:2 * nr + 4], rest[2 * nr + 8:])
        if rider:
            rider.start(*r_io)
        kp[0:BLK, :] = jnp.zeros((BLK, 128), BF16)
        vp[0:BLK, :] = jnp.zeros((BLK, 128), BF16)
        kp[BLK:, :] = k_ref[...].astype(BF16)
        vp[BLK:, :] = v_ref[...].astype(BF16)
        dkp[...] = jnp.zeros_like(dkp)
        dvp[...] = jnp.zeros_like(dvp)
        lo = lax.broadcasted_iota(jnp.int32, (BLK, 128), 1) < 64
        lane8 = lax.broadcasted_iota(jnp.int32, (8, 128), 1)
        band, cur, bias, sink, head = _swa_consts(sink_ref, l)

        def blk(i, dsink):
            st = pl.multiple_of(i * BLK, BLK)
            kk = kp[pl.ds(st, 2 * BLK), :]
            vv = vp[pl.ds(st, 2 * BLK), :]
            qm, dom = _swa_stack(q_ref, st, lo), _swa_stack(do_ref, st, lo)
            p, psink = _swa_scores(qm, kk, band & (cur | (i > 0)), bias, sink)
            dp = _dot_nt(dom, vv)
            dvp[pl.ds(st, 2 * BLK), :] += _dot_tn(p.astype(BF16), dom)
            delta = jnp.sum(p * dp, axis=-1, keepdims=True)
            dsk = -psink * delta
            for h in range(HEADS):
                dsink = dsink + jnp.where(lane8 == h, jnp.sum(jnp.where(head == h, dsk, 0.0)), 0.0)
            dsc = (p * (dp - delta) * SWA_SCALE).astype(BF16)
            for b, dqb in enumerate(_swa_unstack(_dot(dsc, kk), lo)):
                dq_ref[pl.ds(st, BLK), b * 128:(b + 1) * 128] = dqb.astype(BF16)
            dkp[pl.ds(st, 2 * BLK), :] += _dot_tn(dsc, qm)
            return dsink

        dsink_ref[...] = lax.fori_loop(0, nb, blk, jnp.zeros((8, 128), F32), unroll=2)
        dk_ref[...] = dkp[BLK:, :].astype(BF16)
        dv_ref[...] = dvp[BLK:, :].astype(BF16)
        if rider:
            rider.wait(*r_io)

    in_specs = [BS((S, 256), lambda i: (0, C_QS // 256)), BS((S, 128), lambda i: (0, C_KS // 128)),
                BS((S, 128), lambda i: (0, C_VS // 128)), BS(memory_space=pltpu.SMEM), BS((S, 256), lambda i: (0, 3))]
    out_specs = [BS((S, 256), lambda i: (0, 0)), BS((S, 128), lambda i: (0, 0)), BS((S, 128), lambda i: (0, 0)),
                 BS((8, 128), lambda i: (0, 0))]
    out_shape = [SDS((S, 256), BF16), SDS((S, 128), BF16), SDS((S, 128), BF16), SDS((8, 128), F32)]
    scratch = [pltpu.VMEM((S + BLK, 128), BF16), pltpu.VMEM((S + BLK, 128), BF16),
               pltpu.VMEM((S + BLK, 128), F32), pltpu.VMEM((S + BLK, 128), F32)]
    if not rider:
        return _pc(body, name=name, grid=(1,), in_specs=in_specs, out_specs=out_specs, out_shape=out_shape,
                   scratch=scratch)(proj, proj, proj, sinks, dyd)
    out = pl.pallas_call(body, name=name, grid=(1,), in_specs=in_specs + [ANY] * nr, out_specs=out_specs + [ANY] * nr,
                         out_shape=out_shape + rider.out_shape, scratch_shapes=scratch + rider.scratch(),
                         compiler_params=pltpu.CompilerParams(dimension_semantics=("arbitrary",),
                                                              vmem_limit_bytes=VMEM_LIMIT, has_side_effects=True))(
        proj, proj, proj, sinks, dyd, *rider.arrs)
    return (*out[:4], list(out[4:]))


def _down(x, k, t):
    return jnp.where(t >= k, pltpu.roll(x, k, axis=0), 0.0)


def _up(x, k, t):
    n = x.shape[0]
    return jnp.where(t < n - k, pltpu.roll(x, n - k, axis=0), 0.0)


def _conv(proj, w8, l, name):
    S = proj.shape[0]

    def body(gb_ref, gc_ref, u_ref, w_ref, y_ref):
        t = lax.broadcasted_iota(jnp.int32, (S, 128), 0)
        z = gc_ref[...] * u_ref[...]
        c = w_ref[2:3, :] * z + w_ref[1:2, :] * _down(z, 1, t) + w_ref[0:1, :] * _down(z, 2, t)
        y_ref[...] = gb_ref[...] * c

    col = lambda c0: BS((S, 128), lambda i: (0, c0 // 128 + i))
    return _pc(body, name=name, grid=(2,),
               in_specs=[col(C_GB), col(C_GC), col(C_UC), BS((None, 8, 128), lambda i: (l, 0, i))],
               out_specs=BS((S, 128), lambda i: (0, i)), out_shape=SDS((S, 256), F32))(proj, proj, proj, w8)


def _conv_bwd(proj, w8, dycat, l, name):
    S = proj.shape[0]

    def body(gb_ref, gc_ref, u_ref, w_ref, dy_ref, dgb_ref, dgc_ref, du_ref, dw_ref):
        t = lax.broadcasted_iota(jnp.int32, (S, 128), 0)
        gc, u = gc_ref[...], u_ref[...]
        z = gc * u
        z1, z2 = _down(z, 1, t), _down(z, 2, t)
        w0, w1, w2 = w_ref[0:1, :], w_ref[1:2, :], w_ref[2:3, :]
        dy = dy_ref[...]
        dgb_ref[...] = (dy * (w2 * z + w1 * z1 + w0 * z2)).astype(BF16)
        dc = dy * gb_ref[...]
        dz = w2 * dc + w1 * _up(dc, 1, t) + w0 * _up(dc, 2, t)
        dgc_ref[...] = (dz * u).astype(BF16)
        du_ref[...] = (dz * gc).astype(BF16)
        row = lax.broadcasted_iota(jnp.int32, (8, 128), 0)
        sums = [jnp.sum(dc * zz, axis=0, keepdims=True) for zz in (z2, z1, z)]
        dw_ref[...] = jnp.where(row == 0, sums[0], jnp.where(row == 1, sums[1], jnp.where(row == 2, sums[2], 0.0)))

    col = lambda c0: BS((S, 128), lambda i: (0, c0 // 128 + i))
    out = BS((S, 128), lambda i: (0, i))
    return _pc(body, name=name, grid=(2,),
               in_specs=[col(C_GB), col(C_GC), col(C_UC), BS((None, 8, 128), lambda i: (l, 0, i)), col(256)],
               out_specs=[out, out, out, BS((8, 128), lambda i: (0, i))],
               out_shape=[SDS((S, 256), BF16)] * 3 + [SDS((8, 256), F32)])(proj, proj, proj, w8, dycat)


def _pool_parts(u, t, first):
    lo = lax.broadcasted_iota(jnp.int32, u.shape, 1) < 64
    s2 = u + _down(u, 1, t)
    s4 = s2 + _down(s2, 2, t)
    s8 = s4 + _down(s4, 4, t)
    s16 = s8 + _down(s8, 8, t)
    win = jnp.where(lo, jnp.where(first, s2, s8), jnp.where(first, s4, s16))
    wv = jnp.where(lo, jnp.where(first, 2, 8), jnp.where(first, 4, 16))
    cnt = jnp.minimum(t + 1, wv).astype(F32)
    return win, cnt, lo


def _pool(proj, pwd, scale3, l, name):
    S = proj.shape[0]

    def body(u_ref, pw_ref, sc_ref, y_ref):
        t = lax.broadcasted_iota(jnp.int32, (S, 128), 0)
        u = u_ref[...]
        win, cnt, _ = _pool_parts(u, t, pl.program_id(0) == 0)
        pooled = win / cnt - u
        y_ref[...] = _dot(pooled.astype(BF16), pw_ref[...]) * sc_ref[...]

    return _pc(body, name=name, grid=(2,),
               in_specs=[BS((S, 128), lambda i: (0, C_UP // 128 + i)), BS((None, 128, 128), lambda i: (l, i, 0)),
                         BS((None, 1, 128), lambda i: (l, 0, i))],
               out_specs=BS((S, 128), lambda i: (0, i)), out_shape=SDS((S, 256), F32))(proj, pwd, scale3)


def _pool_bwd(proj, pwd, scale3, dycat, l, name):
    S = proj.shape[0]

    def body(u_ref, pw_ref, sc_ref, dy_ref, du_ref, dpw_ref, dsc_ref):
        t = lax.broadcasted_iota(jnp.int32, (S, 128), 0)
        first = pl.program_id(0) == 0
        u = u_ref[...]
        win, cnt, lo = _pool_parts(u, t, first)
        pooled = (win / cnt - u).astype(BF16)
        pw = pw_ref[...]
        dy = dy_ref[...]
        dsc_ref[...] = jnp.broadcast_to(jnp.sum(dy * _dot(pooled, pw), axis=0, keepdims=True), (8, 128))
        dmb = (dy * sc_ref[...]).astype(BF16)
        dpw_ref[...] = _dot_tn(pooled, dmb)
        dpooled = _dot_nt(dmb, pw)
        a1 = dpooled / cnt
        a2 = a1 + _up(a1, 1, t)
        a4 = a2 + _up(a2, 2, t)
        a8 = a4 + _up(a4, 4, t)
        a16 = a8 + _up(a8, 8, t)
        dwin = jnp.where(lo, jnp.where(first, a2, a8), jnp.where(first, a4, a16))
        du_ref[...] = (dwin - dpooled).astype(BF16)

    return _pc(body, name=name, grid=(2,),
               in_specs=[BS((S, 128), lambda i: (0, C_UP // 128 + i)), BS((None, 128, 128), lambda i: (l, i, 0)),
                         BS((None, 1, 128), lambda i: (l, 0, i)), BS((S, 128), lambda i: (0, 4 + i))],
               out_specs=[BS((S, 128), lambda i: (0, i)), BS((128, 128), lambda i: (i, 0)), BS((8, 128), lambda i: (0, i))],
               out_shape=[SDS((S, 256), BF16), SDS((256, 128), F32), SDS((8, 256), F32)])(proj, pwd, scale3, dycat)


def _adamw(w, g, m, v, name, echo=False):
    n, a, b = w.shape
    tr = _row_tile(a, b)

    def body(w_ref, g_ref, m_ref, v_ref, d_ref, nm_ref, nv_ref, *g_out):
        gv = g_ref[...]
        if echo:
            g_out[0][...] = gv
        m_new = B1 * m_ref[...] + (1.0 - B1) * gv
        v_new = B2 * v_ref[...] + (1.0 - B2) * (gv * gv)
        m_hat = m_new / (1.0 - B1 ** STEP)
        v_hat = v_new / (1.0 - B2 ** STEP)
        d_ref[...] = -LR * (m_hat / (jnp.sqrt(v_hat) + ADAM_EPS) + WD * w_ref[...])
        nm_ref[...] = m_new
        nv_ref[...] = v_new

    sp = BS((None, tr, b), lambda i, t: (i, t, 0))
    return _pc(body, name=name, grid=(n, a // tr), in_specs=[sp] * 4, out_specs=[sp] * (3 + echo),
               out_shape=[SDS((n, a, b), F32)] * (3 + echo))(w, g, m, v)


def _prefetch_call(body, name, grid, in_specs, out_specs, out_shape):
    gs = pltpu.PrefetchScalarGridSpec(num_scalar_prefetch=1, grid=grid, in_specs=in_specs, out_specs=out_specs)
    return pl.pallas_call(body, name=name, grid_spec=gs, out_shape=out_shape, compiler_params=_params(len(grid)))


def _place(w, kc, dtype, name):
    _, a, b = w.shape

    def body(kc_ref, w_ref, o_ref):
        o_ref[...] = w_ref[...].astype(dtype)

    return _prefetch_call(body, name, (2,), [BS((None, a, b), lambda l, kc: (l, 0, 0))],
                          BS((None, None, a, b), lambda l, kc: (l, kc[0], 0, 0)), SDS((2, 4, a, b), dtype))(kc, w)


def _pair_sum(g, got, kc, name):
    _, _, a, b = g.shape
    tr = _row_tile(a, b)

    def body(kc_ref, a_ref, b_ref, t32_ref, t16_ref):
        s = a_ref[...] + b_ref[...]
        t16_ref[...] = s.astype(BF16)

        @pl.when(pl.program_id(1) == kc_ref[0])
        def _():
            t32_ref[...] = s

    sp = BS((None, tr, b), lambda t, k, kc: (k, t, 0))
    return _prefetch_call(body, name, (a // tr, 4),
                          [BS((None, None, tr, b), lambda t, k, kc: (kc[1], k, t, 0)), sp],
                          [BS((tr, b), lambda t, k, kc: (t, 0)), sp],
                          [SDS((a, b), F32), SDS((4, a, b), BF16)])(kc, g, got)


def _chip_sum(t32, gots, kc, name, after=None):
    a, b = t32.shape
    tr = _row_tile(a, b)
    ng = len(gots)

    def body(kc_ref, a_ref, *refs):
        acc = a_ref[...]
        for g_ref in refs[:ng]:
            for i in range(g_ref.shape[0]):
                acc = acc + g_ref[i].astype(F32)
        refs[-1][...] = acc

    extra = [] if after is None else [after]
    return _prefetch_call(body, name, (a // tr,),
                          [BS((tr, b), lambda t, kc: (t, 0))]
                          + [BS((g.shape[0], tr, b), lambda t, kc: (0, t, 0)) for g in gots]
                          + [BS((8, 128), lambda t, kc: (0, 0)) for _ in extra],
                          BS((None, tr, b), lambda t, kc: (kc[1], t, 0)), SDS((2, a, b), F32))(kc, t32, *gots, *extra)


def _me():
    return lax.axis_index("x"), lax.axis_index("y"), lax.axis_index("c")


def _other_chips(x, y):
    return [(1 - x, y), (x, 1 - y), (1 - x, 1 - y)]


ANY = BS(memory_space=pl.ANY)
COMM_PARAMS = pltpu.CompilerParams(has_side_effects=True)


def _gather(arrs, name):
    n = len(arrs)

    def body(*refs):
        for phase in _gather_phases(refs[n:2 * n], [a.shape for a in arrs], refs[2 * n], refs[2 * n + 1]):
            phase()

    return pl.pallas_call(body, name=name, out_shape=[SDS(a.shape, a.dtype) for a in arrs],
                          in_specs=[ANY] * n, out_specs=[ANY] * n, input_output_aliases={t: t for t in range(n)},
                          scratch_shapes=[pltpu.SemaphoreType.DMA((7 * n,)), pltpu.SemaphoreType.DMA((7 * n,))],
                          compiler_params=COMM_PARAMS)(*arrs)


def _gather_phases(outs, shapes, send_sems, recv_sems, layer=None):
    n = len(outs)
    cut = [s[2] // 2 // 16 * 16 for s in shapes]
    split = [r > 0 for r in cut]

    def plan():
        x, y, c = _me()
        return (c if layer is None else layer), (x, y), (x, y, c), (x, y, 1 - c), _other_chips(x, y)

    def role(moving, fn):
        if layer is None:
            fn()
        else:
            c = lax.axis_index("c")
            pl.when((c == layer) if moving else (c != layer))(fn)

    def blk(t, chip, layer, half=None):
        r = outs[t].at[layer, 2 * chip[0] + chip[1]]
        if half is None:
            return r
        return r.at[pl.ds(0, cut[t])] if half == 0 else r.at[pl.ds(cut[t], shapes[t][2] - cut[t])]

    def copy(t, k, ref, to):
        return pltpu.make_async_remote_copy(src_ref=ref, dst_ref=ref, send_sem=send_sems.at[7 * t + k],
                                            recv_sem=recv_sems.at[7 * t + k], device_id=to, device_id_type=MESH)

    def own_sends(t):
        c, chip, me, sib, (xn, yn, dg) = plan()
        cps = [copy(t, 0, blk(t, chip, c), (*xn, c)), copy(t, 1, blk(t, chip, c), (*yn, c))]
        return cps if split[t] else cps + [copy(t, 2, blk(t, chip, c), (*dg, c))]

    def relays(t):
        c, chip, me, sib, (xn, yn, dg) = plan()
        after_x = [copy(t, 4, blk(t, xn, c), sib)]
        after_y = [copy(t, 5, blk(t, yn, c), sib)]
        if split[t]:
            after_x.insert(0, copy(t, 2, blk(t, xn, c, 0), (*yn, c)))
            after_y.insert(0, copy(t, 3, blk(t, yn, c, 1), (*xn, c)))
        return after_x, after_y, [copy(t, 6, blk(t, dg, c), sib)]

    def send_own():
        for t in range(n):
            for cp in own_sends(t):
                cp.start()

    def relay_neighbours():
        c, chip, me, sib, (xn, yn, dg) = plan()
        for t in range(n):
            after_x, after_y, _ = relays(t)
            copy(t, 0, blk(t, xn, c), me).wait_recv()
            for cp in after_x:
                cp.start()
            copy(t, 1, blk(t, yn, c), me).wait_recv()
            for cp in after_y:
                cp.start()

    def relay_diagonal():
        c, chip, me, sib, (xn, yn, dg) = plan()
        for t in range(n):
            if split[t]:
                copy(t, 2, blk(t, dg, c, 0), me).wait_recv()
                copy(t, 3, blk(t, dg, c, 1), me).wait_recv()
            else:
                copy(t, 2, blk(t, dg, c), me).wait_recv()
            relays(t)[2][0].start()

    def take_sibling():
        _, chip, me, sib, (xn, yn, dg) = plan()
        theirs = 1 - lax.axis_index("c") if layer is None else layer
        for t in range(n):
            for k, peer in ((4, xn), (5, yn), (6, dg)):
                copy(t, k, blk(t, peer, theirs), me).wait_recv()

    def drain_sends():
        for t in range(n):
            after_x, after_y, after_d = relays(t)
            for cp in own_sends(t) + after_x + after_y + after_d:
                cp.wait_send()

    def finish():
        role(False, take_sibling)
        role(True, drain_sends)

    return ((lambda: role(True, send_own)), (lambda: role(True, relay_neighbours)),
            (lambda: role(True, relay_diagonal)), finish)


def _swap_copies(ins, outs, send_sems, recv_sems):
    x, y, c = _me()
    return [pltpu.make_async_remote_copy(src_ref=ins[t].at[1 - c], dst_ref=outs[t], send_sem=send_sems.at[t],
                                         recv_sem=recv_sems.at[t], device_id=(x, y, 1 - c), device_id_type=MESH)
            for t in range(len(ins))]


def _exchange_copies(peers, ins, outs, send_sems, recv_sems):
    x, y, c = _me()
    chips = _other_chips(x, y)
    n = len(peers)
    return [pltpu.make_async_remote_copy(src_ref=ins[t].at[2 * chips[j][0] + chips[j][1]], dst_ref=outs[t].at[i],
                                         send_sem=send_sems.at[n * t + i], recv_sem=recv_sems.at[n * t + i],
                                         device_id=(*chips[j], c), device_id_type=MESH)
            for i, j in enumerate(peers) for t in range(len(ins))]


class _Rider:
    def __init__(self, arrs, out_shape, nsem, copies):
        self.arrs, self.out_shape, self.nsem, self.copies = list(arrs), out_shape, nsem, copies
        self.n = len(self.arrs)

    def scratch(self):
        return [pltpu.SemaphoreType.DMA((self.nsem,)), pltpu.SemaphoreType.DMA((self.nsem,))]

    def start(self, ins, outs, sems):
        for cp in self.copies(ins, outs, *sems):
            cp.start()

    def wait(self, ins, outs, sems):
        for cp in self.copies(ins, outs, *sems):
            cp.wait()


def _swap_rider(gs):
    return _Rider(gs, [SDS(g.shape[1:], g.dtype) for g in gs], len(gs), _swap_copies)


def _exchange_rider(ts, peers=(0, 1, 2)):
    return _Rider(ts, [SDS((len(peers),) + t.shape[1:], t.dtype) for t in ts], len(peers) * len(ts),
                  functools.partial(_exchange_copies, peers))


HBM = BS(memory_space=pltpu.HBM)
SEM = BS(memory_space=pltpu.SEMAPHORE)
SPLIT_PARAMS = pltpu.CompilerParams(has_side_effects=pltpu.SideEffectType.DATAFLOW_SIDE_EFFECTING)


def _copies_start(rider, name):
    n = rider.n
    lands = [lax.empty(s.shape, s.dtype) for s in rider.out_shape]

    def body(*refs):
        rider.start(refs[:n], refs[n:2 * n], refs[2 * n:2 * n + 2])
        refs[-1][...] = jnp.zeros_like(refs[-1])

    held = [pltpu.with_memory_space_constraint(a, pltpu.HBM) for a in rider.arrs + lands]
    out = pl.pallas_call(
        body, name=name,
        out_shape=(pltpu.SemaphoreType.DMA((rider.nsem,)), pltpu.SemaphoreType.DMA((rider.nsem,)),
                   *[pltpu.HBM(a.shape, a.dtype) for a in held], SDS((8, 128), F32)),
        in_specs=[HBM] * (2 * n), out_specs=(SEM, SEM, *[HBM] * (2 * n), BS(memory_space=pltpu.VMEM)),
        input_output_aliases={i: 2 + i for i in range(2 * n)}, compiler_params=SPLIT_PARAMS)(*held)
    return rider, out[0], out[1], list(out[2:2 + n]), list(out[2 + n:2 + 2 * n]), out[-1]


def _copies_wait(rider, send_sems, recv_sems, srcs, lands, after, name):
    n = rider.n

    def body(*refs):
        for cp in rider.copies(refs[:n], refs[n:2 * n], refs[2 * n], refs[2 * n + 1]):
            cp.wait_send()
            cp.wait_recv()

    out = pl.pallas_call(
        body, name=name, out_shape=tuple(pltpu.HBM(a.shape, a.dtype) for a in srcs + lands),
        in_specs=[HBM] * (2 * n) + [SEM, SEM, ANY], out_specs=[HBM] * (2 * n),
        input_output_aliases={i: i for i in range(2 * n)}, compiler_params=SPLIT_PARAMS)(
        *srcs, *lands, send_sems, recv_sems, after)
    return list(out[n:2 * n])


def _spread_copies(ins, outs, send_sems, recv_sems):
    x, y, c = _me()
    me = 4 * x + 2 * y + c
    return [pltpu.make_async_remote_copy(src_ref=ins[0], dst_ref=outs[0].at[me], send_sem=send_sems.at[m - 1],
                                         recv_sem=recv_sems.at[m - 1],
                                         device_id=(1 - x if m & 4 else x, 1 - y if m & 2 else y, 1 - c if m & 1 else c),
                                         device_id_type=MESH)
            for m in range(1, 8)]


def _spread_rider(v):
    return _Rider([v], [SDS((8,) + v.shape, v.dtype)], 7, _spread_copies)


def _sum8(land, v, me, name):
    M = v.shape[0]

    def body(me_ref, land_ref, v_ref, o_ref):
        acc = jnp.where(me_ref[0] == 0, v_ref[...], land_ref[0])
        for d in range(1, 8):
            acc = acc + jnp.where(me_ref[0] == d, v_ref[...], land_ref[d])
        o_ref[...] = acc

    return _prefetch_call(body, name, (1,), [BS((8, M, LANES), lambda i, me: (0, 0, 0)), BS((M, LANES), lambda i, me: (0, 0))],
                          BS((M, LANES), lambda i, me: (0, 0)), SDS((M, LANES), F32))(me, land, v)


def _ride_alone(rider, name):
    n = rider.n

    def body(*refs):
        rider.start(refs[:n], refs[n:2 * n], refs[2 * n:])
        rider.wait(refs[:n], refs[n:2 * n], refs[2 * n:])

    return pl.pallas_call(body, name=name, out_shape=rider.out_shape, in_specs=[ANY] * n, out_specs=[ANY] * n,
                          scratch_shapes=rider.scratch(), compiler_params=COMM_PARAMS)(*rider.arrs)


def _join_layers(us, name):
    n = len(us)

    def body(*refs):
        outs, send_sems, recv_sems = refs[n:2 * n], refs[2 * n], refs[2 * n + 1]
        x, y, c = _me()
        cps = [pltpu.make_async_remote_copy(src_ref=outs[t].at[c], dst_ref=outs[t].at[c], send_sem=send_sems.at[t],
                                            recv_sem=recv_sems.at[t], device_id=(x, y, 1 - c), device_id_type=MESH)
               for t in range(n)]
        for cp in cps:
            cp.start()
        for cp in cps:
            cp.wait()

    return pl.pallas_call(body, name=name, out_shape=[SDS(u.shape, u.dtype) for u in us],
                          in_specs=[ANY] * n, out_specs=[ANY] * n, input_output_aliases={t: t for t in range(n)},
                          scratch_shapes=[pltpu.SemaphoreType.DMA((n,)), pltpu.SemaphoreType.DMA((n,))],
                          compiler_params=COMM_PARAMS)(*us)


def _allsum_small(v, name, after):
    M = v.shape[0]

    def body(x_ref, after_ref, o_ref, all_ref, send_sems, recv_sems, local_sem):
        x, y, c = _me()
        me, sib = (x, y, c), (x, y, 1 - c)
        chips = _other_chips(x, y)

        def rows(px, py, pc):
            return all_ref.at[pl.ds((4 * px + 2 * py + pc) * M, M), :]

        def copy(k, block, to, src=None):
            return pltpu.make_async_remote_copy(src_ref=rows(*block) if src is None else src, dst_ref=rows(*block),
                                                send_sem=send_sems.at[k], recv_sem=recv_sems.at[k],
                                                device_id=to, device_id_type=MESH)

        mine = pltpu.make_async_copy(x_ref, rows(*me), local_sem)
        mine.start()
        first = [copy(0, me, sib, src=x_ref)]
        first += [copy(1 + j, me, (*chip, c), src=x_ref) for j, chip in enumerate(chips)]
        for cp in first:
            cp.start()
        passed = [copy(4 + j, (*chip, c), sib) for j, chip in enumerate(chips)]
        for j, chip in enumerate(chips):
            copy(1 + j, (*chip, c), me).wait_recv()
            passed[j].start()
        copy(0, sib, me).wait_recv()
        for j, chip in enumerate(chips):
            copy(4 + j, (*chip, 1 - c), me).wait_recv()
        for cp in first + passed:
            cp.wait_send()
        mine.wait()
        acc = all_ref[0:M, :]
        for d in range(1, 8):
            acc = acc + all_ref[d * M:(d + 1) * M, :]
        o_ref[...] = acc

    vm = BS(memory_space=pltpu.VMEM)
    return pl.pallas_call(body, name=name, out_shape=SDS((M, LANES), F32), in_specs=[vm, ANY], out_specs=vm,
                          scratch_shapes=[pltpu.VMEM((8 * M, LANES), F32), pltpu.SemaphoreType.DMA((7,)),
                                          pltpu.SemaphoreType.DMA((7,)), pltpu.SemaphoreType.DMA],
                          compiler_params=pltpu.CompilerParams(has_side_effects=True, vmem_limit_bytes=VMEM_LIMIT))(
        v, after)


FFN = ("w_gate_up", "w_down")
REST = ("w_in", "w_o", "w_uq", "w_ukv")
BIG = FFN + REST
TINY = ("conv_w",)
REPL = ("attn_norm", "mla_q_norm", "mla_kv_norm", "pool_w", "pool_scale", "swa_sinks", "mix_norm", "ffn_norm",
        "final_norm")
ORDER = ("attn_norm", "w_in", "mla_q_norm", "w_uq", "mla_kv_norm", "w_ukv", "conv_w", "pool_w", "pool_scale",
         "swa_sinks", "mix_norm", "w_o", "ffn_norm", "w_gate_up", "w_down", "final_norm")


def _rows8(shape):
    return -(-int(np.prod(shape)) // (8 * LANES)) * 8


def _pack(arrs):
    parts = []
    for a in arrs:
        r = _rows8(a.shape)
        parts.append(jnp.pad(a.reshape(-1), (0, r * LANES - a.size)).reshape(r, LANES))
    return jnp.concatenate(parts, axis=0)


def _unpack(buf, shapes):
    out, r0 = [], 0
    for s in shapes:
        n, r = int(np.prod(s)), _rows8(s)
        rows = buf[r0:r0 + r]
        out.append(rows.reshape(s) if n == r * LANES else rows.reshape(-1)[:n].reshape(s))
        r0 += r
    return out


def _cols_joined(g):
    return jnp.transpose(g, (0, 2, 1, 3)).reshape(g.shape[0], g.shape[2], 4 * g.shape[3])


def _cols_split(w):
    n, a, b4 = w.shape
    return jnp.transpose(w.reshape(n, a, 4, b4 // 4), (0, 2, 1, 3))


def _rope_tables(S):
    inv = 1.0 / (10000.0 ** (jnp.arange(0, 32, 2, dtype=F32) / 32))
    ang = jnp.arange(S, dtype=F32)[:, None] * inv[None, :]
    cos, sin = jnp.cos(ang), jnp.sin(ang)
    z = lambda w: jnp.zeros((S, w), F32)
    tc = jnp.concatenate([jnp.ones((S, 64), F32), cos, cos, jnp.ones((S, 32), F32)], axis=1)
    ts1 = jnp.concatenate([z(64), -sin, z(48)], axis=1)
    ts2 = jnp.concatenate([z(80), sin, z(32)], axis=1)
    return tc, ts1, ts2


def _pad_w_in(wt):
    z = lambda n: jnp.zeros((wt.shape[0], n, wt.shape[2]), wt.dtype)
    return jnp.concatenate([wt[:, 0:384], z(64), wt[:, 384:416], z(32), wt[:, 416:1952]], axis=1)


def _unpad_w_in(dt):
    return jnp.concatenate([dt[:, 0:384], dt[:, 448:480], dt[:, 512:2048]], axis=1)


def _pad_heads(w, src, offs):
    cols = []
    for h in range(HEADS):
        src0, n = src[h]
        z = lambda k: jnp.zeros(w.shape[:-1] + (k,), w.dtype)
        cols += [z(offs[h]), w[..., src0:src0 + n], z(128 - offs[h] - n)]
    return jnp.concatenate(cols, axis=-1)


UQ_SRC = [(h * 96, 96) for h in range(HEADS)]
KN_SRC = [(h * 128, 64) for h in range(HEADS)]
V_SRC = [(h * 128 + 64, 64) for h in range(HEADS)]
ZERO_OFF = [0] * HEADS
V_OFF = [(h % 2) * 64 for h in range(HEADS)]


def _unpad_heads(d, src, offs):
    return [d[..., h * 128 + offs[h]: h * 128 + offs[h] + src[h][1]] for h in range(HEADS)]


def kernel(x, attn_norm, w_in, mla_q_norm, w_uq, mla_kv_norm, w_ukv, conv_w, pool_w, pool_scale, swa_sinks, mix_norm, w_o, ffn_norm, w_gate_up, w_down, final_norm, loss_target, m_attn_norm, m_w_in, m_mla_q_norm, m_w_uq, m_mla_kv_norm, m_w_ukv, m_conv_w, m_pool_w, m_pool_scale, m_swa_sinks, m_mix_norm, m_w_o, m_ffn_norm, m_w_gate_up, m_w_down, m_final_norm, v_attn_norm, v_w_in, v_mla_q_norm, v_w_uq, v_mla_kv_norm, v_w_ukv, v_conv_w, v_pool_w, v_pool_scale, v_swa_sinks, v_mix_norm, v_w_o, v_ffn_norm, v_w_gate_up, v_w_down, v_final_norm):
    W = dict(attn_norm=attn_norm, w_in=w_in, mla_q_norm=mla_q_norm, w_uq=w_uq, mla_kv_norm=mla_kv_norm, w_ukv=w_ukv,
             conv_w=conv_w, pool_w=pool_w, pool_scale=pool_scale, swa_sinks=swa_sinks, mix_norm=mix_norm, w_o=w_o,
             ffn_norm=ffn_norm, w_gate_up=w_gate_up, w_down=w_down, final_norm=final_norm)
    M1 = dict(attn_norm=m_attn_norm, w_in=m_w_in, mla_q_norm=m_mla_q_norm, w_uq=m_w_uq, mla_kv_norm=m_mla_kv_norm,
              w_ukv=m_w_ukv, conv_w=m_conv_w, pool_w=m_pool_w, pool_scale=m_pool_scale, swa_sinks=m_swa_sinks,
              mix_norm=m_mix_norm, w_o=m_w_o, ffn_norm=m_ffn_norm, w_gate_up=m_w_gate_up, w_down=m_w_down,
              final_norm=m_final_norm)
    V2 = dict(attn_norm=v_attn_norm, w_in=v_w_in, mla_q_norm=v_mla_q_norm, w_uq=v_w_uq, mla_kv_norm=v_mla_kv_norm,
              w_ukv=v_w_ukv, conv_w=v_conv_w, pool_w=v_pool_w, pool_scale=v_pool_scale, swa_sinks=v_swa_sinks,
              mix_norm=v_mix_norm, w_o=v_w_o, ffn_norm=v_ffn_norm, w_gate_up=v_w_gate_up, w_down=v_w_down,
              final_norm=v_final_norm)
    S = x.shape[1]
    xc, yc, cc = _me()
    chip = 2 * xc + yc
    kc = jnp.stack([chip, cc]).astype(jnp.int32)

    first, later = ("w_in", "w_uq", "w_ukv", "conv_w"), ("w_o", "w_gate_up", "w_down")
    T = lambda a: jnp.swapaxes(a, 1, 2)
    W["w_in"], M1["w_in"], V2["w_in"] = T(w_in), T(m_w_in), T(v_w_in)
    placed = {n: _place(W[n], kc, F32 if n == "conv_w" else BF16, f"place_{n}") for n in first + later}
    gi, gq, gkv, gcv = _gather([placed[n] for n in first], "gather_weights")
    later_w = [placed[n] for n in later]
    win_p = _pad_w_in(gi.reshape(2, 4 * gi.shape[2], D))
    wuq_p = _pad_heads(_cols_joined(gq), UQ_SRC, ZERO_OFF)
    wukv = _cols_joined(gkv)
    wk_p = _pad_heads(wukv, KN_SRC, ZERO_OFF)
    wv_p = _pad_heads(wukv, V_SRC, V_OFF)
    conv8 = jnp.pad(_cols_joined(gcv), ((0, 0), (0, 5), (0, 0)))
    pwd = jnp.concatenate([jnp.concatenate(
        [jnp.pad(pool_w[:, 2 * b], ((0, 0), (0, 0), (0, 64))), jnp.pad(pool_w[:, 2 * b + 1], ((0, 0), (0, 0), (64, 0)))],
        axis=1) for b in range(2)], axis=1).astype(BF16)
    tabs = _rope_tables(S)
    g_attn, g_q, g_kv, g_mix, g_ffn, g_ps = (_g3(W[n]) for n in ("attn_norm", "mla_q_norm", "mla_kv_norm", "mix_norm",
                                                                  "ffn_norm", "pool_scale"))

    xs = [x[0]]
    saved = []
    for l in range(DEPTH):
        x0 = xs[-1]
        proj, h = _norm_mm(x0, g_attn, l, win_p, _wspec_in(l), D_INP, D_INP, F32, f"in_proj{l}", w_t=True)
        q, k, v, kt, vt = _mla_prep(proj, g_q, g_kv, wuq_p, wk_p, wv_p, tabs, l, f"mla_prep{l}")
        ya, lse, later_w = _mla_attn(q, k, vt, later_w, l, f"mla_attn{l}")
        go, gu4, gd = later_w
        wo, wdown = go.reshape(2, D, D), gd.reshape(2, D_FF, D)
        yb = _conv(proj, conv8, l, f"conv{l}")
        ycp = _pool(proj, pwd, g_ps, l, f"pool{l}")
        yd = _swa(proj, swa_sinks, l, f"swa{l}")
        x1, ycat, mixed = _mix_out(x0, ya, yb, ycp, yd, g_mix, wo, l, f"mix_out{l}")
        gu, h2 = _norm_mm(x1, g_ffn, l, gu4, _wspec_gu(l), 2 * D_FF, 2 * D_FF // 4, BF16, f"gate_up{l}")
        x2, act = _swiglu_mm_res(x1, gu, wdown, l, f"down{l}")
        saved.append(dict(x0=x0, proj=proj, h=h, q=q, k=k, kt=kt, v=v, lse=lse, x1=x1, ycat=ycat, mixed=mixed,
                          gu=gu, h2=h2, act=act))
        xs.append(x2)

    dx, dx16, dg_final, loss_tile = _loss_head(xs[-1], final_norm.reshape(1, D), loss_target[0], "loss_head")
    loss_here = (loss_tile[0, 0] * (0.5 / D)).reshape(1)

    G = {n: [None] * DEPTH for n in ("w_uq", "w_ukv") + TINY + REPL if n != "final_norm"}
    gw_in = gw_o = gw_gu = gw_down = None
    for l in reversed(range(DEPTH)):
        sv = saved[l]
        dgu = _bwd_down(dx16, wdown, sv["gu"], l, f"down_bwd{l}")
        gw_down = _mm_tn(sv["act"], dx16, l, gw_down, f"dw_down{l}")
        gw_gu = _mm_tn(sv["h2"], dgu, l, gw_gu, f"dw_gate_up{l}", split4=True)
        ffn_token = None
        if l == 0:
            g_f = [gw_gu, gw_down.reshape(2, 4, D_FF // 4, D)]
            dx1, dx1_16, dg, got_f = _mm_nt_normbwd(dgu, gu4, l, sv["x1"], g_ffn, dx, 1, f"gate_up_bwd{l}",
                                                    rider=_swap_rider(g_f))
            pairs_f = [_pair_sum(g, o, kc, f"rs_pair_sum_{n}") for g, o, n in zip(g_f, got_f, FFN)]
            ffn_flight = _copies_start(_exchange_rider([p[1] for p in pairs_f]), "rs_exchange_start_ffn")
            ffn_token = ffn_flight[-1]
        else:
            dx1, dx1_16, dg = _mm_nt_normbwd(dgu, gu4, l, sv["x1"], g_ffn, dx, 1, f"gate_up_bwd{l}")
        G["ffn_norm"][l] = dg[0]
        gw_o = _mm_tn(sv["mixed"], dx1_16, l, gw_o, f"dw_o{l}")
        dycat, dg = _mm_nt_normbwd(dx1_16, wo.reshape(2, 1, D, D), l, sv["ycat"], g_mix, None, 4, f"mix_bwd{l}")
        G["mix_norm"][l] = dg[0]

        proj = sv["proj"]
        delta = _mla_delta(dycat, sv["ycat"], f"mla_delta{l}", after=ffn_token)
        dq, dk, dv, _ = _mla_attn_bwd(sv["q"], sv["k"], sv["kt"], sv["v"], dycat, sv["lse"], delta, None,
                                      f"mla_attn_bwd{l}")
        dcq, dckv, dkr, dwuq, dwk, dwv, dgq, dgkv = _mla_prep_bwd(
            dq, dk, dv, proj, g_q, g_kv, wuq_p, wk_p, wv_p, tabs, l, f"mla_prep_bwd{l}")
        dgb, dgc, duc, dcw = _conv_bwd(proj, conv8, dycat, l, f"conv_bwd{l}")
        dup, dpw, dps = _pool_bwd(proj, pwd, g_ps, dycat, l, f"pool_bwd{l}")
        dqs, dks, dvs, dsink = _swa_bwd(proj, swa_sinks, dycat, l, f"swa_bwd{l}")
        dproj = jnp.concatenate([dcq, dckv, dkr, dgb, dgc, duc, dup, dqs, dks, dvs], axis=1)
        gw_in = _mm_tn(dproj, sv["h"], l, gw_in, f"dw_in{l}")
        G["w_uq"][l] = jnp.concatenate(_unpad_heads(dwuq, UQ_SRC, ZERO_OFF), axis=1)
        kn, vv = _unpad_heads(dwk, KN_SRC, ZERO_OFF), _unpad_heads(dwv, V_SRC, V_OFF)
        G["w_ukv"][l] = jnp.concatenate([t for h in range(HEADS) for t in (kn[h], vv[h])], axis=1)
        swap_token = None
        if l == 0:
            g_r = [_unpad_w_in(gw_in).reshape(2, 4, -1, D), gw_o.reshape(2, 4, D // 4, D),
                   _cols_split(jnp.stack(G["w_uq"])), _cols_split(jnp.stack(G["w_ukv"]))]
            swap_flight = _copies_start(_swap_rider(g_r), "rs_swap_start")
            swap_token = swap_flight[-1]
        dx, dx16, dg = _mm_nt_normbwd(dproj, win_p.reshape(2, 1, D_INP, D), l, sv["x0"], g_attn, dx1, 1, f"in_proj_bwd{l}",
                                      w_t=True, after=swap_token)
        G["attn_norm"][l] = dg[0]
        G["mla_q_norm"][l] = dgq[0]
        G["mla_kv_norm"][l] = dgkv[0]
        G["conv_w"][l] = dcw[0:3]
        G["pool_w"][l] = jnp.stack([dpw[0:64, 0:64], dpw[64:128, 64:128], dpw[128:192, 0:64], dpw[192:256, 64:128]])
        G["pool_scale"][l] = dps[0]
        G["swa_sinks"][l] = dsink[0, 0:4]
    grad_x = dx[None]
    Gl = {n: jnp.stack(G[n]) for n in TINY + REPL if n != "final_norm"}
    Gl["final_norm"] = dg_final[0]

    got_r = _copies_wait(*swap_flight[:-1], dx16, "rs_swap_wait")
    pairs_r = [_pair_sum(g, o, kc, f"rs_pair_sum_{n}") for g, o, n in zip(g_r, got_r, REST)]
    got3_f = _copies_wait(*ffn_flight[:-1], dx16, "rs_exchange_wait_ffn")
    small = TINY + REPL
    small_v = _pack([Gl[n] for n in small] + [loss_here])
    in_flight = _copies_start(_exchange_rider([p[1] for p in pairs_r]), "rs_exchange_start")
    spread = _copies_start(_spread_rider(small_v), "allsum_start")
    us_f = [_chip_sum(p[0], [o3], kc, f"rs_chip_sum_{n}", after=tok)
            for p, o3, n, tok in zip(pairs_f, got3_f, FFN, (in_flight[-1], spread[-1]))]
    gsum_f = _join_layers(us_f, "rs_join_cores_ffn")
    res = {}

    def update(names, grads):
        for n, g in zip(names, grads):
            d_, m_, v_, g_ = _adamw(W[n], g, M1[n], V2[n], f"adamw_{n}", echo=True)
            back = T if n == "w_in" else (lambda a: a)
            res["g", n], res["d", n], res["m", n], res["v", n] = back(g_), back(d_), back(m_), back(v_)

    update(FFN, gsum_f)
    full_shapes = [Gl[n].shape for n in small] + [(1,)]
    others = _copies_wait(*spread[:-1], res["d", FFN[-1]], "allsum_wait")[0]
    summed = _sum8(others, small_v, jnp.reshape(2 * chip + cc, (1,)).astype(jnp.int32), "allsum_small")
    got3_r = _copies_wait(*in_flight[:-1], summed, "rs_exchange_wait")
    us_r = [_chip_sum(p[0], [o3], kc, f"rs_chip_sum_{n}") for p, o3, n in zip(pairs_r, got3_r, REST)]
    update(REST, _join_layers(us_r, "rs_join_cores"))
    summed = _unpack(summed, full_shapes)
    loss = summed.pop().reshape(())

    def as3(a):
        if a.ndim <= 2:
            return a.reshape((1,) * (3 - a.ndim) + a.shape)
        return a.reshape(a.shape[0], -1, a.shape[-1])

    for n, g in zip(small, summed):
        if n in TINY:
            wdt = W[n].shape[2]
            g = lax.dynamic_slice_in_dim(g, chip * wdt, wdt, axis=2)
        out = _adamw(as3(W[n]), as3(g), as3(M1[n]), as3(V2[n]), f"adamw_{n}")
        res["g", n] = g
        res["d", n], res["m", n], res["v", n] = (o.reshape(W[n].shape) for o in out)

    return (loss, grad_x, *[res["g", n] for n in ORDER], *[res["d", n] for n in ORDER],
            *[res["m", n] for n in ORDER], *[res["v", n] for n in ORDER])
```

```python
import functools
import math

import numpy as np
import jax
import jax.numpy as jnp
from jax import lax
from jax.experimental import pallas as pl
from jax.experimental.pallas import tpu as pltpu

F32, BF16 = jnp.float32, jnp.bfloat16
SDS = jax.ShapeDtypeStruct
BS = pl.BlockSpec
MESH = pl.DeviceIdType.MESH

D = 1024
DEPTH = 2
HEADS = 4
D_FF = 2816
D_INP = 2048
EPS = 1e-6
SWA_WINDOW = 128
BLK = 128
SLOPES = tuple(2.0 ** (-8.0 * (i + 1) / 4) for i in range(4))
QK_SCALE = 1.0 / math.sqrt(96)
SWA_SCALE = 1.0 / math.sqrt(64)
LR, B1, B2, ADAM_EPS, WD, STEP = 0.001, 0.9, 0.999, 1e-08, 0.01, 10

LANES = 1024
VMEM_LIMIT = 56 * 1024 * 1024
NEG_INF = float("-inf")

C_CQ, C_CKV, C_KR, C_GB, C_GC, C_UC, C_UP, C_QS, C_KS, C_VS = 0, 256, 384, 512, 768, 1024, 1280, 1536, 1792, 1920


def _params(ngrid):
    return pltpu.CompilerParams(dimension_semantics=("arbitrary",) * ngrid, vmem_limit_bytes=VMEM_LIMIT)


def _pc(body, *, name, grid, in_specs, out_specs, out_shape, scratch=(), aliases=None):
    return pl.pallas_call(
        body, name=name, grid=grid, in_specs=in_specs, out_specs=out_specs, out_shape=out_shape,
        scratch_shapes=scratch, input_output_aliases=aliases or {}, compiler_params=_params(len(grid)))


def _dot(a, b):
    return jnp.dot(a, b, preferred_element_type=F32)


def _dot_nt(a, b):
    return lax.dot_general(a, b, (((1,), (1,)), ((), ())), preferred_element_type=F32)


def _dot_tn(a, b):
    return lax.dot_general(a, b, (((0,), (0,)), ((), ())), preferred_element_type=F32)


def _tile(n, cap):
    if n <= cap:
        return n
    t = cap - cap % 128
    while n % t:
        t -= 128
    return t


def _row_tile(a, b, cap=262144):
    bp = -(-b // 128) * 128
    best = None
    for t in range(8, a + 1, 8):
        if a % t == 0 and t * bp <= cap:
            best = t
    if best is None or (best < 64 and a * bp <= 2 * cap):
        return a
    return best


def _g3(a):
    return a.reshape(a.shape[0], 1, a.shape[1])


def _norm_mm(x, g3, l, w, wspec, N, tn, out_dtype, name, w_t=False):
    S, K = x.shape
    tm = min(1024 if out_dtype == BF16 else 512, S)

    def body(x_ref, g_ref, w_ref, y_ref, h_ref):
        @pl.when(pl.program_id(1) == 0)
        def _():
            xv = x_ref[...]
            r = lax.rsqrt(jnp.mean(xv * xv, axis=-1, keepdims=True) + EPS)
            h_ref[...] = (xv * r * g_ref[...]).astype(BF16)

        y_ref[...] = (_dot_nt if w_t else _dot)(h_ref[...], w_ref[...]).astype(out_dtype)

    return _pc(body, name=name, grid=(S // tm, N // tn),
               in_specs=[BS((tm, K), lambda i, j: (i, 0)), BS((None, 1, K), lambda i, j: (l, 0, 0)), wspec],
               out_specs=[BS((tm, tn), lambda i, j: (i, j)), BS((tm, K), lambda i, j: (i, 0))],
               out_shape=[SDS((S, N), out_dtype), SDS((S, K), BF16)])(x, g3, w)


def _wspec_in(l):
    return BS((None, D_INP, D), lambda i, j: (l, j, 0))


def _wspec_gu(l):
    return BS((None, None, D, 2 * D_FF // 4), lambda i, j: (l, j, 0, 0))


def _mix_out(x0, ya, yb, yc, yd, gmix3, wo, l, name):
    S = x0.shape[0]
    tm = min(512, S)

    def body(x_ref, ya_ref, yb_ref, yc_ref, yd_ref, g_ref, w_ref, x1_ref, ycat_ref, mixed_ref):
        groups = [ya_ref[...], yb_ref[...], yc_ref[...], yd_ref[...]]
        for gi, yg in enumerate(groups):
            sl = slice(gi * 256, (gi + 1) * 256)
            r = lax.rsqrt(jnp.mean(yg * yg, axis=-1, keepdims=True) + EPS)
            ycat_ref[:, sl] = yg
            mixed_ref[:, sl] = (yg * r * g_ref[:, sl]).astype(BF16)
        x1_ref[...] = x_ref[...] + _dot(mixed_ref[...], w_ref[...])

    row = lambda w: BS((tm, w), lambda i: (i, 0))
    return _pc(body, name=name, grid=(S // tm,),
               in_specs=[row(D), row(256), row(256), row(256), row(256), BS((None, 1, D), lambda i: (l, 0, 0)),
                         BS((None, D, D), lambda i: (l, 0, 0))],
               out_specs=[row(D), row(D), row(D)],
               out_shape=[SDS((S, D), F32), SDS((S, D), F32), SDS((S, D), BF16)])(x0, ya, yb, yc, yd, gmix3, wo)


def _swiglu_mm_res(x1, gu, wdown, l, name):
    S = x1.shape[0]
    tm = min(256, S)

    def body(x_ref, gate_ref, up_ref, w_ref, x2_ref, act_ref):
        acc = x_ref[...]
        for c0 in range(0, D_FF, D_FF // 2):
            cs = slice(c0, c0 + D_FF // 2)
            gt = gate_ref[:, cs].astype(F32)
            act = (gt * pl.reciprocal(1.0 + jnp.exp(-gt), approx=True) * up_ref[:, cs].astype(F32)).astype(BF16)
            act_ref[:, cs] = act
            acc = acc + _dot(act, w_ref[cs, :])
        x2_ref[...] = acc

    return _pc(body, name=name, grid=(S // tm,),
               in_specs=[BS((tm, D), lambda i: (i, 0)), BS((tm, D_FF), lambda i: (i, 0)),
                         BS((tm, D_FF), lambda i: (i, 1)), BS((None, D_FF, D), lambda i: (l, 0, 0))],
               out_specs=[BS((tm, D), lambda i: (i, 0)), BS((tm, D_FF), lambda i: (i, 0))],
               out_shape=[SDS((S, D), F32), SDS((S, D_FF), BF16)])(x1, gu, gu, wdown)


def _loss_head(x, g, tgt, name):
    S = x.shape[0]
    tm = min(512, S)

    def body(x_ref, g_ref, t_ref, dx_ref, dx16_ref, dg_ref, loss_ref):
        @pl.when(pl.program_id(0) == 0)
        def _():
            dg_ref[...] = jnp.zeros_like(dg_ref)
            loss_ref[...] = jnp.zeros_like(loss_ref)

        xv = x_ref[...]
        r = lax.rsqrt(jnp.mean(xv * xv, axis=-1, keepdims=True) + EPS)
        xh = xv * r
        gv = g_ref[...]
        diff = xh * gv - t_ref[...]
        loss_ref[...] += jnp.sum(diff * diff)
        dy = diff * (1.0 / D)
        dg_ref[...] += jnp.sum(dy * xh, axis=0, keepdims=True)
        dxh = dy * gv
        dx = r * (dxh - xh * jnp.mean(dxh * xh, axis=-1, keepdims=True))
        dx_ref[...] = dx
        dx16_ref[...] = dx.astype(BF16)

    row = BS((tm, D), lambda i: (i, 0))
    return _pc(body, name=name, grid=(S // tm,),
               in_specs=[row, BS((1, D), lambda i: (0, 0)), row],
               out_specs=[row, row, BS((8, D), lambda i: (0, 0)), BS((8, 128), lambda i: (0, 0))],
               out_shape=[SDS((S, D), F32), SDS((S, D), BF16), SDS((8, D), F32), SDS((8, 128), F32)])(x, g, tgt)


def _mm_tn(a, b, l, prev, name, split4=False):
    S, Ka = a.shape
    N = b.shape[1]
    if split4:
        ta, tn = _tile(Ka, 256), N // 4
        out_shape = SDS((2, 4, Ka, tn), F32)
        out_spec = BS((None, None, ta, tn), lambda j, i: (l, j, i, 0))
    else:
        ta, tn = _tile(Ka, 512), _tile(N, 1024)
        out_shape = SDS((2, Ka, N), F32)
        out_spec = BS((None, ta, tn), lambda j, i: (l, i, j))

    def body(a_ref, b_ref, *rest):
        rest[-1][...] = _dot_tn(a_ref[...], b_ref[...])

    in_specs = [BS((S, ta), lambda j, i: (0, i)), BS((S, tn), lambda j, i: (0, j))]
    args = [a, b]
    if prev is not None:
        in_specs.append(BS(memory_space=pl.ANY))
        args.append(prev)
    return _pc(body, name=name, grid=(N // tn, Ka // ta), in_specs=in_specs, out_specs=out_spec, out_shape=out_shape,
               aliases={2: 0} if prev is not None else None)(*args)


def _bwd_down(dx16, wdown, gu, l, name):
    S = dx16.shape[0]
    tm = min(256, S)

    def body(dx_ref, w_ref, gate_ref, up_ref, dgu_ref):
        dxv = dx_ref[...]
        for c0 in range(0, D_FF, 256):
            cs = slice(c0, c0 + 256)
            dact = _dot_nt(dxv, w_ref[cs, :])
            gt = gate_ref[:, cs].astype(F32)
            sg = pl.reciprocal(1.0 + jnp.exp(-gt), approx=True)
            dgu_ref[:, cs] = (dact * up_ref[:, cs].astype(F32) * (sg * (1.0 + gt * (1.0 - sg)))).astype(BF16)
            dgu_ref[:, D_FF + c0:D_FF + c0 + 256] = (dact * (gt * sg)).astype(BF16)

    return _pc(body, name=name, grid=(S // tm,),
               in_specs=[BS((tm, D), lambda i: (i, 0)), BS((None, D_FF, D), lambda i: (l, 0, 0)),
                         BS((tm, D_FF), lambda i: (i, 0)), BS((tm, D_FF), lambda i: (i, 1))],
               out_specs=BS((tm, 2 * D_FF), lambda i: (i, 0)),
               out_shape=SDS((S, 2 * D_FF), BF16))(dx16, wdown, gu, gu)


def _mm_nt_normbwd(dy, w4, l, x, g3, dres, ngroups, name, rider=None, w_t=False, after=None):
    S, K = dy.shape
    nk, kc = w4.shape[1], w4.shape[2 if w_t else 3]
    mm = _dot if w_t else _dot_nt
    tm = min(512, S)
    gw = D // ngroups
    has_res = dres is not None
    nr = rider.n if rider else 0
    n_in, n_out = 4 + has_res + (after is not None), 2 + has_res

    def body(*refs):
        dy_ref, w_ref, x_ref, g_ref = refs[:4]
        res_ref = refs[4] if has_res else None
        outs = refs[n_in + nr:n_in + nr + n_out]
        dx_ref, dg_ref = outs[0], outs[-1]
        dx16_ref = outs[1] if has_res else None
        r_io = (refs[n_in:n_in + nr], refs[n_in + nr + n_out:n_in + 2 * nr + n_out], refs[n_in + 2 * nr + n_out:])
        if rider:
            pl.when(pl.program_id(0) == 0)(lambda: rider.start(*r_io))

        @pl.when(pl.program_id(0) == 0)
        def _():
            dg_ref[...] = jnp.zeros_like(dg_ref)

        dh = mm(dy_ref[:, 0:kc], w_ref[0])
        for k in range(1, nk):
            dh = dh + mm(dy_ref[:, k * kc:(k + 1) * kc], w_ref[k])
        for gi in range(ngroups):
            sl = slice(gi * gw, (gi + 1) * gw)
            xg = x_ref[:, sl]
            r = lax.rsqrt(jnp.mean(xg * xg, axis=-1, keepdims=True) + EPS)
            xh = xg * r
            dhg = dh[:, sl]
            dg_ref[:, sl] += jnp.sum(dhg * xh, axis=0, keepdims=True)
            dxh = dhg * g_ref[:, sl]
            dxg = r * (dxh - xh * jnp.mean(dxh * xh, axis=-1, keepdims=True))
            if has_res:
                dxg = dxg + res_ref[:, sl]
                dx16_ref[:, sl] = dxg.astype(BF16)
            dx_ref[:, sl] = dxg
        if rider:
            pl.when(pl.program_id(0) == S // tm - 1)(lambda: rider.wait(*r_io))

    row = BS((tm, D), lambda i: (i, 0))
    in_specs = [BS((tm, K), lambda i: (i, 0)),
                BS((None,) + tuple(w4.shape[1:]), lambda i: (l, 0, 0, 0), pipeline_mode=pl.Buffered(1)), row,
                BS((None, 1, D), lambda i: (l, 0, 0))]
    args = [dy, w4, x, g3]
    out_specs, out_shape = [row], [SDS((S, D), F32)]
    if has_res:
        in_specs.append(row)
        args.append(dres)
        out_specs.append(row)
        out_shape.append(SDS((S, D), BF16))
    if after is not None:
        in_specs.append(BS((8, 128), lambda i: (0, 0)))
        args.append(after)
    out_specs.append(BS((8, D), lambda i: (0, 0)))
    out_shape.append(SDS((8, D), F32))
    if not rider:
        return _pc(body, name=name, grid=(S // tm,), in_specs=in_specs, out_specs=out_specs, out_shape=out_shape)(*args)
    out = pl.pallas_call(body, name=name, grid=(S // tm,), in_specs=in_specs + [ANY] * nr,
                         out_specs=out_specs + [ANY] * nr, out_shape=out_shape + rider.out_shape,
                         scratch_shapes=rider.scratch(),
                         compiler_params=pltpu.CompilerParams(dimension_semantics=("arbitrary",),
                                                              vmem_limit_bytes=VMEM_LIMIT, has_side_effects=True))(
        *args, *rider.arrs)
    return (*out[:n_out], list(out[n_out:]))


def _rope(x, c, s1, s2):
    return x * c + pltpu.roll(x, 112, axis=1) * s1 + pltpu.roll(x, 16, axis=1) * s2


def _rope_t(dy, c, s1, s2):
    return dy * c + pltpu.roll(dy * s1, 16, axis=1) + pltpu.roll(dy * s2, 112, axis=1)


def _mla_prep(proj, gq3, gkv3, wuq, wk, wv, tabs, l, name):
    S = proj.shape[0]
    tm = min(512, S)
    tc, ts1, ts2 = tabs

    def body(cq_ref, ckv_ref, kr_ref, gq_ref, gkv_ref, wuq_ref, wk_ref, wv_ref, c_ref, s1_ref, s2_ref,
             q_ref, k_ref, v_ref, kt_ref, vt_ref):
        c, s1, s2 = c_ref[...], s1_ref[...], s2_ref[...]
        cq = cq_ref[...]
        rq = lax.rsqrt(jnp.mean(cq * cq, axis=-1, keepdims=True) + EPS)
        qa = _dot((cq * rq * gq_ref[...]).astype(BF16), wuq_ref[...])
        ckv = ckv_ref[...]
        rkv = lax.rsqrt(jnp.mean(ckv * ckv, axis=-1, keepdims=True) + EPS)
        ckvn = (ckv * rkv * gkv_ref[...]).astype(BF16)
        ka = _dot(ckvn, wk_ref[...])
        va = _dot(ckvn, wv_ref[...])
        v_ref[...] = va.astype(BF16)
        vt_ref[...] = va.T.astype(BF16)
        krr = _rope(kr_ref[...], c, s1, s2)
        for h in range(HEADS):
            sl = slice(h * 128, (h + 1) * 128)
            q_ref[:, sl] = (_rope(qa[:, sl], c, s1, s2) * QK_SCALE).astype(BF16)
            kh = ka[:, sl] + krr
            k_ref[:, sl] = kh.astype(BF16)
            kt_ref[sl, :] = kh.T.astype(BF16)

    lay = lambda a, b: BS((None, a, b), lambda i: (l, 0, 0))
    tab = BS((tm, 128), lambda i: (i, 0))
    return _pc(body, name=name, grid=(S // tm,),
               in_specs=[BS((tm, 256), lambda i: (i, 0)), BS((tm, 128), lambda i: (i, 2)), BS((tm, 128), lambda i: (i, 3)),
                         lay(1, 256), lay(1, 128), lay(256, 512), lay(128, 512), lay(128, 512), tab, tab, tab],
               out_specs=[BS((tm, 512), lambda i: (i, 0))] * 3 + [BS((512, tm), lambda i: (0, i))] * 2,
               out_shape=[SDS((S, 512), BF16)] * 3 + [SDS((512, S), BF16)] * 2)(
        proj, proj, proj, gq3, gkv3, wuq, wk, wv, tc, ts1, ts2)


def _mla_prep_bwd(dq, dk, dv, proj, gq3, gkv3, wuq, wk, wv, tabs, l, name):
    S = proj.shape[0]
    tm = min(512, S)
    tc, ts1, ts2 = tabs

    def body(dq_ref, dk_ref, dv_ref, cq_ref, ckv_ref, gq_ref, gkv_ref, wuq_ref, wk_ref, wv_ref, c_ref, s1_ref, s2_ref,
             dcq_ref, dckv_ref, dkr_ref, dwuq_ref, dwk_ref, dwv_ref, dgq_ref, dgkv_ref):
        @pl.when(pl.program_id(0) == 0)
        def _():
            for r in (dwuq_ref, dwk_ref, dwv_ref, dgq_ref, dgkv_ref):
                r[...] = jnp.zeros_like(r)

        c, s1, s2 = c_ref[...], s1_ref[...], s2_ref[...]
        dqp = jnp.concatenate(
            [_rope_t(dq_ref[h * 128:(h + 1) * 128, :].T * QK_SCALE, c, s1, s2) for h in range(HEADS)], axis=1).astype(BF16)
        cq = cq_ref[...]
        rq = lax.rsqrt(jnp.mean(cq * cq, axis=-1, keepdims=True) + EPS)
        cqh = cq * rq
        gq_v = gq_ref[...]
        dwuq_ref[...] += _dot_tn((cqh * gq_v).astype(BF16), dqp)
        dcqn = _dot_nt(dqp, wuq_ref[...])
        dgq_ref[...] += jnp.sum(dcqn * cqh, axis=0, keepdims=True)
        dxh = dcqn * gq_v
        dcq_ref[...] = (rq * (dxh - cqh * jnp.mean(dxh * cqh, axis=-1, keepdims=True))).astype(BF16)

        dkb = dk_ref[...].astype(BF16)
        dvb = dv_ref[...].astype(BF16)
        ckv = ckv_ref[...]
        rkv = lax.rsqrt(jnp.mean(ckv * ckv, axis=-1, keepdims=True) + EPS)
        ckh = ckv * rkv
        gkv_v = gkv_ref[...]
        ckvn = (ckh * gkv_v).astype(BF16)
        dwk_ref[...] += _dot_tn(ckvn, dkb)
        dwv_ref[...] += _dot_tn(ckvn, dvb)
        dckvn = _dot_nt(dkb, wk_ref[...]) + _dot_nt(dvb, wv_ref[...])
        dgkv_ref[...] += jnp.sum(dckvn * ckh, axis=0, keepdims=True)
        dyh = dckvn * gkv_v
        dckv_ref[...] = (rkv * (dyh - ckh * jnp.mean(dyh * ckh, axis=-1, keepdims=True))).astype(BF16)
        dks = dk_ref[:, 0:128] + dk_ref[:, 128:256] + dk_ref[:, 256:384] + dk_ref[:, 384:512]
        dkr_ref[...] = _rope_t(dks, c, s1, s2).astype(BF16)

    full = lambda a, b: BS((a, b), lambda i: (0, 0))
    lay = lambda a, b: BS((None, a, b), lambda i: (l, 0, 0))
    tab = BS((tm, 128), lambda i: (i, 0))
    row = lambda w: BS((tm, w), lambda i: (i, 0))
    return _pc(body, name=name, grid=(S // tm,),
               in_specs=[BS((512, tm), lambda i: (0, i)), row(512), row(512), BS((tm, 256), lambda i: (i, 0)),
                         BS((tm, 128), lambda i: (i, 2)),
                         lay(1, 256), lay(1, 128), lay(256, 512), lay(128, 512), lay(128, 512), tab, tab, tab],
               out_specs=[row(256), row(128), row(128), full(256, 512), full(128, 512), full(128, 512),
                          full(8, 256), full(8, 128)],
               out_shape=[SDS((S, 256), BF16), SDS((S, 128), BF16), SDS((S, 128), BF16), SDS((256, 512), F32),
                          SDS((128, 512), F32), SDS((128, 512), F32), SDS((8, 256), F32), SDS((8, 128), F32)])(
        dq, dk, dv, proj, proj, gq3, gkv3, wuq, wk, wv, tc, ts1, ts2)


def _causal_steps(n, q_outer):
    if q_outer:
        pairs = [(i, j) for i in range(n) for j in range(i + 1)]
    else:
        pairs = [(i, j) for j in range(n) for i in range(j, n)]
    return jnp.asarray([p[0] for p in pairs], jnp.int32), jnp.asarray([p[1] for p in pairs], jnp.int32)


def _mla_attn(q, k, vt, gts, layer, name):
    S = q.shape[0]
    t = min(512, S)
    n = S // t
    ng = len(gts)

    qi, kj = _causal_steps(n, True)
    last = qi.shape[0] - 1

    def body(qi_ref, kj_ref, q_ref, k_ref, vt_ref, *rest):
        (ya_ref, lse_ref), g_refs = rest[ng:ng + 2], rest[ng + 2:2 * ng + 2]
        m_sc, l_sc, acc_sc = rest[2 * ng + 2:2 * ng + 5]
        i, j = qi_ref[pl.program_id(1)], kj_ref[pl.program_id(1)]
        if ng:
            phases = _gather_phases(g_refs, [g.shape for g in gts], rest[2 * ng + 5], rest[2 * ng + 6], layer)
            for ph, (pp, ss) in zip(phases[:3], ((0, 0), (1, 0), (1, 2 * last // 3))):
                pl.when((pl.program_id(0) == pp) & (pl.program_id(1) == ss))(ph)

        @pl.when(j == 0)
        def _():
            m_sc[...] = jnp.full_like(m_sc, NEG_INF)
            l_sc[...] = jnp.zeros_like(l_sc)
            acc_sc[...] = jnp.zeros_like(acc_sc)

        def step(masked):
            for hh in range(2):
                sl = slice(hh * 128, (hh + 1) * 128)
                st = _dot_nt(k_ref[:, sl], q_ref[:, sl])
                if masked:
                    key = lax.broadcasted_iota(jnp.int32, (t, t), 0)
                    qry = lax.broadcasted_iota(jnp.int32, (t, t), 1)
                    st = jnp.where(key <= qry, st, NEG_INF)
                m_prev = m_sc[hh]
                m_new = jnp.maximum(m_prev, jnp.max(st, axis=0, keepdims=True))
                p = jnp.exp(st - m_new)
                alpha = jnp.exp(m_prev - m_new)
                l_sc[hh] = alpha * l_sc[hh] + jnp.sum(p, axis=0, keepdims=True)
                acc_sc[hh] = alpha * acc_sc[hh] + _dot(vt_ref[sl, :], p.astype(BF16))
                m_sc[hh] = m_new

        @pl.when(j < i)
        def _():
            step(False)

        @pl.when(j == i)
        def _():
            step(True)
            ya_ref[...] = (acc_sc[0] / l_sc[0] + acc_sc[1] / l_sc[1]).T
            for hh in range(2):
                lse_ref[hh] = m_sc[hh] + jnp.log(l_sc[hh])

        if ng:
            pl.when((pl.program_id(0) == 1) & (pl.program_id(1) == last))(phases[3])

    gs = pltpu.PrefetchScalarGridSpec(
        num_scalar_prefetch=2, grid=(2, qi.shape[0]),
        in_specs=[BS((t, 256), lambda p, s, qi, kj: (qi[s], p)), BS((t, 256), lambda p, s, qi, kj: (kj[s], p)),
                  BS((256, t), lambda p, s, qi, kj: (p, kj[s]))] + [ANY] * ng,
        out_specs=[BS((t, 128), lambda p, s, qi, kj: (qi[s], p)), BS((2, 1, t), lambda p, s, qi, kj: (p, 0, qi[s]))]
        + [ANY] * ng,
        scratch_shapes=[pltpu.VMEM((2, 1, t), F32), pltpu.VMEM((2, 1, t), F32), pltpu.VMEM((2, 128, t), F32)]
        + ([pltpu.SemaphoreType.DMA((7 * ng,)), pltpu.SemaphoreType.DMA((7 * ng,))] if ng else []))
    out = pl.pallas_call(body, name=name, grid_spec=gs,
                         out_shape=[SDS((S, 256), F32), SDS((HEADS, 1, S), F32)] + [SDS(g.shape, g.dtype) for g in gts],
                         input_output_aliases={5 + m: 2 + m for m in range(ng)},
                         compiler_params=pltpu.CompilerParams(dimension_semantics=("arbitrary",) * 2,
                                                              vmem_limit_bytes=VMEM_LIMIT, has_side_effects=bool(ng)))(
        qi, kj, q, k, vt, *gts)
    return out[0], out[1], list(out[2:])


def _mla_delta(dycat, ya, name, after=None):
    S = ya.shape[0]
    t = min(512, S)

    def body(do_ref, ya_ref, *refs):
        d_ref = refs[-1]
        prod = do_ref[...] * ya_ref[...]
        for p in range(2):
            pt = prod[:, p * 128:(p + 1) * 128].T
            d_ref[2 * p] = jnp.sum(pt[0:64, :], axis=0, keepdims=True)
            d_ref[2 * p + 1] = jnp.sum(pt[64:128, :], axis=0, keepdims=True)

    extra = [] if after is None else [after]
    return _pc(body, name=name, grid=(S // t,),
               in_specs=[BS((t, 256), lambda i: (i, 0)), BS((t, 256), lambda i: (i, 0))]
               + [BS((8, 128), lambda i: (0, 0)) for _ in extra],
               out_specs=BS((HEADS, 1, t), lambda i: (0, 0, i)), out_shape=SDS((HEADS, 1, S), F32))(dycat, ya, *extra)


def _mla_attn_bwd(q, k, kt, v, dya, lse, delta, rider, name):
    S = q.shape[0]
    t = min(512, S)
    n = S // t
    nr = rider.n if rider else 0

    qi, kj = _causal_steps(n, False)
    last = qi.shape[0] - 1

    def body(qi_ref, kj_ref, q_ref, k_ref, kt_ref, v_ref, do_ref, lse_ref, delta_ref, *rest):
        dqt_ref, dk_ref, dv_ref = rest[nr:nr + 3]
        r_io = (rest[:nr], rest[nr + 3:2 * nr + 3], rest[2 * nr + 3:])
        i, j = qi_ref[pl.program_id(1)], kj_ref[pl.program_id(1)]
        if rider:
            pl.when((pl.program_id(0) == 0) & (pl.program_id(1) == 0))(lambda: rider.start(*r_io))

        @pl.when(pl.program_id(1) == 0)
        def _():
            dqt_ref[...] = jnp.zeros_like(dqt_ref)

        @pl.when(i == j)
        def _():
            dk_ref[...] = jnp.zeros_like(dk_ref)
            dv_ref[...] = jnp.zeros_like(dv_ref)

        def step(masked):
            dob = do_ref[...].astype(BF16)
            cols = pl.ds(pl.multiple_of(i * t, t), t)
            for hh in range(2):
                sl = slice(hh * 128, (hh + 1) * 128)
                qv = q_ref[:, sl]
                p = jnp.exp(_dot_nt(k_ref[:, sl], qv) - lse_ref[hh])
                if masked:
                    key = lax.broadcasted_iota(jnp.int32, (t, t), 0)
                    qry = lax.broadcasted_iota(jnp.int32, (t, t), 1)
                    p = jnp.where(key <= qry, p, 0.0)
                dv_ref[:, sl] += _dot(p.astype(BF16), dob)
                ds = (p * (_dot_nt(v_ref[:, sl], dob) - delta_ref[hh])).astype(BF16)
                dk_ref[:, sl] += _dot(ds, qv)
                dqt_ref[sl, cols] += _dot(kt_ref[sl, :], ds)

        @pl.when(i > j)
        def _():
            step(False)

        @pl.when(i == j)
        def _():
            step(True)

        if rider:
            pl.when((pl.program_id(0) == 1) & (pl.program_id(1) == last))(lambda: rider.wait(*r_io))

    qs = BS((t, 256), lambda p, s, qi, kj: (qi[s], p))
    ks = BS((t, 256), lambda p, s, qi, kj: (kj[s], p))
    rowv = BS((2, 1, t), lambda p, s, qi, kj: (p, 0, qi[s]))
    gs = pltpu.PrefetchScalarGridSpec(
        num_scalar_prefetch=2, grid=(2, qi.shape[0]),
        in_specs=[qs, ks, BS((256, t), lambda p, s, qi, kj: (p, kj[s])), ks,
                  BS((t, 128), lambda p, s, qi, kj: (qi[s], p)), rowv, rowv] + [ANY] * nr,
        out_specs=[BS((256, S), lambda p, s, qi, kj: (p, 0)), ks, ks] + [ANY] * nr,
        scratch_shapes=rider.scratch() if rider else [])
    out = pl.pallas_call(body, name=name, grid_spec=gs,
                         out_shape=[SDS((512, S), F32), SDS((S, 512), F32), SDS((S, 512), F32)]
                         + (rider.out_shape if rider else []),
                         compiler_params=pltpu.CompilerParams(dimension_semantics=("arbitrary",) * 2,
                                                              vmem_limit_bytes=VMEM_LIMIT, has_side_effects=bool(rider)))(
        qi, kj, q, k, kt, v, dya, lse, delta, *(rider.arrs if rider else []))
    return out[0], out[1], out[2], list(out[3:])


def _swa_scores(qm, kk, valid, bias, sink):
    sc = jnp.where(valid, _dot_nt(qm, kk) * SWA_SCALE + bias, NEG_INF)
    m = jnp.maximum(jnp.max(sc, axis=-1, keepdims=True), sink)
    e = jnp.exp(sc - m)
    esink = jnp.exp(sink - m)
    den = jnp.sum(e, axis=-1, keepdims=True) + esink
    return e / den, esink / den


def _swa_consts(sink_ref, l):
    rows = HEADS * BLK
    r = lax.broadcasted_iota(jnp.int32, (rows, 2 * BLK), 0)
    c = lax.broadcasted_iota(jnp.int32, (rows, 2 * BLK), 1)
    dist = (r & (BLK - 1)) + BLK - c
    head = lax.broadcasted_iota(jnp.int32, (rows, 1), 0) // BLK

    def per_head(vals):
        return jnp.where(head == 0, vals[0], jnp.where(head == 1, vals[1], jnp.where(head == 2, vals[2], vals[3])))

    bias = -per_head(SLOPES) * dist.astype(F32)
    sink = per_head([sink_ref[l, h] for h in range(HEADS)])
    return (dist >= 0) & (dist < SWA_WINDOW), c >= BLK, bias, sink, head


def _to_half(xb, pos, b):
    return xb if pos == b else pltpu.roll(xb, 64, axis=1)


def _swa_stack(ref, st, lo):
    parts = []
    for b in range(2):
        xb = ref[pl.ds(st, BLK), b * 128:(b + 1) * 128]
        half = lo if b == 0 else ~lo
        parts += [jnp.where(half, _to_half(xb, pos, b), 0.0).astype(BF16) for pos in range(2)]
    return jnp.concatenate(parts, axis=0)


def _swa_unstack(x_all, lo):
    blocks = []
    for b in range(2):
        h0, h1 = (_to_half(x_all[(2 * b + pos) * BLK:(2 * b + pos + 1) * BLK], pos, b) for pos in range(2))
        blocks.append(jnp.where(lo, h0, h1))
    return blocks


def _swa(proj, sinks, l, name):
    S = proj.shape[0]
    nb = S // BLK

    def body(q_ref, k_ref, v_ref, sink_ref, o_ref, kp, vp):
        kp[0:BLK, :] = jnp.zeros((BLK, 128), BF16)
        vp[0:BLK, :] = jnp.zeros((BLK, 128), BF16)
        kp[BLK:, :] = k_ref[...].astype(BF16)
        vp[BLK:, :] = v_ref[...].astype(BF16)
        lo = lax.broadcasted_iota(jnp.int32, (BLK, 128), 1) < 64
        band, cur, bias, sink, _ = _swa_consts(sink_ref, l)

        def blk(i, carry):
            st = pl.multiple_of(i * BLK, BLK)
            kk = kp[pl.ds(st, 2 * BLK), :]
            vv = vp[pl.ds(st, 2 * BLK), :]
            p, _ = _swa_scores(_swa_stack(q_ref, st, lo), kk, band & (cur | (i > 0)), bias, sink)
            for b, ob in enumerate(_swa_unstack(_dot(p.astype(BF16), vv), lo)):
                o_ref[pl.ds(st, BLK), b * 128:(b + 1) * 128] = ob
            return carry

        lax.fori_loop(0, nb, blk, 0, unroll=2)

    return _pc(body, name=name, grid=(1,),
               in_specs=[BS((S, 256), lambda i: (0, C_QS // 256)), BS((S, 128), lambda i: (0, C_KS // 128)),
                         BS((S, 128), lambda i: (0, C_VS // 128)), BS(memory_space=pltpu.SMEM)],
               out_specs=BS((S, 256), lambda i: (0, 0)),
               out_shape=SDS((S, 256), F32),
               scratch=[pltpu.VMEM((S + BLK, 128), BF16), pltpu.VMEM((S + BLK, 128), BF16)])(proj, proj, proj, sinks)


def _swa_bwd(proj, sinks, dyd, l, name, rider=None):
    S = proj.shape[0]
    nb = S // BLK
    nr = rider.n if rider else 0

    def body(q_ref, k_ref, v_ref, sink_ref, do_ref, *rest):
        dq_ref, dk_ref, dv_ref, dsink_ref = rest[nr:nr + 4]
        kp, vp, dkp, dvp = rest[2 * nr + 4:2 * nr + 8]
        r_io = (rest[:nr], rest[nr + 4:2 * nr + 4], rest[2 * nr + 8:])
        if rider:
            rider.start(*r_io)
        kp[0:BLK, :] = jnp.zeros((BLK, 128), BF16)
        vp[0:BLK, :] = jnp.zeros((BLK, 128), BF16)
        kp[BLK:, :] = k_ref[...].astype(BF16)
        vp[BLK:, :] = v_ref[...].astype(BF16)
        dkp[...] = jnp.zeros_like(dkp)
        dvp[...] = jnp.zeros_like(dvp)
        lo = lax.broadcasted_iota(jnp.int32, (BLK, 128), 1) < 64
        lane8 = lax.broadcasted_iota(jnp.int32, (8, 128), 1)
        band, cur, bias, sink, head = _swa_consts(sink_ref, l)

        def blk(i, dsink):
            st = pl.multiple_of(i * BLK, BLK)
            kk = kp[pl.ds(st, 2 * BLK), :]
            vv = vp[pl.ds(st, 2 * BLK), :]
            qm, dom = _swa_stack(q_ref, st, lo), _swa_stack(do_ref, st, lo)
            p, psink = _swa_scores(qm, kk, band & (cur | (i > 0)), bias, sink)
            dp = _dot_nt(dom, vv)
            dvp[pl.ds(st, 2 * BLK), :] += _dot_tn(p.astype(BF16), dom)
            delta = jnp.sum(p * dp, axis=-1, keepdims=True)
            dsk = -psink * delta
            for h in range(HEADS):
                dsink = dsink + jnp.where(lane8 == h, jnp.sum(jnp.where(head == h, dsk, 0.0)), 0.0)
            dsc = (p * (dp - delta) * SWA_SCALE).astype(BF16)
            for b, dqb in enumerate(_swa_unstack(_dot(dsc, kk), lo)):
                dq_ref[pl.ds(st, BLK), b * 128:(b + 1) * 128] = dqb.astype(BF16)
            dkp[pl.ds(st, 2 * BLK), :] += _dot_tn(dsc, qm)
            return dsink

        dsink_ref[...] = lax.fori_loop(0, nb, blk, jnp.zeros((8, 128), F32), unroll=2)
        dk_ref[...] = dkp[BLK:, :].astype(BF16)
        dv_ref[...] = dvp[BLK:, :].astype(BF16)
        if rider:
            rider.wait(*r_io)

    in_specs = [BS((S, 256), lambda i: (0, C_QS // 256)), BS((S, 128), lambda i: (0, C_KS // 128)),
                BS((S, 128), lambda i: (0, C_VS // 128)), BS(memory_space=pltpu.SMEM), BS((S, 256), lambda i: (0, 3))]
    out_specs = [BS((S, 256), lambda i: (0, 0)), BS((S, 128), lambda i: (0, 0)), BS((S, 128), lambda i: (0, 0)),
                 BS((8, 128), lambda i: (0, 0))]
    out_shape = [SDS((S, 256), BF16), SDS((S, 128), BF16), SDS((S, 128), BF16), SDS((8, 128), F32)]
    scratch = [pltpu.VMEM((S + BLK, 128), BF16), pltpu.VMEM((S + BLK, 128), BF16),
               pltpu.VMEM((S + BLK, 128), F32), pltpu.VMEM((S + BLK, 128), F32)]
    if not rider:
        return _pc(body, name=name, grid=(1,), in_specs=in_specs, out_specs=out_specs, out_shape=out_shape,
                   scratch=scratch)(proj, proj, proj, sinks, dyd)
    out = pl.pallas_call(body, name=name, grid=(1,), in_specs=in_specs + [ANY] * nr, out_specs=out_specs + [ANY] * nr,
                         out_shape=out_shape + rider.out_shape, scratch_shapes=scratch + rider.scratch(),
                         compiler_params=pltpu.CompilerParams(dimension_semantics=("arbitrary",),
                                                              vmem_limit_bytes=VMEM_LIMIT, has_side_effects=True))(
        proj, proj, proj, sinks, dyd, *rider.arrs)
    return (*out[:4], list(out[4:]))


def _down(x, k, t):
    return jnp.where(t >= k, pltpu.roll(x, k, axis=0), 0.0)


def _up(x, k, t):
    n = x.shape[0]
    return jnp.where(t < n - k, pltpu.roll(x, n - k, axis=0), 0.0)


def _conv(proj, w8, l, name):
    S = proj.shape[0]

    def body(gb_ref, gc_ref, u_ref, w_ref, y_ref):
        t = lax.broadcasted_iota(jnp.int32, (S, 128), 0)
        z = gc_ref[...] * u_ref[...]
        c = w_ref[2:3, :] * z + w_ref[1:2, :] * _down(z, 1, t) + w_ref[0:1, :] * _down(z, 2, t)
        y_ref[...] = gb_ref[...] * c

    col = lambda c0: BS((S, 128), lambda i: (0, c0 // 128 + i))
    return _pc(body, name=name, grid=(2,),
               in_specs=[col(C_GB), col(C_GC), col(C_UC), BS((None, 8, 128), lambda i: (l, 0, i))],
               out_specs=BS((S, 128), lambda i: (0, i)), out_shape=SDS((S, 256), F32))(proj, proj, proj, w8)


def _conv_bwd(proj, w8, dycat, l, name):
    S = proj.shape[0]

    def body(gb_ref, gc_ref, u_ref, w_ref, dy_ref, dgb_ref, dgc_ref, du_ref, dw_ref):
        t = lax.broadcasted_iota(jnp.int32, (S, 128), 0)
        gc, u = gc_ref[...], u_ref[...]
        z = gc * u
        z1, z2 = _down(z, 1, t), _down(z, 2, t)
        w0, w1, w2 = w_ref[0:1, :], w_ref[1:2, :], w_ref[2:3, :]
        dy = dy_ref[...]
        dgb_ref[...] = (dy * (w2 * z + w1 * z1 + w0 * z2)).astype(BF16)
        dc = dy * gb_ref[...]
        dz = w2 * dc + w1 * _up(dc, 1, t) + w0 * _up(dc, 2, t)
        dgc_ref[...] = (dz * u).astype(BF16)
        du_ref[...] = (dz * gc).astype(BF16)
        row = lax.broadcasted_iota(jnp.int32, (8, 128), 0)
        sums = [jnp.sum(dc * zz, axis=0, keepdims=True) for zz in (z2, z1, z)]
        dw_ref[...] = jnp.where(row == 0, sums[0], jnp.where(row == 1, sums[1], jnp.where(row == 2, sums[2], 0.0)))

    col = lambda c0: BS((S, 128), lambda i: (0, c0 // 128 + i))
    out = BS((S, 128), lambda i: (0, i))
    return _pc(body, name=name, grid=(2,),
               in_specs=[col(C_GB), col(C_GC), col(C_UC), BS((None, 8, 128), lambda i: (l, 0, i)), col(256)],
               out_specs=[out, out, out, BS((8, 128), lambda i: (0, i))],
               out_shape=[SDS((S, 256), BF16)] * 3 + [SDS((8, 256), F32)])(proj, proj, proj, w8, dycat)


def _pool_parts(u, t, first):
    lo = lax.broadcasted_iota(jnp.int32, u.shape, 1) < 64
    s2 = u + _down(u, 1, t)
    s4 = s2 + _down(s2, 2, t)
    s8 = s4 + _down(s4, 4, t)
    s16 = s8 + _down(s8, 8, t)
    win = jnp.where(lo, jnp.where(first, s2, s8), jnp.where(first, s4, s16))
    wv = jnp.where(lo, jnp.where(first, 2, 8), jnp.where(first, 4, 16))
    cnt = jnp.minimum(t + 1, wv).astype(F32)
    return win, cnt, lo


def _pool(proj, pwd, scale3, l, name):
    S = proj.shape[0]

    def body(u_ref, pw_ref, sc_ref, y_ref):
        t = lax.broadcasted_iota(jnp.int32, (S, 128), 0)
        u = u_ref[...]
        win, cnt, _ = _pool_parts(u, t, pl.program_id(0) == 0)
        pooled = win / cnt - u
        y_ref[...] = _dot(pooled.astype(BF16), pw_ref[...]) * sc_ref[...]

    return _pc(body, name=name, grid=(2,),
               in_specs=[BS((S, 128), lambda i: (0, C_UP // 128 + i)), BS((None, 128, 128), lambda i: (l, i, 0)),
                         BS((None, 1, 128), lambda i: (l, 0, i))],
               out_specs=BS((S, 128), lambda i: (0, i)), out_shape=SDS((S, 256), F32))(proj, pwd, scale3)


def _pool_bwd(proj, pwd, scale3, dycat, l, name):
    S = proj.shape[0]

    def body(u_ref, pw_ref, sc_ref, dy_ref, du_ref, dpw_ref, dsc_ref):
        t = lax.broadcasted_iota(jnp.int32, (S, 128), 0)
        first = pl.program_id(0) == 0
        u = u_ref[...]
        win, cnt, lo = _pool_parts(u, t, first)
        pooled = (win / cnt - u).astype(BF16)
        pw = pw_ref[...]
        dy = dy_ref[...]
        dsc_ref[...] = jnp.broadcast_to(jnp.sum(dy * _dot(pooled, pw), axis=0, keepdims=True), (8, 128))
        dmb = (dy * sc_ref[...]).astype(BF16)
        dpw_ref[...] = _dot_tn(pooled, dmb)
        dpooled = _dot_nt(dmb, pw)
        a1 = dpooled / cnt
        a2 = a1 + _up(a1, 1, t)
        a4 = a2 + _up(a2, 2, t)
        a8 = a4 + _up(a4, 4, t)
        a16 = a8 + _up(a8, 8, t)
        dwin = jnp.where(lo, jnp.where(first, a2, a8), jnp.where(first, a4, a16))
        du_ref[...] = (dwin - dpooled).astype(BF16)

    return _pc(body, name=name, grid=(2,),
               in_specs=[BS((S, 128), lambda i: (0, C_UP // 128 + i)), BS((None, 128, 128), lambda i: (l, i, 0)),
                         BS((None, 1, 128), lambda i: (l, 0, i)), BS((S, 128), lambda i: (0, 4 + i))],
               out_specs=[BS((S, 128), lambda i: (0, i)), BS((128, 128), lambda i: (i, 0)), BS((8, 128), lambda i: (0, i))],
               out_shape=[SDS((S, 256), BF16), SDS((256, 128), F32), SDS((8, 256), F32)])(proj, pwd, scale3, dycat)


def _adamw(w, g, m, v, name, echo=False):
    n, a, b = w.shape
    tr = _row_tile(a, b)

    def body(w_ref, g_ref, m_ref, v_ref, d_ref, nm_ref, nv_ref, *g_out):
        gv = g_ref[...]
        if echo:
            g_out[0][...] = gv
        m_new = B1 * m_ref[...] + (1.0 - B1) * gv
        v_new = B2 * v_ref[...] + (1.0 - B2) * (gv * gv)
        m_hat = m_new / (1.0 - B1 ** STEP)
        v_hat = v_new / (1.0 - B2 ** STEP)
        d_ref[...] = -LR * (m_hat / (jnp.sqrt(v_hat) + ADAM_EPS) + WD * w_ref[...])
        nm_ref[...] = m_new
        nv_ref[...] = v_new

    sp = BS((None, tr, b), lambda i, t: (i, t, 0))
    return _pc(body, name=name, grid=(n, a // tr), in_specs=[sp] * 4, out_specs=[sp] * (3 + echo),
               out_shape=[SDS((n, a, b), F32)] * (3 + echo))(w, g, m, v)


def _prefetch_call(body, name, grid, in_specs, out_specs, out_shape):
    gs = pltpu.PrefetchScalarGridSpec(num_scalar_prefetch=1, grid=grid, in_specs=in_specs, out_specs=out_specs)
    return pl.pallas_call(body, name=name, grid_spec=gs, out_shape=out_shape, compiler_params=_params(len(grid)))


def _place(w, kc, dtype, name):
    _, a, b = w.shape

    def body(kc_ref, w_ref, o_ref):
        o_ref[...] = w_ref[...].astype(dtype)

    return _prefetch_call(body, name, (2,), [BS((None, a, b), lambda l, kc: (l, 0, 0))],
                          BS((None, None, a, b), lambda l, kc: (l, kc[0], 0, 0)), SDS((2, 4, a, b), dtype))(kc, w)


def _pair_sum(g, got, kc, name):
    _, _, a, b = g.shape
    tr = _row_tile(a, b)

    def body(kc_ref, a_ref, b_ref, t32_ref, t16_ref):
        s = a_ref[...] + b_ref[...]
        t16_ref[...] = s.astype(BF16)

        @pl.when(pl.program_id(1) == kc_ref[0])
        def _():
            t32_ref[...] = s

    sp = BS((None, tr, b), lambda t, k, kc: (k, t, 0))
    return _prefetch_call(body, name, (a // tr, 4),
                          [BS((None, None, tr, b), lambda t, k, kc: (kc[1], k, t, 0)), sp],
                          [BS((tr, b), lambda t, k, kc: (t, 0)), sp],
                          [SDS((a, b), F32), SDS((4, a, b), BF16)])(kc, g, got)


def _chip_sum(t32, gots, kc, name, after=None):
    a, b = t32.shape
    tr = _row_tile(a, b)
    ng = len(gots)

    def body(kc_ref, a_ref, *refs):
        acc = a_ref[...]
        for g_ref in refs[:ng]:
            for i in range(g_ref.shape[0]):
                acc = acc + g_ref[i].astype(F32)
        refs[-1][...] = acc

    extra = [] if after is None else [after]
    return _prefetch_call(body, name, (a // tr,),
                          [BS((tr, b), lambda t, kc: (t, 0))]
                          + [BS((g.shape[0], tr, b), lambda t, kc: (0, t, 0)) for g in gots]
                          + [BS((8, 128), lambda t, kc: (0, 0)) for _ in extra],
                          BS((None, tr, b), lambda t, kc: (kc[1], t, 0)), SDS((2, a, b), F32))(kc, t32, *gots, *extra)


def _me():
    return lax.axis_index("x"), lax.axis_index("y"), lax.axis_index("c")


def _other_chips(x, y):
    return [(1 - x, y), (x, 1 - y), (1 - x, 1 - y)]


ANY = BS(memory_space=pl.ANY)
COMM_PARAMS = pltpu.CompilerParams(has_side_effects=True)


def _gather(arrs, name):
    n = len(arrs)

    def body(*refs):
        for phase in _gather_phases(refs[n:2 * n], [a.shape for a in arrs], refs[2 * n], refs[2 * n + 1]):
            phase()

    return pl.pallas_call(body, name=name, out_shape=[SDS(a.shape, a.dtype) for a in arrs],
                          in_specs=[ANY] * n, out_specs=[ANY] * n, input_output_aliases={t: t for t in range(n)},
                          scratch_shapes=[pltpu.SemaphoreType.DMA((7 * n,)), pltpu.SemaphoreType.DMA((7 * n,))],
                          compiler_params=COMM_PARAMS)(*arrs)


def _gather_phases(outs, shapes, send_sems, recv_sems, layer=None):
    n = len(outs)
    cut = [s[2] // 2 // 16 * 16 for s in shapes]
    split = [r > 0 for r in cut]

    def plan():
        x, y, c = _me()
        return (c if layer is None else layer), (x, y), (x, y, c), (x, y, 1 - c), _other_chips(x, y)

    def role(moving, fn):
        if layer is None:
            fn()
        else:
            c = lax.axis_index("c")
            pl.when((c == layer) if moving else (c != layer))(fn)

    def blk(t, chip, layer, half=None):
        r = outs[t].at[layer, 2 * chip[0] + chip[1]]
        if half is None:
            return r
        return r.at[pl.ds(0, cut[t])] if half == 0 else r.at[pl.ds(cut[t], shapes[t][2] - cut[t])]

    def copy(t, k, ref, to):
        return pltpu.make_async_remote_copy(src_ref=ref, dst_ref=ref, send_sem=send_sems.at[7 * t + k],
                                            recv_sem=recv_sems.at[7 * t + k], device_id=to, device_id_type=MESH)

    def own_sends(t):
        c, chip, me, sib, (xn, yn, dg) = plan()
        cps = [copy(t, 0, blk(t, chip, c), (*xn, c)), copy(t, 1, blk(t, chip, c), (*yn, c))]
        return cps if split[t] else cps + [copy(t, 2, blk(t, chip, c), (*dg, c))]

    def relays(t):
        c, chip, me, sib, (xn, yn, dg) = plan()
        after_x = [copy(t, 4, blk(t, xn, c), sib)]
        after_y = [copy(t, 5, blk(t, yn, c), sib)]
        if split[t]:
            after_x.insert(0, copy(t, 2, blk(t, xn, c, 0), (*yn, c)))
            after_y.insert(0, copy(t, 3, blk(t, yn, c, 1), (*xn, c)))
        return after_x, after_y, [copy(t, 6, blk(t, dg, c), sib)]

    def send_own():
        for t in range(n):
            for cp in own_sends(t):
                cp.start()

    def relay_neighbours():
        c, chip, me, sib, (xn, yn, dg) = plan()
        for t in range(n):
            after_x, after_y, _ = relays(t)
            copy(t, 0, blk(t, xn, c), me).wait_recv()
            for cp in after_x:
                cp.start()
            copy(t, 1, blk(t, yn, c), me).wait_recv()
            for cp in after_y:
                cp.start()

    def relay_diagonal():
        c, chip, me, sib, (xn, yn, dg) = plan()
        for t in range(n):
            if split[t]:
                copy(t, 2, blk(t, dg, c, 0), me).wait_recv()
                copy(t, 3, blk(t, dg, c, 1), me).wait_recv()
            else:
                copy(t, 2, blk(t, dg, c), me).wait_recv()
            relays(t)[2][0].start()

    def take_sibling():
        _, chip, me, sib, (xn, yn, dg) = plan()
        theirs = 1 - lax.axis_index("c") if layer is None else layer
        for t in range(n):
            for k, peer in ((4, xn), (5, yn), (6, dg)):
                copy(t, k, blk(t, peer, theirs), me).wait_recv()

    def drain_sends():
        for t in range(n):
            after_x, after_y, after_d = relays(t)
            for cp in own_sends(t) + after_x + after_y + after_d:
                cp.wait_send()

    def finish():
        role(False, take_sibling)
        role(True, drain_sends)

    return ((lambda: role(True, send_own)), (lambda: role(True, relay_neighbours)),
            (lambda: role(True, relay_diagonal)), finish)


def _swap_copies(ins, outs, send_sems, recv_sems):
    x, y, c = _me()
    return [pltpu.make_async_remote_copy(src_ref=ins[t].at[1 - c], dst_ref=outs[t], send_sem=send_sems.at[t],
                                         recv_sem=recv_sems.at[t], device_id=(x, y, 1 - c), device_id_type=MESH)
            for t in range(len(ins))]


def _exchange_copies(peers, ins, outs, send_sems, recv_sems):
    x, y, c = _me()
    chips = _other_chips(x, y)
    n = len(peers)
    return [pltpu.make_async_remote_copy(src_ref=ins[t].at[2 * chips[j][0] + chips[j][1]], dst_ref=outs[t].at[i],
                                         send_sem=send_sems.at[n * t + i], recv_sem=recv_sems.at[n * t + i],
                                         device_id=(*chips[j], c), device_id_type=MESH)
            for i, j in enumerate(peers) for t in range(len(ins))]


class _Rider:
    def __init__(self, arrs, out_shape, nsem, copies):
        self.arrs, self.out_shape, self.nsem, self.copies = list(arrs), out_shape, nsem, copies
        self.n = len(self.arrs)

    def scratch(self):
        return [pltpu.SemaphoreType.DMA((self.nsem,)), pltpu.SemaphoreType.DMA((self.nsem,))]

    def start(self, ins, outs, sems):
        for cp in self.copies(ins, outs, *sems):
            cp.start()

    def wait(self, ins, outs, sems):
        for cp in self.copies(ins, outs, *sems):
            cp.wait()


def _swap_rider(gs):
    return _Rider(gs, [SDS(g.shape[1:], g.dtype) for g in gs], len(gs), _swap_copies)


def _exchange_rider(ts, peers=(0, 1, 2)):
    return _Rider(ts, [SDS((len(peers),) + t.shape[1:], t.dtype) for t in ts], len(peers) * len(ts),
                  functools.partial(_exchange_copies, peers))


HBM = BS(memory_space=pltpu.HBM)
SEM = BS(memory_space=pltpu.SEMAPHORE)
SPLIT_PARAMS = pltpu.CompilerParams(has_side_effects=pltpu.SideEffectType.DATAFLOW_SIDE_EFFECTING)


def _copies_start(rider, name):
    n = rider.n
    lands = [lax.empty(s.shape, s.dtype) for s in rider.out_shape]

    def body(*refs):
        rider.start(refs[:n], refs[n:2 * n], refs[2 * n:2 * n + 2])
        refs[-1][...] = jnp.zeros_like(refs[-1])

    held = [pltpu.with_memory_space_constraint(a, pltpu.HBM) for a in rider.arrs + lands]
    out = pl.pallas_call(
        body, name=name,
        out_shape=(pltpu.SemaphoreType.DMA((rider.nsem,)), pltpu.SemaphoreType.DMA((rider.nsem,)),
                   *[pltpu.HBM(a.shape, a.dtype) for a in held], SDS((8, 128), F32)),
        in_specs=[HBM] * (2 * n), out_specs=(SEM, SEM, *[HBM] * (2 * n), BS(memory_space=pltpu.VMEM)),
        input_output_aliases={i: 2 + i for i in range(2 * n)}, compiler_params=SPLIT_PARAMS)(*held)
    return rider, out[0], out[1], list(out[2:2 + n]), list(out[2 + n:2 + 2 * n]), out[-1]


def _copies_wait(rider, send_sems, recv_sems, srcs, lands, after, name):
    n = rider.n

    def body(*refs):
        for cp in rider.copies(refs[:n], refs[n:2 * n], refs[2 * n], refs[2 * n + 1]):
            cp.wait_send()
            cp.wait_recv()

    out = pl.pallas_call(
        body, name=name, out_shape=tuple(pltpu.HBM(a.shape, a.dtype) for a in srcs + lands),
        in_specs=[HBM] * (2 * n) + [SEM, SEM, ANY], out_specs=[HBM] * (2 * n),
        input_output_aliases={i: i for i in range(2 * n)}, compiler_params=SPLIT_PARAMS)(
        *srcs, *lands, send_sems, recv_sems, after)
    return list(out[n:2 * n])


def _spread_copies(ins, outs, send_sems, recv_sems):
    x, y, c = _me()
    me = 4 * x + 2 * y + c
    return [pltpu.make_async_remote_copy(src_ref=ins[0], dst_ref=outs[0].at[me], send_sem=send_sems.at[m - 1],
                                         recv_sem=recv_sems.at[m - 1],
                                         device_id=(1 - x if m & 4 else x, 1 - y if m & 2 else y, 1 - c if m & 1 else c),
                                         device_id_type=MESH)
            for m in range(1, 8)]


def _spread_rider(v):
    return _Rider([v], [SDS((8,) + v.shape, v.dtype)], 7, _spread_copies)


def _sum8(land, v, me, name):
    M = v.shape[0]

    def body(me_ref, land_ref, v_ref, o_ref):
        acc = jnp.where(me_ref[0] == 0, v_ref[...], land_ref[0])
        for d in range(1, 8):
            acc = acc + jnp.where(me_ref[0] == d, v_ref[...], land_ref[d])
        o_ref[...] = acc

    return _prefetch_call(body, name, (1,), [BS((8, M, LANES), lambda i, me: (0, 0, 0)), BS((M, LANES), lambda i, me: (0, 0))],
                          BS((M, LANES), lambda i, me: (0, 0)), SDS((M, LANES), F32))(me, land, v)


def _ride_alone(rider, name):
    n = rider.n

    def body(*refs):
        rider.start(refs[:n], refs[n:2 * n], refs[2 * n:])
        rider.wait(refs[:n], refs[n:2 * n], refs[2 * n:])

    return pl.pallas_call(body, name=name, out_shape=rider.out_shape, in_specs=[ANY] * n, out_specs=[ANY] * n,
                          scratch_shapes=rider.scratch(), compiler_params=COMM_PARAMS)(*rider.arrs)


def _join_layers(us, name):
    n = len(us)

    def body(*refs):
        outs, send_sems, recv_sems = refs[n:2 * n], refs[2 * n], refs[2 * n + 1]
        x, y, c = _me()
        cps = [pltpu.make_async_remote_copy(src_ref=outs[t].at[c], dst_ref=outs[t].at[c], send_sem=send_sems.at[t],
                                            recv_sem=recv_sems.at[t], device_id=(x, y, 1 - c), device_id_type=MESH)
               for t in range(n)]
        for cp in cps:
            cp.start()
        for cp in cps:
            cp.wait()

    return pl.pallas_call(body, name=name, out_shape=[SDS(u.shape, u.dtype) for u in us],
                          in_specs=[ANY] * n, out_specs=[ANY] * n, input_output_aliases={t: t for t in range(n)},
                          scratch_shapes=[pltpu.SemaphoreType.DMA((n,)), pltpu.SemaphoreType.DMA((n,))],
                          compiler_params=COMM_PARAMS)(*us)


def _allsum_small(v, name, after):
    M = v.shape[0]

    def body(x_ref, after_ref, o_ref, all_ref, send_sems, recv_sems, local_sem):
        x, y, c = _me()
        me, sib = (x, y, c), (x, y, 1 - c)
        chips = _other_chips(x, y)

        def rows(px, py, pc):
            return all_ref.at[pl.ds((4 * px + 2 * py + pc) * M, M), :]

        def copy(k, block, to, src=None):
            return pltpu.make_async_remote_copy(src_ref=rows(*block) if src is None else src, dst_ref=rows(*block),
                                                send_sem=send_sems.at[k], recv_sem=recv_sems.at[k],
                                                device_id=to, device_id_type=MESH)

        mine = pltpu.make_async_copy(x_ref, rows(*me), local_sem)
        mine.start()
        first = [copy(0, me, sib, src=x_ref)]
        first += [copy(1 + j, me, (*chip, c), src=x_ref) for j, chip in enumerate(chips)]
        for cp in first:
            cp.start()
        passed = [copy(4 + j, (*chip, c), sib) for j, chip in enumerate(chips)]
        for j, chip in enumerate(chips):
            copy(1 + j, (*chip, c), me).wait_recv()
            passed[j].start()
        copy(0, sib, me).wait_recv()
        for j, chip in enumerate(chips):
            copy(4 + j, (*chip, 1 - c), me).wait_recv()
        for cp in first + passed:
            cp.wait_send()
        mine.wait()
        acc = all_ref[0:M, :]
        for d in range(1, 8):
            acc = acc + all_ref[d * M:(d + 1) * M, :]
        o_ref[...] = acc

    vm = BS(memory_space=pltpu.VMEM)
    return pl.pallas_call(body, name=name, out_shape=SDS((M, LANES), F32), in_specs=[vm, ANY], out_specs=vm,
                          scratch_shapes=[pltpu.VMEM((8 * M, LANES), F32), pltpu.SemaphoreType.DMA((7,)),
                                          pltpu.SemaphoreType.DMA((7,)), pltpu.SemaphoreType.DMA],
                          compiler_params=pltpu.CompilerParams(has_side_effects=True, vmem_limit_bytes=VMEM_LIMIT))(
        v, after)


FFN = ("w_gate_up", "w_down")
REST = ("w_in", "w_o", "w_uq", "w_ukv")
BIG = FFN + REST
TINY = ("conv_w",)
REPL = ("attn_norm", "mla_q_norm", "mla_kv_norm", "pool_w", "pool_scale", "swa_sinks", "mix_norm", "ffn_norm",
        "final_norm")
ORDER = ("attn_norm", "w_in", "mla_q_norm", "w_uq", "mla_kv_norm", "w_ukv", "conv_w", "pool_w", "pool_scale",
         "swa_sinks", "mix_norm", "w_o", "ffn_norm", "w_gate_up", "w_down", "final_norm")


def _rows8(shape):
    return -(-int(np.prod(shape)) // (8 * LANES)) * 8


def _pack(arrs):
    parts = []
    for a in arrs:
        r = _rows8(a.shape)
        parts.append(jnp.pad(a.reshape(-1), (0, r * LANES - a.size)).reshape(r, LANES))
    return jnp.concatenate(parts, axis=0)


def _unpack(buf, shapes):
    out, r0 = [], 0
    for s in shapes:
        n, r = int(np.prod(s)), _rows8(s)
        rows = buf[r0:r0 + r]
        out.append(rows.reshape(s) if n == r * LANES else rows.reshape(-1)[:n].reshape(s))
        r0 += r
    return out


def _cols_joined(g):
    return jnp.transpose(g, (0, 2, 1, 3)).reshape(g.shape[0], g.shape[2], 4 * g.shape[3])


def _cols_split(w):
    n, a, b4 = w.shape
    return jnp.transpose(w.reshape(n, a, 4, b4 // 4), (0, 2, 1, 3))


def _rope_tables(S):
    inv = 1.0 / (10000.0 ** (jnp.arange(0, 32, 2, dtype=F32) / 32))
    ang = jnp.arange(S, dtype=F32)[:, None] * inv[None, :]
    cos, sin = jnp.cos(ang), jnp.sin(ang)
    z = lambda w: jnp.zeros((S, w), F32)
    tc = jnp.concatenate([jnp.ones((S, 64), F32), cos, cos, jnp.ones((S, 32), F32)], axis=1)
    ts1 = jnp.concatenate([z(64), -sin, z(48)], axis=1)
    ts2 = jnp.concatenate([z(80), sin, z(32)], axis=1)
    return tc, ts1, ts2


def _pad_w_in(wt):
    z = lambda n: jnp.zeros((wt.shape[0], n, wt.shape[2]), wt.dtype)
    return jnp.concatenate([wt[:, 0:384], z(64), wt[:, 384:416], z(32), wt[:, 416:1952]], axis=1)


def _unpad_w_in(dt):
    return jnp.concatenate([dt[:, 0:384], dt[:, 448:480], dt[:, 512:2048]], axis=1)


def _pad_heads(w, src, offs):
    cols = []
    for h in range(HEADS):
        src0, n = src[h]
        z = lambda k: jnp.zeros(w.shape[:-1] + (k,), w.dtype)
        cols += [z(offs[h]), w[..., src0:src0 + n], z(128 - offs[h] - n)]
    return jnp.concatenate(cols, axis=-1)


UQ_SRC = [(h * 96, 96) for h in range(HEADS)]
KN_SRC = [(h * 128, 64) for h in range(HEADS)]
V_SRC = [(h * 128 + 64, 64) for h in range(HEADS)]
ZERO_OFF = [0] * HEADS
V_OFF = [(h % 2) * 64 for h in range(HEADS)]


def _unpad_heads(d, src, offs):
    return [d[..., h * 128 + offs[h]: h * 128 + offs[h] + src[h][1]] for h in range(HEADS)]


def kernel(x, attn_norm, w_in, mla_q_norm, w_uq, mla_kv_norm, w_ukv, conv_w, pool_w, pool_scale, swa_sinks, mix_norm, w_o, ffn_norm, w_gate_up, w_down, final_norm, loss_target, m_attn_norm, m_w_in, m_mla_q_norm, m_w_uq, m_mla_kv_norm, m_w_ukv, m_conv_w, m_pool_w, m_pool_scale, m_swa_sinks, m_mix_norm, m_w_o, m_ffn_norm, m_w_gate_up, m_w_down, m_final_norm, v_attn_norm, v_w_in, v_mla_q_norm, v_w_uq, v_mla_kv_norm, v_w_ukv, v_conv_w, v_pool_w, v_pool_scale, v_swa_sinks, v_mix_norm, v_w_o, v_ffn_norm, v_w_gate_up, v_w_down, v_final_norm):
    W = dict(attn_norm=attn_norm, w_in=w_in, mla_q_norm=mla_q_norm, w_uq=w_uq, mla_kv_norm=mla_kv_norm, w_ukv=w_ukv,
             conv_w=conv_w, pool_w=pool_w, pool_scale=pool_scale, swa_sinks=swa_sinks, mix_norm=mix_norm, w_o=w_o,
             ffn_norm=ffn_norm, w_gate_up=w_gate_up, w_down=w_down, final_norm=final_norm)
    M1 = dict(attn_norm=m_attn_norm, w_in=m_w_in, mla_q_norm=m_mla_q_norm, w_uq=m_w_uq, mla_kv_norm=m_mla_kv_norm,
              w_ukv=m_w_ukv, conv_w=m_conv_w, pool_w=m_pool_w, pool_scale=m_pool_scale, swa_sinks=m_swa_sinks,
              mix_norm=m_mix_norm, w_o=m_w_o, ffn_norm=m_ffn_norm, w_gate_up=m_w_gate_up, w_down=m_w_down,
              final_norm=m_final_norm)
    V2 = dict(attn_norm=v_attn_norm, w_in=v_w_in, mla_q_norm=v_mla_q_norm, w_uq=v_w_uq, mla_kv_norm=v_mla_kv_norm,
              w_ukv=v_w_ukv, conv_w=v_conv_w, pool_w=v_pool_w, pool_scale=v_pool_scale, swa_sinks=v_swa_sinks,
              mix_norm=v_mix_norm, w_o=v_w_o, ffn_norm=v_ffn_norm, w_gate_up=v_w_gate_up, w_down=v_w_down,
              final_norm=v_final_norm)
    S = x.shape[1]
    xc, yc, cc = _me()
    chip = 2 * xc + yc
    kc = jnp.stack([chip, cc]).astype(jnp.int32)

    first, later = ("w_in", "w_uq", "w_ukv", "conv_w"), ("w_o", "w_gate_up", "w_down")
    T = lambda a: jnp.swapaxes(a, 1, 2)
    W["w_in"], M1["w_in"], V2["w_in"] = T(w_in), T(m_w_in), T(v_w_in)
    placed = {n: _place(W[n], kc, F32 if n == "conv_w" else BF16, f"place_{n}") for n in first + later}
    gi, gq, gkv, gcv = _gather([placed[n] for n in first], "gather_weights")
    later_w = [placed[n] for n in later]
    win_p = _pad_w_in(gi.reshape(2, 4 * gi.shape[2], D))
    wuq_p = _pad_heads(_cols_joined(gq), UQ_SRC, ZERO_OFF)
    wukv = _cols_joined(gkv)
    wk_p = _pad_heads(wukv, KN_SRC, ZERO_OFF)
    wv_p = _pad_heads(wukv, V_SRC, V_OFF)
    conv8 = jnp.pad(_cols_joined(gcv), ((0, 0), (0, 5), (0, 0)))
    pwd = jnp.concatenate([jnp.concatenate(
        [jnp.pad(pool_w[:, 2 * b], ((0, 0), (0, 0), (0, 64))), jnp.pad(pool_w[:, 2 * b + 1], ((0, 0), (0, 0), (64, 0)))],
        axis=1) for b in range(2)], axis=1).astype(BF16)
    tabs = _rope_tables(S)
    g_attn, g_q, g_kv, g_mix, g_ffn, g_ps = (_g3(W[n]) for n in ("attn_norm", "mla_q_norm", "mla_kv_norm", "mix_norm",
                                                                  "ffn_norm", "pool_scale"))

    xs = [x[0]]
    saved = []
    for l in range(DEPTH):
        x0 = xs[-1]
        proj, h = _norm_mm(x0, g_attn, l, win_p, _wspec_in(l), D_INP, D_INP, F32, f"in_proj{l}", w_t=True)
        q, k, v, kt, vt = _mla_prep(proj, g_q, g_kv, wuq_p, wk_p, wv_p, tabs, l, f"mla_prep{l}")
        ya, lse, later_w = _mla_attn(q, k, vt, later_w, l, f"mla_attn{l}")
        go, gu4, gd = later_w
        wo, wdown = go.reshape(2, D, D), gd.reshape(2, D_FF, D)
        yb = _conv(proj, conv8, l, f"conv{l}")
        ycp = _pool(proj, pwd, g_ps, l, f"pool{l}")
        yd = _swa(proj, swa_sinks, l, f"swa{l}")
        x1, ycat, mixed = _mix_out(x0, ya, yb, ycp, yd, g_mix, wo, l, f"mix_out{l}")
        gu, h2 = _norm_mm(x1, g_ffn, l, gu4, _wspec_gu(l), 2 * D_FF, 2 * D_FF // 4, BF16, f"gate_up{l}")
        x2, act = _swiglu_mm_res(x1, gu, wdown, l, f"down{l}")
        saved.append(dict(x0=x0, proj=proj, h=h, q=q, k=k, kt=kt, v=v, lse=lse, x1=x1, ycat=ycat, mixed=mixed,
                          gu=gu, h2=h2, act=act))
        xs.append(x2)

    dx, dx16, dg_final, loss_tile = _loss_head(xs[-1], final_norm.reshape(1, D), loss_target[0], "loss_head")
    loss_here = (loss_tile[0, 0] * (0.5 / D)).reshape(1)

    G = {n: [None] * DEPTH for n in ("w_uq", "w_ukv") + TINY + REPL if n != "final_norm"}
    gw_in = gw_o = gw_gu = gw_down = None
    for l in reversed(range(DEPTH)):
        sv = saved[l]
        dgu = _bwd_down(dx16, wdown, sv["gu"], l, f"down_bwd{l}")
        gw_down = _mm_tn(sv["act"], dx16, l, gw_down, f"dw_down{l}")
        gw_gu = _mm_tn(sv["h2"], dgu, l, gw_gu, f"dw_gate_up{l}", split4=True)
        ffn_token = None
        if l == 0:
            g_f = [gw_gu, gw_down.reshape(2, 4, D_FF // 4, D)]
            dx1, dx1_16, dg, got_f = _mm_nt_normbwd(dgu, gu4, l, sv["x1"], g_ffn, dx, 1, f"gate_up_bwd{l}",
                                                    rider=_swap_rider(g_f))
            pairs_f = [_pair_sum(g, o, kc, f"rs_pair_sum_{n}") for g, o, n in zip(g_f, got_f, FFN)]
            ffn_flight = _copies_start(_exchange_rider([p[1] for p in pairs_f]), "rs_exchange_start_ffn")
            ffn_token = ffn_flight[-1]
        else:
            dx1, dx1_16, dg = _mm_nt_normbwd(dgu, gu4, l, sv["x1"], g_ffn, dx, 1, f"gate_up_bwd{l}")
        G["ffn_norm"][l] = dg[0]
        gw_o = _mm_tn(sv["mixed"], dx1_16, l, gw_o, f"dw_o{l}")
        dycat, dg = _mm_nt_normbwd(dx1_16, wo.reshape(2, 1, D, D), l, sv["ycat"], g_mix, None, 4, f"mix_bwd{l}")
        G["mix_norm"][l] = dg[0]

        proj = sv["proj"]
        delta = _mla_delta(dycat, sv["ycat"], f"mla_delta{l}", after=ffn_token)
        dq, dk, dv, _ = _mla_attn_bwd(sv["q"], sv["k"], sv["kt"], sv["v"], dycat, sv["lse"], delta, None,
                                      f"mla_attn_bwd{l}")
        dcq, dckv, dkr, dwuq, dwk, dwv, dgq, dgkv = _mla_prep_bwd(
            dq, dk, dv, proj, g_q, g_kv, wuq_p, wk_p, wv_p, tabs, l, f"mla_prep_bwd{l}")
        dgb, dgc, duc, dcw = _conv_bwd(proj, conv8, dycat, l, f"conv_bwd{l}")
        dup, dpw, dps = _pool_bwd(proj, pwd, g_ps, dycat, l, f"pool_bwd{l}")
        dqs, dks, dvs, dsink = _swa_bwd(proj, swa_sinks, dycat, l, f"swa_bwd{l}")
        dproj = jnp.concatenate([dcq, dckv, dkr, dgb, dgc, duc, dup, dqs, dks, dvs], axis=1)
        gw_in = _mm_tn(dproj, sv["h"], l, gw_in, f"dw_in{l}")
        G["w_uq"][l] = jnp.concatenate(_unpad_heads(dwuq, UQ_SRC, ZERO_OFF), axis=1)
        kn, vv = _unpad_heads(dwk, KN_SRC, ZERO_OFF), _unpad_heads(dwv, V_SRC, V_OFF)
        G["w_ukv"][l] = jnp.concatenate([t for h in range(HEADS) for t in (kn[h], vv[h])], axis=1)
        swap_token = None
        if l == 0:
            g_r = [_unpad_w_in(gw_in).reshape(2, 4, -1, D), gw_o.reshape(2, 4, D // 4, D),
                   _cols_split(jnp.stack(G["w_uq"])), _cols_split(jnp.stack(G["w_ukv"]))]
            swap_flight = _copies_start(_swap_rider(g_r), "rs_swap_start")
            swap_token = swap_flight[-1]
        dx, dx16, dg = _mm_nt_normbwd(dproj, win_p.reshape(2, 1, D_INP, D), l, sv["x0"], g_attn, dx1, 1, f"in_proj_bwd{l}",
                                      w_t=True, after=swap_token)
        G["attn_norm"][l] = dg[0]
        G["mla_q_norm"][l] = dgq[0]
        G["mla_kv_norm"][l] = dgkv[0]
        G["conv_w"][l] = dcw[0:3]
        G["pool_w"][l] = jnp.stack([dpw[0:64, 0:64], dpw[64:128, 64:128], dpw[128:192, 0:64], dpw[192:256, 64:128]])
        G["pool_scale"][l] = dps[0]
        G["swa_sinks"][l] = dsink[0, 0:4]
    grad_x = dx[None]
    Gl = {n: jnp.stack(G[n]) for n in TINY + REPL if n != "final_norm"}
    Gl["final_norm"] = dg_final[0]

    got_r = _copies_wait(*swap_flight[:-1], dx16, "rs_swap_wait")
    pairs_r = [_pair_sum(g, o, kc, f"rs_pair_sum_{n}") for g, o, n in zip(g_r, got_r, REST)]
    got3_f = _copies_wait(*ffn_flight[:-1], dx16, "rs_exchange_wait_ffn")
    small = TINY + REPL
    small_v = _pack([Gl[n] for n in small] + [loss_here])
    in_flight = _copies_start(_exchange_rider([p[1] for p in pairs_r]), "rs_exchange_start")
    spread = _copies_start(_spread_rider(small_v), "allsum_start")
    us_f = [_chip_sum(p[0], [o3], kc, f"rs_chip_sum_{n}", after=tok)
            for p, o3, n, tok in zip(pairs_f, got3_f, FFN, (in_flight[-1], spread[-1]))]
    gsum_f = _join_layers(us_f, "rs_join_cores_ffn")
    res = {}

    def update(names, grads):
        for n, g in zip(names, grads):
            d_, m_, v_, g_ = _adamw(W[n], g, M1[n], V2[n], f"adamw_{n}", echo=True)
            back = T if n == "w_in" else (lambda a: a)
            res["g", n], res["d", n], res["m", n], res["v", n] = back(g_), back(d_), back(m_), back(v_)

    update(FFN, gsum_f)
    full_shapes = [Gl[n].shape for n in small] + [(1,)]
    others = _copies_wait(*spread[:-1], res["d", FFN[-1]], "allsum_wait")[0]
    summed = _sum8(others, small_v, jnp.reshape(2 * chip + cc, (1,)).astype(jnp.int32), "allsum_small")
    summed = _unpack(summed, full_shapes)
    loss = summed.pop().reshape(())

    def as3(a):
        if a.ndim <= 2:
            return a.reshape((1,) * (3 - a.ndim) + a.shape)
        return a.reshape(a.shape[0], -1, a.shape[-1])

    for n, g in zip(small, summed):
        if n in TINY:
            wdt = W[n].shape[2]
            g = lax.dynamic_slice_in_dim(g, chip * wdt, wdt, axis=2)
        out = _adamw(as3(W[n]), as3(g), as3(M1[n]), as3(V2[n]), f"adamw_{n}")
        res["g", n] = g
        res["d", n], res["m", n], res["v", n] = (o.reshape(W[n].shape) for o in out)

    got3_r = _copies_wait(*in_flight[:-1], res["v", small[-1]], "rs_exchange_wait")
    us_r = [_chip_sum(p[0], [o3], kc, f"rs_chip_sum_{n}") for p, o3, n in zip(pairs_r, got3_r, REST)]
    update(REST, _join_layers(us_r, "rs_join_cores"))

    return (loss, grad_x, *[res["g", n] for n in ORDER], *[res["d", n] for n in ORDER],
            *[res["m", n] for n in ORDER], *[res["v", n] for n in ORDER])
```

```python
import functools
import math

import numpy as np
import jax
import jax.numpy as jnp
from jax import lax
from jax.experimental import pallas as pl
from jax.experimental.pallas import tpu as pltpu

F32, BF16 = jnp.float32, jnp.bfloat16
SDS = jax.ShapeDtypeStruct
BS = pl.BlockSpec
MESH = pl.DeviceIdType.MESH

D = 1024
DEPTH = 2
HEADS = 4
D_FF = 2816
D_INP = 2048
EPS = 1e-6
SWA_WINDOW = 128
BLK = 128
SLOPES = tuple(2.0 ** (-8.0 * (i + 1) / 4) for i in range(4))
QK_SCALE = 1.0 / math.sqrt(96)
SWA_SCALE = 1.0 / math.sqrt(64)
LR, B1, B2, ADAM_EPS, WD, STEP = 0.001, 0.9, 0.999, 1e-08, 0.01, 10

LANES = 1024
VMEM_LIMIT = 56 * 1024 * 1024
NEG_INF = float("-inf")

C_CQ, C_CKV, C_KR, C_GB, C_GC, C_UC, C_UP, C_QS, C_KS, C_VS = 0, 256, 384, 512, 768, 1024, 1280, 1536, 1792, 1920


def _params(ngrid):
    return pltpu.CompilerParams(dimension_semantics=("arbitrary",) * ngrid, vmem_limit_bytes=VMEM_LIMIT)


def _pc(body, *, name, grid, in_specs, out_specs, out_shape, scratch=(), aliases=None):
    return pl.pallas_call(
        body, name=name, grid=grid, in_specs=in_specs, out_specs=out_specs, out_shape=out_shape,
        scratch_shapes=scratch, input_output_aliases=aliases or {}, compiler_params=_params(len(grid)))


def _dot(a, b):
    return jnp.dot(a, b, preferred_element_type=F32)


def _dot_nt(a, b):
    return lax.dot_general(a, b, (((1,), (1,)), ((), ())), preferred_element_type=F32)


def _dot_tn(a, b):
    return lax.dot_general(a, b, (((0,), (0,)), ((), ())), preferred_element_type=F32)


def _tile(n, cap):
    if n <= cap:
        return n
    t = cap - cap % 128
    while n % t:
        t -= 128
    return t


def _row_tile(a, b, cap=262144):
    bp = -(-b // 128) * 128
    best = None
    for t in range(8, a + 1, 8):
        if a % t == 0 and t * bp <= cap:
            best = t
    if best is None or (best < 64 and a * bp <= 2 * cap):
        return a
    return best


def _g3(a):
    return a.reshape(a.shape[0], 1, a.shape[1])


def _norm_mm(x, g3, l, w, wspec, N, tn, out_dtype, name, w_t=False):
    S, K = x.shape
    tm = min(1024 if out_dtype == BF16 else 512, S)

    def body(x_ref, g_ref, w_ref, y_ref, h_ref):
        @pl.when(pl.program_id(1) == 0)
        def _():
            xv = x_ref[...]
            r = lax.rsqrt(jnp.mean(xv * xv, axis=-1, keepdims=True) + EPS)
            h_ref[...] = (xv * r * g_ref[...]).astype(BF16)

        y_ref[...] = (_dot_nt if w_t else _dot)(h_ref[...], w_ref[...]).astype(out_dtype)

    return _pc(body, name=name, grid=(S // tm, N // tn),
               in_specs=[BS((tm, K), lambda i, j: (i, 0)), BS((None, 1, K), lambda i, j: (l, 0, 0)), wspec],
               out_specs=[BS((tm, tn), lambda i, j: (i, j)), BS((tm, K), lambda i, j: (i, 0))],
               out_shape=[SDS((S, N), out_dtype), SDS((S, K), BF16)])(x, g3, w)


def _wspec_in(l):
    return BS((None, D_INP, D), lambda i, j: (l, j, 0))


def _wspec_gu(l):
    return BS((None, None, D, 2 * D_FF // 4), lambda i, j: (l, j, 0, 0))


def _mix_out(x0, ya, yb, yc, yd, gmix3, wo, l, name):
    S = x0.shape[0]
    tm = min(512, S)

    def body(x_ref, ya_ref, yb_ref, yc_ref, yd_ref, g_ref, w_ref, x1_ref, ycat_ref, mixed_ref):
        groups = [ya_ref[...], yb_ref[...], yc_ref[...], yd_ref[...]]
        for gi, yg in enumerate(groups):
            sl = slice(gi * 256, (gi + 1) * 256)
            r = lax.rsqrt(jnp.mean(yg * yg, axis=-1, keepdims=True) + EPS)
            ycat_ref[:, sl] = yg
            mixed_ref[:, sl] = (yg * r * g_ref[:, sl]).astype(BF16)
        x1_ref[...] = x_ref[...] + _dot(mixed_ref[...], w_ref[...])

    row = lambda w: BS((tm, w), lambda i: (i, 0))
    return _pc(body, name=name, grid=(S // tm,),
               in_specs=[row(D), row(256), row(256), row(256), row(256), BS((None, 1, D), lambda i: (l, 0, 0)),
                         BS((None, D, D), lambda i: (l, 0, 0))],
               out_specs=[row(D), row(D), row(D)],
               out_shape=[SDS((S, D), F32), SDS((S, D), F32), SDS((S, D), BF16)])(x0, ya, yb, yc, yd, gmix3, wo)


def _swiglu_mm_res(x1, gu, wdown, l, name):
    S = x1.shape[0]
    tm = min(256, S)

    def body(x_ref, gate_ref, up_ref, w_ref, x2_ref, act_ref):
        acc = x_ref[...]
        for c0 in range(0, D_FF, D_FF // 2):
            cs = slice(c0, c0 + D_FF // 2)
            gt = gate_ref[:, cs].astype(F32)
            act = (gt * pl.reciprocal(1.0 + jnp.exp(-gt), approx=True) * up_ref[:, cs].astype(F32)).astype(BF16)
            act_ref[:, cs] = act
            acc = acc + _dot(act, w_ref[cs, :])
        x2_ref[...] = acc

    return _pc(body, name=name, grid=(S // tm,),
               in_specs=[BS((tm, D), lambda i: (i, 0)), BS((tm, D_FF), lambda i: (i, 0)),
                         BS((tm, D_FF), lambda i: (i, 1)), BS((None, D_FF, D), lambda i: (l, 0, 0))],
               out_specs=[BS((tm, D), lambda i: (i, 0)), BS((tm, D_FF), lambda i: (i, 0))],
               out_shape=[SDS((S, D), F32), SDS((S, D_FF), BF16)])(x1, gu, gu, wdown)


def _loss_head(x, g, tgt, name):
    S = x.shape[0]
    tm = min(512, S)

    def body(x_ref, g_ref, t_ref, dx_ref, dx16_ref, dg_ref, loss_ref):
        @pl.when(pl.program_id(0) == 0)
        def _():
            dg_ref[...] = jnp.zeros_like(dg_ref)
            loss_ref[...] = jnp.zeros_like(loss_ref)

        xv = x_ref[...]
        r = lax.rsqrt(jnp.mean(xv * xv, axis=-1, keepdims=True) + EPS)
        xh = xv * r
        gv = g_ref[...]
        diff = xh * gv - t_ref[...]
        loss_ref[...] += jnp.sum(diff * diff)
        dy = diff * (1.0 / D)
        dg_ref[...] += jnp.sum(dy * xh, axis=0, keepdims=True)
        dxh = dy * gv
        dx = r * (dxh - xh * jnp.mean(dxh * xh, axis=-1, keepdims=True))
        dx_ref[...] = dx
        dx16_ref[...] = dx.astype(BF16)

    row = BS((tm, D), lambda i: (i, 0))
    return _pc(body, name=name, grid=(S // tm,),
               in_specs=[row, BS((1, D), lambda i: (0, 0)), row],
               out_specs=[row, row, BS((8, D), lambda i: (0, 0)), BS((8, 128), lambda i: (0, 0))],
               out_shape=[SDS((S, D), F32), SDS((S, D), BF16), SDS((8, D), F32), SDS((8, 128), F32)])(x, g, tgt)


def _mm_tn(a, b, l, prev, name, split4=False):
    S, Ka = a.shape
    N = b.shape[1]
    if split4:
        ta, tn = _tile(Ka, 256), N // 4
        out_shape = SDS((2, 4, Ka, tn), F32)
        out_spec = BS((None, None, ta, tn), lambda j, i: (l, j, i, 0))
    else:
        ta, tn = _tile(Ka, 512), _tile(N, 1024)
        out_shape = SDS((2, Ka, N), F32)
        out_spec = BS((None, ta, tn), lambda j, i: (l, i, j))

    def body(a_ref, b_ref, *rest):
        rest[-1][...] = _dot_tn(a_ref[...], b_ref[...])

    in_specs = [BS((S, ta), lambda j, i: (0, i)), BS((S, tn), lambda j, i: (0, j))]
    args = [a, b]
    if prev is not None:
        in_specs.append(BS(memory_space=pl.ANY))
        args.append(prev)
    return _pc(body, name=name, grid=(N // tn, Ka // ta), in_specs=in_specs, out_specs=out_spec, out_shape=out_shape,
               aliases={2: 0} if prev is not None else None)(*args)


def _bwd_down(dx16, wdown, gu, l, name):
    S = dx16.shape[0]
    tm = min(256, S)

    def body(dx_ref, w_ref, gate_ref, up_ref, dgu_ref):
        dxv = dx_ref[...]
        for c0 in range(0, D_FF, 256):
            cs = slice(c0, c0 + 256)
            dact = _dot_nt(dxv, w_ref[cs, :])
            gt = gate_ref[:, cs].astype(F32)
            sg = pl.reciprocal(1.0 + jnp.exp(-gt), approx=True)
            dgu_ref[:, cs] = (dact * up_ref[:, cs].astype(F32) * (sg * (1.0 + gt * (1.0 - sg)))).astype(BF16)
            dgu_ref[:, D_FF + c0:D_FF + c0 + 256] = (dact * (gt * sg)).astype(BF16)

    return _pc(body, name=name, grid=(S // tm,),
               in_specs=[BS((tm, D), lambda i: (i, 0)), BS((None, D_FF, D), lambda i: (l, 0, 0)),
                         BS((tm, D_FF), lambda i: (i, 0)), BS((tm, D_FF), lambda i: (i, 1))],
               out_specs=BS((tm, 2 * D_FF), lambda i: (i, 0)),
               out_shape=SDS((S, 2 * D_FF), BF16))(dx16, wdown, gu, gu)


def _mm_nt_normbwd(dy, w4, l, x, g3, dres, ngroups, name, rider=None, w_t=False, after=None):
    S, K = dy.shape
    nk, kc = w4.shape[1], w4.shape[2 if w_t else 3]
    mm = _dot if w_t else _dot_nt
    tm = min(512, S)
    gw = D // ngroups
    has_res = dres is not None
    nr = rider.n if rider else 0
    n_in, n_out = 4 + has_res + (after is not None), 2 + has_res

    def body(*refs):
        dy_ref, w_ref, x_ref, g_ref = refs[:4]
        res_ref = refs[4] if has_res else None
        outs = refs[n_in + nr:n_in + nr + n_out]
        dx_ref, dg_ref = outs[0], outs[-1]
        dx16_ref = outs[1] if has_res else None
        r_io = (refs[n_in:n_in + nr], refs[n_in + nr + n_out:n_in + 2 * nr + n_out], refs[n_in + 2 * nr + n_out:])
        if rider:
            pl.when(pl.program_id(0) == 0)(lambda: rider.start(*r_io))

        @pl.when(pl.program_id(0) == 0)
        def _():
            dg_ref[...] = jnp.zeros_like(dg_ref)

        dh = mm(dy_ref[:, 0:kc], w_ref[0])
        for k in range(1, nk):
            dh = dh + mm(dy_ref[:, k * kc:(k + 1) * kc], w_ref[k])
        for gi in range(ngroups):
            sl = slice(gi * gw, (gi + 1) * gw)
            xg = x_ref[:, sl]
            r = lax.rsqrt(jnp.mean(xg * xg, axis=-1, keepdims=True) + EPS)
            xh = xg * r
            dhg = dh[:, sl]
            dg_ref[:, sl] += jnp.sum(dhg * xh, axis=0, keepdims=True)
            dxh = dhg * g_ref[:, sl]
            dxg = r * (dxh - xh * jnp.mean(dxh * xh, axis=-1, keepdims=True))
            if has_res:
                dxg = dxg + res_ref[:, sl]
                dx16_ref[:, sl] = dxg.astype(BF16)
            dx_ref[:, sl] = dxg
        if rider:
            pl.when(pl.program_id(0) == S // tm - 1)(lambda: rider.wait(*r_io))

    row = BS((tm, D), lambda i: (i, 0))
    in_specs = [BS((tm, K), lambda i: (i, 0)),
                BS((None,) + tuple(w4.shape[1:]), lambda i: (l, 0, 0, 0), pipeline_mode=pl.Buffered(1)), row,
                BS((None, 1, D), lambda i: (l, 0, 0))]
    args = [dy, w4, x, g3]
    out_specs, out_shape = [row], [SDS((S, D), F32)]
    if has_res:
        in_specs.append(row)
        args.append(dres)
        out_specs.append(row)
        out_shape.append(SDS((S, D), BF16))
    if after is not None:
        in_specs.append(BS((8, 128), lambda i: (0, 0)))
        args.append(after)
    out_specs.append(BS((8, D), lambda i: (0, 0)))
    out_shape.append(SDS((8, D), F32))
    if not rider:
        return _pc(body, name=name, grid=(S // tm,), in_specs=in_specs, out_specs=out_specs, out_shape=out_shape)(*args)
    out = pl.pallas_call(body, name=name, grid=(S // tm,), in_specs=in_specs + [ANY] * nr,
                         out_specs=out_specs + [ANY] * nr, out_shape=out_shape + rider.out_shape,
                         scratch_shapes=rider.scratch(),
                         compiler_params=pltpu.CompilerParams(dimension_semantics=("arbitrary",),
                                                              vmem_limit_bytes=VMEM_LIMIT, has_side_effects=True))(
        *args, *rider.arrs)
    return (*out[:n_out], list(out[n_out:]))


def _rope(x, c, s1, s2):
    return x * c + pltpu.roll(x, 112, axis=1) * s1 + pltpu.roll(x, 16, axis=1) * s2


def _rope_t(dy, c, s1, s2):
    return dy * c + pltpu.roll(dy * s1, 16, axis=1) + pltpu.roll(dy * s2, 112, axis=1)


def _mla_prep(proj, gq3, gkv3, wuq, wk, wv, tabs, l, name):
    S = proj.shape[0]
    tm = min(512, S)
    tc, ts1, ts2 = tabs

    def body(cq_ref, ckv_ref, kr_ref, gq_ref, gkv_ref, wuq_ref, wk_ref, wv_ref, c_ref, s1_ref, s2_ref,
             q_ref, k_ref, v_ref, kt_ref, vt_ref):
        c, s1, s2 = c_ref[...], s1_ref[...], s2_ref[...]
        cq = cq_ref[...]
        rq = lax.rsqrt(jnp.mean(cq * cq, axis=-1, keepdims=True) + EPS)
        qa = _dot((cq * rq * gq_ref[...]).astype(BF16), wuq_ref[...])
        ckv = ckv_ref[...]
        rkv = lax.rsqrt(jnp.mean(ckv * ckv, axis=-1, keepdims=True) + EPS)
        ckvn = (ckv * rkv * gkv_ref[...]).astype(BF16)
        ka = _dot(ckvn, wk_ref[...])
        va = _dot(ckvn, wv_ref[...])
        v_ref[...] = va.astype(BF16)
        vt_ref[...] = va.T.astype(BF16)
        krr = _rope(kr_ref[...], c, s1, s2)
        for h in range(HEADS):
            sl = slice(h * 128, (h + 1) * 128)
            q_ref[:, sl] = (_rope(qa[:, sl], c, s1, s2) * QK_SCALE).astype(BF16)
            kh = ka[:, sl] + krr
            k_ref[:, sl] = kh.astype(BF16)
            kt_ref[sl, :] = kh.T.astype(BF16)

    lay = lambda a, b: BS((None, a, b), lambda i: (l, 0, 0))
    tab = BS((tm, 128), lambda i: (i, 0))
    return _pc(body, name=name, grid=(S // tm,),
               in_specs=[BS((tm, 256), lambda i: (i, 0)), BS((tm, 128), lambda i: (i, 2)), BS((tm, 128), lambda i: (i, 3)),
                         lay(1, 256), lay(1, 128), lay(256, 512), lay(128, 512), lay(128, 512), tab, tab, tab],
               out_specs=[BS((tm, 512), lambda i: (i, 0))] * 3 + [BS((512, tm), lambda i: (0, i))] * 2,
               out_shape=[SDS((S, 512), BF16)] * 3 + [SDS((512, S), BF16)] * 2)(
        proj, proj, proj, gq3, gkv3, wuq, wk, wv, tc, ts1, ts2)


def _mla_prep_bwd(dq, dk, dv, proj, gq3, gkv3, wuq, wk, wv, tabs, l, name):
    S = proj.shape[0]
    tm = min(512, S)
    tc, ts1, ts2 = tabs

    def body(dq_ref, dk_ref, dv_ref, cq_ref, ckv_ref, gq_ref, gkv_ref, wuq_ref, wk_ref, wv_ref, c_ref, s1_ref, s2_ref,
             dcq_ref, dckv_ref, dkr_ref, dwuq_ref, dwk_ref, dwv_ref, dgq_ref, dgkv_ref):
        @pl.when(pl.program_id(0) == 0)
        def _():
            for r in (dwuq_ref, dwk_ref, dwv_ref, dgq_ref, dgkv_ref):
                r[...] = jnp.zeros_like(r)

        c, s1, s2 = c_ref[...], s1_ref[...], s2_ref[...]
        dqp = jnp.concatenate(
            [_rope_t(dq_ref[h * 128:(h + 1) * 128, :].T * QK_SCALE, c, s1, s2) for h in range(HEADS)], axis=1).astype(BF16)
        cq = cq_ref[...]
        rq = lax.rsqrt(jnp.mean(cq * cq, axis=-1, keepdims=True) + EPS)
        cqh = cq * rq
        gq_v = gq_ref[...]
        dwuq_ref[...] += _dot_tn((cqh * gq_v).astype(BF16), dqp)
        dcqn = _dot_nt(dqp, wuq_ref[...])
        dgq_ref[...] += jnp.sum(dcqn * cqh, axis=0, keepdims=True)
        dxh = dcqn * gq_v
        dcq_ref[...] = (rq * (dxh - cqh * jnp.mean(dxh * cqh, axis=-1, keepdims=True))).astype(BF16)

        dkb = dk_ref[...].astype(BF16)
        dvb = dv_ref[...].astype(BF16)
        ckv = ckv_ref[...]
        rkv = lax.rsqrt(jnp.mean(ckv * ckv, axis=-1, keepdims=True) + EPS)
        ckh = ckv * rkv
        gkv_v = gkv_ref[...]
        ckvn = (ckh * gkv_v).astype(BF16)
        dwk_ref[...] += _dot_tn(ckvn, dkb)
        dwv_ref[...] += _dot_tn(ckvn, dvb)
        dckvn = _dot_nt(dkb, wk_ref[...]) + _dot_nt(dvb, wv_ref[...])
        dgkv_ref[...] += jnp.sum(dckvn * ckh, axis=0, keepdims=True)
        dyh = dckvn * gkv_v
        dckv_ref[...] = (rkv * (dyh - ckh * jnp.mean(dyh * ckh, axis=-1, keepdims=True))).astype(BF16)
        dks = dk_ref[:, 0:128] + dk_ref[:, 128:256] + dk_ref[:, 256:384] + dk_ref[:, 384:512]
        dkr_ref[...] = _rope_t(dks, c, s1, s2).astype(BF16)

    full = lambda a, b: BS((a, b), lambda i: (0, 0))
    lay = lambda a, b: BS((None, a, b), lambda i: (l, 0, 0))
    tab = BS((tm, 128), lambda i: (i, 0))
    row = lambda w: BS((tm, w), lambda i: (i, 0))
    return _pc(body, name=name, grid=(S // tm,),
               in_specs=[BS((512, tm), lambda i: (0, i)), row(512), row(512), BS((tm, 256), lambda i: (i, 0)),
                         BS((tm, 128), lambda i: (i, 2)),
                         lay(1, 256), lay(1, 128), lay(256, 512), lay(128, 512), lay(128, 512), tab, tab, tab],
               out_specs=[row(256), row(128), row(128), full(256, 512), full(128, 512), full(128, 512),
                          full(8, 256), full(8, 128)],
               out_shape=[SDS((S, 256), BF16), SDS((S, 128), BF16), SDS((S, 128), BF16), SDS((256, 512), F32),
                          SDS((128, 512), F32), SDS((128, 512), F32), SDS((8, 256), F32), SDS((8, 128), F32)])(
        dq, dk, dv, proj, proj, gq3, gkv3, wuq, wk, wv, tc, ts1, ts2)


def _causal_steps(n, q_outer):
    if q_outer:
        pairs = [(i, j) for i in range(n) for j in range(i + 1)]
    else:
        pairs = [(i, j) for j in range(n) for i in range(j, n)]
    return jnp.asarray([p[0] for p in pairs], jnp.int32), jnp.asarray([p[1] for p in pairs], jnp.int32)


def _mla_attn(q, k, vt, gts, layer, name):
    S = q.shape[0]
    t = min(512, S)
    n = S // t
    ng = len(gts)

    qi, kj = _causal_steps(n, True)
    last = qi.shape[0] - 1

    def body(qi_ref, kj_ref, q_ref, k_ref, vt_ref, *rest):
        (ya_ref, lse_ref), g_refs = rest[ng:ng + 2], rest[ng + 2:2 * ng + 2]
        m_sc, l_sc, acc_sc = rest[2 * ng + 2:2 * ng + 5]
        i, j = qi_ref[pl.program_id(1)], kj_ref[pl.program_id(1)]
        if ng:
            phases = _gather_phases(g_refs, [g.shape for g in gts], rest[2 * ng + 5], rest[2 * ng + 6], layer)
            for ph, (pp, ss) in zip(phases[:3], ((0, 0), (1, 0), (1, 2 * last // 3))):
                pl.when((pl.program_id(0) == pp) & (pl.program_id(1) == ss))(ph)

        @pl.when(j == 0)
        def _():
            m_sc[...] = jnp.full_like(m_sc, NEG_INF)
            l_sc[...] = jnp.zeros_like(l_sc)
            acc_sc[...] = jnp.zeros_like(acc_sc)

        def step(masked):
            for hh in range(2):
                sl = slice(hh * 128, (hh + 1) * 128)
                st = _dot_nt(k_ref[:, sl], q_ref[:, sl])
                if masked:
                    key = lax.broadcasted_iota(jnp.int32, (t, t), 0)
                    qry = lax.broadcasted_iota(jnp.int32, (t, t), 1)
                    st = jnp.where(key <= qry, st, NEG_INF)
                m_prev = m_sc[hh]
                m_new = jnp.maximum(m_prev, jnp.max(st, axis=0, keepdims=True))
                p = jnp.exp(st - m_new)
                alpha = jnp.exp(m_prev - m_new)
                l_sc[hh] = alpha * l_sc[hh] + jnp.sum(p, axis=0, keepdims=True)
                acc_sc[hh] = alpha * acc_sc[hh] + _dot(vt_ref[sl, :], p.astype(BF16))
                m_sc[hh] = m_new

        @pl.when(j < i)
        def _():
            step(False)

        @pl.when(j == i)
        def _():
            step(True)
            ya_ref[...] = (acc_sc[0] / l_sc[0] + acc_sc[1] / l_sc[1]).T
            for hh in range(2):
                lse_ref[hh] = m_sc[hh] + jnp.log(l_sc[hh])

        if ng:
            pl.when((pl.program_id(0) == 1) & (pl.program_id(1) == last))(phases[3])

    gs = pltpu.PrefetchScalarGridSpec(
        num_scalar_prefetch=2, grid=(2, qi.shape[0]),
        in_specs=[BS((t, 256), lambda p, s, qi, kj: (qi[s], p)), BS((t, 256), lambda p, s, qi, kj: (kj[s], p)),
                  BS((256, t), lambda p, s, qi, kj: (p, kj[s]))] + [ANY] * ng,
        out_specs=[BS((t, 128), lambda p, s, qi, kj: (qi[s], p)), BS((2, 1, t), lambda p, s, qi, kj: (p, 0, qi[s]))]
        + [ANY] * ng,
        scratch_shapes=[pltpu.VMEM((2, 1, t), F32), pltpu.VMEM((2, 1, t), F32), pltpu.VMEM((2, 128, t), F32)]
        + ([pltpu.SemaphoreType.DMA((7 * ng,)), pltpu.SemaphoreType.DMA((7 * ng,))] if ng else []))
    out = pl.pallas_call(body, name=name, grid_spec=gs,
                         out_shape=[SDS((S, 256), F32), SDS((HEADS, 1, S), F32)] + [SDS(g.shape, g.dtype) for g in gts],
                         input_output_aliases={5 + m: 2 + m for m in range(ng)},
                         compiler_params=pltpu.CompilerParams(dimension_semantics=("arbitrary",) * 2,
                                                              vmem_limit_bytes=VMEM_LIMIT, has_side_effects=bool(ng)))(
        qi, kj, q, k, vt, *gts)
    return out[0], out[1], list(out[2:])


def _mla_delta(dycat, ya, name, after=None):
    S = ya.shape[0]
    t = min(512, S)

    def body(do_ref, ya_ref, *refs):
        d_ref = refs[-1]
        prod = do_ref[...] * ya_ref[...]
        for p in range(2):
            pt = prod[:, p * 128:(p + 1) * 128].T
            d_ref[2 * p] = jnp.sum(pt[0:64, :], axis=0, keepdims=True)
            d_ref[2 * p + 1] = jnp.sum(pt[64:128, :], axis=0, keepdims=True)

    extra = [] if after is None else [after]
    return _pc(body, name=name, grid=(S // t,),
               in_specs=[BS((t, 256), lambda i: (i, 0)), BS((t, 256), lambda i: (i, 0))]
               + [BS((8, 128), lambda i: (0, 0)) for _ in extra],
               out_specs=BS((HEADS, 1, t), lambda i: (0, 0, i)), out_shape=SDS((HEADS, 1, S), F32))(dycat, ya, *extra)


def _mla_attn_bwd(q, k, kt, v, dya, lse, delta, rider, name):
    S = q.shape[0]
    t = min(512, S)
    n = S // t
    nr = rider.n if rider else 0

    qi, kj = _causal_steps(n, False)
    last = qi.shape[0] - 1

    def body(qi_ref, kj_ref, q_ref, k_ref, kt_ref, v_ref, do_ref, lse_ref, delta_ref, *rest):
        dqt_ref, dk_ref, dv_ref = rest[nr:nr + 3]
        r_io = (rest[:nr], rest[nr + 3:2 * nr + 3], rest[2 * nr + 3:])
        i, j = qi_ref[pl.program_id(1)], kj_ref[pl.program_id(1)]
        if rider:
            pl.when((pl.program_id(0) == 0) & (pl.program_id(1) == 0))(lambda: rider.start(*r_io))

        @pl.when(pl.program_id(1) == 0)
        def _():
            dqt_ref[...] = jnp.zeros_like(dqt_ref)

        @pl.when(i == j)
        def _():
            dk_ref[...] = jnp.zeros_like(dk_ref)
            dv_ref[...] = jnp.zeros_like(dv_ref)

        def step(masked):
            dob = do_ref[...].astype(BF16)
            cols = pl.ds(pl.multiple_of(i * t, t), t)
            for hh in range(2):
                sl = slice(hh * 128, (hh + 1) * 128)
                qv = q_ref[:, sl]
                p = jnp.exp(_dot_nt(k_ref[:, sl], qv) - lse_ref[hh])
                if masked:
                    key = lax.broadcasted_iota(jnp.int32, (t, t), 0)
                    qry = lax.broadcasted_iota(jnp.int32, (t, t), 1)
                    p = jnp.where(key <= qry, p, 0.0)
                dv_ref[:, sl] += _dot(p.astype(BF16), dob)
                ds = (p * (_dot_nt(v_ref[:, sl], dob) - delta_ref[hh])).astype(BF16)
                dk_ref[:, sl] += _dot(ds, qv)
                dqt_ref[sl, cols] += _dot(kt_ref[sl, :], ds)

        @pl.when(i > j)
        def _():
            step(False)

        @pl.when(i == j)
        def _():
            step(True)

        if rider:
            pl.when((pl.program_id(0) == 1) & (pl.program_id(1) == last))(lambda: rider.wait(*r_io))

    qs = BS((t, 256), lambda p, s, qi, kj: (qi[s], p))
    ks = BS((t, 256), lambda p, s, qi, kj: (kj[s], p))
    rowv = BS((2, 1, t), lambda p, s, qi, kj: (p, 0, qi[s]))
    gs = pltpu.PrefetchScalarGridSpec(
        num_scalar_prefetch=2, grid=(2, qi.shape[0]),
        in_specs=[qs, ks, BS((256, t), lambda p, s, qi, kj: (p, kj[s])), ks,
                  BS((t, 128), lambda p, s, qi, kj: (qi[s], p)), rowv, rowv] + [ANY] * nr,
        out_specs=[BS((256, S), lambda p, s, qi, kj: (p, 0)), ks, ks] + [ANY] * nr,
        scratch_shapes=rider.scratch() if rider else [])
    out = pl.pallas_call(body, name=name, grid_spec=gs,
                         out_shape=[SDS((512, S), F32), SDS((S, 512), F32), SDS((S, 512), F32)]
                         + (rider.out_shape if rider else []),
                         compiler_params=pltpu.CompilerParams(dimension_semantics=("arbitrary",) * 2,
                                                              vmem_limit_bytes=VMEM_LIMIT, has_side_effects=bool(rider)))(
        qi, kj, q, k, kt, v, dya, lse, delta, *(rider.arrs if rider else []))
    return out[0], out[1], out[2], list(out[3:])


def _swa_scores(qm, kk, valid, bias, sink):
    sc = jnp.where(valid, _dot_nt(qm, kk) * SWA_SCALE + bias, NEG_INF)
    m = jnp.maximum(jnp.max(sc, axis=-1, keepdims=True), sink)
    e = jnp.exp(sc - m)
    esink = jnp.exp(sink - m)
    den = jnp.sum(e, axis=-1, keepdims=True) + esink
    return e / den, esink / den


def _swa_consts(sink_ref, l):
    rows = HEADS * BLK
    r = lax.broadcasted_iota(jnp.int32, (rows, 2 * BLK), 0)
    c = lax.broadcasted_iota(jnp.int32, (rows, 2 * BLK), 1)
    dist = (r & (BLK - 1)) + BLK - c
    head = lax.broadcasted_iota(jnp.int32, (rows, 1), 0) // BLK

    def per_head(vals):
        return jnp.where(head == 0, vals[0], jnp.where(head == 1, vals[1], jnp.where(head == 2, vals[2], vals[3])))

    bias = -per_head(SLOPES) * dist.astype(F32)
    sink = per_head([sink_ref[l, h] for h in range(HEADS)])
    return (dist >= 0) & (dist < SWA_WINDOW), c >= BLK, bias, sink, head


def _to_half(xb, pos, b):
    return xb if pos == b else pltpu.roll(xb, 64, axis=1)


def _swa_stack(ref, st, lo):
    parts = []
    for b in range(2):
        xb = ref[pl.ds(st, BLK), b * 128:(b + 1) * 128]
        half = lo if b == 0 else ~lo
        parts += [jnp.where(half, _to_half(xb, pos, b), 0.0).astype(BF16) for pos in range(2)]
    return jnp.concatenate(parts, axis=0)


def _swa_unstack(x_all, lo):
    blocks = []
    for b in range(2):
        h0, h1 = (_to_half(x_all[(2 * b + pos) * BLK:(2 * b + pos + 1) * BLK], pos, b) for pos in range(2))
        blocks.append(jnp.where(lo, h0, h1))
    return blocks


def _swa(proj, sinks, l, name):
    S = proj.shape[0]
    nb = S // BLK

    def body(q_ref, k_ref, v_ref, sink_ref, o_ref, kp, vp):
        kp[0:BLK, :] = jnp.zeros((BLK, 128), BF16)
        vp[0:BLK, :] = jnp.zeros((BLK, 128), BF16)
        kp[BLK:, :] = k_ref[...].astype(BF16)
        vp[BLK:, :] = v_ref[...].astype(BF16)
        lo = lax.broadcasted_iota(jnp.int32, (BLK, 128), 1) < 64
        band, cur, bias, sink, _ = _swa_consts(sink_ref, l)

        def blk(i, carry):
            st = pl.multiple_of(i * BLK, BLK)
            kk = kp[pl.ds(st, 2 * BLK), :]
            vv = vp[pl.ds(st, 2 * BLK), :]
            p, _ = _swa_scores(_swa_stack(q_ref, st, lo), kk, band & (cur | (i > 0)), bias, sink)
            for b, ob in enumerate(_swa_unstack(_dot(p.astype(BF16), vv), lo)):
                o_ref[pl.ds(st, BLK), b * 128:(b + 1) * 128] = ob
            return carry

        lax.fori_loop(0, nb, blk, 0, unroll=2)

    return _pc(body, name=name, grid=(1,),
               in_specs=[BS((S, 256), lambda i: (0, C_QS // 256)), BS((S, 128), lambda i: (0, C_KS // 128)),
                         BS((S, 128), lambda i: (0, C_VS // 128)), BS(memory_space=pltpu.SMEM)],
               out_specs=BS((S, 256), lambda i: (0, 0)),
               out_shape=SDS((S, 256), F32),
               scratch=[pltpu.VMEM((S + BLK, 128), BF16), pltpu.VMEM((S + BLK, 128), BF16)])(proj, proj, proj, sinks)


def _swa_bwd(proj, sinks, dyd, l, name, rider=None):
    S = proj.shape[0]
    nb = S // BLK
    nr = rider.n if rider else 0

    def body(q_ref, k_ref, v_ref, sink_ref, do_ref, *rest):
        dq_ref, dk_ref, dv_ref, dsink_ref = rest[nr:nr + 4]
        kp, vp, dkp, dvp = rest[2 * nr + 4:2 * nr + 8]
        r_io = (rest[:nr], rest[nr + 4:2 * nr + 4], rest[2 * nr + 8:])
        if rider:
            rider.start(*r_io)
        kp[0:BLK, :] = jnp.zeros((BLK, 128), BF16)
        vp[0:BLK, :] = jnp.zeros((BLK, 128), BF16)
        kp[BLK:, :] = k_ref[...].astype(BF16)
        vp[BLK:, :] = v_ref[...].astype(BF16)
        dkp[...] = jnp.zeros_like(dkp)
        dvp[...] = jnp.zeros_like(dvp)
        lo = lax.broadcasted_iota(jnp.int32, (BLK, 128), 1) < 64
        lane8 = lax.broadcasted_iota(jnp.int32, (8, 128), 1)
        band, cur, bias, sink, head = _swa_consts(sink_ref, l)

        def blk(i, dsink):
            st = pl.multiple_of(i * BLK, BLK)
            kk = kp[pl.ds(st, 2 * BLK), :]
            vv = vp[pl.ds(st, 2 * BLK), :]
            qm, dom = _swa_stack(q_ref, st, lo), _swa_stack(do_ref, st, lo)
            p, psink = _swa_scores(qm, kk, band & (cur | (i > 0)), bias, sink)
            dp = _dot_nt(dom, vv)
            dvp[pl.ds(st, 2 * BLK), :] += _dot_tn(p.astype(BF16), dom)
            delta = jnp.sum(p * dp, axis=-1, keepdims=True)
            dsk = -psink * delta
            for h in range(HEADS):
                dsink = dsink + jnp.where(lane8 == h, jnp.sum(jnp.where(head == h, dsk, 0.0)), 0.0)
            dsc = (p * (dp - delta) * SWA_SCALE).astype(BF16)
            for b, dqb in enumerate(_swa_unstack(_dot(dsc, kk), lo)):
                dq_ref[pl.ds(st, BLK), b * 128:(b + 1) * 128] = dqb.astype(BF16)
            dkp[pl.ds(st, 2 * BLK), :] += _dot_tn(dsc, qm)
            return dsink

        dsink_ref[...] = lax.fori_loop(0, nb, blk, jnp.zeros((8, 128), F32), unroll=2)
        dk_ref[...] = dkp[BLK:, :].astype(BF16)
        dv_ref[...] = dvp[BLK:, :].astype(BF16)
        if rider:
            rider.wait(*r_io)

    in_specs = [BS((S, 256), lambda i: (0, C_QS // 256)), BS((S, 128), lambda i: (0, C_KS // 128)),
                BS((S, 128), lambda i: (0, C_VS // 128)), BS(memory_space=pltpu.SMEM), BS((S, 256), lambda i: (0, 3))]
    out_specs = [BS((S, 256), lambda i: (0, 0)), BS((S, 128), lambda i: (0, 0)), BS((S, 128), lambda i: (0, 0)),
                 BS((8, 128), lambda i: (0, 0))]
    out_shape = [SDS((S, 256), BF16), SDS((S, 128), BF16), SDS((S, 128), BF16), SDS((8, 128), F32)]
    scratch = [pltpu.VMEM((S + BLK, 128), BF16), pltpu.VMEM((S + BLK, 128), BF16),
               pltpu.VMEM((S + BLK, 128), F32), pltpu.VMEM((S + BLK, 128), F32)]
    if not rider:
        return _pc(body, name=name, grid=(1,), in_specs=in_specs, out_specs=out_specs, out_shape=out_shape,
                   scratch=scratch)(proj, proj, proj, sinks, dyd)
    out = pl.pallas_call(body, name=name, grid=(1,), in_specs=in_specs + [ANY] * nr, out_specs=out_specs + [ANY] * nr,
                         out_shape=out_shape + rider.out_shape, scratch_shapes=scratch + rider.scratch(),
                         compiler_params=pltpu.CompilerParams(dimension_semantics=("arbitrary",),
                                                              vmem_limit_bytes=VMEM_LIMIT, has_side_effects=True))(
        proj, proj, proj, sinks, dyd, *rider.arrs)
    return (*out[:4], list(out[4:]))


def _down(x, k, t):
    return jnp.where(t >= k, pltpu.roll(x, k, axis=0), 0.0)


def _up(x, k, t):
    n = x.shape[0]
    return jnp.where(t < n - k, pltpu.roll(x, n - k, axis=0), 0.0)


def _conv(proj, w8, l, name):
    S = proj.shape[0]

    def body(gb_ref, gc_ref, u_ref, w_ref, y_ref):
        t = lax.broadcasted_iota(jnp.int32, (S, 128), 0)
        z = gc_ref[...] * u_ref[...]
        c = w_ref[2:3, :] * z + w_ref[1:2, :] * _down(z, 1, t) + w_ref[0:1, :] * _down(z, 2, t)
        y_ref[...] = gb_ref[...] * c

    col = lambda c0: BS((S, 128), lambda i: (0, c0 // 128 + i))
    return _pc(body, name=name, grid=(2,),
               in_specs=[col(C_GB), col(C_GC), col(C_UC), BS((None, 8, 128), lambda i: (l, 0, i))],
               out_specs=BS((S, 128), lambda i: (0, i)), out_shape=SDS((S, 256), F32))(proj, proj, proj, w8)


def _conv_bwd(proj, w8, dycat, l, name):
    S = proj.shape[0]

    def body(gb_ref, gc_ref, u_ref, w_ref, dy_ref, dgb_ref, dgc_ref, du_ref, dw_ref):
        t = lax.broadcasted_iota(jnp.int32, (S, 128), 0)
        gc, u = gc_ref[...], u_ref[...]
        z = gc * u
        z1, z2 = _down(z, 1, t), _down(z, 2, t)
        w0, w1, w2 = w_ref[0:1, :], w_ref[1:2, :], w_ref[2:3, :]
        dy = dy_ref[...]
        dgb_ref[...] = (dy * (w2 * z + w1 * z1 + w0 * z2)).astype(BF16)
        dc = dy * gb_ref[...]
        dz = w2 * dc + w1 * _up(dc, 1, t) + w0 * _up(dc, 2, t)
        dgc_ref[...] = (dz * u).astype(BF16)
        du_ref[...] = (dz * gc).astype(BF16)
        row = lax.broadcasted_iota(jnp.int32, (8, 128), 0)
        sums = [jnp.sum(dc * zz, axis=0, keepdims=True) for zz in (z2, z1, z)]
        dw_ref[...] = jnp.where(row == 0, sums[0], jnp.where(row == 1, sums[1], jnp.where(row == 2, sums[2], 0.0)))

    col = lambda c0: BS((S, 128), lambda i: (0, c0 // 128 + i))
    out = BS((S, 128), lambda i: (0, i))
    return _pc(body, name=name, grid=(2,),
               in_specs=[col(C_GB), col(C_GC), col(C_UC), BS((None, 8, 128), lambda i: (l, 0, i)), col(256)],
               out_specs=[out, out, out, BS((8, 128), lambda i: (0, i))],
               out_shape=[SDS((S, 256), BF16)] * 3 + [SDS((8, 256), F32)])(proj, proj, proj, w8, dycat)


def _pool_parts(u, t, first):
    lo = lax.broadcasted_iota(jnp.int32, u.shape, 1) < 64
    s2 = u + _down(u, 1, t)
    s4 = s2 + _down(s2, 2, t)
    s8 = s4 + _down(s4, 4, t)
    s16 = s8 + _down(s8, 8, t)
    win = jnp.where(lo, jnp.where(first, s2, s8), jnp.where(first, s4, s16))
    wv = jnp.where(lo, jnp.where(first, 2, 8), jnp.where(first, 4, 16))
    cnt = jnp.minimum(t + 1, wv).astype(F32)
    return win, cnt, lo


def _pool(proj, pwd, scale3, l, name):
    S = proj.shape[0]

    def body(u_ref, pw_ref, sc_ref, y_ref):
        t = lax.broadcasted_iota(jnp.int32, (S, 128), 0)
        u = u_ref[...]
        win, cnt, _ = _pool_parts(u, t, pl.program_id(0) == 0)
        pooled = win / cnt - u
        y_ref[...] = _dot(pooled.astype(BF16), pw_ref[...]) * sc_ref[...]

    return _pc(body, name=name, grid=(2,),
               in_specs=[BS((S, 128), lambda i: (0, C_UP // 128 + i)), BS((None, 128, 128), lambda i: (l, i, 0)),
                         BS((None, 1, 128), lambda i: (l, 0, i))],
               out_specs=BS((S, 128), lambda i: (0, i)), out_shape=SDS((S, 256), F32))(proj, pwd, scale3)


def _pool_bwd(proj, pwd, scale3, dycat, l, name):
    S = proj.shape[0]

    def body(u_ref, pw_ref, sc_ref, dy_ref, du_ref, dpw_ref, dsc_ref):
        t = lax.broadcasted_iota(jnp.int32, (S, 128), 0)
        first = pl.program_id(0) == 0
        u = u_ref[...]
        win, cnt, lo = _pool_parts(u, t, first)
        pooled = (win / cnt - u).astype(BF16)
        pw = pw_ref[...]
        dy = dy_ref[...]
        dsc_ref[...] = jnp.broadcast_to(jnp.sum(dy * _dot(pooled, pw), axis=0, keepdims=True), (8, 128))
        dmb = (dy * sc_ref[...]).astype(BF16)
        dpw_ref[...] = _dot_tn(pooled, dmb)
        dpooled = _dot_nt(dmb, pw)
        a1 = dpooled / cnt
        a2 = a1 + _up(a1, 1, t)
        a4 = a2 + _up(a2, 2, t)
        a8 = a4 + _up(a4, 4, t)
        a16 = a8 + _up(a8, 8, t)
        dwin = jnp.where(lo, jnp.where(first, a2, a8), jnp.where(first, a4, a16))
        du_ref[...] = (dwin - dpooled).astype(BF16)

    return _pc(body, name=name, grid=(2,),
               in_specs=[BS((S, 128), lambda i: (0, C_UP // 128 + i)), BS((None, 128, 128), lambda i: (l, i, 0)),
                         BS((None, 1, 128), lambda i: (l, 0, i)), BS((S, 128), lambda i: (0, 4 + i))],
               out_specs=[BS((S, 128), lambda i: (0, i)), BS((128, 128), lambda i: (i, 0)), BS((8, 128), lambda i: (0, i))],
               out_shape=[SDS((S, 256), BF16), SDS((256, 128), F32), SDS((8, 256), F32)])(proj, pwd, scale3, dycat)


def _adamw(w, g, m, v, name, echo=False):
    n, a, b = w.shape
    tr = _row_tile(a, b)

    def body(w_ref, g_ref, m_ref, v_ref, d_ref, nm_ref, nv_ref, *g_out):
        gv = g_ref[...]
        if echo:
            g_out[0][...] = gv
        m_new = B1 * m_ref[...] + (1.0 - B1) * gv
        v_new = B2 * v_ref[...] + (1.0 - B2) * (gv * gv)
        m_hat = m_new / (1.0 - B1 ** STEP)
        v_hat = v_new / (1.0 - B2 ** STEP)
        d_ref[...] = -LR * (m_hat / (jnp.sqrt(v_hat) + ADAM_EPS) + WD * w_ref[...])
        nm_ref[...] = m_new
        nv_ref[...] = v_new

    sp = BS((None, tr, b), lambda i, t: (i, t, 0))
    return _pc(body, name=name, grid=(n, a // tr), in_specs=[sp] * 4, out_specs=[sp] * (3 + echo),
               out_shape=[SDS((n, a, b), F32)] * (3 + echo))(w, g, m, v)


def _prefetch_call(body, name, grid, in_specs, out_specs, out_shape):
    gs = pltpu.PrefetchScalarGridSpec(num_scalar_prefetch=1, grid=grid, in_specs=in_specs, out_specs=out_specs)
    return pl.pallas_call(body, name=name, grid_spec=gs, out_shape=out_shape, compiler_params=_params(len(grid)))


def _place(w, kc, dtype, name):
    _, a, b = w.shape

    def body(kc_ref, w_ref, o_ref):
        o_ref[...] = w_ref[...].astype(dtype)

    return _prefetch_call(body, name, (2,), [BS((None, a, b), lambda l, kc: (l, 0, 0))],
                          BS((None, None, a, b), lambda l, kc: (l, kc[0], 0, 0)), SDS((2, 4, a, b), dtype))(kc, w)


def _pair_sum(g, got, kc, name):
    _, _, a, b = g.shape
    tr = _row_tile(a, b)

    def body(kc_ref, a_ref, b_ref, t32_ref, t16_ref):
        s = a_ref[...] + b_ref[...]
        t16_ref[...] = s.astype(BF16)

        @pl.when(pl.program_id(1) == kc_ref[0])
        def _():
            t32_ref[...] = s

    sp = BS((None, tr, b), lambda t, k, kc: (k, t, 0))
    return _prefetch_call(body, name, (a // tr, 4),
                          [BS((None, None, tr, b), lambda t, k, kc: (kc[1], k, t, 0)), sp],
                          [BS((tr, b), lambda t, k, kc: (t, 0)), sp],
                          [SDS((a, b), F32), SDS((4, a, b), BF16)])(kc, g, got)


def _chip_sum(t32, gots, kc, name, after=None):
    a, b = t32.shape
    tr = _row_tile(a, b)
    ng = len(gots)

    def body(kc_ref, a_ref, *refs):
        acc = a_ref[...]
        for g_ref in refs[:ng]:
            for i in range(g_ref.shape[0]):
                acc = acc + g_ref[i].astype(F32)
        refs[-1][...] = acc

    extra = [] if after is None else [after]
    return _prefetch_call(body, name, (a // tr,),
                          [BS((tr, b), lambda t, kc: (t, 0))]
                          + [BS((g.shape[0], tr, b), lambda t, kc: (0, t, 0)) for g in gots]
                          + [BS((8, 128), lambda t, kc: (0, 0)) for _ in extra],
                          BS((None, tr, b), lambda t, kc: (kc[1], t, 0)), SDS((2, a, b), F32))(kc, t32, *gots, *extra)


def _me():
    return lax.axis_index("x"), lax.axis_index("y"), lax.axis_index("c")


def _other_chips(x, y):
    return [(1 - x, y), (x, 1 - y), (1 - x, 1 - y)]


ANY = BS(memory_space=pl.ANY)
COMM_PARAMS = pltpu.CompilerParams(has_side_effects=True)


def _gather(arrs, name):
    n = len(arrs)

    def body(*refs):
        for phase in _gather_phases(refs[n:2 * n], [a.shape for a in arrs], refs[2 * n], refs[2 * n + 1]):
            phase()

    return pl.pallas_call(body, name=name, out_shape=[SDS(a.shape, a.dtype) for a in arrs],
                          in_specs=[ANY] * n, out_specs=[ANY] * n, input_output_aliases={t: t for t in range(n)},
                          scratch_shapes=[pltpu.SemaphoreType.DMA((7 * n,)), pltpu.SemaphoreType.DMA((7 * n,))],
                          compiler_params=COMM_PARAMS)(*arrs)


def _gather_phases(outs, shapes, send_sems, recv_sems, layer=None):
    n = len(outs)
    cut = [s[2] // 2 // 16 * 16 for s in shapes]
    split = [r > 0 for r in cut]

    def plan():
        x, y, c = _me()
        return (c if layer is None else layer), (x, y), (x, y, c), (x, y, 1 - c), _other_chips(x, y)

    def role(moving, fn):
        if layer is None:
            fn()
        else:
            c = lax.axis_index("c")
            pl.when((c == layer) if moving else (c != layer))(fn)

    def blk(t, chip, layer, half=None):
        r = outs[t].at[layer, 2 * chip[0] + chip[1]]
        if half is None:
            return r
        return r.at[pl.ds(0, cut[t])] if half == 0 else r.at[pl.ds(cut[t], shapes[t][2] - cut[t])]

    def copy(t, k, ref, to):
        return pltpu.make_async_remote_copy(src_ref=ref, dst_ref=ref, send_sem=send_sems.at[7 * t + k],
                                            recv_sem=recv_sems.at[7 * t + k], device_id=to, device_id_type=MESH)

    def own_sends(t):
        c, chip, me, sib, (xn, yn, dg) = plan()
        cps = [copy(t, 0, blk(t, chip, c), (*xn, c)), copy(t, 1, blk(t, chip, c), (*yn, c))]
        return cps if split[t] else cps + [copy(t, 2, blk(t, chip, c), (*dg, c))]

    def relays(t):
        c, chip, me, sib, (xn, yn, dg) = plan()
        after_x = [copy(t, 4, blk(t, xn, c), sib)]
        after_y = [copy(t, 5, blk(t, yn, c), sib)]
        if split[t]:
            after_x.insert(0, copy(t, 2, blk(t, xn, c, 0), (*yn, c)))
            after_y.insert(0, copy(t, 3, blk(t, yn, c, 1), (*xn, c)))
        return after_x, after_y, [copy(t, 6, blk(t, dg, c), sib)]

    def send_own():
        for t in range(n):
            for cp in own_sends(t):
                cp.start()

    def relay_neighbours():
        c, chip, me, sib, (xn, yn, dg) = plan()
        for t in range(n):
            after_x, after_y, _ = relays(t)
            copy(t, 0, blk(t, xn, c), me).wait_recv()
            for cp in after_x:
                cp.start()
            copy(t, 1, blk(t, yn, c), me).wait_recv()
            for cp in after_y:
                cp.start()

    def relay_diagonal():
        c, chip, me, sib, (xn, yn, dg) = plan()
        for t in range(n):
            if split[t]:
                copy(t, 2, blk(t, dg, c, 0), me).wait_recv()
                copy(t, 3, blk(t, dg, c, 1), me).wait_recv()
            else:
                copy(t, 2, blk(t, dg, c), me).wait_recv()
            relays(t)[2][0].start()

    def take_sibling():
        _, chip, me, sib, (xn, yn, dg) = plan()
        theirs = 1 - lax.axis_index("c") if layer is None else layer
        for t in range(n):
            for k, peer in ((4, xn), (5, yn), (6, dg)):
                copy(t, k, blk(t, peer, theirs), me).wait_recv()

    def drain_sends():
        for t in range(n):
            after_x, after_y, after_d = relays(t)
            for cp in own_sends(t) + after_x + after_y + after_d:
                cp.wait_send()

    def finish():
        role(False, take_sibling)
        role(True, drain_sends)

    return ((lambda: role(True, send_own)), (lambda: role(True, relay_neighbours)),
            (lambda: role(True, relay_diagonal)), finish)


def _swap_copies(ins, outs, send_sems, recv_sems):
    x, y, c = _me()
    return [pltpu.make_async_remote_copy(src_ref=ins[t].at[1 - c], dst_ref=outs[t], send_sem=send_sems.at[t],
                                         recv_sem=recv_sems.at[t], device_id=(x, y, 1 - c), device_id_type=MESH)
            for t in range(len(ins))]


def _exchange_copies(peers, ins, outs, send_sems, recv_sems):
    x, y, c = _me()
    chips = _other_chips(x, y)
    n = len(peers)
    return [pltpu.make_async_remote_copy(src_ref=ins[t].at[2 * chips[j][0] + chips[j][1]], dst_ref=outs[t].at[i],
                                         send_sem=send_sems.at[n * t + i], recv_sem=recv_sems.at[n * t + i],
                                         device_id=(*chips[j], c), device_id_type=MESH)
            for i, j in enumerate(peers) for t in range(len(ins))]


class _Rider:
    def __init__(self, arrs, out_shape, nsem, copies):
        self.arrs, self.out_shape, self.nsem, self.copies = list(arrs), out_shape, nsem, copies
        self.n = len(self.arrs)

    def scratch(self):
        return [pltpu.SemaphoreType.DMA((self.nsem,)), pltpu.SemaphoreType.DMA((self.nsem,))]

    def start(self, ins, outs, sems):
        for cp in self.copies(ins, outs, *sems):
            cp.start()

    def wait(self, ins, outs, sems):
        for cp in self.copies(ins, outs, *sems):
            cp.wait()


def _swap_rider(gs):
    return _Rider(gs, [SDS(g.shape[1:], g.dtype) for g in gs], len(gs), _swap_copies)


def _exchange_rider(ts, peers=(0, 1, 2)):
    return _Rider(ts, [SDS((len(peers),) + t.shape[1:], t.dtype) for t in ts], len(peers) * len(ts),
                  functools.partial(_exchange_copies, peers))


HBM = BS(memory_space=pltpu.HBM)
SEM = BS(memory_space=pltpu.SEMAPHORE)
SPLIT_PARAMS = pltpu.CompilerParams(has_side_effects=pltpu.SideEffectType.DATAFLOW_SIDE_EFFECTING)


def _split_copies(swap):
    return _swap_copies if swap else functools.partial(_exchange_copies, (0, 1, 2))


def _exchange_start(ts, name, swap=False):
    n = len(ts)
    nsem = n if swap else 3 * n
    lands = [lax.empty(t.shape[1:] if swap else (3,) + t.shape[1:], t.dtype) for t in ts]

    def body(*refs):
        for cp in _split_copies(swap)(refs[:n], refs[n:2 * n], refs[2 * n], refs[2 * n + 1]):
            cp.start()
        refs[-1][...] = jnp.zeros_like(refs[-1])

    held = [pltpu.with_memory_space_constraint(a, pltpu.HBM) for a in list(ts) + lands]
    out = pl.pallas_call(
        body, name=name,
        out_shape=(pltpu.SemaphoreType.DMA((nsem,)), pltpu.SemaphoreType.DMA((nsem,)),
                   *[pltpu.HBM(a.shape, a.dtype) for a in held], SDS((8, 128), F32)),
        in_specs=[HBM] * (2 * n), out_specs=(SEM, SEM, *[HBM] * (2 * n), BS(memory_space=pltpu.VMEM)),
        input_output_aliases={i: 2 + i for i in range(2 * n)}, compiler_params=SPLIT_PARAMS)(*held)
    return out[0], out[1], list(out[2:2 + n]), list(out[2 + n:2 + 2 * n]), out[-1]


def _exchange_wait(send_sems, recv_sems, ts, lands, after, name, swap=False):
    n = len(ts)

    def body(*refs):
        for cp in _split_copies(swap)(refs[:n], refs[n:2 * n], refs[2 * n], refs[2 * n + 1]):
            cp.wait_send()
            cp.wait_recv()

    out = pl.pallas_call(
        body, name=name, out_shape=tuple(pltpu.HBM(a.shape, a.dtype) for a in ts + lands),
        in_specs=[HBM] * (2 * n) + [SEM, SEM, ANY], out_specs=[HBM] * (2 * n),
        input_output_aliases={i: i for i in range(2 * n)}, compiler_params=SPLIT_PARAMS)(
        *ts, *lands, send_sems, recv_sems, after)
    return (list(out[n:2 * n]), list(out[:n])) if swap else list(out[n:2 * n])


def _ride_alone(rider, name):
    n = rider.n

    def body(*refs):
        rider.start(refs[:n], refs[n:2 * n], refs[2 * n:])
        rider.wait(refs[:n], refs[n:2 * n], refs[2 * n:])

    return pl.pallas_call(body, name=name, out_shape=rider.out_shape, in_specs=[ANY] * n, out_specs=[ANY] * n,
                          scratch_shapes=rider.scratch(), compiler_params=COMM_PARAMS)(*rider.arrs)


def _join_layers(us, name):
    n = len(us)

    def body(*refs):
        outs, send_sems, recv_sems = refs[n:2 * n], refs[2 * n], refs[2 * n + 1]
        x, y, c = _me()
        cps = [pltpu.make_async_remote_copy(src_ref=outs[t].at[c], dst_ref=outs[t].at[c], send_sem=send_sems.at[t],
                                            recv_sem=recv_sems.at[t], device_id=(x, y, 1 - c), device_id_type=MESH)
               for t in range(n)]
        for cp in cps:
            cp.start()
        for cp in cps:
            cp.wait()

    return pl.pallas_call(body, name=name, out_shape=[SDS(u.shape, u.dtype) for u in us],
                          in_specs=[ANY] * n, out_specs=[ANY] * n, input_output_aliases={t: t for t in range(n)},
                          scratch_shapes=[pltpu.SemaphoreType.DMA((n,)), pltpu.SemaphoreType.DMA((n,))],
                          compiler_params=COMM_PARAMS)(*us)


def _allsum_small(v, name, after):
    M = v.shape[0]

    def body(x_ref, after_ref, o_ref, all_ref, send_sems, recv_sems, local_sem):
        x, y, c = _me()
        me, sib = (x, y, c), (x, y, 1 - c)
        chips = _other_chips(x, y)

        def rows(px, py, pc):
            return all_ref.at[pl.ds((4 * px + 2 * py + pc) * M, M), :]

        def copy(k, block, to, src=None):
            return pltpu.make_async_remote_copy(src_ref=rows(*block) if src is None else src, dst_ref=rows(*block),
                                                send_sem=send_sems.at[k], recv_sem=recv_sems.at[k],
                                                device_id=to, device_id_type=MESH)

        mine = pltpu.make_async_copy(x_ref, rows(*me), local_sem)
        mine.start()
        first = [copy(0, me, sib, src=x_ref)]
        first += [copy(1 + j, me, (*chip, c), src=x_ref) for j, chip in enumerate(chips)]
        for cp in first:
            cp.start()
        passed = [copy(4 + j, (*chip, c), sib) for j, chip in enumerate(chips)]
        for j, chip in enumerate(chips):
            copy(1 + j, (*chip, c), me).wait_recv()
            passed[j].start()
        copy(0, sib, me).wait_recv()
        for j, chip in enumerate(chips):
            copy(4 + j, (*chip, 1 - c), me).wait_recv()
        for cp in first + passed:
            cp.wait_send()
        mine.wait()
        acc = all_ref[0:M, :]
        for d in range(1, 8):
            acc = acc + all_ref[d * M:(d + 1) * M, :]
        o_ref[...] = acc

    vm = BS(memory_space=pltpu.VMEM)
    return pl.pallas_call(body, name=name, out_shape=SDS((M, LANES), F32), in_specs=[vm, ANY], out_specs=vm,
                          scratch_shapes=[pltpu.VMEM((8 * M, LANES), F32), pltpu.SemaphoreType.DMA((7,)),
                                          pltpu.SemaphoreType.DMA((7,)), pltpu.SemaphoreType.DMA],
                          compiler_params=pltpu.CompilerParams(has_side_effects=True, vmem_limit_bytes=VMEM_LIMIT))(
        v, after)


FFN = ("w_gate_up", "w_down")
REST = ("w_in", "w_o", "w_uq", "w_ukv")
BIG = FFN + REST
TINY = ("conv_w",)
REPL = ("attn_norm", "mla_q_norm", "mla_kv_norm", "pool_w", "pool_scale", "swa_sinks", "mix_norm", "ffn_norm",
        "final_norm")
ORDER = ("attn_norm", "w_in", "mla_q_norm", "w_uq", "mla_kv_norm", "w_ukv", "conv_w", "pool_w", "pool_scale",
         "swa_sinks", "mix_norm", "w_o", "ffn_norm", "w_gate_up", "w_down", "final_norm")


def _rows8(shape):
    return -(-int(np.prod(shape)) // (8 * LANES)) * 8


def _pack(arrs):
    parts = []
    for a in arrs:
        r = _rows8(a.shape)
        parts.append(jnp.pad(a.reshape(-1), (0, r * LANES - a.size)).reshape(r, LANES))
    return jnp.concatenate(parts, axis=0)


def _unpack(buf, shapes):
    out, r0 = [], 0
    for s in shapes:
        n, r = int(np.prod(s)), _rows8(s)
        rows = buf[r0:r0 + r]
        out.append(rows.reshape(s) if n == r * LANES else rows.reshape(-1)[:n].reshape(s))
        r0 += r
    return out


def _cols_joined(g):
    return jnp.transpose(g, (0, 2, 1, 3)).reshape(g.shape[0], g.shape[2], 4 * g.shape[3])


def _cols_split(w):
    n, a, b4 = w.shape
    return jnp.transpose(w.reshape(n, a, 4, b4 // 4), (0, 2, 1, 3))


def _rope_tables(S):
    inv = 1.0 / (10000.0 ** (jnp.arange(0, 32, 2, dtype=F32) / 32))
    ang = jnp.arange(S, dtype=F32)[:, None] * inv[None, :]
    cos, sin = jnp.cos(ang), jnp.sin(ang)
    z = lambda w: jnp.zeros((S, w), F32)
    tc = jnp.concatenate([jnp.ones((S, 64), F32), cos, cos, jnp.ones((S, 32), F32)], axis=1)
    ts1 = jnp.concatenate([z(64), -sin, z(48)], axis=1)
    ts2 = jnp.concatenate([z(80), sin, z(32)], axis=1)
    return tc, ts1, ts2


def _pad_w_in(wt):
    z = lambda n: jnp.zeros((wt.shape[0], n, wt.shape[2]), wt.dtype)
    return jnp.concatenate([wt[:, 0:384], z(64), wt[:, 384:416], z(32), wt[:, 416:1952]], axis=1)


def _unpad_w_in(dt):
    return jnp.concatenate([dt[:, 0:384], dt[:, 448:480], dt[:, 512:2048]], axis=1)


def _pad_heads(w, src, offs):
    cols = []
    for h in range(HEADS):
        src0, n = src[h]
        z = lambda k: jnp.zeros(w.shape[:-1] + (k,), w.dtype)
        cols += [z(offs[h]), w[..., src0:src0 + n], z(128 - offs[h] - n)]
    return jnp.concatenate(cols, axis=-1)


UQ_SRC = [(h * 96, 96) for h in range(HEADS)]
KN_SRC = [(h * 128, 64) for h in range(HEADS)]
V_SRC = [(h * 128 + 64, 64) for h in range(HEADS)]
ZERO_OFF = [0] * HEADS
V_OFF = [(h % 2) * 64 for h in range(HEADS)]


def _unpad_heads(d, src, offs):
    return [d[..., h * 128 + offs[h]: h * 128 + offs[h] + src[h][1]] for h in range(HEADS)]


def kernel(x, attn_norm, w_in, mla_q_norm, w_uq, mla_kv_norm, w_ukv, conv_w, pool_w, pool_scale, swa_sinks, mix_norm, w_o, ffn_norm, w_gate_up, w_down, final_norm, loss_target, m_attn_norm, m_w_in, m_mla_q_norm, m_w_uq, m_mla_kv_norm, m_w_ukv, m_conv_w, m_pool_w, m_pool_scale, m_swa_sinks, m_mix_norm, m_w_o, m_ffn_norm, m_w_gate_up, m_w_down, m_final_norm, v_attn_norm, v_w_in, v_mla_q_norm, v_w_uq, v_mla_kv_norm, v_w_ukv, v_conv_w, v_pool_w, v_pool_scale, v_swa_sinks, v_mix_norm, v_w_o, v_ffn_norm, v_w_gate_up, v_w_down, v_final_norm):
    W = dict(attn_norm=attn_norm, w_in=w_in, mla_q_norm=mla_q_norm, w_uq=w_uq, mla_kv_norm=mla_kv_norm, w_ukv=w_ukv,
             conv_w=conv_w, pool_w=pool_w, pool_scale=pool_scale, swa_sinks=swa_sinks, mix_norm=mix_norm, w_o=w_o,
             ffn_norm=ffn_norm, w_gate_up=w_gate_up, w_down=w_down, final_norm=final_norm)
    M1 = dict(attn_norm=m_attn_norm, w_in=m_w_in, mla_q_norm=m_mla_q_norm, w_uq=m_w_uq, mla_kv_norm=m_mla_kv_norm,
              w_ukv=m_w_ukv, conv_w=m_conv_w, pool_w=m_pool_w, pool_scale=m_pool_scale, swa_sinks=m_swa_sinks,
              mix_norm=m_mix_norm, w_o=m_w_o, ffn_norm=m_ffn_norm, w_gate_up=m_w_gate_up, w_down=m_w_down,
              final_norm=m_final_norm)
    V2 = dict(attn_norm=v_attn_norm, w_in=v_w_in, mla_q_norm=v_mla_q_norm, w_uq=v_w_uq, mla_kv_norm=v_mla_kv_norm,
              w_ukv=v_w_ukv, conv_w=v_conv_w, pool_w=v_pool_w, pool_scale=v_pool_scale, swa_sinks=v_swa_sinks,
              mix_norm=v_mix_norm, w_o=v_w_o, ffn_norm=v_ffn_norm, w_gate_up=v_w_gate_up, w_down=v_w_down,
              final_norm=v_final_norm)
    S = x.shape[1]
    xc, yc, cc = _me()
    chip = 2 * xc + yc
    kc = jnp.stack([chip, cc]).astype(jnp.int32)

    first, later = ("w_in", "w_uq", "w_ukv", "conv_w"), ("w_o", "w_gate_up", "w_down")
    T = lambda a: jnp.swapaxes(a, 1, 2)
    W["w_in"], M1["w_in"], V2["w_in"] = T(w_in), T(m_w_in), T(v_w_in)
    placed = {n: _place(W[n], kc, F32 if n == "conv_w" else BF16, f"place_{n}") for n in first + later}
    gi, gq, gkv, gcv = _gather([placed[n] for n in first], "gather_weights")
    later_w = [placed[n] for n in later]
    win_p = _pad_w_in(gi.reshape(2, 4 * gi.shape[2], D))
    wuq_p = _pad_heads(_cols_joined(gq), UQ_SRC, ZERO_OFF)
    wukv = _cols_joined(gkv)
    wk_p = _pad_heads(wukv, KN_SRC, ZERO_OFF)
    wv_p = _pad_heads(wukv, V_SRC, V_OFF)
    conv8 = jnp.pad(_cols_joined(gcv), ((0, 0), (0, 5), (0, 0)))
    pwd = jnp.concatenate([jnp.concatenate(
        [jnp.pad(pool_w[:, 2 * b], ((0, 0), (0, 0), (0, 64))), jnp.pad(pool_w[:, 2 * b + 1], ((0, 0), (0, 0), (64, 0)))],
        axis=1) for b in range(2)], axis=1).astype(BF16)
    tabs = _rope_tables(S)
    g_attn, g_q, g_kv, g_mix, g_ffn, g_ps = (_g3(W[n]) for n in ("attn_norm", "mla_q_norm", "mla_kv_norm", "mix_norm",
                                                                  "ffn_norm", "pool_scale"))

    xs = [x[0]]
    saved = []
    for l in range(DEPTH):
        x0 = xs[-1]
        proj, h = _norm_mm(x0, g_attn, l, win_p, _wspec_in(l), D_INP, D_INP, F32, f"in_proj{l}", w_t=True)
        q, k, v, kt, vt = _mla_prep(proj, g_q, g_kv, wuq_p, wk_p, wv_p, tabs, l, f"mla_prep{l}")
        ya, lse, later_w = _mla_attn(q, k, vt, later_w, l, f"mla_attn{l}")
        go, gu4, gd = later_w
        wo, wdown = go.reshape(2, D, D), gd.reshape(2, D_FF, D)
        yb = _conv(proj, conv8, l, f"conv{l}")
        ycp = _pool(proj, pwd, g_ps, l, f"pool{l}")
        yd = _swa(proj, swa_sinks, l, f"swa{l}")
        x1, ycat, mixed = _mix_out(x0, ya, yb, ycp, yd, g_mix, wo, l, f"mix_out{l}")
        gu, h2 = _norm_mm(x1, g_ffn, l, gu4, _wspec_gu(l), 2 * D_FF, 2 * D_FF // 4, BF16, f"gate_up{l}")
        x2, act = _swiglu_mm_res(x1, gu, wdown, l, f"down{l}")
        saved.append(dict(x0=x0, proj=proj, h=h, q=q, k=k, kt=kt, v=v, lse=lse, x1=x1, ycat=ycat, mixed=mixed,
                          gu=gu, h2=h2, act=act))
        xs.append(x2)

    dx, dx16, dg_final, loss_tile = _loss_head(xs[-1], final_norm.reshape(1, D), loss_target[0], "loss_head")
    loss_here = (loss_tile[0, 0] * (0.5 / D)).reshape(1)

    G = {n: [None] * DEPTH for n in ("w_uq", "w_ukv") + TINY + REPL if n != "final_norm"}
    gw_in = gw_o = gw_gu = gw_down = None
    for l in reversed(range(DEPTH)):
        sv = saved[l]
        dgu = _bwd_down(dx16, wdown, sv["gu"], l, f"down_bwd{l}")
        gw_down = _mm_tn(sv["act"], dx16, l, gw_down, f"dw_down{l}")
        gw_gu = _mm_tn(sv["h2"], dgu, l, gw_gu, f"dw_gate_up{l}", split4=True)
        ffn_token = None
        if l == 0:
            g_f = [gw_gu, gw_down.reshape(2, 4, D_FF // 4, D)]
            dx1, dx1_16, dg, got_f = _mm_nt_normbwd(dgu, gu4, l, sv["x1"], g_ffn, dx, 1, f"gate_up_bwd{l}",
                                                    rider=_swap_rider(g_f))
            pairs_f = [_pair_sum(g, o, kc, f"rs_pair_sum_{n}") for g, o, n in zip(g_f, got_f, FFN)]
            ffn_flight = _exchange_start([p[1] for p in pairs_f], "rs_exchange_start_ffn")
            ffn_token = ffn_flight[-1]
        else:
            dx1, dx1_16, dg = _mm_nt_normbwd(dgu, gu4, l, sv["x1"], g_ffn, dx, 1, f"gate_up_bwd{l}")
        G["ffn_norm"][l] = dg[0]
        gw_o = _mm_tn(sv["mixed"], dx1_16, l, gw_o, f"dw_o{l}")
        dycat, dg = _mm_nt_normbwd(dx1_16, wo.reshape(2, 1, D, D), l, sv["ycat"], g_mix, None, 4, f"mix_bwd{l}")
        G["mix_norm"][l] = dg[0]

        proj = sv["proj"]
        delta = _mla_delta(dycat, sv["ycat"], f"mla_delta{l}", after=ffn_token)
        dq, dk, dv, _ = _mla_attn_bwd(sv["q"], sv["k"], sv["kt"], sv["v"], dycat, sv["lse"], delta, None,
                                      f"mla_attn_bwd{l}")
        dcq, dckv, dkr, dwuq, dwk, dwv, dgq, dgkv = _mla_prep_bwd(
            dq, dk, dv, proj, g_q, g_kv, wuq_p, wk_p, wv_p, tabs, l, f"mla_prep_bwd{l}")
        dgb, dgc, duc, dcw = _conv_bwd(proj, conv8, dycat, l, f"conv_bwd{l}")
        dup, dpw, dps = _pool_bwd(proj, pwd, g_ps, dycat, l, f"pool_bwd{l}")
        dqs, dks, dvs, dsink = _swa_bwd(proj, swa_sinks, dycat, l, f"swa_bwd{l}")
        dproj = jnp.concatenate([dcq, dckv, dkr, dgb, dgc, duc, dup, dqs, dks, dvs], axis=1)
        gw_in = _mm_tn(dproj, sv["h"], l, gw_in, f"dw_in{l}")
        G["w_uq"][l] = jnp.concatenate(_unpad_heads(dwuq, UQ_SRC, ZERO_OFF), axis=1)
        kn, vv = _unpad_heads(dwk, KN_SRC, ZERO_OFF), _unpad_heads(dwv, V_SRC, V_OFF)
        G["w_ukv"][l] = jnp.concatenate([t for h in range(HEADS) for t in (kn[h], vv[h])], axis=1)
        swap_token = None
        if l == 0:
            g_r = [_unpad_w_in(gw_in).reshape(2, 4, -1, D), gw_o.reshape(2, 4, D // 4, D),
                   _cols_split(jnp.stack(G["w_uq"])), _cols_split(jnp.stack(G["w_ukv"]))]
            swap_flight = _exchange_start(g_r, "rs_swap_start", swap=True)
            swap_token = swap_flight[-1]
        dx, dx16, dg = _mm_nt_normbwd(dproj, win_p.reshape(2, 1, D_INP, D), l, sv["x0"], g_attn, dx1, 1, f"in_proj_bwd{l}",
                                      w_t=True, after=swap_token)
        G["attn_norm"][l] = dg[0]
        G["mla_q_norm"][l] = dgq[0]
        G["mla_kv_norm"][l] = dgkv[0]
        G["conv_w"][l] = dcw[0:3]
        G["pool_w"][l] = jnp.stack([dpw[0:64, 0:64], dpw[64:128, 64:128], dpw[128:192, 0:64], dpw[192:256, 64:128]])
        G["pool_scale"][l] = dps[0]
        G["swa_sinks"][l] = dsink[0, 0:4]
    grad_x = dx[None]
    Gl = {n: jnp.stack(G[n]) for n in TINY + REPL if n != "final_norm"}
    Gl["final_norm"] = dg_final[0]

    got_r, g_r = _exchange_wait(*swap_flight[:-1], dx16, "rs_swap_wait", swap=True)
    pairs_r = [_pair_sum(g, o, kc, f"rs_pair_sum_{n}") for g, o, n in zip(g_r, got_r, REST)]
    got3_f = _exchange_wait(*ffn_flight[:-1], dx16, "rs_exchange_wait_ffn")
    in_flight = _exchange_start([p[1] for p in pairs_r], "rs_exchange_start")
    us_f = [_chip_sum(p[0], [o3], kc, f"rs_chip_sum_{n}", after=in_flight[-1])
            for p, o3, n in zip(pairs_f, got3_f, FFN)]
    gsum_f = _join_layers(us_f, "rs_join_cores_ffn")
    res = {}

    def update(names, grads):
        for n, g in zip(names, grads):
            d_, m_, v_, g_ = _adamw(W[n], g, M1[n], V2[n], f"adamw_{n}", echo=True)
            back = T if n == "w_in" else (lambda a: a)
            res["g", n], res["d", n], res["m", n], res["v", n] = back(g_), back(d_), back(m_), back(v_)

    update(FFN, gsum_f)
    small = TINY + REPL
    full_shapes = [Gl[n].shape for n in small] + [(1,)]
    summed = _allsum_small(_pack([Gl[n] for n in small] + [loss_here]), "allsum_small", after=res["d", FFN[-1]])
    got3_r = _exchange_wait(*in_flight[:-1], summed, "rs_exchange_wait")
    us_r = [_chip_sum(p[0], [o3], kc, f"rs_chip_sum_{n}") for p, o3, n in zip(pairs_r, got3_r, REST)]
    update(REST, _join_layers(us_r, "rs_join_cores"))
    summed = _unpack(summed, full_shapes)
    loss = summed.pop().reshape(())

    def as3(a):
        if a.ndim <= 2:
            return a.reshape((1,) * (3 - a.ndim) + a.shape)
        return a.reshape(a.shape[0], -1, a.shape[-1])

    for n, g in zip(small, summed):
        if n in TINY:
            wdt = W[n].shape[2]
            g = lax.dynamic_slice_in_dim(g, chip * wdt, wdt, axis=2)
        out = _adamw(as3(W[n]), as3(g), as3(M1[n]), as3(V2[n]), f"adamw_{n}")
        res["g", n] = g
        res["d", n], res["m", n], res["v", n] = (o.reshape(W[n].shape) for o in out)

    return (loss, grad_x, *[res["g", n] for n in ORDER], *[res["d", n] for n in ORDER],
            *[res["m", n] for n in ORDER], *[res["v", n] for n in ORDER])
```

```python
import functools
import math

import numpy as np
import jax
import jax.numpy as jnp
from jax import lax
from jax.experimental import pallas as pl
from jax.experimental.pallas import tpu as pltpu

F32, BF16 = jnp.float32, jnp.bfloat16
SDS = jax.ShapeDtypeStruct
BS = pl.BlockSpec
MESH = pl.DeviceIdType.MESH

D = 1024
DEPTH = 2
HEADS = 4
D_FF = 2816
D_INP = 2048
EPS = 1e-6
SWA_WINDOW = 128
BLK = 128
SLOPES = tuple(2.0 ** (-8.0 * (i + 1) / 4) for i in range(4))
QK_SCALE = 1.0 / math.sqrt(96)
SWA_SCALE = 1.0 / math.sqrt(64)
LR, B1, B2, ADAM_EPS, WD, STEP = 0.001, 0.9, 0.999, 1e-08, 0.01, 10

LANES = 1024
VMEM_LIMIT = 56 * 1024 * 1024
NEG_INF = float("-inf")

C_CQ, C_CKV, C_KR, C_GB, C_GC, C_UC, C_UP, C_QS, C_KS, C_VS = 0, 256, 384, 512, 768, 1024, 1280, 1536, 1792, 1920


def _params(ngrid):
    return pltpu.CompilerParams(dimension_semantics=("arbitrary",) * ngrid, vmem_limit_bytes=VMEM_LIMIT)


def _pc(body, *, name, grid, in_specs, out_specs, out_shape, scratch=(), aliases=None):
    return pl.pallas_call(
        body, name=name, grid=grid, in_specs=in_specs, out_specs=out_specs, out_shape=out_shape,
        scratch_shapes=scratch, input_output_aliases=aliases or {}, compiler_params=_params(len(grid)))


def _dot(a, b):
    return jnp.dot(a, b, preferred_element_type=F32)


def _dot_nt(a, b):
    return lax.dot_general(a, b, (((1,), (1,)), ((), ())), preferred_element_type=F32)


def _dot_tn(a, b):
    return lax.dot_general(a, b, (((0,), (0,)), ((), ())), preferred_element_type=F32)


def _tile(n, cap):
    if n <= cap:
        return n
    t = cap - cap % 128
    while n % t:
        t -= 128
    return t


def _row_tile(a, b, cap=262144):
    bp = -(-b // 128) * 128
    best = None
    for t in range(8, a + 1, 8):
        if a % t == 0 and t * bp <= cap:
            best = t
    if best is None or (best < 64 and a * bp <= 2 * cap):
        return a
    return best


def _g3(a):
    return a.reshape(a.shape[0], 1, a.shape[1])


def _norm_mm(x, g3, l, w, wspec, N, tn, out_dtype, name, w_t=False):
    S, K = x.shape
    tm = min(1024 if out_dtype == BF16 else 512, S)

    def body(x_ref, g_ref, w_ref, y_ref, h_ref):
        @pl.when(pl.program_id(1) == 0)
        def _():
            xv = x_ref[...]
            r = lax.rsqrt(jnp.mean(xv * xv, axis=-1, keepdims=True) + EPS)
            h_ref[...] = (xv * r * g_ref[...]).astype(BF16)

        y_ref[...] = (_dot_nt if w_t else _dot)(h_ref[...], w_ref[...]).astype(out_dtype)

    return _pc(body, name=name, grid=(S // tm, N // tn),
               in_specs=[BS((tm, K), lambda i, j: (i, 0)), BS((None, 1, K), lambda i, j: (l, 0, 0)), wspec],
               out_specs=[BS((tm, tn), lambda i, j: (i, j)), BS((tm, K), lambda i, j: (i, 0))],
               out_shape=[SDS((S, N), out_dtype), SDS((S, K), BF16)])(x, g3, w)


def _wspec_in(l):
    return BS((None, D_INP, D), lambda i, j: (l, j, 0))


def _wspec_gu(l):
    return BS((None, None, D, 2 * D_FF // 4), lambda i, j: (l, j, 0, 0))


def _mix_out(x0, ya, yb, yc, yd, gmix3, wo, l, name):
    S = x0.shape[0]
    tm = min(512, S)

    def body(x_ref, ya_ref, yb_ref, yc_ref, yd_ref, g_ref, w_ref, x1_ref, ycat_ref, mixed_ref):
        groups = [ya_ref[...], yb_ref[...], yc_ref[...], yd_ref[...]]
        for gi, yg in enumerate(groups):
            sl = slice(gi * 256, (gi + 1) * 256)
            r = lax.rsqrt(jnp.mean(yg * yg, axis=-1, keepdims=True) + EPS)
            ycat_ref[:, sl] = yg
            mixed_ref[:, sl] = (yg * r * g_ref[:, sl]).astype(BF16)
        x1_ref[...] = x_ref[...] + _dot(mixed_ref[...], w_ref[...])

    row = lambda w: BS((tm, w), lambda i: (i, 0))
    return _pc(body, name=name, grid=(S // tm,),
               in_specs=[row(D), row(256), row(256), row(256), row(256), BS((None, 1, D), lambda i: (l, 0, 0)),
                         BS((None, D, D), lambda i: (l, 0, 0))],
               out_specs=[row(D), row(D), row(D)],
               out_shape=[SDS((S, D), F32), SDS((S, D), F32), SDS((S, D), BF16)])(x0, ya, yb, yc, yd, gmix3, wo)


def _swiglu_mm_res(x1, gu, wdown, l, name):
    S = x1.shape[0]
    tm = min(256, S)

    def body(x_ref, gate_ref, up_ref, w_ref, x2_ref, act_ref):
        acc = x_ref[...]
        for c0 in range(0, D_FF, D_FF // 2):
            cs = slice(c0, c0 + D_FF // 2)
            gt = gate_ref[:, cs].astype(F32)
            act = (gt * pl.reciprocal(1.0 + jnp.exp(-gt), approx=True) * up_ref[:, cs].astype(F32)).astype(BF16)
            act_ref[:, cs] = act
            acc = acc + _dot(act, w_ref[cs, :])
        x2_ref[...] = acc

    return _pc(body, name=name, grid=(S // tm,),
               in_specs=[BS((tm, D), lambda i: (i, 0)), BS((tm, D_FF), lambda i: (i, 0)),
                         BS((tm, D_FF), lambda i: (i, 1)), BS((None, D_FF, D), lambda i: (l, 0, 0))],
               out_specs=[BS((tm, D), lambda i: (i, 0)), BS((tm, D_FF), lambda i: (i, 0))],
               out_shape=[SDS((S, D), F32), SDS((S, D_FF), BF16)])(x1, gu, gu, wdown)


def _loss_head(x, g, tgt, name):
    S = x.shape[0]
    tm = min(512, S)

    def body(x_ref, g_ref, t_ref, dx_ref, dx16_ref, dg_ref, loss_ref):
        @pl.when(pl.program_id(0) == 0)
        def _():
            dg_ref[...] = jnp.zeros_like(dg_ref)
            loss_ref[...] = jnp.zeros_like(loss_ref)

        xv = x_ref[...]
        r = lax.rsqrt(jnp.mean(xv * xv, axis=-1, keepdims=True) + EPS)
        xh = xv * r
        gv = g_ref[...]
        diff = xh * gv - t_ref[...]
        loss_ref[...] += jnp.sum(diff * diff)
        dy = diff * (1.0 / D)
        dg_ref[...] += jnp.sum(dy * xh, axis=0, keepdims=True)
        dxh = dy * gv
        dx = r * (dxh - xh * jnp.mean(dxh * xh, axis=-1, keepdims=True))
        dx_ref[...] = dx
        dx16_ref[...] = dx.astype(BF16)

    row = BS((tm, D), lambda i: (i, 0))
    return _pc(body, name=name, grid=(S // tm,),
               in_specs=[row, BS((1, D), lambda i: (0, 0)), row],
               out_specs=[row, row, BS((8, D), lambda i: (0, 0)), BS((8, 128), lambda i: (0, 0))],
               out_shape=[SDS((S, D), F32), SDS((S, D), BF16), SDS((8, D), F32), SDS((8, 128), F32)])(x, g, tgt)


def _mm_tn(a, b, l, prev, name, split4=False):
    S, Ka = a.shape
    N = b.shape[1]
    if split4:
        ta, tn = _tile(Ka, 256), N // 4
        out_shape = SDS((2, 4, Ka, tn), F32)
        out_spec = BS((None, None, ta, tn), lambda j, i: (l, j, i, 0))
    else:
        ta, tn = _tile(Ka, 512), _tile(N, 1024)
        out_shape = SDS((2, Ka, N), F32)
        out_spec = BS((None, ta, tn), lambda j, i: (l, i, j))

    def body(a_ref, b_ref, *rest):
        rest[-1][...] = _dot_tn(a_ref[...], b_ref[...])

    in_specs = [BS((S, ta), lambda j, i: (0, i)), BS((S, tn), lambda j, i: (0, j))]
    args = [a, b]
    if prev is not None:
        in_specs.append(BS(memory_space=pl.ANY))
        args.append(prev)
    return _pc(body, name=name, grid=(N // tn, Ka // ta), in_specs=in_specs, out_specs=out_spec, out_shape=out_shape,
               aliases={2: 0} if prev is not None else None)(*args)


def _bwd_down(dx16, wdown, gu, l, name):
    S = dx16.shape[0]
    tm = min(256, S)

    def body(dx_ref, w_ref, gate_ref, up_ref, dgu_ref):
        dxv = dx_ref[...]
        for c0 in range(0, D_FF, 256):
            cs = slice(c0, c0 + 256)
            dact = _dot_nt(dxv, w_ref[cs, :])
            gt = gate_ref[:, cs].astype(F32)
            sg = pl.reciprocal(1.0 + jnp.exp(-gt), approx=True)
            dgu_ref[:, cs] = (dact * up_ref[:, cs].astype(F32) * (sg * (1.0 + gt * (1.0 - sg)))).astype(BF16)
            dgu_ref[:, D_FF + c0:D_FF + c0 + 256] = (dact * (gt * sg)).astype(BF16)

    return _pc(body, name=name, grid=(S // tm,),
               in_specs=[BS((tm, D), lambda i: (i, 0)), BS((None, D_FF, D), lambda i: (l, 0, 0)),
                         BS((tm, D_FF), lambda i: (i, 0)), BS((tm, D_FF), lambda i: (i, 1))],
               out_specs=BS((tm, 2 * D_FF), lambda i: (i, 0)),
               out_shape=SDS((S, 2 * D_FF), BF16))(dx16, wdown, gu, gu)


def _mm_nt_normbwd(dy, w4, l, x, g3, dres, ngroups, name, rider=None, w_t=False, after=None):
    S, K = dy.shape
    nk, kc = w4.shape[1], w4.shape[2 if w_t else 3]
    mm = _dot if w_t else _dot_nt
    tm = min(512, S)
    gw = D // ngroups
    has_res = dres is not None
    nr = rider.n if rider else 0
    n_in, n_out = 4 + has_res + (after is not None), 2 + has_res

    def body(*refs):
        dy_ref, w_ref, x_ref, g_ref = refs[:4]
        res_ref = refs[4] if has_res else None
        outs = refs[n_in + nr:n_in + nr + n_out]
        dx_ref, dg_ref = outs[0], outs[-1]
        dx16_ref = outs[1] if has_res else None
        r_io = (refs[n_in:n_in + nr], refs[n_in + nr + n_out:n_in + 2 * nr + n_out], refs[n_in + 2 * nr + n_out:])
        if rider:
            pl.when(pl.program_id(0) == 0)(lambda: rider.start(*r_io))

        @pl.when(pl.program_id(0) == 0)
        def _():
            dg_ref[...] = jnp.zeros_like(dg_ref)

        dh = mm(dy_ref[:, 0:kc], w_ref[0])
        for k in range(1, nk):
            dh = dh + mm(dy_ref[:, k * kc:(k + 1) * kc], w_ref[k])
        for gi in range(ngroups):
            sl = slice(gi * gw, (gi + 1) * gw)
            xg = x_ref[:, sl]
            r = lax.rsqrt(jnp.mean(xg * xg, axis=-1, keepdims=True) + EPS)
            xh = xg * r
            dhg = dh[:, sl]
            dg_ref[:, sl] += jnp.sum(dhg * xh, axis=0, keepdims=True)
            dxh = dhg * g_ref[:, sl]
            dxg = r * (dxh - xh * jnp.mean(dxh * xh, axis=-1, keepdims=True))
            if has_res:
                dxg = dxg + res_ref[:, sl]
                dx16_ref[:, sl] = dxg.astype(BF16)
            dx_ref[:, sl] = dxg
        if rider:
            pl.when(pl.program_id(0) == S // tm - 1)(lambda: rider.wait(*r_io))

    row = BS((tm, D), lambda i: (i, 0))
    in_specs = [BS((tm, K), lambda i: (i, 0)),
                BS((None,) + tuple(w4.shape[1:]), lambda i: (l, 0, 0, 0), pipeline_mode=pl.Buffered(1)), row,
                BS((None, 1, D), lambda i: (l, 0, 0))]
    args = [dy, w4, x, g3]
    out_specs, out_shape = [row], [SDS((S, D), F32)]
    if has_res:
        in_specs.append(row)
        args.append(dres)
        out_specs.append(row)
        out_shape.append(SDS((S, D), BF16))
    if after is not None:
        in_specs.append(BS((8, 128), lambda i: (0, 0)))
        args.append(after)
    out_specs.append(BS((8, D), lambda i: (0, 0)))
    out_shape.append(SDS((8, D), F32))
    if not rider:
        return _pc(body, name=name, grid=(S // tm,), in_specs=in_specs, out_specs=out_specs, out_shape=out_shape)(*args)
    out = pl.pallas_call(body, name=name, grid=(S // tm,), in_specs=in_specs + [ANY] * nr,
                         out_specs=out_specs + [ANY] * nr, out_shape=out_shape + rider.out_shape,
                         scratch_shapes=rider.scratch(),
                         compiler_params=pltpu.CompilerParams(dimension_semantics=("arbitrary",),
                                                              vmem_limit_bytes=VMEM_LIMIT, has_side_effects=True))(
        *args, *rider.arrs)
    return (*out[:n_out], list(out[n_out:]))


def _rope(x, c, s1, s2):
    return x * c + pltpu.roll(x, 112, axis=1) * s1 + pltpu.roll(x, 16, axis=1) * s2


def _rope_t(dy, c, s1, s2):
    return dy * c + pltpu.roll(dy * s1, 16, axis=1) + pltpu.roll(dy * s2, 112, axis=1)


def _mla_prep(proj, gq3, gkv3, wuq, wk, wv, tabs, l, name):
    S = proj.shape[0]
    tm = min(512, S)
    tc, ts1, ts2 = tabs

    def body(cq_ref, ckv_ref, kr_ref, gq_ref, gkv_ref, wuq_ref, wk_ref, wv_ref, c_ref, s1_ref, s2_ref,
             q_ref, k_ref, v_ref, kt_ref, vt_ref):
        c, s1, s2 = c_ref[...], s1_ref[...], s2_ref[...]
        cq = cq_ref[...]
        rq = lax.rsqrt(jnp.mean(cq * cq, axis=-1, keepdims=True) + EPS)
        qa = _dot((cq * rq * gq_ref[...]).astype(BF16), wuq_ref[...])
        ckv = ckv_ref[...]
        rkv = lax.rsqrt(jnp.mean(ckv * ckv, axis=-1, keepdims=True) + EPS)
        ckvn = (ckv * rkv * gkv_ref[...]).astype(BF16)
        ka = _dot(ckvn, wk_ref[...])
        va = _dot(ckvn, wv_ref[...])
        v_ref[...] = va.astype(BF16)
        vt_ref[...] = va.T.astype(BF16)
        krr = _rope(kr_ref[...], c, s1, s2)
        for h in range(HEADS):
            sl = slice(h * 128, (h + 1) * 128)
            q_ref[:, sl] = (_rope(qa[:, sl], c, s1, s2) * QK_SCALE).astype(BF16)
            kh = ka[:, sl] + krr
            k_ref[:, sl] = kh.astype(BF16)
            kt_ref[sl, :] = kh.T.astype(BF16)

    lay = lambda a, b: BS((None, a, b), lambda i: (l, 0, 0))
    tab = BS((tm, 128), lambda i: (i, 0))
    return _pc(body, name=name, grid=(S // tm,),
               in_specs=[BS((tm, 256), lambda i: (i, 0)), BS((tm, 128), lambda i: (i, 2)), BS((tm, 128), lambda i: (i, 3)),
                         lay(1, 256), lay(1, 128), lay(256, 512), lay(128, 512), lay(128, 512), tab, tab, tab],
               out_specs=[BS((tm, 512), lambda i: (i, 0))] * 3 + [BS((512, tm), lambda i: (0, i))] * 2,
               out_shape=[SDS((S, 512), BF16)] * 3 + [SDS((512, S), BF16)] * 2)(
        proj, proj, proj, gq3, gkv3, wuq, wk, wv, tc, ts1, ts2)


def _mla_prep_bwd(dq, dk, dv, proj, gq3, gkv3, wuq, wk, wv, tabs, l, name):
    S = proj.shape[0]
    tm = min(512, S)
    tc, ts1, ts2 = tabs

    def body(dq_ref, dk_ref, dv_ref, cq_ref, ckv_ref, gq_ref, gkv_ref, wuq_ref, wk_ref, wv_ref, c_ref, s1_ref, s2_ref,
             dcq_ref, dckv_ref, dkr_ref, dwuq_ref, dwk_ref, dwv_ref, dgq_ref, dgkv_ref):
        @pl.when(pl.program_id(0) == 0)
        def _():
            for r in (dwuq_ref, dwk_ref, dwv_ref, dgq_ref, dgkv_ref):
                r[...] = jnp.zeros_like(r)

        c, s1, s2 = c_ref[...], s1_ref[...], s2_ref[...]
        dqp = jnp.concatenate(
            [_rope_t(dq_ref[h * 128:(h + 1) * 128, :].T * QK_SCALE, c, s1, s2) for h in range(HEADS)], axis=1).astype(BF16)
        cq = cq_ref[...]
        rq = lax.rsqrt(jnp.mean(cq * cq, axis=-1, keepdims=True) + EPS)
        cqh = cq * rq
        gq_v = gq_ref[...]
        dwuq_ref[...] += _dot_tn((cqh * gq_v).astype(BF16), dqp)
        dcqn = _dot_nt(dqp, wuq_ref[...])
        dgq_ref[...] += jnp.sum(dcqn * cqh, axis=0, keepdims=True)
        dxh = dcqn * gq_v
        dcq_ref[...] = (rq * (dxh - cqh * jnp.mean(dxh * cqh, axis=-1, keepdims=True))).astype(BF16)

        dkb = dk_ref[...].astype(BF16)
        dvb = dv_ref[...].astype(BF16)
        ckv = ckv_ref[...]
        rkv = lax.rsqrt(jnp.mean(ckv * ckv, axis=-1, keepdims=True) + EPS)
        ckh = ckv * rkv
        gkv_v = gkv_ref[...]
        ckvn = (ckh * gkv_v).astype(BF16)
        dwk_ref[...] += _dot_tn(ckvn, dkb)
        dwv_ref[...] += _dot_tn(ckvn, dvb)
        dckvn = _dot_nt(dkb, wk_ref[...]) + _dot_nt(dvb, wv_ref[...])
        dgkv_ref[...] += jnp.sum(dckvn * ckh, axis=0, keepdims=True)
        dyh = dckvn * gkv_v
        dckv_ref[...] = (rkv * (dyh - ckh * jnp.mean(dyh * ckh, axis=-1, keepdims=True))).astype(BF16)
        dks = dk_ref[:, 0:128] + dk_ref[:, 128:256] + dk_ref[:, 256:384] + dk_ref[:, 384:512]
        dkr_ref[...] = _rope_t(dks, c, s1, s2).astype(BF16)

    full = lambda a, b: BS((a, b), lambda i: (0, 0))
    lay = lambda a, b: BS((None, a, b), lambda i: (l, 0, 0))
    tab = BS((tm, 128), lambda i: (i, 0))
    row = lambda w: BS((tm, w), lambda i: (i, 0))
    return _pc(body, name=name, grid=(S // tm,),
               in_specs=[BS((512, tm), lambda i: (0, i)), row(512), row(512), BS((tm, 256), lambda i: (i, 0)),
                         BS((tm, 128), lambda i: (i, 2)),
                         lay(1, 256), lay(1, 128), lay(256, 512), lay(128, 512), lay(128, 512), tab, tab, tab],
               out_specs=[row(256), row(128), row(128), full(256, 512), full(128, 512), full(128, 512),
                          full(8, 256), full(8, 128)],
               out_shape=[SDS((S, 256), BF16), SDS((S, 128), BF16), SDS((S, 128), BF16), SDS((256, 512), F32),
                          SDS((128, 512), F32), SDS((128, 512), F32), SDS((8, 256), F32), SDS((8, 128), F32)])(
        dq, dk, dv, proj, proj, gq3, gkv3, wuq, wk, wv, tc, ts1, ts2)


def _causal_steps(n, q_outer):
    if q_outer:
        pairs = [(i, j) for i in range(n) for j in range(i + 1)]
    else:
        pairs = [(i, j) for j in range(n) for i in range(j, n)]
    return jnp.asarray([p[0] for p in pairs], jnp.int32), jnp.asarray([p[1] for p in pairs], jnp.int32)


def _mla_attn(q, k, vt, gts, layer, name):
    S = q.shape[0]
    t = min(512, S)
    n = S // t
    ng = len(gts)

    qi, kj = _causal_steps(n, True)
    last = qi.shape[0] - 1

    def body(qi_ref, kj_ref, q_ref, k_ref, vt_ref, *rest):
        (ya_ref, lse_ref), g_refs = rest[ng:ng + 2], rest[ng + 2:2 * ng + 2]
        m_sc, l_sc, acc_sc = rest[2 * ng + 2:2 * ng + 5]
        i, j = qi_ref[pl.program_id(1)], kj_ref[pl.program_id(1)]
        if ng:
            phases = _gather_phases(g_refs, [g.shape for g in gts], rest[2 * ng + 5], rest[2 * ng + 6], layer)
            for ph, (pp, ss) in zip(phases[:3], ((0, 0), (1, 0), (1, 2 * last // 3))):
                pl.when((pl.program_id(0) == pp) & (pl.program_id(1) == ss))(ph)

        @pl.when(j == 0)
        def _():
            m_sc[...] = jnp.full_like(m_sc, NEG_INF)
            l_sc[...] = jnp.zeros_like(l_sc)
            acc_sc[...] = jnp.zeros_like(acc_sc)

        def step(masked):
            hq = min(256, t)
            for hh in range(2):
                sl = slice(hh * 128, (hh + 1) * 128)
                for q0 in range(0, t, hq):
                    qs = slice(q0, q0 + hq)
                    st = _dot_nt(k_ref[:, sl], q_ref[qs, sl])
                    if masked:
                        key = lax.broadcasted_iota(jnp.int32, (t, hq), 0)
                        qry = lax.broadcasted_iota(jnp.int32, (t, hq), 1) + q0
                        st = jnp.where(key <= qry, st, NEG_INF)
                    m_prev = m_sc[hh, :, qs]
                    m_new = jnp.maximum(m_prev, jnp.max(st, axis=0, keepdims=True))
                    p = jnp.exp(st - m_new)
                    alpha = jnp.exp(m_prev - m_new)
                    l_sc[hh, :, qs] = alpha * l_sc[hh, :, qs] + jnp.sum(p, axis=0, keepdims=True)
                    acc_sc[hh, :, qs] = alpha * acc_sc[hh, :, qs] + _dot(vt_ref[sl, :], p.astype(BF16))
                    m_sc[hh, :, qs] = m_new

        @pl.when(j < i)
        def _():
            step(False)

        @pl.when(j == i)
        def _():
            step(True)
            ya_ref[...] = (acc_sc[0] / l_sc[0] + acc_sc[1] / l_sc[1]).T
            for hh in range(2):
                lse_ref[hh] = m_sc[hh] + jnp.log(l_sc[hh])

        if ng:
            pl.when((pl.program_id(0) == 1) & (pl.program_id(1) == last))(phases[3])

    gs = pltpu.PrefetchScalarGridSpec(
        num_scalar_prefetch=2, grid=(2, qi.shape[0]),
        in_specs=[BS((t, 256), lambda p, s, qi, kj: (qi[s], p)), BS((t, 256), lambda p, s, qi, kj: (kj[s], p)),
                  BS((256, t), lambda p, s, qi, kj: (p, kj[s]))] + [ANY] * ng,
        out_specs=[BS((t, 128), lambda p, s, qi, kj: (qi[s], p)), BS((2, 1, t), lambda p, s, qi, kj: (p, 0, qi[s]))]
        + [ANY] * ng,
        scratch_shapes=[pltpu.VMEM((2, 1, t), F32), pltpu.VMEM((2, 1, t), F32), pltpu.VMEM((2, 128, t), F32)]
        + ([pltpu.SemaphoreType.DMA((7 * ng,)), pltpu.SemaphoreType.DMA((7 * ng,))] if ng else []))
    out = pl.pallas_call(body, name=name, grid_spec=gs,
                         out_shape=[SDS((S, 256), F32), SDS((HEADS, 1, S), F32)] + [SDS(g.shape, g.dtype) for g in gts],
                         input_output_aliases={5 + m: 2 + m for m in range(ng)},
                         compiler_params=pltpu.CompilerParams(dimension_semantics=("arbitrary",) * 2,
                                                              vmem_limit_bytes=VMEM_LIMIT, has_side_effects=bool(ng)))(
        qi, kj, q, k, vt, *gts)
    return out[0], out[1], list(out[2:])


def _mla_delta(dycat, ya, name, after=None):
    S = ya.shape[0]
    t = min(512, S)

    def body(do_ref, ya_ref, *refs):
        d_ref = refs[-1]
        prod = do_ref[...] * ya_ref[...]
        for p in range(2):
            pt = prod[:, p * 128:(p + 1) * 128].T
            d_ref[2 * p] = jnp.sum(pt[0:64, :], axis=0, keepdims=True)
            d_ref[2 * p + 1] = jnp.sum(pt[64:128, :], axis=0, keepdims=True)

    extra = [] if after is None else [after]
    return _pc(body, name=name, grid=(S // t,),
               in_specs=[BS((t, 256), lambda i: (i, 0)), BS((t, 256), lambda i: (i, 0))]
               + [BS((8, 128), lambda i: (0, 0)) for _ in extra],
               out_specs=BS((HEADS, 1, t), lambda i: (0, 0, i)), out_shape=SDS((HEADS, 1, S), F32))(dycat, ya, *extra)


def _mla_attn_bwd(q, k, kt, v, dya, lse, delta, rider, name):
    S = q.shape[0]
    t = min(512, S)
    n = S // t
    nr = rider.n if rider else 0

    qi, kj = _causal_steps(n, False)
    last = qi.shape[0] - 1

    def body(qi_ref, kj_ref, q_ref, k_ref, kt_ref, v_ref, do_ref, lse_ref, delta_ref, *rest):
        dqt_ref, dk_ref, dv_ref = rest[nr:nr + 3]
        r_io = (rest[:nr], rest[nr + 3:2 * nr + 3], rest[2 * nr + 3:])
        i, j = qi_ref[pl.program_id(1)], kj_ref[pl.program_id(1)]
        if rider:
            pl.when((pl.program_id(0) == 0) & (pl.program_id(1) == 0))(lambda: rider.start(*r_io))

        @pl.when(pl.program_id(1) == 0)
        def _():
            dqt_ref[...] = jnp.zeros_like(dqt_ref)

        @pl.when(i == j)
        def _():
            dk_ref[...] = jnp.zeros_like(dk_ref)
            dv_ref[...] = jnp.zeros_like(dv_ref)

        def step(masked):
            dob = do_ref[...].astype(BF16)
            cols = pl.ds(pl.multiple_of(i * t, t), t)
            for hh in range(2):
                sl = slice(hh * 128, (hh + 1) * 128)
                qv = q_ref[:, sl]
                p = jnp.exp(_dot_nt(k_ref[:, sl], qv) - lse_ref[hh])
                if masked:
                    key = lax.broadcasted_iota(jnp.int32, (t, t), 0)
                    qry = lax.broadcasted_iota(jnp.int32, (t, t), 1)
                    p = jnp.where(key <= qry, p, 0.0)
                dv_ref[:, sl] += _dot(p.astype(BF16), dob)
                ds = (p * (_dot_nt(v_ref[:, sl], dob) - delta_ref[hh])).astype(BF16)
                dk_ref[:, sl] += _dot(ds, qv)
                dqt_ref[sl, cols] += _dot(kt_ref[sl, :], ds)

        @pl.when(i > j)
        def _():
            step(False)

        @pl.when(i == j)
        def _():
            step(True)

        if rider:
            pl.when((pl.program_id(0) == 1) & (pl.program_id(1) == last))(lambda: rider.wait(*r_io))

    qs = BS((t, 256), lambda p, s, qi, kj: (qi[s], p))
    ks = BS((t, 256), lambda p, s, qi, kj: (kj[s], p))
    rowv = BS((2, 1, t), lambda p, s, qi, kj: (p, 0, qi[s]))
    gs = pltpu.PrefetchScalarGridSpec(
        num_scalar_prefetch=2, grid=(2, qi.shape[0]),
        in_specs=[qs, ks, BS((256, t), lambda p, s, qi, kj: (p, kj[s])), ks,
                  BS((t, 128), lambda p, s, qi, kj: (qi[s], p)), rowv, rowv] + [ANY] * nr,
        out_specs=[BS((256, S), lambda p, s, qi, kj: (p, 0)), ks, ks] + [ANY] * nr,
        scratch_shapes=rider.scratch() if rider else [])
    out = pl.pallas_call(body, name=name, grid_spec=gs,
                         out_shape=[SDS((512, S), F32), SDS((S, 512), F32), SDS((S, 512), F32)]
                         + (rider.out_shape if rider else []),
                         compiler_params=pltpu.CompilerParams(dimension_semantics=("arbitrary",) * 2,
                                                              vmem_limit_bytes=VMEM_LIMIT, has_side_effects=bool(rider)))(
        qi, kj, q, k, kt, v, dya, lse, delta, *(rider.arrs if rider else []))
    return out[0], out[1], out[2], list(out[3:])


def _swa_scores(qm, kk, valid, bias, sink):
    sc = jnp.where(valid, _dot_nt(qm, kk) * SWA_SCALE + bias, NEG_INF)
    m = jnp.maximum(jnp.max(sc, axis=-1, keepdims=True), sink)
    e = jnp.exp(sc - m)
    esink = jnp.exp(sink - m)
    den = jnp.sum(e, axis=-1, keepdims=True) + esink
    return e / den, esink / den


def _swa_consts(sink_ref, l):
    rows = HEADS * BLK
    r = lax.broadcasted_iota(jnp.int32, (rows, 2 * BLK), 0)
    c = lax.broadcasted_iota(jnp.int32, (rows, 2 * BLK), 1)
    dist = (r & (BLK - 1)) + BLK - c
    head = lax.broadcasted_iota(jnp.int32, (rows, 1), 0) // BLK

    def per_head(vals):
        return jnp.where(head == 0, vals[0], jnp.where(head == 1, vals[1], jnp.where(head == 2, vals[2], vals[3])))

    bias = -per_head(SLOPES) * dist.astype(F32)
    sink = per_head([sink_ref[l, h] for h in range(HEADS)])
    return (dist >= 0) & (dist < SWA_WINDOW), c >= BLK, bias, sink, head


def _to_half(xb, pos, b):
    return xb if pos == b else pltpu.roll(xb, 64, axis=1)


def _swa_stack(ref, st, lo):
    parts = []
    for b in range(2):
        xb = ref[pl.ds(st, BLK), b * 128:(b + 1) * 128]
        half = lo if b == 0 else ~lo
        parts += [jnp.where(half, _to_half(xb, pos, b), 0.0).astype(BF16) for pos in range(2)]
    return jnp.concatenate(parts, axis=0)


def _swa_unstack(x_all, lo):
    blocks = []
    for b in range(2):
        h0, h1 = (_to_half(x_all[(2 * b + pos) * BLK:(2 * b + pos + 1) * BLK], pos, b) for pos in range(2))
        blocks.append(jnp.where(lo, h0, h1))
    return blocks


def _swa(proj, sinks, l, name):
    S = proj.shape[0]
    nb = S // BLK

    def body(q_ref, k_ref, v_ref, sink_ref, o_ref, kp, vp):
        kp[0:BLK, :] = jnp.zeros((BLK, 128), BF16)
        vp[0:BLK, :] = jnp.zeros((BLK, 128), BF16)
        kp[BLK:, :] = k_ref[...].astype(BF16)
        vp[BLK:, :] = v_ref[...].astype(BF16)
        lo = lax.broadcasted_iota(jnp.int32, (BLK, 128), 1) < 64
        band, cur, bias, sink, _ = _swa_consts(sink_ref, l)

        def blk(i, carry):
            st = pl.multiple_of(i * BLK, BLK)
            kk = kp[pl.ds(st, 2 * BLK), :]
            vv = vp[pl.ds(st, 2 * BLK), :]
            p, _ = _swa_scores(_swa_stack(q_ref, st, lo), kk, band & (cur | (i > 0)), bias, sink)
            for b, ob in enumerate(_swa_unstack(_dot(p.astype(BF16), vv), lo)):
                o_ref[pl.ds(st, BLK), b * 128:(b + 1) * 128] = ob
            return carry

        lax.fori_loop(0, nb, blk, 0, unroll=2)

    return _pc(body, name=name, grid=(1,),
               in_specs=[BS((S, 256), lambda i: (0, C_QS // 256)), BS((S, 128), lambda i: (0, C_KS // 128)),
                         BS((S, 128), lambda i: (0, C_VS // 128)), BS(memory_space=pltpu.SMEM)],
               out_specs=BS((S, 256), lambda i: (0, 0)),
               out_shape=SDS((S, 256), F32),
               scratch=[pltpu.VMEM((S + BLK, 128), BF16), pltpu.VMEM((S + BLK, 128), BF16)])(proj, proj, proj, sinks)


def _swa_bwd(proj, sinks, dyd, l, name, rider=None):
    S = proj.shape[0]
    nb = S // BLK
    nr = rider.n if rider else 0

    def body(q_ref, k_ref, v_ref, sink_ref, do_ref, *rest):
        dq_ref, dk_ref, dv_ref, dsink_ref = rest[nr:nr + 4]
        kp, vp, dkp, dvp = rest[2 * nr + 4:2 * nr + 8]
        r_io = (rest[:nr], rest[nr + 4:2 * nr + 4], rest[2 * nr + 8:])
        if rider:
            rider.start(*r_io)
        kp[0:BLK, :] = jnp.zeros((BLK, 128), BF16)
        vp[0:BLK, :] = jnp.zeros((BLK, 128), BF16)
        kp[BLK:, :] = k_ref[...].astype(BF16)
        vp[BLK:, :] = v_ref[...].astype(BF16)
        dkp[...] = jnp.zeros_like(dkp)
        dvp[...] = jnp.zeros_like(dvp)
        lo = lax.broadcasted_iota(jnp.int32, (BLK, 128), 1) < 64
        lane8 = lax.broadcasted_iota(jnp.int32, (8, 128), 1)
        band, cur, bias, sink, head = _swa_consts(sink_ref, l)

        def blk(i, dsink):
            st = pl.multiple_of(i * BLK, BLK)
            kk = kp[pl.ds(st, 2 * BLK), :]
            vv = vp[pl.ds(st, 2 * BLK), :]
            qm, dom = _swa_stack(q_ref, st, lo), _swa_stack(do_ref, st, lo)
            p, psink = _swa_scores(qm, kk, band & (cur | (i > 0)), bias, sink)
            dp = _dot_nt(dom, vv)
            dvp[pl.ds(st, 2 * BLK), :] += _dot_tn(p.astype(BF16), dom)
            delta = jnp.sum(p * dp, axis=-1, keepdims=True)
            dsk = -psink * delta
            for h in range(HEADS):
                dsink = dsink + jnp.where(lane8 == h, jnp.sum(jnp.where(head == h, dsk, 0.0)), 0.0)
            dsc = (p * (dp - delta) * SWA_SCALE).astype(BF16)
            for b, dqb in enumerate(_swa_unstack(_dot(dsc, kk), lo)):
                dq_ref[pl.ds(st, BLK), b * 128:(b + 1) * 128] = dqb.astype(BF16)
            dkp[pl.ds(st, 2 * BLK), :] += _dot_tn(dsc, qm)
            return dsink

        dsink_ref[...] = lax.fori_loop(0, nb, blk, jnp.zeros((8, 128), F32), unroll=2)
        dk_ref[...] = dkp[BLK:, :].astype(BF16)
        dv_ref[...] = dvp[BLK:, :].astype(BF16)
        if rider:
            rider.wait(*r_io)

    in_specs = [BS((S, 256), lambda i: (0, C_QS // 256)), BS((S, 128), lambda i: (0, C_KS // 128)),
                BS((S, 128), lambda i: (0, C_VS // 128)), BS(memory_space=pltpu.SMEM), BS((S, 256), lambda i: (0, 3))]
    out_specs = [BS((S, 256), lambda i: (0, 0)), BS((S, 128), lambda i: (0, 0)), BS((S, 128), lambda i: (0, 0)),
                 BS((8, 128), lambda i: (0, 0))]
    out_shape = [SDS((S, 256), BF16), SDS((S, 128), BF16), SDS((S, 128), BF16), SDS((8, 128), F32)]
    scratch = [pltpu.VMEM((S + BLK, 128), BF16), pltpu.VMEM((S + BLK, 128), BF16),
               pltpu.VMEM((S + BLK, 128), F32), pltpu.VMEM((S + BLK, 128), F32)]
    if not rider:
        return _pc(body, name=name, grid=(1,), in_specs=in_specs, out_specs=out_specs, out_shape=out_shape,
                   scratch=scratch)(proj, proj, proj, sinks, dyd)
    out = pl.pallas_call(body, name=name, grid=(1,), in_specs=in_specs + [ANY] * nr, out_specs=out_specs + [ANY] * nr,
                         out_shape=out_shape + rider.out_shape, scratch_shapes=scratch + rider.scratch(),
                         compiler_params=pltpu.CompilerParams(dimension_semantics=("arbitrary",),
                                                              vmem_limit_bytes=VMEM_LIMIT, has_side_effects=True))(
        proj, proj, proj, sinks, dyd, *rider.arrs)
    return (*out[:4], list(out[4:]))


def _down(x, k, t):
    return jnp.where(t >= k, pltpu.roll(x, k, axis=0), 0.0)


def _up(x, k, t):
    n = x.shape[0]
    return jnp.where(t < n - k, pltpu.roll(x, n - k, axis=0), 0.0)


def _conv(proj, w8, l, name):
    S = proj.shape[0]

    def body(gb_ref, gc_ref, u_ref, w_ref, y_ref):
        t = lax.broadcasted_iota(jnp.int32, (S, 128), 0)
        z = gc_ref[...] * u_ref[...]
        c = w_ref[2:3, :] * z + w_ref[1:2, :] * _down(z, 1, t) + w_ref[0:1, :] * _down(z, 2, t)
        y_ref[...] = gb_ref[...] * c

    col = lambda c0: BS((S, 128), lambda i: (0, c0 // 128 + i))
    return _pc(body, name=name, grid=(2,),
               in_specs=[col(C_GB), col(C_GC), col(C_UC), BS((None, 8, 128), lambda i: (l, 0, i))],
               out_specs=BS((S, 128), lambda i: (0, i)), out_shape=SDS((S, 256), F32))(proj, proj, proj, w8)


def _conv_bwd(proj, w8, dycat, l, name):
    S = proj.shape[0]

    def body(gb_ref, gc_ref, u_ref, w_ref, dy_ref, dgb_ref, dgc_ref, du_ref, dw_ref):
        t = lax.broadcasted_iota(jnp.int32, (S, 128), 0)
        gc, u = gc_ref[...], u_ref[...]
        z = gc * u
        z1, z2 = _down(z, 1, t), _down(z, 2, t)
        w0, w1, w2 = w_ref[0:1, :], w_ref[1:2, :], w_ref[2:3, :]
        dy = dy_ref[...]
        dgb_ref[...] = (dy * (w2 * z + w1 * z1 + w0 * z2)).astype(BF16)
        dc = dy * gb_ref[...]
        dz = w2 * dc + w1 * _up(dc, 1, t) + w0 * _up(dc, 2, t)
        dgc_ref[...] = (dz * u).astype(BF16)
        du_ref[...] = (dz * gc).astype(BF16)
        row = lax.broadcasted_iota(jnp.int32, (8, 128), 0)
        sums = [jnp.sum(dc * zz, axis=0, keepdims=True) for zz in (z2, z1, z)]
        dw_ref[...] = jnp.where(row == 0, sums[0], jnp.where(row == 1, sums[1], jnp.where(row == 2, sums[2], 0.0)))

    col = lambda c0: BS((S, 128), lambda i: (0, c0 // 128 + i))
    out = BS((S, 128), lambda i: (0, i))
    return _pc(body, name=name, grid=(2,),
               in_specs=[col(C_GB), col(C_GC), col(C_UC), BS((None, 8, 128), lambda i: (l, 0, i)), col(256)],
               out_specs=[out, out, out, BS((8, 128), lambda i: (0, i))],
               out_shape=[SDS((S, 256), BF16)] * 3 + [SDS((8, 256), F32)])(proj, proj, proj, w8, dycat)


def _pool_parts(u, t, first):
    lo = lax.broadcasted_iota(jnp.int32, u.shape, 1) < 64
    s2 = u + _down(u, 1, t)
    s4 = s2 + _down(s2, 2, t)
    s8 = s4 + _down(s4, 4, t)
    s16 = s8 + _down(s8, 8, t)
    win = jnp.where(lo, jnp.where(first, s2, s8), jnp.where(first, s4, s16))
    wv = jnp.where(lo, jnp.where(first, 2, 8), jnp.where(first, 4, 16))
    cnt = jnp.minimum(t + 1, wv).astype(F32)
    return win, cnt, lo


def _pool(proj, pwd, scale3, l, name):
    S = proj.shape[0]

    def body(u_ref, pw_ref, sc_ref, y_ref):
        t = lax.broadcasted_iota(jnp.int32, (S, 128), 0)
        u = u_ref[...]
        win, cnt, _ = _pool_parts(u, t, pl.program_id(0) == 0)
        pooled = win / cnt - u
        y_ref[...] = _dot(pooled.astype(BF16), pw_ref[...]) * sc_ref[...]

    return _pc(body, name=name, grid=(2,),
               in_specs=[BS((S, 128), lambda i: (0, C_UP // 128 + i)), BS((None, 128, 128), lambda i: (l, i, 0)),
                         BS((None, 1, 128), lambda i: (l, 0, i))],
               out_specs=BS((S, 128), lambda i: (0, i)), out_shape=SDS((S, 256), F32))(proj, pwd, scale3)


def _pool_bwd(proj, pwd, scale3, dycat, l, name):
    S = proj.shape[0]

    def body(u_ref, pw_ref, sc_ref, dy_ref, du_ref, dpw_ref, dsc_ref):
        t = lax.broadcasted_iota(jnp.int32, (S, 128), 0)
        first = pl.program_id(0) == 0
        u = u_ref[...]
        win, cnt, lo = _pool_parts(u, t, first)
        pooled = (win / cnt - u).astype(BF16)
        pw = pw_ref[...]
        dy = dy_ref[...]
        dsc_ref[...] = jnp.broadcast_to(jnp.sum(dy * _dot(pooled, pw), axis=0, keepdims=True), (8, 128))
        dmb = (dy * sc_ref[...]).astype(BF16)
        dpw_ref[...] = _dot_tn(pooled, dmb)
        dpooled = _dot_nt(dmb, pw)
        a1 = dpooled / cnt
        a2 = a1 + _up(a1, 1, t)
        a4 = a2 + _up(a2, 2, t)
        a8 = a4 + _up(a4, 4, t)
        a16 = a8 + _up(a8, 8, t)
        dwin = jnp.where(lo, jnp.where(first, a2, a8), jnp.where(first, a4, a16))
        du_ref[...] = (dwin - dpooled).astype(BF16)

    return _pc(body, name=name, grid=(2,),
               in_specs=[BS((S, 128), lambda i: (0, C_UP // 128 + i)), BS((None, 128, 128), lambda i: (l, i, 0)),
                         BS((None, 1, 128), lambda i: (l, 0, i)), BS((S, 128), lambda i: (0, 4 + i))],
               out_specs=[BS((S, 128), lambda i: (0, i)), BS((128, 128), lambda i: (i, 0)), BS((8, 128), lambda i: (0, i))],
               out_shape=[SDS((S, 256), BF16), SDS((256, 128), F32), SDS((8, 256), F32)])(proj, pwd, scale3, dycat)


def _adamw(w, g, m, v, name, echo=False):
    n, a, b = w.shape
    tr = _row_tile(a, b)

    def body(w_ref, g_ref, m_ref, v_ref, d_ref, nm_ref, nv_ref, *g_out):
        gv = g_ref[...]
        if echo:
            g_out[0][...] = gv
        m_new = B1 * m_ref[...] + (1.0 - B1) * gv
        v_new = B2 * v_ref[...] + (1.0 - B2) * (gv * gv)
        m_hat = m_new / (1.0 - B1 ** STEP)
        v_hat = v_new / (1.0 - B2 ** STEP)
        d_ref[...] = -LR * (m_hat / (jnp.sqrt(v_hat) + ADAM_EPS) + WD * w_ref[...])
        nm_ref[...] = m_new
        nv_ref[...] = v_new

    sp = BS((None, tr, b), lambda i, t: (i, t, 0))
    return _pc(body, name=name, grid=(n, a // tr), in_specs=[sp] * 4, out_specs=[sp] * (3 + echo),
               out_shape=[SDS((n, a, b), F32)] * (3 + echo))(w, g, m, v)


def _prefetch_call(body, name, grid, in_specs, out_specs, out_shape):
    gs = pltpu.PrefetchScalarGridSpec(num_scalar_prefetch=1, grid=grid, in_specs=in_specs, out_specs=out_specs)
    return pl.pallas_call(body, name=name, grid_spec=gs, out_shape=out_shape, compiler_params=_params(len(grid)))


def _place(w, kc, dtype, name):
    _, a, b = w.shape

    def body(kc_ref, w_ref, o_ref):
        o_ref[...] = w_ref[...].astype(dtype)

    return _prefetch_call(body, name, (2,), [BS((None, a, b), lambda l, kc: (l, 0, 0))],
                          BS((None, None, a, b), lambda l, kc: (l, kc[0], 0, 0)), SDS((2, 4, a, b), dtype))(kc, w)


def _pair_sum(g, got, kc, name):
    _, _, a, b = g.shape
    tr = _row_tile(a, b)

    def body(kc_ref, a_ref, b_ref, t32_ref, t16_ref):
        s = a_ref[...] + b_ref[...]
        t16_ref[...] = s.astype(BF16)

        @pl.when(pl.program_id(1) == kc_ref[0])
        def _():
            t32_ref[...] = s

    sp = BS((None, tr, b), lambda t, k, kc: (k, t, 0))
    return _prefetch_call(body, name, (a // tr, 4),
                          [BS((None, None, tr, b), lambda t, k, kc: (kc[1], k, t, 0)), sp],
                          [BS((tr, b), lambda t, k, kc: (t, 0)), sp],
                          [SDS((a, b), F32), SDS((4, a, b), BF16)])(kc, g, got)


def _chip_sum(t32, gots, kc, name, after=None):
    a, b = t32.shape
    tr = _row_tile(a, b)
    ng = len(gots)

    def body(kc_ref, a_ref, *refs):
        acc = a_ref[...]
        for g_ref in refs[:ng]:
            for i in range(g_ref.shape[0]):
                acc = acc + g_ref[i].astype(F32)
        refs[-1][...] = acc

    extra = [] if after is None else [after]
    return _prefetch_call(body, name, (a // tr,),
                          [BS((tr, b), lambda t, kc: (t, 0))]
                          + [BS((g.shape[0], tr, b), lambda t, kc: (0, t, 0)) for g in gots]
                          + [BS((8, 128), lambda t, kc: (0, 0)) for _ in extra],
                          BS((None, tr, b), lambda t, kc: (kc[1], t, 0)), SDS((2, a, b), F32))(kc, t32, *gots, *extra)


def _me():
    return lax.axis_index("x"), lax.axis_index("y"), lax.axis_index("c")


def _other_chips(x, y):
    return [(1 - x, y), (x, 1 - y), (1 - x, 1 - y)]


ANY = BS(memory_space=pl.ANY)
COMM_PARAMS = pltpu.CompilerParams(has_side_effects=True)


def _gather(arrs, name):
    n = len(arrs)

    def body(*refs):
        for phase in _gather_phases(refs[n:2 * n], [a.shape for a in arrs], refs[2 * n], refs[2 * n + 1]):
            phase()

    return pl.pallas_call(body, name=name, out_shape=[SDS(a.shape, a.dtype) for a in arrs],
                          in_specs=[ANY] * n, out_specs=[ANY] * n, input_output_aliases={t: t for t in range(n)},
                          scratch_shapes=[pltpu.SemaphoreType.DMA((7 * n,)), pltpu.SemaphoreType.DMA((7 * n,))],
                          compiler_params=COMM_PARAMS)(*arrs)


def _gather_phases(outs, shapes, send_sems, recv_sems, layer=None):
    n = len(outs)
    cut = [s[2] // 2 // 16 * 16 for s in shapes]
    split = [r > 0 for r in cut]

    def plan():
        x, y, c = _me()
        return (c if layer is None else layer), (x, y), (x, y, c), (x, y, 1 - c), _other_chips(x, y)

    def role(moving, fn):
        if layer is None:
            fn()
        else:
            c = lax.axis_index("c")
            pl.when((c == layer) if moving else (c != layer))(fn)

    def blk(t, chip, layer, half=None):
        r = outs[t].at[layer, 2 * chip[0] + chip[1]]
        if half is None:
            return r
        return r.at[pl.ds(0, cut[t])] if half == 0 else r.at[pl.ds(cut[t], shapes[t][2] - cut[t])]

    def copy(t, k, ref, to):
        return pltpu.make_async_remote_copy(src_ref=ref, dst_ref=ref, send_sem=send_sems.at[7 * t + k],
                                            recv_sem=recv_sems.at[7 * t + k], device_id=to, device_id_type=MESH)

    def own_sends(t):
        c, chip, me, sib, (xn, yn, dg) = plan()
        cps = [copy(t, 0, blk(t, chip, c), (*xn, c)), copy(t, 1, blk(t, chip, c), (*yn, c))]
        return cps if split[t] else cps + [copy(t, 2, blk(t, chip, c), (*dg, c))]

    def relays(t):
        c, chip, me, sib, (xn, yn, dg) = plan()
        after_x = [copy(t, 4, blk(t, xn, c), sib)]
        after_y = [copy(t, 5, blk(t, yn, c), sib)]
        if split[t]:
            after_x.insert(0, copy(t, 2, blk(t, xn, c, 0), (*yn, c)))
            after_y.insert(0, copy(t, 3, blk(t, yn, c, 1), (*xn, c)))
        return after_x, after_y, [copy(t, 6, blk(t, dg, c), sib)]

    def send_own():
        for t in range(n):
            for cp in own_sends(t):
                cp.start()

    def relay_neighbours():
        c, chip, me, sib, (xn, yn, dg) = plan()
        for t in range(n):
            after_x, after_y, _ = relays(t)
            copy(t, 0, blk(t, xn, c), me).wait_recv()
            for cp in after_x:
                cp.start()
            copy(t, 1, blk(t, yn, c), me).wait_recv()
            for cp in after_y:
                cp.start()

    def relay_diagonal():
        c, chip, me, sib, (xn, yn, dg) = plan()
        for t in range(n):
            if split[t]:
                copy(t, 2, blk(t, dg, c, 0), me).wait_recv()
                copy(t, 3, blk(t, dg, c, 1), me).wait_recv()
            else:
                copy(t, 2, blk(t, dg, c), me).wait_recv()
            relays(t)[2][0].start()

    def take_sibling():
        _, chip, me, sib, (xn, yn, dg) = plan()
        theirs = 1 - lax.axis_index("c") if layer is None else layer
        for t in range(n):
            for k, peer in ((4, xn), (5, yn), (6, dg)):
                copy(t, k, blk(t, peer, theirs), me).wait_recv()

    def drain_sends():
        for t in range(n):
            after_x, after_y, after_d = relays(t)
            for cp in own_sends(t) + after_x + after_y + after_d:
                cp.wait_send()

    def finish():
        role(False, take_sibling)
        role(True, drain_sends)

    return ((lambda: role(True, send_own)), (lambda: role(True, relay_neighbours)),
            (lambda: role(True, relay_diagonal)), finish)


def _swap_copies(ins, outs, send_sems, recv_sems):
    x, y, c = _me()
    return [pltpu.make_async_remote_copy(src_ref=ins[t].at[1 - c], dst_ref=outs[t], send_sem=send_sems.at[t],
                                         recv_sem=recv_sems.at[t], device_id=(x, y, 1 - c), device_id_type=MESH)
            for t in range(len(ins))]


def _exchange_copies(peers, ins, outs, send_sems, recv_sems):
    x, y, c = _me()
    chips = _other_chips(x, y)
    n = len(peers)
    return [pltpu.make_async_remote_copy(src_ref=ins[t].at[2 * chips[j][0] + chips[j][1]], dst_ref=outs[t].at[i],
                                         send_sem=send_sems.at[n * t + i], recv_sem=recv_sems.at[n * t + i],
                                         device_id=(*chips[j], c), device_id_type=MESH)
            for i, j in enumerate(peers) for t in range(len(ins))]


class _Rider:
    def __init__(self, arrs, out_shape, nsem, copies):
        self.arrs, self.out_shape, self.nsem, self.copies = list(arrs), out_shape, nsem, copies
        self.n = len(self.arrs)

    def scratch(self):
        return [pltpu.SemaphoreType.DMA((self.nsem,)), pltpu.SemaphoreType.DMA((self.nsem,))]

    def start(self, ins, outs, sems):
        for cp in self.copies(ins, outs, *sems):
            cp.start()

    def wait(self, ins, outs, sems):
        for cp in self.copies(ins, outs, *sems):
            cp.wait()


def _swap_rider(gs):
    return _Rider(gs, [SDS(g.shape[1:], g.dtype) for g in gs], len(gs), _swap_copies)


def _exchange_rider(ts, peers=(0, 1, 2)):
    return _Rider(ts, [SDS((len(peers),) + t.shape[1:], t.dtype) for t in ts], len(peers) * len(ts),
                  functools.partial(_exchange_copies, peers))


HBM = BS(memory_space=pltpu.HBM)
SEM = BS(memory_space=pltpu.SEMAPHORE)
SPLIT_PARAMS = pltpu.CompilerParams(has_side_effects=pltpu.SideEffectType.DATAFLOW_SIDE_EFFECTING)


def _split_copies(swap):
    return _swap_copies if swap else functools.partial(_exchange_copies, (0, 1, 2))


def _exchange_start(ts, name, swap=False):
    n = len(ts)
    nsem = n if swap else 3 * n
    lands = [lax.empty(t.shape[1:] if swap else (3,) + t.shape[1:], t.dtype) for t in ts]

    def body(*refs):
        for cp in _split_copies(swap)(refs[:n], refs[n:2 * n], refs[2 * n], refs[2 * n + 1]):
            cp.start()
        refs[-1][...] = jnp.zeros_like(refs[-1])

    held = [pltpu.with_memory_space_constraint(a, pltpu.HBM) for a in list(ts) + lands]
    out = pl.pallas_call(
        body, name=name,
        out_shape=(pltpu.SemaphoreType.DMA((nsem,)), pltpu.SemaphoreType.DMA((nsem,)),
                   *[pltpu.HBM(a.shape, a.dtype) for a in held], SDS((8, 128), F32)),
        in_specs=[HBM] * (2 * n), out_specs=(SEM, SEM, *[HBM] * (2 * n), BS(memory_space=pltpu.VMEM)),
        input_output_aliases={i: 2 + i for i in range(2 * n)}, compiler_params=SPLIT_PARAMS)(*held)
    return out[0], out[1], list(out[2:2 + n]), list(out[2 + n:2 + 2 * n]), out[-1]


def _exchange_wait(send_sems, recv_sems, ts, lands, after, name, swap=False):
    n = len(ts)

    def body(*refs):
        for cp in _split_copies(swap)(refs[:n], refs[n:2 * n], refs[2 * n], refs[2 * n + 1]):
            cp.wait_send()
            cp.wait_recv()

    out = pl.pallas_call(
        body, name=name, out_shape=tuple(pltpu.HBM(a.shape, a.dtype) for a in ts + lands),
        in_specs=[HBM] * (2 * n) + [SEM, SEM, ANY], out_specs=[HBM] * (2 * n),
        input_output_aliases={i: i for i in range(2 * n)}, compiler_params=SPLIT_PARAMS)(
        *ts, *lands, send_sems, recv_sems, after)
    return (list(out[n:2 * n]), list(out[:n])) if swap else list(out[n:2 * n])


def _ride_alone(rider, name):
    n = rider.n

    def body(*refs):
        rider.start(refs[:n], refs[n:2 * n], refs[2 * n:])
        rider.wait(refs[:n], refs[n:2 * n], refs[2 * n:])

    return pl.pallas_call(body, name=name, out_shape=rider.out_shape, in_specs=[ANY] * n, out_specs=[ANY] * n,
                          scratch_shapes=rider.scratch(), compiler_params=COMM_PARAMS)(*rider.arrs)


def _join_layers(us, name):
    n = len(us)

    def body(*refs):
        outs, send_sems, recv_sems = refs[n:2 * n], refs[2 * n], refs[2 * n + 1]
        x, y, c = _me()
        cps = [pltpu.make_async_remote_copy(src_ref=outs[t].at[c], dst_ref=outs[t].at[c], send_sem=send_sems.at[t],
                                            recv_sem=recv_sems.at[t], device_id=(x, y, 1 - c), device_id_type=MESH)
               for t in range(n)]
        for cp in cps:
            cp.start()
        for cp in cps:
            cp.wait()

    return pl.pallas_call(body, name=name, out_shape=[SDS(u.shape, u.dtype) for u in us],
                          in_specs=[ANY] * n, out_specs=[ANY] * n, input_output_aliases={t: t for t in range(n)},
                          scratch_shapes=[pltpu.SemaphoreType.DMA((n,)), pltpu.SemaphoreType.DMA((n,))],
                          compiler_params=COMM_PARAMS)(*us)


def _allsum_small(v, name, after):
    M = v.shape[0]

    def body(x_ref, after_ref, o_ref, all_ref, send_sems, recv_sems, local_sem):
        x, y, c = _me()
        me, sib = (x, y, c), (x, y, 1 - c)
        chips = _other_chips(x, y)

        def rows(px, py, pc):
            return all_ref.at[pl.ds((4 * px + 2 * py + pc) * M, M), :]

        def copy(k, block, to, src=None):
            return pltpu.make_async_remote_copy(src_ref=rows(*block) if src is None else src, dst_ref=rows(*block),
                                                send_sem=send_sems.at[k], recv_sem=recv_sems.at[k],
                                                device_id=to, device_id_type=MESH)

        mine = pltpu.make_async_copy(x_ref, rows(*me), local_sem)
        mine.start()
        first = [copy(0, me, sib, src=x_ref)]
        first += [copy(1 + j, me, (*chip, c), src=x_ref) for j, chip in enumerate(chips)]
        for cp in first:
            cp.start()
        passed = [copy(4 + j, (*chip, c), sib) for j, chip in enumerate(chips)]
        for j, chip in enumerate(chips):
            copy(1 + j, (*chip, c), me).wait_recv()
            passed[j].start()
        copy(0, sib, me).wait_recv()
        for j, chip in enumerate(chips):
            copy(4 + j, (*chip, 1 - c), me).wait_recv()
        for cp in first + passed:
            cp.wait_send()
        mine.wait()
        acc = all_ref[0:M, :]
        for d in range(1, 8):
            acc = acc + all_ref[d * M:(d + 1) * M, :]
        o_ref[...] = acc

    vm = BS(memory_space=pltpu.VMEM)
    return pl.pallas_call(body, name=name, out_shape=SDS((M, LANES), F32), in_specs=[vm, ANY], out_specs=vm,
                          scratch_shapes=[pltpu.VMEM((8 * M, LANES), F32), pltpu.SemaphoreType.DMA((7,)),
                                          pltpu.SemaphoreType.DMA((7,)), pltpu.SemaphoreType.DMA],
                          compiler_params=pltpu.CompilerParams(has_side_effects=True, vmem_limit_bytes=VMEM_LIMIT))(
        v, after)


FFN = ("w_gate_up", "w_down")
REST = ("w_in", "w_o", "w_uq", "w_ukv")
BIG = FFN + REST
TINY = ("conv_w",)
REPL = ("attn_norm", "mla_q_norm", "mla_kv_norm", "pool_w", "pool_scale", "swa_sinks", "mix_norm", "ffn_norm",
        "final_norm")
ORDER = ("attn_norm", "w_in", "mla_q_norm", "w_uq", "mla_kv_norm", "w_ukv", "conv_w", "pool_w", "pool_scale",
         "swa_sinks", "mix_norm", "w_o", "ffn_norm", "w_gate_up", "w_down", "final_norm")


def _rows8(shape):
    return -(-int(np.prod(shape)) // (8 * LANES)) * 8


def _pack(arrs):
    parts = []
    for a in arrs:
        r = _rows8(a.shape)
        parts.append(jnp.pad(a.reshape(-1), (0, r * LANES - a.size)).reshape(r, LANES))
    return jnp.concatenate(parts, axis=0)


def _unpack(buf, shapes):
    out, r0 = [], 0
    for s in shapes:
        n, r = int(np.prod(s)), _rows8(s)
        rows = buf[r0:r0 + r]
        out.append(rows.reshape(s) if n == r * LANES else rows.reshape(-1)[:n].reshape(s))
        r0 += r
    return out


def _cols_joined(g):
    return jnp.transpose(g, (0, 2, 1, 3)).reshape(g.shape[0], g.shape[2], 4 * g.shape[3])


def _cols_split(w):
    n, a, b4 = w.shape
    return jnp.transpose(w.reshape(n, a, 4, b4 // 4), (0, 2, 1, 3))


def _rope_tables(S):
    inv = 1.0 / (10000.0 ** (jnp.arange(0, 32, 2, dtype=F32) / 32))
    ang = jnp.arange(S, dtype=F32)[:, None] * inv[None, :]
    cos, sin = jnp.cos(ang), jnp.sin(ang)
    z = lambda w: jnp.zeros((S, w), F32)
    tc = jnp.concatenate([jnp.ones((S, 64), F32), cos, cos, jnp.ones((S, 32), F32)], axis=1)
    ts1 = jnp.concatenate([z(64), -sin, z(48)], axis=1)
    ts2 = jnp.concatenate([z(80), sin, z(32)], axis=1)
    return tc, ts1, ts2


def _pad_w_in(wt):
    z = lambda n: jnp.zeros((wt.shape[0], n, wt.shape[2]), wt.dtype)
    return jnp.concatenate([wt[:, 0:384], z(64), wt[:, 384:416], z(32), wt[:, 416:1952]], axis=1)


def _unpad_w_in(dt):
    return jnp.concatenate([dt[:, 0:384], dt[:, 448:480], dt[:, 512:2048]], axis=1)


def _pad_heads(w, src, offs):
    cols = []
    for h in range(HEADS):
        src0, n = src[h]
        z = lambda k: jnp.zeros(w.shape[:-1] + (k,), w.dtype)
        cols += [z(offs[h]), w[..., src0:src0 + n], z(128 - offs[h] - n)]
    return jnp.concatenate(cols, axis=-1)


UQ_SRC = [(h * 96, 96) for h in range(HEADS)]
KN_SRC = [(h * 128, 64) for h in range(HEADS)]
V_SRC = [(h * 128 + 64, 64) for h in range(HEADS)]
ZERO_OFF = [0] * HEADS
V_OFF = [(h % 2) * 64 for h in range(HEADS)]


def _unpad_heads(d, src, offs):
    return [d[..., h * 128 + offs[h]: h * 128 + offs[h] + src[h][1]] for h in range(HEADS)]


def kernel(x, attn_norm, w_in, mla_q_norm, w_uq, mla_kv_norm, w_ukv, conv_w, pool_w, pool_scale, swa_sinks, mix_norm, w_o, ffn_norm, w_gate_up, w_down, final_norm, loss_target, m_attn_norm, m_w_in, m_mla_q_norm, m_w_uq, m_mla_kv_norm, m_w_ukv, m_conv_w, m_pool_w, m_pool_scale, m_swa_sinks, m_mix_norm, m_w_o, m_ffn_norm, m_w_gate_up, m_w_down, m_final_norm, v_attn_norm, v_w_in, v_mla_q_norm, v_w_uq, v_mla_kv_norm, v_w_ukv, v_conv_w, v_pool_w, v_pool_scale, v_swa_sinks, v_mix_norm, v_w_o, v_ffn_norm, v_w_gate_up, v_w_down, v_final_norm):
    W = dict(attn_norm=attn_norm, w_in=w_in, mla_q_norm=mla_q_norm, w_uq=w_uq, mla_kv_norm=mla_kv_norm, w_ukv=w_ukv,
             conv_w=conv_w, pool_w=pool_w, pool_scale=pool_scale, swa_sinks=swa_sinks, mix_norm=mix_norm, w_o=w_o,
             ffn_norm=ffn_norm, w_gate_up=w_gate_up, w_down=w_down, final_norm=final_norm)
    M1 = dict(attn_norm=m_attn_norm, w_in=m_w_in, mla_q_norm=m_mla_q_norm, w_uq=m_w_uq, mla_kv_norm=m_mla_kv_norm,
              w_ukv=m_w_ukv, conv_w=m_conv_w, pool_w=m_pool_w, pool_scale=m_pool_scale, swa_sinks=m_swa_sinks,
              mix_norm=m_mix_norm, w_o=m_w_o, ffn_norm=m_ffn_norm, w_gate_up=m_w_gate_up, w_down=m_w_down,
              final_norm=m_final_norm)
    V2 = dict(attn_norm=v_attn_norm, w_in=v_w_in, mla_q_norm=v_mla_q_norm, w_uq=v_w_uq, mla_kv_norm=v_mla_kv_norm,
              w_ukv=v_w_ukv, conv_w=v_conv_w, pool_w=v_pool_w, pool_scale=v_pool_scale, swa_sinks=v_swa_sinks,
              mix_norm=v_mix_norm, w_o=v_w_o, ffn_norm=v_ffn_norm, w_gate_up=v_w_gate_up, w_down=v_w_down,
              final_norm=v_final_norm)
    S = x.shape[1]
    xc, yc, cc = _me()
    chip = 2 * xc + yc
    kc = jnp.stack([chip, cc]).astype(jnp.int32)

    first, later = ("w_in", "w_uq", "w_ukv", "conv_w"), ("w_o", "w_gate_up", "w_down")
    T = lambda a: jnp.swapaxes(a, 1, 2)
    W["w_in"], M1["w_in"], V2["w_in"] = T(w_in), T(m_w_in), T(v_w_in)
    placed = {n: _place(W[n], kc, F32 if n == "conv_w" else BF16, f"place_{n}") for n in first + later}
    gi, gq, gkv, gcv = _gather([placed[n] for n in first], "gather_weights")
    later_w = [placed[n] for n in later]
    win_p = _pad_w_in(gi.reshape(2, 4 * gi.shape[2], D))
    wuq_p = _pad_heads(_cols_joined(gq), UQ_SRC, ZERO_OFF)
    wukv = _cols_joined(gkv)
    wk_p = _pad_heads(wukv, KN_SRC, ZERO_OFF)
    wv_p = _pad_heads(wukv, V_SRC, V_OFF)
    conv8 = jnp.pad(_cols_joined(gcv), ((0, 0), (0, 5), (0, 0)))
    pwd = jnp.concatenate([jnp.concatenate(
        [jnp.pad(pool_w[:, 2 * b], ((0, 0), (0, 0), (0, 64))), jnp.pad(pool_w[:, 2 * b + 1], ((0, 0), (0, 0), (64, 0)))],
        axis=1) for b in range(2)], axis=1).astype(BF16)
    tabs = _rope_tables(S)
    g_attn, g_q, g_kv, g_mix, g_ffn, g_ps = (_g3(W[n]) for n in ("attn_norm", "mla_q_norm", "mla_kv_norm", "mix_norm",
                                                                  "ffn_norm", "pool_scale"))

    xs = [x[0]]
    saved = []
    for l in range(DEPTH):
        x0 = xs[-1]
        proj, h = _norm_mm(x0, g_attn, l, win_p, _wspec_in(l), D_INP, D_INP, F32, f"in_proj{l}", w_t=True)
        q, k, v, kt, vt = _mla_prep(proj, g_q, g_kv, wuq_p, wk_p, wv_p, tabs, l, f"mla_prep{l}")
        ya, lse, later_w = _mla_attn(q, k, vt, later_w, l, f"mla_attn{l}")
        go, gu4, gd = later_w
        wo, wdown = go.reshape(2, D, D), gd.reshape(2, D_FF, D)
        yb = _conv(proj, conv8, l, f"conv{l}")
        ycp = _pool(proj, pwd, g_ps, l, f"pool{l}")
        yd = _swa(proj, swa_sinks, l, f"swa{l}")
        x1, ycat, mixed = _mix_out(x0, ya, yb, ycp, yd, g_mix, wo, l, f"mix_out{l}")
        gu, h2 = _norm_mm(x1, g_ffn, l, gu4, _wspec_gu(l), 2 * D_FF, 2 * D_FF // 4, BF16, f"gate_up{l}")
        x2, act = _swiglu_mm_res(x1, gu, wdown, l, f"down{l}")
        saved.append(dict(x0=x0, proj=proj, h=h, q=q, k=k, kt=kt, v=v, lse=lse, x1=x1, ycat=ycat, mixed=mixed,
                          gu=gu, h2=h2, act=act))
        xs.append(x2)

    dx, dx16, dg_final, loss_tile = _loss_head(xs[-1], final_norm.reshape(1, D), loss_target[0], "loss_head")
    loss_here = (loss_tile[0, 0] * (0.5 / D)).reshape(1)

    G = {n: [None] * DEPTH for n in ("w_uq", "w_ukv") + TINY + REPL if n != "final_norm"}
    gw_in = gw_o = gw_gu = gw_down = None
    for l in reversed(range(DEPTH)):
        sv = saved[l]
        dgu = _bwd_down(dx16, wdown, sv["gu"], l, f"down_bwd{l}")
        gw_down = _mm_tn(sv["act"], dx16, l, gw_down, f"dw_down{l}")
        gw_gu = _mm_tn(sv["h2"], dgu, l, gw_gu, f"dw_gate_up{l}", split4=True)
        ffn_token = None
        if l == 0:
            g_f = [gw_gu, gw_down.reshape(2, 4, D_FF // 4, D)]
            dx1, dx1_16, dg, got_f = _mm_nt_normbwd(dgu, gu4, l, sv["x1"], g_ffn, dx, 1, f"gate_up_bwd{l}",
                                                    rider=_swap_rider(g_f))
            pairs_f = [_pair_sum(g, o, kc, f"rs_pair_sum_{n}") for g, o, n in zip(g_f, got_f, FFN)]
            ffn_flight = _exchange_start([p[1] for p in pairs_f], "rs_exchange_start_ffn")
            ffn_token = ffn_flight[-1]
        else:
            dx1, dx1_16, dg = _mm_nt_normbwd(dgu, gu4, l, sv["x1"], g_ffn, dx, 1, f"gate_up_bwd{l}")
        G["ffn_norm"][l] = dg[0]
        gw_o = _mm_tn(sv["mixed"], dx1_16, l, gw_o, f"dw_o{l}")
        dycat, dg = _mm_nt_normbwd(dx1_16, wo.reshape(2, 1, D, D), l, sv["ycat"], g_mix, None, 4, f"mix_bwd{l}")
        G["mix_norm"][l] = dg[0]

        proj = sv["proj"]
        delta = _mla_delta(dycat, sv["ycat"], f"mla_delta{l}", after=ffn_token)
        dq, dk, dv, _ = _mla_attn_bwd(sv["q"], sv["k"], sv["kt"], sv["v"], dycat, sv["lse"], delta, None,
                                      f"mla_attn_bwd{l}")
        dcq, dckv, dkr, dwuq, dwk, dwv, dgq, dgkv = _mla_prep_bwd(
            dq, dk, dv, proj, g_q, g_kv, wuq_p, wk_p, wv_p, tabs, l, f"mla_prep_bwd{l}")
        dgb, dgc, duc, dcw = _conv_bwd(proj, conv8, dycat, l, f"conv_bwd{l}")
        dup, dpw, dps = _pool_bwd(proj, pwd, g_ps, dycat, l, f"pool_bwd{l}")
        dqs, dks, dvs, dsink = _swa_bwd(proj, swa_sinks, dycat, l, f"swa_bwd{l}")
        dproj = jnp.concatenate([dcq, dckv, dkr, dgb, dgc, duc, dup, dqs, dks, dvs], axis=1)
        gw_in = _mm_tn(dproj, sv["h"], l, gw_in, f"dw_in{l}")
        G["w_uq"][l] = jnp.concatenate(_unpad_heads(dwuq, UQ_SRC, ZERO_OFF), axis=1)
        kn, vv = _unpad_heads(dwk, KN_SRC, ZERO_OFF), _unpad_heads(dwv, V_SRC, V_OFF)
        G["w_ukv"][l] = jnp.concatenate([t for h in range(HEADS) for t in (kn[h], vv[h])], axis=1)
        swap_token = None
        if l == 0:
            g_r = [_unpad_w_in(gw_in).reshape(2, 4, -1, D), gw_o.reshape(2, 4, D // 4, D),
                   _cols_split(jnp.stack(G["w_uq"])), _cols_split(jnp.stack(G["w_ukv"]))]
            swap_flight = _exchange_start(g_r, "rs_swap_start", swap=True)
            swap_token = swap_flight[-1]
        dx, dx16, dg = _mm_nt_normbwd(dproj, win_p.reshape(2, 1, D_INP, D), l, sv["x0"], g_attn, dx1, 1, f"in_proj_bwd{l}",
                                      w_t=True, after=swap_token)
        G["attn_norm"][l] = dg[0]
        G["mla_q_norm"][l] = dgq[0]
        G["mla_kv_norm"][l] = dgkv[0]
        G["conv_w"][l] = dcw[0:3]
        G["pool_w"][l] = jnp.stack([dpw[0:64, 0:64], dpw[64:128, 64:128], dpw[128:192, 0:64], dpw[192:256, 64:128]])
        G["pool_scale"][l] = dps[0]
        G["swa_sinks"][l] = dsink[0, 0:4]
    grad_x = dx[None]
    Gl = {n: jnp.stack(G[n]) for n in TINY + REPL if n != "final_norm"}
    Gl["final_norm"] = dg_final[0]

    got_r, g_r = _exchange_wait(*swap_flight[:-1], dx16, "rs_swap_wait", swap=True)
    pairs_r = [_pair_sum(g, o, kc, f"rs_pair_sum_{n}") for g, o, n in zip(g_r, got_r, REST)]
    got3_f = _exchange_wait(*ffn_flight[:-1], dx16, "rs_exchange_wait_ffn")
    in_flight = _exchange_start([p[1] for p in pairs_r], "rs_exchange_start")
    us_f = [_chip_sum(p[0], [o3], kc, f"rs_chip_sum_{n}", after=in_flight[-1])
            for p, o3, n in zip(pairs_f, got3_f, FFN)]
    gsum_f = _join_layers(us_f, "rs_join_cores_ffn")
    res = {}

    def update(names, grads):
        for n, g in zip(names, grads):
            d_, m_, v_, g_ = _adamw(W[n], g, M1[n], V2[n], f"adamw_{n}", echo=True)
            back = T if n == "w_in" else (lambda a: a)
            res["g", n], res["d", n], res["m", n], res["v", n] = back(g_), back(d_), back(m_), back(v_)

    update(FFN, gsum_f)
    small = TINY + REPL
    full_shapes = [Gl[n].shape for n in small] + [(1,)]
    summed = _allsum_small(_pack([Gl[n] for n in small] + [loss_here]), "allsum_small", after=res["d", FFN[-1]])
    got3_r = _exchange_wait(*in_flight[:-1], summed, "rs_exchange_wait")
    us_r = [_chip_sum(p[0], [o3], kc, f"rs_chip_sum_{n}") for p, o3, n in zip(pairs_r, got3_r, REST)]
    update(REST, _join_layers(us_r, "rs_join_cores"))
    summed = _unpack(summed, full_shapes)
    loss = summed.pop().reshape(())

    def as3(a):
        if a.ndim <= 2:
            return a.reshape((1,) * (3 - a.ndim) + a.shape)
        return a.reshape(a.shape[0], -1, a.shape[-1])

    for n, g in zip(small, summed):
        if n in TINY:
            wdt = W[n].shape[2]
            g = lax.dynamic_slice_in_dim(g, chip * wdt, wdt, axis=2)
        out = _adamw(as3(W[n]), as3(g), as3(M1[n]), as3(V2[n]), f"adamw_{n}")
        res["g", n] = g
        res["d", n], res["m", n], res["v", n] = (o.reshape(W[n].shape) for o in out)

    return (loss, grad_x, *[res["g", n] for n in ORDER], *[res["d", n] for n in ORDER],
            *[res["m", n] for n in ORDER], *[res["v", n] for n in ORDER])
```
